```python
import jax
import jax.numpy as jnp
from jax import lax
import numpy as np

D_MODEL = 1024
BATCH = 8
SEQ = 2048
DEPTH = 2
DEC_BATCH = 16
DEC_SEQ = 16
PAST_LEN = 2048

CHUNK = 64
N_A = DEPTH // 2
N_B = DEPTH - N_A
GMLP_BLOCK = 128
D_GATE = 2 * D_MODEL
N_SG = 8
SG_W = D_GATE // N_SG
N_HEADS = 8
NOPE_DIM = 64
ROPE_DIM = 32
V_DIM = 64
Q_RANK = 384
KV_RANK = 256
ROPE_THETA = 10000.0
Q_BLOCK = 128
SCALE = (NOPE_DIM + ROPE_DIM) ** -0.5
N_EGROUPS = 4
EXPERTS_PER_GROUP = 4
N_EXPERTS = N_EGROUPS * EXPERTS_PER_GROUP
TOP_K = 2
D_EXPERT = 512
EPS = 1e-6
NEG = -1e30

kernel_name = 'yoco_gmlp_mla_hmoe_stream_step'


def rms_norm(x, g):
    xf = x.astype(jnp.float32)
    y = xf * lax.rsqrt(jnp.mean(xf * xf, axis=-1, keepdims=True) + EPS)
    return (y * g.astype(jnp.float32)).astype(x.dtype)


def layer_norm(x, g, b):
    xf = x.astype(jnp.float32)
    mu = jnp.mean(xf, axis=-1, keepdims=True)
    xc = xf - mu
    var = jnp.mean(xc * xc, axis=-1, keepdims=True)
    return (xc * lax.rsqrt(var + EPS) * g.astype(jnp.float32) + b.astype(jnp.float32)).astype(x.dtype)


def rope(x, pos):
    half = ROPE_DIM // 2
    inv_freq = ROPE_THETA ** (-jnp.arange(half, dtype=jnp.float32) / half)
    ang = pos.astype(jnp.float32)[:, None] * inv_freq[None, :]
    bshape = (ang.shape[0],) + (1,) * (x.ndim - 3) + (half,)
    cos = jnp.cos(ang).reshape(bshape)
    sin = jnp.sin(ang).reshape(bshape)
    xf = x.astype(jnp.float32)
    x1, x2 = xf[..., :half], xf[..., half:]
    return jnp.concatenate([x1 * cos - x2 * sin, x1 * sin + x2 * cos], axis=-1).astype(x.dtype)


def gmlp_mixer(xn, w_in, b_in, ln_g, ln_b, w_s, b_s, w_out, b_out):
    bsz, L, _ = xn.shape
    z = jax.nn.gelu(xn @ w_in + b_in)
    u, v = jnp.split(z, 2, axis=-1)
    v = layer_norm(v, ln_g, ln_b)
    n = min(L, GMLP_BLOCK)
    nb = L // n
    idx = jnp.arange(n)
    allowed = (idx[None, :] // CHUNK) <= (idx[:, None] // CHUNK)
    ws = jnp.where(allowed[None], w_s[:, :n, :n], 0.0).astype(v.dtype)
    vb = v.reshape(bsz, nb, n, N_SG, SG_W)
    s = jnp.einsum('gij,bnjgc->bnigc', ws, vb) + b_s[:, :n].T[None, None, :, :, None]
    s = s.reshape(bsz, L, D_GATE)
    return (u * s) @ w_out + b_out, v


def hier_moe(xn, w_group, b_group, w_expert, b_expert, w1, w3, w2):
    shp = xn.shape
    t = xn.reshape(-1, D_MODEL)
    tf = t.astype(jnp.float32)
    lg = tf @ w_group.astype(jnp.float32) + b_group.astype(jnp.float32)
    pg = jax.nn.softmax(lg, axis=-1)
    g_idx = jnp.argmax(lg, axis=-1)
    g_p = jnp.take_along_axis(pg, g_idx[:, None], axis=-1)
    le = (tf @ w_expert.astype(jnp.float32) + b_expert.astype(jnp.float32)).reshape(-1, N_EGROUPS, EXPERTS_PER_GROUP)
    le_sel = jnp.take_along_axis(le, g_idx[:, None, None], axis=1)[:, 0]
    top_v, top_i = lax.top_k(le_sel, TOP_K)
    wts = jax.nn.softmax(top_v, axis=-1) * g_p
    e_idx = g_idx[:, None] * EXPERTS_PER_GROUP + top_i
    gates = jnp.sum(jax.nn.one_hot(e_idx, N_EXPERTS, dtype=jnp.float32) * wts[..., None], axis=1).astype(t.dtype)
    y = jnp.zeros_like(t)
    for e in range(N_EXPERTS):
        hdn = jax.nn.silu(t @ w1[e]) * (t @ w3[e])
        y = y + gates[:, e:e + 1] * (hdn @ w2[e])
    return y.reshape(shp)


def mla_latent(h, pos, kv_norm, w_dkv, kv_a_norm, k_rope_norm):
    c = rms_norm(h, kv_norm) @ w_dkv
    c_kv = rms_norm(c[..., :KV_RANK], kv_a_norm)
    k_rope = rope(rms_norm(c[..., KV_RANK:], k_rope_norm), pos)
    return c_kv, k_rope


def mla_expand(c_kv, w_uk, w_uv, k_nope_norm):
    k_nope = rms_norm(jnp.einsum('blr,rhd->blhd', c_kv, w_uk), k_nope_norm)
    v = jnp.einsum('blr,rhd->blhd', c_kv, w_uv)
    return k_nope, v


def mla_query(xn, pos, w_dq, q_a_norm, w_uq, q_nope_norm, q_rope_norm):
    cq = rms_norm(xn @ w_dq, q_a_norm)
    q = jnp.einsum('blr,rhd->blhd', cq, w_uq)
    q_nope = rms_norm(q[..., :NOPE_DIM], q_nope_norm)
    q_rope = rope(rms_norm(q[..., NOPE_DIM:], q_rope_norm), pos)
    return q_nope, q_rope


def chunk_attend(q_nope, q_rope, q_pos, k_nope, k_rope, v, k_pos):
    s = (jnp.einsum('bqhd,bkhd->bhqk', q_nope, k_nope, preferred_element_type=jnp.float32)
         + jnp.einsum('bqhd,bkd->bhqk', q_rope, k_rope, preferred_element_type=jnp.float32)) * SCALE
    allowed = (k_pos[None, :] // CHUNK) <= (q_pos[:, None] // CHUNK)
    s = jnp.where(allowed[None, None], s, NEG)
    p = jax.nn.softmax(s, axis=-1).astype(v.dtype)
    return jnp.einsum('bhqk,bkhd->bqhd', p, v)


def blocked_attention(q_nope, q_rope, q_pos, k_nope, k_rope, v, k_pos):
    bsz, L = q_nope.shape[:2]
    nq = L // Q_BLOCK
    qn = q_nope.reshape(bsz, nq, Q_BLOCK, N_HEADS, NOPE_DIM).swapaxes(0, 1)
    qr = q_rope.reshape(bsz, nq, Q_BLOCK, N_HEADS, ROPE_DIM).swapaxes(0, 1)
    qp = q_pos.reshape(nq, Q_BLOCK)
    o = lax.map(lambda a: chunk_attend(a[0], a[1], a[2], k_nope, k_rope, v, k_pos), (qn, qr, qp))
    return o.swapaxes(0, 1).reshape(bsz, L, N_HEADS, V_DIM)


def trunk(x, past_ckv, past_krope, norm_mix, norm_ffn, gm_w_in, gm_b_in, gm_ln_g, gm_ln_b, gm_w_s,
          gm_b_s, gm_w_out, gm_b_out, kv_norm, w_dkv, kv_a_norm, k_rope_norm, w_uk, w_uv, k_nope_norm,
          w_dq, q_a_norm, w_uq, q_nope_norm, q_rope_norm, w_o, moe_w_group, moe_b_group, moe_w_expert,
          moe_b_expert, moe_w1, moe_w3, moe_w2):
    bsz, L, _ = x.shape
    past = 0 if past_ckv is None else past_ckv.shape[1]
    pos = past + jnp.arange(L, dtype=jnp.int32)
    h = x
    v_rows = []
    for l in range(N_A):
        mix, v_l = gmlp_mixer(rms_norm(h, norm_mix[l]), gm_w_in[l], gm_b_in[l], gm_ln_g[l], gm_ln_b[l],
                              gm_w_s[l], gm_b_s[l], gm_w_out[l], gm_b_out[l])
        h = h + mix
        v_rows.append(v_l)
        h = h + hier_moe(rms_norm(h, norm_ffn[l]), moe_w_group[l], moe_b_group[l], moe_w_expert[l],
                         moe_b_expert[l], moe_w1[l], moe_w3[l], moe_w2[l])
    c_kv, k_rope = mla_latent(h, pos, kv_norm, w_dkv, kv_a_norm, k_rope_norm)
    if past_ckv is None:
        ckv_all, krope_all = c_kv, k_rope
    else:
        ckv_all = jnp.concatenate([past_ckv.astype(c_kv.dtype), c_kv], axis=1)
        krope_all = jnp.concatenate([past_krope.astype(k_rope.dtype), k_rope], axis=1)
    k_pos = jnp.arange(past + L, dtype=jnp.int32)
    k_nope, v = mla_expand(ckv_all, w_uk, w_uv, k_nope_norm)
    for j in range(N_B):
        l = N_A + j
        q_nope, q_rope = mla_query(rms_norm(h, norm_mix[l]), pos, w_dq[j], q_a_norm[j], w_uq[j],
                                   q_nope_norm[j], q_rope_norm[j])
        if past_ckv is None:
            o = blocked_attention(q_nope, q_rope, pos, k_nope, krope_all, v, k_pos)
        else:
            o = chunk_attend(q_nope, q_rope, pos, k_nope, krope_all, v, k_pos)
        h = h + o.reshape(bsz, L, N_HEADS * V_DIM) @ w_o[j]
        h = h + hier_moe(rms_norm(h, norm_ffn[l]), moe_w_group[l], moe_b_group[l], moe_w_expert[l],
                         moe_b_expert[l], moe_w1[l], moe_w3[l], moe_w2[l])
    return h, c_kv, k_rope, jnp.stack(v_rows)


def setup_inputs(seed: int = 0) -> dict:
    key = jax.random.key(seed)
    ks = iter(jax.random.split(key, 40))
    f32 = jnp.float32

    def w(shape, fan_in):
        return jax.random.normal(next(ks), shape, f32) * fan_in ** -0.5

    def gain(shape):
        return 1.0 + 0.02 * jax.random.normal(next(ks), shape, f32)

    def bias(shape, s=0.02):
        return s * jax.random.normal(next(ks), shape, f32)

    return {
        'x_prompt': jax.random.normal(next(ks), (BATCH, SEQ, D_MODEL), f32),
        'x_sample': jax.random.normal(next(ks), (DEC_BATCH, DEC_SEQ, D_MODEL), f32),
        'cache_ckv': jax.random.normal(next(ks), (DEC_BATCH, PAST_LEN, KV_RANK), f32),
        'cache_krope': jax.random.normal(next(ks), (DEC_BATCH, PAST_LEN, ROPE_DIM), f32),
        'norm_mix': gain((DEPTH, D_MODEL)),
        'norm_ffn': gain((DEPTH, D_MODEL)),
        'gm_w_in': w((N_A, D_MODEL, 2 * D_GATE), D_MODEL),
        'gm_b_in': bias((N_A, 2 * D_GATE)),
        'gm_ln_g': gain((N_A, D_GATE)),
        'gm_ln_b': bias((N_A, D_GATE)),
        'gm_w_s': w((N_A, N_SG, GMLP_BLOCK, GMLP_BLOCK), GMLP_BLOCK),
        'gm_b_s': gain((N_A, N_SG, GMLP_BLOCK)),
        'gm_w_out': w((N_A, D_GATE, D_MODEL), D_GATE),
        'gm_b_out': bias((N_A, D_MODEL)),
        'kv_norm': gain((D_MODEL,)),
        'w_dkv': w((D_MODEL, KV_RANK + ROPE_DIM), D_MODEL),
        'kv_a_norm': gain((KV_RANK,)),
        'k_rope_norm': gain((ROPE_DIM,)),
        'w_uk': w((KV_RANK, N_HEADS, NOPE_DIM), KV_RANK),
        'w_uv': w((KV_RANK, N_HEADS, V_DIM), KV_RANK),
        'k_nope_norm': gain((NOPE_DIM,)),
        'w_dq': w((N_B, D_MODEL, Q_RANK), D_MODEL),
        'q_a_norm': gain((N_B, Q_RANK)),
        'w_uq': w((N_B, Q_RANK, N_HEADS, NOPE_DIM + ROPE_DIM), Q_RANK),
        'q_nope_norm': gain((N_B, NOPE_DIM)),
        'q_rope_norm': gain((N_B, ROPE_DIM)),
        'w_o': w((N_B, N_HEADS * V_DIM, D_MODEL), N_HEADS * V_DIM),
        'moe_w_group': w((DEPTH, D_MODEL, N_EGROUPS), D_MODEL),
        'moe_b_group': bias((DEPTH, N_EGROUPS), 0.01),
        'moe_w_expert': w((DEPTH, D_MODEL, N_EXPERTS), D_MODEL),
        'moe_b_expert': bias((DEPTH, N_EXPERTS), 0.01),
        'moe_w1': w((DEPTH, N_EXPERTS, D_MODEL, D_EXPERT), D_MODEL),
        'moe_w3': w((DEPTH, N_EXPERTS, D_MODEL, D_EXPERT), D_MODEL),
        'moe_w2': w((DEPTH, N_EXPERTS, D_EXPERT, D_MODEL), D_EXPERT),
    }


def reference(x_prompt, x_sample, cache_ckv, cache_krope, norm_mix, norm_ffn, gm_w_in, gm_b_in, gm_ln_g,
              gm_ln_b, gm_w_s, gm_b_s, gm_w_out, gm_b_out, kv_norm, w_dkv, kv_a_norm, k_rope_norm, w_uk,
              w_uv, k_nope_norm, w_dq, q_a_norm, w_uq, q_nope_norm, q_rope_norm, w_o, moe_w_group,
              moe_b_group, moe_w_expert, moe_b_expert, moe_w1, moe_w3, moe_w2):
    weights = (norm_mix, norm_ffn, gm_w_in, gm_b_in, gm_ln_g, gm_ln_b, gm_w_s, gm_b_s, gm_w_out, gm_b_out,
               kv_norm, w_dkv, kv_a_norm, k_rope_norm, w_uk, w_uv, k_nope_norm, w_dq, q_a_norm, w_uq,
               q_nope_norm, q_rope_norm, w_o, moe_w_group, moe_b_group, moe_w_expert, moe_b_expert,
               moe_w1, moe_w3, moe_w2)
    y_prompt, ckv_prompt, krope_prompt, _ = trunk(x_prompt, None, None, *weights)
    y_sample, ckv_sample, krope_sample, gmlp_v_sample = trunk(x_sample, cache_ckv, cache_krope, *weights)
    return (y_prompt, y_sample, ckv_prompt, krope_prompt, ckv_sample, krope_sample, gmlp_v_sample)
```

```python
import functools

import jax
import jax.numpy as jnp
import numpy as np
from jax import lax
from jax.experimental import pallas as pl
from jax.experimental.pallas import tpu as pltpu

F32 = jnp.float32
BF16 = jnp.bfloat16

D_MODEL = 1024
BATCH = 8
SEQ = 2048
DEC_BATCH = 16
DEC_SEQ = 16
PAST_LEN = 2048
CHUNK = 64
GMLP_BLOCK = 128
D_GATE = 2 * D_MODEL
N_SG = 8
SG_W = D_GATE // N_SG
N_HEADS = 8
NOPE_DIM = 64
ROPE_DIM = 32
V_DIM = 64
Q_RANK = 384
KV_RANK = 256
ROPE_THETA = 10000.0
SCALE = (NOPE_DIM + ROPE_DIM) ** -0.5
N_EGROUPS = 4
EXPERTS_PER_GROUP = 4
N_EXPERTS = N_EGROUPS * EXPERTS_PER_GROUP
D_EXPERT = 512
EPS = 1e-6
NEG = -1e30

LANES = 128
SUBLANES = 8
ROW_TILES = D_MODEL // LANES
assert ROW_TILES == SUBLANES

N_PROMPT = BATCH * SEQ
N_DEC = DEC_BATCH * DEC_SEQ
T = N_PROMPT + N_DEC
TM = 256
assert N_PROMPT % TM == 0 and N_DEC == TM
N_TILES = T // TM
HEAD_W = LANES
ROPE_LO = NOPE_DIM
ROPE_HALF = ROPE_DIM // 2

PAIR_A = (0, 0, 0, 1, 1, 3)
PAIR_B = (1, 2, 3, 3, 2, 2)
N_PAIRS = 6
N_BUCKETS = N_EGROUPS * N_PAIRS
MOE_TILES = (T + N_BUCKETS * (TM - 1) + TM - 1) // TM
P_ROWS = MOE_TILES * TM

VMEM_LIMIT = 56 * 1024 * 1024


def _cparams(n_axes=1, vmem=VMEM_LIMIT):
    return pltpu.CompilerParams(dimension_semantics=("arbitrary",) * n_axes, vmem_limit_bytes=vmem)


def _rms(x, g):
    return x * lax.rsqrt(jnp.mean(x * x, axis=-1, keepdims=True) + EPS) * g


def _load_rows(ref, n):
    return jnp.concatenate([ref[pl.ds(s, n, stride=ROW_TILES), :] for s in range(ROW_TILES)], axis=1)


def _store_rows(ref, x, n):
    for s in range(ROW_TILES):
        ref[pl.ds(s, n, stride=ROW_TILES), :] = x[:, s * LANES:(s + 1) * LANES]


def _dot(a, b):
    return jnp.dot(a, b, preferred_element_type=F32)


def _split_dot(x, m):
    hi = x.astype(BF16)
    lo = (x - hi.astype(F32)).astype(BF16)
    return _dot(hi, m) + _dot(lo, m)


def _gmlp_body(x_ref, nm_ref, win_ref, bin_ref, lng_ref, lnb_ref, wsp_ref, wsd_ref, bsp_ref, bsd_ref,
               wout_ref, bout_ref, h_ref, v_ref, gated_ref):
    i = pl.program_id(0)
    is_dec = i == N_TILES - 1
    x = x_ref[...]
    xn = _rms(x, nm_ref[...]).astype(BF16)
    z = _dot(xn, win_ref[...]) + bin_ref[...]
    z = z * (0.5 * (1.0 + jnp.tanh(np.float32(np.sqrt(2.0 / np.pi)) * (z + 0.044715 * (z * z * z)))))
    u = z[:, :D_GATE]
    v = z[:, D_GATE:]
    mu = jnp.mean(v, axis=-1, keepdims=True)
    vc = v - mu
    var = jnp.mean(vc * vc, axis=-1, keepdims=True)
    v = vc * lax.rsqrt(var + EPS) * lng_ref[...] + lnb_ref[...]

    @pl.when(is_dec)
    def _():
        v_ref[...] = v

    vb = v.astype(BF16)
    for g in range(N_SG):
        ws = jnp.where(is_dec, wsd_ref[g], wsp_ref[g])
        bs = jnp.where(is_dec, bsd_ref[g], bsp_ref[g])
        for b in range(TM // GMLP_BLOCK):
            rows = slice(b * GMLP_BLOCK, (b + 1) * GMLP_BLOCK)
            cols = slice(g * SG_W, (g + 1) * SG_W)
            s = _dot(ws, vb[rows, cols]) + bs
            gated_ref[rows, cols] = (u[rows, cols] * s).astype(BF16)
    h_ref[...] = x + _dot(gated_ref[...], wout_ref[...]) + bout_ref[...]


def _gmlp_layer(x, nm, w_in, b_in, ln_g, ln_b, ws_p, ws_d, bs_p, bs_d, w_out, b_out):
    const = lambda *shape: pl.BlockSpec(shape, lambda i: (0,) * len(shape))
    return pl.pallas_call(
        _gmlp_body,
        grid=(N_TILES,),
        in_specs=[
            pl.BlockSpec((TM, D_MODEL), lambda i: (i, 0)),
            const(1, D_MODEL), const(D_MODEL, 2 * D_GATE), const(1, 2 * D_GATE),
            const(1, D_GATE), const(1, D_GATE),
            const(N_SG, GMLP_BLOCK, GMLP_BLOCK), const(N_SG, GMLP_BLOCK, GMLP_BLOCK),
            const(N_SG, GMLP_BLOCK, 1), const(N_SG, GMLP_BLOCK, 1),
            const(D_GATE, D_MODEL), const(1, D_MODEL),
        ],
        out_specs=[
            pl.BlockSpec((TM, D_MODEL), lambda i: (i, 0)),
            const(N_DEC, D_GATE),
        ],
        out_shape=[jax.ShapeDtypeStruct((T, D_MODEL), F32), jax.ShapeDtypeStruct((N_DEC, D_GATE), F32)],
        scratch_shapes=[pltpu.VMEM((TM, D_GATE), BF16)],
        compiler_params=_cparams(),
        name="gmlp_layer",
    )(x, nm, w_in, b_in, ln_g, ln_b, ws_p, ws_d, bs_p, bs_d, w_out, b_out)


def _router_body(hp_ref, hd_ref, nf_ref, wr_ref, br_ref, rows_ref, info_ref, cnt_ref, carry_ref):
    i = pl.program_id(0)

    @pl.when(i == 0)
    def _():
        carry_ref[...] = jnp.zeros_like(carry_ref)

    h = jnp.where(i == N_TILES - 1, hd_ref[...], hp_ref[...])
    _store_rows(rows_ref, h, TM)
    xn = _rms(h, nf_ref[...])
    logits = jnp.dot(xn, wr_ref[...], precision=lax.Precision.HIGHEST, preferred_element_type=F32) + br_ref[...]
    lane = lax.broadcasted_iota(jnp.int32, (TM, LANES), 1).astype(F32)

    def first_max(vals):
        vmax = jnp.max(vals, axis=-1, keepdims=True)
        idx = jnp.min(jnp.where(vals == vmax, lane, float(LANES)), axis=-1, keepdims=True)
        return vmax, idx

    lg = jnp.where(lane < N_EGROUPS, logits, -jnp.inf)
    gmax, g_idx = first_max(lg)
    g_p = 1.0 / jnp.sum(jnp.exp(lg - gmax), axis=-1, keepdims=True)
    e_lo = N_EGROUPS + EXPERTS_PER_GROUP * g_idx
    le = jnp.where((lane >= e_lo) & (lane < e_lo + EXPERTS_PER_GROUP), logits, -jnp.inf)
    v1, i1 = first_max(le)
    v2, i2 = first_max(jnp.where(lane == i1, -jnp.inf, le))
    e2 = jnp.exp(v2 - v1)
    w1 = (1.0 / (1.0 + e2)) * g_p
    w2 = (e2 / (1.0 + e2)) * g_p
    a1 = i1 - e_lo
    a2 = i2 - e_lo
    lo = jnp.minimum(a1, a2)
    hi = jnp.maximum(a1, a2)
    pair = jnp.where(lo == 0.0, hi - 1.0, jnp.where(lo == 1.0, jnp.where(hi == 3.0, 3.0, 4.0), 5.0))
    ea = jnp.where(pair < 3.0, 0.0, jnp.where(pair < 5.0, 1.0, 3.0))
    ga = jnp.where(a1 == ea, w1, w2)
    gb = jnp.where(a1 == ea, w2, w1)
    bucket = g_idx * N_PAIRS + pair

    onehot = (lane == bucket).astype(F32)
    r = lax.broadcasted_iota(jnp.int32, (TM, TM), 0)
    c = lax.broadcasted_iota(jnp.int32, (TM, TM), 1)
    before = _dot((c < r).astype(BF16), onehot.astype(BF16))
    carry = carry_ref[0:1, :]
    rank = jnp.sum(onehot * (before + carry), axis=-1, keepdims=True)
    new_carry = carry + jnp.sum(onehot, axis=0, keepdims=True)
    carry_ref[...] = jnp.broadcast_to(new_carry, carry_ref.shape)
    cnt_ref[...] = jnp.broadcast_to(new_carry, cnt_ref.shape)
    info_ref[...] = jnp.where(lane == 0.0, bucket,
                              jnp.where(lane == 1.0, rank, jnp.where(lane == 2.0, ga, jnp.where(lane == 3.0, gb, 0.0))))


def _router(h_prompt, h_dec, nf, wr, br):
    const = lambda *shape: pl.BlockSpec(shape, lambda i: (0,) * len(shape))
    return pl.pallas_call(
        _router_body,
        grid=(N_TILES,),
        in_specs=[pl.BlockSpec((TM, D_MODEL), lambda i: (jnp.minimum(i, N_TILES - 2), 0)), const(N_DEC, D_MODEL),
                  const(1, D_MODEL), const(D_MODEL, LANES), const(1, LANES)],
        out_specs=[pl.BlockSpec((TM * ROW_TILES, LANES), lambda i: (i, 0)),
                   pl.BlockSpec((TM, LANES), lambda i: (i, 0)),
                   const(SUBLANES, LANES)],
        out_shape=[jax.ShapeDtypeStruct((T * ROW_TILES, LANES), F32), jax.ShapeDtypeStruct((T, LANES), F32),
                   jax.ShapeDtypeStruct((SUBLANES, LANES), F32)],
        scratch_shapes=[pltpu.VMEM((SUBLANES, LANES), F32)],
        compiler_params=_cparams(),
        name="moe_router",
    )(h_prompt, h_dec, nf, wr, br)


PERM_ROWS = 256


def _permute_body(idx_ref, src_ref, out_ref, sem):
    base = pl.program_id(0) * PERM_ROWS

    def row_copy(r):
        src = pl.multiple_of(idx_ref[base + r] * ROW_TILES, ROW_TILES)
        dst = pl.multiple_of((base + r) * ROW_TILES, ROW_TILES)
        return pltpu.make_async_copy(src_ref.at[pl.ds(src, ROW_TILES), :], out_ref.at[pl.ds(dst, ROW_TILES), :], sem)

    def start(r, carry):
        row_copy(r).start()
        return carry

    def wait(r, carry):
        row_copy(r).wait()
        return carry

    lax.fori_loop(0, PERM_ROWS, start, 0)
    lax.fori_loop(0, PERM_ROWS, wait, 0)


def _permute_rows(src, idx):
    n = idx.shape[0]
    assert n % PERM_ROWS == 0
    return pl.pallas_call(
        _permute_body,
        grid_spec=pltpu.PrefetchScalarGridSpec(
            num_scalar_prefetch=1,
            grid=(n // PERM_ROWS,),
            in_specs=[pl.BlockSpec(memory_space=pl.ANY)],
            out_specs=pl.BlockSpec(memory_space=pl.ANY),
            scratch_shapes=[pltpu.SemaphoreType.DMA],
        ),
        out_shape=jax.ShapeDtypeStruct((n * ROW_TILES, LANES), F32),
        compiler_params=_cparams(),
        name="permute_rows",
    )(idx, src)


def _moe_body(ea_ref, eb_ref, cha_ref, chb_ref, valid_ref, x_ref, g_ref, nf_ref,
              w1a_ref, w3a_ref, w2a_ref, w1b_ref, w3b_ref, w2b_ref, out_ref,
              s1a, s3a, s2a, s1b, s3b, s2b):
    i = pl.program_id(0)

    @pl.when(cha_ref[i] == 1)
    def _():
        s1a[...] = w1a_ref[...].astype(BF16)
        s3a[...] = w3a_ref[...].astype(BF16)
        s2a[...] = w2a_ref[...].astype(BF16)

    @pl.when(chb_ref[i] == 1)
    def _():
        s1b[...] = w1b_ref[...].astype(BF16)
        s3b[...] = w3b_ref[...].astype(BF16)
        s2b[...] = w2b_ref[...].astype(BF16)

    @pl.when(valid_ref[i] == 1)
    def _():
        h = _load_rows(x_ref, TM)
        xn = _rms(h, nf_ref[...]).astype(BF16)
        g = g_ref[...]

        def ffn(w1, w3, w2):
            a = _dot(xn, w1[...])
            hdn = (a * (1.0 / (1.0 + jnp.exp(-a)))) * _dot(xn, w3[...])
            return _dot(hdn.astype(BF16), w2[...])

        y = g[:, 0:1] * ffn(s1a, s3a, s2a) + g[:, 1:2] * ffn(s1b, s3b, s2b)
        _store_rows(out_ref, h + y, TM)

    @pl.when(valid_ref[i] == 0)
    def _():
        out_ref[...] = jnp.zeros_like(out_ref)


def _moe_ffn(layer, ea, eb, cha, chb, valid, xs, gates, nf, w1, w3, w2):
    wa = lambda shape: pl.BlockSpec((None, None) + shape, lambda i, ea, eb, ca, cb, va: (layer, ea[i], 0, 0))
    wb = lambda shape: pl.BlockSpec((None, None) + shape, lambda i, ea, eb, ca, cb, va: (layer, eb[i], 0, 0))
    up, down = (D_MODEL, D_EXPERT), (D_EXPERT, D_MODEL)
    return pl.pallas_call(
        _moe_body,
        grid_spec=pltpu.PrefetchScalarGridSpec(
            num_scalar_prefetch=5,
            grid=(MOE_TILES,),
            in_specs=[
                pl.BlockSpec((TM * ROW_TILES, LANES), lambda i, *_: (i, 0)),
                pl.BlockSpec((TM, 2), lambda i, *_: (i, 0)),
                pl.BlockSpec((1, D_MODEL), lambda i, *_: (0, 0)),
                wa(up), wa(up), wa(down), wb(up), wb(up), wb(down),
            ],
            out_specs=pl.BlockSpec((TM * ROW_TILES, LANES), lambda i, *_: (i, 0)),
            scratch_shapes=[pltpu.VMEM(up, BF16), pltpu.VMEM(up, BF16), pltpu.VMEM(down, BF16),
                            pltpu.VMEM(up, BF16), pltpu.VMEM(up, BF16), pltpu.VMEM(down, BF16)],
        ),
        out_shape=jax.ShapeDtypeStruct((P_ROWS * ROW_TILES, LANES), F32),
        compiler_params=_cparams(),
        name="moe_ffn",
    )(ea, eb, cha, chb, valid, xs, gates, nf, w1, w3, w2, w1, w3, w2)


def _moe_layer(layer, h_prompt, h_dec, norm_ffn, w_group, b_group, w_expert, b_expert, w1, w3, w2):
    wr = jnp.zeros((D_MODEL, LANES), F32)
    wr = wr.at[:, :N_EGROUPS].set(w_group[layer]).at[:, N_EGROUPS:N_EGROUPS + N_EXPERTS].set(w_expert[layer])
    br = jnp.zeros((1, LANES), F32)
    br = br.at[0, :N_EGROUPS].set(b_group[layer]).at[0, N_EGROUPS:N_EGROUPS + N_EXPERTS].set(b_expert[layer])
    nf = norm_ffn[layer].reshape(1, D_MODEL)
    rows, info, cnt = _router(h_prompt, h_dec, nf, wr, br)

    bucket = info[:, 0].astype(jnp.int32)
    rank = info[:, 1].astype(jnp.int32)
    counts = cnt[0, :N_BUCKETS].astype(jnp.int32)
    n_tiles = (counts + TM - 1) // TM
    tile_end = jnp.cumsum(n_tiles)
    tile_start = tile_end - n_tiles
    start_of = jnp.sum(jnp.where(bucket[:, None] == jnp.arange(N_BUCKETS)[None, :], tile_start[None, :], 0), axis=1)
    pos = start_of * TM + rank
    idx_sorted = jnp.zeros((P_ROWS,), jnp.int32).at[pos].set(jnp.arange(T, dtype=jnp.int32))
    gates = jnp.zeros((P_ROWS, 2), F32).at[pos].set(info[:, 2:4])
    total = tile_end[-1]
    j = jnp.minimum(jnp.arange(MOE_TILES), total - 1)
    tb = jnp.sum((j[:, None] >= tile_end[None, :]).astype(jnp.int32), axis=1)
    grp, pair = tb // N_PAIRS, tb % N_PAIRS
    ea = (grp * EXPERTS_PER_GROUP + jnp.asarray(PAIR_A, jnp.int32)[pair]).astype(jnp.int32)
    eb = (grp * EXPERTS_PER_GROUP + jnp.asarray(PAIR_B, jnp.int32)[pair]).astype(jnp.int32)
    first = jnp.arange(MOE_TILES) == 0
    cha = (first | (ea != jnp.roll(ea, 1))).astype(jnp.int32)
    chb = (first | (eb != jnp.roll(eb, 1))).astype(jnp.int32)
    valid = (jnp.arange(MOE_TILES) < total).astype(jnp.int32)

    xs = _permute_rows(rows, idx_sorted)
    ys = _moe_ffn(layer, ea, eb, cha, chb, valid, xs, gates, nf, w1, w3, w2)
    return _permute_rows(ys, jnp.concatenate([pos, jnp.zeros((N_TILES * PERM_ROWS - T,), jnp.int32)]))


def _seg_matrix():
    lane = np.arange(LANES)
    seg = np.where(lane < NOPE_DIM, 0, np.where(lane < NOPE_DIM + ROPE_DIM, 1, 2))
    return jnp.asarray(seg[:, None] == seg[None, :], BF16)


def _seg_count():
    lane = np.arange(LANES)
    return jnp.asarray(np.where(lane < NOPE_DIM, 1.0 / NOPE_DIM, 1.0 / ROPE_DIM), F32).reshape(1, LANES)


def _rope_swap(x):
    lane = lax.broadcasted_iota(jnp.int32, x.shape, 1)
    return jnp.where(lane < ROPE_LO + ROPE_HALF, pltpu.roll(x, LANES - ROPE_HALF, 1), pltpu.roll(x, ROPE_HALF, 1))


def _head_norm(x, seg, cnt, gain):
    ms = _split_dot(x * x, seg) * cnt
    return x * lax.rsqrt(ms + EPS) * gain


def _expand_kv(ckv, kr, wuk_ref, wuv_ref, seg, cnt, kg_ref, k_ref, v_ref):
    cb = ckv.astype(BF16)
    kn = _dot(cb, wuk_ref[...])
    for hh in range(N_HEADS):
        cols = slice(hh * HEAD_W, (hh + 1) * HEAD_W)
        k_ref[:, cols] = (_head_norm(kn[:, cols], seg, cnt, kg_ref[...]) + kr).astype(BF16)
    v_ref[...] = _dot(cb, wuv_ref[...]).astype(BF16)


def _mla_proj_body(rows_ref, cos_ref, sin_ref, kvn_ref, wdkv_ref, kvan_ref, krg_ref, wuk_ref, wuv_ref, kg_ref,
                   nmq_ref, wdq_ref, qan_ref, wuq_ref, qg_ref, seg_ref, cnt_ref,
                   ckv_ref, krope_ref, k_ref, v_ref, q_ref):
    h = _load_rows(rows_ref, TM)
    seg, cnt = seg_ref[...], cnt_ref[...]
    cos, sin = cos_ref[...], sin_ref[...]
    c = _dot(_rms(h, kvn_ref[...]).astype(BF16), wdkv_ref[...])
    ckv = _rms(c[:, :KV_RANK], kvan_ref[...])
    ckv_ref[...] = ckv
    kr = c[:, KV_RANK:]
    kr = kr * lax.rsqrt(jnp.sum(kr * kr, axis=-1, keepdims=True) * (1.0 / ROPE_DIM) + EPS) * krg_ref[...]
    kr = kr * cos + _rope_swap(kr) * sin
    krope_ref[...] = kr[:, ROPE_LO:ROPE_LO + ROPE_DIM]
    _expand_kv(ckv, kr, wuk_ref, wuv_ref, seg, cnt, kg_ref, k_ref, v_ref)
    cq = _rms(_dot(_rms(h, nmq_ref[...]).astype(BF16), wdq_ref[...]), qan_ref[...]).astype(BF16)
    q = _dot(cq, wuq_ref[...])
    for hh in range(N_HEADS):
        cols = slice(hh * HEAD_W, (hh + 1) * HEAD_W)
        qh = _head_norm(q[:, cols], seg, cnt, qg_ref[...])
        qh = qh * cos + _rope_swap(qh) * sin
        q_ref[:, cols] = (qh * SCALE).astype(BF16)


def _mla_proj(rows, cos_t, sin_t, kvn, wdkv, kvan, krg, wuk, wuv, kg, nmq, wdq, qan, wuq, qg):
    const = lambda *shape: pl.BlockSpec(shape, lambda i: (0,) * len(shape))
    tab = pl.BlockSpec((TM, LANES), lambda i: (jnp.where(i < N_PROMPT // TM, i % (SEQ // TM), SEQ // TM), 0))
    row = lambda w: pl.BlockSpec((TM, w), lambda i: (i, 0))
    return pl.pallas_call(
        _mla_proj_body,
        grid=(N_TILES,),
        in_specs=[
            pl.BlockSpec((TM * ROW_TILES, LANES), lambda i: (i, 0)), tab, tab,
            const(1, D_MODEL), const(D_MODEL, KV_RANK + LANES), const(1, KV_RANK), const(1, LANES),
            const(KV_RANK, N_HEADS * HEAD_W), const(KV_RANK, N_HEADS * V_DIM), const(1, LANES),
            const(1, D_MODEL), const(D_MODEL, Q_RANK), const(1, Q_RANK), const(Q_RANK, N_HEADS * HEAD_W),
            const(1, LANES), const(LANES, LANES), const(1, LANES),
        ],
        out_specs=[row(KV_RANK), row(ROPE_DIM), row(N_HEADS * HEAD_W), row(N_HEADS * V_DIM), row(N_HEADS * HEAD_W)],
        out_shape=[
            jax.ShapeDtypeStruct((T, KV_RANK), F32), jax.ShapeDtypeStruct((T, ROPE_DIM), F32),
            jax.ShapeDtypeStruct((T, N_HEADS * HEAD_W), BF16), jax.ShapeDtypeStruct((T, N_HEADS * V_DIM), BF16),
            jax.ShapeDtypeStruct((T, N_HEADS * HEAD_W), BF16),
        ],
        compiler_params=_cparams(),
        name="mla_proj",
    )(rows, cos_t, sin_t, kvn, wdkv, kvan, krg, wuk, wuv, kg, nmq, wdq, qan, wuq, qg, _seg_matrix(), _seg_count())


def _cache_kv_body(ckv_ref, kr_ref, place_ref, wuk_ref, wuv_ref, kg_ref, seg_ref, cnt_ref, k_ref, v_ref):
    kr = _dot(kr_ref[...].astype(BF16), place_ref[...])
    _expand_kv(ckv_ref[...], kr, wuk_ref, wuv_ref, seg_ref[...], cnt_ref[...], kg_ref, k_ref, v_ref)


def _cache_kv(ckv, kr, wuk, wuv, kg):
    n = ckv.shape[0]
    place = jnp.asarray(np.arange(ROPE_DIM)[:, None] + ROPE_LO == np.arange(LANES)[None, :], BF16)
    const = lambda *shape: pl.BlockSpec(shape, lambda i: (0,) * len(shape))
    row = lambda w: pl.BlockSpec((TM, w), lambda i: (i, 0))
    return pl.pallas_call(
        _cache_kv_body,
        grid=(n // TM,),
        in_specs=[row(KV_RANK), row(ROPE_DIM), const(ROPE_DIM, LANES), const(KV_RANK, N_HEADS * HEAD_W),
                  const(KV_RANK, N_HEADS * V_DIM), const(1, LANES), const(LANES, LANES), const(1, LANES)],
        out_specs=[row(N_HEADS * HEAD_W), row(N_HEADS * V_DIM)],
        out_shape=[jax.ShapeDtypeStruct((n, N_HEADS * HEAD_W), BF16), jax.ShapeDtypeStruct((n, N_HEADS * V_DIM), BF16)],
        compiler_params=_cparams(),
        name="cache_kv",
    )(ckv, kr, place, wuk, wuv, kg, _seg_matrix(), _seg_count())


TQ = 256
TK = 256
assert TQ == TK and TQ % CHUNK == 0


def _qk(q, k):
    return lax.dot_general(q, k, (((1,), (1,)), ((), ())), preferred_element_type=F32)


def _softmax_step(s, m, l, acc, v):
    m_new = jnp.maximum(m, jnp.max(s, axis=-1, keepdims=True))
    alpha = jnp.exp(m - m_new)
    p = jnp.exp(s - m_new)
    l = alpha * l + jnp.sum(p, axis=-1, keepdims=True)
    acc = alpha * acc + _dot(p.astype(BF16), v)
    return m_new, l, acc


def _merge_heads(o_ref, outs, rows):
    lane = lax.broadcasted_iota(jnp.int32, (rows, LANES), 1)
    for pr in range(N_HEADS // 2):
        o_ref[:, pr * LANES:(pr + 1) * LANES] = jnp.where(lane < V_DIM, outs[2 * pr], outs[2 * pr + 1]).astype(BF16)


def _prompt_attn_body(q_ref, k_ref, v_ref, rows_ref, wo_ref, out_ref, o_scr):
    qi = pl.program_id(1)
    qc = lax.broadcasted_iota(jnp.int32, (TQ, TK), 0) // CHUNK
    kc = lax.broadcasted_iota(jnp.int32, (TQ, TK), 1) // CHUNK
    diag_ok = kc <= qc
    outs = []
    for hh in range(N_HEADS):
        hcols = slice(hh * HEAD_W, (hh + 1) * HEAD_W)
        vcols = slice((hh // 2) * LANES, (hh // 2 + 1) * LANES)
        q = q_ref[:, hcols]

        def step(j, carry, hcols=hcols, vcols=vcols, q=q):
            ks = pl.ds(pl.multiple_of(j * TK, TK), TK)
            return _softmax_step(_qk(q, k_ref[ks, hcols]), *carry, v_ref[ks, vcols])

        init = (jnp.full((TQ, 1), NEG, F32), jnp.zeros((TQ, 1), F32), jnp.zeros((TQ, LANES), F32))
        carry = lax.fori_loop(0, qi, step, init)
        ks = pl.ds(pl.multiple_of(qi * TK, TK), TK)
        s = jnp.where(diag_ok, _qk(q, k_ref[ks, hcols]), NEG)
        _, l, acc = _softmax_step(s, *carry, v_ref[ks, vcols])
        outs.append(acc / l)
    _merge_heads(o_scr, outs, TQ)
    out_ref[...] = _load_rows(rows_ref, TQ) + _dot(o_scr[...], wo_ref[...])


def _prompt_attn(q, k, v, rows, wo):
    nq = SEQ // TQ
    return pl.pallas_call(
        _prompt_attn_body,
        grid=(BATCH, nq),
        in_specs=[
            pl.BlockSpec((TQ, N_HEADS * HEAD_W), lambda b, i: (b * nq + i, 0)),
            pl.BlockSpec((SEQ, N_HEADS * HEAD_W), lambda b, i: (b, 0)),
            pl.BlockSpec((SEQ, N_HEADS * V_DIM), lambda b, i: (b, 0)),
            pl.BlockSpec((TQ * ROW_TILES, LANES), lambda b, i: (b * nq + i, 0)),
            pl.BlockSpec((N_HEADS * V_DIM, D_MODEL), lambda b, i: (0, 0)),
        ],
        out_specs=pl.BlockSpec((TQ, D_MODEL), lambda b, i: (b * nq + i, 0)),
        out_shape=jax.ShapeDtypeStruct((N_PROMPT, D_MODEL), F32),
        scratch_shapes=[pltpu.VMEM((TQ, N_HEADS * V_DIM), BF16)],
        compiler_params=_cparams(2),
        name="prompt_attn",
    )(q, k, v, rows, wo)


def _sample_attn_body(q_ref, kc_ref, vc_ref, kn_ref, vn_ref, rows_ref, wo_ref, out_ref, o_scr):
    outs = []
    for hh in range(N_HEADS):
        hcols = slice(hh * HEAD_W, (hh + 1) * HEAD_W)
        vcols = slice((hh // 2) * LANES, (hh // 2 + 1) * LANES)
        q = q_ref[:, hcols]
        sc = _qk(q, kc_ref[:, hcols])
        sn = _qk(q, kn_ref[:, hcols])
        m = jnp.maximum(jnp.max(sc, axis=-1, keepdims=True), jnp.max(sn, axis=-1, keepdims=True))
        pc = jnp.exp(sc - m)
        pn = jnp.exp(sn - m)
        l = jnp.sum(pc, axis=-1, keepdims=True) + jnp.sum(pn, axis=-1, keepdims=True)
        acc = _dot(pc.astype(BF16), vc_ref[:, vcols]) + _dot(pn.astype(BF16), vn_ref[:, vcols])
        outs.append(acc / l)
    _merge_heads(o_scr, outs, DEC_SEQ)
    out_ref[...] = _load_rows(rows_ref, DEC_SEQ) + _dot(o_scr[...], wo_ref[...])


def _sample_attn(q, kc, vc, kn, vn, rows, wo):
    off = N_PROMPT // DEC_SEQ
    return pl.pallas_call(
        _sample_attn_body,
        grid=(DEC_BATCH,),
        in_specs=[
            pl.BlockSpec((DEC_SEQ, N_HEADS * HEAD_W), lambda b: (off + b, 0)),
            pl.BlockSpec((PAST_LEN, N_HEADS * HEAD_W), lambda b: (b, 0)),
            pl.BlockSpec((PAST_LEN, N_HEADS * V_DIM), lambda b: (b, 0)),
            pl.BlockSpec((DEC_SEQ, N_HEADS * HEAD_W), lambda b: (off + b, 0)),
            pl.BlockSpec((DEC_SEQ, N_HEADS * V_DIM), lambda b: (off + b, 0)),
            pl.BlockSpec((DEC_SEQ * ROW_TILES, LANES), lambda b: (off + b, 0)),
            pl.BlockSpec((N_HEADS * V_DIM, D_MODEL), lambda b: (0, 0)),
        ],
        out_specs=pl.BlockSpec((DEC_SEQ, D_MODEL), lambda b: (b, 0)),
        out_shape=jax.ShapeDtypeStruct((N_DEC, D_MODEL), F32),
        scratch_shapes=[pltpu.VMEM((DEC_SEQ, N_HEADS * V_DIM), BF16)],
        compiler_params=_cparams(),
        name="sample_attn",
    )(q, kc, vc, kn, vn, rows, wo)


def _finish_body(rows_ref, yp_ref, ys_ref):
    i = pl.program_id(0)
    y = _load_rows(rows_ref, TM)

    @pl.when(i < N_TILES - 1)
    def _():
        yp_ref[...] = y

    @pl.when(i == N_TILES - 1)
    def _():
        ys_ref[...] = y


def _finish(rows):
    return pl.pallas_call(
        _finish_body,
        grid=(N_TILES,),
        in_specs=[pl.BlockSpec((TM * ROW_TILES, LANES), lambda i: (i, 0))],
        out_specs=[pl.BlockSpec((TM, D_MODEL), lambda i: (jnp.minimum(i, N_TILES - 2), 0)),
                   pl.BlockSpec((N_DEC, D_MODEL), lambda i: (0, 0))],
        out_shape=[jax.ShapeDtypeStruct((N_PROMPT, D_MODEL), F32), jax.ShapeDtypeStruct((N_DEC, D_MODEL), F32)],
        compiler_params=_cparams(),
        name="finish",
    )(rows)


def _rope_tables():
    half = ROPE_DIM // 2
    inv_freq = ROPE_THETA ** (-jnp.arange(half, dtype=F32) / half)
    dec_pos = PAST_LEN + jnp.tile(jnp.arange(DEC_SEQ, dtype=jnp.int32), DEC_BATCH)
    pos = jnp.concatenate([jnp.arange(SEQ, dtype=jnp.int32), dec_pos])
    ang = pos.astype(F32)[:, None] * inv_freq[None, :]
    cos, sin = jnp.cos(ang), jnp.sin(ang)
    n = pos.shape[0]
    cos_t = jnp.ones((n, LANES), F32).at[:, ROPE_LO:ROPE_LO + ROPE_DIM].set(jnp.concatenate([cos, cos], axis=1))
    sin_t = jnp.zeros((n, LANES), F32).at[:, ROPE_LO:ROPE_LO + ROPE_DIM].set(jnp.concatenate([-sin, sin], axis=1))
    return cos_t, sin_t


def _on_lanes(vec, lo):
    return jnp.zeros((1, LANES), F32).at[0, lo:lo + vec.shape[0]].set(vec)


def kernel(x_prompt, x_sample, cache_ckv, cache_krope, norm_mix, norm_ffn, gm_w_in, gm_b_in, gm_ln_g, gm_ln_b, gm_w_s, gm_b_s, gm_w_out, gm_b_out, kv_norm, w_dkv, kv_a_norm, k_rope_norm, w_uk, w_uv, k_nope_norm, w_dq, q_a_norm, w_uq, q_nope_norm, q_rope_norm, w_o, moe_w_group, moe_b_group, moe_w_expert, moe_b_expert, moe_w1, moe_w3, moe_w2):
    x = jnp.concatenate([x_prompt.reshape(N_PROMPT, D_MODEL), x_sample.reshape(N_DEC, D_MODEL)], axis=0)

    idx = np.arange(GMLP_BLOCK)
    allowed = (idx[None, :] // CHUNK) <= (idx[:, None] // CHUNK)
    ws_p = jnp.where(allowed[None], gm_w_s[0], 0.0).astype(BF16)
    same_seq = (idx[None, :] // DEC_SEQ) == (idx[:, None] // DEC_SEQ)
    ws_d = jnp.where(same_seq[None], jnp.tile(gm_w_s[0][:, :DEC_SEQ, :DEC_SEQ], (1, GMLP_BLOCK // DEC_SEQ, GMLP_BLOCK // DEC_SEQ)), 0.0).astype(BF16)
    bs_p = gm_b_s[0][:, :, None]
    bs_d = jnp.tile(gm_b_s[0][:, :DEC_SEQ], (1, GMLP_BLOCK // DEC_SEQ))[:, :, None]
    h, v_rows = _gmlp_layer(
        x, norm_mix[0].reshape(1, -1), gm_w_in[0].astype(BF16), gm_b_in[0].reshape(1, -1),
        gm_ln_g[0].reshape(1, -1), gm_ln_b[0].reshape(1, -1), ws_p, ws_d, bs_p, bs_d,
        gm_w_out[0].astype(BF16), gm_b_out[0].reshape(1, -1))
    rows = _moe_layer(0, h, h[N_PROMPT:], norm_ffn, moe_w_group, moe_b_group, moe_w_expert, moe_b_expert, moe_w1, moe_w3, moe_w2)

    cos_t, sin_t = _rope_tables()
    wdkv = jnp.zeros((D_MODEL, KV_RANK + LANES), F32).at[:, :KV_RANK].set(w_dkv[:, :KV_RANK])
    wdkv = wdkv.at[:, KV_RANK + ROPE_LO:KV_RANK + ROPE_LO + ROPE_DIM].set(w_dkv[:, KV_RANK:]).astype(BF16)
    wuk = jnp.zeros((KV_RANK, N_HEADS, HEAD_W), F32).at[:, :, :NOPE_DIM].set(w_uk).reshape(KV_RANK, -1).astype(BF16)
    wuv = w_uv.reshape(KV_RANK, -1).astype(BF16)
    wuq = jnp.zeros((Q_RANK, N_HEADS, HEAD_W), F32).at[:, :, :NOPE_DIM + ROPE_DIM].set(w_uq[0]).reshape(Q_RANK, -1).astype(BF16)
    kg = _on_lanes(k_nope_norm, 0)
    krg = _on_lanes(k_rope_norm, ROPE_LO)
    qg = _on_lanes(jnp.concatenate([q_nope_norm[0], q_rope_norm[0]]), 0)
    ckv, krope, k_new, v_new, q = _mla_proj(
        rows, cos_t, sin_t, kv_norm.reshape(1, -1), wdkv, kv_a_norm.reshape(1, -1), krg, wuk, wuv, kg,
        norm_mix[1].reshape(1, -1), w_dq[0].astype(BF16), q_a_norm[0].reshape(1, -1), wuq, qg)
    k_cache, v_cache = _cache_kv(cache_ckv.reshape(-1, KV_RANK), cache_krope.reshape(-1, ROPE_DIM), wuk, wuv, kg)

    wo = w_o[0].astype(BF16)
    h_prompt = _prompt_attn(q, k_new, v_new, rows, wo)
    h_dec = _sample_attn(q, k_cache, v_cache, k_new, v_new, rows, wo)
    rows = _moe_layer(1, h_prompt, h_dec, norm_ffn, moe_w_group, moe_b_group, moe_w_expert, moe_b_expert, moe_w1, moe_w3, moe_w2)
    y_prompt, y_sample = _finish(rows)

    return (y_prompt.reshape(BATCH, SEQ, D_MODEL), y_sample.reshape(DEC_BATCH, DEC_SEQ, D_MODEL),
            ckv[:N_PROMPT].reshape(BATCH, SEQ, KV_RANK), krope[:N_PROMPT].reshape(BATCH, SEQ, ROPE_DIM),
            ckv[N_PROMPT:].reshape(DEC_BATCH, DEC_SEQ, KV_RANK), krope[N_PROMPT:].reshape(DEC_BATCH, DEC_SEQ, ROPE_DIM),
            v_rows.reshape(1, DEC_BATCH, DEC_SEQ, D_GATE))
```

```python
import functools

import jax
import jax.numpy as jnp
import numpy as np
from jax import lax
from jax.experimental import pallas as pl
from jax.experimental.pallas import tpu as pltpu

F32 = jnp.float32
BF16 = jnp.bfloat16

D_MODEL = 1024
BATCH = 8
SEQ = 2048
DEC_BATCH = 16
DEC_SEQ = 16
PAST_LEN = 2048
CHUNK = 64
GMLP_BLOCK = 128
D_GATE = 2 * D_MODEL
N_SG = 8
SG_W = D_GATE // N_SG
N_HEADS = 8
NOPE_DIM = 64
ROPE_DIM = 32
V_DIM = 64
Q_RANK = 384
KV_RANK = 256
ROPE_THETA = 10000.0
SCALE = (NOPE_DIM + ROPE_DIM) ** -0.5
N_EGROUPS = 4
EXPERTS_PER_GROUP = 4
N_EXPERTS = N_EGROUPS * EXPERTS_PER_GROUP
D_EXPERT = 512
EPS = 1e-6
NEG = -1e30

LANES = 128
SUBLANES = 8
ROW_TILES = D_MODEL // LANES
assert ROW_TILES == SUBLANES

N_PROMPT = BATCH * SEQ
N_DEC = DEC_BATCH * DEC_SEQ
T = N_PROMPT + N_DEC
TM = 256
assert N_PROMPT % TM == 0 and N_DEC == TM
N_TILES = T // TM
HEAD_W = LANES
ROPE_LO = NOPE_DIM
ROPE_HALF = ROPE_DIM // 2

PAIR_A = (0, 0, 0, 1, 1, 3)
PAIR_B = (1, 2, 3, 3, 2, 2)
N_PAIRS = 6
N_BUCKETS = N_EGROUPS * N_PAIRS
MOE_TILES = (T + N_BUCKETS * (TM - 1) + TM - 1) // TM
P_ROWS = MOE_TILES * TM

VMEM_LIMIT = 56 * 1024 * 1024


def _cparams(n_axes=1, vmem=VMEM_LIMIT):
    return pltpu.CompilerParams(dimension_semantics=("arbitrary",) * n_axes, vmem_limit_bytes=vmem)


def _rms(x, g):
    return x * lax.rsqrt(jnp.mean(x * x, axis=-1, keepdims=True) + EPS) * g


def _load_rows(ref, n):
    return jnp.concatenate([ref[pl.ds(s, n, stride=ROW_TILES), :] for s in range(ROW_TILES)], axis=1)


def _store_rows(ref, x, n):
    for s in range(ROW_TILES):
        ref[pl.ds(s, n, stride=ROW_TILES), :] = x[:, s * LANES:(s + 1) * LANES]


def _dot(a, b):
    return jnp.dot(a, b, preferred_element_type=F32)


def _split_dot(x, m):
    hi = x.astype(BF16)
    lo = (x - hi.astype(F32)).astype(BF16)
    return _dot(hi, m) + _dot(lo, m)


def _gmlp_body(x_ref, nm_ref, win_ref, bin_ref, lng_ref, lnb_ref, wsp_ref, wsd_ref, bsp_ref, bsd_ref,
               wout_ref, bout_ref, h_ref, v_ref, gated_ref):
    i = pl.program_id(0)
    is_dec = i == N_TILES - 1
    x = x_ref[...]
    xn = _rms(x, nm_ref[...]).astype(BF16)
    z = _dot(xn, win_ref[...]) + bin_ref[...]
    z = z * (0.5 * (1.0 + jnp.tanh(np.float32(np.sqrt(2.0 / np.pi)) * (z + 0.044715 * (z * z * z)))))
    u = z[:, :D_GATE]
    v = z[:, D_GATE:]
    mu = jnp.mean(v, axis=-1, keepdims=True)
    vc = v - mu
    var = jnp.mean(vc * vc, axis=-1, keepdims=True)
    v = vc * lax.rsqrt(var + EPS) * lng_ref[...] + lnb_ref[...]

    @pl.when(is_dec)
    def _():
        v_ref[...] = v

    vb = v.astype(BF16)
    for g in range(N_SG):
        ws = jnp.where(is_dec, wsd_ref[g], wsp_ref[g])
        bs = jnp.where(is_dec, bsd_ref[g], bsp_ref[g])
        for b in range(TM // GMLP_BLOCK):
            rows = slice(b * GMLP_BLOCK, (b + 1) * GMLP_BLOCK)
            cols = slice(g * SG_W, (g + 1) * SG_W)
            s = _dot(ws, vb[rows, cols]) + bs
            gated_ref[rows, cols] = (u[rows, cols] * s).astype(BF16)
    h_ref[...] = x + _dot(gated_ref[...], wout_ref[...]) + bout_ref[...]


def _gmlp_layer(x, nm, w_in, b_in, ln_g, ln_b, ws_p, ws_d, bs_p, bs_d, w_out, b_out):
    const = lambda *shape: pl.BlockSpec(shape, lambda i: (0,) * len(shape))
    return pl.pallas_call(
        _gmlp_body,
        grid=(N_TILES,),
        in_specs=[
            pl.BlockSpec((TM, D_MODEL), lambda i: (i, 0)),
            const(1, D_MODEL), const(D_MODEL, 2 * D_GATE), const(1, 2 * D_GATE),
            const(1, D_GATE), const(1, D_GATE),
            const(N_SG, GMLP_BLOCK, GMLP_BLOCK), const(N_SG, GMLP_BLOCK, GMLP_BLOCK),
            const(N_SG, GMLP_BLOCK, 1), const(N_SG, GMLP_BLOCK, 1),
            const(D_GATE, D_MODEL), const(1, D_MODEL),
        ],
        out_specs=[
            pl.BlockSpec((TM, D_MODEL), lambda i: (i, 0)),
            const(N_DEC, D_GATE),
        ],
        out_shape=[jax.ShapeDtypeStruct((T, D_MODEL), F32), jax.ShapeDtypeStruct((N_DEC, D_GATE), F32)],
        scratch_shapes=[pltpu.VMEM((TM, D_GATE), BF16)],
        compiler_params=_cparams(),
        name="gmlp_layer",
    )(x, nm, w_in, b_in, ln_g, ln_b, ws_p, ws_d, bs_p, bs_d, w_out, b_out)


def _router_body(hp_ref, hd_ref, nf_ref, wr_ref, br_ref, rows_ref, info_ref, cnt_ref, carry_ref):
    i = pl.program_id(0)

    @pl.when(i == 0)
    def _():
        carry_ref[...] = jnp.zeros_like(carry_ref)

    h = jnp.where(i == N_TILES - 1, hd_ref[...], hp_ref[...])
    _store_rows(rows_ref, h, TM)
    xn = _rms(h, nf_ref[...])
    logits = jnp.dot(xn, wr_ref[...], precision=lax.Precision.HIGHEST, preferred_element_type=F32) + br_ref[...]
    lane = lax.broadcasted_iota(jnp.int32, (TM, LANES), 1).astype(F32)

    def first_max(vals):
        vmax = jnp.max(vals, axis=-1, keepdims=True)
        idx = jnp.min(jnp.where(vals == vmax, lane, float(LANES)), axis=-1, keepdims=True)
        return vmax, idx

    lg = jnp.where(lane < N_EGROUPS, logits, -jnp.inf)
    gmax, g_idx = first_max(lg)
    g_p = 1.0 / jnp.sum(jnp.exp(lg - gmax), axis=-1, keepdims=True)
    e_lo = N_EGROUPS + EXPERTS_PER_GROUP * g_idx
    le = jnp.where((lane >= e_lo) & (lane < e_lo + EXPERTS_PER_GROUP), logits, -jnp.inf)
    v1, i1 = first_max(le)
    v2, i2 = first_max(jnp.where(lane == i1, -jnp.inf, le))
    e2 = jnp.exp(v2 - v1)
    w1 = (1.0 / (1.0 + e2)) * g_p
    w2 = (e2 / (1.0 + e2)) * g_p
    a1 = i1 - e_lo
    a2 = i2 - e_lo
    lo = jnp.minimum(a1, a2)
    hi = jnp.maximum(a1, a2)
    pair = jnp.where(lo == 0.0, hi - 1.0, jnp.where(lo == 1.0, jnp.where(hi == 3.0, 3.0, 4.0), 5.0))
    ea = jnp.where(pair < 3.0, 0.0, jnp.where(pair < 5.0, 1.0, 3.0))
    ga = jnp.where(a1 == ea, w1, w2)
    gb = jnp.where(a1 == ea, w2, w1)
    bucket = g_idx * N_PAIRS + pair

    onehot = (lane == bucket).astype(F32)
    r = lax.broadcasted_iota(jnp.int32, (TM, TM), 0)
    c = lax.broadcasted_iota(jnp.int32, (TM, TM), 1)
    before = _dot((c < r).astype(BF16), onehot.astype(BF16))
    carry = carry_ref[0:1, :]
    rank = jnp.sum(onehot * (before + carry), axis=-1, keepdims=True)
    new_carry = carry + jnp.sum(onehot, axis=0, keepdims=True)
    carry_ref[...] = jnp.broadcast_to(new_carry, carry_ref.shape)
    cnt_ref[...] = jnp.broadcast_to(new_carry, cnt_ref.shape)
    info_ref[...] = jnp.where(lane == 0.0, bucket,
                              jnp.where(lane == 1.0, rank, jnp.where(lane == 2.0, ga, jnp.where(lane == 3.0, gb, 0.0))))


def _router(h_prompt, h_dec, nf, wr, br):
    const = lambda *shape: pl.BlockSpec(shape, lambda i: (0,) * len(shape))
    return pl.pallas_call(
        _router_body,
        grid=(N_TILES,),
        in_specs=[pl.BlockSpec((TM, D_MODEL), lambda i: (jnp.minimum(i, N_TILES - 2), 0)), const(N_DEC, D_MODEL),
                  const(1, D_MODEL), const(D_MODEL, LANES), const(1, LANES)],
        out_specs=[pl.BlockSpec((TM * ROW_TILES, LANES), lambda i: (i, 0)),
                   pl.BlockSpec((TM, LANES), lambda i: (i, 0)),
                   const(SUBLANES, LANES)],
        out_shape=[jax.ShapeDtypeStruct((T * ROW_TILES, LANES), F32), jax.ShapeDtypeStruct((T, LANES), F32),
                   jax.ShapeDtypeStruct((SUBLANES, LANES), F32)],
        scratch_shapes=[pltpu.VMEM((SUBLANES, LANES), F32)],
        compiler_params=_cparams(),
        name="moe_router",
    )(h_prompt, h_dec, nf, wr, br)


PERM_ROWS = 256


GATHER_UNROLL = 8


def _gather_rows_start(idx_ref, base, src_ref, dst_ref, sem, n):
    def group(g, carry):
        for u in range(GATHER_UNROLL):
            r = g * GATHER_UNROLL + u
            src = pl.multiple_of(idx_ref[base + r] * ROW_TILES, ROW_TILES)
            dst = pl.multiple_of(r * ROW_TILES, ROW_TILES)
            pltpu.make_async_copy(src_ref.at[pl.ds(src, ROW_TILES), :], dst_ref.at[pl.ds(dst, ROW_TILES), :], sem).start()
        return carry

    lax.fori_loop(0, n // GATHER_UNROLL, group, 0)


def _gather_rows_wait(src_ref, dst_ref, sem, n):
    pltpu.make_async_copy(src_ref.at[pl.ds(0, n * ROW_TILES), :], dst_ref.at[pl.ds(0, n * ROW_TILES), :], sem).wait()


def _permute_body(idx_ref, src_ref, out_ref, sem):
    base = pl.program_id(0) * PERM_ROWS
    _gather_rows_start(idx_ref, base, src_ref, out_ref, sem, PERM_ROWS)
    _gather_rows_wait(src_ref, out_ref, sem, PERM_ROWS)


def _permute_rows(src, idx):
    n = idx.shape[0]
    assert n % PERM_ROWS == 0
    return pl.pallas_call(
        _permute_body,
        grid_spec=pltpu.PrefetchScalarGridSpec(
            num_scalar_prefetch=1,
            grid=(n // PERM_ROWS,),
            in_specs=[pl.BlockSpec(memory_space=pl.ANY)],
            out_specs=pl.BlockSpec((PERM_ROWS * ROW_TILES, LANES), lambda i, idx: (i, 0)),
            scratch_shapes=[pltpu.SemaphoreType.DMA],
        ),
        out_shape=jax.ShapeDtypeStruct((n * ROW_TILES, LANES), F32),
        compiler_params=_cparams(),
        name="permute_rows",
    )(idx, src)


def _moe_body(ea_ref, eb_ref, cha_ref, chb_ref, valid_ref, x_ref, g_ref, nf_ref,
              w1a_ref, w3a_ref, w2a_ref, w1b_ref, w3b_ref, w2b_ref, out_ref,
              s1a, s3a, s2a, s1b, s3b, s2b):
    i = pl.program_id(0)

    @pl.when(cha_ref[i] == 1)
    def _():
        s1a[...] = w1a_ref[...].astype(BF16)
        s3a[...] = w3a_ref[...].astype(BF16)
        s2a[...] = w2a_ref[...].astype(BF16)

    @pl.when(chb_ref[i] == 1)
    def _():
        s1b[...] = w1b_ref[...].astype(BF16)
        s3b[...] = w3b_ref[...].astype(BF16)
        s2b[...] = w2b_ref[...].astype(BF16)

    @pl.when(valid_ref[i] == 1)
    def _():
        h = _load_rows(x_ref, TM)
        xn = _rms(h, nf_ref[...]).astype(BF16)
        g = g_ref[...]

        def ffn(w1, w3, w2):
            a = _dot(xn, w1[...])
            hdn = (a * (1.0 / (1.0 + jnp.exp(-a)))) * _dot(xn, w3[...])
            return _dot(hdn.astype(BF16), w2[...])

        y = g[:, 0:1] * ffn(s1a, s3a, s2a) + g[:, 1:2] * ffn(s1b, s3b, s2b)
        _store_rows(out_ref, h + y, TM)

    @pl.when(valid_ref[i] == 0)
    def _():
        out_ref[...] = jnp.zeros_like(out_ref)


def _moe_ffn(layer, ea, eb, cha, chb, valid, xs, gates, nf, w1, w3, w2):
    wa = lambda shape: pl.BlockSpec((None, None) + shape, lambda i, ea, eb, ca, cb, va: (layer, ea[i], 0, 0))
    wb = lambda shape: pl.BlockSpec((None, None) + shape, lambda i, ea, eb, ca, cb, va: (layer, eb[i], 0, 0))
    up, down = (D_MODEL, D_EXPERT), (D_EXPERT, D_MODEL)
    return pl.pallas_call(
        _moe_body,
        grid_spec=pltpu.PrefetchScalarGridSpec(
            num_scalar_prefetch=5,
            grid=(MOE_TILES,),
            in_specs=[
                pl.BlockSpec((TM * ROW_TILES, LANES), lambda i, *_: (i, 0)),
                pl.BlockSpec((TM, 2), lambda i, *_: (i, 0)),
                pl.BlockSpec((1, D_MODEL), lambda i, *_: (0, 0)),
                wa(up), wa(up), wa(down), wb(up), wb(up), wb(down),
            ],
            out_specs=pl.BlockSpec((TM * ROW_TILES, LANES), lambda i, *_: (i, 0)),
            scratch_shapes=[pltpu.VMEM(up, BF16), pltpu.VMEM(up, BF16), pltpu.VMEM(down, BF16),
                            pltpu.VMEM(up, BF16), pltpu.VMEM(up, BF16), pltpu.VMEM(down, BF16)],
        ),
        out_shape=jax.ShapeDtypeStruct((P_ROWS * ROW_TILES, LANES), F32),
        compiler_params=_cparams(),
        name="moe_ffn",
    )(ea, eb, cha, chb, valid, xs, gates, nf, w1, w3, w2, w1, w3, w2)


def _moe_layer(layer, h_prompt, h_dec, norm_ffn, w_group, b_group, w_expert, b_expert, w1, w3, w2):
    wr = jnp.zeros((D_MODEL, LANES), F32)
    wr = wr.at[:, :N_EGROUPS].set(w_group[layer]).at[:, N_EGROUPS:N_EGROUPS + N_EXPERTS].set(w_expert[layer])
    br = jnp.zeros((1, LANES), F32)
    br = br.at[0, :N_EGROUPS].set(b_group[layer]).at[0, N_EGROUPS:N_EGROUPS + N_EXPERTS].set(b_expert[layer])
    nf = norm_ffn[layer].reshape(1, D_MODEL)
    rows, info, cnt = _router(h_prompt, h_dec, nf, wr, br)

    bucket = info[:, 0].astype(jnp.int32)
    rank = info[:, 1].astype(jnp.int32)
    counts = cnt[0, :N_BUCKETS].astype(jnp.int32)
    n_tiles = (counts + TM - 1) // TM
    tile_end = jnp.cumsum(n_tiles)
    tile_start = tile_end - n_tiles
    start_of = jnp.sum(jnp.where(bucket[:, None] == jnp.arange(N_BUCKETS)[None, :], tile_start[None, :], 0), axis=1)
    pos = start_of * TM + rank
    packed = jnp.concatenate([jnp.arange(T, dtype=F32)[:, None], info[:, 2:4]], axis=1)
    packed = jnp.zeros((P_ROWS, 3), F32).at[pos].set(packed, unique_indices=True, indices_are_sorted=False)
    idx_sorted = packed[:, 0].astype(jnp.int32)
    gates = packed[:, 1:3]
    total = tile_end[-1]
    j = jnp.minimum(jnp.arange(MOE_TILES), total - 1)
    tb = jnp.sum((j[:, None] >= tile_end[None, :]).astype(jnp.int32), axis=1)
    grp, pair = tb // N_PAIRS, tb % N_PAIRS
    ea = (grp * EXPERTS_PER_GROUP + jnp.asarray(PAIR_A, jnp.int32)[pair]).astype(jnp.int32)
    eb = (grp * EXPERTS_PER_GROUP + jnp.asarray(PAIR_B, jnp.int32)[pair]).astype(jnp.int32)
    first = jnp.arange(MOE_TILES) == 0
    cha = (first | (ea != jnp.roll(ea, 1))).astype(jnp.int32)
    chb = (first | (eb != jnp.roll(eb, 1))).astype(jnp.int32)
    valid = (jnp.arange(MOE_TILES) < total).astype(jnp.int32)

    xs = _permute_rows(rows, idx_sorted)
    ys = _moe_ffn(layer, ea, eb, cha, chb, valid, xs, gates, nf, w1, w3, w2)
    return _permute_rows(ys, jnp.concatenate([pos, jnp.zeros((N_TILES * PERM_ROWS - T,), jnp.int32)]))


def _seg_matrix():
    lane = np.arange(LANES)
    seg = np.where(lane < NOPE_DIM, 0, np.where(lane < NOPE_DIM + ROPE_DIM, 1, 2))
    return jnp.asarray(seg[:, None] == seg[None, :], BF16)


def _seg_count():
    lane = np.arange(LANES)
    return jnp.asarray(np.where(lane < NOPE_DIM, 1.0 / NOPE_DIM, 1.0 / ROPE_DIM), F32).reshape(1, LANES)


def _rope_swap(x):
    lane = lax.broadcasted_iota(jnp.int32, x.shape, 1)
    return jnp.where(lane < ROPE_LO + ROPE_HALF, pltpu.roll(x, LANES - ROPE_HALF, 1), pltpu.roll(x, ROPE_HALF, 1))


def _head_norm(x, seg, cnt, gain):
    ms = _split_dot(x * x, seg) * cnt
    return x * lax.rsqrt(ms + EPS) * gain


def _expand_kv(ckv, kr, wuk_ref, wuv_ref, seg, cnt, kg_ref, k_ref):
    cb = ckv.astype(BF16)
    kn = _dot(cb, wuk_ref[...])
    for hh in range(N_HEADS):
        cols = slice(hh * HEAD_W, (hh + 1) * HEAD_W)
        k_ref[:, cols] = (_head_norm(kn[:, cols], seg, cnt, kg_ref[...]) + kr).astype(BF16)
    return _dot(cb, wuv_ref[...])


def _mla_proj_body(rows_ref, cos_ref, sin_ref, kvn_ref, wdkv_ref, kvan_ref, krg_ref, wuk_ref, wuv_ref, kg_ref,
                   nmq_ref, wdq_ref, qan_ref, wuq_ref, qg_ref, seg_ref, cnt_ref,
                   ckv_ref, krope_ref, k_ref, v_ref, vt_ref, qt_ref, qdec_ref):
    is_dec = pl.program_id(0) == N_TILES - 1
    h = _load_rows(rows_ref, TM)
    seg, cnt = seg_ref[...], cnt_ref[...]
    cos, sin = cos_ref[...], sin_ref[...]
    c = _dot(_rms(h, kvn_ref[...]).astype(BF16), wdkv_ref[...])
    ckv = _rms(c[:, :KV_RANK], kvan_ref[...])
    ckv_ref[...] = ckv
    kr = c[:, KV_RANK:]
    kr = kr * lax.rsqrt(jnp.sum(kr * kr, axis=-1, keepdims=True) * (1.0 / ROPE_DIM) + EPS) * krg_ref[...]
    kr = kr * cos + _rope_swap(kr) * sin
    krope_ref[...] = kr[:, ROPE_LO:ROPE_LO + ROPE_DIM]
    v = _expand_kv(ckv, kr, wuk_ref, wuv_ref, seg, cnt, kg_ref, k_ref)
    v_ref[...] = v.astype(BF16)
    vt_ref[0] = v.T.astype(BF16)
    cq = _rms(_dot(_rms(h, nmq_ref[...]).astype(BF16), wdq_ref[...]), qan_ref[...]).astype(BF16)
    q = _dot(cq, wuq_ref[...])
    for hh in range(N_HEADS):
        cols = slice(hh * HEAD_W, (hh + 1) * HEAD_W)
        qh = _head_norm(q[:, cols], seg, cnt, qg_ref[...])
        qh = (qh * cos + _rope_swap(qh) * sin) * SCALE
        qt_ref[0, cols, :] = qh.T.astype(BF16)

        @pl.when(is_dec)
        def _():
            qdec_ref[:, cols] = qh.astype(BF16)


def _mla_proj(rows, cos_t, sin_t, kvn, wdkv, kvan, krg, wuk, wuv, kg, nmq, wdq, qan, wuq, qg):
    const = lambda *shape: pl.BlockSpec(shape, lambda i: (0,) * len(shape))
    tab = pl.BlockSpec((TM, LANES), lambda i: (jnp.where(i < N_PROMPT // TM, i % (SEQ // TM), SEQ // TM), 0))
    row = lambda w: pl.BlockSpec((TM, w), lambda i: (i, 0))
    return pl.pallas_call(
        _mla_proj_body,
        grid=(N_TILES,),
        in_specs=[
            pl.BlockSpec((TM * ROW_TILES, LANES), lambda i: (i, 0)), tab, tab,
            const(1, D_MODEL), const(D_MODEL, KV_RANK + LANES), const(1, KV_RANK), const(1, LANES),
            const(KV_RANK, N_HEADS * HEAD_W), const(KV_RANK, N_HEADS * V_DIM), const(1, LANES),
            const(1, D_MODEL), const(D_MODEL, Q_RANK), const(1, Q_RANK), const(Q_RANK, N_HEADS * HEAD_W),
            const(1, LANES), const(LANES, LANES), const(1, LANES),
        ],
        out_specs=[row(KV_RANK), row(ROPE_DIM), row(N_HEADS * HEAD_W), row(N_HEADS * V_DIM),
                   pl.BlockSpec((1, N_HEADS * V_DIM, TM), lambda i: (i, 0, 0)),
                   pl.BlockSpec((1, N_HEADS * HEAD_W, TM), lambda i: (i, 0, 0)), const(N_DEC, N_HEADS * HEAD_W)],
        out_shape=[
            jax.ShapeDtypeStruct((T, KV_RANK), F32), jax.ShapeDtypeStruct((T, ROPE_DIM), F32),
            jax.ShapeDtypeStruct((T, N_HEADS * HEAD_W), BF16), jax.ShapeDtypeStruct((T, N_HEADS * V_DIM), BF16),
            jax.ShapeDtypeStruct((N_TILES, N_HEADS * V_DIM, TM), BF16),
            jax.ShapeDtypeStruct((N_TILES, N_HEADS * HEAD_W, TM), BF16),
            jax.ShapeDtypeStruct((N_DEC, N_HEADS * HEAD_W), BF16),
        ],
        compiler_params=_cparams(),
        name="mla_proj",
    )(rows, cos_t, sin_t, kvn, wdkv, kvan, krg, wuk, wuv, kg, nmq, wdq, qan, wuq, qg, _seg_matrix(), _seg_count())


def _cache_kv_body(ckv_ref, kr_ref, place_ref, wuk_ref, wuv_ref, kg_ref, seg_ref, cnt_ref, k_ref, v_ref):
    kr = _dot(kr_ref[...].astype(BF16), place_ref[...])
    v_ref[...] = _expand_kv(ckv_ref[...], kr, wuk_ref, wuv_ref, seg_ref[...], cnt_ref[...], kg_ref, k_ref).astype(BF16)


def _cache_kv(ckv, kr, wuk, wuv, kg):
    n = ckv.shape[0]
    place = jnp.asarray(np.arange(ROPE_DIM)[:, None] + ROPE_LO == np.arange(LANES)[None, :], BF16)
    const = lambda *shape: pl.BlockSpec(shape, lambda i: (0,) * len(shape))
    row = lambda w: pl.BlockSpec((TM, w), lambda i: (i, 0))
    return pl.pallas_call(
        _cache_kv_body,
        grid=(n // TM,),
        in_specs=[row(KV_RANK), row(ROPE_DIM), const(ROPE_DIM, LANES), const(KV_RANK, N_HEADS * HEAD_W),
                  const(KV_RANK, N_HEADS * V_DIM), const(1, LANES), const(LANES, LANES), const(1, LANES)],
        out_specs=[row(N_HEADS * HEAD_W), row(N_HEADS * V_DIM)],
        out_shape=[jax.ShapeDtypeStruct((n, N_HEADS * HEAD_W), BF16), jax.ShapeDtypeStruct((n, N_HEADS * V_DIM), BF16)],
        compiler_params=_cparams(),
        name="cache_kv",
    )(ckv, kr, place, wuk, wuv, kg, _seg_matrix(), _seg_count())


TQ = 256
TK = 256
assert TQ == TK and TQ % CHUNK == 0
SCORE_LOOKAHEAD = 4


def _qk(q, k):
    return lax.dot_general(q, k, (((1,), (1,)), ((), ())), preferred_element_type=F32)


def _merge_heads(o_ref, outs, rows):
    lane = lax.broadcasted_iota(jnp.int32, (rows, LANES), 1)
    for pr in range(N_HEADS // 2):
        o_ref[:, pr * LANES:(pr + 1) * LANES] = jnp.where(lane < V_DIM, outs[2 * pr], outs[2 * pr + 1]).astype(BF16)


def _prompt_attn_body(qt_ref, k_ref, vt_ref, rows_ref, wo_ref, out_ref, m_scr, l_scr, acc_scr):
    qi = pl.program_id(1)
    m_scr[...] = jnp.full(m_scr.shape, NEG, F32)
    l_scr[...] = jnp.zeros(l_scr.shape, F32)
    acc_scr[...] = jnp.zeros(acc_scr.shape, F32)

    def all_heads(j, mask):
        ks = pl.ds(pl.multiple_of(j * TK, TK), TK)

        def scores(hh):
            hcols = slice(hh * HEAD_W, (hh + 1) * HEAD_W)
            return _dot(k_ref[ks, hcols], qt_ref[0, hcols, :])

        ahead = [scores(hh) for hh in range(SCORE_LOOKAHEAD)]
        for hh in range(N_HEADS):
            vrows = slice((hh // 2) * LANES, (hh // 2 + 1) * LANES)
            s = ahead.pop(0)
            if hh + SCORE_LOOKAHEAD < N_HEADS:
                ahead.append(scores(hh + SCORE_LOOKAHEAD))
            if mask is not None:
                s = jnp.where(mask, s, NEG)
            m_old = m_scr[hh]
            m_new = jnp.maximum(m_old, jnp.max(s, axis=0, keepdims=True))
            alpha = jnp.exp(m_old - m_new)
            p = jnp.exp(s - m_new)
            m_scr[hh] = m_new
            l_scr[hh] = alpha * l_scr[hh] + jnp.sum(p, axis=0, keepdims=True)
            acc_scr[hh] = alpha * acc_scr[hh] + _dot(vt_ref[j, vrows, :], p.astype(BF16))

    def step(j, carry):
        all_heads(j, None)
        return carry

    lax.fori_loop(0, qi, step, 0)
    kc = lax.broadcasted_iota(jnp.int32, (TK, TQ), 0) // CHUNK
    qc = lax.broadcasted_iota(jnp.int32, (TK, TQ), 1) // CHUNK
    all_heads(qi, kc <= qc)
    row = lax.broadcasted_iota(jnp.int32, (LANES, TQ), 0)
    pairs = []
    for pr in range(N_HEADS // 2):
        even = acc_scr[2 * pr] / l_scr[2 * pr]
        odd = acc_scr[2 * pr + 1] / l_scr[2 * pr + 1]
        pairs.append(jnp.where(row < V_DIM, even, odd))
    o = jnp.concatenate(pairs, axis=0).T.astype(BF16)
    out_ref[...] = _load_rows(rows_ref, TQ) + _dot(o, wo_ref[...])


def _prompt_attn(qt, k, vt, rows, wo):
    nq = SEQ // TQ
    return pl.pallas_call(
        _prompt_attn_body,
        grid=(BATCH, nq),
        in_specs=[
            pl.BlockSpec((1, N_HEADS * HEAD_W, TQ), lambda b, i: (b * nq + i, 0, 0)),
            pl.BlockSpec((SEQ, N_HEADS * HEAD_W), lambda b, i: (b, 0)),
            pl.BlockSpec((SEQ // TK, N_HEADS * V_DIM, TK), lambda b, i: (b, 0, 0)),
            pl.BlockSpec((TQ * ROW_TILES, LANES), lambda b, i: (b * nq + i, 0)),
            pl.BlockSpec((N_HEADS * V_DIM, D_MODEL), lambda b, i: (0, 0)),
        ],
        out_specs=pl.BlockSpec((TQ, D_MODEL), lambda b, i: (b * nq + i, 0)),
        out_shape=jax.ShapeDtypeStruct((N_PROMPT, D_MODEL), F32),
        scratch_shapes=[pltpu.VMEM((N_HEADS, 1, TQ), F32), pltpu.VMEM((N_HEADS, 1, TQ), F32),
                        pltpu.VMEM((N_HEADS, LANES, TQ), F32)],
        compiler_params=_cparams(2),
        name="prompt_attn",
    )(qt, k, vt, rows, wo)


def _sample_attn_body(q_ref, kc_ref, vc_ref, kn_ref, vn_ref, rows_ref, wo_ref, out_ref, o_scr):
    outs = []
    for hh in range(N_HEADS):
        hcols = slice(hh * HEAD_W, (hh + 1) * HEAD_W)
        vcols = slice((hh // 2) * LANES, (hh // 2 + 1) * LANES)
        q = q_ref[:, hcols]
        sc = _qk(q, kc_ref[:, hcols])
        sn = _qk(q, kn_ref[:, hcols])
        m = jnp.maximum(jnp.max(sc, axis=-1, keepdims=True), jnp.max(sn, axis=-1, keepdims=True))
        pc = jnp.exp(sc - m)
        pn = jnp.exp(sn - m)
        l = jnp.sum(pc, axis=-1, keepdims=True) + jnp.sum(pn, axis=-1, keepdims=True)
        acc = _dot(pc.astype(BF16), vc_ref[:, vcols]) + _dot(pn.astype(BF16), vn_ref[:, vcols])
        outs.append(acc / l)
    _merge_heads(o_scr, outs, DEC_SEQ)
    out_ref[...] = _load_rows(rows_ref, DEC_SEQ) + _dot(o_scr[...], wo_ref[...])


def _sample_attn(q, kc, vc, kn, vn, rows, wo):
    off = N_PROMPT // DEC_SEQ
    return pl.pallas_call(
        _sample_attn_body,
        grid=(DEC_BATCH,),
        in_specs=[
            pl.BlockSpec((DEC_SEQ, N_HEADS * HEAD_W), lambda b: (b, 0)),
            pl.BlockSpec((PAST_LEN, N_HEADS * HEAD_W), lambda b: (b, 0)),
            pl.BlockSpec((PAST_LEN, N_HEADS * V_DIM), lambda b: (b, 0)),
            pl.BlockSpec((DEC_SEQ, N_HEADS * HEAD_W), lambda b: (off + b, 0)),
            pl.BlockSpec((DEC_SEQ, N_HEADS * V_DIM), lambda b: (off + b, 0)),
            pl.BlockSpec((DEC_SEQ * ROW_TILES, LANES), lambda b: (off + b, 0)),
            pl.BlockSpec((N_HEADS * V_DIM, D_MODEL), lambda b: (0, 0)),
        ],
        out_specs=pl.BlockSpec((DEC_SEQ, D_MODEL), lambda b: (b, 0)),
        out_shape=jax.ShapeDtypeStruct((N_DEC, D_MODEL), F32),
        scratch_shapes=[pltpu.VMEM((DEC_SEQ, N_HEADS * V_DIM), BF16)],
        compiler_params=_cparams(),
        name="sample_attn",
    )(q, kc, vc, kn, vn, rows, wo)


def _finish_body(rows_ref, yp_ref, ys_ref):
    i = pl.program_id(0)
    y = _load_rows(rows_ref, TM)

    @pl.when(i < N_TILES - 1)
    def _():
        yp_ref[...] = y

    @pl.when(i == N_TILES - 1)
    def _():
        ys_ref[...] = y


def _finish(rows):
    return pl.pallas_call(
        _finish_body,
        grid=(N_TILES,),
        in_specs=[pl.BlockSpec((TM * ROW_TILES, LANES), lambda i: (i, 0))],
        out_specs=[pl.BlockSpec((TM, D_MODEL), lambda i: (jnp.minimum(i, N_TILES - 2), 0)),
                   pl.BlockSpec((N_DEC, D_MODEL), lambda i: (0, 0))],
        out_shape=[jax.ShapeDtypeStruct((N_PROMPT, D_MODEL), F32), jax.ShapeDtypeStruct((N_DEC, D_MODEL), F32)],
        compiler_params=_cparams(),
        name="finish",
    )(rows)


def _rope_tables():
    half = ROPE_DIM // 2
    inv_freq = ROPE_THETA ** (-jnp.arange(half, dtype=F32) / half)
    dec_pos = PAST_LEN + jnp.tile(jnp.arange(DEC_SEQ, dtype=jnp.int32), DEC_BATCH)
    pos = jnp.concatenate([jnp.arange(SEQ, dtype=jnp.int32), dec_pos])
    ang = pos.astype(F32)[:, None] * inv_freq[None, :]
    cos, sin = jnp.cos(ang), jnp.sin(ang)
    n = pos.shape[0]
    cos_t = jnp.ones((n, LANES), F32).at[:, ROPE_LO:ROPE_LO + ROPE_DIM].set(jnp.concatenate([cos, cos], axis=1))
    sin_t = jnp.zeros((n, LANES), F32).at[:, ROPE_LO:ROPE_LO + ROPE_DIM].set(jnp.concatenate([-sin, sin], axis=1))
    return cos_t, sin_t


def _on_lanes(vec, lo):
    return jnp.zeros((1, LANES), F32).at[0, lo:lo + vec.shape[0]].set(vec)


def kernel(x_prompt, x_sample, cache_ckv, cache_krope, norm_mix, norm_ffn, gm_w_in, gm_b_in, gm_ln_g, gm_ln_b, gm_w_s, gm_b_s, gm_w_out, gm_b_out, kv_norm, w_dkv, kv_a_norm, k_rope_norm, w_uk, w_uv, k_nope_norm, w_dq, q_a_norm, w_uq, q_nope_norm, q_rope_norm, w_o, moe_w_group, moe_b_group, moe_w_expert, moe_b_expert, moe_w1, moe_w3, moe_w2):
    x = jnp.concatenate([x_prompt.reshape(N_PROMPT, D_MODEL), x_sample.reshape(N_DEC, D_MODEL)], axis=0)

    idx = np.arange(GMLP_BLOCK)
    allowed = (idx[None, :] // CHUNK) <= (idx[:, None] // CHUNK)
    ws_p = jnp.where(allowed[None], gm_w_s[0], 0.0).astype(BF16)
    same_seq = (idx[None, :] // DEC_SEQ) == (idx[:, None] // DEC_SEQ)
    ws_d = jnp.where(same_seq[None], jnp.tile(gm_w_s[0][:, :DEC_SEQ, :DEC_SEQ], (1, GMLP_BLOCK // DEC_SEQ, GMLP_BLOCK // DEC_SEQ)), 0.0).astype(BF16)
    bs_p = gm_b_s[0][:, :, None]
    bs_d = jnp.tile(gm_b_s[0][:, :DEC_SEQ], (1, GMLP_BLOCK // DEC_SEQ))[:, :, None]
    h, v_rows = _gmlp_layer(
        x, norm_mix[0].reshape(1, -1), gm_w_in[0].astype(BF16), gm_b_in[0].reshape(1, -1),
        gm_ln_g[0].reshape(1, -1), gm_ln_b[0].reshape(1, -1), ws_p, ws_d, bs_p, bs_d,
        gm_w_out[0].astype(BF16), gm_b_out[0].reshape(1, -1))
    rows = _moe_layer(0, h, h[N_PROMPT:], norm_ffn, moe_w_group, moe_b_group, moe_w_expert, moe_b_expert, moe_w1, moe_w3, moe_w2)

    cos_t, sin_t = _rope_tables()
    wdkv = jnp.zeros((D_MODEL, KV_RANK + LANES), F32).at[:, :KV_RANK].set(w_dkv[:, :KV_RANK])
    wdkv = wdkv.at[:, KV_RANK + ROPE_LO:KV_RANK + ROPE_LO + ROPE_DIM].set(w_dkv[:, KV_RANK:]).astype(BF16)
    wuk = jnp.zeros((KV_RANK, N_HEADS, HEAD_W), F32).at[:, :, :NOPE_DIM].set(w_uk).reshape(KV_RANK, -1).astype(BF16)
    wuv = w_uv.reshape(KV_RANK, -1).astype(BF16)
    wuq = jnp.zeros((Q_RANK, N_HEADS, HEAD_W), F32).at[:, :, :NOPE_DIM + ROPE_DIM].set(w_uq[0]).reshape(Q_RANK, -1).astype(BF16)
    kg = _on_lanes(k_nope_norm, 0)
    krg = _on_lanes(k_rope_norm, ROPE_LO)
    qg = _on_lanes(jnp.concatenate([q_nope_norm[0], q_rope_norm[0]]), 0)
    ckv, krope, k_new, v_new, vt_new, qt, q_dec = _mla_proj(
        rows, cos_t, sin_t, kv_norm.reshape(1, -1), wdkv, kv_a_norm.reshape(1, -1), krg, wuk, wuv, kg,
        norm_mix[1].reshape(1, -1), w_dq[0].astype(BF16), q_a_norm[0].reshape(1, -1), wuq, qg)
    k_cache, v_cache = _cache_kv(cache_ckv.reshape(-1, KV_RANK), cache_krope.reshape(-1, ROPE_DIM), wuk, wuv, kg)

    wo = w_o[0].astype(BF16)
    h_prompt = _prompt_attn(qt, k_new, vt_new, rows, wo)
    h_dec = _sample_attn(q_dec, k_cache, v_cache, k_new, v_new, rows, wo)
    rows = _moe_layer(1, h_prompt, h_dec, norm_ffn, moe_w_group, moe_b_group, moe_w_expert, moe_b_expert, moe_w1, moe_w3, moe_w2)
    y_prompt, y_sample = _finish(rows)

    return (y_prompt.reshape(BATCH, SEQ, D_MODEL), y_sample.reshape(DEC_BATCH, DEC_SEQ, D_MODEL),
            ckv[:N_PROMPT].reshape(BATCH, SEQ, KV_RANK), krope[:N_PROMPT].reshape(BATCH, SEQ, ROPE_DIM),
            ckv[N_PROMPT:].reshape(DEC_BATCH, DEC_SEQ, KV_RANK), krope[N_PROMPT:].reshape(DEC_BATCH, DEC_SEQ, ROPE_DIM),
            v_rows.reshape(1, DEC_BATCH, DEC_SEQ, D_GATE))
```

```python
import functools

import jax
import jax.numpy as jnp
import numpy as np
from jax import lax
from jax.experimental import pallas as pl
from jax.experimental.pallas import tpu as pltpu

F32 = jnp.float32
BF16 = jnp.bfloat16

D_MODEL = 1024
BATCH = 8
SEQ = 2048
DEC_BATCH = 16
DEC_SEQ = 16
PAST_LEN = 2048
CHUNK = 64
GMLP_BLOCK = 128
D_GATE = 2 * D_MODEL
N_SG = 8
SG_W = D_GATE // N_SG
N_HEADS = 8
NOPE_DIM = 64
ROPE_DIM = 32
V_DIM = 64
Q_RANK = 384
KV_RANK = 256
ROPE_THETA = 10000.0
SCALE = (NOPE_DIM + ROPE_DIM) ** -0.5
N_EGROUPS = 4
EXPERTS_PER_GROUP = 4
N_EXPERTS = N_EGROUPS * EXPERTS_PER_GROUP
D_EXPERT = 512
EPS = 1e-6
NEG = -1e30

LANES = 128
SUBLANES = 8
ROW_TILES = D_MODEL // LANES
assert ROW_TILES == SUBLANES

N_PROMPT = BATCH * SEQ
N_DEC = DEC_BATCH * DEC_SEQ
T = N_PROMPT + N_DEC
TM = 256
assert N_PROMPT % TM == 0 and N_DEC == TM
N_TILES = T // TM
HEAD_W = LANES
ROPE_LO = NOPE_DIM
ROPE_HALF = ROPE_DIM // 2

PAIR_A = (0, 0, 0, 1, 1, 3)
PAIR_B = (1, 2, 3, 3, 2, 2)
N_PAIRS = 6
N_BUCKETS = N_EGROUPS * N_PAIRS
MOE_TILES = (T + N_BUCKETS * (TM - 1) + TM - 1) // TM
P_ROWS = MOE_TILES * TM

VMEM_LIMIT = 56 * 1024 * 1024


def _cparams(n_axes=1, vmem=VMEM_LIMIT):
    return pltpu.CompilerParams(dimension_semantics=("arbitrary",) * n_axes, vmem_limit_bytes=vmem)


def _rms(x, g):
    return x * lax.rsqrt(jnp.mean(x * x, axis=-1, keepdims=True) + EPS) * g


def _load_rows(ref, n):
    return jnp.concatenate([ref[pl.ds(s, n, stride=ROW_TILES), :] for s in range(ROW_TILES)], axis=1)


def _store_rows(ref, x, n):
    for s in range(ROW_TILES):
        ref[pl.ds(s, n, stride=ROW_TILES), :] = x[:, s * LANES:(s + 1) * LANES]


def _dot(a, b):
    return jnp.dot(a, b, preferred_element_type=F32)


def _split_dot(x, m):
    hi = x.astype(BF16)
    lo = (x - hi.astype(F32)).astype(BF16)
    return _dot(hi, m) + _dot(lo, m)


def _gmlp_body(x_ref, nm_ref, win_ref, bin_ref, lng_ref, lnb_ref, wsp_ref, wsd_ref, bsp_ref, bsd_ref,
               wout_ref, bout_ref, h_ref, v_ref, gated_ref):
    i = pl.program_id(0)
    is_dec = i == N_TILES - 1
    x = x_ref[...]
    xn = _rms(x, nm_ref[...]).astype(BF16)
    z = _dot(xn, win_ref[...]) + bin_ref[...]
    z = z * (0.5 * (1.0 + jnp.tanh(np.float32(np.sqrt(2.0 / np.pi)) * (z + 0.044715 * (z * z * z)))))
    u = z[:, :D_GATE]
    v = z[:, D_GATE:]
    mu = jnp.mean(v, axis=-1, keepdims=True)
    vc = v - mu
    var = jnp.mean(vc * vc, axis=-1, keepdims=True)
    v = vc * lax.rsqrt(var + EPS) * lng_ref[...] + lnb_ref[...]

    @pl.when(is_dec)
    def _():
        v_ref[...] = v

    vb = v.astype(BF16)
    for g in range(N_SG):
        ws = jnp.where(is_dec, wsd_ref[g], wsp_ref[g])
        bs = jnp.where(is_dec, bsd_ref[g], bsp_ref[g])
        for b in range(TM // GMLP_BLOCK):
            rows = slice(b * GMLP_BLOCK, (b + 1) * GMLP_BLOCK)
            cols = slice(g * SG_W, (g + 1) * SG_W)
            s = _dot(ws, vb[rows, cols]) + bs
            gated_ref[rows, cols] = (u[rows, cols] * s).astype(BF16)
    h_ref[...] = x + _dot(gated_ref[...], wout_ref[...]) + bout_ref[...]


def _gmlp_layer(x, nm, w_in, b_in, ln_g, ln_b, ws_p, ws_d, bs_p, bs_d, w_out, b_out):
    const = lambda *shape: pl.BlockSpec(shape, lambda i: (0,) * len(shape))
    return pl.pallas_call(
        _gmlp_body,
        grid=(N_TILES,),
        in_specs=[
            pl.BlockSpec((TM, D_MODEL), lambda i: (i, 0)),
            const(1, D_MODEL), const(D_MODEL, 2 * D_GATE), const(1, 2 * D_GATE),
            const(1, D_GATE), const(1, D_GATE),
            const(N_SG, GMLP_BLOCK, GMLP_BLOCK), const(N_SG, GMLP_BLOCK, GMLP_BLOCK),
            const(N_SG, GMLP_BLOCK, 1), const(N_SG, GMLP_BLOCK, 1),
            const(D_GATE, D_MODEL), const(1, D_MODEL),
        ],
        out_specs=[
            pl.BlockSpec((TM, D_MODEL), lambda i: (i, 0)),
            const(N_DEC, D_GATE),
        ],
        out_shape=[jax.ShapeDtypeStruct((T, D_MODEL), F32), jax.ShapeDtypeStruct((N_DEC, D_GATE), F32)],
        scratch_shapes=[pltpu.VMEM((TM, D_GATE), BF16)],
        compiler_params=_cparams(),
        name="gmlp_layer",
    )(x, nm, w_in, b_in, ln_g, ln_b, ws_p, ws_d, bs_p, bs_d, w_out, b_out)


def _router_body(hp_ref, hd_ref, nf_ref, wr_ref, br_ref, rows_ref, info_ref, cnt_ref, carry_ref):
    i = pl.program_id(0)

    @pl.when(i == 0)
    def _():
        carry_ref[...] = jnp.zeros_like(carry_ref)

    h = jnp.where(i == N_TILES - 1, hd_ref[...], hp_ref[...])
    _store_rows(rows_ref, h, TM)
    xn = _rms(h, nf_ref[...])
    logits = jnp.dot(xn, wr_ref[...], precision=lax.Precision.HIGHEST, preferred_element_type=F32) + br_ref[...]
    lane = lax.broadcasted_iota(jnp.int32, (TM, LANES), 1).astype(F32)

    def first_max(vals):
        vmax = jnp.max(vals, axis=-1, keepdims=True)
        idx = jnp.min(jnp.where(vals == vmax, lane, float(LANES)), axis=-1, keepdims=True)
        return vmax, idx

    lg = jnp.where(lane < N_EGROUPS, logits, -jnp.inf)
    gmax, g_idx = first_max(lg)
    g_p = 1.0 / jnp.sum(jnp.exp(lg - gmax), axis=-1, keepdims=True)
    e_lo = N_EGROUPS + EXPERTS_PER_GROUP * g_idx
    le = jnp.where((lane >= e_lo) & (lane < e_lo + EXPERTS_PER_GROUP), logits, -jnp.inf)
    v1, i1 = first_max(le)
    v2, i2 = first_max(jnp.where(lane == i1, -jnp.inf, le))
    e2 = jnp.exp(v2 - v1)
    w1 = (1.0 / (1.0 + e2)) * g_p
    w2 = (e2 / (1.0 + e2)) * g_p
    a1 = i1 - e_lo
    a2 = i2 - e_lo
    lo = jnp.minimum(a1, a2)
    hi = jnp.maximum(a1, a2)
    pair = jnp.where(lo == 0.0, hi - 1.0, jnp.where(lo == 1.0, jnp.where(hi == 3.0, 3.0, 4.0), 5.0))
    ea = jnp.where(pair < 3.0, 0.0, jnp.where(pair < 5.0, 1.0, 3.0))
    ga = jnp.where(a1 == ea, w1, w2)
    gb = jnp.where(a1 == ea, w2, w1)
    bucket = g_idx * N_PAIRS + pair

    onehot = (lane == bucket).astype(F32)
    r = lax.broadcasted_iota(jnp.int32, (TM, TM), 0)
    c = lax.broadcasted_iota(jnp.int32, (TM, TM), 1)
    before = _dot((c < r).astype(BF16), onehot.astype(BF16))
    carry = carry_ref[0:1, :]
    rank = jnp.sum(onehot * (before + carry), axis=-1, keepdims=True)
    new_carry = carry + jnp.sum(onehot, axis=0, keepdims=True)
    carry_ref[...] = jnp.broadcast_to(new_carry, carry_ref.shape)
    cnt_ref[...] = jnp.broadcast_to(new_carry, cnt_ref.shape)
    info_ref[...] = jnp.where(lane == 0.0, bucket,
                              jnp.where(lane == 1.0, rank, jnp.where(lane == 2.0, ga, jnp.where(lane == 3.0, gb, 0.0))))


def _router(h_prompt, h_dec, nf, wr, br):
    const = lambda *shape: pl.BlockSpec(shape, lambda i: (0,) * len(shape))
    return pl.pallas_call(
        _router_body,
        grid=(N_TILES,),
        in_specs=[pl.BlockSpec((TM, D_MODEL), lambda i: (jnp.minimum(i, N_TILES - 2), 0)), const(N_DEC, D_MODEL),
                  const(1, D_MODEL), const(D_MODEL, LANES), const(1, LANES)],
        out_specs=[pl.BlockSpec((TM * ROW_TILES, LANES), lambda i: (i, 0)),
                   pl.BlockSpec((TM, LANES), lambda i: (i, 0)),
                   const(SUBLANES, LANES)],
        out_shape=[jax.ShapeDtypeStruct((T * ROW_TILES, LANES), F32), jax.ShapeDtypeStruct((T, LANES), F32),
                   jax.ShapeDtypeStruct((SUBLANES, LANES), F32)],
        scratch_shapes=[pltpu.VMEM((SUBLANES, LANES), F32)],
        compiler_params=_cparams(),
        name="moe_router",
    )(h_prompt, h_dec, nf, wr, br)


GATHER_UNROLL = 8


def _gather_rows_start(idx_ref, base, src_ref, dst_ref, sem, n):
    def group(g, carry):
        for u in range(GATHER_UNROLL):
            r = g * GATHER_UNROLL + u
            src = pl.multiple_of(idx_ref[base + r] * ROW_TILES, ROW_TILES)
            dst = pl.multiple_of(r * ROW_TILES, ROW_TILES)
            pltpu.make_async_copy(src_ref.at[pl.ds(src, ROW_TILES), :], dst_ref.at[pl.ds(dst, ROW_TILES), :], sem).start()
        return carry

    lax.fori_loop(0, n // GATHER_UNROLL, group, 0)


def _gather_rows_wait(src_ref, dst_ref, sem, n):
    pltpu.make_async_copy(src_ref.at[pl.ds(0, n * ROW_TILES), :], dst_ref.at[pl.ds(0, n * ROW_TILES), :], sem).wait()


def _gather_tile(idx_ref, src_ref, buf, sem, n):
    i = pl.program_id(0)
    slot = lax.rem(i, 2)

    @pl.when(i == 0)
    def _():
        _gather_rows_start(idx_ref, 0, src_ref, buf.at[0], sem.at[0], n)

    @pl.when(i + 1 < pl.num_programs(0))
    def _():
        _gather_rows_start(idx_ref, (i + 1) * n, src_ref, buf.at[1 - slot], sem.at[1 - slot], n)

    _gather_rows_wait(src_ref, buf.at[slot], sem.at[slot], n)
    return buf.at[slot]


def _gather_scratch(n):
    return [pltpu.VMEM((2, n * ROW_TILES, LANES), F32), pltpu.SemaphoreType.DMA((2,))]


def _moe_body(idx_ref, ea_ref, eb_ref, cha_ref, chb_ref, valid_ref, rows_hbm, g_ref, nf_ref,
              w1a_ref, w3a_ref, w2a_ref, w1b_ref, w3b_ref, w2b_ref, out_ref,
              s1a, s3a, s2a, s1b, s3b, s2b, xbuf, xsem):
    i = pl.program_id(0)
    x_ref = _gather_tile(idx_ref, rows_hbm, xbuf, xsem, TM)

    @pl.when(cha_ref[i] == 1)
    def _():
        s1a[...] = w1a_ref[...].astype(BF16)
        s3a[...] = w3a_ref[...].astype(BF16)
        s2a[...] = w2a_ref[...].astype(BF16)

    @pl.when(chb_ref[i] == 1)
    def _():
        s1b[...] = w1b_ref[...].astype(BF16)
        s3b[...] = w3b_ref[...].astype(BF16)
        s2b[...] = w2b_ref[...].astype(BF16)

    @pl.when(valid_ref[i] == 1)
    def _():
        h = _load_rows(x_ref, TM)
        xn = _rms(h, nf_ref[...]).astype(BF16)
        g = g_ref[...]

        def ffn(w1, w3, w2):
            a = _dot(xn, w1[...])
            hdn = (a * (1.0 / (1.0 + jnp.exp(-a)))) * _dot(xn, w3[...])
            return _dot(hdn.astype(BF16), w2[...])

        y = g[:, 0:1] * ffn(s1a, s3a, s2a) + g[:, 1:2] * ffn(s1b, s3b, s2b)
        _store_rows(out_ref, h + y, TM)

    @pl.when(valid_ref[i] == 0)
    def _():
        out_ref[...] = jnp.zeros_like(out_ref)


def _moe_ffn(layer, idx_sorted, ea, eb, cha, chb, valid, rows, gates, nf, w1, w3, w2):
    wa = lambda shape: pl.BlockSpec((None, None) + shape, lambda i, ix, ea, eb, ca, cb, va: (layer, ea[i], 0, 0))
    wb = lambda shape: pl.BlockSpec((None, None) + shape, lambda i, ix, ea, eb, ca, cb, va: (layer, eb[i], 0, 0))
    up, down = (D_MODEL, D_EXPERT), (D_EXPERT, D_MODEL)
    return pl.pallas_call(
        _moe_body,
        grid_spec=pltpu.PrefetchScalarGridSpec(
            num_scalar_prefetch=6,
            grid=(MOE_TILES,),
            in_specs=[
                pl.BlockSpec(memory_space=pl.ANY),
                pl.BlockSpec((TM, 2), lambda i, *_: (i, 0)),
                pl.BlockSpec((1, D_MODEL), lambda i, *_: (0, 0)),
                wa(up), wa(up), wa(down), wb(up), wb(up), wb(down),
            ],
            out_specs=pl.BlockSpec((TM * ROW_TILES, LANES), lambda i, *_: (i, 0)),
            scratch_shapes=[pltpu.VMEM(up, BF16), pltpu.VMEM(up, BF16), pltpu.VMEM(down, BF16),
                            pltpu.VMEM(up, BF16), pltpu.VMEM(up, BF16), pltpu.VMEM(down, BF16)]
            + _gather_scratch(TM),
        ),
        out_shape=jax.ShapeDtypeStruct((P_ROWS * ROW_TILES, LANES), F32),
        compiler_params=_cparams(),
        name="moe_ffn",
    )(idx_sorted, ea, eb, cha, chb, valid, rows, gates, nf, w1, w3, w2, w1, w3, w2)


def _moe_layer(layer, h_prompt, h_dec, norm_ffn, w_group, b_group, w_expert, b_expert, w1, w3, w2):
    wr = jnp.zeros((D_MODEL, LANES), F32)
    wr = wr.at[:, :N_EGROUPS].set(w_group[layer]).at[:, N_EGROUPS:N_EGROUPS + N_EXPERTS].set(w_expert[layer])
    br = jnp.zeros((1, LANES), F32)
    br = br.at[0, :N_EGROUPS].set(b_group[layer]).at[0, N_EGROUPS:N_EGROUPS + N_EXPERTS].set(b_expert[layer])
    nf = norm_ffn[layer].reshape(1, D_MODEL)
    rows, info, cnt = _router(h_prompt, h_dec, nf, wr, br)

    bucket = info[:, 0].astype(jnp.int32)
    rank = info[:, 1].astype(jnp.int32)
    counts = cnt[0, :N_BUCKETS].astype(jnp.int32)
    n_tiles = (counts + TM - 1) // TM
    tile_end = jnp.cumsum(n_tiles)
    tile_start = tile_end - n_tiles
    start_of = jnp.sum(jnp.where(bucket[:, None] == jnp.arange(N_BUCKETS)[None, :], tile_start[None, :], 0), axis=1)
    pos = start_of * TM + rank
    packed = jnp.concatenate([jnp.arange(T, dtype=F32)[:, None], info[:, 2:4]], axis=1)
    base = jnp.zeros((P_ROWS, 3), F32).at[:, 0].set((jnp.arange(P_ROWS) % T).astype(F32))
    packed = base.at[pos].set(packed, unique_indices=True, indices_are_sorted=False)
    idx_sorted = packed[:, 0].astype(jnp.int32)
    gates = packed[:, 1:3]
    total = tile_end[-1]
    j = jnp.minimum(jnp.arange(MOE_TILES), total - 1)
    tb = jnp.sum((j[:, None] >= tile_end[None, :]).astype(jnp.int32), axis=1)
    grp, pair = tb // N_PAIRS, tb % N_PAIRS
    ea = (grp * EXPERTS_PER_GROUP + jnp.asarray(PAIR_A, jnp.int32)[pair]).astype(jnp.int32)
    eb = (grp * EXPERTS_PER_GROUP + jnp.asarray(PAIR_B, jnp.int32)[pair]).astype(jnp.int32)
    first = jnp.arange(MOE_TILES) == 0
    cha = (first | (ea != jnp.roll(ea, 1))).astype(jnp.int32)
    chb = (first | (eb != jnp.roll(eb, 1))).astype(jnp.int32)
    valid = (jnp.arange(MOE_TILES) < total).astype(jnp.int32)

    return _moe_ffn(layer, idx_sorted, ea, eb, cha, chb, valid, rows, gates, nf, w1, w3, w2), pos


def _seg_matrix():
    lane = np.arange(LANES)
    seg = np.where(lane < NOPE_DIM, 0, np.where(lane < NOPE_DIM + ROPE_DIM, 1, 2))
    return jnp.asarray(seg[:, None] == seg[None, :], BF16)


def _seg_count():
    lane = np.arange(LANES)
    return jnp.asarray(np.where(lane < NOPE_DIM, 1.0 / NOPE_DIM, 1.0 / ROPE_DIM), F32).reshape(1, LANES)


def _rope_swap(x):
    lane = lax.broadcasted_iota(jnp.int32, x.shape, 1)
    return jnp.where(lane < ROPE_LO + ROPE_HALF, pltpu.roll(x, LANES - ROPE_HALF, 1), pltpu.roll(x, ROPE_HALF, 1))


def _head_norm(x, seg, cnt, gain):
    ms = _split_dot(x * x, seg) * cnt
    return x * lax.rsqrt(ms + EPS) * gain


def _expand_kv(ckv, kr, wuk_ref, wuv_ref, seg, cnt, kg_ref, k_ref):
    cb = ckv.astype(BF16)
    kn = _dot(cb, wuk_ref[...])
    for hh in range(N_HEADS):
        cols = slice(hh * HEAD_W, (hh + 1) * HEAD_W)
        k_ref[:, cols] = (_head_norm(kn[:, cols], seg, cnt, kg_ref[...]) + kr).astype(BF16)
    return _dot(cb, wuv_ref[...])


def _mla_proj_body(pos_ref, sorted_hbm, cos_ref, sin_ref, kvn_ref, wdkv_ref, kvan_ref, krg_ref, wuk_ref, wuv_ref,
                   kg_ref, nmq_ref, wdq_ref, qan_ref, wuq_ref, qg_ref, seg_ref, cnt_ref,
                   rows_ref, ckv_ref, krope_ref, k_ref, v_ref, vt_ref, qt_ref, qdec_ref, xbuf, xsem):
    is_dec = pl.program_id(0) == N_TILES - 1
    x_ref = _gather_tile(pos_ref, sorted_hbm, xbuf, xsem, TM)
    rows_ref[...] = x_ref[...]
    h = _load_rows(x_ref, TM)
    seg, cnt = seg_ref[...], cnt_ref[...]
    cos, sin = cos_ref[...], sin_ref[...]
    c = _dot(_rms(h, kvn_ref[...]).astype(BF16), wdkv_ref[...])
    ckv = _rms(c[:, :KV_RANK], kvan_ref[...])
    ckv_ref[...] = ckv
    kr = c[:, KV_RANK:]
    kr = kr * lax.rsqrt(jnp.sum(kr * kr, axis=-1, keepdims=True) * (1.0 / ROPE_DIM) + EPS) * krg_ref[...]
    kr = kr * cos + _rope_swap(kr) * sin
    krope_ref[...] = kr[:, ROPE_LO:ROPE_LO + ROPE_DIM]
    v = _expand_kv(ckv, kr, wuk_ref, wuv_ref, seg, cnt, kg_ref, k_ref)
    v_ref[...] = v.astype(BF16)
    vt_ref[0] = v.T.astype(BF16)
    cq = _rms(_dot(_rms(h, nmq_ref[...]).astype(BF16), wdq_ref[...]), qan_ref[...]).astype(BF16)
    q = _dot(cq, wuq_ref[...])
    for hh in range(N_HEADS):
        cols = slice(hh * HEAD_W, (hh + 1) * HEAD_W)
        qh = _head_norm(q[:, cols], seg, cnt, qg_ref[...])
        qh = (qh * cos + _rope_swap(qh) * sin) * SCALE
        qt_ref[0, cols, :] = qh.T.astype(BF16)

        @pl.when(is_dec)
        def _():
            qdec_ref[:, cols] = qh.astype(BF16)


def _mla_proj(pos, sorted_rows, cos_t, sin_t, kvn, wdkv, kvan, krg, wuk, wuv, kg, nmq, wdq, qan, wuq, qg):
    const = lambda *shape: pl.BlockSpec(shape, lambda i, p: (0,) * len(shape))
    tab = pl.BlockSpec((TM, LANES), lambda i, p: (jnp.where(i < N_PROMPT // TM, i % (SEQ // TM), SEQ // TM), 0))
    row = lambda w: pl.BlockSpec((TM, w), lambda i, p: (i, 0))
    return pl.pallas_call(
        _mla_proj_body,
        grid_spec=pltpu.PrefetchScalarGridSpec(
            num_scalar_prefetch=1,
            grid=(N_TILES,),
            in_specs=[
                pl.BlockSpec(memory_space=pl.ANY), tab, tab,
                const(1, D_MODEL), const(D_MODEL, KV_RANK + LANES), const(1, KV_RANK), const(1, LANES),
                const(KV_RANK, N_HEADS * HEAD_W), const(KV_RANK, N_HEADS * V_DIM), const(1, LANES),
                const(1, D_MODEL), const(D_MODEL, Q_RANK), const(1, Q_RANK), const(Q_RANK, N_HEADS * HEAD_W),
                const(1, LANES), const(LANES, LANES), const(1, LANES),
            ],
            out_specs=[pl.BlockSpec((TM * ROW_TILES, LANES), lambda i, p: (i, 0)),
                       row(KV_RANK), row(ROPE_DIM), row(N_HEADS * HEAD_W), row(N_HEADS * V_DIM),
                       pl.BlockSpec((1, N_HEADS * V_DIM, TM), lambda i, p: (i, 0, 0)),
                       pl.BlockSpec((1, N_HEADS * HEAD_W, TM), lambda i, p: (i, 0, 0)),
                       const(N_DEC, N_HEADS * HEAD_W)],
            scratch_shapes=_gather_scratch(TM),
        ),
        out_shape=[
            jax.ShapeDtypeStruct((T * ROW_TILES, LANES), F32),
            jax.ShapeDtypeStruct((T, KV_RANK), F32), jax.ShapeDtypeStruct((T, ROPE_DIM), F32),
            jax.ShapeDtypeStruct((T, N_HEADS * HEAD_W), BF16), jax.ShapeDtypeStruct((T, N_HEADS * V_DIM), BF16),
            jax.ShapeDtypeStruct((N_TILES, N_HEADS * V_DIM, TM), BF16),
            jax.ShapeDtypeStruct((N_TILES, N_HEADS * HEAD_W, TM), BF16),
            jax.ShapeDtypeStruct((N_DEC, N_HEADS * HEAD_W), BF16),
        ],
        compiler_params=_cparams(),
        name="mla_proj",
    )(pos, sorted_rows, cos_t, sin_t, kvn, wdkv, kvan, krg, wuk, wuv, kg, nmq, wdq, qan, wuq, qg,
      _seg_matrix(), _seg_count())


def _cache_kv_body(ckv_ref, kr_ref, place_ref, wuk_ref, wuv_ref, kg_ref, seg_ref, cnt_ref, k_ref, v_ref):
    kr = _dot(kr_ref[...].astype(BF16), place_ref[...])
    v_ref[...] = _expand_kv(ckv_ref[...], kr, wuk_ref, wuv_ref, seg_ref[...], cnt_ref[...], kg_ref, k_ref).astype(BF16)


def _cache_kv(ckv, kr, wuk, wuv, kg):
    n = ckv.shape[0]
    place = jnp.asarray(np.arange(ROPE_DIM)[:, None] + ROPE_LO == np.arange(LANES)[None, :], BF16)
    const = lambda *shape: pl.BlockSpec(shape, lambda i: (0,) * len(shape))
    row = lambda w: pl.BlockSpec((TM, w), lambda i: (i, 0))
    return pl.pallas_call(
        _cache_kv_body,
        grid=(n // TM,),
        in_specs=[row(KV_RANK), row(ROPE_DIM), const(ROPE_DIM, LANES), const(KV_RANK, N_HEADS * HEAD_W),
                  const(KV_RANK, N_HEADS * V_DIM), const(1, LANES), const(LANES, LANES), const(1, LANES)],
        out_specs=[row(N_HEADS * HEAD_W), row(N_HEADS * V_DIM)],
        out_shape=[jax.ShapeDtypeStruct((n, N_HEADS * HEAD_W), BF16), jax.ShapeDtypeStruct((n, N_HEADS * V_DIM), BF16)],
        compiler_params=_cparams(),
        name="cache_kv",
    )(ckv, kr, place, wuk, wuv, kg, _seg_matrix(), _seg_count())


TQ = 256
TK = 256
assert TQ == TK and TQ % CHUNK == 0
SCORE_LOOKAHEAD = 4


def _qk(q, k):
    return lax.dot_general(q, k, (((1,), (1,)), ((), ())), preferred_element_type=F32)


def _merge_heads(o_ref, outs, rows):
    lane = lax.broadcasted_iota(jnp.int32, (rows, LANES), 1)
    for pr in range(N_HEADS // 2):
        o_ref[:, pr * LANES:(pr + 1) * LANES] = jnp.where(lane < V_DIM, outs[2 * pr], outs[2 * pr + 1]).astype(BF16)


def _prompt_attn_body(qt_ref, k_ref, vt_ref, rows_ref, wo_ref, out_ref, m_scr, l_scr, acc_scr):
    qi = pl.program_id(1)
    m_scr[...] = jnp.full(m_scr.shape, NEG, F32)
    l_scr[...] = jnp.zeros(l_scr.shape, F32)
    acc_scr[...] = jnp.zeros(acc_scr.shape, F32)

    def all_heads(j, mask):
        ks = pl.ds(pl.multiple_of(j * TK, TK), TK)

        def scores(hh):
            hcols = slice(hh * HEAD_W, (hh + 1) * HEAD_W)
            return _dot(k_ref[ks, hcols], qt_ref[0, hcols, :])

        ahead = [scores(hh) for hh in range(SCORE_LOOKAHEAD)]
        for hh in range(N_HEADS):
            vrows = slice((hh // 2) * LANES, (hh // 2 + 1) * LANES)
            s = ahead.pop(0)
            if hh + SCORE_LOOKAHEAD < N_HEADS:
                ahead.append(scores(hh + SCORE_LOOKAHEAD))
            if mask is not None:
                s = jnp.where(mask, s, NEG)
            m_old = m_scr[hh]
            m_new = jnp.maximum(m_old, jnp.max(s, axis=0, keepdims=True))
            alpha = jnp.exp(m_old - m_new)
            p = jnp.exp(s - m_new)
            m_scr[hh] = m_new
            l_scr[hh] = alpha * l_scr[hh] + jnp.sum(p, axis=0, keepdims=True)
            acc_scr[hh] = alpha * acc_scr[hh] + _dot(vt_ref[j, vrows, :], p.astype(BF16))

    def step(j, carry):
        all_heads(j, None)
        return carry

    lax.fori_loop(0, qi, step, 0)
    kc = lax.broadcasted_iota(jnp.int32, (TK, TQ), 0) // CHUNK
    qc = lax.broadcasted_iota(jnp.int32, (TK, TQ), 1) // CHUNK
    all_heads(qi, kc <= qc)
    row = lax.broadcasted_iota(jnp.int32, (LANES, TQ), 0)
    pairs = []
    for pr in range(N_HEADS // 2):
        even = acc_scr[2 * pr] / l_scr[2 * pr]
        odd = acc_scr[2 * pr + 1] / l_scr[2 * pr + 1]
        pairs.append(jnp.where(row < V_DIM, even, odd))
    o = jnp.concatenate(pairs, axis=0).T.astype(BF16)
    out_ref[...] = _load_rows(rows_ref, TQ) + _dot(o, wo_ref[...])


def _prompt_attn(qt, k, vt, rows, wo):
    nq = SEQ // TQ
    return pl.pallas_call(
        _prompt_attn_body,
        grid=(BATCH, nq),
        in_specs=[
            pl.BlockSpec((1, N_HEADS * HEAD_W, TQ), lambda b, i: (b * nq + i, 0, 0)),
            pl.BlockSpec((SEQ, N_HEADS * HEAD_W), lambda b, i: (b, 0)),
            pl.BlockSpec((SEQ // TK, N_HEADS * V_DIM, TK), lambda b, i: (b, 0, 0)),
            pl.BlockSpec((TQ * ROW_TILES, LANES), lambda b, i: (b * nq + i, 0)),
            pl.BlockSpec((N_HEADS * V_DIM, D_MODEL), lambda b, i: (0, 0)),
        ],
        out_specs=pl.BlockSpec((TQ, D_MODEL), lambda b, i: (b * nq + i, 0)),
        out_shape=jax.ShapeDtypeStruct((N_PROMPT, D_MODEL), F32),
        scratch_shapes=[pltpu.VMEM((N_HEADS, 1, TQ), F32), pltpu.VMEM((N_HEADS, 1, TQ), F32),
                        pltpu.VMEM((N_HEADS, LANES, TQ), F32)],
        compiler_params=_cparams(2),
        name="prompt_attn",
    )(qt, k, vt, rows, wo)


def _sample_attn_body(q_ref, kc_ref, vc_ref, kn_ref, vn_ref, rows_ref, wo_ref, out_ref, o_scr):
    outs = []
    for hh in range(N_HEADS):
        hcols = slice(hh * HEAD_W, (hh + 1) * HEAD_W)
        vcols = slice((hh // 2) * LANES, (hh // 2 + 1) * LANES)
        q = q_ref[:, hcols]
        sc = _qk(q, kc_ref[:, hcols])
        sn = _qk(q, kn_ref[:, hcols])
        m = jnp.maximum(jnp.max(sc, axis=-1, keepdims=True), jnp.max(sn, axis=-1, keepdims=True))
        pc = jnp.exp(sc - m)
        pn = jnp.exp(sn - m)
        l = jnp.sum(pc, axis=-1, keepdims=True) + jnp.sum(pn, axis=-1, keepdims=True)
        acc = _dot(pc.astype(BF16), vc_ref[:, vcols]) + _dot(pn.astype(BF16), vn_ref[:, vcols])
        outs.append(acc / l)
    _merge_heads(o_scr, outs, DEC_SEQ)
    out_ref[...] = _load_rows(rows_ref, DEC_SEQ) + _dot(o_scr[...], wo_ref[...])


def _sample_attn(q, kc, vc, kn, vn, rows, wo):
    off = N_PROMPT // DEC_SEQ
    return pl.pallas_call(
        _sample_attn_body,
        grid=(DEC_BATCH,),
        in_specs=[
            pl.BlockSpec((DEC_SEQ, N_HEADS * HEAD_W), lambda b: (b, 0)),
            pl.BlockSpec((PAST_LEN, N_HEADS * HEAD_W), lambda b: (b, 0)),
            pl.BlockSpec((PAST_LEN, N_HEADS * V_DIM), lambda b: (b, 0)),
            pl.BlockSpec((DEC_SEQ, N_HEADS * HEAD_W), lambda b: (off + b, 0)),
            pl.BlockSpec((DEC_SEQ, N_HEADS * V_DIM), lambda b: (off + b, 0)),
            pl.BlockSpec((DEC_SEQ * ROW_TILES, LANES), lambda b: (off + b, 0)),
            pl.BlockSpec((N_HEADS * V_DIM, D_MODEL), lambda b: (0, 0)),
        ],
        out_specs=pl.BlockSpec((DEC_SEQ, D_MODEL), lambda b: (b, 0)),
        out_shape=jax.ShapeDtypeStruct((N_DEC, D_MODEL), F32),
        scratch_shapes=[pltpu.VMEM((DEC_SEQ, N_HEADS * V_DIM), BF16)],
        compiler_params=_cparams(),
        name="sample_attn",
    )(q, kc, vc, kn, vn, rows, wo)


def _finish_body(pos_ref, sorted_hbm, yp_ref, ys_ref, xbuf, xsem):
    i = pl.program_id(0)
    y = _load_rows(_gather_tile(pos_ref, sorted_hbm, xbuf, xsem, TM), TM)

    @pl.when(i < N_TILES - 1)
    def _():
        yp_ref[...] = y

    @pl.when(i == N_TILES - 1)
    def _():
        ys_ref[...] = y


def _finish(pos, sorted_rows):
    return pl.pallas_call(
        _finish_body,
        grid_spec=pltpu.PrefetchScalarGridSpec(
            num_scalar_prefetch=1,
            grid=(N_TILES,),
            in_specs=[pl.BlockSpec(memory_space=pl.ANY)],
            out_specs=[pl.BlockSpec((TM, D_MODEL), lambda i, p: (jnp.minimum(i, N_TILES - 2), 0)),
                       pl.BlockSpec((N_DEC, D_MODEL), lambda i, p: (0, 0))],
            scratch_shapes=_gather_scratch(TM),
        ),
        out_shape=[jax.ShapeDtypeStruct((N_PROMPT, D_MODEL), F32), jax.ShapeDtypeStruct((N_DEC, D_MODEL), F32)],
        compiler_params=_cparams(),
        name="finish",
    )(pos, sorted_rows)


def _rope_tables():
    half = ROPE_DIM // 2
    inv_freq = ROPE_THETA ** (-jnp.arange(half, dtype=F32) / half)
    dec_pos = PAST_LEN + jnp.tile(jnp.arange(DEC_SEQ, dtype=jnp.int32), DEC_BATCH)
    pos = jnp.concatenate([jnp.arange(SEQ, dtype=jnp.int32), dec_pos])
    ang = pos.astype(F32)[:, None] * inv_freq[None, :]
    cos, sin = jnp.cos(ang), jnp.sin(ang)
    n = pos.shape[0]
    cos_t = jnp.ones((n, LANES), F32).at[:, ROPE_LO:ROPE_LO + ROPE_DIM].set(jnp.concatenate([cos, cos], axis=1))
    sin_t = jnp.zeros((n, LANES), F32).at[:, ROPE_LO:ROPE_LO + ROPE_DIM].set(jnp.concatenate([-sin, sin], axis=1))
    return cos_t, sin_t


def _on_lanes(vec, lo):
    return jnp.zeros((1, LANES), F32).at[0, lo:lo + vec.shape[0]].set(vec)


def kernel(x_prompt, x_sample, cache_ckv, cache_krope, norm_mix, norm_ffn, gm_w_in, gm_b_in, gm_ln_g, gm_ln_b, gm_w_s, gm_b_s, gm_w_out, gm_b_out, kv_norm, w_dkv, kv_a_norm, k_rope_norm, w_uk, w_uv, k_nope_norm, w_dq, q_a_norm, w_uq, q_nope_norm, q_rope_norm, w_o, moe_w_group, moe_b_group, moe_w_expert, moe_b_expert, moe_w1, moe_w3, moe_w2):
    x = jnp.concatenate([x_prompt.reshape(N_PROMPT, D_MODEL), x_sample.reshape(N_DEC, D_MODEL)], axis=0)

    idx = np.arange(GMLP_BLOCK)
    allowed = (idx[None, :] // CHUNK) <= (idx[:, None] // CHUNK)
    ws_p = jnp.where(allowed[None], gm_w_s[0], 0.0).astype(BF16)
    same_seq = (idx[None, :] // DEC_SEQ) == (idx[:, None] // DEC_SEQ)
    ws_d = jnp.where(same_seq[None], jnp.tile(gm_w_s[0][:, :DEC_SEQ, :DEC_SEQ], (1, GMLP_BLOCK // DEC_SEQ, GMLP_BLOCK // DEC_SEQ)), 0.0).astype(BF16)
    bs_p = gm_b_s[0][:, :, None]
    bs_d = jnp.tile(gm_b_s[0][:, :DEC_SEQ], (1, GMLP_BLOCK // DEC_SEQ))[:, :, None]
    h, v_rows = _gmlp_layer(
        x, norm_mix[0].reshape(1, -1), gm_w_in[0].astype(BF16), gm_b_in[0].reshape(1, -1),
        gm_ln_g[0].reshape(1, -1), gm_ln_b[0].reshape(1, -1), ws_p, ws_d, bs_p, bs_d,
        gm_w_out[0].astype(BF16), gm_b_out[0].reshape(1, -1))
    sorted_rows, pos = _moe_layer(0, h, h[N_PROMPT:], norm_ffn, moe_w_group, moe_b_group, moe_w_expert, moe_b_expert, moe_w1, moe_w3, moe_w2)

    cos_t, sin_t = _rope_tables()
    wdkv = jnp.zeros((D_MODEL, KV_RANK + LANES), F32).at[:, :KV_RANK].set(w_dkv[:, :KV_RANK])
    wdkv = wdkv.at[:, KV_RANK + ROPE_LO:KV_RANK + ROPE_LO + ROPE_DIM].set(w_dkv[:, KV_RANK:]).astype(BF16)
    wuk = jnp.zeros((KV_RANK, N_HEADS, HEAD_W), F32).at[:, :, :NOPE_DIM].set(w_uk).reshape(KV_RANK, -1).astype(BF16)
    wuv = w_uv.reshape(KV_RANK, -1).astype(BF16)
    wuq = jnp.zeros((Q_RANK, N_HEADS, HEAD_W), F32).at[:, :, :NOPE_DIM + ROPE_DIM].set(w_uq[0]).reshape(Q_RANK, -1).astype(BF16)
    kg = _on_lanes(k_nope_norm, 0)
    krg = _on_lanes(k_rope_norm, ROPE_LO)
    qg = _on_lanes(jnp.concatenate([q_nope_norm[0], q_rope_norm[0]]), 0)
    rows, ckv, krope, k_new, v_new, vt_new, qt, q_dec = _mla_proj(
        pos, sorted_rows, cos_t, sin_t, kv_norm.reshape(1, -1), wdkv, kv_a_norm.reshape(1, -1), krg, wuk, wuv, kg,
        norm_mix[1].reshape(1, -1), w_dq[0].astype(BF16), q_a_norm[0].reshape(1, -1), wuq, qg)
    k_cache, v_cache = _cache_kv(cache_ckv.reshape(-1, KV_RANK), cache_krope.reshape(-1, ROPE_DIM), wuk, wuv, kg)

    wo = w_o[0].astype(BF16)
    h_prompt = _prompt_attn(qt, k_new, vt_new, rows, wo)
    h_dec = _sample_attn(q_dec, k_cache, v_cache, k_new, v_new, rows, wo)
    sorted_rows, pos = _moe_layer(1, h_prompt, h_dec, norm_ffn, moe_w_group, moe_b_group, moe_w_expert, moe_b_expert, moe_w1, moe_w3, moe_w2)
    y_prompt, y_sample = _finish(pos, sorted_rows)

    return (y_prompt.reshape(BATCH, SEQ, D_MODEL), y_sample.reshape(DEC_BATCH, DEC_SEQ, D_MODEL),
            ckv[:N_PROMPT].reshape(BATCH, SEQ, KV_RANK), krope[:N_PROMPT].reshape(BATCH, SEQ, ROPE_DIM),
            ckv[N_PROMPT:].reshape(DEC_BATCH, DEC_SEQ, KV_RANK), krope[N_PROMPT:].reshape(DEC_BATCH, DEC_SEQ, ROPE_DIM),
            v_rows.reshape(1, DEC_BATCH, DEC_SEQ, D_GATE))
```

```python
import functools

import jax
import jax.numpy as jnp
import numpy as np
from jax import lax
from jax.experimental import pallas as pl
from jax.experimental.pallas import tpu as pltpu

F32 = jnp.float32
BF16 = jnp.bfloat16

D_MODEL = 1024
BATCH = 8
SEQ = 2048
DEC_BATCH = 16
DEC_SEQ = 16
PAST_LEN = 2048
CHUNK = 64
GMLP_BLOCK = 128
D_GATE = 2 * D_MODEL
N_SG = 8
SG_W = D_GATE // N_SG
N_HEADS = 8
NOPE_DIM = 64
ROPE_DIM = 32
V_DIM = 64
Q_RANK = 384
KV_RANK = 256
ROPE_THETA = 10000.0
SCALE = (NOPE_DIM + ROPE_DIM) ** -0.5
N_EGROUPS = 4
EXPERTS_PER_GROUP = 4
N_EXPERTS = N_EGROUPS * EXPERTS_PER_GROUP
D_EXPERT = 512
EPS = 1e-6
NEG = -1e30

LANES = 128
SUBLANES = 8
ROW_TILES = D_MODEL // LANES
assert ROW_TILES == SUBLANES

N_PROMPT = BATCH * SEQ
N_DEC = DEC_BATCH * DEC_SEQ
T = N_PROMPT + N_DEC
TM = 256
assert N_PROMPT % TM == 0 and N_DEC == TM
N_TILES = T // TM
HEAD_W = LANES
ROPE_LO = NOPE_DIM
ROPE_HALF = ROPE_DIM // 2

PAIR_A = (0, 0, 0, 1, 1, 3)
PAIR_B = (1, 2, 3, 3, 2, 2)
N_PAIRS = 6
N_BUCKETS = N_EGROUPS * N_PAIRS
MOE_TILES = (T + N_BUCKETS * (TM - 1) + TM - 1) // TM
P_ROWS = MOE_TILES * TM

VMEM_LIMIT = 56 * 1024 * 1024


def _cparams(n_axes=1, vmem=VMEM_LIMIT):
    return pltpu.CompilerParams(dimension_semantics=("arbitrary",) * n_axes, vmem_limit_bytes=vmem)


def _rms(x, g):
    return x * lax.rsqrt(jnp.mean(x * x, axis=-1, keepdims=True) + EPS) * g


def _load_rows(ref, n):
    return jnp.concatenate([ref[pl.ds(s, n, stride=ROW_TILES), :] for s in range(ROW_TILES)], axis=1)


def _store_rows(ref, x, n):
    for s in range(ROW_TILES):
        ref[pl.ds(s, n, stride=ROW_TILES), :] = x[:, s * LANES:(s + 1) * LANES]


def _dot(a, b):
    return jnp.dot(a, b, preferred_element_type=F32)


def _split_dot(x, m):
    hi = x.astype(BF16)
    lo = (x - hi.astype(F32)).astype(BF16)
    return _dot(hi, m) + _dot(lo, m)


def _gmlp_body(x_ref, nm_ref, win_ref, bin_ref, lng_ref, lnb_ref, wsp_ref, wsd_ref, bsp_ref, bsd_ref,
               wout_ref, bout_ref, h_ref, v_ref, gated_ref):
    i = pl.program_id(0)
    is_dec = i == N_TILES - 1
    x = x_ref[...]
    xn = _rms(x, nm_ref[...]).astype(BF16)
    z = _dot(xn, win_ref[...]) + bin_ref[...]
    z = z * (0.5 * (1.0 + jnp.tanh(np.float32(np.sqrt(2.0 / np.pi)) * (z + 0.044715 * (z * z * z)))))
    u = z[:, :D_GATE]
    v = z[:, D_GATE:]
    mu = jnp.mean(v, axis=-1, keepdims=True)
    vc = v - mu
    var = jnp.mean(vc * vc, axis=-1, keepdims=True)
    v = vc * lax.rsqrt(var + EPS) * lng_ref[...] + lnb_ref[...]

    v_ref[...] = v
    vb = v.astype(BF16)
    for g in range(N_SG):
        ws = jnp.where(is_dec, wsd_ref[g], wsp_ref[g])
        bs = jnp.where(is_dec, bsd_ref[g], bsp_ref[g])
        for b in range(TM // GMLP_BLOCK):
            rows = slice(b * GMLP_BLOCK, (b + 1) * GMLP_BLOCK)
            cols = slice(g * SG_W, (g + 1) * SG_W)
            s = _dot(ws, vb[rows, cols]) + bs
            gated_ref[rows, cols] = (u[rows, cols] * s).astype(BF16)
    h_ref[...] = x + _dot(gated_ref[...], wout_ref[...]) + bout_ref[...]


def _gmlp_layer(x, nm, w_in, b_in, ln_g, ln_b, ws_p, ws_d, bs_p, bs_d, w_out, b_out):
    const = lambda *shape: pl.BlockSpec(shape, lambda i: (0,) * len(shape))
    return pl.pallas_call(
        _gmlp_body,
        grid=(N_TILES,),
        in_specs=[
            pl.BlockSpec((TM, D_MODEL), lambda i: (i, 0)),
            const(1, D_MODEL), const(D_MODEL, 2 * D_GATE), const(1, 2 * D_GATE),
            const(1, D_GATE), const(1, D_GATE),
            const(N_SG, GMLP_BLOCK, GMLP_BLOCK), const(N_SG, GMLP_BLOCK, GMLP_BLOCK),
            const(N_SG, GMLP_BLOCK, 1), const(N_SG, GMLP_BLOCK, 1),
            const(D_GATE, D_MODEL), const(1, D_MODEL),
        ],
        out_specs=[
            pl.BlockSpec((TM, D_MODEL), lambda i: (i, 0)),
            const(N_DEC, D_GATE),
        ],
        out_shape=[jax.ShapeDtypeStruct((T, D_MODEL), F32), jax.ShapeDtypeStruct((N_DEC, D_GATE), F32)],
        scratch_shapes=[pltpu.VMEM((TM, D_GATE), BF16)],
        compiler_params=_cparams(),
        name="gmlp_layer",
    )(x, nm, w_in, b_in, ln_g, ln_b, ws_p, ws_d, bs_p, bs_d, w_out, b_out)


def _router_body(hp_ref, hd_ref, nf_ref, wrh_ref, wrl_ref, br_ref, rows_ref, info_ref, cnt_ref, carry_ref):
    i = pl.program_id(0)

    @pl.when(i == 0)
    def _():
        carry_ref[...] = jnp.zeros_like(carry_ref)

    h = jnp.where(i == N_TILES - 1, hd_ref[...], hp_ref[...])
    _store_rows(rows_ref, h, TM)
    xn = _rms(h, nf_ref[...])
    xh = xn.astype(BF16)
    xl = (xn - xh.astype(F32)).astype(BF16)
    logits = _dot(xh, wrh_ref[...]) + (_dot(xl, wrh_ref[...]) + _dot(xh, wrl_ref[...])) + br_ref[...]
    lane = lax.broadcasted_iota(jnp.int32, (TM, LANES), 1).astype(F32)

    def first_max(vals):
        vmax = jnp.max(vals, axis=-1, keepdims=True)
        idx = jnp.min(jnp.where(vals == vmax, lane, float(LANES)), axis=-1, keepdims=True)
        return vmax, idx

    lg = jnp.where(lane < N_EGROUPS, logits, -jnp.inf)
    gmax, g_idx = first_max(lg)
    g_p = 1.0 / jnp.sum(jnp.exp(lg - gmax), axis=-1, keepdims=True)
    e_lo = N_EGROUPS + EXPERTS_PER_GROUP * g_idx
    le = jnp.where((lane >= e_lo) & (lane < e_lo + EXPERTS_PER_GROUP), logits, -jnp.inf)
    v1, i1 = first_max(le)
    v2, i2 = first_max(jnp.where(lane == i1, -jnp.inf, le))
    e2 = jnp.exp(v2 - v1)
    w1 = (1.0 / (1.0 + e2)) * g_p
    w2 = (e2 / (1.0 + e2)) * g_p
    a1 = i1 - e_lo
    a2 = i2 - e_lo
    lo = jnp.minimum(a1, a2)
    hi = jnp.maximum(a1, a2)
    pair = jnp.where(lo == 0.0, hi - 1.0, jnp.where(lo == 1.0, jnp.where(hi == 3.0, 3.0, 4.0), 5.0))
    ea = jnp.where(pair < 3.0, 0.0, jnp.where(pair < 5.0, 1.0, 3.0))
    ga = jnp.where(a1 == ea, w1, w2)
    gb = jnp.where(a1 == ea, w2, w1)
    bucket = g_idx * N_PAIRS + pair

    onehot = (lane == bucket).astype(F32)
    r = lax.broadcasted_iota(jnp.int32, (TM, TM), 0)
    c = lax.broadcasted_iota(jnp.int32, (TM, TM), 1)
    before = _dot((c < r).astype(BF16), onehot.astype(BF16))
    carry = carry_ref[0:1, :]
    rank = jnp.sum(onehot * (before + carry), axis=-1, keepdims=True)
    new_carry = carry + jnp.sum(onehot, axis=0, keepdims=True)
    carry_ref[...] = jnp.broadcast_to(new_carry, carry_ref.shape)
    cnt_ref[...] = jnp.broadcast_to(new_carry, cnt_ref.shape)
    info_ref[...] = jnp.where(lane == 0.0, bucket,
                              jnp.where(lane == 1.0, rank, jnp.where(lane == 2.0, ga, jnp.where(lane == 3.0, gb, 0.0))))


def _router(h_prompt, h_dec, nf, wr, br):
    const = lambda *shape: pl.BlockSpec(shape, lambda i: (0,) * len(shape))
    wr_hi = wr.astype(BF16)
    return pl.pallas_call(
        _router_body,
        grid=(N_TILES,),
        in_specs=[pl.BlockSpec((TM, D_MODEL), lambda i: (jnp.minimum(i, N_TILES - 2), 0)), const(N_DEC, D_MODEL),
                  const(1, D_MODEL), const(D_MODEL, LANES), const(D_MODEL, LANES), const(1, LANES)],
        out_specs=[pl.BlockSpec((TM * ROW_TILES, LANES), lambda i: (i, 0)),
                   pl.BlockSpec((TM, LANES), lambda i: (i, 0)),
                   const(SUBLANES, LANES)],
        out_shape=[jax.ShapeDtypeStruct((T * ROW_TILES, LANES), F32), jax.ShapeDtypeStruct((T, LANES), F32),
                   jax.ShapeDtypeStruct((SUBLANES, LANES), F32)],
        scratch_shapes=[pltpu.VMEM((SUBLANES, LANES), F32)],
        compiler_params=_cparams(),
        name="moe_router",
    )(h_prompt, h_dec, nf, wr_hi, (wr - wr_hi.astype(F32)).astype(BF16), br)


GATHER_UNROLL = 8


def _gather_rows_start(idx_ref, base, src_ref, dst_ref, sem, n):
    def group(g, carry):
        for u in range(GATHER_UNROLL):
            r = g * GATHER_UNROLL + u
            src = pl.multiple_of(idx_ref[base + r] * ROW_TILES, ROW_TILES)
            dst = pl.multiple_of(r * ROW_TILES, ROW_TILES)
            pltpu.make_async_copy(src_ref.at[pl.ds(src, ROW_TILES), :], dst_ref.at[pl.ds(dst, ROW_TILES), :], sem).start()
        return carry

    lax.fori_loop(0, n // GATHER_UNROLL, group, 0)


def _gather_rows_wait(src_ref, dst_ref, sem, n):
    pltpu.make_async_copy(src_ref.at[pl.ds(0, n * ROW_TILES), :], dst_ref.at[pl.ds(0, n * ROW_TILES), :], sem).wait()


def _gather_tile(idx_ref, src_ref, buf, sem, n):
    i = pl.program_id(0)
    slot = lax.rem(i, 2)

    @pl.when(i == 0)
    def _():
        _gather_rows_start(idx_ref, 0, src_ref, buf.at[0], sem.at[0], n)

    @pl.when(i + 1 < pl.num_programs(0))
    def _():
        _gather_rows_start(idx_ref, (i + 1) * n, src_ref, buf.at[1 - slot], sem.at[1 - slot], n)

    _gather_rows_wait(src_ref, buf.at[slot], sem.at[slot], n)
    return buf.at[slot]


def _gather_scratch(n):
    return [pltpu.VMEM((2, n * ROW_TILES, LANES), F32), pltpu.SemaphoreType.DMA((2,))]


def _moe_body(idx_ref, ea_ref, eb_ref, cha_ref, chb_ref, valid_ref, rows_hbm, g_ref, nf_ref,
              w1a_ref, w3a_ref, w2a_ref, w1b_ref, w3b_ref, w2b_ref, out_ref,
              s1a, s3a, s2a, s1b, s3b, s2b, xbuf, xsem):
    i = pl.program_id(0)
    x_ref = _gather_tile(idx_ref, rows_hbm, xbuf, xsem, TM)

    @pl.when(cha_ref[i] == 1)
    def _():
        s1a[...] = w1a_ref[...].astype(BF16)
        s3a[...] = w3a_ref[...].astype(BF16)
        s2a[...] = w2a_ref[...].astype(BF16)

    @pl.when(chb_ref[i] == 1)
    def _():
        s1b[...] = w1b_ref[...].astype(BF16)
        s3b[...] = w3b_ref[...].astype(BF16)
        s2b[...] = w2b_ref[...].astype(BF16)

    @pl.when(valid_ref[i] == 1)
    def _():
        h = _load_rows(x_ref, TM)
        xn = _rms(h, nf_ref[...]).astype(BF16)
        g = g_ref[...]

        def ffn(w1, w3, w2):
            a = _dot(xn, w1[...])
            hdn = (a * (1.0 / (1.0 + jnp.exp(-a)))) * _dot(xn, w3[...])
            return _dot(hdn.astype(BF16), w2[...])

        y = g[:, 0:1] * ffn(s1a, s3a, s2a) + g[:, 1:2] * ffn(s1b, s3b, s2b)
        _store_rows(out_ref, h + y, TM)

    @pl.when(valid_ref[i] == 0)
    def _():
        out_ref[...] = jnp.zeros_like(out_ref)


def _moe_ffn(layer, idx_sorted, ea, eb, cha, chb, valid, rows, gates, nf, w1, w3, w2):
    wa = lambda shape: pl.BlockSpec((None, None) + shape, lambda i, ix, ea, eb, ca, cb, va: (layer, ea[i], 0, 0))
    wb = lambda shape: pl.BlockSpec((None, None) + shape, lambda i, ix, ea, eb, ca, cb, va: (layer, eb[i], 0, 0))
    up, down = (D_MODEL, D_EXPERT), (D_EXPERT, D_MODEL)
    return pl.pallas_call(
        _moe_body,
        grid_spec=pltpu.PrefetchScalarGridSpec(
            num_scalar_prefetch=6,
            grid=(MOE_TILES,),
            in_specs=[
                pl.BlockSpec(memory_space=pl.ANY),
                pl.BlockSpec((TM, 2), lambda i, *_: (i, 0)),
                pl.BlockSpec((1, D_MODEL), lambda i, *_: (0, 0)),
                wa(up), wa(up), wa(down), wb(up), wb(up), wb(down),
            ],
            out_specs=pl.BlockSpec((TM * ROW_TILES, LANES), lambda i, *_: (i, 0)),
            scratch_shapes=[pltpu.VMEM(up, BF16), pltpu.VMEM(up, BF16), pltpu.VMEM(down, BF16),
                            pltpu.VMEM(up, BF16), pltpu.VMEM(up, BF16), pltpu.VMEM(down, BF16)]
            + _gather_scratch(TM),
        ),
        out_shape=jax.ShapeDtypeStruct((P_ROWS * ROW_TILES, LANES), F32),
        compiler_params=_cparams(),
        name="moe_ffn",
    )(idx_sorted, ea, eb, cha, chb, valid, rows, gates, nf, w1, w3, w2, w1, w3, w2)


def _moe_layer(layer, h_prompt, h_dec, norm_ffn, w_group, b_group, w_expert, b_expert, w1, w3, w2):
    wr = jnp.zeros((D_MODEL, LANES), F32)
    wr = wr.at[:, :N_EGROUPS].set(w_group[layer]).at[:, N_EGROUPS:N_EGROUPS + N_EXPERTS].set(w_expert[layer])
    br = jnp.zeros((1, LANES), F32)
    br = br.at[0, :N_EGROUPS].set(b_group[layer]).at[0, N_EGROUPS:N_EGROUPS + N_EXPERTS].set(b_expert[layer])
    nf = norm_ffn[layer].reshape(1, D_MODEL)
    rows, info, cnt = _router(h_prompt, h_dec, nf, wr, br)

    bucket = info[:, 0].astype(jnp.int32)
    rank = info[:, 1].astype(jnp.int32)
    counts = cnt[0, :N_BUCKETS].astype(jnp.int32)
    n_tiles = (counts + TM - 1) // TM
    tile_end = jnp.cumsum(n_tiles)
    tile_start = tile_end - n_tiles
    start_of = jnp.sum(jnp.where(bucket[:, None] == jnp.arange(N_BUCKETS)[None, :], tile_start[None, :], 0), axis=1)
    pos = start_of * TM + rank
    packed = jnp.concatenate([jnp.arange(T, dtype=F32)[:, None], info[:, 2:4]], axis=1)
    base = jnp.zeros((P_ROWS, 3), F32).at[:, 0].set((jnp.arange(P_ROWS) % T).astype(F32))
    packed = base.at[pos].set(packed, unique_indices=True, indices_are_sorted=False)
    idx_sorted = packed[:, 0].astype(jnp.int32)
    gates = packed[:, 1:3]
    total = tile_end[-1]
    j = jnp.minimum(jnp.arange(MOE_TILES), total - 1)
    tb = jnp.sum((j[:, None] >= tile_end[None, :]).astype(jnp.int32), axis=1)
    grp, pair = tb // N_PAIRS, tb % N_PAIRS
    ea = (grp * EXPERTS_PER_GROUP + jnp.asarray(PAIR_A, jnp.int32)[pair]).astype(jnp.int32)
    eb = (grp * EXPERTS_PER_GROUP + jnp.asarray(PAIR_B, jnp.int32)[pair]).astype(jnp.int32)
    first = jnp.arange(MOE_TILES) == 0
    cha = (first | (ea != jnp.roll(ea, 1))).astype(jnp.int32)
    chb = (first | (eb != jnp.roll(eb, 1))).astype(jnp.int32)
    valid = (jnp.arange(MOE_TILES) < total).astype(jnp.int32)

    return _moe_ffn(layer, idx_sorted, ea, eb, cha, chb, valid, rows, gates, nf, w1, w3, w2), pos


def _seg_matrix():
    lane = np.arange(LANES)
    seg = np.where(lane < NOPE_DIM, 0, np.where(lane < NOPE_DIM + ROPE_DIM, 1, 2))
    return jnp.asarray(seg[:, None] == seg[None, :], BF16)


def _seg_count():
    lane = np.arange(LANES)
    return jnp.asarray(np.where(lane < NOPE_DIM, 1.0 / NOPE_DIM, 1.0 / ROPE_DIM), F32).reshape(1, LANES)


def _rope_swap(x):
    lane = lax.broadcasted_iota(jnp.int32, x.shape, 1)
    return jnp.where(lane < ROPE_LO + ROPE_HALF, pltpu.roll(x, LANES - ROPE_HALF, 1), pltpu.roll(x, ROPE_HALF, 1))


def _expand_k(cb, kr, wuk_ref, seg, cnt, kg_ref, k_ref):
    kn = _dot(cb, wuk_ref[...])
    heads = [kn[:, hh * HEAD_W:(hh + 1) * HEAD_W] for hh in range(N_HEADS)]
    ms = [_split_dot(x * x, seg) * cnt for x in heads]
    for hh in range(N_HEADS):
        k_ref[:, hh * HEAD_W:(hh + 1) * HEAD_W] = (heads[hh] * lax.rsqrt(ms[hh] + EPS) * kg_ref[...] + kr).astype(BF16)


def _mla_proj_body(pos_ref, sorted_hbm, cos_ref, sin_ref, cost_ref, sint_ref, kvn_ref, wdkv_ref, kvan_ref, krg_ref,
                   wuk_ref, wuv_ref, wuvt_ref, kg_ref, nmq_ref, wdq_ref, qan_ref, wuqt_ref, qg_ref, seg_ref, cnt_ref,
                   rows_ref, ckv_ref, krope_ref, k_ref, vdec_ref, vt_ref, qt_ref, qdec_ref, xbuf, xsem):
    is_dec = pl.program_id(0) == N_TILES - 1
    x_ref = _gather_tile(pos_ref, sorted_hbm, xbuf, xsem, TM)
    rows_ref[...] = x_ref[...]
    h = _load_rows(x_ref, TM)
    c = _dot(_rms(h, kvn_ref[...]).astype(BF16), wdkv_ref[...])
    ckv = _rms(c[:, :KV_RANK], kvan_ref[...])
    ckv_ref[...] = ckv
    kr = c[:, KV_RANK:]
    kr = kr * lax.rsqrt(jnp.sum(kr * kr, axis=-1, keepdims=True) * (1.0 / ROPE_DIM) + EPS) * krg_ref[...]
    kr = kr * cos_ref[...] + _rope_swap(kr) * sin_ref[...]
    krope_ref[...] = kr[:, ROPE_LO:ROPE_LO + ROPE_DIM]
    cb = ckv.astype(BF16)
    _expand_k(cb, kr, wuk_ref, seg_ref[...], cnt_ref[...], kg_ref, k_ref)
    vt_ref[0] = _dot(wuvt_ref[...], ckv.T.astype(BF16)).astype(BF16)
    cq = _rms(_dot(_rms(h, nmq_ref[...]).astype(BF16), wdq_ref[...]), qan_ref[...])
    qt = _dot(wuqt_ref[...], cq.T.astype(BF16))
    cost, sint, qg = cost_ref[0], sint_ref[0], qg_ref[...]
    for hh in range(N_HEADS):
        x = qt[hh * HEAD_W:(hh + 1) * HEAD_W, :]
        xn, xr = x[:NOPE_DIM], x[ROPE_LO:ROPE_LO + ROPE_DIM]
        xn = xn * lax.rsqrt(jnp.mean(xn * xn, axis=0, keepdims=True) + EPS) * qg[:NOPE_DIM]
        xr = xr * lax.rsqrt(jnp.mean(xr * xr, axis=0, keepdims=True) + EPS) * qg[ROPE_LO:ROPE_LO + ROPE_DIM]
        x1, x2 = xr[:ROPE_HALF], xr[ROPE_HALF:]
        qh = jnp.concatenate([xn, x1 * cost - x2 * sint, x1 * sint + x2 * cost,
                              jnp.zeros((HEAD_W - NOPE_DIM - ROPE_DIM, TM), F32)], axis=0) * SCALE
        qt_ref[0, hh * HEAD_W:(hh + 1) * HEAD_W, :] = qh.astype(BF16)

    @pl.when(is_dec)
    def _():
        vdec_ref[...] = _dot(cb, wuv_ref[...]).astype(BF16)
        for hh in range(N_HEADS):
            rows = slice(hh * HEAD_W, (hh + 1) * HEAD_W)
            qdec_ref[:, rows] = qt_ref[0, rows, :].astype(F32).T.astype(BF16)


def _mla_proj(pos, sorted_rows, cos_t, sin_t, cos_tt, sin_tt, kvn, wdkv, kvan, krg, wuk, wuv, kg, nmq, wdq, qan, wuq, qg):
    const = lambda *shape: pl.BlockSpec(shape, lambda i, p: (0,) * len(shape))
    tab_tile = lambda i: jnp.where(i < N_PROMPT // TM, i % (SEQ // TM), SEQ // TM)
    tab = pl.BlockSpec((TM, LANES), lambda i, p: (tab_tile(i), 0))
    tab_t = pl.BlockSpec((1, ROPE_HALF, TM), lambda i, p: (tab_tile(i), 0, 0))
    row = lambda w: pl.BlockSpec((TM, w), lambda i, p: (i, 0))
    return pl.pallas_call(
        _mla_proj_body,
        grid_spec=pltpu.PrefetchScalarGridSpec(
            num_scalar_prefetch=1,
            grid=(N_TILES,),
            in_specs=[
                pl.BlockSpec(memory_space=pl.ANY), tab, tab, tab_t, tab_t,
                const(1, D_MODEL), const(D_MODEL, KV_RANK + LANES), const(1, KV_RANK), const(1, LANES),
                const(KV_RANK, N_HEADS * HEAD_W), const(KV_RANK, N_HEADS * V_DIM), const(N_HEADS * V_DIM, KV_RANK),
                const(1, LANES),
                const(1, D_MODEL), const(D_MODEL, Q_RANK), const(1, Q_RANK), const(N_HEADS * HEAD_W, Q_RANK),
                const(HEAD_W, TM), const(LANES, LANES), const(1, LANES),
            ],
            out_specs=[pl.BlockSpec((TM * ROW_TILES, LANES), lambda i, p: (i, 0)),
                       row(KV_RANK), row(ROPE_DIM), row(N_HEADS * HEAD_W), const(N_DEC, N_HEADS * V_DIM),
                       pl.BlockSpec((1, N_HEADS * V_DIM, TM), lambda i, p: (i, 0, 0)),
                       pl.BlockSpec((1, N_HEADS * HEAD_W, TM), lambda i, p: (i, 0, 0)),
                       const(N_DEC, N_HEADS * HEAD_W)],
            scratch_shapes=_gather_scratch(TM),
        ),
        out_shape=[
            jax.ShapeDtypeStruct((T * ROW_TILES, LANES), F32),
            jax.ShapeDtypeStruct((T, KV_RANK), F32), jax.ShapeDtypeStruct((T, ROPE_DIM), F32),
            jax.ShapeDtypeStruct((T, N_HEADS * HEAD_W), BF16), jax.ShapeDtypeStruct((N_DEC, N_HEADS * V_DIM), BF16),
            jax.ShapeDtypeStruct((N_TILES, N_HEADS * V_DIM, TM), BF16),
            jax.ShapeDtypeStruct((N_TILES, N_HEADS * HEAD_W, TM), BF16),
            jax.ShapeDtypeStruct((N_DEC, N_HEADS * HEAD_W), BF16),
        ],
        compiler_params=_cparams(),
        name="mla_proj",
    )(pos, sorted_rows, cos_t, sin_t, cos_tt, sin_tt, kvn, wdkv, kvan, krg, wuk, wuv, wuv.T, kg, nmq, wdq, qan,
      wuq.T, jnp.broadcast_to(qg.reshape(HEAD_W, 1), (HEAD_W, TM)), _seg_matrix(), _seg_count())


def _cache_kv_body(ckv_ref, kr_ref, place_ref, wuk_ref, wuv_ref, kg_ref, seg_ref, cnt_ref, k_ref, v_ref):
    kr = _dot(kr_ref[...].astype(BF16), place_ref[...])
    cb = ckv_ref[...].astype(BF16)
    _expand_k(cb, kr, wuk_ref, seg_ref[...], cnt_ref[...], kg_ref, k_ref)
    v_ref[...] = _dot(cb, wuv_ref[...]).astype(BF16)


CACHE_ROWS = 1024
assert PAST_LEN % CACHE_ROWS == 0


def _cache_kv(ckv, kr, wuk, wuv, kg):
    n = ckv.shape[0]
    place = jnp.asarray(np.arange(ROPE_DIM)[:, None] + ROPE_LO == np.arange(LANES)[None, :], BF16)
    const = lambda *shape: pl.BlockSpec(shape, lambda i: (0,) * len(shape))
    row = lambda w: pl.BlockSpec((CACHE_ROWS, w), lambda i: (i, 0))
    return pl.pallas_call(
        _cache_kv_body,
        grid=(n // CACHE_ROWS,),
        in_specs=[row(KV_RANK), row(ROPE_DIM), const(ROPE_DIM, LANES), const(KV_RANK, N_HEADS * HEAD_W),
                  const(KV_RANK, N_HEADS * V_DIM), const(1, LANES), const(LANES, LANES), const(1, LANES)],
        out_specs=[row(N_HEADS * HEAD_W), row(N_HEADS * V_DIM)],
        out_shape=[jax.ShapeDtypeStruct((n, N_HEADS * HEAD_W), BF16), jax.ShapeDtypeStruct((n, N_HEADS * V_DIM), BF16)],
        compiler_params=_cparams(),
        name="cache_kv",
    )(ckv, kr, place, wuk, wuv, kg, _seg_matrix(), _seg_count())


TQ = 256
TK = 256
assert TQ == TK and TQ % CHUNK == 0
SCORE_LOOKAHEAD = 4


def _qk(q, k):
    return lax.dot_general(q, k, (((1,), (1,)), ((), ())), preferred_element_type=F32)


def _merge_heads(o_ref, outs, rows):
    lane = lax.broadcasted_iota(jnp.int32, (rows, LANES), 1)
    for pr in range(N_HEADS // 2):
        o_ref[:, pr * LANES:(pr + 1) * LANES] = jnp.where(lane < V_DIM, outs[2 * pr], outs[2 * pr + 1]).astype(BF16)


def _prompt_attn_body(qt_ref, k_ref, vt_ref, rows_ref, wo_ref, out_ref, m_scr, l_scr, acc_scr):
    qi = pl.program_id(1)
    m_scr[...] = jnp.full(m_scr.shape, NEG, F32)
    l_scr[...] = jnp.zeros(l_scr.shape, F32)
    acc_scr[...] = jnp.zeros(acc_scr.shape, F32)

    def all_heads(j, mask):
        ks = pl.ds(pl.multiple_of(j * TK, TK), TK)

        def scores(hh):
            hcols = slice(hh * HEAD_W, (hh + 1) * HEAD_W)
            return _dot(k_ref[ks, hcols], qt_ref[0, hcols, :])

        ahead = [scores(hh) for hh in range(SCORE_LOOKAHEAD)]
        for hh in range(N_HEADS):
            vrows = slice((hh // 2) * LANES, (hh // 2 + 1) * LANES)
            s = ahead.pop(0)
            if hh + SCORE_LOOKAHEAD < N_HEADS:
                ahead.append(scores(hh + SCORE_LOOKAHEAD))
            if mask is not None:
                s = jnp.where(mask, s, NEG)
            m_old = m_scr[hh]
            m_new = jnp.maximum(m_old, jnp.max(s, axis=0, keepdims=True))
            alpha = jnp.exp(m_old - m_new)
            p = jnp.exp(s - m_new)
            m_scr[hh] = m_new
            l_scr[hh] = alpha * l_scr[hh] + jnp.sum(p, axis=0, keepdims=True)
            acc_scr[hh] = alpha * acc_scr[hh] + _dot(vt_ref[j, vrows, :], p.astype(BF16))

    def step(j, carry):
        all_heads(j, None)
        return carry

    lax.fori_loop(0, qi, step, 0)
    kc = lax.broadcasted_iota(jnp.int32, (TK, TQ), 0) // CHUNK
    qc = lax.broadcasted_iota(jnp.int32, (TK, TQ), 1) // CHUNK
    all_heads(qi, kc <= qc)
    row = lax.broadcasted_iota(jnp.int32, (LANES, TQ), 0)
    pairs = []
    for pr in range(N_HEADS // 2):
        even = acc_scr[2 * pr] / l_scr[2 * pr]
        odd = acc_scr[2 * pr + 1] / l_scr[2 * pr + 1]
        pairs.append(jnp.where(row < V_DIM, even, odd))
    o = jnp.concatenate(pairs, axis=0).T.astype(BF16)
    out_ref[...] = _load_rows(rows_ref, TQ) + _dot(o, wo_ref[...])


def _prompt_attn(qt, k, vt, rows, wo):
    nq = SEQ // TQ
    return pl.pallas_call(
        _prompt_attn_body,
        grid=(BATCH, nq),
        in_specs=[
            pl.BlockSpec((1, N_HEADS * HEAD_W, TQ), lambda b, i: (b * nq + i, 0, 0)),
            pl.BlockSpec((SEQ, N_HEADS * HEAD_W), lambda b, i: (b, 0)),
            pl.BlockSpec((SEQ // TK, N_HEADS * V_DIM, TK), lambda b, i: (b, 0, 0)),
            pl.BlockSpec((TQ * ROW_TILES, LANES), lambda b, i: (b * nq + i, 0)),
            pl.BlockSpec((N_HEADS * V_DIM, D_MODEL), lambda b, i: (0, 0)),
        ],
        out_specs=pl.BlockSpec((TQ, D_MODEL), lambda b, i: (b * nq + i, 0)),
        out_shape=jax.ShapeDtypeStruct((N_PROMPT, D_MODEL), F32),
        scratch_shapes=[pltpu.VMEM((N_HEADS, 1, TQ), F32), pltpu.VMEM((N_HEADS, 1, TQ), F32),
                        pltpu.VMEM((N_HEADS, LANES, TQ), F32)],
        compiler_params=_cparams(2),
        name="prompt_attn",
    )(qt, k, vt, rows, wo)


def _sample_attn_body(q_ref, kc_ref, vc_ref, kn_ref, vn_ref, rows_ref, wo_ref, out_ref, o_scr):
    outs = []
    for hh in range(N_HEADS):
        hcols = slice(hh * HEAD_W, (hh + 1) * HEAD_W)
        vcols = slice((hh // 2) * LANES, (hh // 2 + 1) * LANES)
        q = q_ref[:, hcols]
        sc = _qk(q, kc_ref[:, hcols])
        sn = _qk(q, kn_ref[:, hcols])
        m = jnp.maximum(jnp.max(sc, axis=-1, keepdims=True), jnp.max(sn, axis=-1, keepdims=True))
        pc = jnp.exp(sc - m)
        pn = jnp.exp(sn - m)
        l = jnp.sum(pc, axis=-1, keepdims=True) + jnp.sum(pn, axis=-1, keepdims=True)
        acc = _dot(pc.astype(BF16), vc_ref[:, vcols]) + _dot(pn.astype(BF16), vn_ref[:, vcols])
        outs.append(acc / l)
    _merge_heads(o_scr, outs, DEC_SEQ)
    out_ref[...] = _load_rows(rows_ref, DEC_SEQ) + _dot(o_scr[...], wo_ref[...])


def _sample_attn(q, kc, vc, kn, vn, rows, wo):
    off = N_PROMPT // DEC_SEQ
    return pl.pallas_call(
        _sample_attn_body,
        grid=(DEC_BATCH,),
        in_specs=[
            pl.BlockSpec((DEC_SEQ, N_HEADS * HEAD_W), lambda b: (b, 0)),
            pl.BlockSpec((PAST_LEN, N_HEADS * HEAD_W), lambda b: (b, 0)),
            pl.BlockSpec((PAST_LEN, N_HEADS * V_DIM), lambda b: (b, 0)),
            pl.BlockSpec((DEC_SEQ, N_HEADS * HEAD_W), lambda b: (off + b, 0)),
            pl.BlockSpec((DEC_SEQ, N_HEADS * V_DIM), lambda b: (b, 0)),
            pl.BlockSpec((DEC_SEQ * ROW_TILES, LANES), lambda b: (off + b, 0)),
            pl.BlockSpec((N_HEADS * V_DIM, D_MODEL), lambda b: (0, 0)),
        ],
        out_specs=pl.BlockSpec((DEC_SEQ, D_MODEL), lambda b: (b, 0)),
        out_shape=jax.ShapeDtypeStruct((N_DEC, D_MODEL), F32),
        scratch_shapes=[pltpu.VMEM((DEC_SEQ, N_HEADS * V_DIM), BF16)],
        compiler_params=_cparams(),
        name="sample_attn",
    )(q, kc, vc, kn, vn, rows, wo)


def _finish_body(pos_ref, sorted_hbm, yp_ref, ys_ref, xbuf, xsem):
    i = pl.program_id(0)
    y = _load_rows(_gather_tile(pos_ref, sorted_hbm, xbuf, xsem, TM), TM)

    @pl.when(i < N_TILES - 1)
    def _():
        yp_ref[...] = y

    @pl.when(i == N_TILES - 1)
    def _():
        ys_ref[...] = y


def _finish(pos, sorted_rows):
    return pl.pallas_call(
        _finish_body,
        grid_spec=pltpu.PrefetchScalarGridSpec(
            num_scalar_prefetch=1,
            grid=(N_TILES,),
            in_specs=[pl.BlockSpec(memory_space=pl.ANY)],
            out_specs=[pl.BlockSpec((TM, D_MODEL), lambda i, p: (jnp.minimum(i, N_TILES - 2), 0)),
                       pl.BlockSpec((N_DEC, D_MODEL), lambda i, p: (0, 0))],
            scratch_shapes=_gather_scratch(TM),
        ),
        out_shape=[jax.ShapeDtypeStruct((N_PROMPT, D_MODEL), F32), jax.ShapeDtypeStruct((N_DEC, D_MODEL), F32)],
        compiler_params=_cparams(),
        name="finish",
    )(pos, sorted_rows)


def _rope_tables():
    half = ROPE_DIM // 2
    inv_freq = ROPE_THETA ** (-jnp.arange(half, dtype=F32) / half)
    dec_pos = PAST_LEN + jnp.tile(jnp.arange(DEC_SEQ, dtype=jnp.int32), DEC_BATCH)
    pos = jnp.concatenate([jnp.arange(SEQ, dtype=jnp.int32), dec_pos])
    ang = pos.astype(F32)[:, None] * inv_freq[None, :]
    cos, sin = jnp.cos(ang), jnp.sin(ang)
    n = pos.shape[0]
    cos_t = jnp.ones((n, LANES), F32).at[:, ROPE_LO:ROPE_LO + ROPE_DIM].set(jnp.concatenate([cos, cos], axis=1))
    sin_t = jnp.zeros((n, LANES), F32).at[:, ROPE_LO:ROPE_LO + ROPE_DIM].set(jnp.concatenate([-sin, sin], axis=1))
    to_tiles = lambda a: a.reshape(n // TM, TM, half).transpose(0, 2, 1)
    return cos_t, sin_t, to_tiles(cos), to_tiles(sin)


def _on_lanes(vec, lo):
    return jnp.zeros((1, LANES), F32).at[0, lo:lo + vec.shape[0]].set(vec)


def kernel(x_prompt, x_sample, cache_ckv, cache_krope, norm_mix, norm_ffn, gm_w_in, gm_b_in, gm_ln_g, gm_ln_b, gm_w_s, gm_b_s, gm_w_out, gm_b_out, kv_norm, w_dkv, kv_a_norm, k_rope_norm, w_uk, w_uv, k_nope_norm, w_dq, q_a_norm, w_uq, q_nope_norm, q_rope_norm, w_o, moe_w_group, moe_b_group, moe_w_expert, moe_b_expert, moe_w1, moe_w3, moe_w2):
    x = jnp.concatenate([x_prompt.reshape(N_PROMPT, D_MODEL), x_sample.reshape(N_DEC, D_MODEL)], axis=0)

    idx = np.arange(GMLP_BLOCK)
    allowed = (idx[None, :] // CHUNK) <= (idx[:, None] // CHUNK)
    ws_p = jnp.where(allowed[None], gm_w_s[0], 0.0).astype(BF16)
    same_seq = (idx[None, :] // DEC_SEQ) == (idx[:, None] // DEC_SEQ)
    ws_d = jnp.where(same_seq[None], jnp.tile(gm_w_s[0][:, :DEC_SEQ, :DEC_SEQ], (1, GMLP_BLOCK // DEC_SEQ, GMLP_BLOCK // DEC_SEQ)), 0.0).astype(BF16)
    bs_p = gm_b_s[0][:, :, None]
    bs_d = jnp.tile(gm_b_s[0][:, :DEC_SEQ], (1, GMLP_BLOCK // DEC_SEQ))[:, :, None]
    h, v_rows = _gmlp_layer(
        x, norm_mix[0].reshape(1, -1), gm_w_in[0].astype(BF16), gm_b_in[0].reshape(1, -1),
        gm_ln_g[0].reshape(1, -1), gm_ln_b[0].reshape(1, -1), ws_p, ws_d, bs_p, bs_d,
        gm_w_out[0].astype(BF16), gm_b_out[0].reshape(1, -1))
    sorted_rows, pos = _moe_layer(0, h, h[N_PROMPT:], norm_ffn, moe_w_group, moe_b_group, moe_w_expert, moe_b_expert, moe_w1, moe_w3, moe_w2)

    cos_t, sin_t, cos_tt, sin_tt = _rope_tables()
    wdkv = jnp.zeros((D_MODEL, KV_RANK + LANES), F32).at[:, :KV_RANK].set(w_dkv[:, :KV_RANK])
    wdkv = wdkv.at[:, KV_RANK + ROPE_LO:KV_RANK + ROPE_LO + ROPE_DIM].set(w_dkv[:, KV_RANK:]).astype(BF16)
    wuk = jnp.zeros((KV_RANK, N_HEADS, HEAD_W), F32).at[:, :, :NOPE_DIM].set(w_uk).reshape(KV_RANK, -1).astype(BF16)
    wuv = w_uv.reshape(KV_RANK, -1).astype(BF16)
    wuq = jnp.zeros((Q_RANK, N_HEADS, HEAD_W), F32).at[:, :, :NOPE_DIM + ROPE_DIM].set(w_uq[0]).reshape(Q_RANK, -1).astype(BF16)
    kg = _on_lanes(k_nope_norm, 0)
    krg = _on_lanes(k_rope_norm, ROPE_LO)
    qg = _on_lanes(jnp.concatenate([q_nope_norm[0], q_rope_norm[0]]), 0)
    rows, ckv, krope, k_new, v_dec, vt_new, qt, q_dec = _mla_proj(
        pos, sorted_rows, cos_t, sin_t, cos_tt, sin_tt, kv_norm.reshape(1, -1), wdkv, kv_a_norm.reshape(1, -1), krg, wuk, wuv, kg,
        norm_mix[1].reshape(1, -1), w_dq[0].astype(BF16), q_a_norm[0].reshape(1, -1), wuq, qg)
    k_cache, v_cache = _cache_kv(cache_ckv.reshape(-1, KV_RANK), cache_krope.reshape(-1, ROPE_DIM), wuk, wuv, kg)

    wo = w_o[0].astype(BF16)
    h_prompt = _prompt_attn(qt, k_new, vt_new, rows, wo)
    h_dec = _sample_attn(q_dec, k_cache, v_cache, k_new, v_dec, rows, wo)
    sorted_rows, pos = _moe_layer(1, h_prompt, h_dec, norm_ffn, moe_w_group, moe_b_group, moe_w_expert, moe_b_expert, moe_w1, moe_w3, moe_w2)
    y_prompt, y_sample = _finish(pos, sorted_rows)

    return (y_prompt.reshape(BATCH, SEQ, D_MODEL), y_sample.reshape(DEC_BATCH, DEC_SEQ, D_MODEL),
            ckv[:N_PROMPT].reshape(BATCH, SEQ, KV_RANK), krope[:N_PROMPT].reshape(BATCH, SEQ, ROPE_DIM),
            ckv[N_PROMPT:].reshape(DEC_BATCH, DEC_SEQ, KV_RANK), krope[N_PROMPT:].reshape(DEC_BATCH, DEC_SEQ, ROPE_DIM),
            v_rows.reshape(1, DEC_BATCH, DEC_SEQ, D_GATE))
```

```python
import functools

import jax
import jax.numpy as jnp
import numpy as np
from jax import lax
from jax.experimental import pallas as pl
from jax.experimental.pallas import tpu as pltpu

F32 = jnp.float32
BF16 = jnp.bfloat16

D_MODEL = 1024
BATCH = 8
SEQ = 2048
DEC_BATCH = 16
DEC_SEQ = 16
PAST_LEN = 2048
CHUNK = 64
GMLP_BLOCK = 128
D_GATE = 2 * D_MODEL
N_SG = 8
SG_W = D_GATE // N_SG
N_HEADS = 8
NOPE_DIM = 64
ROPE_DIM = 32
V_DIM = 64
Q_RANK = 384
KV_RANK = 256
ROPE_THETA = 10000.0
SCALE = (NOPE_DIM + ROPE_DIM) ** -0.5
Q_SCALE = SCALE * float(np.log2(np.e))
N_EGROUPS = 4
EXPERTS_PER_GROUP = 4
N_EXPERTS = N_EGROUPS * EXPERTS_PER_GROUP
D_EXPERT = 512
EPS = 1e-6
NEG = -1e30

LANES = 128
SUBLANES = 8
ROW_TILES = D_MODEL // LANES
assert ROW_TILES == SUBLANES

N_PROMPT = BATCH * SEQ
N_DEC = DEC_BATCH * DEC_SEQ
T = N_PROMPT + N_DEC
TM = 256
assert N_PROMPT % TM == 0 and N_DEC == TM
N_TILES = T // TM
HEAD_W = LANES
ROPE_LO = NOPE_DIM
ROPE_HALF = ROPE_DIM // 2

PAIR_A = (0, 0, 0, 1, 1, 3)
PAIR_B = (1, 2, 3, 3, 2, 2)
N_PAIRS = 6
N_BUCKETS = N_EGROUPS * N_PAIRS
MOE_TILES = (T + N_BUCKETS * (TM - 1) + TM - 1) // TM
P_ROWS = MOE_TILES * TM

VMEM_LIMIT = 56 * 1024 * 1024


def _cparams(n_axes=1, vmem=VMEM_LIMIT):
    return pltpu.CompilerParams(dimension_semantics=("arbitrary",) * n_axes, vmem_limit_bytes=vmem)


def _rms(x, g):
    return x * lax.rsqrt(jnp.mean(x * x, axis=-1, keepdims=True) + EPS) * g


def _load_rows(ref, n):
    return jnp.concatenate([ref[pl.ds(s, n, stride=ROW_TILES), :] for s in range(ROW_TILES)], axis=1)


def _store_rows(ref, x, n):
    for s in range(ROW_TILES):
        ref[pl.ds(s, n, stride=ROW_TILES), :] = x[:, s * LANES:(s + 1) * LANES]


def _dot(a, b):
    return jnp.dot(a, b, preferred_element_type=F32)


def _split_dot(x, m):
    hi = x.astype(BF16)
    lo = (x - hi.astype(F32)).astype(BF16)
    return _dot(hi, m) + _dot(lo, m)


def _gmlp_body(xp_ref, xd_ref, nm_ref, win_ref, bin_ref, lng_ref, lnb_ref, wsp_ref, wsd_ref, bsp_ref, bsd_ref,
               wout_ref, bout_ref, nf_ref, wrh_ref, wrl_ref, br_ref,
               rows_ref, v_ref, info_ref, cnt_ref, gated_ref, carry_ref):
    i = pl.program_id(0)
    is_dec = i == N_TILES - 1
    _route_init(carry_ref)
    x = jnp.where(is_dec, xd_ref[...], xp_ref[...])
    xn = _rms(x, nm_ref[...]).astype(BF16)
    z = _dot(xn, win_ref[...]) + bin_ref[...]
    z = z * (0.5 * (1.0 + jnp.tanh(np.float32(np.sqrt(2.0 / np.pi)) * (z + 0.044715 * (z * z * z)))))
    u = z[:, :D_GATE]
    v = z[:, D_GATE:]
    mu = jnp.mean(v, axis=-1, keepdims=True)
    vc = v - mu
    var = jnp.mean(vc * vc, axis=-1, keepdims=True)
    v = vc * lax.rsqrt(var + EPS) * lng_ref[...] + lnb_ref[...]

    v_ref[...] = v
    vb = v.astype(BF16)
    for g in range(N_SG):
        ws = jnp.where(is_dec, wsd_ref[g], wsp_ref[g])
        bs = jnp.where(is_dec, bsd_ref[g], bsp_ref[g])
        for b in range(TM // GMLP_BLOCK):
            rows = slice(b * GMLP_BLOCK, (b + 1) * GMLP_BLOCK)
            cols = slice(g * SG_W, (g + 1) * SG_W)
            s = _dot(ws, vb[rows, cols]) + bs
            gated_ref[rows, cols] = (u[rows, cols] * s).astype(BF16)
    h = x + _dot(gated_ref[...], wout_ref[...]) + bout_ref[...]
    _store_rows(rows_ref, h, TM)
    _route_tile(h, nf_ref, wrh_ref, wrl_ref, br_ref, info_ref, cnt_ref, carry_ref)


def _gmlp_layer(x_prompt, x_dec, nm, w_in, b_in, ln_g, ln_b, ws_p, ws_d, bs_p, bs_d, w_out, b_out, nf, wr, br):
    const = lambda *shape: pl.BlockSpec(shape, lambda i: (0,) * len(shape))
    wr_hi = wr.astype(BF16)
    return pl.pallas_call(
        _gmlp_body,
        grid=(N_TILES,),
        in_specs=[
            pl.BlockSpec((TM, D_MODEL), lambda i: (jnp.minimum(i, N_TILES - 2), 0)), const(N_DEC, D_MODEL),
            const(1, D_MODEL), const(D_MODEL, 2 * D_GATE), const(1, 2 * D_GATE),
            const(1, D_GATE), const(1, D_GATE),
            const(N_SG, GMLP_BLOCK, GMLP_BLOCK), const(N_SG, GMLP_BLOCK, GMLP_BLOCK),
            const(N_SG, GMLP_BLOCK, 1), const(N_SG, GMLP_BLOCK, 1),
            const(D_GATE, D_MODEL), const(1, D_MODEL),
            const(1, D_MODEL), const(D_MODEL, LANES), const(D_MODEL, LANES), const(1, LANES),
        ],
        out_specs=[
            pl.BlockSpec((TM * ROW_TILES, LANES), lambda i: (i, 0)),
            const(N_DEC, D_GATE),
            pl.BlockSpec((TM, LANES), lambda i: (i, 0)),
            const(SUBLANES, LANES),
        ],
        out_shape=[jax.ShapeDtypeStruct((T * ROW_TILES, LANES), F32), jax.ShapeDtypeStruct((N_DEC, D_GATE), F32),
                   jax.ShapeDtypeStruct((T, LANES), F32), jax.ShapeDtypeStruct((SUBLANES, LANES), F32)],
        scratch_shapes=[pltpu.VMEM((TM, D_GATE), BF16), pltpu.VMEM((SUBLANES, LANES), F32)],
        compiler_params=_cparams(),
        name="gmlp_layer",
    )(x_prompt, x_dec, nm, w_in, b_in, ln_g, ln_b, ws_p, ws_d, bs_p, bs_d, w_out, b_out,
      nf, wr_hi, (wr - wr_hi.astype(F32)).astype(BF16), br)


def _router_body(hp_ref, hd_ref, nf_ref, wrh_ref, wrl_ref, br_ref, rows_ref, info_ref, cnt_ref, carry_ref):
    _route_init(carry_ref)
    h = jnp.where(pl.program_id(0) == N_TILES - 1, hd_ref[...], hp_ref[...])
    _store_rows(rows_ref, h, TM)
    _route_tile(h, nf_ref, wrh_ref, wrl_ref, br_ref, info_ref, cnt_ref, carry_ref)


def _route_init(carry_ref):
    @pl.when(pl.program_id(0) == 0)
    def _():
        carry_ref[...] = jnp.zeros_like(carry_ref)


def _route_tile(h, nf_ref, wrh_ref, wrl_ref, br_ref, info_ref, cnt_ref, carry_ref):
    xn = _rms(h, nf_ref[...])
    xh = xn.astype(BF16)
    xl = (xn - xh.astype(F32)).astype(BF16)
    logits = _dot(xh, wrh_ref[...]) + (_dot(xl, wrh_ref[...]) + _dot(xh, wrl_ref[...])) + br_ref[...]
    lane = lax.broadcasted_iota(jnp.int32, (TM, LANES), 1).astype(F32)

    def first_max(vals):
        vmax = jnp.max(vals, axis=-1, keepdims=True)
        idx = jnp.min(jnp.where(vals == vmax, lane, float(LANES)), axis=-1, keepdims=True)
        return vmax, idx

    lg = jnp.where(lane < N_EGROUPS, logits, -jnp.inf)
    gmax, g_idx = first_max(lg)
    g_p = 1.0 / jnp.sum(jnp.exp(lg - gmax), axis=-1, keepdims=True)
    e_lo = N_EGROUPS + EXPERTS_PER_GROUP * g_idx
    le = jnp.where((lane >= e_lo) & (lane < e_lo + EXPERTS_PER_GROUP), logits, -jnp.inf)
    v1, i1 = first_max(le)
    v2, i2 = first_max(jnp.where(lane == i1, -jnp.inf, le))
    e2 = jnp.exp(v2 - v1)
    w1 = (1.0 / (1.0 + e2)) * g_p
    w2 = (e2 / (1.0 + e2)) * g_p
    a1 = i1 - e_lo
    a2 = i2 - e_lo
    lo = jnp.minimum(a1, a2)
    hi = jnp.maximum(a1, a2)
    pair = jnp.where(lo == 0.0, hi - 1.0, jnp.where(lo == 1.0, jnp.where(hi == 3.0, 3.0, 4.0), 5.0))
    ea = jnp.where(pair < 3.0, 0.0, jnp.where(pair < 5.0, 1.0, 3.0))
    ga = jnp.where(a1 == ea, w1, w2)
    gb = jnp.where(a1 == ea, w2, w1)
    bucket = g_idx * N_PAIRS + pair

    onehot = (lane == bucket).astype(F32)
    r = lax.broadcasted_iota(jnp.int32, (TM, TM), 0)
    c = lax.broadcasted_iota(jnp.int32, (TM, TM), 1)
    before = _dot((c < r).astype(BF16), onehot.astype(BF16))
    carry = carry_ref[0:1, :]
    rank = jnp.sum(onehot * (before + carry), axis=-1, keepdims=True)
    new_carry = carry + jnp.sum(onehot, axis=0, keepdims=True)
    carry_ref[...] = jnp.broadcast_to(new_carry, carry_ref.shape)
    cnt_ref[...] = jnp.broadcast_to(new_carry, cnt_ref.shape)
    info_ref[...] = jnp.where(lane == 0.0, bucket,
                              jnp.where(lane == 1.0, rank, jnp.where(lane == 2.0, ga, jnp.where(lane == 3.0, gb, 0.0))))


def _router(h_prompt, h_dec, nf, wr, br):
    const = lambda *shape: pl.BlockSpec(shape, lambda i: (0,) * len(shape))
    wr_hi = wr.astype(BF16)
    return pl.pallas_call(
        _router_body,
        grid=(N_TILES,),
        in_specs=[pl.BlockSpec((TM, D_MODEL), lambda i: (jnp.minimum(i, N_TILES - 2), 0)), const(N_DEC, D_MODEL),
                  const(1, D_MODEL), const(D_MODEL, LANES), const(D_MODEL, LANES), const(1, LANES)],
        out_specs=[pl.BlockSpec((TM * ROW_TILES, LANES), lambda i: (i, 0)),
                   pl.BlockSpec((TM, LANES), lambda i: (i, 0)),
                   const(SUBLANES, LANES)],
        out_shape=[jax.ShapeDtypeStruct((T * ROW_TILES, LANES), F32), jax.ShapeDtypeStruct((T, LANES), F32),
                   jax.ShapeDtypeStruct((SUBLANES, LANES), F32)],
        scratch_shapes=[pltpu.VMEM((SUBLANES, LANES), F32)],
        compiler_params=_cparams(),
        name="moe_router",
    )(h_prompt, h_dec, nf, wr_hi, (wr - wr_hi.astype(F32)).astype(BF16), br)


GATHER_UNROLL = 8


def _gather_rows_start(idx_ref, base, src_ref, dst_ref, sem, n):
    def group(g, carry):
        for u in range(GATHER_UNROLL):
            r = g * GATHER_UNROLL + u
            src = pl.multiple_of(idx_ref[base + r] * ROW_TILES, ROW_TILES)
            dst = pl.multiple_of(r * ROW_TILES, ROW_TILES)
            pltpu.make_async_copy(src_ref.at[pl.ds(src, ROW_TILES), :], dst_ref.at[pl.ds(dst, ROW_TILES), :], sem).start()
        return carry

    lax.fori_loop(0, n // GATHER_UNROLL, group, 0)


def _gather_rows_wait(src_ref, dst_ref, sem, n):
    pltpu.make_async_copy(src_ref.at[pl.ds(0, n * ROW_TILES), :], dst_ref.at[pl.ds(0, n * ROW_TILES), :], sem).wait()


def _gather_tile(idx_ref, src_ref, buf, sem, n):
    i = pl.program_id(0)
    slot = lax.rem(i, 2)

    @pl.when(i == 0)
    def _():
        _gather_rows_start(idx_ref, 0, src_ref, buf.at[0], sem.at[0], n)

    @pl.when(i + 1 < pl.num_programs(0))
    def _():
        _gather_rows_start(idx_ref, (i + 1) * n, src_ref, buf.at[1 - slot], sem.at[1 - slot], n)

    _gather_rows_wait(src_ref, buf.at[slot], sem.at[slot], n)
    return buf.at[slot]


def _gather_scratch(n):
    return [pltpu.VMEM((2, n * ROW_TILES, LANES), F32), pltpu.SemaphoreType.DMA((2,))]


def _moe_body(idx_ref, ea_ref, eb_ref, cha_ref, chb_ref, valid_ref, rows_hbm, g_ref, nf_ref,
              w1a_ref, w3a_ref, w2a_ref, w1b_ref, w3b_ref, w2b_ref, out_ref,
              s1a, s3a, s2a, s1b, s3b, s2b, xbuf, xsem):
    i = pl.program_id(0)
    x_ref = _gather_tile(idx_ref, rows_hbm, xbuf, xsem, TM)

    @pl.when(cha_ref[i] == 1)
    def _():
        s1a[...] = w1a_ref[...].astype(BF16)
        s3a[...] = w3a_ref[...].astype(BF16)
        s2a[...] = w2a_ref[...].astype(BF16)

    @pl.when(chb_ref[i] == 1)
    def _():
        s1b[...] = w1b_ref[...].astype(BF16)
        s3b[...] = w3b_ref[...].astype(BF16)
        s2b[...] = w2b_ref[...].astype(BF16)

    @pl.when(valid_ref[i] == 1)
    def _():
        h = _load_rows(x_ref, TM)
        xn = _rms(h, nf_ref[...]).astype(BF16)
        g = g_ref[...]

        def ffn(w1, w3, w2):
            a = _dot(xn, w1[...])
            hdn = (a * (1.0 / (1.0 + jnp.exp(-a)))) * _dot(xn, w3[...])
            return _dot(hdn.astype(BF16), w2[...])

        y = g[:, 0:1] * ffn(s1a, s3a, s2a) + g[:, 1:2] * ffn(s1b, s3b, s2b)
        _store_rows(out_ref, h + y, TM)

    @pl.when(valid_ref[i] == 0)
    def _():
        out_ref[...] = jnp.zeros_like(out_ref)


def _moe_ffn(layer, idx_sorted, ea, eb, cha, chb, valid, rows, gates, nf, w1, w3, w2):
    wa = lambda shape: pl.BlockSpec((None, None) + shape, lambda i, ix, ea, eb, ca, cb, va: (layer, ea[i], 0, 0))
    wb = lambda shape: pl.BlockSpec((None, None) + shape, lambda i, ix, ea, eb, ca, cb, va: (layer, eb[i], 0, 0))
    up, down = (D_MODEL, D_EXPERT), (D_EXPERT, D_MODEL)
    return pl.pallas_call(
        _moe_body,
        grid_spec=pltpu.PrefetchScalarGridSpec(
            num_scalar_prefetch=6,
            grid=(MOE_TILES,),
            in_specs=[
                pl.BlockSpec(memory_space=pl.ANY),
                pl.BlockSpec((TM, 2), lambda i, *_: (i, 0)),
                pl.BlockSpec((1, D_MODEL), lambda i, *_: (0, 0)),
                wa(up), wa(up), wa(down), wb(up), wb(up), wb(down),
            ],
            out_specs=pl.BlockSpec((TM * ROW_TILES, LANES), lambda i, *_: (i, 0)),
            scratch_shapes=[pltpu.VMEM(up, BF16), pltpu.VMEM(up, BF16), pltpu.VMEM(down, BF16),
                            pltpu.VMEM(up, BF16), pltpu.VMEM(up, BF16), pltpu.VMEM(down, BF16)]
            + _gather_scratch(TM),
        ),
        out_shape=jax.ShapeDtypeStruct((P_ROWS * ROW_TILES, LANES), F32),
        compiler_params=_cparams(),
        name="moe_ffn",
    )(idx_sorted, ea, eb, cha, chb, valid, rows, gates, nf, w1, w3, w2, w1, w3, w2)


def _router_weights(layer, w_group, b_group, w_expert, b_expert):
    wr = jnp.zeros((D_MODEL, LANES), F32)
    wr = wr.at[:, :N_EGROUPS].set(w_group[layer]).at[:, N_EGROUPS:N_EGROUPS + N_EXPERTS].set(w_expert[layer])
    br = jnp.zeros((1, LANES), F32)
    br = br.at[0, :N_EGROUPS].set(b_group[layer]).at[0, N_EGROUPS:N_EGROUPS + N_EXPERTS].set(b_expert[layer])
    return wr, br


def _moe_layer(layer, rows, info, cnt, nf, w1, w3, w2):
    bucket = info[:, 0].astype(jnp.int32)
    rank = info[:, 1].astype(jnp.int32)
    counts = cnt[0, :N_BUCKETS].astype(jnp.int32)
    n_tiles = (counts + TM - 1) // TM
    tile_end = jnp.cumsum(n_tiles)
    tile_start = tile_end - n_tiles
    start_of = jnp.sum(jnp.where(bucket[:, None] == jnp.arange(N_BUCKETS)[None, :], tile_start[None, :], 0), axis=1)
    pos = start_of * TM + rank
    packed = jnp.concatenate([jnp.arange(T, dtype=F32)[:, None], info[:, 2:4]], axis=1)
    base = jnp.zeros((P_ROWS, 3), F32).at[:, 0].set((jnp.arange(P_ROWS) % T).astype(F32))
    packed = base.at[pos].set(packed, unique_indices=True, indices_are_sorted=False)
    idx_sorted = packed[:, 0].astype(jnp.int32)
    gates = packed[:, 1:3]
    total = tile_end[-1]
    j = jnp.minimum(jnp.arange(MOE_TILES), total - 1)
    tb = jnp.sum((j[:, None] >= tile_end[None, :]).astype(jnp.int32), axis=1)
    grp, pair = tb // N_PAIRS, tb % N_PAIRS
    ea = (grp * EXPERTS_PER_GROUP + jnp.asarray(PAIR_A, jnp.int32)[pair]).astype(jnp.int32)
    eb = (grp * EXPERTS_PER_GROUP + jnp.asarray(PAIR_B, jnp.int32)[pair]).astype(jnp.int32)
    first = jnp.arange(MOE_TILES) == 0
    cha = (first | (ea != jnp.roll(ea, 1))).astype(jnp.int32)
    chb = (first | (eb != jnp.roll(eb, 1))).astype(jnp.int32)
    valid = (jnp.arange(MOE_TILES) < total).astype(jnp.int32)

    return _moe_ffn(layer, idx_sorted, ea, eb, cha, chb, valid, rows, gates, nf, w1, w3, w2), pos


VT_ROWS = LANES + 16


def _seg_matrix():
    lane = np.arange(LANES)
    seg = np.where(lane < NOPE_DIM, 0, np.where(lane < NOPE_DIM + ROPE_DIM, 1, 2))
    return jnp.asarray(seg[:, None] == seg[None, :], BF16)


def _seg_count():
    lane = np.arange(LANES)
    return jnp.asarray(np.where(lane < NOPE_DIM, 1.0 / NOPE_DIM, 1.0 / ROPE_DIM), F32).reshape(1, LANES)


def _rope_swap(x):
    lane = lax.broadcasted_iota(jnp.int32, x.shape, 1)
    return jnp.where(lane < ROPE_LO + ROPE_HALF, pltpu.roll(x, LANES - ROPE_HALF, 1), pltpu.roll(x, ROPE_HALF, 1))


def _expand_k(cb, kr, wuk_ref, seg, cnt, kg_ref, k_ref):
    kn = _dot(cb, wuk_ref[...])
    heads = [kn[:, hh * HEAD_W:(hh + 1) * HEAD_W] for hh in range(N_HEADS)]
    ms = [_split_dot(x * x, seg) * cnt for x in heads]
    for hh in range(N_HEADS):
        k_ref[:, hh * HEAD_W:(hh + 1) * HEAD_W] = (heads[hh] * lax.rsqrt(ms[hh] + EPS) * kg_ref[...] + kr).astype(BF16)


def _mla_proj_body(pos_ref, sorted_hbm, cos_ref, sin_ref, cost_ref, sint_ref, kvn_ref, wdkv_ref, kvan_ref, krg_ref,
                   wuk_ref, wuv_ref, wuvt_ref, kg_ref, nmq_ref, wdq_ref, qan_ref, wuqt_ref, qg_ref, seg_ref, cnt_ref,
                   rows_ref, ckvp_ref, ckvd_ref, krp_ref, krd_ref, k_ref, vdec_ref, vt_ref, qt_ref, qdec_ref,
                   xbuf, xsem):
    is_dec = pl.program_id(0) == N_TILES - 1
    x_ref = _gather_tile(pos_ref, sorted_hbm, xbuf, xsem, TM)
    rows_ref[...] = x_ref[...]
    h = _load_rows(x_ref, TM)
    c = _dot(_rms(h, kvn_ref[...]).astype(BF16), wdkv_ref[...])
    ckv = _rms(c[:, :KV_RANK], kvan_ref[...])
    kr = c[:, KV_RANK:]
    kr = kr * lax.rsqrt(jnp.sum(kr * kr, axis=-1, keepdims=True) * (1.0 / ROPE_DIM) + EPS) * krg_ref[...]
    kr = kr * cos_ref[...] + _rope_swap(kr) * sin_ref[...]
    cb = ckv.astype(BF16)
    _expand_k(cb, kr, wuk_ref, seg_ref[...], cnt_ref[...], kg_ref, k_ref)
    vt = _dot(wuvt_ref[...], ckv.T.astype(BF16)).astype(BF16)
    for pr in range(N_HEADS // 2):
        vt_ref[0, pr * VT_ROWS:pr * VT_ROWS + LANES, :] = vt[pr * LANES:(pr + 1) * LANES]
        vt_ref[0, pr * VT_ROWS + LANES:(pr + 1) * VT_ROWS, :] = jnp.ones((VT_ROWS - LANES, TM), BF16)
    cq = _rms(_dot(_rms(h, nmq_ref[...]).astype(BF16), wdq_ref[...]), qan_ref[...])
    qt = _dot(wuqt_ref[...], cq.T.astype(BF16))
    cost, sint, qg = cost_ref[0], sint_ref[0], qg_ref[...]
    for hh in range(N_HEADS):
        x = qt[hh * HEAD_W:(hh + 1) * HEAD_W, :]
        xn, xr = x[:NOPE_DIM], x[ROPE_LO:ROPE_LO + ROPE_DIM]
        xn = xn * lax.rsqrt(jnp.mean(xn * xn, axis=0, keepdims=True) + EPS) * qg[:NOPE_DIM]
        xr = xr * lax.rsqrt(jnp.mean(xr * xr, axis=0, keepdims=True) + EPS) * qg[ROPE_LO:ROPE_LO + ROPE_DIM]
        x1, x2 = xr[:ROPE_HALF], xr[ROPE_HALF:]
        qh = jnp.concatenate([xn, x1 * cost - x2 * sint, x1 * sint + x2 * cost,
                              jnp.zeros((HEAD_W - NOPE_DIM - ROPE_DIM, TM), F32)], axis=0) * Q_SCALE
        qt_ref[0, hh * HEAD_W:(hh + 1) * HEAD_W, :] = qh.astype(BF16)

    @pl.when(jnp.logical_not(is_dec))
    def _():
        ckvp_ref[...] = ckv
        krp_ref[...] = kr[:, ROPE_LO:ROPE_LO + ROPE_DIM]

    @pl.when(is_dec)
    def _():
        ckvd_ref[...] = ckv
        krd_ref[...] = kr[:, ROPE_LO:ROPE_LO + ROPE_DIM]
        vdec_ref[...] = _dot(cb, wuv_ref[...]).astype(BF16)
        for hh in range(N_HEADS):
            rows = slice(hh * HEAD_W, (hh + 1) * HEAD_W)
            qdec_ref[:, rows] = qt_ref[0, rows, :].astype(F32).T.astype(BF16)


def _mla_proj(pos, sorted_rows, cos_t, sin_t, cos_tt, sin_tt, kvn, wdkv, kvan, krg, wuk, wuv, kg, nmq, wdq, qan, wuq, qg):
    const = lambda *shape: pl.BlockSpec(shape, lambda i, p: (0,) * len(shape))
    tab_tile = lambda i: jnp.where(i < N_PROMPT // TM, i % (SEQ // TM), SEQ // TM)
    tab = pl.BlockSpec((TM, LANES), lambda i, p: (tab_tile(i), 0))
    tab_t = pl.BlockSpec((1, ROPE_HALF, TM), lambda i, p: (tab_tile(i), 0, 0))
    row = lambda w: pl.BlockSpec((TM, w), lambda i, p: (i, 0))
    prow = lambda w: pl.BlockSpec((TM, w), lambda i, p: (jnp.minimum(i, N_TILES - 2), 0))
    return pl.pallas_call(
        _mla_proj_body,
        grid_spec=pltpu.PrefetchScalarGridSpec(
            num_scalar_prefetch=1,
            grid=(N_TILES,),
            in_specs=[
                pl.BlockSpec(memory_space=pl.ANY), tab, tab, tab_t, tab_t,
                const(1, D_MODEL), const(D_MODEL, KV_RANK + LANES), const(1, KV_RANK), const(1, LANES),
                const(KV_RANK, N_HEADS * HEAD_W), const(KV_RANK, N_HEADS * V_DIM), const(N_HEADS * V_DIM, KV_RANK),
                const(1, LANES),
                const(1, D_MODEL), const(D_MODEL, Q_RANK), const(1, Q_RANK), const(N_HEADS * HEAD_W, Q_RANK),
                const(HEAD_W, TM), const(LANES, LANES), const(1, LANES),
            ],
            out_specs=[pl.BlockSpec((TM * ROW_TILES, LANES), lambda i, p: (i, 0)),
                       prow(KV_RANK), const(N_DEC, KV_RANK), prow(ROPE_DIM), const(N_DEC, ROPE_DIM),
                       row(N_HEADS * HEAD_W), const(N_DEC, N_HEADS * V_DIM),
                       pl.BlockSpec((1, N_HEADS // 2 * VT_ROWS, TM), lambda i, p: (i, 0, 0)),
                       pl.BlockSpec((1, N_HEADS * HEAD_W, TM), lambda i, p: (i, 0, 0)),
                       const(N_DEC, N_HEADS * HEAD_W)],
            scratch_shapes=_gather_scratch(TM),
        ),
        out_shape=[
            jax.ShapeDtypeStruct((T * ROW_TILES, LANES), F32),
            jax.ShapeDtypeStruct((N_PROMPT, KV_RANK), F32), jax.ShapeDtypeStruct((N_DEC, KV_RANK), F32),
            jax.ShapeDtypeStruct((N_PROMPT, ROPE_DIM), F32), jax.ShapeDtypeStruct((N_DEC, ROPE_DIM), F32),
            jax.ShapeDtypeStruct((T, N_HEADS * HEAD_W), BF16), jax.ShapeDtypeStruct((N_DEC, N_HEADS * V_DIM), BF16),
            jax.ShapeDtypeStruct((N_TILES, N_HEADS // 2 * VT_ROWS, TM), BF16),
            jax.ShapeDtypeStruct((N_TILES, N_HEADS * HEAD_W, TM), BF16),
            jax.ShapeDtypeStruct((N_DEC, N_HEADS * HEAD_W), BF16),
        ],
        compiler_params=_cparams(),
        name="mla_proj",
    )(pos, sorted_rows, cos_t, sin_t, cos_tt, sin_tt, kvn, wdkv, kvan, krg, wuk, wuv, wuv.T, kg, nmq, wdq, qan,
      wuq.T, jnp.broadcast_to(qg.reshape(HEAD_W, 1), (HEAD_W, TM)), _seg_matrix(), _seg_count())


def _cache_kv_body(ckv_ref, kr_ref, place_ref, wuk_ref, wuv_ref, kg_ref, seg_ref, cnt_ref, k_ref, v_ref):
    kr = _dot(kr_ref[...].astype(BF16), place_ref[...])
    cb = ckv_ref[...].astype(BF16)
    _expand_k(cb, kr, wuk_ref, seg_ref[...], cnt_ref[...], kg_ref, k_ref)
    v_ref[...] = _dot(cb, wuv_ref[...]).astype(BF16)


CACHE_ROWS = 1024
assert PAST_LEN % CACHE_ROWS == 0


def _cache_kv(ckv, kr, wuk, wuv, kg):
    n = ckv.shape[0]
    place = jnp.asarray(np.arange(ROPE_DIM)[:, None] + ROPE_LO == np.arange(LANES)[None, :], BF16)
    const = lambda *shape: pl.BlockSpec(shape, lambda i: (0,) * len(shape))
    row = lambda w: pl.BlockSpec((CACHE_ROWS, w), lambda i: (i, 0))
    return pl.pallas_call(
        _cache_kv_body,
        grid=(n // CACHE_ROWS,),
        in_specs=[row(KV_RANK), row(ROPE_DIM), const(ROPE_DIM, LANES), const(KV_RANK, N_HEADS * HEAD_W),
                  const(KV_RANK, N_HEADS * V_DIM), const(1, LANES), const(LANES, LANES), const(1, LANES)],
        out_specs=[row(N_HEADS * HEAD_W), row(N_HEADS * V_DIM)],
        out_shape=[jax.ShapeDtypeStruct((n, N_HEADS * HEAD_W), BF16), jax.ShapeDtypeStruct((n, N_HEADS * V_DIM), BF16)],
        compiler_params=_cparams(),
        name="cache_kv",
    )(ckv, kr, place, wuk, wuv, kg, _seg_matrix(), _seg_count())


TQ = 256
TK = 256
assert TQ == TK and TQ % CHUNK == 0
SCORE_LOOKAHEAD = 4


def _qk(q, k):
    return lax.dot_general(q, k, (((1,), (1,)), ((), ())), preferred_element_type=F32)


def _merge_heads(o_ref, outs, rows):
    lane = lax.broadcasted_iota(jnp.int32, (rows, LANES), 1)
    for pr in range(N_HEADS // 2):
        o_ref[:, pr * LANES:(pr + 1) * LANES] = jnp.where(lane < V_DIM, outs[2 * pr], outs[2 * pr + 1]).astype(BF16)


def _prompt_attn_body(qt_ref, k_ref, vt_ref, rows_ref, wo_ref, out_ref, m_scr, acc_scr):
    qi = pl.program_id(1)
    m_scr[...] = jnp.full(m_scr.shape, NEG, F32)
    acc_scr[...] = jnp.zeros(acc_scr.shape, F32)

    def all_heads(j, mask):
        ks = pl.ds(pl.multiple_of(j * TK, TK), TK)

        def scores(hh):
            hcols = slice(hh * HEAD_W, (hh + 1) * HEAD_W)
            return _dot(k_ref[ks, hcols], qt_ref[0, hcols, :])

        ahead = [scores(hh) for hh in range(SCORE_LOOKAHEAD)]
        for hh in range(N_HEADS):
            vrows = slice((hh // 2) * VT_ROWS, (hh // 2 + 1) * VT_ROWS)
            s = ahead.pop(0)
            if hh + SCORE_LOOKAHEAD < N_HEADS:
                ahead.append(scores(hh + SCORE_LOOKAHEAD))
            if mask is not None:
                s = jnp.where(mask, s, NEG)
            m_old = m_scr[hh]
            m_new = jnp.maximum(m_old, jnp.max(s, axis=0, keepdims=True))
            p = jnp.exp2(s - m_new).astype(BF16)
            m_scr[hh] = m_new
            acc_scr[hh] = jnp.exp2(m_old - m_new) * acc_scr[hh] + _dot(vt_ref[j, vrows, :], p)

    def step(j, carry):
        all_heads(j, None)
        return carry

    lax.fori_loop(0, qi, step, 0)
    kc = lax.broadcasted_iota(jnp.int32, (TK, TQ), 0) // CHUNK
    qc = lax.broadcasted_iota(jnp.int32, (TK, TQ), 1) // CHUNK
    all_heads(qi, kc <= qc)
    row = lax.broadcasted_iota(jnp.int32, (LANES, TQ), 0)
    pairs = []
    for pr in range(N_HEADS // 2):
        even = acc_scr[2 * pr, :LANES] / acc_scr[2 * pr, LANES:LANES + 1]
        odd = acc_scr[2 * pr + 1, :LANES] / acc_scr[2 * pr + 1, LANES:LANES + 1]
        pairs.append(jnp.where(row < V_DIM, even, odd))
    o = jnp.concatenate(pairs, axis=0).T.astype(BF16)
    out_ref[...] = _load_rows(rows_ref, TQ) + _dot(o, wo_ref[...])


def _prompt_attn(qt, k, vt, rows, wo):
    nq = SEQ // TQ
    return pl.pallas_call(
        _prompt_attn_body,
        grid=(BATCH, nq),
        in_specs=[
            pl.BlockSpec((1, N_HEADS * HEAD_W, TQ), lambda b, i: (b * nq + i, 0, 0)),
            pl.BlockSpec((SEQ, N_HEADS * HEAD_W), lambda b, i: (b, 0)),
            pl.BlockSpec((SEQ // TK, N_HEADS // 2 * VT_ROWS, TK), lambda b, i: (b, 0, 0)),
            pl.BlockSpec((TQ * ROW_TILES, LANES), lambda b, i: (b * nq + i, 0)),
            pl.BlockSpec((N_HEADS * V_DIM, D_MODEL), lambda b, i: (0, 0)),
        ],
        out_specs=pl.BlockSpec((TQ, D_MODEL), lambda b, i: (b * nq + i, 0)),
        out_shape=jax.ShapeDtypeStruct((N_PROMPT, D_MODEL), F32),
        scratch_shapes=[pltpu.VMEM((N_HEADS, 1, TQ), F32), pltpu.VMEM((N_HEADS, VT_ROWS, TQ), F32)],
        compiler_params=_cparams(2),
        name="prompt_attn",
    )(qt, k, vt, rows, wo)


def _sample_attn_body(q_ref, kc_ref, vc_ref, kn_ref, vn_ref, rows_ref, wo_ref, out_ref, o_scr):
    outs = []
    for hh in range(N_HEADS):
        hcols = slice(hh * HEAD_W, (hh + 1) * HEAD_W)
        vcols = slice((hh // 2) * LANES, (hh // 2 + 1) * LANES)
        q = q_ref[:, hcols]
        sc = _qk(q, kc_ref[:, hcols])
        sn = _qk(q, kn_ref[:, hcols])
        m = jnp.maximum(jnp.max(sc, axis=-1, keepdims=True), jnp.max(sn, axis=-1, keepdims=True))
        pc = jnp.exp2(sc - m)
        pn = jnp.exp2(sn - m)
        l = jnp.sum(pc, axis=-1, keepdims=True) + jnp.sum(pn, axis=-1, keepdims=True)
        acc = _dot(pc.astype(BF16), vc_ref[:, vcols]) + _dot(pn.astype(BF16), vn_ref[:, vcols])
        outs.append(acc / l)
    _merge_heads(o_scr, outs, DEC_SEQ)
    out_ref[...] = _load_rows(rows_ref, DEC_SEQ) + _dot(o_scr[...], wo_ref[...])


def _sample_attn(q, kc, vc, kn, vn, rows, wo):
    off = N_PROMPT // DEC_SEQ
    return pl.pallas_call(
        _sample_attn_body,
        grid=(DEC_BATCH,),
        in_specs=[
            pl.BlockSpec((DEC_SEQ, N_HEADS * HEAD_W), lambda b: (b, 0)),
            pl.BlockSpec((PAST_LEN, N_HEADS * HEAD_W), lambda b: (b, 0)),
            pl.BlockSpec((PAST_LEN, N_HEADS * V_DIM), lambda b: (b, 0)),
            pl.BlockSpec((DEC_SEQ, N_HEADS * HEAD_W), lambda b: (off + b, 0)),
            pl.BlockSpec((DEC_SEQ, N_HEADS * V_DIM), lambda b: (b, 0)),
            pl.BlockSpec((DEC_SEQ * ROW_TILES, LANES), lambda b: (off + b, 0)),
            pl.BlockSpec((N_HEADS * V_DIM, D_MODEL), lambda b: (0, 0)),
        ],
        out_specs=pl.BlockSpec((DEC_SEQ, D_MODEL), lambda b: (b, 0)),
        out_shape=jax.ShapeDtypeStruct((N_DEC, D_MODEL), F32),
        scratch_shapes=[pltpu.VMEM((DEC_SEQ, N_HEADS * V_DIM), BF16)],
        compiler_params=_cparams(),
        name="sample_attn",
    )(q, kc, vc, kn, vn, rows, wo)


def _finish_body(pos_ref, sorted_hbm, yp_ref, ys_ref, xbuf, xsem):
    i = pl.program_id(0)
    y = _load_rows(_gather_tile(pos_ref, sorted_hbm, xbuf, xsem, TM), TM)

    @pl.when(i < N_TILES - 1)
    def _():
        yp_ref[...] = y

    @pl.when(i == N_TILES - 1)
    def _():
        ys_ref[...] = y


def _finish(pos, sorted_rows):
    return pl.pallas_call(
        _finish_body,
        grid_spec=pltpu.PrefetchScalarGridSpec(
            num_scalar_prefetch=1,
            grid=(N_TILES,),
            in_specs=[pl.BlockSpec(memory_space=pl.ANY)],
            out_specs=[pl.BlockSpec((TM, D_MODEL), lambda i, p: (jnp.minimum(i, N_TILES - 2), 0)),
                       pl.BlockSpec((N_DEC, D_MODEL), lambda i, p: (0, 0))],
            scratch_shapes=_gather_scratch(TM),
        ),
        out_shape=[jax.ShapeDtypeStruct((N_PROMPT, D_MODEL), F32), jax.ShapeDtypeStruct((N_DEC, D_MODEL), F32)],
        compiler_params=_cparams(),
        name="finish",
    )(pos, sorted_rows)


def _rope_tables():
    half = ROPE_DIM // 2
    inv_freq = ROPE_THETA ** (-jnp.arange(half, dtype=F32) / half)
    dec_pos = PAST_LEN + jnp.tile(jnp.arange(DEC_SEQ, dtype=jnp.int32), DEC_BATCH)
    pos = jnp.concatenate([jnp.arange(SEQ, dtype=jnp.int32), dec_pos])
    ang = pos.astype(F32)[:, None] * inv_freq[None, :]
    cos, sin = jnp.cos(ang), jnp.sin(ang)
    n = pos.shape[0]
    cos_t = jnp.ones((n, LANES), F32).at[:, ROPE_LO:ROPE_LO + ROPE_DIM].set(jnp.concatenate([cos, cos], axis=1))
    sin_t = jnp.zeros((n, LANES), F32).at[:, ROPE_LO:ROPE_LO + ROPE_DIM].set(jnp.concatenate([-sin, sin], axis=1))
    to_tiles = lambda a: a.reshape(n // TM, TM, half).transpose(0, 2, 1)
    return cos_t, sin_t, to_tiles(cos), to_tiles(sin)


def _on_lanes(vec, lo):
    return jnp.zeros((1, LANES), F32).at[0, lo:lo + vec.shape[0]].set(vec)


def kernel(x_prompt, x_sample, cache_ckv, cache_krope, norm_mix, norm_ffn, gm_w_in, gm_b_in, gm_ln_g, gm_ln_b, gm_w_s, gm_b_s, gm_w_out, gm_b_out, kv_norm, w_dkv, kv_a_norm, k_rope_norm, w_uk, w_uv, k_nope_norm, w_dq, q_a_norm, w_uq, q_nope_norm, q_rope_norm, w_o, moe_w_group, moe_b_group, moe_w_expert, moe_b_expert, moe_w1, moe_w3, moe_w2):
    nf0, nf1 = norm_ffn[0].reshape(1, D_MODEL), norm_ffn[1].reshape(1, D_MODEL)

    idx = np.arange(GMLP_BLOCK)
    allowed = (idx[None, :] // CHUNK) <= (idx[:, None] // CHUNK)
    ws_p = jnp.where(allowed[None], gm_w_s[0], 0.0).astype(BF16)
    same_seq = (idx[None, :] // DEC_SEQ) == (idx[:, None] // DEC_SEQ)
    ws_d = jnp.where(same_seq[None], jnp.tile(gm_w_s[0][:, :DEC_SEQ, :DEC_SEQ], (1, GMLP_BLOCK // DEC_SEQ, GMLP_BLOCK // DEC_SEQ)), 0.0).astype(BF16)
    bs_p = gm_b_s[0][:, :, None]
    bs_d = jnp.tile(gm_b_s[0][:, :DEC_SEQ], (1, GMLP_BLOCK // DEC_SEQ))[:, :, None]
    rows, v_rows, info, cnt = _gmlp_layer(
        x_prompt.reshape(N_PROMPT, D_MODEL), x_sample.reshape(N_DEC, D_MODEL),
        norm_mix[0].reshape(1, -1), gm_w_in[0].astype(BF16), gm_b_in[0].reshape(1, -1),
        gm_ln_g[0].reshape(1, -1), gm_ln_b[0].reshape(1, -1), ws_p, ws_d, bs_p, bs_d,
        gm_w_out[0].astype(BF16), gm_b_out[0].reshape(1, -1),
        nf0, *_router_weights(0, moe_w_group, moe_b_group, moe_w_expert, moe_b_expert))
    sorted_rows, pos = _moe_layer(0, rows, info, cnt, nf0, moe_w1, moe_w3, moe_w2)

    cos_t, sin_t, cos_tt, sin_tt = _rope_tables()
    wdkv = jnp.zeros((D_MODEL, KV_RANK + LANES), F32).at[:, :KV_RANK].set(w_dkv[:, :KV_RANK])
    wdkv = wdkv.at[:, KV_RANK + ROPE_LO:KV_RANK + ROPE_LO + ROPE_DIM].set(w_dkv[:, KV_RANK:]).astype(BF16)
    wuk = jnp.zeros((KV_RANK, N_HEADS, HEAD_W), F32).at[:, :, :NOPE_DIM].set(w_uk).reshape(KV_RANK, -1).astype(BF16)
    wuv = w_uv.reshape(KV_RANK, -1).astype(BF16)
    wuq = jnp.zeros((Q_RANK, N_HEADS, HEAD_W), F32).at[:, :, :NOPE_DIM + ROPE_DIM].set(w_uq[0]).reshape(Q_RANK, -1).astype(BF16)
    kg = _on_lanes(k_nope_norm, 0)
    krg = _on_lanes(k_rope_norm, ROPE_LO)
    qg = _on_lanes(jnp.concatenate([q_nope_norm[0], q_rope_norm[0]]), 0)
    rows, ckv_p, ckv_d, krope_p, krope_d, k_new, v_dec, vt_new, qt, q_dec = _mla_proj(
        pos, sorted_rows, cos_t, sin_t, cos_tt, sin_tt, kv_norm.reshape(1, -1), wdkv, kv_a_norm.reshape(1, -1), krg, wuk, wuv, kg,
        norm_mix[1].reshape(1, -1), w_dq[0].astype(BF16), q_a_norm[0].reshape(1, -1), wuq, qg)
    k_cache, v_cache = _cache_kv(cache_ckv.reshape(-1, KV_RANK), cache_krope.reshape(-1, ROPE_DIM), wuk, wuv, kg)

    wo = w_o[0].astype(BF16)
    h_prompt = _prompt_attn(qt, k_new, vt_new, rows, wo)
    h_dec = _sample_attn(q_dec, k_cache, v_cache, k_new, v_dec, rows, wo)
    rows, info, cnt = _router(h_prompt, h_dec, nf1,
                              *_router_weights(1, moe_w_group, moe_b_group, moe_w_expert, moe_b_expert))
    sorted_rows, pos = _moe_layer(1, rows, info, cnt, nf1, moe_w1, moe_w3, moe_w2)
    y_prompt, y_sample = _finish(pos, sorted_rows)

    return (y_prompt.reshape(BATCH, SEQ, D_MODEL), y_sample.reshape(DEC_BATCH, DEC_SEQ, D_MODEL),
            ckv_p.reshape(BATCH, SEQ, KV_RANK), krope_p.reshape(BATCH, SEQ, ROPE_DIM),
            ckv_d.reshape(DEC_BATCH, DEC_SEQ, KV_RANK), krope_d.reshape(DEC_BATCH, DEC_SEQ, ROPE_DIM),
            v_rows.reshape(1, DEC_BATCH, DEC_SEQ, D_GATE))
```

```python
import functools

import jax
import jax.numpy as jnp
import numpy as np
from jax import lax
from jax.experimental import pallas as pl
from jax.experimental.pallas import tpu as pltpu

F32 = jnp.float32
BF16 = jnp.bfloat16

D_MODEL = 1024
BATCH = 8
SEQ = 2048
DEC_BATCH = 16
DEC_SEQ = 16
PAST_LEN = 2048
CHUNK = 64
GMLP_BLOCK = 128
D_GATE = 2 * D_MODEL
N_SG = 8
SG_W = D_GATE // N_SG
N_HEADS = 8
NOPE_DIM = 64
ROPE_DIM = 32
V_DIM = 64
Q_RANK = 384
KV_RANK = 256
ROPE_THETA = 10000.0
SCALE = (NOPE_DIM + ROPE_DIM) ** -0.5
Q_SCALE = SCALE * float(np.log2(np.e))
N_EGROUPS = 4
EXPERTS_PER_GROUP = 4
N_EXPERTS = N_EGROUPS * EXPERTS_PER_GROUP
D_EXPERT = 512
EPS = 1e-6
NEG = -1e30

LANES = 128
SUBLANES = 8
ROW_TILES = D_MODEL // LANES
assert ROW_TILES == SUBLANES

N_PROMPT = BATCH * SEQ
N_DEC = DEC_BATCH * DEC_SEQ
T = N_PROMPT + N_DEC
TM = 256
assert N_PROMPT % TM == 0 and N_DEC == TM
N_TILES = T // TM
HEAD_W = LANES
ROPE_LO = NOPE_DIM
ROPE_HALF = ROPE_DIM // 2

PAIR_A = (0, 0, 0, 1, 1, 3)
PAIR_B = (1, 2, 3, 3, 2, 2)
N_PAIRS = 6
N_BUCKETS = N_EGROUPS * N_PAIRS
MOE_TILES = (T + N_BUCKETS * (TM - 1) + TM - 1) // TM
P_ROWS = MOE_TILES * TM

VMEM_LIMIT = 56 * 1024 * 1024


def _cparams(n_axes=1, vmem=VMEM_LIMIT):
    return pltpu.CompilerParams(dimension_semantics=("arbitrary",) * n_axes, vmem_limit_bytes=vmem)


def _rms(x, g):
    return x * lax.rsqrt(jnp.mean(x * x, axis=-1, keepdims=True) + EPS) * g


def _load_rows(ref, n):
    return jnp.concatenate([ref[pl.ds(s, n, stride=ROW_TILES), :] for s in range(ROW_TILES)], axis=1)


def _store_rows(ref, x, n):
    for s in range(ROW_TILES):
        ref[pl.ds(s, n, stride=ROW_TILES), :] = x[:, s * LANES:(s + 1) * LANES]


def _dot(a, b):
    return jnp.dot(a, b, preferred_element_type=F32)


def _split_dot(x, m):
    hi = x.astype(BF16)
    lo = (x - hi.astype(F32)).astype(BF16)
    return _dot(hi, m) + _dot(lo, m)


def _gmlp_body(xp_ref, xd_ref, nm_ref, win_ref, bin_ref, lng_ref, lnb_ref, wsp_ref, wsd_ref, bsp_ref, bsd_ref,
               wout_ref, bout_ref, nf_ref, wrh_ref, wrl_ref, br_ref,
               rows_ref, v_ref, info_ref, cnt_ref, gated_ref, carry_ref):
    i = pl.program_id(0)
    is_dec = i == N_TILES - 1
    _route_init(carry_ref)
    x = jnp.where(is_dec, xd_ref[...], xp_ref[...])
    xn = _rms(x, nm_ref[...]).astype(BF16)
    z = _dot(xn, win_ref[...]) + bin_ref[...]
    z = z * (0.5 * (1.0 + jnp.tanh(np.float32(np.sqrt(2.0 / np.pi)) * (z + 0.044715 * (z * z * z)))))
    u = z[:, :D_GATE]
    v = z[:, D_GATE:]
    mu = jnp.mean(v, axis=-1, keepdims=True)
    vc = v - mu
    var = jnp.mean(vc * vc, axis=-1, keepdims=True)
    v = vc * lax.rsqrt(var + EPS) * lng_ref[...] + lnb_ref[...]

    v_ref[...] = v
    vb = v.astype(BF16)
    for g in range(N_SG):
        ws = jnp.where(is_dec, wsd_ref[g], wsp_ref[g])
        bs = jnp.where(is_dec, bsd_ref[g], bsp_ref[g])
        for b in range(TM // GMLP_BLOCK):
            rows = slice(b * GMLP_BLOCK, (b + 1) * GMLP_BLOCK)
            cols = slice(g * SG_W, (g + 1) * SG_W)
            s = _dot(ws, vb[rows, cols]) + bs
            gated_ref[rows, cols] = (u[rows, cols] * s).astype(BF16)
    h = x + _dot(gated_ref[...], wout_ref[...]) + bout_ref[...]
    _store_rows(rows_ref, h, TM)
    _route_tile(h, nf_ref, wrh_ref, wrl_ref, br_ref, info_ref, cnt_ref, carry_ref)


def _gmlp_layer(x_prompt, x_dec, nm, w_in, b_in, ln_g, ln_b, ws_p, ws_d, bs_p, bs_d, w_out, b_out, nf, wr, br):
    const = lambda *shape: pl.BlockSpec(shape, lambda i: (0,) * len(shape))
    wr_hi = wr.astype(BF16)
    return pl.pallas_call(
        _gmlp_body,
        grid=(N_TILES,),
        in_specs=[
            pl.BlockSpec((TM, D_MODEL), lambda i: (jnp.minimum(i, N_TILES - 2), 0)), const(N_DEC, D_MODEL),
            const(1, D_MODEL), const(D_MODEL, 2 * D_GATE), const(1, 2 * D_GATE),
            const(1, D_GATE), const(1, D_GATE),
            const(N_SG, GMLP_BLOCK, GMLP_BLOCK), const(N_SG, GMLP_BLOCK, GMLP_BLOCK),
            const(N_SG, GMLP_BLOCK, 1), const(N_SG, GMLP_BLOCK, 1),
            const(D_GATE, D_MODEL), const(1, D_MODEL),
            const(1, D_MODEL), const(D_MODEL, LANES), const(D_MODEL, LANES), const(1, LANES),
        ],
        out_specs=[
            pl.BlockSpec((TM * ROW_TILES, LANES), lambda i: (i, 0)),
            const(N_DEC, D_GATE),
            pl.BlockSpec((SUBLANES, TM), lambda i: (0, i)),
            const(SUBLANES, LANES),
        ],
        out_shape=[jax.ShapeDtypeStruct((T * ROW_TILES, LANES), F32), jax.ShapeDtypeStruct((N_DEC, D_GATE), F32),
                   jax.ShapeDtypeStruct((SUBLANES, T), F32), jax.ShapeDtypeStruct((SUBLANES, LANES), F32)],
        scratch_shapes=[pltpu.VMEM((TM, D_GATE), BF16), pltpu.VMEM((SUBLANES, LANES), F32)],
        compiler_params=_cparams(),
        name="gmlp_layer",
    )(x_prompt, x_dec, nm, w_in, b_in, ln_g, ln_b, ws_p, ws_d, bs_p, bs_d, w_out, b_out,
      nf, wr_hi, (wr - wr_hi.astype(F32)).astype(BF16), br)


def _router_body(hp_ref, hd_ref, nf_ref, wrh_ref, wrl_ref, br_ref, rows_ref, info_ref, cnt_ref, carry_ref):
    _route_init(carry_ref)
    h = jnp.where(pl.program_id(0) == N_TILES - 1, hd_ref[...], hp_ref[...])
    _store_rows(rows_ref, h, TM)
    _route_tile(h, nf_ref, wrh_ref, wrl_ref, br_ref, info_ref, cnt_ref, carry_ref)


def _route_init(carry_ref):
    @pl.when(pl.program_id(0) == 0)
    def _():
        carry_ref[...] = jnp.zeros_like(carry_ref)


def _route_tile(h, nf_ref, wrh_ref, wrl_ref, br_ref, info_ref, cnt_ref, carry_ref):
    xn = _rms(h, nf_ref[...])
    xh = xn.astype(BF16)
    xl = (xn - xh.astype(F32)).astype(BF16)
    logits = _dot(xh, wrh_ref[...]) + (_dot(xl, wrh_ref[...]) + _dot(xh, wrl_ref[...])) + br_ref[...]
    lane = lax.broadcasted_iota(jnp.int32, (TM, LANES), 1).astype(F32)

    def first_max(vals):
        vmax = jnp.max(vals, axis=-1, keepdims=True)
        idx = jnp.min(jnp.where(vals == vmax, lane, float(LANES)), axis=-1, keepdims=True)
        return vmax, idx

    lg = jnp.where(lane < N_EGROUPS, logits, -jnp.inf)
    gmax, g_idx = first_max(lg)
    g_p = 1.0 / jnp.sum(jnp.exp(lg - gmax), axis=-1, keepdims=True)
    e_lo = N_EGROUPS + EXPERTS_PER_GROUP * g_idx
    le = jnp.where((lane >= e_lo) & (lane < e_lo + EXPERTS_PER_GROUP), logits, -jnp.inf)
    v1, i1 = first_max(le)
    v2, i2 = first_max(jnp.where(lane == i1, -jnp.inf, le))
    e2 = jnp.exp(v2 - v1)
    w1 = (1.0 / (1.0 + e2)) * g_p
    w2 = (e2 / (1.0 + e2)) * g_p
    a1 = i1 - e_lo
    a2 = i2 - e_lo
    lo = jnp.minimum(a1, a2)
    hi = jnp.maximum(a1, a2)
    pair = jnp.where(lo == 0.0, hi - 1.0, jnp.where(lo == 1.0, jnp.where(hi == 3.0, 3.0, 4.0), 5.0))
    ea = jnp.where(pair < 3.0, 0.0, jnp.where(pair < 5.0, 1.0, 3.0))
    ga = jnp.where(a1 == ea, w1, w2)
    gb = jnp.where(a1 == ea, w2, w1)
    bucket = g_idx * N_PAIRS + pair

    onehot = (lane == bucket).astype(F32)
    r = lax.broadcasted_iota(jnp.int32, (TM, TM), 0)
    c = lax.broadcasted_iota(jnp.int32, (TM, TM), 1)
    before = _dot((c < r).astype(BF16), onehot.astype(BF16))
    carry = carry_ref[0:1, :]
    rank = jnp.sum(onehot * (before + carry), axis=-1, keepdims=True)
    new_carry = carry + jnp.sum(onehot, axis=0, keepdims=True)
    carry_ref[...] = jnp.broadcast_to(new_carry, carry_ref.shape)
    cnt_ref[...] = jnp.broadcast_to(new_carry, cnt_ref.shape)
    info = jnp.where(lane == 0.0, bucket,
                     jnp.where(lane == 1.0, rank, jnp.where(lane == 2.0, ga, jnp.where(lane == 3.0, gb, 0.0))))
    info_ref[...] = info.T[:SUBLANES]


def _router(h_prompt, h_dec, nf, wr, br):
    const = lambda *shape: pl.BlockSpec(shape, lambda i: (0,) * len(shape))
    wr_hi = wr.astype(BF16)
    return pl.pallas_call(
        _router_body,
        grid=(N_TILES,),
        in_specs=[pl.BlockSpec((TM, D_MODEL), lambda i: (jnp.minimum(i, N_TILES - 2), 0)), const(N_DEC, D_MODEL),
                  const(1, D_MODEL), const(D_MODEL, LANES), const(D_MODEL, LANES), const(1, LANES)],
        out_specs=[pl.BlockSpec((TM * ROW_TILES, LANES), lambda i: (i, 0)),
                   pl.BlockSpec((SUBLANES, TM), lambda i: (0, i)),
                   const(SUBLANES, LANES)],
        out_shape=[jax.ShapeDtypeStruct((T * ROW_TILES, LANES), F32), jax.ShapeDtypeStruct((SUBLANES, T), F32),
                   jax.ShapeDtypeStruct((SUBLANES, LANES), F32)],
        scratch_shapes=[pltpu.VMEM((SUBLANES, LANES), F32)],
        compiler_params=_cparams(),
        name="moe_router",
    )(h_prompt, h_dec, nf, wr_hi, (wr - wr_hi.astype(F32)).astype(BF16), br)


GATHER_UNROLL = 8


def _gather_rows_start(idx_ref, base, src_ref, dst_ref, sem, n):
    def group(g, carry):
        for u in range(GATHER_UNROLL):
            r = g * GATHER_UNROLL + u
            src = pl.multiple_of(idx_ref[base + r] * ROW_TILES, ROW_TILES)
            dst = pl.multiple_of(r * ROW_TILES, ROW_TILES)
            pltpu.make_async_copy(src_ref.at[pl.ds(src, ROW_TILES), :], dst_ref.at[pl.ds(dst, ROW_TILES), :], sem).start()
        return carry

    lax.fori_loop(0, n // GATHER_UNROLL, group, 0)


def _gather_rows_wait(src_ref, dst_ref, sem, n):
    pltpu.make_async_copy(src_ref.at[pl.ds(0, n * ROW_TILES), :], dst_ref.at[pl.ds(0, n * ROW_TILES), :], sem).wait()


def _gather_tile(idx_ref, src_ref, buf, sem, n):
    i = pl.program_id(0)
    slot = lax.rem(i, 2)

    @pl.when(i == 0)
    def _():
        _gather_rows_start(idx_ref, 0, src_ref, buf.at[0], sem.at[0], n)

    @pl.when(i + 1 < pl.num_programs(0))
    def _():
        _gather_rows_start(idx_ref, (i + 1) * n, src_ref, buf.at[1 - slot], sem.at[1 - slot], n)

    _gather_rows_wait(src_ref, buf.at[slot], sem.at[slot], n)
    return buf.at[slot]


def _gather_scratch(n):
    return [pltpu.VMEM((2, n * ROW_TILES, LANES), F32), pltpu.SemaphoreType.DMA((2,))]


def _moe_body(idx_ref, ea_ref, eb_ref, cha_ref, chb_ref, valid_ref, rows_hbm, g_ref, nf_ref,
              w1a_ref, w3a_ref, w2a_ref, w1b_ref, w3b_ref, w2b_ref, out_ref,
              s1a, s3a, s2a, s1b, s3b, s2b, xbuf, xsem):
    i = pl.program_id(0)
    x_ref = _gather_tile(idx_ref, rows_hbm, xbuf, xsem, TM)

    @pl.when(cha_ref[i] == 1)
    def _():
        s1a[...] = w1a_ref[...].astype(BF16)
        s3a[...] = w3a_ref[...].astype(BF16)
        s2a[...] = w2a_ref[...].astype(BF16)

    @pl.when(chb_ref[i] == 1)
    def _():
        s1b[...] = w1b_ref[...].astype(BF16)
        s3b[...] = w3b_ref[...].astype(BF16)
        s2b[...] = w2b_ref[...].astype(BF16)

    @pl.when(valid_ref[i] == 1)
    def _():
        h = _load_rows(x_ref, TM)
        xn = _rms(h, nf_ref[...]).astype(BF16)
        g = g_ref[...]

        def ffn(w1, w3, w2):
            a = _dot(xn, w1[...])
            hdn = (a * (1.0 / (1.0 + jnp.exp(-a)))) * _dot(xn, w3[...])
            return _dot(hdn.astype(BF16), w2[...])

        y = g[:, 0:1] * ffn(s1a, s3a, s2a) + g[:, 1:2] * ffn(s1b, s3b, s2b)
        _store_rows(out_ref, h + y, TM)

    @pl.when(valid_ref[i] == 0)
    def _():
        out_ref[...] = jnp.zeros_like(out_ref)


def _moe_ffn(layer, idx_sorted, ea, eb, cha, chb, valid, rows, gates, nf, w1, w3, w2):
    wa = lambda shape: pl.BlockSpec((None, None) + shape, lambda i, ix, ea, eb, ca, cb, va: (layer, ea[i], 0, 0))
    wb = lambda shape: pl.BlockSpec((None, None) + shape, lambda i, ix, ea, eb, ca, cb, va: (layer, eb[i], 0, 0))
    up, down = (D_MODEL, D_EXPERT), (D_EXPERT, D_MODEL)
    return pl.pallas_call(
        _moe_body,
        grid_spec=pltpu.PrefetchScalarGridSpec(
            num_scalar_prefetch=6,
            grid=(MOE_TILES,),
            in_specs=[
                pl.BlockSpec(memory_space=pl.ANY),
                pl.BlockSpec((TM, 2), lambda i, *_: (i, 0)),
                pl.BlockSpec((1, D_MODEL), lambda i, *_: (0, 0)),
                wa(up), wa(up), wa(down), wb(up), wb(up), wb(down),
            ],
            out_specs=pl.BlockSpec((TM * ROW_TILES, LANES), lambda i, *_: (i, 0)),
            scratch_shapes=[pltpu.VMEM(up, BF16), pltpu.VMEM(up, BF16), pltpu.VMEM(down, BF16),
                            pltpu.VMEM(up, BF16), pltpu.VMEM(up, BF16), pltpu.VMEM(down, BF16)]
            + _gather_scratch(TM),
        ),
        out_shape=jax.ShapeDtypeStruct((P_ROWS * ROW_TILES, LANES), F32),
        compiler_params=_cparams(),
        name="moe_ffn",
    )(idx_sorted, ea, eb, cha, chb, valid, rows, gates, nf, w1, w3, w2, w1, w3, w2)


def _router_weights(layer, w_group, b_group, w_expert, b_expert):
    wr = jnp.zeros((D_MODEL, LANES), F32)
    wr = wr.at[:, :N_EGROUPS].set(w_group[layer]).at[:, N_EGROUPS:N_EGROUPS + N_EXPERTS].set(w_expert[layer])
    br = jnp.zeros((1, LANES), F32)
    br = br.at[0, :N_EGROUPS].set(b_group[layer]).at[0, N_EGROUPS:N_EGROUPS + N_EXPERTS].set(b_expert[layer])
    return wr, br


def _moe_layer(layer, rows, info, cnt, nf, w1, w3, w2):
    bucket = info[0].astype(jnp.int32)
    rank = info[1].astype(jnp.int32)
    counts = cnt[0, :N_BUCKETS].astype(jnp.int32)
    n_tiles = (counts + TM - 1) // TM
    tile_end = jnp.cumsum(n_tiles)
    tile_start = tile_end - n_tiles
    start_of = jnp.sum(jnp.where(bucket[:, None] == jnp.arange(N_BUCKETS)[None, :], tile_start[None, :], 0), axis=1)
    pos = start_of * TM + rank
    packed = jnp.stack([jnp.arange(T, dtype=F32), info[2], info[3]], axis=1)
    base = jnp.zeros((P_ROWS, 3), F32).at[:, 0].set((jnp.arange(P_ROWS) % T).astype(F32))
    packed = base.at[pos].set(packed, unique_indices=True, indices_are_sorted=False, mode="promise_in_bounds")
    idx_sorted = packed[:, 0].astype(jnp.int32)
    gates = packed[:, 1:3]
    total = tile_end[-1]
    j = jnp.minimum(jnp.arange(MOE_TILES), total - 1)
    tb = jnp.sum((j[:, None] >= tile_end[None, :]).astype(jnp.int32), axis=1)
    grp, pair = tb // N_PAIRS, tb % N_PAIRS
    ea = (grp * EXPERTS_PER_GROUP + jnp.asarray(PAIR_A, jnp.int32)[pair]).astype(jnp.int32)
    eb = (grp * EXPERTS_PER_GROUP + jnp.asarray(PAIR_B, jnp.int32)[pair]).astype(jnp.int32)
    first = jnp.arange(MOE_TILES) == 0
    cha = (first | (ea != jnp.roll(ea, 1))).astype(jnp.int32)
    chb = (first | (eb != jnp.roll(eb, 1))).astype(jnp.int32)
    valid = (jnp.arange(MOE_TILES) < total).astype(jnp.int32)

    return _moe_ffn(layer, idx_sorted, ea, eb, cha, chb, valid, rows, gates, nf, w1, w3, w2), pos


VT_ROWS = LANES + 16


def _seg_matrix():
    lane = np.arange(LANES)
    seg = np.where(lane < NOPE_DIM, 0, np.where(lane < NOPE_DIM + ROPE_DIM, 1, 2))
    return jnp.asarray(seg[:, None] == seg[None, :], BF16)


def _seg_count():
    lane = np.arange(LANES)
    return jnp.asarray(np.where(lane < NOPE_DIM, 1.0 / NOPE_DIM, 1.0 / ROPE_DIM), F32).reshape(1, LANES)


def _rope_swap(x):
    lane = lax.broadcasted_iota(jnp.int32, x.shape, 1)
    return jnp.where(lane < ROPE_LO + ROPE_HALF, pltpu.roll(x, LANES - ROPE_HALF, 1), pltpu.roll(x, ROPE_HALF, 1))


def _expand_k(cb, kr, wuk_ref, seg, cnt, kg_ref, k_ref):
    kn = _dot(cb, wuk_ref[...])
    heads = [kn[:, hh * HEAD_W:(hh + 1) * HEAD_W] for hh in range(N_HEADS)]
    ms = [_split_dot(x * x, seg) * cnt for x in heads]
    for hh in range(N_HEADS):
        k_ref[:, hh * HEAD_W:(hh + 1) * HEAD_W] = (heads[hh] * lax.rsqrt(ms[hh] + EPS) * kg_ref[...] + kr).astype(BF16)


def _mla_proj_body(pos_ref, sorted_hbm, cos_ref, sin_ref, cost_ref, sint_ref, kvn_ref, wdkv_ref, kvan_ref, krg_ref,
                   wuk_ref, wuv_ref, wuvt_ref, kg_ref, nmq_ref, wdq_ref, qan_ref, wuqt_ref, qg_ref, seg_ref, cnt_ref,
                   rows_ref, ckvp_ref, ckvd_ref, krp_ref, krd_ref, k_ref, vdec_ref, vt_ref, qt_ref, qdec_ref,
                   xbuf, xsem):
    is_dec = pl.program_id(0) == N_TILES - 1
    x_ref = _gather_tile(pos_ref, sorted_hbm, xbuf, xsem, TM)
    rows_ref[...] = x_ref[...]
    h = _load_rows(x_ref, TM)
    c = _dot(_rms(h, kvn_ref[...]).astype(BF16), wdkv_ref[...])
    ckv = _rms(c[:, :KV_RANK], kvan_ref[...])
    kr = c[:, KV_RANK:]
    kr = kr * lax.rsqrt(jnp.sum(kr * kr, axis=-1, keepdims=True) * (1.0 / ROPE_DIM) + EPS) * krg_ref[...]
    kr = kr * cos_ref[...] + _rope_swap(kr) * sin_ref[...]
    cb = ckv.astype(BF16)
    _expand_k(cb, kr, wuk_ref, seg_ref[...], cnt_ref[...], kg_ref, k_ref)
    vt = _dot(wuvt_ref[...], ckv.T.astype(BF16)).astype(BF16)
    for pr in range(N_HEADS // 2):
        vt_ref[0, pr * VT_ROWS:pr * VT_ROWS + LANES, :] = vt[pr * LANES:(pr + 1) * LANES]
        vt_ref[0, pr * VT_ROWS + LANES:(pr + 1) * VT_ROWS, :] = jnp.ones((VT_ROWS - LANES, TM), BF16)
    cq = _rms(_dot(_rms(h, nmq_ref[...]).astype(BF16), wdq_ref[...]), qan_ref[...])
    qt = _dot(wuqt_ref[...], cq.T.astype(BF16))
    cost, sint, qg = cost_ref[0], sint_ref[0], qg_ref[...]
    for hh in range(N_HEADS):
        x = qt[hh * HEAD_W:(hh + 1) * HEAD_W, :]
        xn, xr = x[:NOPE_DIM], x[ROPE_LO:ROPE_LO + ROPE_DIM]
        xn = xn * lax.rsqrt(jnp.mean(xn * xn, axis=0, keepdims=True) + EPS) * qg[:NOPE_DIM]
        xr = xr * lax.rsqrt(jnp.mean(xr * xr, axis=0, keepdims=True) + EPS) * qg[ROPE_LO:ROPE_LO + ROPE_DIM]
        x1, x2 = xr[:ROPE_HALF], xr[ROPE_HALF:]
        qh = jnp.concatenate([xn, x1 * cost - x2 * sint, x1 * sint + x2 * cost,
                              jnp.zeros((HEAD_W - NOPE_DIM - ROPE_DIM, TM), F32)], axis=0) * Q_SCALE
        qt_ref[0, hh * HEAD_W:(hh + 1) * HEAD_W, :] = qh.astype(BF16)

    @pl.when(jnp.logical_not(is_dec))
    def _():
        ckvp_ref[...] = ckv
        krp_ref[...] = kr[:, ROPE_LO:ROPE_LO + ROPE_DIM]

    @pl.when(is_dec)
    def _():
        ckvd_ref[...] = ckv
        krd_ref[...] = kr[:, ROPE_LO:ROPE_LO + ROPE_DIM]
        vdec_ref[...] = _dot(cb, wuv_ref[...]).astype(BF16)
        for hh in range(N_HEADS):
            rows = slice(hh * HEAD_W, (hh + 1) * HEAD_W)
            qdec_ref[:, rows] = qt_ref[0, rows, :].astype(F32).T.astype(BF16)


def _mla_proj(pos, sorted_rows, cos_t, sin_t, cos_tt, sin_tt, kvn, wdkv, kvan, krg, wuk, wuv, kg, nmq, wdq, qan, wuq, qg):
    const = lambda *shape: pl.BlockSpec(shape, lambda i, p: (0,) * len(shape))
    tab_tile = lambda i: jnp.where(i < N_PROMPT // TM, i % (SEQ // TM), SEQ // TM)
    tab = pl.BlockSpec((TM, LANES), lambda i, p: (tab_tile(i), 0))
    tab_t = pl.BlockSpec((1, ROPE_HALF, TM), lambda i, p: (tab_tile(i), 0, 0))
    row = lambda w: pl.BlockSpec((TM, w), lambda i, p: (i, 0))
    prow = lambda w: pl.BlockSpec((TM, w), lambda i, p: (jnp.minimum(i, N_TILES - 2), 0))
    return pl.pallas_call(
        _mla_proj_body,
        grid_spec=pltpu.PrefetchScalarGridSpec(
            num_scalar_prefetch=1,
            grid=(N_TILES,),
            in_specs=[
                pl.BlockSpec(memory_space=pl.ANY), tab, tab, tab_t, tab_t,
                const(1, D_MODEL), const(D_MODEL, KV_RANK + LANES), const(1, KV_RANK), const(1, LANES),
                const(KV_RANK, N_HEADS * HEAD_W), const(KV_RANK, N_HEADS * V_DIM), const(N_HEADS * V_DIM, KV_RANK),
                const(1, LANES),
                const(1, D_MODEL), const(D_MODEL, Q_RANK), const(1, Q_RANK), const(N_HEADS * HEAD_W, Q_RANK),
                const(HEAD_W, TM), const(LANES, LANES), const(1, LANES),
            ],
            out_specs=[pl.BlockSpec((TM * ROW_TILES, LANES), lambda i, p: (i, 0)),
                       prow(KV_RANK), const(N_DEC, KV_RANK), prow(ROPE_DIM), const(N_DEC, ROPE_DIM),
                       row(N_HEADS * HEAD_W), const(N_DEC, N_HEADS * V_DIM),
                       pl.BlockSpec((1, N_HEADS // 2 * VT_ROWS, TM), lambda i, p: (i, 0, 0)),
                       pl.BlockSpec((1, N_HEADS * HEAD_W, TM), lambda i, p: (i, 0, 0)),
                       const(N_DEC, N_HEADS * HEAD_W)],
            scratch_shapes=_gather_scratch(TM),
        ),
        out_shape=[
            jax.ShapeDtypeStruct((T * ROW_TILES, LANES), F32),
            jax.ShapeDtypeStruct((N_PROMPT, KV_RANK), F32), jax.ShapeDtypeStruct((N_DEC, KV_RANK), F32),
            jax.ShapeDtypeStruct((N_PROMPT, ROPE_DIM), F32), jax.ShapeDtypeStruct((N_DEC, ROPE_DIM), F32),
            jax.ShapeDtypeStruct((T, N_HEADS * HEAD_W), BF16), jax.ShapeDtypeStruct((N_DEC, N_HEADS * V_DIM), BF16),
            jax.ShapeDtypeStruct((N_TILES, N_HEADS // 2 * VT_ROWS, TM), BF16),
            jax.ShapeDtypeStruct((N_TILES, N_HEADS * HEAD_W, TM), BF16),
            jax.ShapeDtypeStruct((N_DEC, N_HEADS * HEAD_W), BF16),
        ],
        compiler_params=_cparams(),
        name="mla_proj",
    )(pos, sorted_rows, cos_t, sin_t, cos_tt, sin_tt, kvn, wdkv, kvan, krg, wuk, wuv, wuv.T, kg, nmq, wdq, qan,
      wuq.T, jnp.broadcast_to(qg.reshape(HEAD_W, 1), (HEAD_W, TM)), _seg_matrix(), _seg_count())


def _cache_kv_body(ckv_ref, kr_ref, place_ref, wuk_ref, wuv_ref, kg_ref, seg_ref, cnt_ref, k_ref, v_ref):
    kr = _dot(kr_ref[...].astype(BF16), place_ref[...])
    cb = ckv_ref[...].astype(BF16)
    _expand_k(cb, kr, wuk_ref, seg_ref[...], cnt_ref[...], kg_ref, k_ref)
    v_ref[...] = _dot(cb, wuv_ref[...]).astype(BF16)


CACHE_ROWS = 1024
assert PAST_LEN % CACHE_ROWS == 0


def _cache_kv(ckv, kr, wuk, wuv, kg):
    n = ckv.shape[0]
    place = jnp.asarray(np.arange(ROPE_DIM)[:, None] + ROPE_LO == np.arange(LANES)[None, :], BF16)
    const = lambda *shape: pl.BlockSpec(shape, lambda i: (0,) * len(shape))
    row = lambda w: pl.BlockSpec((CACHE_ROWS, w), lambda i: (i, 0))
    return pl.pallas_call(
        _cache_kv_body,
        grid=(n // CACHE_ROWS,),
        in_specs=[row(KV_RANK), row(ROPE_DIM), const(ROPE_DIM, LANES), const(KV_RANK, N_HEADS * HEAD_W),
                  const(KV_RANK, N_HEADS * V_DIM), const(1, LANES), const(LANES, LANES), const(1, LANES)],
        out_specs=[row(N_HEADS * HEAD_W), row(N_HEADS * V_DIM)],
        out_shape=[jax.ShapeDtypeStruct((n, N_HEADS * HEAD_W), BF16), jax.ShapeDtypeStruct((n, N_HEADS * V_DIM), BF16)],
        compiler_params=_cparams(),
        name="cache_kv",
    )(ckv, kr, place, wuk, wuv, kg, _seg_matrix(), _seg_count())


TQ = 256
TK = 256
assert TQ == TK and TQ % CHUNK == 0
SCORE_LOOKAHEAD = 4


def _qk(q, k):
    return lax.dot_general(q, k, (((1,), (1,)), ((), ())), preferred_element_type=F32)


def _merge_heads(o_ref, outs, rows):
    lane = lax.broadcasted_iota(jnp.int32, (rows, LANES), 1)
    for pr in range(N_HEADS // 2):
        o_ref[:, pr * LANES:(pr + 1) * LANES] = jnp.where(lane < V_DIM, outs[2 * pr], outs[2 * pr + 1]).astype(BF16)


def _prompt_attn_body(qt_ref, k_ref, vt_ref, rows_ref, wo_ref, out_ref, m_scr, acc_scr):
    qi = pl.program_id(1)
    m_scr[...] = jnp.full(m_scr.shape, NEG, F32)
    acc_scr[...] = jnp.zeros(acc_scr.shape, F32)

    def all_heads(j, mask):
        ks = pl.ds(pl.multiple_of(j * TK, TK), TK)

        def scores(hh):
            hcols = slice(hh * HEAD_W, (hh + 1) * HEAD_W)
            return _dot(k_ref[ks, hcols], qt_ref[0, hcols, :])

        ahead = [scores(hh) for hh in range(SCORE_LOOKAHEAD)]
        for hh in range(N_HEADS):
            vrows = slice((hh // 2) * VT_ROWS, (hh // 2 + 1) * VT_ROWS)
            s = ahead.pop(0)
            if hh + SCORE_LOOKAHEAD < N_HEADS:
                ahead.append(scores(hh + SCORE_LOOKAHEAD))
            if mask is not None:
                s = jnp.where(mask, s, NEG)
            m_old = m_scr[hh]
            m_new = jnp.maximum(m_old, jnp.max(s, axis=0, keepdims=True))
            p = jnp.exp2(s - m_new).astype(BF16)
            m_scr[hh] = m_new
            acc_scr[hh] = jnp.exp2(m_old - m_new) * acc_scr[hh] + _dot(vt_ref[j, vrows, :], p)

    def step(j, carry):
        all_heads(j, None)
        return carry

    lax.fori_loop(0, qi, step, 0)
    kc = lax.broadcasted_iota(jnp.int32, (TK, TQ), 0) // CHUNK
    qc = lax.broadcasted_iota(jnp.int32, (TK, TQ), 1) // CHUNK
    all_heads(qi, kc <= qc)
    row = lax.broadcasted_iota(jnp.int32, (LANES, TQ), 0)
    pairs = []
    for pr in range(N_HEADS // 2):
        even = acc_scr[2 * pr, :LANES] / acc_scr[2 * pr, LANES:LANES + 1]
        odd = acc_scr[2 * pr + 1, :LANES] / acc_scr[2 * pr + 1, LANES:LANES + 1]
        pairs.append(jnp.where(row < V_DIM, even, odd))
    o = jnp.concatenate(pairs, axis=0).T.astype(BF16)
    out_ref[...] = _load_rows(rows_ref, TQ) + _dot(o, wo_ref[...])


def _prompt_attn(qt, k, vt, rows, wo):
    nq = SEQ // TQ
    return pl.pallas_call(
        _prompt_attn_body,
        grid=(BATCH, nq),
        in_specs=[
            pl.BlockSpec((1, N_HEADS * HEAD_W, TQ), lambda b, i: (b * nq + i, 0, 0)),
            pl.BlockSpec((SEQ, N_HEADS * HEAD_W), lambda b, i: (b, 0)),
            pl.BlockSpec((SEQ // TK, N_HEADS // 2 * VT_ROWS, TK), lambda b, i: (b, 0, 0)),
            pl.BlockSpec((TQ * ROW_TILES, LANES), lambda b, i: (b * nq + i, 0)),
            pl.BlockSpec((N_HEADS * V_DIM, D_MODEL), lambda b, i: (0, 0)),
        ],
        out_specs=pl.BlockSpec((TQ, D_MODEL), lambda b, i: (b * nq + i, 0)),
        out_shape=jax.ShapeDtypeStruct((N_PROMPT, D_MODEL), F32),
        scratch_shapes=[pltpu.VMEM((N_HEADS, 1, TQ), F32), pltpu.VMEM((N_HEADS, VT_ROWS, TQ), F32)],
        compiler_params=_cparams(2),
        name="prompt_attn",
    )(qt, k, vt, rows, wo)


def _sample_attn_body(q_ref, kc_ref, vc_ref, kn_ref, vn_ref, rows_ref, wo_ref, out_ref, o_scr):
    outs = []
    for hh in range(N_HEADS):
        hcols = slice(hh * HEAD_W, (hh + 1) * HEAD_W)
        vcols = slice((hh // 2) * LANES, (hh // 2 + 1) * LANES)
        q = q_ref[:, hcols]
        sc = _qk(q, kc_ref[:, hcols])
        sn = _qk(q, kn_ref[:, hcols])
        m = jnp.maximum(jnp.max(sc, axis=-1, keepdims=True), jnp.max(sn, axis=-1, keepdims=True))
        pc = jnp.exp2(sc - m)
        pn = jnp.exp2(sn - m)
        l = jnp.sum(pc, axis=-1, keepdims=True) + jnp.sum(pn, axis=-1, keepdims=True)
        acc = _dot(pc.astype(BF16), vc_ref[:, vcols]) + _dot(pn.astype(BF16), vn_ref[:, vcols])
        outs.append(acc / l)
    _merge_heads(o_scr, outs, DEC_SEQ)
    out_ref[...] = _load_rows(rows_ref, DEC_SEQ) + _dot(o_scr[...], wo_ref[...])


def _sample_attn(q, kc, vc, kn, vn, rows, wo):
    off = N_PROMPT // DEC_SEQ
    return pl.pallas_call(
        _sample_attn_body,
        grid=(DEC_BATCH,),
        in_specs=[
            pl.BlockSpec((DEC_SEQ, N_HEADS * HEAD_W), lambda b: (b, 0)),
            pl.BlockSpec((PAST_LEN, N_HEADS * HEAD_W), lambda b: (b, 0)),
            pl.BlockSpec((PAST_LEN, N_HEADS * V_DIM), lambda b: (b, 0)),
            pl.BlockSpec((DEC_SEQ, N_HEADS * HEAD_W), lambda b: (off + b, 0)),
            pl.BlockSpec((DEC_SEQ, N_HEADS * V_DIM), lambda b: (b, 0)),
            pl.BlockSpec((DEC_SEQ * ROW_TILES, LANES), lambda b: (off + b, 0)),
            pl.BlockSpec((N_HEADS * V_DIM, D_MODEL), lambda b: (0, 0)),
        ],
        out_specs=pl.BlockSpec((DEC_SEQ, D_MODEL), lambda b: (b, 0)),
        out_shape=jax.ShapeDtypeStruct((N_DEC, D_MODEL), F32),
        scratch_shapes=[pltpu.VMEM((DEC_SEQ, N_HEADS * V_DIM), BF16)],
        compiler_params=_cparams(),
        name="sample_attn",
    )(q, kc, vc, kn, vn, rows, wo)


def _finish_body(pos_ref, sorted_hbm, yp_ref, ys_ref, xbuf, xsem):
    i = pl.program_id(0)
    y = _load_rows(_gather_tile(pos_ref, sorted_hbm, xbuf, xsem, TM), TM)

    @pl.when(i < N_TILES - 1)
    def _():
        yp_ref[...] = y

    @pl.when(i == N_TILES - 1)
    def _():
        ys_ref[...] = y


def _finish(pos, sorted_rows):
    return pl.pallas_call(
        _finish_body,
        grid_spec=pltpu.PrefetchScalarGridSpec(
            num_scalar_prefetch=1,
            grid=(N_TILES,),
            in_specs=[pl.BlockSpec(memory_space=pl.ANY)],
            out_specs=[pl.BlockSpec((TM, D_MODEL), lambda i, p: (jnp.minimum(i, N_TILES - 2), 0)),
                       pl.BlockSpec((N_DEC, D_MODEL), lambda i, p: (0, 0))],
            scratch_shapes=_gather_scratch(TM),
        ),
        out_shape=[jax.ShapeDtypeStruct((N_PROMPT, D_MODEL), F32), jax.ShapeDtypeStruct((N_DEC, D_MODEL), F32)],
        compiler_params=_cparams(),
        name="finish",
    )(pos, sorted_rows)


def _rope_tables():
    half = ROPE_DIM // 2
    inv_freq = ROPE_THETA ** (-jnp.arange(half, dtype=F32) / half)
    dec_pos = PAST_LEN + jnp.tile(jnp.arange(DEC_SEQ, dtype=jnp.int32), DEC_BATCH)
    pos = jnp.concatenate([jnp.arange(SEQ, dtype=jnp.int32), dec_pos])
    ang = pos.astype(F32)[:, None] * inv_freq[None, :]
    cos, sin = jnp.cos(ang), jnp.sin(ang)
    n = pos.shape[0]
    cos_t = jnp.ones((n, LANES), F32).at[:, ROPE_LO:ROPE_LO + ROPE_DIM].set(jnp.concatenate([cos, cos], axis=1))
    sin_t = jnp.zeros((n, LANES), F32).at[:, ROPE_LO:ROPE_LO + ROPE_DIM].set(jnp.concatenate([-sin, sin], axis=1))
    to_tiles = lambda a: a.reshape(n // TM, TM, half).transpose(0, 2, 1)
    return cos_t, sin_t, to_tiles(cos), to_tiles(sin)


def _on_lanes(vec, lo):
    return jnp.zeros((1, LANES), F32).at[0, lo:lo + vec.shape[0]].set(vec)


def kernel(x_prompt, x_sample, cache_ckv, cache_krope, norm_mix, norm_ffn, gm_w_in, gm_b_in, gm_ln_g, gm_ln_b, gm_w_s, gm_b_s, gm_w_out, gm_b_out, kv_norm, w_dkv, kv_a_norm, k_rope_norm, w_uk, w_uv, k_nope_norm, w_dq, q_a_norm, w_uq, q_nope_norm, q_rope_norm, w_o, moe_w_group, moe_b_group, moe_w_expert, moe_b_expert, moe_w1, moe_w3, moe_w2):
    nf0, nf1 = norm_ffn[0].reshape(1, D_MODEL), norm_ffn[1].reshape(1, D_MODEL)

    idx = np.arange(GMLP_BLOCK)
    allowed = (idx[None, :] // CHUNK) <= (idx[:, None] // CHUNK)
    ws_p = jnp.where(allowed[None], gm_w_s[0], 0.0).astype(BF16)
    same_seq = (idx[None, :] // DEC_SEQ) == (idx[:, None] // DEC_SEQ)
    ws_d = jnp.where(same_seq[None], jnp.tile(gm_w_s[0][:, :DEC_SEQ, :DEC_SEQ], (1, GMLP_BLOCK // DEC_SEQ, GMLP_BLOCK // DEC_SEQ)), 0.0).astype(BF16)
    bs_p = gm_b_s[0][:, :, None]
    bs_d = jnp.tile(gm_b_s[0][:, :DEC_SEQ], (1, GMLP_BLOCK // DEC_SEQ))[:, :, None]
    rows, v_rows, info, cnt = _gmlp_layer(
        x_prompt.reshape(N_PROMPT, D_MODEL), x_sample.reshape(N_DEC, D_MODEL),
        norm_mix[0].reshape(1, -1), gm_w_in[0].astype(BF16), gm_b_in[0].reshape(1, -1),
        gm_ln_g[0].reshape(1, -1), gm_ln_b[0].reshape(1, -1), ws_p, ws_d, bs_p, bs_d,
        gm_w_out[0].astype(BF16), gm_b_out[0].reshape(1, -1),
        nf0, *_router_weights(0, moe_w_group, moe_b_group, moe_w_expert, moe_b_expert))
    sorted_rows, pos = _moe_layer(0, rows, info, cnt, nf0, moe_w1, moe_w3, moe_w2)

    cos_t, sin_t, cos_tt, sin_tt = _rope_tables()
    wdkv = jnp.zeros((D_MODEL, KV_RANK + LANES), F32).at[:, :KV_RANK].set(w_dkv[:, :KV_RANK])
    wdkv = wdkv.at[:, KV_RANK + ROPE_LO:KV_RANK + ROPE_LO + ROPE_DIM].set(w_dkv[:, KV_RANK:]).astype(BF16)
    wuk = jnp.zeros((KV_RANK, N_HEADS, HEAD_W), F32).at[:, :, :NOPE_DIM].set(w_uk).reshape(KV_RANK, -1).astype(BF16)
    wuv = w_uv.reshape(KV_RANK, -1).astype(BF16)
    wuq = jnp.zeros((Q_RANK, N_HEADS, HEAD_W), F32).at[:, :, :NOPE_DIM + ROPE_DIM].set(w_uq[0]).reshape(Q_RANK, -1).astype(BF16)
    kg = _on_lanes(k_nope_norm, 0)
    krg = _on_lanes(k_rope_norm, ROPE_LO)
    qg = _on_lanes(jnp.concatenate([q_nope_norm[0], q_rope_norm[0]]), 0)
    rows, ckv_p, ckv_d, krope_p, krope_d, k_new, v_dec, vt_new, qt, q_dec = _mla_proj(
        pos, sorted_rows, cos_t, sin_t, cos_tt, sin_tt, kv_norm.reshape(1, -1), wdkv, kv_a_norm.reshape(1, -1), krg, wuk, wuv, kg,
        norm_mix[1].reshape(1, -1), w_dq[0].astype(BF16), q_a_norm[0].reshape(1, -1), wuq, qg)
    k_cache, v_cache = _cache_kv(cache_ckv.reshape(-1, KV_RANK), cache_krope.reshape(-1, ROPE_DIM), wuk, wuv, kg)

    wo = w_o[0].astype(BF16)
    h_prompt = _prompt_attn(qt, k_new, vt_new, rows, wo)
    h_dec = _sample_attn(q_dec, k_cache, v_cache, k_new, v_dec, rows, wo)
    rows, info, cnt = _router(h_prompt, h_dec, nf1,
                              *_router_weights(1, moe_w_group, moe_b_group, moe_w_expert, moe_b_expert))
    sorted_rows, pos = _moe_layer(1, rows, info, cnt, nf1, moe_w1, moe_w3, moe_w2)
    y_prompt, y_sample = _finish(pos, sorted_rows)

    return (y_prompt.reshape(BATCH, SEQ, D_MODEL), y_sample.reshape(DEC_BATCH, DEC_SEQ, D_MODEL),
            ckv_p.reshape(BATCH, SEQ, KV_RANK), krope_p.reshape(BATCH, SEQ, ROPE_DIM),
            ckv_d.reshape(DEC_BATCH, DEC_SEQ, KV_RANK), krope_d.reshape(DEC_BATCH, DEC_SEQ, ROPE_DIM),
            v_rows.reshape(1, DEC_BATCH, DEC_SEQ, D_GATE))
```

```python
import functools

import jax
import jax.numpy as jnp
import numpy as np
from jax import lax
from jax.experimental import pallas as pl
from jax.experimental.pallas import tpu as pltpu

F32 = jnp.float32
BF16 = jnp.bfloat16

D_MODEL = 1024
BATCH = 8
SEQ = 2048
DEC_BATCH = 16
DEC_SEQ = 16
PAST_LEN = 2048
CHUNK = 64
GMLP_BLOCK = 128
D_GATE = 2 * D_MODEL
N_SG = 8
SG_W = D_GATE // N_SG
N_HEADS = 8
NOPE_DIM = 64
ROPE_DIM = 32
V_DIM = 64
Q_RANK = 384
KV_RANK = 256
ROPE_THETA = 10000.0
SCALE = (NOPE_DIM + ROPE_DIM) ** -0.5
Q_SCALE = SCALE * float(np.log2(np.e))
N_EGROUPS = 4
EXPERTS_PER_GROUP = 4
N_EXPERTS = N_EGROUPS * EXPERTS_PER_GROUP
D_EXPERT = 512
EPS = 1e-6
NEG = -1e30

LANES = 128
SUBLANES = 8
ROW_TILES = D_MODEL // LANES
assert ROW_TILES == SUBLANES

N_PROMPT = BATCH * SEQ
N_DEC = DEC_BATCH * DEC_SEQ
T = N_PROMPT + N_DEC
TM = 256
assert N_PROMPT % TM == 0 and N_DEC == TM
N_TILES = T // TM
HEAD_W = LANES
ROPE_LO = NOPE_DIM
ROPE_HALF = ROPE_DIM // 2

PAIR_A = (0, 0, 0, 1, 1, 3)
PAIR_B = (1, 2, 3, 3, 2, 2)
N_PAIRS = 6
N_BUCKETS = N_EGROUPS * N_PAIRS
MOE_TILES = (T + N_BUCKETS * (TM - 1) + TM - 1) // TM
P_ROWS = MOE_TILES * TM

VMEM_LIMIT = 56 * 1024 * 1024


def _cparams(n_axes=1, vmem=VMEM_LIMIT):
    return pltpu.CompilerParams(dimension_semantics=("arbitrary",) * n_axes, vmem_limit_bytes=vmem)


def _rms(x, g):
    return x * lax.rsqrt(jnp.mean(x * x, axis=-1, keepdims=True) + EPS) * g


def _load_rows(ref, n):
    return jnp.concatenate([ref[pl.ds(s, n, stride=ROW_TILES), :] for s in range(ROW_TILES)], axis=1)


def _store_rows(ref, x, n):
    for s in range(ROW_TILES):
        ref[pl.ds(s, n, stride=ROW_TILES), :] = x[:, s * LANES:(s + 1) * LANES]


def _dot(a, b):
    return jnp.dot(a, b, preferred_element_type=F32)


def _split_dot(x, m):
    hi = x.astype(BF16)
    lo = (x - hi.astype(F32)).astype(BF16)
    return _dot(hi, m) + _dot(lo, m)


def _gmlp_body(xp_ref, xd_ref, nm_ref, win_ref, bin_ref, lng_ref, lnb_ref, wsp_ref, wsd_ref, bsp_ref, bsd_ref,
               wout_ref, bout_ref, nf_ref, wrh_ref, wrl_ref, br_ref,
               rows_ref, v_ref, info_ref, cnt_ref, gated_ref, carry_ref):
    i = pl.program_id(0)
    is_dec = i == N_TILES - 1
    _route_init(carry_ref)
    x = jnp.where(is_dec, xd_ref[...], xp_ref[...])
    xn = _rms(x, nm_ref[...]).astype(BF16)
    z = _dot(xn, win_ref[...]) + bin_ref[...]
    z = z * (0.5 * (1.0 + jnp.tanh(np.float32(np.sqrt(2.0 / np.pi)) * (z + 0.044715 * (z * z * z)))))
    u = z[:, :D_GATE]
    v = z[:, D_GATE:]
    mu = jnp.mean(v, axis=-1, keepdims=True)
    vc = v - mu
    var = jnp.mean(vc * vc, axis=-1, keepdims=True)
    v = vc * lax.rsqrt(var + EPS) * lng_ref[...] + lnb_ref[...]

    v_ref[...] = v
    vb = v.astype(BF16)
    for g in range(N_SG):
        ws = jnp.where(is_dec, wsd_ref[g], wsp_ref[g])
        bs = jnp.where(is_dec, bsd_ref[g], bsp_ref[g])
        for b in range(TM // GMLP_BLOCK):
            rows = slice(b * GMLP_BLOCK, (b + 1) * GMLP_BLOCK)
            cols = slice(g * SG_W, (g + 1) * SG_W)
            s = _dot(ws, vb[rows, cols]) + bs
            gated_ref[rows, cols] = (u[rows, cols] * s).astype(BF16)
    h = x + _dot(gated_ref[...], wout_ref[...]) + bout_ref[...]
    _store_rows(rows_ref, h, TM)
    _route_tile(h, nf_ref, wrh_ref, wrl_ref, br_ref, info_ref, cnt_ref, carry_ref)


def _gmlp_layer(x_prompt, x_dec, nm, w_in, b_in, ln_g, ln_b, ws_p, ws_d, bs_p, bs_d, w_out, b_out, nf, wr, br):
    const = lambda *shape: pl.BlockSpec(shape, lambda i: (0,) * len(shape))
    wr_hi = wr.astype(BF16)
    return pl.pallas_call(
        _gmlp_body,
        grid=(N_TILES,),
        in_specs=[
            pl.BlockSpec((TM, D_MODEL), lambda i: (jnp.minimum(i, N_TILES - 2), 0)), const(N_DEC, D_MODEL),
            const(1, D_MODEL), const(D_MODEL, 2 * D_GATE), const(1, 2 * D_GATE),
            const(1, D_GATE), const(1, D_GATE),
            const(N_SG, GMLP_BLOCK, GMLP_BLOCK), const(N_SG, GMLP_BLOCK, GMLP_BLOCK),
            const(N_SG, GMLP_BLOCK, 1), const(N_SG, GMLP_BLOCK, 1),
            const(D_GATE, D_MODEL), const(1, D_MODEL),
            const(1, D_MODEL), const(D_MODEL, LANES), const(D_MODEL, LANES), const(1, LANES),
        ],
        out_specs=[
            pl.BlockSpec((TM * ROW_TILES, LANES), lambda i: (i, 0)),
            const(N_DEC, D_GATE),
            pl.BlockSpec((SUBLANES, TM), lambda i: (0, i)),
            const(SUBLANES, LANES),
        ],
        out_shape=[jax.ShapeDtypeStruct((T * ROW_TILES, LANES), F32), jax.ShapeDtypeStruct((N_DEC, D_GATE), F32),
                   jax.ShapeDtypeStruct((SUBLANES, T), F32), jax.ShapeDtypeStruct((SUBLANES, LANES), F32)],
        scratch_shapes=[pltpu.VMEM((TM, D_GATE), BF16), pltpu.VMEM((SUBLANES, LANES), F32)],
        compiler_params=_cparams(),
        name="gmlp_layer",
    )(x_prompt, x_dec, nm, w_in, b_in, ln_g, ln_b, ws_p, ws_d, bs_p, bs_d, w_out, b_out,
      nf, wr_hi, (wr - wr_hi.astype(F32)).astype(BF16), br)


def _router_body(hp_ref, hd_ref, nf_ref, wrh_ref, wrl_ref, br_ref, rows_ref, info_ref, cnt_ref, carry_ref):
    _route_init(carry_ref)
    h = jnp.where(pl.program_id(0) == N_TILES - 1, hd_ref[...], hp_ref[...])
    _store_rows(rows_ref, h, TM)
    _route_tile(h, nf_ref, wrh_ref, wrl_ref, br_ref, info_ref, cnt_ref, carry_ref)


def _route_init(carry_ref):
    @pl.when(pl.program_id(0) == 0)
    def _():
        carry_ref[...] = jnp.zeros_like(carry_ref)


def _route_tile(h, nf_ref, wrh_ref, wrl_ref, br_ref, info_ref, cnt_ref, carry_ref):
    xn = _rms(h, nf_ref[...])
    xh = xn.astype(BF16)
    xl = (xn - xh.astype(F32)).astype(BF16)
    logits = _dot(xh, wrh_ref[...]) + (_dot(xl, wrh_ref[...]) + _dot(xh, wrl_ref[...])) + br_ref[...]
    lane = lax.broadcasted_iota(jnp.int32, (TM, LANES), 1).astype(F32)

    def first_max(vals):
        vmax = jnp.max(vals, axis=-1, keepdims=True)
        idx = jnp.min(jnp.where(vals == vmax, lane, float(LANES)), axis=-1, keepdims=True)
        return vmax, idx

    lg = jnp.where(lane < N_EGROUPS, logits, -jnp.inf)
    gmax, g_idx = first_max(lg)
    g_p = 1.0 / jnp.sum(jnp.exp(lg - gmax), axis=-1, keepdims=True)
    e_lo = N_EGROUPS + EXPERTS_PER_GROUP * g_idx
    le = jnp.where((lane >= e_lo) & (lane < e_lo + EXPERTS_PER_GROUP), logits, -jnp.inf)
    v1, i1 = first_max(le)
    v2, i2 = first_max(jnp.where(lane == i1, -jnp.inf, le))
    e2 = jnp.exp(v2 - v1)
    w1 = (1.0 / (1.0 + e2)) * g_p
    w2 = (e2 / (1.0 + e2)) * g_p
    a1 = i1 - e_lo
    a2 = i2 - e_lo
    lo = jnp.minimum(a1, a2)
    hi = jnp.maximum(a1, a2)
    pair = jnp.where(lo == 0.0, hi - 1.0, jnp.where(lo == 1.0, jnp.where(hi == 3.0, 3.0, 4.0), 5.0))
    ea = jnp.where(pair < 3.0, 0.0, jnp.where(pair < 5.0, 1.0, 3.0))
    ga = jnp.where(a1 == ea, w1, w2)
    gb = jnp.where(a1 == ea, w2, w1)
    bucket = g_idx * N_PAIRS + pair

    onehot = (lane == bucket).astype(F32)
    r = lax.broadcasted_iota(jnp.int32, (TM, TM), 0)
    c = lax.broadcasted_iota(jnp.int32, (TM, TM), 1)
    before = _dot((c < r).astype(BF16), onehot.astype(BF16))
    carry = carry_ref[0:1, :]
    rank = jnp.sum(onehot * (before + carry), axis=-1, keepdims=True)
    new_carry = carry + jnp.sum(onehot, axis=0, keepdims=True)
    carry_ref[...] = jnp.broadcast_to(new_carry, carry_ref.shape)
    cnt_ref[...] = jnp.broadcast_to(new_carry, cnt_ref.shape)
    info = jnp.where(lane == 0.0, bucket,
                     jnp.where(lane == 1.0, rank, jnp.where(lane == 2.0, ga, jnp.where(lane == 3.0, gb, 0.0))))
    info_ref[...] = info.T[:SUBLANES]


def _router(h_prompt, h_dec, nf, wr, br):
    const = lambda *shape: pl.BlockSpec(shape, lambda i: (0,) * len(shape))
    wr_hi = wr.astype(BF16)
    return pl.pallas_call(
        _router_body,
        grid=(N_TILES,),
        in_specs=[pl.BlockSpec((TM, D_MODEL), lambda i: (jnp.minimum(i, N_TILES - 2), 0)), const(N_DEC, D_MODEL),
                  const(1, D_MODEL), const(D_MODEL, LANES), const(D_MODEL, LANES), const(1, LANES)],
        out_specs=[pl.BlockSpec((TM * ROW_TILES, LANES), lambda i: (i, 0)),
                   pl.BlockSpec((SUBLANES, TM), lambda i: (0, i)),
                   const(SUBLANES, LANES)],
        out_shape=[jax.ShapeDtypeStruct((T * ROW_TILES, LANES), F32), jax.ShapeDtypeStruct((SUBLANES, T), F32),
                   jax.ShapeDtypeStruct((SUBLANES, LANES), F32)],
        scratch_shapes=[pltpu.VMEM((SUBLANES, LANES), F32)],
        compiler_params=_cparams(),
        name="moe_router",
    )(h_prompt, h_dec, nf, wr_hi, (wr - wr_hi.astype(F32)).astype(BF16), br)


GATHER_UNROLL = 8


def _gather_rows_start(idx_ref, base, src_ref, dst_ref, sem, n):
    def group(g, carry):
        for u in range(GATHER_UNROLL):
            r = g * GATHER_UNROLL + u
            src = pl.multiple_of(idx_ref[base + r] * ROW_TILES, ROW_TILES)
            dst = pl.multiple_of(r * ROW_TILES, ROW_TILES)
            pltpu.make_async_copy(src_ref.at[pl.ds(src, ROW_TILES), :], dst_ref.at[pl.ds(dst, ROW_TILES), :],
                                  sem).start(priority=u % 2)
        return carry

    lax.fori_loop(0, n // GATHER_UNROLL, group, 0)


def _gather_rows_wait(src_ref, dst_ref, sem, n):
    pltpu.make_async_copy(src_ref.at[pl.ds(0, n * ROW_TILES), :], dst_ref.at[pl.ds(0, n * ROW_TILES), :], sem).wait()


def _gather_tile(idx_ref, src_ref, buf, sem, n, n_live=None):
    i = pl.program_id(0)
    slot = lax.rem(i, 2)
    n_live = pl.num_programs(0) if n_live is None else n_live

    @pl.when(i == 0)
    def _():
        _gather_rows_start(idx_ref, 0, src_ref, buf.at[0], sem.at[0], n)

    @pl.when(i + 1 < n_live)
    def _():
        _gather_rows_start(idx_ref, (i + 1) * n, src_ref, buf.at[1 - slot], sem.at[1 - slot], n)

    @pl.when(i < n_live)
    def _():
        _gather_rows_wait(src_ref, buf.at[slot], sem.at[slot], n)

    return buf.at[slot]


def _gather_scratch(n):
    return [pltpu.VMEM((2, n * ROW_TILES, LANES), F32), pltpu.SemaphoreType.DMA((2,))]


def _moe_body(idx_ref, ea_ref, eb_ref, cha_ref, chb_ref, nlive_ref, rows_hbm, g_ref, nf_ref,
              w1a_ref, w3a_ref, w2a_ref, w1b_ref, w3b_ref, w2b_ref, out_ref,
              s1a, s3a, s2a, s1b, s3b, s2b, xbuf, xsem):
    i = pl.program_id(0)
    live = i < nlive_ref[0]
    x_ref = _gather_tile(idx_ref, rows_hbm, xbuf, xsem, TM, nlive_ref[0])

    @pl.when(cha_ref[i] == 1)
    def _():
        s1a[...] = w1a_ref[...].astype(BF16)
        s3a[...] = w3a_ref[...].astype(BF16)
        s2a[...] = w2a_ref[...].astype(BF16)

    @pl.when(chb_ref[i] == 1)
    def _():
        s1b[...] = w1b_ref[...].astype(BF16)
        s3b[...] = w3b_ref[...].astype(BF16)
        s2b[...] = w2b_ref[...].astype(BF16)

    @pl.when(live)
    def _():
        h = _load_rows(x_ref, TM)
        xn = _rms(h, nf_ref[...]).astype(BF16)
        g = g_ref[...]

        def ffn(w1, w3, w2):
            a = _dot(xn, w1[...])
            hdn = (a * (1.0 / (1.0 + jnp.exp(-a)))) * _dot(xn, w3[...])
            return _dot(hdn.astype(BF16), w2[...])

        y = g[:, 0:1] * ffn(s1a, s3a, s2a) + g[:, 1:2] * ffn(s1b, s3b, s2b)
        _store_rows(out_ref, h + y, TM)

    @pl.when(jnp.logical_not(live))
    def _():
        out_ref[...] = jnp.zeros_like(out_ref)


def _moe_ffn(layer, idx_sorted, ea, eb, cha, chb, n_live, rows, gates, nf, w1, w3, w2):
    wa = lambda shape: pl.BlockSpec((None, None) + shape, lambda i, ix, ea, eb, ca, cb, va: (layer, ea[i], 0, 0))
    wb = lambda shape: pl.BlockSpec((None, None) + shape, lambda i, ix, ea, eb, ca, cb, va: (layer, eb[i], 0, 0))
    up, down = (D_MODEL, D_EXPERT), (D_EXPERT, D_MODEL)
    return pl.pallas_call(
        _moe_body,
        grid_spec=pltpu.PrefetchScalarGridSpec(
            num_scalar_prefetch=6,
            grid=(MOE_TILES,),
            in_specs=[
                pl.BlockSpec(memory_space=pl.ANY),
                pl.BlockSpec((TM, 2), lambda i, *_: (i, 0)),
                pl.BlockSpec((1, D_MODEL), lambda i, *_: (0, 0)),
                wa(up), wa(up), wa(down), wb(up), wb(up), wb(down),
            ],
            out_specs=pl.BlockSpec((TM * ROW_TILES, LANES), lambda i, *_: (i, 0)),
            scratch_shapes=[pltpu.VMEM(up, BF16), pltpu.VMEM(up, BF16), pltpu.VMEM(down, BF16),
                            pltpu.VMEM(up, BF16), pltpu.VMEM(up, BF16), pltpu.VMEM(down, BF16)]
            + _gather_scratch(TM),
        ),
        out_shape=jax.ShapeDtypeStruct((P_ROWS * ROW_TILES, LANES), F32),
        compiler_params=_cparams(),
        name="moe_ffn",
    )(idx_sorted, ea, eb, cha, chb, n_live, rows, gates, nf, w1, w3, w2, w1, w3, w2)


def _router_weights(layer, w_group, b_group, w_expert, b_expert):
    wr = jnp.zeros((D_MODEL, LANES), F32)
    wr = wr.at[:, :N_EGROUPS].set(w_group[layer]).at[:, N_EGROUPS:N_EGROUPS + N_EXPERTS].set(w_expert[layer])
    br = jnp.zeros((1, LANES), F32)
    br = br.at[0, :N_EGROUPS].set(b_group[layer]).at[0, N_EGROUPS:N_EGROUPS + N_EXPERTS].set(b_expert[layer])
    return wr, br


def _moe_layer(layer, rows, info, cnt, nf, w1, w3, w2):
    bucket = info[0].astype(jnp.int32)
    rank = info[1].astype(jnp.int32)
    counts = cnt[0, :N_BUCKETS].astype(jnp.int32)
    n_tiles = (counts + TM - 1) // TM
    tile_end = jnp.cumsum(n_tiles)
    tile_start = tile_end - n_tiles
    start_of = jnp.sum(jnp.where(bucket[:, None] == jnp.arange(N_BUCKETS)[None, :], tile_start[None, :], 0), axis=1)
    pos = start_of * TM + rank
    packed = jnp.stack([jnp.arange(T, dtype=F32), info[2], info[3]], axis=1)
    base = jnp.zeros((P_ROWS, 3), F32).at[:, 0].set((jnp.arange(P_ROWS) % T).astype(F32))
    packed = base.at[pos].set(packed, unique_indices=True, indices_are_sorted=False, mode="promise_in_bounds")
    idx_sorted = packed[:, 0].astype(jnp.int32)
    gates = packed[:, 1:3]
    total = tile_end[-1]
    j = jnp.minimum(jnp.arange(MOE_TILES), total - 1)
    tb = jnp.sum((j[:, None] >= tile_end[None, :]).astype(jnp.int32), axis=1)
    grp, pair = tb // N_PAIRS, tb % N_PAIRS
    ea = (grp * EXPERTS_PER_GROUP + jnp.asarray(PAIR_A, jnp.int32)[pair]).astype(jnp.int32)
    eb = (grp * EXPERTS_PER_GROUP + jnp.asarray(PAIR_B, jnp.int32)[pair]).astype(jnp.int32)
    first = jnp.arange(MOE_TILES) == 0
    cha = (first | (ea != jnp.roll(ea, 1))).astype(jnp.int32)
    chb = (first | (eb != jnp.roll(eb, 1))).astype(jnp.int32)
    n_live = total.reshape(1).astype(jnp.int32)
    return _moe_ffn(layer, idx_sorted, ea, eb, cha, chb, n_live, rows, gates, nf, w1, w3, w2), pos


VT_ROWS = LANES + 16


def _seg_matrix():
    lane = np.arange(LANES)
    seg = np.where(lane < NOPE_DIM, 0, np.where(lane < NOPE_DIM + ROPE_DIM, 1, 2))
    return jnp.asarray(seg[:, None] == seg[None, :], BF16)


def _seg_count():
    lane = np.arange(LANES)
    return jnp.asarray(np.where(lane < NOPE_DIM, 1.0 / NOPE_DIM, 1.0 / ROPE_DIM), F32).reshape(1, LANES)


def _rope_swap(x):
    lane = lax.broadcasted_iota(jnp.int32, x.shape, 1)
    return jnp.where(lane < ROPE_LO + ROPE_HALF, pltpu.roll(x, LANES - ROPE_HALF, 1), pltpu.roll(x, ROPE_HALF, 1))


def _expand_k(cb, kr, wuk_ref, seg, cnt, kg_ref, k_ref):
    kn = _dot(cb, wuk_ref[...])
    heads = [kn[:, hh * HEAD_W:(hh + 1) * HEAD_W] for hh in range(N_HEADS)]
    ms = [_split_dot(x * x, seg) * cnt for x in heads]
    for hh in range(N_HEADS):
        k_ref[:, hh * HEAD_W:(hh + 1) * HEAD_W] = (heads[hh] * lax.rsqrt(ms[hh] + EPS) * kg_ref[...] + kr).astype(BF16)


def _mla_proj_body(pos_ref, sorted_hbm, cos_ref, sin_ref, cost_ref, sint_ref, kvn_ref, wdkv_ref, kvan_ref, krg_ref,
                   wuk_ref, wuv_ref, wuvt_ref, kg_ref, nmq_ref, wdq_ref, qan_ref, wuqt_ref, qg_ref, seg_ref, cnt_ref,
                   rows_ref, ckvp_ref, ckvd_ref, krp_ref, krd_ref, k_ref, vdec_ref, vt_ref, qt_ref, qdec_ref,
                   xbuf, xsem):
    is_dec = pl.program_id(0) == N_TILES - 1
    x_ref = _gather_tile(pos_ref, sorted_hbm, xbuf, xsem, TM)
    rows_ref[...] = x_ref[...]
    h = _load_rows(x_ref, TM)
    c = _dot(_rms(h, kvn_ref[...]).astype(BF16), wdkv_ref[...])
    ckv = _rms(c[:, :KV_RANK], kvan_ref[...])
    kr = c[:, KV_RANK:]
    kr = kr * lax.rsqrt(jnp.sum(kr * kr, axis=-1, keepdims=True) * (1.0 / ROPE_DIM) + EPS) * krg_ref[...]
    kr = kr * cos_ref[...] + _rope_swap(kr) * sin_ref[...]
    cb = ckv.astype(BF16)
    _expand_k(cb, kr, wuk_ref, seg_ref[...], cnt_ref[...], kg_ref, k_ref)
    vt = _dot(wuvt_ref[...], ckv.T.astype(BF16)).astype(BF16)
    for pr in range(N_HEADS // 2):
        vt_ref[0, pr * VT_ROWS:pr * VT_ROWS + LANES, :] = vt[pr * LANES:(pr + 1) * LANES]
        vt_ref[0, pr * VT_ROWS + LANES:(pr + 1) * VT_ROWS, :] = jnp.ones((VT_ROWS - LANES, TM), BF16)
    cq = _rms(_dot(_rms(h, nmq_ref[...]).astype(BF16), wdq_ref[...]), qan_ref[...])
    qt = _dot(wuqt_ref[...], cq.T.astype(BF16))
    cost, sint, qg = cost_ref[0], sint_ref[0], qg_ref[...]
    for hh in range(N_HEADS):
        x = qt[hh * HEAD_W:(hh + 1) * HEAD_W, :]
        xn, xr = x[:NOPE_DIM], x[ROPE_LO:ROPE_LO + ROPE_DIM]
        xn = xn * lax.rsqrt(jnp.mean(xn * xn, axis=0, keepdims=True) + EPS) * qg[:NOPE_DIM]
        xr = xr * lax.rsqrt(jnp.mean(xr * xr, axis=0, keepdims=True) + EPS) * qg[ROPE_LO:ROPE_LO + ROPE_DIM]
        x1, x2 = xr[:ROPE_HALF], xr[ROPE_HALF:]
        qh = jnp.concatenate([xn, x1 * cost - x2 * sint, x1 * sint + x2 * cost,
                              jnp.zeros((HEAD_W - NOPE_DIM - ROPE_DIM, TM), F32)], axis=0) * Q_SCALE
        qt_ref[0, hh * HEAD_W:(hh + 1) * HEAD_W, :] = qh.astype(BF16)

    @pl.when(jnp.logical_not(is_dec))
    def _():
        ckvp_ref[...] = ckv
        krp_ref[...] = kr[:, ROPE_LO:ROPE_LO + ROPE_DIM]

    @pl.when(is_dec)
    def _():
        ckvd_ref[...] = ckv
        krd_ref[...] = kr[:, ROPE_LO:ROPE_LO + ROPE_DIM]
        vdec_ref[...] = _dot(cb, wuv_ref[...]).astype(BF16)
        for hh in range(N_HEADS):
            rows = slice(hh * HEAD_W, (hh + 1) * HEAD_W)
            qdec_ref[:, rows] = qt_ref[0, rows, :].astype(F32).T.astype(BF16)


def _mla_proj(pos, sorted_rows, cos_t, sin_t, cos_tt, sin_tt, kvn, wdkv, kvan, krg, wuk, wuv, kg, nmq, wdq, qan, wuq, qg):
    const = lambda *shape: pl.BlockSpec(shape, lambda i, p: (0,) * len(shape))
    tab_tile = lambda i: jnp.where(i < N_PROMPT // TM, i % (SEQ // TM), SEQ // TM)
    tab = pl.BlockSpec((TM, LANES), lambda i, p: (tab_tile(i), 0))
    tab_t = pl.BlockSpec((1, ROPE_HALF, TM), lambda i, p: (tab_tile(i), 0, 0))
    row = lambda w: pl.BlockSpec((TM, w), lambda i, p: (i, 0))
    prow = lambda w: pl.BlockSpec((TM, w), lambda i, p: (jnp.minimum(i, N_TILES - 2), 0))
    return pl.pallas_call(
        _mla_proj_body,
        grid_spec=pltpu.PrefetchScalarGridSpec(
            num_scalar_prefetch=1,
            grid=(N_TILES,),
            in_specs=[
                pl.BlockSpec(memory_space=pl.ANY), tab, tab, tab_t, tab_t,
                const(1, D_MODEL), const(D_MODEL, KV_RANK + LANES), const(1, KV_RANK), const(1, LANES),
                const(KV_RANK, N_HEADS * HEAD_W), const(KV_RANK, N_HEADS * V_DIM), const(N_HEADS * V_DIM, KV_RANK),
                const(1, LANES),
                const(1, D_MODEL), const(D_MODEL, Q_RANK), const(1, Q_RANK), const(N_HEADS * HEAD_W, Q_RANK),
                const(HEAD_W, TM), const(LANES, LANES), const(1, LANES),
            ],
            out_specs=[pl.BlockSpec((TM * ROW_TILES, LANES), lambda i, p: (i, 0)),
                       prow(KV_RANK), const(N_DEC, KV_RANK), prow(ROPE_DIM), const(N_DEC, ROPE_DIM),
                       row(N_HEADS * HEAD_W), const(N_DEC, N_HEADS * V_DIM),
                       pl.BlockSpec((1, N_HEADS // 2 * VT_ROWS, TM), lambda i, p: (i, 0, 0)),
                       pl.BlockSpec((1, N_HEADS * HEAD_W, TM), lambda i, p: (i, 0, 0)),
                       const(N_DEC, N_HEADS * HEAD_W)],
            scratch_shapes=_gather_scratch(TM),
        ),
        out_shape=[
            jax.ShapeDtypeStruct((T * ROW_TILES, LANES), F32),
            jax.ShapeDtypeStruct((N_PROMPT, KV_RANK), F32), jax.ShapeDtypeStruct((N_DEC, KV_RANK), F32),
            jax.ShapeDtypeStruct((N_PROMPT, ROPE_DIM), F32), jax.ShapeDtypeStruct((N_DEC, ROPE_DIM), F32),
            jax.ShapeDtypeStruct((T, N_HEADS * HEAD_W), BF16), jax.ShapeDtypeStruct((N_DEC, N_HEADS * V_DIM), BF16),
            jax.ShapeDtypeStruct((N_TILES, N_HEADS // 2 * VT_ROWS, TM), BF16),
            jax.ShapeDtypeStruct((N_TILES, N_HEADS * HEAD_W, TM), BF16),
            jax.ShapeDtypeStruct((N_DEC, N_HEADS * HEAD_W), BF16),
        ],
        compiler_params=_cparams(),
        name="mla_proj",
    )(pos, sorted_rows, cos_t, sin_t, cos_tt, sin_tt, kvn, wdkv, kvan, krg, wuk, wuv, wuv.T, kg, nmq, wdq, qan,
      wuq.T, jnp.broadcast_to(qg.reshape(HEAD_W, 1), (HEAD_W, TM)), _seg_matrix(), _seg_count())


def _cache_kv_body(ckv_ref, kr_ref, place_ref, wuk_ref, wuv_ref, kg_ref, seg_ref, cnt_ref, k_ref, v_ref):
    kr = _dot(kr_ref[...].astype(BF16), place_ref[...])
    cb = ckv_ref[...].astype(BF16)
    _expand_k(cb, kr, wuk_ref, seg_ref[...], cnt_ref[...], kg_ref, k_ref)
    v_ref[...] = _dot(cb, wuv_ref[...]).astype(BF16)


CACHE_ROWS = 1024
assert PAST_LEN % CACHE_ROWS == 0


def _cache_kv(ckv, kr, wuk, wuv, kg):
    n = ckv.shape[0]
    place = jnp.asarray(np.arange(ROPE_DIM)[:, None] + ROPE_LO == np.arange(LANES)[None, :], BF16)
    const = lambda *shape: pl.BlockSpec(shape, lambda i: (0,) * len(shape))
    row = lambda w: pl.BlockSpec((CACHE_ROWS, w), lambda i: (i, 0))
    return pl.pallas_call(
        _cache_kv_body,
        grid=(n // CACHE_ROWS,),
        in_specs=[row(KV_RANK), row(ROPE_DIM), const(ROPE_DIM, LANES), const(KV_RANK, N_HEADS * HEAD_W),
                  const(KV_RANK, N_HEADS * V_DIM), const(1, LANES), const(LANES, LANES), const(1, LANES)],
        out_specs=[row(N_HEADS * HEAD_W), row(N_HEADS * V_DIM)],
        out_shape=[jax.ShapeDtypeStruct((n, N_HEADS * HEAD_W), BF16), jax.ShapeDtypeStruct((n, N_HEADS * V_DIM), BF16)],
        compiler_params=_cparams(),
        name="cache_kv",
    )(ckv, kr, place, wuk, wuv, kg, _seg_matrix(), _seg_count())


TQ = 256
TK = 256
assert TQ == TK and TQ % CHUNK == 0
SCORE_LOOKAHEAD = 4


def _qk(q, k):
    return lax.dot_general(q, k, (((1,), (1,)), ((), ())), preferred_element_type=F32)


def _merge_heads(o_ref, outs, rows):
    lane = lax.broadcasted_iota(jnp.int32, (rows, LANES), 1)
    for pr in range(N_HEADS // 2):
        o_ref[:, pr * LANES:(pr + 1) * LANES] = jnp.where(lane < V_DIM, outs[2 * pr], outs[2 * pr + 1]).astype(BF16)


def _prompt_attn_body(qt_ref, k_ref, vt_ref, rows_ref, wo_ref, out_ref, m_scr, acc_scr):
    qi = pl.program_id(1)
    m_scr[...] = jnp.full(m_scr.shape, NEG, F32)
    acc_scr[...] = jnp.zeros(acc_scr.shape, F32)

    def all_heads(j, mask):
        ks = pl.ds(pl.multiple_of(j * TK, TK), TK)

        def scores(hh):
            hcols = slice(hh * HEAD_W, (hh + 1) * HEAD_W)
            return _dot(k_ref[ks, hcols], qt_ref[0, hcols, :])

        ahead = [scores(hh) for hh in range(SCORE_LOOKAHEAD)]
        for hh in range(N_HEADS):
            vrows = slice((hh // 2) * VT_ROWS, (hh // 2 + 1) * VT_ROWS)
            s = ahead.pop(0)
            if hh + SCORE_LOOKAHEAD < N_HEADS:
                ahead.append(scores(hh + SCORE_LOOKAHEAD))
            if mask is not None:
                s = jnp.where(mask, s, NEG)
            m_old = m_scr[hh]
            m_new = jnp.maximum(m_old, jnp.max(s, axis=0, keepdims=True))
            p = jnp.exp2(s - m_new).astype(BF16)
            m_scr[hh] = m_new
            acc_scr[hh] = jnp.exp2(m_old - m_new) * acc_scr[hh] + _dot(vt_ref[j, vrows, :], p)

    def step(j, carry):
        all_heads(j, None)
        return carry

    lax.fori_loop(0, qi, step, 0)
    kc = lax.broadcasted_iota(jnp.int32, (TK, TQ), 0) // CHUNK
    qc = lax.broadcasted_iota(jnp.int32, (TK, TQ), 1) // CHUNK
    all_heads(qi, kc <= qc)
    row = lax.broadcasted_iota(jnp.int32, (LANES, TQ), 0)
    pairs = []
    for pr in range(N_HEADS // 2):
        even = acc_scr[2 * pr, :LANES] / acc_scr[2 * pr, LANES:LANES + 1]
        odd = acc_scr[2 * pr + 1, :LANES] / acc_scr[2 * pr + 1, LANES:LANES + 1]
        pairs.append(jnp.where(row < V_DIM, even, odd))
    o = jnp.concatenate(pairs, axis=0).T.astype(BF16)
    out_ref[...] = _load_rows(rows_ref, TQ) + _dot(o, wo_ref[...])


def _prompt_attn(qt, k, vt, rows, wo):
    nq = SEQ // TQ
    return pl.pallas_call(
        _prompt_attn_body,
        grid=(BATCH, nq),
        in_specs=[
            pl.BlockSpec((1, N_HEADS * HEAD_W, TQ), lambda b, i: (b * nq + i, 0, 0)),
            pl.BlockSpec((SEQ, N_HEADS * HEAD_W), lambda b, i: (b, 0)),
            pl.BlockSpec((SEQ // TK, N_HEADS // 2 * VT_ROWS, TK), lambda b, i: (b, 0, 0)),
            pl.BlockSpec((TQ * ROW_TILES, LANES), lambda b, i: (b * nq + i, 0)),
            pl.BlockSpec((N_HEADS * V_DIM, D_MODEL), lambda b, i: (0, 0)),
        ],
        out_specs=pl.BlockSpec((TQ, D_MODEL), lambda b, i: (b * nq + i, 0)),
        out_shape=jax.ShapeDtypeStruct((N_PROMPT, D_MODEL), F32),
        scratch_shapes=[pltpu.VMEM((N_HEADS, 1, TQ), F32), pltpu.VMEM((N_HEADS, VT_ROWS, TQ), F32)],
        compiler_params=_cparams(2),
        name="prompt_attn",
    )(qt, k, vt, rows, wo)


def _sample_attn_body(q_ref, kc_ref, vc_ref, kn_ref, vn_ref, rows_ref, wo_ref, out_ref, o_scr):
    outs = []
    for hh in range(N_HEADS):
        hcols = slice(hh * HEAD_W, (hh + 1) * HEAD_W)
        vcols = slice((hh // 2) * LANES, (hh // 2 + 1) * LANES)
        q = q_ref[:, hcols]
        sc = _qk(q, kc_ref[:, hcols])
        sn = _qk(q, kn_ref[:, hcols])
        m = jnp.maximum(jnp.max(sc, axis=-1, keepdims=True), jnp.max(sn, axis=-1, keepdims=True))
        pc = jnp.exp2(sc - m)
        pn = jnp.exp2(sn - m)
        l = jnp.sum(pc, axis=-1, keepdims=True) + jnp.sum(pn, axis=-1, keepdims=True)
        acc = _dot(pc.astype(BF16), vc_ref[:, vcols]) + _dot(pn.astype(BF16), vn_ref[:, vcols])
        outs.append(acc / l)
    _merge_heads(o_scr, outs, DEC_SEQ)
    out_ref[...] = _load_rows(rows_ref, DEC_SEQ) + _dot(o_scr[...], wo_ref[...])


def _sample_attn(q, kc, vc, kn, vn, rows, wo):
    off = N_PROMPT // DEC_SEQ
    return pl.pallas_call(
        _sample_attn_body,
        grid=(DEC_BATCH,),
        in_specs=[
            pl.BlockSpec((DEC_SEQ, N_HEADS * HEAD_W), lambda b: (b, 0)),
            pl.BlockSpec((PAST_LEN, N_HEADS * HEAD_W), lambda b: (b, 0)),
            pl.BlockSpec((PAST_LEN, N_HEADS * V_DIM), lambda b: (b, 0)),
            pl.BlockSpec((DEC_SEQ, N_HEADS * HEAD_W), lambda b: (off + b, 0)),
            pl.BlockSpec((DEC_SEQ, N_HEADS * V_DIM), lambda b: (b, 0)),
            pl.BlockSpec((DEC_SEQ * ROW_TILES, LANES), lambda b: (off + b, 0)),
            pl.BlockSpec((N_HEADS * V_DIM, D_MODEL), lambda b: (0, 0)),
        ],
        out_specs=pl.BlockSpec((DEC_SEQ, D_MODEL), lambda b: (b, 0)),
        out_shape=jax.ShapeDtypeStruct((N_DEC, D_MODEL), F32),
        scratch_shapes=[pltpu.VMEM((DEC_SEQ, N_HEADS * V_DIM), BF16)],
        compiler_params=_cparams(),
        name="sample_attn",
    )(q, kc, vc, kn, vn, rows, wo)


def _finish_body(pos_ref, sorted_hbm, yp_ref, ys_ref, xbuf, xsem):
    i = pl.program_id(0)
    y = _load_rows(_gather_tile(pos_ref, sorted_hbm, xbuf, xsem, TM), TM)

    @pl.when(i < N_TILES - 1)
    def _():
        yp_ref[...] = y

    @pl.when(i == N_TILES - 1)
    def _():
        ys_ref[...] = y


def _finish(pos, sorted_rows):
    return pl.pallas_call(
        _finish_body,
        grid_spec=pltpu.PrefetchScalarGridSpec(
            num_scalar_prefetch=1,
            grid=(N_TILES,),
            in_specs=[pl.BlockSpec(memory_space=pl.ANY)],
            out_specs=[pl.BlockSpec((TM, D_MODEL), lambda i, p: (jnp.minimum(i, N_TILES - 2), 0)),
                       pl.BlockSpec((N_DEC, D_MODEL), lambda i, p: (0, 0))],
            scratch_shapes=_gather_scratch(TM),
        ),
        out_shape=[jax.ShapeDtypeStruct((N_PROMPT, D_MODEL), F32), jax.ShapeDtypeStruct((N_DEC, D_MODEL), F32)],
        compiler_params=_cparams(),
        name="finish",
    )(pos, sorted_rows)


def _rope_tables():
    half = ROPE_DIM // 2
    inv_freq = ROPE_THETA ** (-jnp.arange(half, dtype=F32) / half)
    dec_pos = PAST_LEN + jnp.tile(jnp.arange(DEC_SEQ, dtype=jnp.int32), DEC_BATCH)
    pos = jnp.concatenate([jnp.arange(SEQ, dtype=jnp.int32), dec_pos])
    ang = pos.astype(F32)[:, None] * inv_freq[None, :]
    cos, sin = jnp.cos(ang), jnp.sin(ang)
    n = pos.shape[0]
    cos_t = jnp.ones((n, LANES), F32).at[:, ROPE_LO:ROPE_LO + ROPE_DIM].set(jnp.concatenate([cos, cos], axis=1))
    sin_t = jnp.zeros((n, LANES), F32).at[:, ROPE_LO:ROPE_LO + ROPE_DIM].set(jnp.concatenate([-sin, sin], axis=1))
    to_tiles = lambda a: a.reshape(n // TM, TM, half).transpose(0, 2, 1)
    return cos_t, sin_t, to_tiles(cos), to_tiles(sin)


def _on_lanes(vec, lo):
    return jnp.zeros((1, LANES), F32).at[0, lo:lo + vec.shape[0]].set(vec)


def kernel(x_prompt, x_sample, cache_ckv, cache_krope, norm_mix, norm_ffn, gm_w_in, gm_b_in, gm_ln_g, gm_ln_b, gm_w_s, gm_b_s, gm_w_out, gm_b_out, kv_norm, w_dkv, kv_a_norm, k_rope_norm, w_uk, w_uv, k_nope_norm, w_dq, q_a_norm, w_uq, q_nope_norm, q_rope_norm, w_o, moe_w_group, moe_b_group, moe_w_expert, moe_b_expert, moe_w1, moe_w3, moe_w2):
    nf0, nf1 = norm_ffn[0].reshape(1, D_MODEL), norm_ffn[1].reshape(1, D_MODEL)

    idx = np.arange(GMLP_BLOCK)
    allowed = (idx[None, :] // CHUNK) <= (idx[:, None] // CHUNK)
    ws_p = jnp.where(allowed[None], gm_w_s[0], 0.0).astype(BF16)
    same_seq = (idx[None, :] // DEC_SEQ) == (idx[:, None] // DEC_SEQ)
    ws_d = jnp.where(same_seq[None], jnp.tile(gm_w_s[0][:, :DEC_SEQ, :DEC_SEQ], (1, GMLP_BLOCK // DEC_SEQ, GMLP_BLOCK // DEC_SEQ)), 0.0).astype(BF16)
    bs_p = gm_b_s[0][:, :, None]
    bs_d = jnp.tile(gm_b_s[0][:, :DEC_SEQ], (1, GMLP_BLOCK // DEC_SEQ))[:, :, None]
    rows, v_rows, info, cnt = _gmlp_layer(
        x_prompt.reshape(N_PROMPT, D_MODEL), x_sample.reshape(N_DEC, D_MODEL),
        norm_mix[0].reshape(1, -1), gm_w_in[0].astype(BF16), gm_b_in[0].reshape(1, -1),
        gm_ln_g[0].reshape(1, -1), gm_ln_b[0].reshape(1, -1), ws_p, ws_d, bs_p, bs_d,
        gm_w_out[0].astype(BF16), gm_b_out[0].reshape(1, -1),
        nf0, *_router_weights(0, moe_w_group, moe_b_group, moe_w_expert, moe_b_expert))
    sorted_rows, pos = _moe_layer(0, rows, info, cnt, nf0, moe_w1, moe_w3, moe_w2)

    cos_t, sin_t, cos_tt, sin_tt = _rope_tables()
    wdkv = jnp.zeros((D_MODEL, KV_RANK + LANES), F32).at[:, :KV_RANK].set(w_dkv[:, :KV_RANK])
    wdkv = wdkv.at[:, KV_RANK + ROPE_LO:KV_RANK + ROPE_LO + ROPE_DIM].set(w_dkv[:, KV_RANK:]).astype(BF16)
    wuk = jnp.zeros((KV_RANK, N_HEADS, HEAD_W), F32).at[:, :, :NOPE_DIM].set(w_uk).reshape(KV_RANK, -1).astype(BF16)
    wuv = w_uv.reshape(KV_RANK, -1).astype(BF16)
    wuq = jnp.zeros((Q_RANK, N_HEADS, HEAD_W), F32).at[:, :, :NOPE_DIM + ROPE_DIM].set(w_uq[0]).reshape(Q_RANK, -1).astype(BF16)
    kg = _on_lanes(k_nope_norm, 0)
    krg = _on_lanes(k_rope_norm, ROPE_LO)
    qg = _on_lanes(jnp.concatenate([q_nope_norm[0], q_rope_norm[0]]), 0)
    rows, ckv_p, ckv_d, krope_p, krope_d, k_new, v_dec, vt_new, qt, q_dec = _mla_proj(
        pos, sorted_rows, cos_t, sin_t, cos_tt, sin_tt, kv_norm.reshape(1, -1), wdkv, kv_a_norm.reshape(1, -1), krg, wuk, wuv, kg,
        norm_mix[1].reshape(1, -1), w_dq[0].astype(BF16), q_a_norm[0].reshape(1, -1), wuq, qg)
    k_cache, v_cache = _cache_kv(cache_ckv.reshape(-1, KV_RANK), cache_krope.reshape(-1, ROPE_DIM), wuk, wuv, kg)

    wo = w_o[0].astype(BF16)
    h_prompt = _prompt_attn(qt, k_new, vt_new, rows, wo)
    h_dec = _sample_attn(q_dec, k_cache, v_cache, k_new, v_dec, rows, wo)
    rows, info, cnt = _router(h_prompt, h_dec, nf1,
                              *_router_weights(1, moe_w_group, moe_b_group, moe_w_expert, moe_b_expert))
    sorted_rows, pos = _moe_layer(1, rows, info, cnt, nf1, moe_w1, moe_w3, moe_w2)
    y_prompt, y_sample = _finish(pos, sorted_rows)

    return (y_prompt.reshape(BATCH, SEQ, D_MODEL), y_sample.reshape(DEC_BATCH, DEC_SEQ, D_MODEL),
            ckv_p.reshape(BATCH, SEQ, KV_RANK), krope_p.reshape(BATCH, SEQ, ROPE_DIM),
            ckv_d.reshape(DEC_BATCH, DEC_SEQ, KV_RANK), krope_d.reshape(DEC_BATCH, DEC_SEQ, ROPE_DIM),
            v_rows.reshape(1, DEC_BATCH, DEC_SEQ, D_GATE))
```

```python
import functools

import jax
import jax.numpy as jnp
import numpy as np
from jax import lax
from jax.experimental import pallas as pl
from jax.experimental.pallas import tpu as pltpu

F32 = jnp.float32
BF16 = jnp.bfloat16

D_MODEL = 1024
BATCH = 8
SEQ = 2048
DEC_BATCH = 16
DEC_SEQ = 16
PAST_LEN = 2048
CHUNK = 64
GMLP_BLOCK = 128
D_GATE = 2 * D_MODEL
N_SG = 8
SG_W = D_GATE // N_SG
N_HEADS = 8
NOPE_DIM = 64
ROPE_DIM = 32
V_DIM = 64
Q_RANK = 384
KV_RANK = 256
ROPE_THETA = 10000.0
SCALE = (NOPE_DIM + ROPE_DIM) ** -0.5
Q_SCALE = SCALE * float(np.log2(np.e))
N_EGROUPS = 4
EXPERTS_PER_GROUP = 4
N_EXPERTS = N_EGROUPS * EXPERTS_PER_GROUP
D_EXPERT = 512
EPS = 1e-6
NEG = -1e30

LANES = 128
SUBLANES = 8
ROW_TILES = D_MODEL // LANES
assert ROW_TILES == SUBLANES

N_PROMPT = BATCH * SEQ
N_DEC = DEC_BATCH * DEC_SEQ
T = N_PROMPT + N_DEC
TM = 256
assert N_PROMPT % TM == 0 and N_DEC == TM
N_TILES = T // TM
HEAD_W = LANES
ROPE_LO = NOPE_DIM
ROPE_HALF = ROPE_DIM // 2

PAIR_A = (0, 0, 0, 1, 1, 3)
PAIR_B = (1, 2, 3, 3, 2, 2)
N_PAIRS = 6
N_BUCKETS = N_EGROUPS * N_PAIRS
MOE_TILES = (T + N_BUCKETS * (TM - 1) + TM - 1) // TM
P_ROWS = MOE_TILES * TM

VMEM_LIMIT = 56 * 1024 * 1024


def _cparams(n_axes=1, vmem=VMEM_LIMIT):
    return pltpu.CompilerParams(dimension_semantics=("arbitrary",) * n_axes, vmem_limit_bytes=vmem)


def _rms(x, g):
    return x * lax.rsqrt(jnp.mean(x * x, axis=-1, keepdims=True) + EPS) * g


def _load_rows(ref, n):
    return jnp.concatenate([ref[pl.ds(s, n, stride=ROW_TILES), :] for s in range(ROW_TILES)], axis=1)


def _store_rows(ref, x, n):
    for s in range(ROW_TILES):
        ref[pl.ds(s, n, stride=ROW_TILES), :] = x[:, s * LANES:(s + 1) * LANES]


def _dot(a, b):
    return jnp.dot(a, b, preferred_element_type=F32)


def _split_dot(x, m):
    hi = x.astype(BF16)
    lo = (x - hi.astype(F32)).astype(BF16)
    return _dot(hi, m) + _dot(lo, m)


def _gmlp_body(xp_ref, xd_ref, nm_ref, win_ref, bin_ref, lng_ref, lnb_ref, wsp_ref, wsd_ref, bsp_ref, bsd_ref,
               wout_ref, bout_ref, nf_ref, wrh_ref, wrl_ref, br_ref,
               rows_ref, v_ref, info_ref, cnt_ref, gated_ref, carry_ref):
    i = pl.program_id(0)
    is_dec = i == N_TILES - 1
    _route_init(carry_ref)
    x = jnp.where(is_dec, xd_ref[...], xp_ref[...])
    xn = _rms(x, nm_ref[...]).astype(BF16)
    z = _dot(xn, win_ref[...]) + bin_ref[...]
    z = z * (0.5 * (1.0 + jnp.tanh(np.float32(np.sqrt(2.0 / np.pi)) * (z + 0.044715 * (z * z * z)))))
    u = z[:, :D_GATE]
    v = z[:, D_GATE:]
    mu = jnp.mean(v, axis=-1, keepdims=True)
    vc = v - mu
    var = jnp.mean(vc * vc, axis=-1, keepdims=True)
    v = vc * lax.rsqrt(var + EPS) * lng_ref[...] + lnb_ref[...]

    v_ref[...] = v
    vb = v.astype(BF16)
    for g in range(N_SG):
        ws = jnp.where(is_dec, wsd_ref[g], wsp_ref[g])
        bs = jnp.where(is_dec, bsd_ref[g], bsp_ref[g])
        for b in range(TM // GMLP_BLOCK):
            rows = slice(b * GMLP_BLOCK, (b + 1) * GMLP_BLOCK)
            cols = slice(g * SG_W, (g + 1) * SG_W)
            s = _dot(ws, vb[rows, cols]) + bs
            gated_ref[rows, cols] = (u[rows, cols] * s).astype(BF16)
    h = x + _dot(gated_ref[...], wout_ref[...]) + bout_ref[...]
    _store_rows(rows_ref, h, TM)
    _route_tile(h, nf_ref, wrh_ref, wrl_ref, br_ref, info_ref, cnt_ref, carry_ref)


def _gmlp_layer(x_prompt, x_dec, nm, w_in, b_in, ln_g, ln_b, ws_p, ws_d, bs_p, bs_d, w_out, b_out, nf, wr, br):
    const = lambda *shape: pl.BlockSpec(shape, lambda i: (0,) * len(shape))
    wr_hi = wr.astype(BF16)
    return pl.pallas_call(
        _gmlp_body,
        grid=(N_TILES,),
        in_specs=[
            pl.BlockSpec((TM, D_MODEL), lambda i: (jnp.minimum(i, N_TILES - 2), 0)), const(N_DEC, D_MODEL),
            const(1, D_MODEL), const(D_MODEL, 2 * D_GATE), const(1, 2 * D_GATE),
            const(1, D_GATE), const(1, D_GATE),
            const(N_SG, GMLP_BLOCK, GMLP_BLOCK), const(N_SG, GMLP_BLOCK, GMLP_BLOCK),
            const(N_SG, GMLP_BLOCK, 1), const(N_SG, GMLP_BLOCK, 1),
            const(D_GATE, D_MODEL), const(1, D_MODEL),
            const(1, D_MODEL), const(D_MODEL, LANES), const(D_MODEL, LANES), const(1, LANES),
        ],
        out_specs=[
            pl.BlockSpec((TM * ROW_TILES, LANES), lambda i: (i, 0)),
            const(N_DEC, D_GATE),
            pl.BlockSpec((SUBLANES, TM), lambda i: (0, i)),
            const(SUBLANES, LANES),
        ],
        out_shape=[jax.ShapeDtypeStruct((T * ROW_TILES, LANES), F32), jax.ShapeDtypeStruct((N_DEC, D_GATE), F32),
                   jax.ShapeDtypeStruct((SUBLANES, T), F32), jax.ShapeDtypeStruct((SUBLANES, LANES), F32)],
        scratch_shapes=[pltpu.VMEM((TM, D_GATE), BF16), pltpu.VMEM((SUBLANES, LANES), F32)],
        compiler_params=_cparams(),
        name="gmlp_layer",
    )(x_prompt, x_dec, nm, w_in, b_in, ln_g, ln_b, ws_p, ws_d, bs_p, bs_d, w_out, b_out,
      nf, wr_hi, (wr - wr_hi.astype(F32)).astype(BF16), br)


def _router_body(hp_ref, hd_ref, nf_ref, wrh_ref, wrl_ref, br_ref, rows_ref, info_ref, cnt_ref, carry_ref):
    _route_init(carry_ref)
    h = jnp.where(pl.program_id(0) == N_TILES - 1, hd_ref[...], hp_ref[...])
    _store_rows(rows_ref, h, TM)
    _route_tile(h, nf_ref, wrh_ref, wrl_ref, br_ref, info_ref, cnt_ref, carry_ref)


def _route_init(carry_ref):
    @pl.when(pl.program_id(0) == 0)
    def _():
        carry_ref[...] = jnp.zeros_like(carry_ref)


def _route_tile(h, nf_ref, wrh_ref, wrl_ref, br_ref, info_ref, cnt_ref, carry_ref):
    xn = _rms(h, nf_ref[...])
    xh = xn.astype(BF16)
    xl = (xn - xh.astype(F32)).astype(BF16)
    logits = _dot(xh, wrh_ref[...]) + (_dot(xl, wrh_ref[...]) + _dot(xh, wrl_ref[...])) + br_ref[...]
    lane = lax.broadcasted_iota(jnp.int32, (TM, LANES), 1).astype(F32)

    def first_max(vals):
        vmax = jnp.max(vals, axis=-1, keepdims=True)
        idx = jnp.min(jnp.where(vals == vmax, lane, float(LANES)), axis=-1, keepdims=True)
        return vmax, idx

    lg = jnp.where(lane < N_EGROUPS, logits, -jnp.inf)
    gmax, g_idx = first_max(lg)
    g_p = 1.0 / jnp.sum(jnp.exp(lg - gmax), axis=-1, keepdims=True)
    e_lo = N_EGROUPS + EXPERTS_PER_GROUP * g_idx
    le = jnp.where((lane >= e_lo) & (lane < e_lo + EXPERTS_PER_GROUP), logits, -jnp.inf)
    v1, i1 = first_max(le)
    v2, i2 = first_max(jnp.where(lane == i1, -jnp.inf, le))
    e2 = jnp.exp(v2 - v1)
    w1 = (1.0 / (1.0 + e2)) * g_p
    w2 = (e2 / (1.0 + e2)) * g_p
    a1 = i1 - e_lo
    a2 = i2 - e_lo
    lo = jnp.minimum(a1, a2)
    hi = jnp.maximum(a1, a2)
    pair = jnp.where(lo == 0.0, hi - 1.0, jnp.where(lo == 1.0, jnp.where(hi == 3.0, 3.0, 4.0), 5.0))
    ea = jnp.where(pair < 3.0, 0.0, jnp.where(pair < 5.0, 1.0, 3.0))
    ga = jnp.where(a1 == ea, w1, w2)
    gb = jnp.where(a1 == ea, w2, w1)
    bucket = g_idx * N_PAIRS + pair

    onehot = (lane == bucket).astype(F32)
    r = lax.broadcasted_iota(jnp.int32, (TM, TM), 0)
    c = lax.broadcasted_iota(jnp.int32, (TM, TM), 1)
    before = _dot((c < r).astype(BF16), onehot.astype(BF16))
    carry = carry_ref[0:1, :]
    rank = jnp.sum(onehot * (before + carry), axis=-1, keepdims=True)
    new_carry = carry + jnp.sum(onehot, axis=0, keepdims=True)
    carry_ref[...] = jnp.broadcast_to(new_carry, carry_ref.shape)
    cnt_ref[...] = jnp.broadcast_to(new_carry, cnt_ref.shape)
    info = jnp.where(lane == 0.0, bucket,
                     jnp.where(lane == 1.0, rank, jnp.where(lane == 2.0, ga, jnp.where(lane == 3.0, gb, 0.0))))
    info_ref[...] = info.T[:SUBLANES]


def _router(h_prompt, h_dec, nf, wr, br):
    const = lambda *shape: pl.BlockSpec(shape, lambda i: (0,) * len(shape))
    wr_hi = wr.astype(BF16)
    return pl.pallas_call(
        _router_body,
        grid=(N_TILES,),
        in_specs=[pl.BlockSpec((TM, D_MODEL), lambda i: (jnp.minimum(i, N_TILES - 2), 0)), const(N_DEC, D_MODEL),
                  const(1, D_MODEL), const(D_MODEL, LANES), const(D_MODEL, LANES), const(1, LANES)],
        out_specs=[pl.BlockSpec((TM * ROW_TILES, LANES), lambda i: (i, 0)),
                   pl.BlockSpec((SUBLANES, TM), lambda i: (0, i)),
                   const(SUBLANES, LANES)],
        out_shape=[jax.ShapeDtypeStruct((T * ROW_TILES, LANES), F32), jax.ShapeDtypeStruct((SUBLANES, T), F32),
                   jax.ShapeDtypeStruct((SUBLANES, LANES), F32)],
        scratch_shapes=[pltpu.VMEM((SUBLANES, LANES), F32)],
        compiler_params=_cparams(),
        name="moe_router",
    )(h_prompt, h_dec, nf, wr_hi, (wr - wr_hi.astype(F32)).astype(BF16), br)


GATHER_UNROLL = 8


def _gather_rows_start(idx_ref, base, src_ref, dst_ref, sem, n):
    def group(g, carry):
        for u in range(GATHER_UNROLL):
            r = g * GATHER_UNROLL + u
            src = pl.multiple_of(idx_ref[base + r] * ROW_TILES, ROW_TILES)
            dst = pl.multiple_of(r * ROW_TILES, ROW_TILES)
            pltpu.make_async_copy(src_ref.at[pl.ds(src, ROW_TILES), :], dst_ref.at[pl.ds(dst, ROW_TILES), :],
                                  sem).start(priority=u % 2)
        return carry

    lax.fori_loop(0, n // GATHER_UNROLL, group, 0)


def _gather_rows_wait(src_ref, dst_ref, sem, n):
    pltpu.make_async_copy(src_ref.at[pl.ds(0, n * ROW_TILES), :], dst_ref.at[pl.ds(0, n * ROW_TILES), :], sem).wait()


def _gather_tile(idx_ref, src_ref, buf, sem, n, n_live=None):
    i = pl.program_id(0)
    slot = lax.rem(i, 2)
    n_live = pl.num_programs(0) if n_live is None else n_live

    @pl.when(i == 0)
    def _():
        _gather_rows_start(idx_ref, 0, src_ref, buf.at[0], sem.at[0], n)

    @pl.when(i + 1 < n_live)
    def _():
        _gather_rows_start(idx_ref, (i + 1) * n, src_ref, buf.at[1 - slot], sem.at[1 - slot], n)

    @pl.when(i < n_live)
    def _():
        _gather_rows_wait(src_ref, buf.at[slot], sem.at[slot], n)

    return buf.at[slot]


def _gather_scratch(n):
    return [pltpu.VMEM((2, n * ROW_TILES, LANES), F32), pltpu.SemaphoreType.DMA((2,))]


def _moe_body(idx_ref, ea_ref, eb_ref, cha_ref, chb_ref, nlive_ref, rows_hbm, g_ref, nf_ref,
              w1a_ref, w3a_ref, w2a_ref, w1b_ref, w3b_ref, w2b_ref, out_ref,
              s1a, s3a, s2a, s1b, s3b, s2b, xbuf, xsem):
    i = pl.program_id(0)
    live = i < nlive_ref[0]
    x_ref = _gather_tile(idx_ref, rows_hbm, xbuf, xsem, TM, nlive_ref[0])

    @pl.when(cha_ref[i] == 1)
    def _():
        s1a[...] = w1a_ref[...].astype(BF16)
        s3a[...] = w3a_ref[...].astype(BF16)
        s2a[...] = w2a_ref[...].astype(BF16)

    @pl.when(chb_ref[i] == 1)
    def _():
        s1b[...] = w1b_ref[...].astype(BF16)
        s3b[...] = w3b_ref[...].astype(BF16)
        s2b[...] = w2b_ref[...].astype(BF16)

    @pl.when(live)
    def _():
        h = _load_rows(x_ref, TM)
        xn = _rms(h, nf_ref[...]).astype(BF16)
        g = g_ref[...]

        def ffn(w1, w3, w2):
            a = _dot(xn, w1[...])
            hdn = (a * (1.0 / (1.0 + jnp.exp(-a)))) * _dot(xn, w3[...])
            return _dot(hdn.astype(BF16), w2[...])

        y = g[:, 0:1] * ffn(s1a, s3a, s2a) + g[:, 1:2] * ffn(s1b, s3b, s2b)
        _store_rows(out_ref, h + y, TM)

    @pl.when(jnp.logical_not(live))
    def _():
        out_ref[...] = jnp.zeros_like(out_ref)


def _moe_ffn(layer, idx_sorted, ea, eb, cha, chb, n_live, rows, gates, nf, w1, w3, w2):
    wa = lambda shape: pl.BlockSpec((None, None) + shape, lambda i, ix, ea, eb, ca, cb, va: (layer, ea[i], 0, 0))
    wb = lambda shape: pl.BlockSpec((None, None) + shape, lambda i, ix, ea, eb, ca, cb, va: (layer, eb[i], 0, 0))
    up, down = (D_MODEL, D_EXPERT), (D_EXPERT, D_MODEL)
    return pl.pallas_call(
        _moe_body,
        grid_spec=pltpu.PrefetchScalarGridSpec(
            num_scalar_prefetch=6,
            grid=(MOE_TILES,),
            in_specs=[
                pl.BlockSpec(memory_space=pl.ANY),
                pl.BlockSpec((TM, 2), lambda i, *_: (i, 0)),
                pl.BlockSpec((1, D_MODEL), lambda i, *_: (0, 0)),
                wa(up), wa(up), wa(down), wb(up), wb(up), wb(down),
            ],
            out_specs=pl.BlockSpec((TM * ROW_TILES, LANES), lambda i, *_: (i, 0)),
            scratch_shapes=[pltpu.VMEM(up, BF16), pltpu.VMEM(up, BF16), pltpu.VMEM(down, BF16),
                            pltpu.VMEM(up, BF16), pltpu.VMEM(up, BF16), pltpu.VMEM(down, BF16)]
            + _gather_scratch(TM),
        ),
        out_shape=jax.ShapeDtypeStruct((P_ROWS * ROW_TILES, LANES), F32),
        compiler_params=_cparams(),
        name="moe_ffn",
    )(idx_sorted, ea, eb, cha, chb, n_live, rows, gates, nf, w1, w3, w2, w1, w3, w2)


INVERT_UNROLL = 8


def _invert_body(pos_ref, ga_ref, gb_ref, fill_lo_ref, fill_hi_ref, idx_ref, gsa_ref, gsb_ref):
    def fill(p, carry):
        idx_ref[p] = jnp.where(p >= T, p - T, p)
        gsa_ref[p] = 0.0
        gsb_ref[p] = 0.0
        return carry

    for b in range(N_BUCKETS + 1):
        lax.fori_loop(fill_lo_ref[b], fill_hi_ref[b], fill, 0)

    def group(g, carry):
        for u in range(INVERT_UNROLL):
            t = g * INVERT_UNROLL + u
            p = pos_ref[t]
            idx_ref[p] = t
            gsa_ref[p] = ga_ref[t]
            gsb_ref[p] = gb_ref[t]
        return carry

    lax.fori_loop(0, T // INVERT_UNROLL, group, 0)


def _invert_routing(pos, ga, gb, fill_lo, fill_hi):
    smem = pl.BlockSpec(memory_space=pltpu.SMEM)
    return pl.pallas_call(
        _invert_body,
        in_specs=[smem] * 5,
        out_specs=[smem] * 3,
        out_shape=[jax.ShapeDtypeStruct((P_ROWS,), jnp.int32), jax.ShapeDtypeStruct((P_ROWS,), F32),
                   jax.ShapeDtypeStruct((P_ROWS,), F32)],
        name="invert_routing",
    )(pos, ga, gb, fill_lo, fill_hi)


def _router_weights(layer, w_group, b_group, w_expert, b_expert):
    wr = jnp.zeros((D_MODEL, LANES), F32)
    wr = wr.at[:, :N_EGROUPS].set(w_group[layer]).at[:, N_EGROUPS:N_EGROUPS + N_EXPERTS].set(w_expert[layer])
    br = jnp.zeros((1, LANES), F32)
    br = br.at[0, :N_EGROUPS].set(b_group[layer]).at[0, N_EGROUPS:N_EGROUPS + N_EXPERTS].set(b_expert[layer])
    return wr, br


def _moe_layer(layer, rows, info, cnt, nf, w1, w3, w2):
    bucket = info[0].astype(jnp.int32)
    rank = info[1].astype(jnp.int32)
    counts = cnt[0, :N_BUCKETS].astype(jnp.int32)
    n_tiles = (counts + TM - 1) // TM
    tile_end = jnp.cumsum(n_tiles)
    tile_start = tile_end - n_tiles
    start_of = jnp.sum(jnp.where(bucket[:, None] == jnp.arange(N_BUCKETS)[None, :], tile_start[None, :], 0), axis=1)
    pos = start_of * TM + rank
    fill_lo = jnp.concatenate([tile_start * TM + counts, tile_end[-1:] * TM]).astype(jnp.int32)
    fill_hi = jnp.concatenate([tile_end * TM, jnp.full((1,), P_ROWS, jnp.int32)]).astype(jnp.int32)
    idx_sorted, gate_a, gate_b = _invert_routing(pos, info[2], info[3], fill_lo, fill_hi)
    gates = jnp.stack([gate_a, gate_b], axis=1)
    total = tile_end[-1]
    j = jnp.minimum(jnp.arange(MOE_TILES), total - 1)
    tb = jnp.sum((j[:, None] >= tile_end[None, :]).astype(jnp.int32), axis=1)
    grp, pair = tb // N_PAIRS, tb % N_PAIRS
    ea = (grp * EXPERTS_PER_GROUP + jnp.asarray(PAIR_A, jnp.int32)[pair]).astype(jnp.int32)
    eb = (grp * EXPERTS_PER_GROUP + jnp.asarray(PAIR_B, jnp.int32)[pair]).astype(jnp.int32)
    first = jnp.arange(MOE_TILES) == 0
    cha = (first | (ea != jnp.roll(ea, 1))).astype(jnp.int32)
    chb = (first | (eb != jnp.roll(eb, 1))).astype(jnp.int32)
    n_live = total.reshape(1).astype(jnp.int32)
    return _moe_ffn(layer, idx_sorted, ea, eb, cha, chb, n_live, rows, gates, nf, w1, w3, w2), pos


VT_ROWS = LANES + 16


def _seg_matrix():
    lane = np.arange(LANES)
    seg = np.where(lane < NOPE_DIM, 0, np.where(lane < NOPE_DIM + ROPE_DIM, 1, 2))
    return jnp.asarray(seg[:, None] == seg[None, :], BF16)


def _seg_count():
    lane = np.arange(LANES)
    return jnp.asarray(np.where(lane < NOPE_DIM, 1.0 / NOPE_DIM, 1.0 / ROPE_DIM), F32).reshape(1, LANES)


def _rope_swap(x):
    lane = lax.broadcasted_iota(jnp.int32, x.shape, 1)
    return jnp.where(lane < ROPE_LO + ROPE_HALF, pltpu.roll(x, LANES - ROPE_HALF, 1), pltpu.roll(x, ROPE_HALF, 1))


def _expand_k(cb, kr, wuk_ref, seg, cnt, kg_ref, k_ref):
    kn = _dot(cb, wuk_ref[...])
    heads = [kn[:, hh * HEAD_W:(hh + 1) * HEAD_W] for hh in range(N_HEADS)]
    ms = [_split_dot(x * x, seg) * cnt for x in heads]
    for hh in range(N_HEADS):
        k_ref[:, hh * HEAD_W:(hh + 1) * HEAD_W] = (heads[hh] * lax.rsqrt(ms[hh] + EPS) * kg_ref[...] + kr).astype(BF16)


def _mla_proj_body(pos_ref, sorted_hbm, cos_ref, sin_ref, cost_ref, sint_ref, kvn_ref, wdkv_ref, kvan_ref, krg_ref,
                   wuk_ref, wuv_ref, wuvt_ref, kg_ref, nmq_ref, wdq_ref, qan_ref, wuqt_ref, qg_ref, seg_ref, cnt_ref,
                   rows_ref, ckvp_ref, ckvd_ref, krp_ref, krd_ref, k_ref, vdec_ref, vt_ref, qt_ref, qdec_ref,
                   xbuf, xsem):
    is_dec = pl.program_id(0) == N_TILES - 1
    x_ref = _gather_tile(pos_ref, sorted_hbm, xbuf, xsem, TM)
    rows_ref[...] = x_ref[...]
    h = _load_rows(x_ref, TM)
    c = _dot(_rms(h, kvn_ref[...]).astype(BF16), wdkv_ref[...])
    ckv = _rms(c[:, :KV_RANK], kvan_ref[...])
    kr = c[:, KV_RANK:]
    kr = kr * lax.rsqrt(jnp.sum(kr * kr, axis=-1, keepdims=True) * (1.0 / ROPE_DIM) + EPS) * krg_ref[...]
    kr = kr * cos_ref[...] + _rope_swap(kr) * sin_ref[...]
    cb = ckv.astype(BF16)
    _expand_k(cb, kr, wuk_ref, seg_ref[...], cnt_ref[...], kg_ref, k_ref)
    vt = _dot(wuvt_ref[...], ckv.T.astype(BF16)).astype(BF16)
    for pr in range(N_HEADS // 2):
        vt_ref[0, pr * VT_ROWS:pr * VT_ROWS + LANES, :] = vt[pr * LANES:(pr + 1) * LANES]
        vt_ref[0, pr * VT_ROWS + LANES:(pr + 1) * VT_ROWS, :] = jnp.ones((VT_ROWS - LANES, TM), BF16)
    cq = _rms(_dot(_rms(h, nmq_ref[...]).astype(BF16), wdq_ref[...]), qan_ref[...])
    qt = _dot(wuqt_ref[...], cq.T.astype(BF16))
    cost, sint, qg = cost_ref[0], sint_ref[0], qg_ref[...]
    for hh in range(N_HEADS):
        x = qt[hh * HEAD_W:(hh + 1) * HEAD_W, :]
        xn, xr = x[:NOPE_DIM], x[ROPE_LO:ROPE_LO + ROPE_DIM]
        xn = xn * lax.rsqrt(jnp.mean(xn * xn, axis=0, keepdims=True) + EPS) * qg[:NOPE_DIM]
        xr = xr * lax.rsqrt(jnp.mean(xr * xr, axis=0, keepdims=True) + EPS) * qg[ROPE_LO:ROPE_LO + ROPE_DIM]
        x1, x2 = xr[:ROPE_HALF], xr[ROPE_HALF:]
        qh = jnp.concatenate([xn, x1 * cost - x2 * sint, x1 * sint + x2 * cost,
                              jnp.zeros((HEAD_W - NOPE_DIM - ROPE_DIM, TM), F32)], axis=0) * Q_SCALE
        qt_ref[0, hh * HEAD_W:(hh + 1) * HEAD_W, :] = qh.astype(BF16)

    @pl.when(jnp.logical_not(is_dec))
    def _():
        ckvp_ref[...] = ckv
        krp_ref[...] = kr[:, ROPE_LO:ROPE_LO + ROPE_DIM]

    @pl.when(is_dec)
    def _():
        ckvd_ref[...] = ckv
        krd_ref[...] = kr[:, ROPE_LO:ROPE_LO + ROPE_DIM]
        vdec_ref[...] = _dot(cb, wuv_ref[...]).astype(BF16)
        for hh in range(N_HEADS):
            rows = slice(hh * HEAD_W, (hh + 1) * HEAD_W)
            qdec_ref[:, rows] = qt_ref[0, rows, :].astype(F32).T.astype(BF16)


def _mla_proj(pos, sorted_rows, cos_t, sin_t, cos_tt, sin_tt, kvn, wdkv, kvan, krg, wuk, wuv, kg, nmq, wdq, qan, wuq, qg):
    const = lambda *shape: pl.BlockSpec(shape, lambda i, p: (0,) * len(shape))
    tab_tile = lambda i: jnp.where(i < N_PROMPT // TM, i % (SEQ // TM), SEQ // TM)
    tab = pl.BlockSpec((TM, LANES), lambda i, p: (tab_tile(i), 0))
    tab_t = pl.BlockSpec((1, ROPE_HALF, TM), lambda i, p: (tab_tile(i), 0, 0))
    row = lambda w: pl.BlockSpec((TM, w), lambda i, p: (i, 0))
    prow = lambda w: pl.BlockSpec((TM, w), lambda i, p: (jnp.minimum(i, N_TILES - 2), 0))
    return pl.pallas_call(
        _mla_proj_body,
        grid_spec=pltpu.PrefetchScalarGridSpec(
            num_scalar_prefetch=1,
            grid=(N_TILES,),
            in_specs=[
                pl.BlockSpec(memory_space=pl.ANY), tab, tab, tab_t, tab_t,
                const(1, D_MODEL), const(D_MODEL, KV_RANK + LANES), const(1, KV_RANK), const(1, LANES),
                const(KV_RANK, N_HEADS * HEAD_W), const(KV_RANK, N_HEADS * V_DIM), const(N_HEADS * V_DIM, KV_RANK),
                const(1, LANES),
                const(1, D_MODEL), const(D_MODEL, Q_RANK), const(1, Q_RANK), const(N_HEADS * HEAD_W, Q_RANK),
                const(HEAD_W, TM), const(LANES, LANES), const(1, LANES),
            ],
            out_specs=[pl.BlockSpec((TM * ROW_TILES, LANES), lambda i, p: (i, 0)),
                       prow(KV_RANK), const(N_DEC, KV_RANK), prow(ROPE_DIM), const(N_DEC, ROPE_DIM),
                       row(N_HEADS * HEAD_W), const(N_DEC, N_HEADS * V_DIM),
                       pl.BlockSpec((1, N_HEADS // 2 * VT_ROWS, TM), lambda i, p: (i, 0, 0)),
                       pl.BlockSpec((1, N_HEADS * HEAD_W, TM), lambda i, p: (i, 0, 0)),
                       const(N_DEC, N_HEADS * HEAD_W)],
            scratch_shapes=_gather_scratch(TM),
        ),
        out_shape=[
            jax.ShapeDtypeStruct((T * ROW_TILES, LANES), F32),
            jax.ShapeDtypeStruct((N_PROMPT, KV_RANK), F32), jax.ShapeDtypeStruct((N_DEC, KV_RANK), F32),
            jax.ShapeDtypeStruct((N_PROMPT, ROPE_DIM), F32), jax.ShapeDtypeStruct((N_DEC, ROPE_DIM), F32),
            jax.ShapeDtypeStruct((T, N_HEADS * HEAD_W), BF16), jax.ShapeDtypeStruct((N_DEC, N_HEADS * V_DIM), BF16),
            jax.ShapeDtypeStruct((N_TILES, N_HEADS // 2 * VT_ROWS, TM), BF16),
            jax.ShapeDtypeStruct((N_TILES, N_HEADS * HEAD_W, TM), BF16),
            jax.ShapeDtypeStruct((N_DEC, N_HEADS * HEAD_W), BF16),
        ],
        compiler_params=_cparams(),
        name="mla_proj",
    )(pos, sorted_rows, cos_t, sin_t, cos_tt, sin_tt, kvn, wdkv, kvan, krg, wuk, wuv, wuv.T, kg, nmq, wdq, qan,
      wuq.T, jnp.broadcast_to(qg.reshape(HEAD_W, 1), (HEAD_W, TM)), _seg_matrix(), _seg_count())


def _cache_kv_body(ckv_ref, kr_ref, place_ref, wuk_ref, wuv_ref, kg_ref, seg_ref, cnt_ref, k_ref, v_ref):
    kr = _dot(kr_ref[...].astype(BF16), place_ref[...])
    cb = ckv_ref[...].astype(BF16)
    _expand_k(cb, kr, wuk_ref, seg_ref[...], cnt_ref[...], kg_ref, k_ref)
    v_ref[...] = _dot(cb, wuv_ref[...]).astype(BF16)


CACHE_ROWS = 1024
assert PAST_LEN % CACHE_ROWS == 0


def _cache_kv(ckv, kr, wuk, wuv, kg):
    n = ckv.shape[0]
    place = jnp.asarray(np.arange(ROPE_DIM)[:, None] + ROPE_LO == np.arange(LANES)[None, :], BF16)
    const = lambda *shape: pl.BlockSpec(shape, lambda i: (0,) * len(shape))
    row = lambda w: pl.BlockSpec((CACHE_ROWS, w), lambda i: (i, 0))
    return pl.pallas_call(
        _cache_kv_body,
        grid=(n // CACHE_ROWS,),
        in_specs=[row(KV_RANK), row(ROPE_DIM), const(ROPE_DIM, LANES), const(KV_RANK, N_HEADS * HEAD_W),
                  const(KV_RANK, N_HEADS * V_DIM), const(1, LANES), const(LANES, LANES), const(1, LANES)],
        out_specs=[row(N_HEADS * HEAD_W), row(N_HEADS * V_DIM)],
        out_shape=[jax.ShapeDtypeStruct((n, N_HEADS * HEAD_W), BF16), jax.ShapeDtypeStruct((n, N_HEADS * V_DIM), BF16)],
        compiler_params=_cparams(),
        name="cache_kv",
    )(ckv, kr, place, wuk, wuv, kg, _seg_matrix(), _seg_count())


TQ = 256
TK = 256
assert TQ == TK and TQ % CHUNK == 0
SCORE_LOOKAHEAD = 4


def _qk(q, k):
    return lax.dot_general(q, k, (((1,), (1,)), ((), ())), preferred_element_type=F32)


def _merge_heads(o_ref, outs, rows):
    lane = lax.broadcasted_iota(jnp.int32, (rows, LANES), 1)
    for pr in range(N_HEADS // 2):
        o_ref[:, pr * LANES:(pr + 1) * LANES] = jnp.where(lane < V_DIM, outs[2 * pr], outs[2 * pr + 1]).astype(BF16)


def _prompt_attn_body(qt_ref, k_ref, vt_ref, rows_ref, wo_ref, out_ref, m_scr, acc_scr):
    qi = pl.program_id(1)
    m_scr[...] = jnp.full(m_scr.shape, NEG, F32)
    acc_scr[...] = jnp.zeros(acc_scr.shape, F32)

    def all_heads(j, mask):
        ks = pl.ds(pl.multiple_of(j * TK, TK), TK)

        def scores(hh):
            hcols = slice(hh * HEAD_W, (hh + 1) * HEAD_W)
            return _dot(k_ref[ks, hcols], qt_ref[0, hcols, :])

        ahead = [scores(hh) for hh in range(SCORE_LOOKAHEAD)]
        for hh in range(N_HEADS):
            vrows = slice((hh // 2) * VT_ROWS, (hh // 2 + 1) * VT_ROWS)
            s = ahead.pop(0)
            if hh + SCORE_LOOKAHEAD < N_HEADS:
                ahead.append(scores(hh + SCORE_LOOKAHEAD))
            if mask is not None:
                s = jnp.where(mask, s, NEG)
            m_old = m_scr[hh]
            m_new = jnp.maximum(m_old, jnp.max(s, axis=0, keepdims=True))
            p = jnp.exp2(s - m_new).astype(BF16)
            m_scr[hh] = m_new
            acc_scr[hh] = jnp.exp2(m_old - m_new) * acc_scr[hh] + _dot(vt_ref[j, vrows, :], p)

    def step(j, carry):
        all_heads(j, None)
        return carry

    lax.fori_loop(0, qi, step, 0)
    kc = lax.broadcasted_iota(jnp.int32, (TK, TQ), 0) // CHUNK
    qc = lax.broadcasted_iota(jnp.int32, (TK, TQ), 1) // CHUNK
    all_heads(qi, kc <= qc)
    row = lax.broadcasted_iota(jnp.int32, (LANES, TQ), 0)
    pairs = []
    for pr in range(N_HEADS // 2):
        even = acc_scr[2 * pr, :LANES] / acc_scr[2 * pr, LANES:LANES + 1]
        odd = acc_scr[2 * pr + 1, :LANES] / acc_scr[2 * pr + 1, LANES:LANES + 1]
        pairs.append(jnp.where(row < V_DIM, even, odd))
    o = jnp.concatenate(pairs, axis=0).T.astype(BF16)
    out_ref[...] = _load_rows(rows_ref, TQ) + _dot(o, wo_ref[...])


def _prompt_attn(qt, k, vt, rows, wo):
    nq = SEQ // TQ
    return pl.pallas_call(
        _prompt_attn_body,
        grid=(BATCH, nq),
        in_specs=[
            pl.BlockSpec((1, N_HEADS * HEAD_W, TQ), lambda b, i: (b * nq + i, 0, 0)),
            pl.BlockSpec((SEQ, N_HEADS * HEAD_W), lambda b, i: (b, 0)),
            pl.BlockSpec((SEQ // TK, N_HEADS // 2 * VT_ROWS, TK), lambda b, i: (b, 0, 0)),
            pl.BlockSpec((TQ * ROW_TILES, LANES), lambda b, i: (b * nq + i, 0)),
            pl.BlockSpec((N_HEADS * V_DIM, D_MODEL), lambda b, i: (0, 0)),
        ],
        out_specs=pl.BlockSpec((TQ, D_MODEL), lambda b, i: (b * nq + i, 0)),
        out_shape=jax.ShapeDtypeStruct((N_PROMPT, D_MODEL), F32),
        scratch_shapes=[pltpu.VMEM((N_HEADS, 1, TQ), F32), pltpu.VMEM((N_HEADS, VT_ROWS, TQ), F32)],
        compiler_params=_cparams(2),
        name="prompt_attn",
    )(qt, k, vt, rows, wo)


def _sample_attn_body(q_ref, kc_ref, vc_ref, kn_ref, vn_ref, rows_ref, wo_ref, out_ref, o_scr):
    outs = []
    for hh in range(N_HEADS):
        hcols = slice(hh * HEAD_W, (hh + 1) * HEAD_W)
        vcols = slice((hh // 2) * LANES, (hh // 2 + 1) * LANES)
        q = q_ref[:, hcols]
        sc = _qk(q, kc_ref[:, hcols])
        sn = _qk(q, kn_ref[:, hcols])
        m = jnp.maximum(jnp.max(sc, axis=-1, keepdims=True), jnp.max(sn, axis=-1, keepdims=True))
        pc = jnp.exp2(sc - m)
        pn = jnp.exp2(sn - m)
        l = jnp.sum(pc, axis=-1, keepdims=True) + jnp.sum(pn, axis=-1, keepdims=True)
        acc = _dot(pc.astype(BF16), vc_ref[:, vcols]) + _dot(pn.astype(BF16), vn_ref[:, vcols])
        outs.append(acc / l)
    _merge_heads(o_scr, outs, DEC_SEQ)
    out_ref[...] = _load_rows(rows_ref, DEC_SEQ) + _dot(o_scr[...], wo_ref[...])


def _sample_attn(q, kc, vc, kn, vn, rows, wo):
    off = N_PROMPT // DEC_SEQ
    return pl.pallas_call(
        _sample_attn_body,
        grid=(DEC_BATCH,),
        in_specs=[
            pl.BlockSpec((DEC_SEQ, N_HEADS * HEAD_W), lambda b: (b, 0)),
            pl.BlockSpec((PAST_LEN, N_HEADS * HEAD_W), lambda b: (b, 0)),
            pl.BlockSpec((PAST_LEN, N_HEADS * V_DIM), lambda b: (b, 0)),
            pl.BlockSpec((DEC_SEQ, N_HEADS * HEAD_W), lambda b: (off + b, 0)),
            pl.BlockSpec((DEC_SEQ, N_HEADS * V_DIM), lambda b: (b, 0)),
            pl.BlockSpec((DEC_SEQ * ROW_TILES, LANES), lambda b: (off + b, 0)),
            pl.BlockSpec((N_HEADS * V_DIM, D_MODEL), lambda b: (0, 0)),
        ],
        out_specs=pl.BlockSpec((DEC_SEQ, D_MODEL), lambda b: (b, 0)),
        out_shape=jax.ShapeDtypeStruct((N_DEC, D_MODEL), F32),
        scratch_shapes=[pltpu.VMEM((DEC_SEQ, N_HEADS * V_DIM), BF16)],
        compiler_params=_cparams(),
        name="sample_attn",
    )(q, kc, vc, kn, vn, rows, wo)


def _finish_body(pos_ref, sorted_hbm, yp_ref, ys_ref, xbuf, xsem):
    i = pl.program_id(0)
    y = _load_rows(_gather_tile(pos_ref, sorted_hbm, xbuf, xsem, TM), TM)

    @pl.when(i < N_TILES - 1)
    def _():
        yp_ref[...] = y

    @pl.when(i == N_TILES - 1)
    def _():
        ys_ref[...] = y


def _finish(pos, sorted_rows):
    return pl.pallas_call(
        _finish_body,
        grid_spec=pltpu.PrefetchScalarGridSpec(
            num_scalar_prefetch=1,
            grid=(N_TILES,),
            in_specs=[pl.BlockSpec(memory_space=pl.ANY)],
            out_specs=[pl.BlockSpec((TM, D_MODEL), lambda i, p: (jnp.minimum(i, N_TILES - 2), 0)),
                       pl.BlockSpec((N_DEC, D_MODEL), lambda i, p: (0, 0))],
            scratch_shapes=_gather_scratch(TM),
        ),
        out_shape=[jax.ShapeDtypeStruct((N_PROMPT, D_MODEL), F32), jax.ShapeDtypeStruct((N_DEC, D_MODEL), F32)],
        compiler_params=_cparams(),
        name="finish",
    )(pos, sorted_rows)


def _rope_tables():
    half = ROPE_DIM // 2
    inv_freq = ROPE_THETA ** (-jnp.arange(half, dtype=F32) / half)
    dec_pos = PAST_LEN + jnp.tile(jnp.arange(DEC_SEQ, dtype=jnp.int32), DEC_BATCH)
    pos = jnp.concatenate([jnp.arange(SEQ, dtype=jnp.int32), dec_pos])
    ang = pos.astype(F32)[:, None] * inv_freq[None, :]
    cos, sin = jnp.cos(ang), jnp.sin(ang)
    n = pos.shape[0]
    cos_t = jnp.ones((n, LANES), F32).at[:, ROPE_LO:ROPE_LO + ROPE_DIM].set(jnp.concatenate([cos, cos], axis=1))
    sin_t = jnp.zeros((n, LANES), F32).at[:, ROPE_LO:ROPE_LO + ROPE_DIM].set(jnp.concatenate([-sin, sin], axis=1))
    to_tiles = lambda a: a.reshape(n // TM, TM, half).transpose(0, 2, 1)
    return cos_t, sin_t, to_tiles(cos), to_tiles(sin)


def _on_lanes(vec, lo):
    return jnp.zeros((1, LANES), F32).at[0, lo:lo + vec.shape[0]].set(vec)


def kernel(x_prompt, x_sample, cache_ckv, cache_krope, norm_mix, norm_ffn, gm_w_in, gm_b_in, gm_ln_g, gm_ln_b, gm_w_s, gm_b_s, gm_w_out, gm_b_out, kv_norm, w_dkv, kv_a_norm, k_rope_norm, w_uk, w_uv, k_nope_norm, w_dq, q_a_norm, w_uq, q_nope_norm, q_rope_norm, w_o, moe_w_group, moe_b_group, moe_w_expert, moe_b_expert, moe_w1, moe_w3, moe_w2):
    nf0, nf1 = norm_ffn[0].reshape(1, D_MODEL), norm_ffn[1].reshape(1, D_MODEL)

    idx = np.arange(GMLP_BLOCK)
    allowed = (idx[None, :] // CHUNK) <= (idx[:, None] // CHUNK)
    ws_p = jnp.where(allowed[None], gm_w_s[0], 0.0).astype(BF16)
    same_seq = (idx[None, :] // DEC_SEQ) == (idx[:, None] // DEC_SEQ)
    ws_d = jnp.where(same_seq[None], jnp.tile(gm_w_s[0][:, :DEC_SEQ, :DEC_SEQ], (1, GMLP_BLOCK // DEC_SEQ, GMLP_BLOCK // DEC_SEQ)), 0.0).astype(BF16)
    bs_p = gm_b_s[0][:, :, None]
    bs_d = jnp.tile(gm_b_s[0][:, :DEC_SEQ], (1, GMLP_BLOCK // DEC_SEQ))[:, :, None]
    rows, v_rows, info, cnt = _gmlp_layer(
        x_prompt.reshape(N_PROMPT, D_MODEL), x_sample.reshape(N_DEC, D_MODEL),
        norm_mix[0].reshape(1, -1), gm_w_in[0].astype(BF16), gm_b_in[0].reshape(1, -1),
        gm_ln_g[0].reshape(1, -1), gm_ln_b[0].reshape(1, -1), ws_p, ws_d, bs_p, bs_d,
        gm_w_out[0].astype(BF16), gm_b_out[0].reshape(1, -1),
        nf0, *_router_weights(0, moe_w_group, moe_b_group, moe_w_expert, moe_b_expert))
    sorted_rows, pos = _moe_layer(0, rows, info, cnt, nf0, moe_w1, moe_w3, moe_w2)

    cos_t, sin_t, cos_tt, sin_tt = _rope_tables()
    wdkv = jnp.zeros((D_MODEL, KV_RANK + LANES), F32).at[:, :KV_RANK].set(w_dkv[:, :KV_RANK])
    wdkv = wdkv.at[:, KV_RANK + ROPE_LO:KV_RANK + ROPE_LO + ROPE_DIM].set(w_dkv[:, KV_RANK:]).astype(BF16)
    wuk = jnp.zeros((KV_RANK, N_HEADS, HEAD_W), F32).at[:, :, :NOPE_DIM].set(w_uk).reshape(KV_RANK, -1).astype(BF16)
    wuv = w_uv.reshape(KV_RANK, -1).astype(BF16)
    wuq = jnp.zeros((Q_RANK, N_HEADS, HEAD_W), F32).at[:, :, :NOPE_DIM + ROPE_DIM].set(w_uq[0]).reshape(Q_RANK, -1).astype(BF16)
    kg = _on_lanes(k_nope_norm, 0)
    krg = _on_lanes(k_rope_norm, ROPE_LO)
    qg = _on_lanes(jnp.concatenate([q_nope_norm[0], q_rope_norm[0]]), 0)
    rows, ckv_p, ckv_d, krope_p, krope_d, k_new, v_dec, vt_new, qt, q_dec = _mla_proj(
        pos, sorted_rows, cos_t, sin_t, cos_tt, sin_tt, kv_norm.reshape(1, -1), wdkv, kv_a_norm.reshape(1, -1), krg, wuk, wuv, kg,
        norm_mix[1].reshape(1, -1), w_dq[0].astype(BF16), q_a_norm[0].reshape(1, -1), wuq, qg)
    k_cache, v_cache = _cache_kv(cache_ckv.reshape(-1, KV_RANK), cache_krope.reshape(-1, ROPE_DIM), wuk, wuv, kg)

    wo = w_o[0].astype(BF16)
    h_prompt = _prompt_attn(qt, k_new, vt_new, rows, wo)
    h_dec = _sample_attn(q_dec, k_cache, v_cache, k_new, v_dec, rows, wo)
    rows, info, cnt = _router(h_prompt, h_dec, nf1,
                              *_router_weights(1, moe_w_group, moe_b_group, moe_w_expert, moe_b_expert))
    sorted_rows, pos = _moe_layer(1, rows, info, cnt, nf1, moe_w1, moe_w3, moe_w2)
    y_prompt, y_sample = _finish(pos, sorted_rows)

    return (y_prompt.reshape(BATCH, SEQ, D_MODEL), y_sample.reshape(DEC_BATCH, DEC_SEQ, D_MODEL),
            ckv_p.reshape(BATCH, SEQ, KV_RANK), krope_p.reshape(BATCH, SEQ, ROPE_DIM),
            ckv_d.reshape(DEC_BATCH, DEC_SEQ, KV_RANK), krope_d.reshape(DEC_BATCH, DEC_SEQ, ROPE_DIM),
            v_rows.reshape(1, DEC_BATCH, DEC_SEQ, D_GATE))
```

```python
import functools

import jax
import jax.numpy as jnp
import numpy as np
from jax import lax
from jax.experimental import pallas as pl
from jax.experimental.pallas import tpu as pltpu

F32 = jnp.float32
BF16 = jnp.bfloat16

D_MODEL = 1024
BATCH = 8
SEQ = 2048
DEC_BATCH = 16
DEC_SEQ = 16
PAST_LEN = 2048
CHUNK = 64
GMLP_BLOCK = 128
D_GATE = 2 * D_MODEL
N_SG = 8
SG_W = D_GATE // N_SG
N_HEADS = 8
NOPE_DIM = 64
ROPE_DIM = 32
V_DIM = 64
Q_RANK = 384
KV_RANK = 256
ROPE_THETA = 10000.0
SCALE = (NOPE_DIM + ROPE_DIM) ** -0.5
Q_SCALE = SCALE * float(np.log2(np.e))
N_EGROUPS = 4
EXPERTS_PER_GROUP = 4
N_EXPERTS = N_EGROUPS * EXPERTS_PER_GROUP
D_EXPERT = 512
EPS = 1e-6
NEG = -1e30

LANES = 128
SUBLANES = 8
ROW_TILES = D_MODEL // LANES
assert ROW_TILES == SUBLANES

N_PROMPT = BATCH * SEQ
N_DEC = DEC_BATCH * DEC_SEQ
T = N_PROMPT + N_DEC
TM = 256
assert N_PROMPT % TM == 0 and N_DEC == TM
N_TILES = T // TM
HEAD_W = LANES
ROPE_LO = NOPE_DIM
ROPE_HALF = ROPE_DIM // 2

PAIR_A = (0, 0, 0, 1, 1, 3)
PAIR_B = (1, 2, 3, 3, 2, 2)
N_PAIRS = 6
N_BUCKETS = N_EGROUPS * N_PAIRS
MOE_TILES = (T + N_BUCKETS * (TM - 1) + TM - 1) // TM
P_ROWS = MOE_TILES * TM

VMEM_LIMIT = 56 * 1024 * 1024


def _cparams(n_axes=1, vmem=VMEM_LIMIT):
    return pltpu.CompilerParams(dimension_semantics=("arbitrary",) * n_axes, vmem_limit_bytes=vmem)


def _rms(x, g):
    return x * lax.rsqrt(jnp.mean(x * x, axis=-1, keepdims=True) + EPS) * g


def _load_rows(ref, n):
    return jnp.concatenate([ref[pl.ds(s, n, stride=ROW_TILES), :] for s in range(ROW_TILES)], axis=1)


def _store_rows(ref, x, n):
    for s in range(ROW_TILES):
        ref[pl.ds(s, n, stride=ROW_TILES), :] = x[:, s * LANES:(s + 1) * LANES]


def _dot(a, b):
    return jnp.dot(a, b, preferred_element_type=F32)


def _split_dot(x, m):
    hi = x.astype(BF16)
    lo = (x - hi.astype(F32)).astype(BF16)
    return _dot(hi, m) + _dot(lo, m)


GELU_K1 = float(-2.0 * np.sqrt(2.0 / np.pi) * np.log2(np.e))
GELU_K3 = GELU_K1 * 0.044715


def _gmlp_body(xp_ref, xd_ref, nm_ref, win_ref, bin_ref, lng_ref, lnb_ref, wsp_ref, wsd_ref, bsp_ref, bsd_ref,
               wout_ref, bout_ref, nf_ref, wrh_ref, wrl_ref, br_ref,
               rows_ref, v_ref, info_ref, cnt_ref, gated_ref, carry_ref):
    i = pl.program_id(0)
    is_dec = i == N_TILES - 1
    _route_init(carry_ref)
    x = jnp.where(is_dec, xd_ref[...], xp_ref[...])
    xn = _rms(x, nm_ref[...]).astype(BF16)
    z = _dot(xn, win_ref[...]) + bin_ref[...]
    z = z / (1.0 + jnp.exp2(z * (GELU_K1 + GELU_K3 * (z * z))))
    u = z[:, :D_GATE]
    v = z[:, D_GATE:]
    mu = jnp.mean(v, axis=-1, keepdims=True)
    vc = v - mu
    var = jnp.mean(vc * vc, axis=-1, keepdims=True)
    v = vc * lax.rsqrt(var + EPS) * lng_ref[...] + lnb_ref[...]

    v_ref[...] = v
    vb = v.astype(BF16)
    for g in range(N_SG):
        ws = jnp.where(is_dec, wsd_ref[g], wsp_ref[g])
        bs = jnp.where(is_dec, bsd_ref[g], bsp_ref[g])
        for b in range(TM // GMLP_BLOCK):
            rows = slice(b * GMLP_BLOCK, (b + 1) * GMLP_BLOCK)
            cols = slice(g * SG_W, (g + 1) * SG_W)
            s = _dot(ws, vb[rows, cols]) + bs
            gated_ref[rows, cols] = (u[rows, cols] * s).astype(BF16)
    h = x + _dot(gated_ref[...], wout_ref[...]) + bout_ref[...]
    _store_rows(rows_ref, h, TM)
    _route_tile(h, nf_ref, wrh_ref, wrl_ref, br_ref, info_ref, cnt_ref, carry_ref)


def _gmlp_layer(x_prompt, x_dec, nm, w_in, b_in, ln_g, ln_b, ws_p, ws_d, bs_p, bs_d, w_out, b_out, nf, wr, br):
    const = lambda *shape: pl.BlockSpec(shape, lambda i: (0,) * len(shape))
    wr_hi = wr.astype(BF16)
    return pl.pallas_call(
        _gmlp_body,
        grid=(N_TILES,),
        in_specs=[
            pl.BlockSpec((TM, D_MODEL), lambda i: (jnp.minimum(i, N_TILES - 2), 0)), const(N_DEC, D_MODEL),
            const(1, D_MODEL), const(D_MODEL, 2 * D_GATE), const(1, 2 * D_GATE),
            const(1, D_GATE), const(1, D_GATE),
            const(N_SG, GMLP_BLOCK, GMLP_BLOCK), const(N_SG, GMLP_BLOCK, GMLP_BLOCK),
            const(N_SG, GMLP_BLOCK, 1), const(N_SG, GMLP_BLOCK, 1),
            const(D_GATE, D_MODEL), const(1, D_MODEL),
            const(1, D_MODEL), const(D_MODEL, LANES), const(D_MODEL, LANES), const(1, LANES),
        ],
        out_specs=[
            pl.BlockSpec((TM * ROW_TILES, LANES), lambda i: (i, 0)),
            const(N_DEC, D_GATE),
            pl.BlockSpec((SUBLANES, TM), lambda i: (0, i)),
            const(SUBLANES, LANES),
        ],
        out_shape=[jax.ShapeDtypeStruct((T * ROW_TILES, LANES), F32), jax.ShapeDtypeStruct((N_DEC, D_GATE), F32),
                   jax.ShapeDtypeStruct((SUBLANES, T), F32), jax.ShapeDtypeStruct((SUBLANES, LANES), F32)],
        scratch_shapes=[pltpu.VMEM((TM, D_GATE), BF16), pltpu.VMEM((SUBLANES, LANES), F32)],
        compiler_params=_cparams(),
        name="gmlp_layer",
    )(x_prompt, x_dec, nm, w_in, b_in, ln_g, ln_b, ws_p, ws_d, bs_p, bs_d, w_out, b_out,
      nf, wr_hi, (wr - wr_hi.astype(F32)).astype(BF16), br)


def _router_body(hp_ref, hd_ref, nf_ref, wrh_ref, wrl_ref, br_ref, rows_ref, info_ref, cnt_ref, carry_ref):
    _route_init(carry_ref)
    h = jnp.where(pl.program_id(0) == N_TILES - 1, hd_ref[...], hp_ref[...])
    _store_rows(rows_ref, h, TM)
    _route_tile(h, nf_ref, wrh_ref, wrl_ref, br_ref, info_ref, cnt_ref, carry_ref)


def _route_init(carry_ref):
    @pl.when(pl.program_id(0) == 0)
    def _():
        carry_ref[...] = jnp.zeros_like(carry_ref)


def _route_tile(h, nf_ref, wrh_ref, wrl_ref, br_ref, info_ref, cnt_ref, carry_ref):
    xn = _rms(h, nf_ref[...])
    xh = xn.astype(BF16)
    xl = (xn - xh.astype(F32)).astype(BF16)
    logits = _dot(xh, wrh_ref[...]) + (_dot(xl, wrh_ref[...]) + _dot(xh, wrl_ref[...])) + br_ref[...]
    lane = lax.broadcasted_iota(jnp.int32, (TM, LANES), 1).astype(F32)

    def first_max(vals):
        vmax = jnp.max(vals, axis=-1, keepdims=True)
        idx = jnp.min(jnp.where(vals == vmax, lane, float(LANES)), axis=-1, keepdims=True)
        return vmax, idx

    lg = jnp.where(lane < N_EGROUPS, logits, -jnp.inf)
    gmax, g_idx = first_max(lg)
    g_p = 1.0 / jnp.sum(jnp.exp(lg - gmax), axis=-1, keepdims=True)
    e_lo = N_EGROUPS + EXPERTS_PER_GROUP * g_idx
    le = jnp.where((lane >= e_lo) & (lane < e_lo + EXPERTS_PER_GROUP), logits, -jnp.inf)
    v1, i1 = first_max(le)
    v2, i2 = first_max(jnp.where(lane == i1, -jnp.inf, le))
    e2 = jnp.exp(v2 - v1)
    w1 = (1.0 / (1.0 + e2)) * g_p
    w2 = (e2 / (1.0 + e2)) * g_p
    a1 = i1 - e_lo
    a2 = i2 - e_lo
    lo = jnp.minimum(a1, a2)
    hi = jnp.maximum(a1, a2)
    pair = jnp.where(lo == 0.0, hi - 1.0, jnp.where(lo == 1.0, jnp.where(hi == 3.0, 3.0, 4.0), 5.0))
    ea = jnp.where(pair < 3.0, 0.0, jnp.where(pair < 5.0, 1.0, 3.0))
    ga = jnp.where(a1 == ea, w1, w2)
    gb = jnp.where(a1 == ea, w2, w1)
    bucket = g_idx * N_PAIRS + pair

    onehot = (lane == bucket).astype(F32)
    r = lax.broadcasted_iota(jnp.int32, (TM, TM), 0)
    c = lax.broadcasted_iota(jnp.int32, (TM, TM), 1)
    before = _dot((c < r).astype(BF16), onehot.astype(BF16))
    carry = carry_ref[0:1, :]
    rank = jnp.sum(onehot * (before + carry), axis=-1, keepdims=True)
    new_carry = carry + jnp.sum(onehot, axis=0, keepdims=True)
    carry_ref[...] = jnp.broadcast_to(new_carry, carry_ref.shape)
    cnt_ref[...] = jnp.broadcast_to(new_carry, cnt_ref.shape)
    info = jnp.where(lane == 0.0, bucket,
                     jnp.where(lane == 1.0, rank, jnp.where(lane == 2.0, ga, jnp.where(lane == 3.0, gb, 0.0))))
    info_ref[...] = info.T[:SUBLANES]


def _router(h_prompt, h_dec, nf, wr, br):
    const = lambda *shape: pl.BlockSpec(shape, lambda i: (0,) * len(shape))
    wr_hi = wr.astype(BF16)
    return pl.pallas_call(
        _router_body,
        grid=(N_TILES,),
        in_specs=[pl.BlockSpec((TM, D_MODEL), lambda i: (jnp.minimum(i, N_TILES - 2), 0)), const(N_DEC, D_MODEL),
                  const(1, D_MODEL), const(D_MODEL, LANES), const(D_MODEL, LANES), const(1, LANES)],
        out_specs=[pl.BlockSpec((TM * ROW_TILES, LANES), lambda i: (i, 0)),
                   pl.BlockSpec((SUBLANES, TM), lambda i: (0, i)),
                   const(SUBLANES, LANES)],
        out_shape=[jax.ShapeDtypeStruct((T * ROW_TILES, LANES), F32), jax.ShapeDtypeStruct((SUBLANES, T), F32),
                   jax.ShapeDtypeStruct((SUBLANES, LANES), F32)],
        scratch_shapes=[pltpu.VMEM((SUBLANES, LANES), F32)],
        compiler_params=_cparams(),
        name="moe_router",
    )(h_prompt, h_dec, nf, wr_hi, (wr - wr_hi.astype(F32)).astype(BF16), br)


GATHER_UNROLL = 32


def _gather_rows_start(idx_ref, base, src_ref, dst_ref, sem, n):
    def group(g, carry):
        for u in range(GATHER_UNROLL):
            r = g * GATHER_UNROLL + u
            src = pl.multiple_of(idx_ref[base + r] * ROW_TILES, ROW_TILES)
            dst = pl.multiple_of(r * ROW_TILES, ROW_TILES)
            pltpu.make_async_copy(src_ref.at[pl.ds(src, ROW_TILES), :], dst_ref.at[pl.ds(dst, ROW_TILES), :],
                                  sem).start(priority=u % 2)
        return carry

    lax.fori_loop(0, n // GATHER_UNROLL, group, 0)


def _gather_rows_wait(src_ref, dst_ref, sem, n):
    pltpu.make_async_copy(src_ref.at[pl.ds(0, n * ROW_TILES), :], dst_ref.at[pl.ds(0, n * ROW_TILES), :], sem).wait()


def _gather_tile(idx_ref, src_ref, buf, sem, n, n_live=None):
    i = pl.program_id(0)
    slot = lax.rem(i, 2)
    n_live = pl.num_programs(0) if n_live is None else n_live

    @pl.when(i == 0)
    def _():
        _gather_rows_start(idx_ref, 0, src_ref, buf.at[0], sem.at[0], n)

    @pl.when(i + 1 < n_live)
    def _():
        _gather_rows_start(idx_ref, (i + 1) * n, src_ref, buf.at[1 - slot], sem.at[1 - slot], n)

    @pl.when(i < n_live)
    def _():
        _gather_rows_wait(src_ref, buf.at[slot], sem.at[slot], n)

    return buf.at[slot]


def _gather_scratch(n):
    return [pltpu.VMEM((2, n * ROW_TILES, LANES), F32), pltpu.SemaphoreType.DMA((2,))]


def _moe_body(idx_ref, ea_ref, eb_ref, cha_ref, chb_ref, nlive_ref, rows_hbm, g_ref, nf_ref,
              w1a_ref, w3a_ref, w2a_ref, w1b_ref, w3b_ref, w2b_ref, out_ref,
              s1a, s3a, s2a, s1b, s3b, s2b, xbuf, xsem):
    i = pl.program_id(0)
    live = i < nlive_ref[0]
    x_ref = _gather_tile(idx_ref, rows_hbm, xbuf, xsem, TM, nlive_ref[0])

    @pl.when(cha_ref[i] == 1)
    def _():
        s1a[...] = w1a_ref[...].astype(BF16)
        s3a[...] = w3a_ref[...].astype(BF16)
        s2a[...] = w2a_ref[...].astype(BF16)

    @pl.when(chb_ref[i] == 1)
    def _():
        s1b[...] = w1b_ref[...].astype(BF16)
        s3b[...] = w3b_ref[...].astype(BF16)
        s2b[...] = w2b_ref[...].astype(BF16)

    @pl.when(live)
    def _():
        h = _load_rows(x_ref, TM)
        xn = _rms(h, nf_ref[...]).astype(BF16)
        g = g_ref[...]

        def ffn(w1, w3, w2):
            a = _dot(xn, w1[...])
            hdn = (a * (1.0 / (1.0 + jnp.exp(-a)))) * _dot(xn, w3[...])
            return _dot(hdn.astype(BF16), w2[...])

        y = g[:, 0:1] * ffn(s1a, s3a, s2a) + g[:, 1:2] * ffn(s1b, s3b, s2b)
        _store_rows(out_ref, h + y, TM)

    @pl.when(jnp.logical_not(live))
    def _():
        out_ref[...] = jnp.zeros_like(out_ref)


def _moe_ffn(layer, idx_sorted, ea, eb, cha, chb, n_live, rows, gates, nf, w1, w3, w2):
    wa = lambda shape: pl.BlockSpec((None, None) + shape, lambda i, ix, ea, eb, ca, cb, va: (layer, ea[i], 0, 0))
    wb = lambda shape: pl.BlockSpec((None, None) + shape, lambda i, ix, ea, eb, ca, cb, va: (layer, eb[i], 0, 0))
    up, down = (D_MODEL, D_EXPERT), (D_EXPERT, D_MODEL)
    return pl.pallas_call(
        _moe_body,
        grid_spec=pltpu.PrefetchScalarGridSpec(
            num_scalar_prefetch=6,
            grid=(MOE_TILES,),
            in_specs=[
                pl.BlockSpec(memory_space=pl.ANY),
                pl.BlockSpec((TM, 2), lambda i, *_: (i, 0)),
                pl.BlockSpec((1, D_MODEL), lambda i, *_: (0, 0)),
                wa(up), wa(up), wa(down), wb(up), wb(up), wb(down),
            ],
            out_specs=pl.BlockSpec((TM * ROW_TILES, LANES), lambda i, *_: (i, 0)),
            scratch_shapes=[pltpu.VMEM(up, BF16), pltpu.VMEM(up, BF16), pltpu.VMEM(down, BF16),
                            pltpu.VMEM(up, BF16), pltpu.VMEM(up, BF16), pltpu.VMEM(down, BF16)]
            + _gather_scratch(TM),
        ),
        out_shape=jax.ShapeDtypeStruct((P_ROWS * ROW_TILES, LANES), F32),
        compiler_params=_cparams(),
        name="moe_ffn",
    )(idx_sorted, ea, eb, cha, chb, n_live, rows, gates, nf, w1, w3, w2, w1, w3, w2)


def _router_weights(layer, w_group, b_group, w_expert, b_expert):
    wr = jnp.zeros((D_MODEL, LANES), F32)
    wr = wr.at[:, :N_EGROUPS].set(w_group[layer]).at[:, N_EGROUPS:N_EGROUPS + N_EXPERTS].set(w_expert[layer])
    br = jnp.zeros((1, LANES), F32)
    br = br.at[0, :N_EGROUPS].set(b_group[layer]).at[0, N_EGROUPS:N_EGROUPS + N_EXPERTS].set(b_expert[layer])
    return wr, br


def _moe_layer(layer, rows, info, cnt, nf, w1, w3, w2):
    bucket = info[0].astype(jnp.int32)
    rank = info[1].astype(jnp.int32)
    counts = cnt[0, :N_BUCKETS].astype(jnp.int32)
    n_tiles = (counts + TM - 1) // TM
    tile_end = jnp.cumsum(n_tiles)
    tile_start = tile_end - n_tiles
    start_of = jnp.sum(jnp.where(bucket[:, None] == jnp.arange(N_BUCKETS)[None, :], tile_start[None, :], 0), axis=1)
    pos = start_of * TM + rank
    packed = jnp.stack([jnp.arange(T, dtype=F32), info[2], info[3]], axis=1)
    base = jnp.zeros((P_ROWS, 3), F32).at[:, 0].set((jnp.arange(P_ROWS) % T).astype(F32))
    packed = base.at[pos].set(packed, unique_indices=True, indices_are_sorted=False, mode="promise_in_bounds")
    idx_sorted = packed[:, 0].astype(jnp.int32)
    gates = packed[:, 1:3]
    total = tile_end[-1]
    j = jnp.minimum(jnp.arange(MOE_TILES), total - 1)
    tb = jnp.sum((j[:, None] >= tile_end[None, :]).astype(jnp.int32), axis=1)
    grp, pair = tb // N_PAIRS, tb % N_PAIRS
    ea = (grp * EXPERTS_PER_GROUP + jnp.asarray(PAIR_A, jnp.int32)[pair]).astype(jnp.int32)
    eb = (grp * EXPERTS_PER_GROUP + jnp.asarray(PAIR_B, jnp.int32)[pair]).astype(jnp.int32)
    first = jnp.arange(MOE_TILES) == 0
    cha = (first | (ea != jnp.roll(ea, 1))).astype(jnp.int32)
    chb = (first | (eb != jnp.roll(eb, 1))).astype(jnp.int32)
    n_live = total.reshape(1).astype(jnp.int32)
    return _moe_ffn(layer, idx_sorted, ea, eb, cha, chb, n_live, rows, gates, nf, w1, w3, w2), pos


VT_ROWS = LANES + 16


def _seg_matrix():
    lane = np.arange(LANES)
    seg = np.where(lane < NOPE_DIM, 0, np.where(lane < NOPE_DIM + ROPE_DIM, 1, 2))
    return jnp.asarray(seg[:, None] == seg[None, :], BF16)


def _seg_count():
    lane = np.arange(LANES)
    return jnp.asarray(np.where(lane < NOPE_DIM, 1.0 / NOPE_DIM, 1.0 / ROPE_DIM), F32).reshape(1, LANES)


def _rope_swap(x):
    lane = lax.broadcasted_iota(jnp.int32, x.shape, 1)
    return jnp.where(lane < ROPE_LO + ROPE_HALF, pltpu.roll(x, LANES - ROPE_HALF, 1), pltpu.roll(x, ROPE_HALF, 1))


def _expand_k(cb, kr, wuk_ref, seg, cnt, kg_ref, k_ref):
    kn = _dot(cb, wuk_ref[...])
    heads = [kn[:, hh * HEAD_W:(hh + 1) * HEAD_W] for hh in range(N_HEADS)]
    ms = [_split_dot(x * x, seg) * cnt for x in heads]
    for hh in range(N_HEADS):
        k_ref[:, hh * HEAD_W:(hh + 1) * HEAD_W] = (heads[hh] * lax.rsqrt(ms[hh] + EPS) * kg_ref[...] + kr).astype(BF16)


def _mla_proj_body(pos_ref, sorted_hbm, cos_ref, sin_ref, cost_ref, sint_ref, kvn_ref, wdkv_ref, kvan_ref, krg_ref,
                   wuk_ref, wuv_ref, wuvt_ref, kg_ref, nmq_ref, wdq_ref, qan_ref, wuqt_ref, qg_ref, seg_ref, cnt_ref,
                   rows_ref, ckvp_ref, ckvd_ref, krp_ref, krd_ref, k_ref, vdec_ref, vt_ref, qt_ref, qdec_ref,
                   xbuf, xsem):
    is_dec = pl.program_id(0) == N_TILES - 1
    x_ref = _gather_tile(pos_ref, sorted_hbm, xbuf, xsem, TM)
    rows_ref[...] = x_ref[...]
    h = _load_rows(x_ref, TM)
    c = _dot(_rms(h, kvn_ref[...]).astype(BF16), wdkv_ref[...])
    ckv = _rms(c[:, :KV_RANK], kvan_ref[...])
    kr = c[:, KV_RANK:]
    kr = kr * lax.rsqrt(jnp.sum(kr * kr, axis=-1, keepdims=True) * (1.0 / ROPE_DIM) + EPS) * krg_ref[...]
    kr = kr * cos_ref[...] + _rope_swap(kr) * sin_ref[...]
    cb = ckv.astype(BF16)
    _expand_k(cb, kr, wuk_ref, seg_ref[...], cnt_ref[...], kg_ref, k_ref)
    vt = _dot(wuvt_ref[...], ckv.T.astype(BF16)).astype(BF16)
    for pr in range(N_HEADS // 2):
        vt_ref[0, pr * VT_ROWS:pr * VT_ROWS + LANES, :] = vt[pr * LANES:(pr + 1) * LANES]
        vt_ref[0, pr * VT_ROWS + LANES:(pr + 1) * VT_ROWS, :] = jnp.ones((VT_ROWS - LANES, TM), BF16)
    cq = _rms(_dot(_rms(h, nmq_ref[...]).astype(BF16), wdq_ref[...]), qan_ref[...])
    qt = _dot(wuqt_ref[...], cq.T.astype(BF16))
    cost, sint, qg = cost_ref[0], sint_ref[0], qg_ref[...]
    for hh in range(N_HEADS):
        x = qt[hh * HEAD_W:(hh + 1) * HEAD_W, :]
        xn, xr = x[:NOPE_DIM], x[ROPE_LO:ROPE_LO + ROPE_DIM]
        xn = xn * lax.rsqrt(jnp.mean(xn * xn, axis=0, keepdims=True) + EPS) * qg[:NOPE_DIM]
        xr = xr * lax.rsqrt(jnp.mean(xr * xr, axis=0, keepdims=True) + EPS) * qg[ROPE_LO:ROPE_LO + ROPE_DIM]
        x1, x2 = xr[:ROPE_HALF], xr[ROPE_HALF:]
        qh = jnp.concatenate([xn, x1 * cost - x2 * sint, x1 * sint + x2 * cost,
                              jnp.zeros((HEAD_W - NOPE_DIM - ROPE_DIM, TM), F32)], axis=0) * Q_SCALE
        qt_ref[0, hh * HEAD_W:(hh + 1) * HEAD_W, :] = qh.astype(BF16)

    @pl.when(jnp.logical_not(is_dec))
    def _():
        ckvp_ref[...] = ckv
        krp_ref[...] = kr[:, ROPE_LO:ROPE_LO + ROPE_DIM]

    @pl.when(is_dec)
    def _():
        ckvd_ref[...] = ckv
        krd_ref[...] = kr[:, ROPE_LO:ROPE_LO + ROPE_DIM]
        vdec_ref[...] = _dot(cb, wuv_ref[...]).astype(BF16)
        for hh in range(N_HEADS):
            rows = slice(hh * HEAD_W, (hh + 1) * HEAD_W)
            qdec_ref[:, rows] = qt_ref[0, rows, :].astype(F32).T.astype(BF16)


def _mla_proj(pos, sorted_rows, cos_t, sin_t, cos_tt, sin_tt, kvn, wdkv, kvan, krg, wuk, wuv, kg, nmq, wdq, qan, wuq, qg):
    const = lambda *shape: pl.BlockSpec(shape, lambda i, p: (0,) * len(shape))
    tab_tile = lambda i: jnp.where(i < N_PROMPT // TM, i % (SEQ // TM), SEQ // TM)
    tab = pl.BlockSpec((TM, LANES), lambda i, p: (tab_tile(i), 0))
    tab_t = pl.BlockSpec((1, ROPE_HALF, TM), lambda i, p: (tab_tile(i), 0, 0))
    row = lambda w: pl.BlockSpec((TM, w), lambda i, p: (i, 0))
    prow = lambda w: pl.BlockSpec((TM, w), lambda i, p: (jnp.minimum(i, N_TILES - 2), 0))
    return pl.pallas_call(
        _mla_proj_body,
        grid_spec=pltpu.PrefetchScalarGridSpec(
            num_scalar_prefetch=1,
            grid=(N_TILES,),
            in_specs=[
                pl.BlockSpec(memory_space=pl.ANY), tab, tab, tab_t, tab_t,
                const(1, D_MODEL), const(D_MODEL, KV_RANK + LANES), const(1, KV_RANK), const(1, LANES),
                const(KV_RANK, N_HEADS * HEAD_W), const(KV_RANK, N_HEADS * V_DIM), const(N_HEADS * V_DIM, KV_RANK),
                const(1, LANES),
                const(1, D_MODEL), const(D_MODEL, Q_RANK), const(1, Q_RANK), const(N_HEADS * HEAD_W, Q_RANK),
                const(HEAD_W, TM), const(LANES, LANES), const(1, LANES),
            ],
            out_specs=[pl.BlockSpec((TM * ROW_TILES, LANES), lambda i, p: (i, 0)),
                       prow(KV_RANK), const(N_DEC, KV_RANK), prow(ROPE_DIM), const(N_DEC, ROPE_DIM),
                       row(N_HEADS * HEAD_W), const(N_DEC, N_HEADS * V_DIM),
                       pl.BlockSpec((1, N_HEADS // 2 * VT_ROWS, TM), lambda i, p: (i, 0, 0)),
                       pl.BlockSpec((1, N_HEADS * HEAD_W, TM), lambda i, p: (i, 0, 0)),
                       const(N_DEC, N_HEADS * HEAD_W)],
            scratch_shapes=_gather_scratch(TM),
        ),
        out_shape=[
            jax.ShapeDtypeStruct((T * ROW_TILES, LANES), F32),
            jax.ShapeDtypeStruct((N_PROMPT, KV_RANK), F32), jax.ShapeDtypeStruct((N_DEC, KV_RANK), F32),
            jax.ShapeDtypeStruct((N_PROMPT, ROPE_DIM), F32), jax.ShapeDtypeStruct((N_DEC, ROPE_DIM), F32),
            jax.ShapeDtypeStruct((T, N_HEADS * HEAD_W), BF16), jax.ShapeDtypeStruct((N_DEC, N_HEADS * V_DIM), BF16),
            jax.ShapeDtypeStruct((N_TILES, N_HEADS // 2 * VT_ROWS, TM), BF16),
            jax.ShapeDtypeStruct((N_TILES, N_HEADS * HEAD_W, TM), BF16),
            jax.ShapeDtypeStruct((N_DEC, N_HEADS * HEAD_W), BF16),
        ],
        compiler_params=_cparams(),
        name="mla_proj",
    )(pos, sorted_rows, cos_t, sin_t, cos_tt, sin_tt, kvn, wdkv, kvan, krg, wuk, wuv, wuv.T, kg, nmq, wdq, qan,
      wuq.T, jnp.broadcast_to(qg.reshape(HEAD_W, 1), (HEAD_W, TM)), _seg_matrix(), _seg_count())


def _cache_kv_body(ckv_ref, kr_ref, place_ref, wuk_ref, wuv_ref, kg_ref, seg_ref, cnt_ref, k_ref, v_ref):
    kr = _dot(kr_ref[...].astype(BF16), place_ref[...])
    cb = ckv_ref[...].astype(BF16)
    _expand_k(cb, kr, wuk_ref, seg_ref[...], cnt_ref[...], kg_ref, k_ref)
    v_ref[...] = _dot(cb, wuv_ref[...]).astype(BF16)


CACHE_ROWS = 1024
assert PAST_LEN % CACHE_ROWS == 0


def _cache_kv(ckv, kr, wuk, wuv, kg):
    n = ckv.shape[0]
    place = jnp.asarray(np.arange(ROPE_DIM)[:, None] + ROPE_LO == np.arange(LANES)[None, :], BF16)
    const = lambda *shape: pl.BlockSpec(shape, lambda i: (0,) * len(shape))
    row = lambda w: pl.BlockSpec((CACHE_ROWS, w), lambda i: (i, 0))
    return pl.pallas_call(
        _cache_kv_body,
        grid=(n // CACHE_ROWS,),
        in_specs=[row(KV_RANK), row(ROPE_DIM), const(ROPE_DIM, LANES), const(KV_RANK, N_HEADS * HEAD_W),
                  const(KV_RANK, N_HEADS * V_DIM), const(1, LANES), const(LANES, LANES), const(1, LANES)],
        out_specs=[row(N_HEADS * HEAD_W), row(N_HEADS * V_DIM)],
        out_shape=[jax.ShapeDtypeStruct((n, N_HEADS * HEAD_W), BF16), jax.ShapeDtypeStruct((n, N_HEADS * V_DIM), BF16)],
        compiler_params=_cparams(),
        name="cache_kv",
    )(ckv, kr, place, wuk, wuv, kg, _seg_matrix(), _seg_count())


TQ = 256
TK = 256
assert TQ == TK and TQ % CHUNK == 0
SCORE_LOOKAHEAD = 4


def _qk(q, k):
    return lax.dot_general(q, k, (((1,), (1,)), ((), ())), preferred_element_type=F32)


def _merge_heads(o_ref, outs, rows):
    lane = lax.broadcasted_iota(jnp.int32, (rows, LANES), 1)
    for pr in range(N_HEADS // 2):
        o_ref[:, pr * LANES:(pr + 1) * LANES] = jnp.where(lane < V_DIM, outs[2 * pr], outs[2 * pr + 1]).astype(BF16)


def _prompt_attn_body(qt_ref, k_ref, vt_ref, rows_ref, wo_ref, out_ref, m_scr, acc_scr):
    qi = pl.program_id(1)
    m_scr[...] = jnp.full(m_scr.shape, NEG, F32)
    acc_scr[...] = jnp.zeros(acc_scr.shape, F32)

    def all_heads(j, mask):
        ks = pl.ds(pl.multiple_of(j * TK, TK), TK)

        def scores(hh):
            hcols = slice(hh * HEAD_W, (hh + 1) * HEAD_W)
            return _dot(k_ref[ks, hcols], qt_ref[0, hcols, :])

        ahead = [scores(hh) for hh in range(SCORE_LOOKAHEAD)]
        for hh in range(N_HEADS):
            vrows = slice((hh // 2) * VT_ROWS, (hh // 2 + 1) * VT_ROWS)
            s = ahead.pop(0)
            if hh + SCORE_LOOKAHEAD < N_HEADS:
                ahead.append(scores(hh + SCORE_LOOKAHEAD))
            if mask is not None:
                s = jnp.where(mask, s, NEG)
            m_old = m_scr[hh]
            m_new = jnp.maximum(m_old, jnp.max(s, axis=0, keepdims=True))
            p = jnp.exp2(s - m_new).astype(BF16)
            m_scr[hh] = m_new
            acc_scr[hh] = jnp.exp2(m_old - m_new) * acc_scr[hh] + _dot(vt_ref[j, vrows, :], p)

    def step(j, carry):
        all_heads(j, None)
        return carry

    lax.fori_loop(0, qi, step, 0)
    kc = lax.broadcasted_iota(jnp.int32, (TK, TQ), 0) // CHUNK
    qc = lax.broadcasted_iota(jnp.int32, (TK, TQ), 1) // CHUNK
    all_heads(qi, kc <= qc)
    row = lax.broadcasted_iota(jnp.int32, (LANES, TQ), 0)
    pairs = []
    for pr in range(N_HEADS // 2):
        even = acc_scr[2 * pr, :LANES] / acc_scr[2 * pr, LANES:LANES + 1]
        odd = acc_scr[2 * pr + 1, :LANES] / acc_scr[2 * pr + 1, LANES:LANES + 1]
        pairs.append(jnp.where(row < V_DIM, even, odd))
    o = jnp.concatenate(pairs, axis=0).T.astype(BF16)
    out_ref[...] = _load_rows(rows_ref, TQ) + _dot(o, wo_ref[...])


def _prompt_attn(qt, k, vt, rows, wo):
    nq = SEQ // TQ
    return pl.pallas_call(
        _prompt_attn_body,
        grid=(BATCH, nq),
        in_specs=[
            pl.BlockSpec((1, N_HEADS * HEAD_W, TQ), lambda b, i: (b * nq + i, 0, 0)),
            pl.BlockSpec((SEQ, N_HEADS * HEAD_W), lambda b, i: (b, 0)),
            pl.BlockSpec((SEQ // TK, N_HEADS // 2 * VT_ROWS, TK), lambda b, i: (b, 0, 0)),
            pl.BlockSpec((TQ * ROW_TILES, LANES), lambda b, i: (b * nq + i, 0)),
            pl.BlockSpec((N_HEADS * V_DIM, D_MODEL), lambda b, i: (0, 0)),
        ],
        out_specs=pl.BlockSpec((TQ, D_MODEL), lambda b, i: (b * nq + i, 0)),
        out_shape=jax.ShapeDtypeStruct((N_PROMPT, D_MODEL), F32),
        scratch_shapes=[pltpu.VMEM((N_HEADS, 1, TQ), F32), pltpu.VMEM((N_HEADS, VT_ROWS, TQ), F32)],
        compiler_params=_cparams(2),
        name="prompt_attn",
    )(qt, k, vt, rows, wo)


def _sample_attn_body(q_ref, kc_ref, vc_ref, kn_ref, vn_ref, rows_ref, wo_ref, out_ref, o_scr):
    outs = []
    for hh in range(N_HEADS):
        hcols = slice(hh * HEAD_W, (hh + 1) * HEAD_W)
        vcols = slice((hh // 2) * LANES, (hh // 2 + 1) * LANES)
        q = q_ref[:, hcols]
        sc = _qk(q, kc_ref[:, hcols])
        sn = _qk(q, kn_ref[:, hcols])
        m = jnp.maximum(jnp.max(sc, axis=-1, keepdims=True), jnp.max(sn, axis=-1, keepdims=True))
        pc = jnp.exp2(sc - m)
        pn = jnp.exp2(sn - m)
        l = jnp.sum(pc, axis=-1, keepdims=True) + jnp.sum(pn, axis=-1, keepdims=True)
        acc = _dot(pc.astype(BF16), vc_ref[:, vcols]) + _dot(pn.astype(BF16), vn_ref[:, vcols])
        outs.append(acc / l)
    _merge_heads(o_scr, outs, DEC_SEQ)
    out_ref[...] = _load_rows(rows_ref, DEC_SEQ) + _dot(o_scr[...], wo_ref[...])


def _sample_attn(q, kc, vc, kn, vn, rows, wo):
    off = N_PROMPT // DEC_SEQ
    return pl.pallas_call(
        _sample_attn_body,
        grid=(DEC_BATCH,),
        in_specs=[
            pl.BlockSpec((DEC_SEQ, N_HEADS * HEAD_W), lambda b: (b, 0)),
            pl.BlockSpec((PAST_LEN, N_HEADS * HEAD_W), lambda b: (b, 0)),
            pl.BlockSpec((PAST_LEN, N_HEADS * V_DIM), lambda b: (b, 0)),
            pl.BlockSpec((DEC_SEQ, N_HEADS * HEAD_W), lambda b: (off + b, 0)),
            pl.BlockSpec((DEC_SEQ, N_HEADS * V_DIM), lambda b: (b, 0)),
            pl.BlockSpec((DEC_SEQ * ROW_TILES, LANES), lambda b: (off + b, 0)),
            pl.BlockSpec((N_HEADS * V_DIM, D_MODEL), lambda b: (0, 0)),
        ],
        out_specs=pl.BlockSpec((DEC_SEQ, D_MODEL), lambda b: (b, 0)),
        out_shape=jax.ShapeDtypeStruct((N_DEC, D_MODEL), F32),
        scratch_shapes=[pltpu.VMEM((DEC_SEQ, N_HEADS * V_DIM), BF16)],
        compiler_params=_cparams(),
        name="sample_attn",
    )(q, kc, vc, kn, vn, rows, wo)


def _finish_body(pos_ref, sorted_hbm, yp_ref, ys_ref, xbuf, xsem):
    i = pl.program_id(0)
    y = _load_rows(_gather_tile(pos_ref, sorted_hbm, xbuf, xsem, TM), TM)

    @pl.when(i < N_TILES - 1)
    def _():
        yp_ref[...] = y

    @pl.when(i == N_TILES - 1)
    def _():
        ys_ref[...] = y


def _finish(pos, sorted_rows):
    return pl.pallas_call(
        _finish_body,
        grid_spec=pltpu.PrefetchScalarGridSpec(
            num_scalar_prefetch=1,
            grid=(N_TILES,),
            in_specs=[pl.BlockSpec(memory_space=pl.ANY)],
            out_specs=[pl.BlockSpec((TM, D_MODEL), lambda i, p: (jnp.minimum(i, N_TILES - 2), 0)),
                       pl.BlockSpec((N_DEC, D_MODEL), lambda i, p: (0, 0))],
            scratch_shapes=_gather_scratch(TM),
        ),
        out_shape=[jax.ShapeDtypeStruct((N_PROMPT, D_MODEL), F32), jax.ShapeDtypeStruct((N_DEC, D_MODEL), F32)],
        compiler_params=_cparams(),
        name="finish",
    )(pos, sorted_rows)


def _rope_tables():
    half = ROPE_DIM // 2
    inv_freq = ROPE_THETA ** (-jnp.arange(half, dtype=F32) / half)
    dec_pos = PAST_LEN + jnp.tile(jnp.arange(DEC_SEQ, dtype=jnp.int32), DEC_BATCH)
    pos = jnp.concatenate([jnp.arange(SEQ, dtype=jnp.int32), dec_pos])
    ang = pos.astype(F32)[:, None] * inv_freq[None, :]
    cos, sin = jnp.cos(ang), jnp.sin(ang)
    n = pos.shape[0]
    cos_t = jnp.ones((n, LANES), F32).at[:, ROPE_LO:ROPE_LO + ROPE_DIM].set(jnp.concatenate([cos, cos], axis=1))
    sin_t = jnp.zeros((n, LANES), F32).at[:, ROPE_LO:ROPE_LO + ROPE_DIM].set(jnp.concatenate([-sin, sin], axis=1))
    to_tiles = lambda a: a.reshape(n // TM, TM, half).transpose(0, 2, 1)
    return cos_t, sin_t, to_tiles(cos), to_tiles(sin)


def _on_lanes(vec, lo):
    return jnp.zeros((1, LANES), F32).at[0, lo:lo + vec.shape[0]].set(vec)


def kernel(x_prompt, x_sample, cache_ckv, cache_krope, norm_mix, norm_ffn, gm_w_in, gm_b_in, gm_ln_g, gm_ln_b, gm_w_s, gm_b_s, gm_w_out, gm_b_out, kv_norm, w_dkv, kv_a_norm, k_rope_norm, w_uk, w_uv, k_nope_norm, w_dq, q_a_norm, w_uq, q_nope_norm, q_rope_norm, w_o, moe_w_group, moe_b_group, moe_w_expert, moe_b_expert, moe_w1, moe_w3, moe_w2):
    nf0, nf1 = norm_ffn[0].reshape(1, D_MODEL), norm_ffn[1].reshape(1, D_MODEL)

    idx = np.arange(GMLP_BLOCK)
    allowed = (idx[None, :] // CHUNK) <= (idx[:, None] // CHUNK)
    ws_p = jnp.where(allowed[None], gm_w_s[0], 0.0).astype(BF16)
    same_seq = (idx[None, :] // DEC_SEQ) == (idx[:, None] // DEC_SEQ)
    ws_d = jnp.where(same_seq[None], jnp.tile(gm_w_s[0][:, :DEC_SEQ, :DEC_SEQ], (1, GMLP_BLOCK // DEC_SEQ, GMLP_BLOCK // DEC_SEQ)), 0.0).astype(BF16)
    bs_p = gm_b_s[0][:, :, None]
    bs_d = jnp.tile(gm_b_s[0][:, :DEC_SEQ], (1, GMLP_BLOCK // DEC_SEQ))[:, :, None]
    rows, v_rows, info, cnt = _gmlp_layer(
        x_prompt.reshape(N_PROMPT, D_MODEL), x_sample.reshape(N_DEC, D_MODEL),
        norm_mix[0].reshape(1, -1), gm_w_in[0].astype(BF16), gm_b_in[0].reshape(1, -1),
        gm_ln_g[0].reshape(1, -1), gm_ln_b[0].reshape(1, -1), ws_p, ws_d, bs_p, bs_d,
        gm_w_out[0].astype(BF16), gm_b_out[0].reshape(1, -1),
        nf0, *_router_weights(0, moe_w_group, moe_b_group, moe_w_expert, moe_b_expert))
    sorted_rows, pos = _moe_layer(0, rows, info, cnt, nf0, moe_w1, moe_w3, moe_w2)

    cos_t, sin_t, cos_tt, sin_tt = _rope_tables()
    wdkv = jnp.zeros((D_MODEL, KV_RANK + LANES), F32).at[:, :KV_RANK].set(w_dkv[:, :KV_RANK])
    wdkv = wdkv.at[:, KV_RANK + ROPE_LO:KV_RANK + ROPE_LO + ROPE_DIM].set(w_dkv[:, KV_RANK:]).astype(BF16)
    wuk = jnp.zeros((KV_RANK, N_HEADS, HEAD_W), F32).at[:, :, :NOPE_DIM].set(w_uk).reshape(KV_RANK, -1).astype(BF16)
    wuv = w_uv.reshape(KV_RANK, -1).astype(BF16)
    wuq = jnp.zeros((Q_RANK, N_HEADS, HEAD_W), F32).at[:, :, :NOPE_DIM + ROPE_DIM].set(w_uq[0]).reshape(Q_RANK, -1).astype(BF16)
    kg = _on_lanes(k_nope_norm, 0)
    krg = _on_lanes(k_rope_norm, ROPE_LO)
    qg = _on_lanes(jnp.concatenate([q_nope_norm[0], q_rope_norm[0]]), 0)
    rows, ckv_p, ckv_d, krope_p, krope_d, k_new, v_dec, vt_new, qt, q_dec = _mla_proj(
        pos, sorted_rows, cos_t, sin_t, cos_tt, sin_tt, kv_norm.reshape(1, -1), wdkv, kv_a_norm.reshape(1, -1), krg, wuk, wuv, kg,
        norm_mix[1].reshape(1, -1), w_dq[0].astype(BF16), q_a_norm[0].reshape(1, -1), wuq, qg)
    k_cache, v_cache = _cache_kv(cache_ckv.reshape(-1, KV_RANK), cache_krope.reshape(-1, ROPE_DIM), wuk, wuv, kg)

    wo = w_o[0].astype(BF16)
    h_prompt = _prompt_attn(qt, k_new, vt_new, rows, wo)
    h_dec = _sample_attn(q_dec, k_cache, v_cache, k_new, v_dec, rows, wo)
    rows, info, cnt = _router(h_prompt, h_dec, nf1,
                              *_router_weights(1, moe_w_group, moe_b_group, moe_w_expert, moe_b_expert))
    sorted_rows, pos = _moe_layer(1, rows, info, cnt, nf1, moe_w1, moe_w3, moe_w2)
    y_prompt, y_sample = _finish(pos, sorted_rows)

    return (y_prompt.reshape(BATCH, SEQ, D_MODEL), y_sample.reshape(DEC_BATCH, DEC_SEQ, D_MODEL),
            ckv_p.reshape(BATCH, SEQ, KV_RANK), krope_p.reshape(BATCH, SEQ, ROPE_DIM),
            ckv_d.reshape(DEC_BATCH, DEC_SEQ, KV_RANK), krope_d.reshape(DEC_BATCH, DEC_SEQ, ROPE_DIM),
            v_rows.reshape(1, DEC_BATCH, DEC_SEQ, D_GATE))
```

```python
import functools

import jax
import jax.numpy as jnp
import numpy as np
from jax import lax
from jax.experimental import pallas as pl
from jax.experimental.pallas import tpu as pltpu

F32 = jnp.float32
BF16 = jnp.bfloat16

D_MODEL = 1024
BATCH = 8
SEQ = 2048
DEC_BATCH = 16
DEC_SEQ = 16
PAST_LEN = 2048
CHUNK = 64
GMLP_BLOCK = 128
D_GATE = 2 * D_MODEL
N_SG = 8
SG_W = D_GATE // N_SG
N_HEADS = 8
NOPE_DIM = 64
ROPE_DIM = 32
V_DIM = 64
Q_RANK = 384
KV_RANK = 256
ROPE_THETA = 10000.0
SCALE = (NOPE_DIM + ROPE_DIM) ** -0.5
Q_SCALE = SCALE * float(np.log2(np.e))
N_EGROUPS = 4
EXPERTS_PER_GROUP = 4
N_EXPERTS = N_EGROUPS * EXPERTS_PER_GROUP
D_EXPERT = 512
EPS = 1e-6
NEG = -1e30

LANES = 128
SUBLANES = 8
ROW_TILES = D_MODEL // LANES
assert ROW_TILES == SUBLANES

N_PROMPT = BATCH * SEQ
N_DEC = DEC_BATCH * DEC_SEQ
T = N_PROMPT + N_DEC
TM = 256
assert N_PROMPT % TM == 0 and N_DEC == TM
N_TILES = T // TM
HEAD_W = LANES
ROPE_LO = NOPE_DIM
ROPE_HALF = ROPE_DIM // 2

PAIR_A = (0, 0, 0, 1, 1, 3)
PAIR_B = (1, 2, 3, 3, 2, 2)
N_PAIRS = 6
N_BUCKETS = N_EGROUPS * N_PAIRS
MOE_TILES = (T + N_BUCKETS * (TM - 1) + TM - 1) // TM
P_ROWS = MOE_TILES * TM

VMEM_LIMIT = 56 * 1024 * 1024


def _cparams(n_axes=1, vmem=VMEM_LIMIT):
    return pltpu.CompilerParams(dimension_semantics=("arbitrary",) * n_axes, vmem_limit_bytes=vmem)


def _rms(x, g):
    return x * lax.rsqrt(jnp.mean(x * x, axis=-1, keepdims=True) + EPS) * g


def _load_rows(ref, n):
    return jnp.concatenate([ref[pl.ds(s, n, stride=ROW_TILES), :] for s in range(ROW_TILES)], axis=1)


def _store_rows(ref, x, n):
    for s in range(ROW_TILES):
        ref[pl.ds(s, n, stride=ROW_TILES), :] = x[:, s * LANES:(s + 1) * LANES]


def _dot(a, b):
    return jnp.dot(a, b, preferred_element_type=F32)


def _split_dot(x, m):
    hi = x.astype(BF16)
    lo = (x - hi.astype(F32)).astype(BF16)
    return _dot(hi, m) + _dot(lo, m)


GELU_K1 = float(-2.0 * np.sqrt(2.0 / np.pi) * np.log2(np.e))
GELU_K3 = GELU_K1 * 0.044715


def _gmlp_body(xp_ref, xd_ref, nm_ref, win_ref, bin_ref, lng_ref, lnb_ref, wsp_ref, wsd_ref, bsp_ref, bsd_ref,
               wout_ref, bout_ref, nf_ref, wrh_ref, wr2_ref, br_ref,
               rows_ref, v_ref, info_ref, cnt_ref, gated_ref, carry_ref):
    i = pl.program_id(0)
    is_dec = i == N_TILES - 1
    _route_init(carry_ref)
    x = jnp.where(is_dec, xd_ref[...], xp_ref[...])
    xn = _rms(x, nm_ref[...]).astype(BF16)
    z = _dot(xn, win_ref[...]) + bin_ref[...]
    z = z / (1.0 + jnp.exp2(z * (GELU_K1 + GELU_K3 * (z * z))))
    u = z[:, :D_GATE]
    v = z[:, D_GATE:]
    mu = jnp.mean(v, axis=-1, keepdims=True)
    vc = v - mu
    var = jnp.mean(vc * vc, axis=-1, keepdims=True)
    v = vc * lax.rsqrt(var + EPS) * lng_ref[...] + lnb_ref[...]

    v_ref[...] = v
    vb = v.astype(BF16)
    for g in range(N_SG):
        ws = jnp.where(is_dec, wsd_ref[g], wsp_ref[g])
        bs = jnp.where(is_dec, bsd_ref[g], bsp_ref[g])
        for b in range(TM // GMLP_BLOCK):
            rows = slice(b * GMLP_BLOCK, (b + 1) * GMLP_BLOCK)
            cols = slice(g * SG_W, (g + 1) * SG_W)
            s = _dot(ws, vb[rows, cols]) + bs
            gated_ref[rows, cols] = (u[rows, cols] * s).astype(BF16)
    h = x + _dot(gated_ref[...], wout_ref[...]) + bout_ref[...]
    _store_rows(rows_ref, h, TM)
    _route_tile(h, nf_ref, wrh_ref, wr2_ref, br_ref, info_ref, cnt_ref, carry_ref)


def _gmlp_layer(x_prompt, x_dec, nm, w_in, b_in, ln_g, ln_b, ws_p, ws_d, bs_p, bs_d, w_out, b_out, nf, wr, br):
    const = lambda *shape: pl.BlockSpec(shape, lambda i: (0,) * len(shape))
    return pl.pallas_call(
        _gmlp_body,
        grid=(N_TILES,),
        in_specs=[
            pl.BlockSpec((TM, D_MODEL), lambda i: (jnp.minimum(i, N_TILES - 2), 0)), const(N_DEC, D_MODEL),
            const(1, D_MODEL), const(D_MODEL, 2 * D_GATE), const(1, 2 * D_GATE),
            const(1, D_GATE), const(1, D_GATE),
            const(N_SG, GMLP_BLOCK, GMLP_BLOCK), const(N_SG, GMLP_BLOCK, GMLP_BLOCK),
            const(N_SG, GMLP_BLOCK, 1), const(N_SG, GMLP_BLOCK, 1),
            const(D_GATE, D_MODEL), const(1, D_MODEL),
            const(1, D_MODEL), const(D_MODEL, LANES), const(D_MODEL, 2 * LANES), const(1, LANES),
        ],
        out_specs=[
            pl.BlockSpec((TM * ROW_TILES, LANES), lambda i: (i, 0)),
            const(N_DEC, D_GATE),
            pl.BlockSpec((SUBLANES, TM), lambda i: (0, i)),
            const(SUBLANES, LANES),
        ],
        out_shape=[jax.ShapeDtypeStruct((T * ROW_TILES, LANES), F32), jax.ShapeDtypeStruct((N_DEC, D_GATE), F32),
                   jax.ShapeDtypeStruct((SUBLANES, T), F32), jax.ShapeDtypeStruct((SUBLANES, LANES), F32)],
        scratch_shapes=[pltpu.VMEM((TM, D_GATE), BF16), pltpu.VMEM((SUBLANES, LANES), F32)],
        compiler_params=_cparams(),
        name="gmlp_layer",
    )(x_prompt, x_dec, nm, w_in, b_in, ln_g, ln_b, ws_p, ws_d, bs_p, bs_d, w_out, b_out,
      nf, *_router_split(wr), br)


def _router_body(hp_ref, hd_ref, nf_ref, wrh_ref, wr2_ref, br_ref, rows_ref, info_ref, cnt_ref, carry_ref):
    _route_init(carry_ref)
    h = jnp.where(pl.program_id(0) == N_TILES - 1, hd_ref[...], hp_ref[...])
    _store_rows(rows_ref, h, TM)
    _route_tile(h, nf_ref, wrh_ref, wr2_ref, br_ref, info_ref, cnt_ref, carry_ref)


def _router_split(wr):
    hi = wr.astype(BF16)
    return hi, jnp.concatenate([hi, (wr - hi.astype(F32)).astype(BF16)], axis=1)


def _route_init(carry_ref):
    @pl.when(pl.program_id(0) == 0)
    def _():
        carry_ref[...] = jnp.zeros_like(carry_ref)


def _route_tile(h, nf_ref, wrh_ref, wr2_ref, br_ref, info_ref, cnt_ref, carry_ref):
    xn = _rms(h, nf_ref[...])
    xh = xn.astype(BF16)
    xl = (xn - xh.astype(F32)).astype(BF16)
    both = _dot(xh, wr2_ref[...])
    logits = both[:, :LANES] + (_dot(xl, wrh_ref[...]) + both[:, LANES:]) + br_ref[...]
    lane = lax.broadcasted_iota(jnp.int32, (TM, LANES), 1).astype(F32)

    def first_max(vals):
        vmax = jnp.max(vals, axis=-1, keepdims=True)
        idx = jnp.min(jnp.where(vals == vmax, lane, float(LANES)), axis=-1, keepdims=True)
        return vmax, idx

    lg = jnp.where(lane < N_EGROUPS, logits, -jnp.inf)
    gmax, g_idx = first_max(lg)
    g_p = 1.0 / jnp.sum(jnp.exp(lg - gmax), axis=-1, keepdims=True)
    e_lo = N_EGROUPS + EXPERTS_PER_GROUP * g_idx
    le = jnp.where((lane >= e_lo) & (lane < e_lo + EXPERTS_PER_GROUP), logits, -jnp.inf)
    v1, i1 = first_max(le)
    v2, i2 = first_max(jnp.where(lane == i1, -jnp.inf, le))
    e2 = jnp.exp(v2 - v1)
    w1 = (1.0 / (1.0 + e2)) * g_p
    w2 = (e2 / (1.0 + e2)) * g_p
    a1 = i1 - e_lo
    a2 = i2 - e_lo
    lo = jnp.minimum(a1, a2)
    hi = jnp.maximum(a1, a2)
    pair = jnp.where(lo == 0.0, hi - 1.0, jnp.where(lo == 1.0, jnp.where(hi == 3.0, 3.0, 4.0), 5.0))
    ea = jnp.where(pair < 3.0, 0.0, jnp.where(pair < 5.0, 1.0, 3.0))
    ga = jnp.where(a1 == ea, w1, w2)
    gb = jnp.where(a1 == ea, w2, w1)
    bucket = g_idx * N_PAIRS + pair

    onehot = (lane == bucket).astype(F32)
    r = lax.broadcasted_iota(jnp.int32, (TM, TM), 0)
    c = lax.broadcasted_iota(jnp.int32, (TM, TM), 1)
    before = _dot((c < r).astype(BF16), onehot.astype(BF16))
    carry = carry_ref[0:1, :]
    rank = jnp.sum(onehot * (before + carry), axis=-1, keepdims=True)
    new_carry = carry + jnp.sum(onehot, axis=0, keepdims=True)
    carry_ref[...] = jnp.broadcast_to(new_carry, carry_ref.shape)
    cnt_ref[...] = jnp.broadcast_to(new_carry, cnt_ref.shape)
    info = jnp.where(lane == 0.0, bucket,
                     jnp.where(lane == 1.0, rank, jnp.where(lane == 2.0, ga, jnp.where(lane == 3.0, gb, 0.0))))
    info_ref[...] = info.T[:SUBLANES]


def _router(h_prompt, h_dec, nf, wr, br):
    const = lambda *shape: pl.BlockSpec(shape, lambda i: (0,) * len(shape))
    return pl.pallas_call(
        _router_body,
        grid=(N_TILES,),
        in_specs=[pl.BlockSpec((TM, D_MODEL), lambda i: (jnp.minimum(i, N_TILES - 2), 0)), const(N_DEC, D_MODEL),
                  const(1, D_MODEL), const(D_MODEL, LANES), const(D_MODEL, 2 * LANES), const(1, LANES)],
        out_specs=[pl.BlockSpec((TM * ROW_TILES, LANES), lambda i: (i, 0)),
                   pl.BlockSpec((SUBLANES, TM), lambda i: (0, i)),
                   const(SUBLANES, LANES)],
        out_shape=[jax.ShapeDtypeStruct((T * ROW_TILES, LANES), F32), jax.ShapeDtypeStruct((SUBLANES, T), F32),
                   jax.ShapeDtypeStruct((SUBLANES, LANES), F32)],
        scratch_shapes=[pltpu.VMEM((SUBLANES, LANES), F32)],
        compiler_params=_cparams(),
        name="moe_router",
    )(h_prompt, h_dec, nf, *_router_split(wr), br)


GATHER_UNROLL = 32


def _gather_rows_start(idx_ref, base, src_ref, dst_ref, sem, n):
    def group(g, carry):
        for u in range(GATHER_UNROLL):
            r = g * GATHER_UNROLL + u
            src = pl.multiple_of(idx_ref[base + r] * ROW_TILES, ROW_TILES)
            dst = pl.multiple_of(r * ROW_TILES, ROW_TILES)
            pltpu.make_async_copy(src_ref.at[pl.ds(src, ROW_TILES), :], dst_ref.at[pl.ds(dst, ROW_TILES), :],
                                  sem).start(priority=u % 2)
        return carry

    lax.fori_loop(0, n // GATHER_UNROLL, group, 0)


def _gather_rows_wait(src_ref, dst_ref, sem, n):
    pltpu.make_async_copy(src_ref.at[pl.ds(0, n * ROW_TILES), :], dst_ref.at[pl.ds(0, n * ROW_TILES), :], sem).wait()


def _gather_tile(idx_ref, src_ref, buf, sem, n, n_live=None):
    i = pl.program_id(0)
    slot = lax.rem(i, 2)
    n_live = pl.num_programs(0) if n_live is None else n_live

    @pl.when(i == 0)
    def _():
        _gather_rows_start(idx_ref, 0, src_ref, buf.at[0], sem.at[0], n)

    @pl.when(i + 1 < n_live)
    def _():
        _gather_rows_start(idx_ref, (i + 1) * n, src_ref, buf.at[1 - slot], sem.at[1 - slot], n)

    @pl.when(i < n_live)
    def _():
        _gather_rows_wait(src_ref, buf.at[slot], sem.at[slot], n)

    return buf.at[slot]


def _gather_scratch(n):
    return [pltpu.VMEM((2, n * ROW_TILES, LANES), F32), pltpu.SemaphoreType.DMA((2,))]


def _moe_body(idx_ref, ea_ref, eb_ref, cha_ref, chb_ref, nlive_ref, rows_hbm, g_ref, nf_ref,
              w1a_ref, w3a_ref, w2a_ref, w1b_ref, w3b_ref, w2b_ref, out_ref,
              s1a, s3a, s2a, s1b, s3b, s2b, xbuf, xsem):
    i = pl.program_id(0)
    live = i < nlive_ref[0]
    x_ref = _gather_tile(idx_ref, rows_hbm, xbuf, xsem, TM, nlive_ref[0])

    @pl.when(cha_ref[i] == 1)
    def _():
        s1a[...] = w1a_ref[...].astype(BF16)
        s3a[...] = w3a_ref[...].astype(BF16)
        s2a[...] = w2a_ref[...].astype(BF16)

    @pl.when(chb_ref[i] == 1)
    def _():
        s1b[...] = w1b_ref[...].astype(BF16)
        s3b[...] = w3b_ref[...].astype(BF16)
        s2b[...] = w2b_ref[...].astype(BF16)

    @pl.when(live)
    def _():
        h = _load_rows(x_ref, TM)
        xn = _rms(h, nf_ref[...]).astype(BF16)
        g = g_ref[...]

        def ffn(w1, w3, w2):
            a = _dot(xn, w1[...])
            hdn = (a * (1.0 / (1.0 + jnp.exp(-a)))) * _dot(xn, w3[...])
            return _dot(hdn.astype(BF16), w2[...])

        y = g[:, 0:1] * ffn(s1a, s3a, s2a) + g[:, 1:2] * ffn(s1b, s3b, s2b)
        _store_rows(out_ref, h + y, TM)

    @pl.when(jnp.logical_not(live))
    def _():
        out_ref[...] = jnp.zeros_like(out_ref)


def _moe_ffn(layer, idx_sorted, ea, eb, cha, chb, n_live, rows, gates, nf, w1, w3, w2):
    wa = lambda shape: pl.BlockSpec((None, None) + shape, lambda i, ix, ea, eb, ca, cb, va: (layer, ea[i], 0, 0))
    wb = lambda shape: pl.BlockSpec((None, None) + shape, lambda i, ix, ea, eb, ca, cb, va: (layer, eb[i], 0, 0))
    up, down = (D_MODEL, D_EXPERT), (D_EXPERT, D_MODEL)
    return pl.pallas_call(
        _moe_body,
        grid_spec=pltpu.PrefetchScalarGridSpec(
            num_scalar_prefetch=6,
            grid=(MOE_TILES,),
            in_specs=[
                pl.BlockSpec(memory_space=pl.ANY),
                pl.BlockSpec((TM, 2), lambda i, *_: (i, 0)),
                pl.BlockSpec((1, D_MODEL), lambda i, *_: (0, 0)),
                wa(up), wa(up), wa(down), wb(up), wb(up), wb(down),
            ],
            out_specs=pl.BlockSpec((TM * ROW_TILES, LANES), lambda i, *_: (i, 0)),
            scratch_shapes=[pltpu.VMEM(up, BF16), pltpu.VMEM(up, BF16), pltpu.VMEM(down, BF16),
                            pltpu.VMEM(up, BF16), pltpu.VMEM(up, BF16), pltpu.VMEM(down, BF16)]
            + _gather_scratch(TM),
        ),
        out_shape=jax.ShapeDtypeStruct((P_ROWS * ROW_TILES, LANES), F32),
        compiler_params=_cparams(),
        name="moe_ffn",
    )(idx_sorted, ea, eb, cha, chb, n_live, rows, gates, nf, w1, w3, w2, w1, w3, w2)


def _router_weights(layer, w_group, b_group, w_expert, b_expert):
    wr = jnp.zeros((D_MODEL, LANES), F32)
    wr = wr.at[:, :N_EGROUPS].set(w_group[layer]).at[:, N_EGROUPS:N_EGROUPS + N_EXPERTS].set(w_expert[layer])
    br = jnp.zeros((1, LANES), F32)
    br = br.at[0, :N_EGROUPS].set(b_group[layer]).at[0, N_EGROUPS:N_EGROUPS + N_EXPERTS].set(b_expert[layer])
    return wr, br


def _moe_layer(layer, rows, info, cnt, nf, w1, w3, w2):
    bucket = info[0].astype(jnp.int32)
    rank = info[1].astype(jnp.int32)
    counts = cnt[0, :N_BUCKETS].astype(jnp.int32)
    n_tiles = (counts + TM - 1) // TM
    tile_end = jnp.cumsum(n_tiles)
    tile_start = tile_end - n_tiles
    start_of = jnp.sum(jnp.where(bucket[:, None] == jnp.arange(N_BUCKETS)[None, :], tile_start[None, :], 0), axis=1)
    pos = start_of * TM + rank
    packed = jnp.stack([jnp.arange(T, dtype=F32), info[2], info[3]], axis=0)
    base = jnp.zeros((3, P_ROWS), F32).at[0].set((jnp.arange(P_ROWS) % T).astype(F32))
    packed = base.at[:, pos].set(packed, unique_indices=True, indices_are_sorted=False, mode="promise_in_bounds")
    idx_sorted = packed[0].astype(jnp.int32)
    gates = packed[1:3].T
    total = tile_end[-1]
    j = jnp.minimum(jnp.arange(MOE_TILES), total - 1)
    tb = jnp.sum((j[:, None] >= tile_end[None, :]).astype(jnp.int32), axis=1)
    grp, pair = tb // N_PAIRS, tb % N_PAIRS
    ea = (grp * EXPERTS_PER_GROUP + jnp.asarray(PAIR_A, jnp.int32)[pair]).astype(jnp.int32)
    eb = (grp * EXPERTS_PER_GROUP + jnp.asarray(PAIR_B, jnp.int32)[pair]).astype(jnp.int32)
    first = jnp.arange(MOE_TILES) == 0
    cha = (first | (ea != jnp.roll(ea, 1))).astype(jnp.int32)
    chb = (first | (eb != jnp.roll(eb, 1))).astype(jnp.int32)
    n_live = total.reshape(1).astype(jnp.int32)
    return _moe_ffn(layer, idx_sorted, ea, eb, cha, chb, n_live, rows, gates, nf, w1, w3, w2), pos


VT_ROWS = LANES + 16


def _seg_matrix():
    lane = np.arange(LANES)
    seg = np.where(lane < NOPE_DIM, 0, np.where(lane < NOPE_DIM + ROPE_DIM, 1, 2))
    return jnp.asarray(seg[:, None] == seg[None, :], BF16)


def _seg_count():
    lane = np.arange(LANES)
    return jnp.asarray(np.where(lane < NOPE_DIM, 1.0 / NOPE_DIM, 1.0 / ROPE_DIM), F32).reshape(1, LANES)


def _rope_swap(x):
    lane = lax.broadcasted_iota(jnp.int32, x.shape, 1)
    return jnp.where(lane < ROPE_LO + ROPE_HALF, pltpu.roll(x, LANES - ROPE_HALF, 1), pltpu.roll(x, ROPE_HALF, 1))


def _expand_k(cb, kr, wuk_ref, seg, cnt, kg_ref, k_ref):
    kn = _dot(cb, wuk_ref[...])
    heads = [kn[:, hh * HEAD_W:(hh + 1) * HEAD_W] for hh in range(N_HEADS)]
    ms = [_split_dot(x * x, seg) * cnt for x in heads]
    for hh in range(N_HEADS):
        k_ref[:, hh * HEAD_W:(hh + 1) * HEAD_W] = (heads[hh] * lax.rsqrt(ms[hh] + EPS) * kg_ref[...] + kr).astype(BF16)


def _mla_proj_body(pos_ref, sorted_hbm, cos_ref, sin_ref, cost_ref, sint_ref, kvn_ref, wdkv_ref, kvan_ref, krg_ref,
                   wuk_ref, wuv_ref, wuvt_ref, kg_ref, nmq_ref, wdq_ref, qan_ref, wuqt_ref, qg_ref, seg_ref, cnt_ref,
                   rows_ref, ckvp_ref, ckvd_ref, krp_ref, krd_ref, k_ref, vdec_ref, vt_ref, qt_ref, qdec_ref,
                   xbuf, xsem):
    is_dec = pl.program_id(0) == N_TILES - 1
    x_ref = _gather_tile(pos_ref, sorted_hbm, xbuf, xsem, TM)
    rows_ref[...] = x_ref[...]
    h = _load_rows(x_ref, TM)
    c = _dot(_rms(h, kvn_ref[...]).astype(BF16), wdkv_ref[...])
    ckv = _rms(c[:, :KV_RANK], kvan_ref[...])
    kr = c[:, KV_RANK:]
    kr = kr * lax.rsqrt(jnp.sum(kr * kr, axis=-1, keepdims=True) * (1.0 / ROPE_DIM) + EPS) * krg_ref[...]
    kr = kr * cos_ref[...] + _rope_swap(kr) * sin_ref[...]
    cb = ckv.astype(BF16)
    _expand_k(cb, kr, wuk_ref, seg_ref[...], cnt_ref[...], kg_ref, k_ref)
    vt = _dot(wuvt_ref[...], ckv.T.astype(BF16)).astype(BF16)
    for pr in range(N_HEADS // 2):
        vt_ref[0, pr * VT_ROWS:pr * VT_ROWS + LANES, :] = vt[pr * LANES:(pr + 1) * LANES]
        vt_ref[0, pr * VT_ROWS + LANES:(pr + 1) * VT_ROWS, :] = jnp.ones((VT_ROWS - LANES, TM), BF16)
    cq = _rms(_dot(_rms(h, nmq_ref[...]).astype(BF16), wdq_ref[...]), qan_ref[...])
    qt = _dot(wuqt_ref[...], cq.T.astype(BF16))
    cost, sint, qg = cost_ref[0], sint_ref[0], qg_ref[...]
    for hh in range(N_HEADS):
        x = qt[hh * HEAD_W:(hh + 1) * HEAD_W, :]
        xn, xr = x[:NOPE_DIM], x[ROPE_LO:ROPE_LO + ROPE_DIM]
        xn = xn * lax.rsqrt(jnp.mean(xn * xn, axis=0, keepdims=True) + EPS) * qg[:NOPE_DIM]
        xr = xr * lax.rsqrt(jnp.mean(xr * xr, axis=0, keepdims=True) + EPS) * qg[ROPE_LO:ROPE_LO + ROPE_DIM]
        x1, x2 = xr[:ROPE_HALF], xr[ROPE_HALF:]
        qh = jnp.concatenate([xn, x1 * cost - x2 * sint, x1 * sint + x2 * cost,
                              jnp.zeros((HEAD_W - NOPE_DIM - ROPE_DIM, TM), F32)], axis=0) * Q_SCALE
        qt_ref[0, hh * HEAD_W:(hh + 1) * HEAD_W, :] = qh.astype(BF16)

    @pl.when(jnp.logical_not(is_dec))
    def _():
        ckvp_ref[...] = ckv
        krp_ref[...] = kr[:, ROPE_LO:ROPE_LO + ROPE_DIM]

    @pl.when(is_dec)
    def _():
        ckvd_ref[...] = ckv
        krd_ref[...] = kr[:, ROPE_LO:ROPE_LO + ROPE_DIM]
        vdec_ref[...] = _dot(cb, wuv_ref[...]).astype(BF16)
        for hh in range(N_HEADS):
            rows = slice(hh * HEAD_W, (hh + 1) * HEAD_W)
            qdec_ref[:, rows] = qt_ref[0, rows, :].astype(F32).T.astype(BF16)


def _mla_proj(pos, sorted_rows, cos_t, sin_t, cos_tt, sin_tt, kvn, wdkv, kvan, krg, wuk, wuv, kg, nmq, wdq, qan, wuq, qg):
    const = lambda *shape: pl.BlockSpec(shape, lambda i, p: (0,) * len(shape))
    tab_tile = lambda i: jnp.where(i < N_PROMPT // TM, i % (SEQ // TM), SEQ // TM)
    tab = pl.BlockSpec((TM, LANES), lambda i, p: (tab_tile(i), 0))
    tab_t = pl.BlockSpec((1, ROPE_HALF, TM), lambda i, p: (tab_tile(i), 0, 0))
    row = lambda w: pl.BlockSpec((TM, w), lambda i, p: (i, 0))
    prow = lambda w: pl.BlockSpec((TM, w), lambda i, p: (jnp.minimum(i, N_TILES - 2), 0))
    return pl.pallas_call(
        _mla_proj_body,
        grid_spec=pltpu.PrefetchScalarGridSpec(
            num_scalar_prefetch=1,
            grid=(N_TILES,),
            in_specs=[
                pl.BlockSpec(memory_space=pl.ANY), tab, tab, tab_t, tab_t,
                const(1, D_MODEL), const(D_MODEL, KV_RANK + LANES), const(1, KV_RANK), const(1, LANES),
                const(KV_RANK, N_HEADS * HEAD_W), const(KV_RANK, N_HEADS * V_DIM), const(N_HEADS * V_DIM, KV_RANK),
                const(1, LANES),
                const(1, D_MODEL), const(D_MODEL, Q_RANK), const(1, Q_RANK), const(N_HEADS * HEAD_W, Q_RANK),
                const(HEAD_W, TM), const(LANES, LANES), const(1, LANES),
            ],
            out_specs=[pl.BlockSpec((TM * ROW_TILES, LANES), lambda i, p: (i, 0)),
                       prow(KV_RANK), const(N_DEC, KV_RANK), prow(ROPE_DIM), const(N_DEC, ROPE_DIM),
                       row(N_HEADS * HEAD_W), const(N_DEC, N_HEADS * V_DIM),
                       pl.BlockSpec((1, N_HEADS // 2 * VT_ROWS, TM), lambda i, p: (i, 0, 0)),
                       pl.BlockSpec((1, N_HEADS * HEAD_W, TM), lambda i, p: (i, 0, 0)),
                       const(N_DEC, N_HEADS * HEAD_W)],
            scratch_shapes=_gather_scratch(TM),
        ),
        out_shape=[
            jax.ShapeDtypeStruct((T * ROW_TILES, LANES), F32),
            jax.ShapeDtypeStruct((N_PROMPT, KV_RANK), F32), jax.ShapeDtypeStruct((N_DEC, KV_RANK), F32),
            jax.ShapeDtypeStruct((N_PROMPT, ROPE_DIM), F32), jax.ShapeDtypeStruct((N_DEC, ROPE_DIM), F32),
            jax.ShapeDtypeStruct((T, N_HEADS * HEAD_W), BF16), jax.ShapeDtypeStruct((N_DEC, N_HEADS * V_DIM), BF16),
            jax.ShapeDtypeStruct((N_TILES, N_HEADS // 2 * VT_ROWS, TM), BF16),
            jax.ShapeDtypeStruct((N_TILES, N_HEADS * HEAD_W, TM), BF16),
            jax.ShapeDtypeStruct((N_DEC, N_HEADS * HEAD_W), BF16),
        ],
        compiler_params=_cparams(),
        name="mla_proj",
    )(pos, sorted_rows, cos_t, sin_t, cos_tt, sin_tt, kvn, wdkv, kvan, krg, wuk, wuv, wuv.T, kg, nmq, wdq, qan,
      wuq.T, jnp.broadcast_to(qg.reshape(HEAD_W, 1), (HEAD_W, TM)), _seg_matrix(), _seg_count())


def _cache_kv_body(ckv_ref, kr_ref, place_ref, wuk_ref, wuv_ref, kg_ref, seg_ref, cnt_ref, k_ref, v_ref):
    kr = _dot(kr_ref[...].astype(BF16), place_ref[...])
    cb = ckv_ref[...].astype(BF16)
    _expand_k(cb, kr, wuk_ref, seg_ref[...], cnt_ref[...], kg_ref, k_ref)
    v_ref[...] = _dot(cb, wuv_ref[...]).astype(BF16)


CACHE_ROWS = 1024
assert PAST_LEN % CACHE_ROWS == 0


def _cache_kv(ckv, kr, wuk, wuv, kg):
    n = ckv.shape[0]
    place = jnp.asarray(np.arange(ROPE_DIM)[:, None] + ROPE_LO == np.arange(LANES)[None, :], BF16)
    const = lambda *shape: pl.BlockSpec(shape, lambda i: (0,) * len(shape))
    row = lambda w: pl.BlockSpec((CACHE_ROWS, w), lambda i: (i, 0))
    return pl.pallas_call(
        _cache_kv_body,
        grid=(n // CACHE_ROWS,),
        in_specs=[row(KV_RANK), row(ROPE_DIM), const(ROPE_DIM, LANES), const(KV_RANK, N_HEADS * HEAD_W),
                  const(KV_RANK, N_HEADS * V_DIM), const(1, LANES), const(LANES, LANES), const(1, LANES)],
        out_specs=[row(N_HEADS * HEAD_W), row(N_HEADS * V_DIM)],
        out_shape=[jax.ShapeDtypeStruct((n, N_HEADS * HEAD_W), BF16), jax.ShapeDtypeStruct((n, N_HEADS * V_DIM), BF16)],
        compiler_params=_cparams(),
        name="cache_kv",
    )(ckv, kr, place, wuk, wuv, kg, _seg_matrix(), _seg_count())


TQ = 256
TK = 256
assert TQ == TK and TQ % CHUNK == 0
SCORE_LOOKAHEAD = 4


def _qk(q, k):
    return lax.dot_general(q, k, (((1,), (1,)), ((), ())), preferred_element_type=F32)


def _merge_heads(o_ref, outs, rows):
    lane = lax.broadcasted_iota(jnp.int32, (rows, LANES), 1)
    for pr in range(N_HEADS // 2):
        o_ref[:, pr * LANES:(pr + 1) * LANES] = jnp.where(lane < V_DIM, outs[2 * pr], outs[2 * pr + 1]).astype(BF16)


def _prompt_attn_body(qt_ref, k_ref, vt_ref, rows_ref, wo_ref, out_ref, m_scr, acc_scr):
    qi = pl.program_id(1)
    m_scr[...] = jnp.full(m_scr.shape, NEG, F32)
    acc_scr[...] = jnp.zeros(acc_scr.shape, F32)

    def all_heads(j, mask):
        ks = pl.ds(pl.multiple_of(j * TK, TK), TK)

        def scores(hh):
            hcols = slice(hh * HEAD_W, (hh + 1) * HEAD_W)
            return _dot(k_ref[ks, hcols], qt_ref[0, hcols, :])

        ahead = [scores(hh) for hh in range(SCORE_LOOKAHEAD)]
        for hh in range(N_HEADS):
            vrows = slice((hh // 2) * VT_ROWS, (hh // 2 + 1) * VT_ROWS)
            s = ahead.pop(0)
            if hh + SCORE_LOOKAHEAD < N_HEADS:
                ahead.append(scores(hh + SCORE_LOOKAHEAD))
            if mask is not None:
                s = jnp.where(mask, s, NEG)
            m_old = m_scr[hh]
            m_new = jnp.maximum(m_old, jnp.max(s, axis=0, keepdims=True))
            p = jnp.exp2(s - m_new).astype(BF16)
            m_scr[hh] = m_new
            acc_scr[hh] = jnp.exp2(m_old - m_new) * acc_scr[hh] + _dot(vt_ref[j, vrows, :], p)

    def step(j, carry):
        all_heads(j, None)
        return carry

    lax.fori_loop(0, qi, step, 0)
    kc = lax.broadcasted_iota(jnp.int32, (TK, TQ), 0) // CHUNK
    qc = lax.broadcasted_iota(jnp.int32, (TK, TQ), 1) // CHUNK
    all_heads(qi, kc <= qc)
    row = lax.broadcasted_iota(jnp.int32, (LANES, TQ), 0)
    pairs = []
    for pr in range(N_HEADS // 2):
        even = acc_scr[2 * pr, :LANES] / acc_scr[2 * pr, LANES:LANES + 1]
        odd = acc_scr[2 * pr + 1, :LANES] / acc_scr[2 * pr + 1, LANES:LANES + 1]
        pairs.append(jnp.where(row < V_DIM, even, odd))
    o = jnp.concatenate(pairs, axis=0).T.astype(BF16)
    out_ref[...] = _load_rows(rows_ref, TQ) + _dot(o, wo_ref[...])


def _prompt_attn(qt, k, vt, rows, wo):
    nq = SEQ // TQ
    return pl.pallas_call(
        _prompt_attn_body,
        grid=(BATCH, nq),
        in_specs=[
            pl.BlockSpec((1, N_HEADS * HEAD_W, TQ), lambda b, i: (b * nq + i, 0, 0)),
            pl.BlockSpec((SEQ, N_HEADS * HEAD_W), lambda b, i: (b, 0)),
            pl.BlockSpec((SEQ // TK, N_HEADS // 2 * VT_ROWS, TK), lambda b, i: (b, 0, 0)),
            pl.BlockSpec((TQ * ROW_TILES, LANES), lambda b, i: (b * nq + i, 0)),
            pl.BlockSpec((N_HEADS * V_DIM, D_MODEL), lambda b, i: (0, 0)),
        ],
        out_specs=pl.BlockSpec((TQ, D_MODEL), lambda b, i: (b * nq + i, 0)),
        out_shape=jax.ShapeDtypeStruct((N_PROMPT, D_MODEL), F32),
        scratch_shapes=[pltpu.VMEM((N_HEADS, 1, TQ), F32), pltpu.VMEM((N_HEADS, VT_ROWS, TQ), F32)],
        compiler_params=_cparams(2),
        name="prompt_attn",
    )(qt, k, vt, rows, wo)


def _sample_attn_body(q_ref, kc_ref, vc_ref, kn_ref, vn_ref, rows_ref, wo_ref, out_ref, o_scr):
    outs = []
    for hh in range(N_HEADS):
        hcols = slice(hh * HEAD_W, (hh + 1) * HEAD_W)
        vcols = slice((hh // 2) * LANES, (hh // 2 + 1) * LANES)
        q = q_ref[:, hcols]
        sc = _qk(q, kc_ref[:, hcols])
        sn = _qk(q, kn_ref[:, hcols])
        m = jnp.maximum(jnp.max(sc, axis=-1, keepdims=True), jnp.max(sn, axis=-1, keepdims=True))
        pc = jnp.exp2(sc - m)
        pn = jnp.exp2(sn - m)
        l = jnp.sum(pc, axis=-1, keepdims=True) + jnp.sum(pn, axis=-1, keepdims=True)
        acc = _dot(pc.astype(BF16), vc_ref[:, vcols]) + _dot(pn.astype(BF16), vn_ref[:, vcols])
        outs.append(acc / l)
    _merge_heads(o_scr, outs, DEC_SEQ)
    out_ref[...] = _load_rows(rows_ref, DEC_SEQ) + _dot(o_scr[...], wo_ref[...])


def _sample_attn(q, kc, vc, kn, vn, rows, wo):
    off = N_PROMPT // DEC_SEQ
    return pl.pallas_call(
        _sample_attn_body,
        grid=(DEC_BATCH,),
        in_specs=[
            pl.BlockSpec((DEC_SEQ, N_HEADS * HEAD_W), lambda b: (b, 0)),
            pl.BlockSpec((PAST_LEN, N_HEADS * HEAD_W), lambda b: (b, 0)),
            pl.BlockSpec((PAST_LEN, N_HEADS * V_DIM), lambda b: (b, 0)),
            pl.BlockSpec((DEC_SEQ, N_HEADS * HEAD_W), lambda b: (off + b, 0)),
            pl.BlockSpec((DEC_SEQ, N_HEADS * V_DIM), lambda b: (b, 0)),
            pl.BlockSpec((DEC_SEQ * ROW_TILES, LANES), lambda b: (off + b, 0)),
            pl.BlockSpec((N_HEADS * V_DIM, D_MODEL), lambda b: (0, 0)),
        ],
        out_specs=pl.BlockSpec((DEC_SEQ, D_MODEL), lambda b: (b, 0)),
        out_shape=jax.ShapeDtypeStruct((N_DEC, D_MODEL), F32),
        scratch_shapes=[pltpu.VMEM((DEC_SEQ, N_HEADS * V_DIM), BF16)],
        compiler_params=_cparams(),
        name="sample_attn",
    )(q, kc, vc, kn, vn, rows, wo)


def _finish_body(pos_ref, sorted_hbm, yp_ref, ys_ref, xbuf, xsem):
    i = pl.program_id(0)
    y = _load_rows(_gather_tile(pos_ref, sorted_hbm, xbuf, xsem, TM), TM)

    @pl.when(i < N_TILES - 1)
    def _():
        yp_ref[...] = y

    @pl.when(i == N_TILES - 1)
    def _():
        ys_ref[...] = y


def _finish(pos, sorted_rows):
    return pl.pallas_call(
        _finish_body,
        grid_spec=pltpu.PrefetchScalarGridSpec(
            num_scalar_prefetch=1,
            grid=(N_TILES,),
            in_specs=[pl.BlockSpec(memory_space=pl.ANY)],
            out_specs=[pl.BlockSpec((TM, D_MODEL), lambda i, p: (jnp.minimum(i, N_TILES - 2), 0)),
                       pl.BlockSpec((N_DEC, D_MODEL), lambda i, p: (0, 0))],
            scratch_shapes=_gather_scratch(TM),
        ),
        out_shape=[jax.ShapeDtypeStruct((N_PROMPT, D_MODEL), F32), jax.ShapeDtypeStruct((N_DEC, D_MODEL), F32)],
        compiler_params=_cparams(),
        name="finish",
    )(pos, sorted_rows)


def _rope_tables():
    half = ROPE_DIM // 2
    inv_freq = ROPE_THETA ** (-jnp.arange(half, dtype=F32) / half)
    dec_pos = PAST_LEN + jnp.tile(jnp.arange(DEC_SEQ, dtype=jnp.int32), DEC_BATCH)
    pos = jnp.concatenate([jnp.arange(SEQ, dtype=jnp.int32), dec_pos])
    ang = pos.astype(F32)[:, None] * inv_freq[None, :]
    cos, sin = jnp.cos(ang), jnp.sin(ang)
    n = pos.shape[0]
    cos_t = jnp.ones((n, LANES), F32).at[:, ROPE_LO:ROPE_LO + ROPE_DIM].set(jnp.concatenate([cos, cos], axis=1))
    sin_t = jnp.zeros((n, LANES), F32).at[:, ROPE_LO:ROPE_LO + ROPE_DIM].set(jnp.concatenate([-sin, sin], axis=1))
    to_tiles = lambda a: a.reshape(n // TM, TM, half).transpose(0, 2, 1)
    return cos_t, sin_t, to_tiles(cos), to_tiles(sin)


def _on_lanes(vec, lo):
    return jnp.zeros((1, LANES), F32).at[0, lo:lo + vec.shape[0]].set(vec)


def kernel(x_prompt, x_sample, cache_ckv, cache_krope, norm_mix, norm_ffn, gm_w_in, gm_b_in, gm_ln_g, gm_ln_b, gm_w_s, gm_b_s, gm_w_out, gm_b_out, kv_norm, w_dkv, kv_a_norm, k_rope_norm, w_uk, w_uv, k_nope_norm, w_dq, q_a_norm, w_uq, q_nope_norm, q_rope_norm, w_o, moe_w_group, moe_b_group, moe_w_expert, moe_b_expert, moe_w1, moe_w3, moe_w2):
    nf0, nf1 = norm_ffn[0].reshape(1, D_MODEL), norm_ffn[1].reshape(1, D_MODEL)

    idx = np.arange(GMLP_BLOCK)
    allowed = (idx[None, :] // CHUNK) <= (idx[:, None] // CHUNK)
    ws_p = jnp.where(allowed[None], gm_w_s[0], 0.0).astype(BF16)
    same_seq = (idx[None, :] // DEC_SEQ) == (idx[:, None] // DEC_SEQ)
    ws_d = jnp.where(same_seq[None], jnp.tile(gm_w_s[0][:, :DEC_SEQ, :DEC_SEQ], (1, GMLP_BLOCK // DEC_SEQ, GMLP_BLOCK // DEC_SEQ)), 0.0).astype(BF16)
    bs_p = gm_b_s[0][:, :, None]
    bs_d = jnp.tile(gm_b_s[0][:, :DEC_SEQ], (1, GMLP_BLOCK // DEC_SEQ))[:, :, None]
    rows, v_rows, info, cnt = _gmlp_layer(
        x_prompt.reshape(N_PROMPT, D_MODEL), x_sample.reshape(N_DEC, D_MODEL),
        norm_mix[0].reshape(1, -1), gm_w_in[0].astype(BF16), gm_b_in[0].reshape(1, -1),
        gm_ln_g[0].reshape(1, -1), gm_ln_b[0].reshape(1, -1), ws_p, ws_d, bs_p, bs_d,
        gm_w_out[0].astype(BF16), gm_b_out[0].reshape(1, -1),
        nf0, *_router_weights(0, moe_w_group, moe_b_group, moe_w_expert, moe_b_expert))
    sorted_rows, pos = _moe_layer(0, rows, info, cnt, nf0, moe_w1, moe_w3, moe_w2)

    cos_t, sin_t, cos_tt, sin_tt = _rope_tables()
    wdkv = jnp.zeros((D_MODEL, KV_RANK + LANES), F32).at[:, :KV_RANK].set(w_dkv[:, :KV_RANK])
    wdkv = wdkv.at[:, KV_RANK + ROPE_LO:KV_RANK + ROPE_LO + ROPE_DIM].set(w_dkv[:, KV_RANK:]).astype(BF16)
    wuk = jnp.zeros((KV_RANK, N_HEADS, HEAD_W), F32).at[:, :, :NOPE_DIM].set(w_uk).reshape(KV_RANK, -1).astype(BF16)
    wuv = w_uv.reshape(KV_RANK, -1).astype(BF16)
    wuq = jnp.zeros((Q_RANK, N_HEADS, HEAD_W), F32).at[:, :, :NOPE_DIM + ROPE_DIM].set(w_uq[0]).reshape(Q_RANK, -1).astype(BF16)
    kg = _on_lanes(k_nope_norm, 0)
    krg = _on_lanes(k_rope_norm, ROPE_LO)
    qg = _on_lanes(jnp.concatenate([q_nope_norm[0], q_rope_norm[0]]), 0)
    rows, ckv_p, ckv_d, krope_p, krope_d, k_new, v_dec, vt_new, qt, q_dec = _mla_proj(
        pos, sorted_rows, cos_t, sin_t, cos_tt, sin_tt, kv_norm.reshape(1, -1), wdkv, kv_a_norm.reshape(1, -1), krg, wuk, wuv, kg,
        norm_mix[1].reshape(1, -1), w_dq[0].astype(BF16), q_a_norm[0].reshape(1, -1), wuq, qg)
    k_cache, v_cache = _cache_kv(cache_ckv.reshape(-1, KV_RANK), cache_krope.reshape(-1, ROPE_DIM), wuk, wuv, kg)

    wo = w_o[0].astype(BF16)
    h_prompt = _prompt_attn(qt, k_new, vt_new, rows, wo)
    h_dec = _sample_attn(q_dec, k_cache, v_cache, k_new, v_dec, rows, wo)
    rows, info, cnt = _router(h_prompt, h_dec, nf1,
                              *_router_weights(1, moe_w_group, moe_b_group, moe_w_expert, moe_b_expert))
    sorted_rows, pos = _moe_layer(1, rows, info, cnt, nf1, moe_w1, moe_w3, moe_w2)
    y_prompt, y_sample = _finish(pos, sorted_rows)

    return (y_prompt.reshape(BATCH, SEQ, D_MODEL), y_sample.reshape(DEC_BATCH, DEC_SEQ, D_MODEL),
            ckv_p.reshape(BATCH, SEQ, KV_RANK), krope_p.reshape(BATCH, SEQ, ROPE_DIM),
            ckv_d.reshape(DEC_BATCH, DEC_SEQ, KV_RANK), krope_d.reshape(DEC_BATCH, DEC_SEQ, ROPE_DIM),
            v_rows.reshape(1, DEC_BATCH, DEC_SEQ, D_GATE))
```

```python
import functools

import jax
import jax.numpy as jnp
import numpy as np
from jax import lax
from jax.experimental import pallas as pl
from jax.experimental.pallas import tpu as pltpu

F32 = jnp.float32
BF16 = jnp.bfloat16

D_MODEL = 1024
BATCH = 8
SEQ = 2048
DEC_BATCH = 16
DEC_SEQ = 16
PAST_LEN = 2048
CHUNK = 64
GMLP_BLOCK = 128
D_GATE = 2 * D_MODEL
N_SG = 8
SG_W = D_GATE // N_SG
N_HEADS = 8
NOPE_DIM = 64
ROPE_DIM = 32
V_DIM = 64
Q_RANK = 384
KV_RANK = 256
ROPE_THETA = 10000.0
SCALE = (NOPE_DIM + ROPE_DIM) ** -0.5
Q_SCALE = SCALE * float(np.log2(np.e))
N_EGROUPS = 4
EXPERTS_PER_GROUP = 4
N_EXPERTS = N_EGROUPS * EXPERTS_PER_GROUP
D_EXPERT = 512
EPS = 1e-6
NEG = -1e30

LANES = 128
SUBLANES = 8
ROW_TILES = D_MODEL // LANES
assert ROW_TILES == SUBLANES

N_PROMPT = BATCH * SEQ
N_DEC = DEC_BATCH * DEC_SEQ
T = N_PROMPT + N_DEC
TM = 256
assert N_PROMPT % TM == 0 and N_DEC == TM
N_TILES = T // TM
HEAD_W = LANES
ROPE_LO = NOPE_DIM
ROPE_HALF = ROPE_DIM // 2

PAIR_A = (0, 0, 0, 1, 1, 3)
PAIR_B = (1, 2, 3, 3, 2, 2)
N_PAIRS = 6
N_BUCKETS = N_EGROUPS * N_PAIRS
MOE_TILES = (T + N_BUCKETS * (TM - 1) + TM - 1) // TM
P_ROWS = MOE_TILES * TM

VMEM_LIMIT = 56 * 1024 * 1024


def _cparams(n_axes=1, vmem=VMEM_LIMIT):
    return pltpu.CompilerParams(dimension_semantics=("arbitrary",) * n_axes, vmem_limit_bytes=vmem)


def _rms(x, g):
    return x * lax.rsqrt(jnp.mean(x * x, axis=-1, keepdims=True) + EPS) * g


def _load_rows(ref, n):
    return jnp.concatenate([ref[pl.ds(s, n, stride=ROW_TILES), :] for s in range(ROW_TILES)], axis=1)


def _store_rows(ref, x, n):
    for s in range(ROW_TILES):
        ref[pl.ds(s, n, stride=ROW_TILES), :] = x[:, s * LANES:(s + 1) * LANES]


def _dot(a, b):
    return jnp.dot(a, b, preferred_element_type=F32)


def _split_dot(x, m):
    hi = x.astype(BF16)
    lo = (x - hi.astype(F32)).astype(BF16)
    return _dot(hi, m) + _dot(lo, m)


GELU_K1 = float(-2.0 * np.sqrt(2.0 / np.pi) * np.log2(np.e))
GELU_K3 = GELU_K1 * 0.044715


def _gmlp_body(xp_ref, xd_ref, nm_ref, win_ref, bin_ref, lng_ref, lnb_ref, wsp_ref, wsd_ref, bsp_ref, bsd_ref,
               wout_ref, bout_ref, nf_ref, wrh_ref, wr2_ref, br_ref,
               rows_ref, v_ref, info_ref, cnt_ref, gated_ref, carry_ref):
    i = pl.program_id(0)
    is_dec = i == N_TILES - 1
    _route_init(carry_ref)
    x = jnp.where(is_dec, xd_ref[...], xp_ref[...])
    xn = _rms(x, nm_ref[...]).astype(BF16)
    z = _dot(xn, win_ref[...]) + bin_ref[...]
    z = z / (1.0 + jnp.exp2(z * (GELU_K1 + GELU_K3 * (z * z))))
    u = z[:, :D_GATE]
    v = z[:, D_GATE:]
    mu = jnp.mean(v, axis=-1, keepdims=True)
    vc = v - mu
    var = jnp.mean(vc * vc, axis=-1, keepdims=True)
    v = vc * lax.rsqrt(var + EPS) * lng_ref[...] + lnb_ref[...]

    v_ref[...] = v
    vb = v.astype(BF16)
    for g in range(N_SG):
        ws = jnp.where(is_dec, wsd_ref[g], wsp_ref[g])
        bs = jnp.where(is_dec, bsd_ref[g], bsp_ref[g])
        for b in range(TM // GMLP_BLOCK):
            rows = slice(b * GMLP_BLOCK, (b + 1) * GMLP_BLOCK)
            cols = slice(g * SG_W, (g + 1) * SG_W)
            s = _dot(ws, vb[rows, cols]) + bs
            gated_ref[rows, cols] = (u[rows, cols] * s).astype(BF16)
    h = x + _dot(gated_ref[...], wout_ref[...]) + bout_ref[...]
    _store_rows(rows_ref, h, TM)
    _route_tile(h, nf_ref, wrh_ref, wr2_ref, br_ref, info_ref, cnt_ref, carry_ref)


def _gmlp_layer(x_prompt, x_dec, nm, w_in, b_in, ln_g, ln_b, ws_p, ws_d, bs_p, bs_d, w_out, b_out, nf, wr, br):
    const = lambda *shape: pl.BlockSpec(shape, lambda i: (0,) * len(shape))
    return pl.pallas_call(
        _gmlp_body,
        grid=(N_TILES,),
        in_specs=[
            pl.BlockSpec((TM, D_MODEL), lambda i: (jnp.minimum(i, N_TILES - 2), 0)), const(N_DEC, D_MODEL),
            const(1, D_MODEL), const(D_MODEL, 2 * D_GATE), const(1, 2 * D_GATE),
            const(1, D_GATE), const(1, D_GATE),
            const(N_SG, GMLP_BLOCK, GMLP_BLOCK), const(N_SG, GMLP_BLOCK, GMLP_BLOCK),
            const(N_SG, GMLP_BLOCK, 1), const(N_SG, GMLP_BLOCK, 1),
            const(D_GATE, D_MODEL), const(1, D_MODEL),
            const(1, D_MODEL), const(D_MODEL, LANES), const(D_MODEL, 2 * LANES), const(1, LANES),
        ],
        out_specs=[
            pl.BlockSpec((TM * ROW_TILES, LANES), lambda i: (i, 0)),
            const(N_DEC, D_GATE),
            pl.BlockSpec((SUBLANES, TM), lambda i: (0, i)),
            const(SUBLANES, LANES),
        ],
        out_shape=[jax.ShapeDtypeStruct((T * ROW_TILES, LANES), F32), jax.ShapeDtypeStruct((N_DEC, D_GATE), F32),
                   jax.ShapeDtypeStruct((SUBLANES, T), F32), jax.ShapeDtypeStruct((SUBLANES, LANES), F32)],
        scratch_shapes=[pltpu.VMEM((TM, D_GATE), BF16), pltpu.VMEM((SUBLANES, LANES), F32)],
        compiler_params=_cparams(),
        name="gmlp_layer",
    )(x_prompt, x_dec, nm, w_in, b_in, ln_g, ln_b, ws_p, ws_d, bs_p, bs_d, w_out, b_out,
      nf, *_router_split(wr), br)


def _router_body(hp_ref, hd_ref, nf_ref, wrh_ref, wr2_ref, br_ref, rows_ref, info_ref, cnt_ref, carry_ref):
    _route_init(carry_ref)
    h = jnp.where(pl.program_id(0) == N_TILES - 1, hd_ref[...], hp_ref[...])
    _store_rows(rows_ref, h, TM)
    _route_tile(h, nf_ref, wrh_ref, wr2_ref, br_ref, info_ref, cnt_ref, carry_ref)


def _router_split(wr):
    hi = wr.astype(BF16)
    return hi, jnp.concatenate([hi, (wr - hi.astype(F32)).astype(BF16)], axis=1)


def _route_init(carry_ref):
    @pl.when(pl.program_id(0) == 0)
    def _():
        carry_ref[...] = jnp.zeros_like(carry_ref)


def _route_tile(h, nf_ref, wrh_ref, wr2_ref, br_ref, info_ref, cnt_ref, carry_ref):
    xn = _rms(h, nf_ref[...])
    xh = xn.astype(BF16)
    xl = (xn - xh.astype(F32)).astype(BF16)
    both = _dot(xh, wr2_ref[...])
    logits = both[:, :LANES] + (_dot(xl, wrh_ref[...]) + both[:, LANES:]) + br_ref[...]
    lane = lax.broadcasted_iota(jnp.int32, (TM, LANES), 1).astype(F32)

    def first_max(vals):
        vmax = jnp.max(vals, axis=-1, keepdims=True)
        idx = jnp.min(jnp.where(vals == vmax, lane, float(LANES)), axis=-1, keepdims=True)
        return vmax, idx

    lg = jnp.where(lane < N_EGROUPS, logits, -jnp.inf)
    gmax, g_idx = first_max(lg)
    g_p = 1.0 / jnp.sum(jnp.exp(lg - gmax), axis=-1, keepdims=True)
    e_lo = N_EGROUPS + EXPERTS_PER_GROUP * g_idx
    le = jnp.where((lane >= e_lo) & (lane < e_lo + EXPERTS_PER_GROUP), logits, -jnp.inf)
    v1, i1 = first_max(le)
    v2, i2 = first_max(jnp.where(lane == i1, -jnp.inf, le))
    e2 = jnp.exp(v2 - v1)
    w1 = (1.0 / (1.0 + e2)) * g_p
    w2 = (e2 / (1.0 + e2)) * g_p
    a1 = i1 - e_lo
    a2 = i2 - e_lo
    lo = jnp.minimum(a1, a2)
    hi = jnp.maximum(a1, a2)
    pair = jnp.where(lo == 0.0, hi - 1.0, jnp.where(lo == 1.0, jnp.where(hi == 3.0, 3.0, 4.0), 5.0))
    ea = jnp.where(pair < 3.0, 0.0, jnp.where(pair < 5.0, 1.0, 3.0))
    ga = jnp.where(a1 == ea, w1, w2)
    gb = jnp.where(a1 == ea, w2, w1)
    bucket = g_idx * N_PAIRS + pair

    onehot = (lane == bucket).astype(F32)
    r = lax.broadcasted_iota(jnp.int32, (TM, TM), 0)
    c = lax.broadcasted_iota(jnp.int32, (TM, TM), 1)
    before = _dot((c < r).astype(BF16), onehot.astype(BF16))
    carry = carry_ref[0:1, :]
    rank = jnp.sum(onehot * (before + carry), axis=-1, keepdims=True)
    new_carry = carry + jnp.sum(onehot, axis=0, keepdims=True)
    carry_ref[...] = jnp.broadcast_to(new_carry, carry_ref.shape)
    cnt_ref[...] = jnp.broadcast_to(new_carry, cnt_ref.shape)
    info = jnp.where(lane == 0.0, bucket,
                     jnp.where(lane == 1.0, rank, jnp.where(lane == 2.0, ga, jnp.where(lane == 3.0, gb, 0.0))))
    info_ref[...] = info.T[:SUBLANES]


def _router(h_prompt, h_dec, nf, wr, br):
    const = lambda *shape: pl.BlockSpec(shape, lambda i: (0,) * len(shape))
    return pl.pallas_call(
        _router_body,
        grid=(N_TILES,),
        in_specs=[pl.BlockSpec((TM, D_MODEL), lambda i: (jnp.minimum(i, N_TILES - 2), 0)), const(N_DEC, D_MODEL),
                  const(1, D_MODEL), const(D_MODEL, LANES), const(D_MODEL, 2 * LANES), const(1, LANES)],
        out_specs=[pl.BlockSpec((TM * ROW_TILES, LANES), lambda i: (i, 0)),
                   pl.BlockSpec((SUBLANES, TM), lambda i: (0, i)),
                   const(SUBLANES, LANES)],
        out_shape=[jax.ShapeDtypeStruct((T * ROW_TILES, LANES), F32), jax.ShapeDtypeStruct((SUBLANES, T), F32),
                   jax.ShapeDtypeStruct((SUBLANES, LANES), F32)],
        scratch_shapes=[pltpu.VMEM((SUBLANES, LANES), F32)],
        compiler_params=_cparams(),
        name="moe_router",
    )(h_prompt, h_dec, nf, *_router_split(wr), br)


GATHER_UNROLL = 32


def _gather_rows_start(idx_ref, base, src_ref, dst_ref, sem, n):
    def group(g, carry):
        for u in range(GATHER_UNROLL):
            r = g * GATHER_UNROLL + u
            src = pl.multiple_of(idx_ref[base + r] * ROW_TILES, ROW_TILES)
            dst = pl.multiple_of(r * ROW_TILES, ROW_TILES)
            pltpu.make_async_copy(src_ref.at[pl.ds(src, ROW_TILES), :], dst_ref.at[pl.ds(dst, ROW_TILES), :],
                                  sem).start(priority=u % 2)
        return carry

    lax.fori_loop(0, n // GATHER_UNROLL, group, 0)


def _gather_rows_wait(src_ref, dst_ref, sem, n):
    pltpu.make_async_copy(src_ref.at[pl.ds(0, n * ROW_TILES), :], dst_ref.at[pl.ds(0, n * ROW_TILES), :], sem).wait()


def _gather_tile(idx_ref, src_ref, buf, sem, n, n_live=None):
    i = pl.program_id(0)
    slot = lax.rem(i, 2)
    n_live = pl.num_programs(0) if n_live is None else n_live

    @pl.when(i == 0)
    def _():
        _gather_rows_start(idx_ref, 0, src_ref, buf.at[0], sem.at[0], n)

    @pl.when(i + 1 < n_live)
    def _():
        _gather_rows_start(idx_ref, (i + 1) * n, src_ref, buf.at[1 - slot], sem.at[1 - slot], n)

    @pl.when(i < n_live)
    def _():
        _gather_rows_wait(src_ref, buf.at[slot], sem.at[slot], n)

    return buf.at[slot]


def _gather_scratch(n):
    return [pltpu.VMEM((2, n * ROW_TILES, LANES), F32), pltpu.SemaphoreType.DMA((2,))]


def _moe_body(idx_ref, ea_ref, eb_ref, cha_ref, chb_ref, nlive_ref, rows_hbm, g_ref, nf_ref,
              w1a_ref, w3a_ref, w2a_ref, w1b_ref, w3b_ref, w2b_ref, out_ref,
              s1a, s3a, s2a, s1b, s3b, s2b, xbuf, xsem):
    i = pl.program_id(0)
    live = i < nlive_ref[0]
    x_ref = _gather_tile(idx_ref, rows_hbm, xbuf, xsem, TM, nlive_ref[0])

    @pl.when(cha_ref[i] == 1)
    def _():
        s1a[...] = w1a_ref[...].astype(BF16)
        s3a[...] = w3a_ref[...].astype(BF16)
        s2a[...] = w2a_ref[...].astype(BF16)

    @pl.when(chb_ref[i] == 1)
    def _():
        s1b[...] = w1b_ref[...].astype(BF16)
        s3b[...] = w3b_ref[...].astype(BF16)
        s2b[...] = w2b_ref[...].astype(BF16)

    @pl.when(live)
    def _():
        h = _load_rows(x_ref, TM)
        xn = _rms(h, nf_ref[...]).astype(BF16)
        g = g_ref[...]

        def ffn(w1, w3, w2):
            a = _dot(xn, w1[...])
            hdn = (a * (1.0 / (1.0 + jnp.exp(-a)))) * _dot(xn, w3[...])
            return _dot(hdn.astype(BF16), w2[...])

        y = g[:, 0:1] * ffn(s1a, s3a, s2a) + g[:, 1:2] * ffn(s1b, s3b, s2b)
        _store_rows(out_ref, h + y, TM)

    @pl.when(jnp.logical_not(live))
    def _():
        out_ref[...] = jnp.zeros_like(out_ref)


def _moe_ffn(layer, idx_sorted, ea, eb, cha, chb, n_live, rows, gates, nf, w1, w3, w2):
    wa = lambda shape: pl.BlockSpec((None, None) + shape, lambda i, ix, ea, eb, ca, cb, va: (layer, ea[i], 0, 0))
    wb = lambda shape: pl.BlockSpec((None, None) + shape, lambda i, ix, ea, eb, ca, cb, va: (layer, eb[i], 0, 0))
    up, down = (D_MODEL, D_EXPERT), (D_EXPERT, D_MODEL)
    return pl.pallas_call(
        _moe_body,
        grid_spec=pltpu.PrefetchScalarGridSpec(
            num_scalar_prefetch=6,
            grid=(MOE_TILES,),
            in_specs=[
                pl.BlockSpec(memory_space=pl.ANY),
                pl.BlockSpec((TM, 2), lambda i, *_: (i, 0)),
                pl.BlockSpec((1, D_MODEL), lambda i, *_: (0, 0)),
                wa(up), wa(up), wa(down), wb(up), wb(up), wb(down),
            ],
            out_specs=pl.BlockSpec((TM * ROW_TILES, LANES), lambda i, *_: (i, 0)),
            scratch_shapes=[pltpu.VMEM(up, BF16), pltpu.VMEM(up, BF16), pltpu.VMEM(down, BF16),
                            pltpu.VMEM(up, BF16), pltpu.VMEM(up, BF16), pltpu.VMEM(down, BF16)]
            + _gather_scratch(TM),
        ),
        out_shape=jax.ShapeDtypeStruct((P_ROWS * ROW_TILES, LANES), F32),
        compiler_params=_cparams(),
        name="moe_ffn",
    )(idx_sorted, ea, eb, cha, chb, n_live, rows, gates, nf, w1, w3, w2, w1, w3, w2)


def _router_weights(layer, w_group, b_group, w_expert, b_expert):
    wr = jnp.zeros((D_MODEL, LANES), F32)
    wr = wr.at[:, :N_EGROUPS].set(w_group[layer]).at[:, N_EGROUPS:N_EGROUPS + N_EXPERTS].set(w_expert[layer])
    br = jnp.zeros((1, LANES), F32)
    br = br.at[0, :N_EGROUPS].set(b_group[layer]).at[0, N_EGROUPS:N_EGROUPS + N_EXPERTS].set(b_expert[layer])
    return wr, br


def _moe_layer(layer, rows, info, cnt, nf, w1, w3, w2):
    bucket = info[0].astype(jnp.int32)
    rank = info[1].astype(jnp.int32)
    counts = cnt[0, :N_BUCKETS].astype(jnp.int32)
    n_tiles = (counts + TM - 1) // TM
    tile_end = jnp.cumsum(n_tiles)
    tile_start = tile_end - n_tiles
    start_of = jnp.sum(jnp.where(bucket[:, None] == jnp.arange(N_BUCKETS)[None, :], tile_start[None, :], 0), axis=1)
    pos = start_of * TM + rank
    packed = jnp.stack([jnp.arange(T, dtype=F32), info[2], info[3]], axis=0)
    base = jnp.zeros((3, P_ROWS), F32).at[0].set((jnp.arange(P_ROWS) % T).astype(F32))
    packed = base.at[:, pos].set(packed, unique_indices=True, indices_are_sorted=False, mode="promise_in_bounds")
    idx_sorted = packed[0].astype(jnp.int32)
    gates = packed[1:3].T
    total = tile_end[-1]
    j = jnp.minimum(jnp.arange(MOE_TILES), total - 1)
    tb = jnp.sum((j[:, None] >= tile_end[None, :]).astype(jnp.int32), axis=1)
    grp, pair = tb // N_PAIRS, tb % N_PAIRS
    ea = (grp * EXPERTS_PER_GROUP + jnp.asarray(PAIR_A, jnp.int32)[pair]).astype(jnp.int32)
    eb = (grp * EXPERTS_PER_GROUP + jnp.asarray(PAIR_B, jnp.int32)[pair]).astype(jnp.int32)
    first = jnp.arange(MOE_TILES) == 0
    cha = (first | (ea != jnp.roll(ea, 1))).astype(jnp.int32)
    chb = (first | (eb != jnp.roll(eb, 1))).astype(jnp.int32)
    n_live = total.reshape(1).astype(jnp.int32)
    return _moe_ffn(layer, idx_sorted, ea, eb, cha, chb, n_live, rows, gates, nf, w1, w3, w2), pos


VT_ROWS = LANES + 16


def _pair_seg_matrix():
    grp = np.arange(2 * HEAD_W) // HEAD_W
    return jnp.asarray(grp[:, None] == grp[None, :], BF16)


def _rope_swap(x):
    lane = lax.broadcasted_iota(jnp.int32, x.shape, 1)
    return jnp.where(lane < ROPE_LO + ROPE_HALF, pltpu.roll(x, LANES - ROPE_HALF, 1), pltpu.roll(x, ROPE_HALF, 1))


def _expand_k(cb, kr, wuk_ref, seg, kg_ref, k_ref):
    kn = _dot(cb, wuk_ref[...])
    pairs = [kn[:, pr * 2 * HEAD_W:(pr + 1) * 2 * HEAD_W] for pr in range(N_HEADS // 2)]
    ms = [_split_dot(x * x, seg) * (1.0 / NOPE_DIM) for x in pairs]
    kg2 = jnp.concatenate([kg_ref[...]] * 2, axis=1)
    kr2 = jnp.concatenate([kr, kr], axis=1)
    for pr in range(N_HEADS // 2):
        k_ref[:, pr * 2 * HEAD_W:(pr + 1) * 2 * HEAD_W] = (pairs[pr] * lax.rsqrt(ms[pr] + EPS) * kg2 + kr2).astype(BF16)


def _mla_proj_body(pos_ref, sorted_hbm, cos_ref, sin_ref, cost_ref, sint_ref, kvn_ref, wdkv_ref, kvan_ref, krg_ref,
                   wuk_ref, wuv_ref, wuvt_ref, kg_ref, nmq_ref, wdq_ref, qan_ref, wuqt_ref, qg_ref, seg_ref,
                   rows_ref, ckvp_ref, ckvd_ref, krp_ref, krd_ref, k_ref, vdec_ref, vt_ref, qt_ref, qdec_ref,
                   xbuf, xsem):
    is_dec = pl.program_id(0) == N_TILES - 1
    x_ref = _gather_tile(pos_ref, sorted_hbm, xbuf, xsem, TM)
    rows_ref[...] = x_ref[...]
    h = _load_rows(x_ref, TM)
    hn = h * lax.rsqrt(jnp.mean(h * h, axis=-1, keepdims=True) + EPS)
    c = _dot((hn * kvn_ref[...]).astype(BF16), wdkv_ref[...])
    ckv = _rms(c[:, :KV_RANK], kvan_ref[...])
    kr = c[:, KV_RANK:]
    kr = kr * lax.rsqrt(jnp.sum(kr * kr, axis=-1, keepdims=True) * (1.0 / ROPE_DIM) + EPS) * krg_ref[...]
    kr = kr * cos_ref[...] + _rope_swap(kr) * sin_ref[...]
    cb = ckv.astype(BF16)
    _expand_k(cb, kr, wuk_ref, seg_ref[...], kg_ref, k_ref)
    vt = _dot(wuvt_ref[...], ckv.T.astype(BF16)).astype(BF16)
    for pr in range(N_HEADS // 2):
        vt_ref[0, pr * VT_ROWS:pr * VT_ROWS + LANES, :] = vt[pr * LANES:(pr + 1) * LANES]
        vt_ref[0, pr * VT_ROWS + LANES:(pr + 1) * VT_ROWS, :] = jnp.ones((VT_ROWS - LANES, TM), BF16)
    cq = _rms(_dot((hn * nmq_ref[...]).astype(BF16), wdq_ref[...]), qan_ref[...])
    qt = _dot(wuqt_ref[...], cq.T.astype(BF16))
    cost, sint, qg = cost_ref[0], sint_ref[0], qg_ref[...]
    for hh in range(N_HEADS):
        x = qt[hh * HEAD_W:(hh + 1) * HEAD_W, :]
        xn, xr = x[:NOPE_DIM], x[ROPE_LO:ROPE_LO + ROPE_DIM]
        xn = xn * lax.rsqrt(jnp.mean(xn * xn, axis=0, keepdims=True) + EPS) * qg[:NOPE_DIM]
        xr = xr * lax.rsqrt(jnp.mean(xr * xr, axis=0, keepdims=True) + EPS) * qg[ROPE_LO:ROPE_LO + ROPE_DIM]
        x1, x2 = xr[:ROPE_HALF], xr[ROPE_HALF:]
        qh = jnp.concatenate([xn, x1 * cost - x2 * sint, x1 * sint + x2 * cost,
                              jnp.zeros((HEAD_W - NOPE_DIM - ROPE_DIM, TM), F32)], axis=0) * Q_SCALE
        qt_ref[0, hh * HEAD_W:(hh + 1) * HEAD_W, :] = qh.astype(BF16)

    @pl.when(jnp.logical_not(is_dec))
    def _():
        ckvp_ref[...] = ckv
        krp_ref[...] = kr[:, ROPE_LO:ROPE_LO + ROPE_DIM]

    @pl.when(is_dec)
    def _():
        ckvd_ref[...] = ckv
        krd_ref[...] = kr[:, ROPE_LO:ROPE_LO + ROPE_DIM]
        vdec_ref[...] = _dot(cb, wuv_ref[...]).astype(BF16)
        for hh in range(N_HEADS):
            rows = slice(hh * HEAD_W, (hh + 1) * HEAD_W)
            qdec_ref[:, rows] = qt_ref[0, rows, :].astype(F32).T.astype(BF16)


def _mla_proj(pos, sorted_rows, cos_t, sin_t, cos_tt, sin_tt, kvn, wdkv, kvan, krg, wuk, wuv, kg, nmq, wdq, qan, wuq, qg):
    const = lambda *shape: pl.BlockSpec(shape, lambda i, p: (0,) * len(shape))
    tab_tile = lambda i: jnp.where(i < N_PROMPT // TM, i % (SEQ // TM), SEQ // TM)
    tab = pl.BlockSpec((TM, LANES), lambda i, p: (tab_tile(i), 0))
    tab_t = pl.BlockSpec((1, ROPE_HALF, TM), lambda i, p: (tab_tile(i), 0, 0))
    row = lambda w: pl.BlockSpec((TM, w), lambda i, p: (i, 0))
    prow = lambda w: pl.BlockSpec((TM, w), lambda i, p: (jnp.minimum(i, N_TILES - 2), 0))
    return pl.pallas_call(
        _mla_proj_body,
        grid_spec=pltpu.PrefetchScalarGridSpec(
            num_scalar_prefetch=1,
            grid=(N_TILES,),
            in_specs=[
                pl.BlockSpec(memory_space=pl.ANY), tab, tab, tab_t, tab_t,
                const(1, D_MODEL), const(D_MODEL, KV_RANK + LANES), const(1, KV_RANK), const(1, LANES),
                const(KV_RANK, N_HEADS * HEAD_W), const(KV_RANK, N_HEADS * V_DIM), const(N_HEADS * V_DIM, KV_RANK),
                const(1, LANES),
                const(1, D_MODEL), const(D_MODEL, Q_RANK), const(1, Q_RANK), const(N_HEADS * HEAD_W, Q_RANK),
                const(HEAD_W, TM), const(2 * HEAD_W, 2 * HEAD_W),
            ],
            out_specs=[pl.BlockSpec((TM * ROW_TILES, LANES), lambda i, p: (i, 0)),
                       prow(KV_RANK), const(N_DEC, KV_RANK), prow(ROPE_DIM), const(N_DEC, ROPE_DIM),
                       row(N_HEADS * HEAD_W), const(N_DEC, N_HEADS * V_DIM),
                       pl.BlockSpec((1, N_HEADS // 2 * VT_ROWS, TM), lambda i, p: (i, 0, 0)),
                       pl.BlockSpec((1, N_HEADS * HEAD_W, TM), lambda i, p: (i, 0, 0)),
                       const(N_DEC, N_HEADS * HEAD_W)],
            scratch_shapes=_gather_scratch(TM),
        ),
        out_shape=[
            jax.ShapeDtypeStruct((T * ROW_TILES, LANES), F32),
            jax.ShapeDtypeStruct((N_PROMPT, KV_RANK), F32), jax.ShapeDtypeStruct((N_DEC, KV_RANK), F32),
            jax.ShapeDtypeStruct((N_PROMPT, ROPE_DIM), F32), jax.ShapeDtypeStruct((N_DEC, ROPE_DIM), F32),
            jax.ShapeDtypeStruct((T, N_HEADS * HEAD_W), BF16), jax.ShapeDtypeStruct((N_DEC, N_HEADS * V_DIM), BF16),
            jax.ShapeDtypeStruct((N_TILES, N_HEADS // 2 * VT_ROWS, TM), BF16),
            jax.ShapeDtypeStruct((N_TILES, N_HEADS * HEAD_W, TM), BF16),
            jax.ShapeDtypeStruct((N_DEC, N_HEADS * HEAD_W), BF16),
        ],
        compiler_params=_cparams(),
        name="mla_proj",
    )(pos, sorted_rows, cos_t, sin_t, cos_tt, sin_tt, kvn, wdkv, kvan, krg, wuk, wuv, wuv.T, kg, nmq, wdq, qan,
      wuq.T, jnp.broadcast_to(qg.reshape(HEAD_W, 1), (HEAD_W, TM)), _pair_seg_matrix())


def _cache_kv_body(ckv_ref, kr_ref, place_ref, wuk_ref, wuv_ref, kg_ref, seg_ref, k_ref, v_ref):
    kr = _dot(kr_ref[...].astype(BF16), place_ref[...])
    cb = ckv_ref[...].astype(BF16)
    _expand_k(cb, kr, wuk_ref, seg_ref[...], kg_ref, k_ref)
    v_ref[...] = _dot(cb, wuv_ref[...]).astype(BF16)


CACHE_ROWS = 1024
assert PAST_LEN % CACHE_ROWS == 0


def _cache_kv(ckv, kr, wuk, wuv, kg):
    n = ckv.shape[0]
    place = jnp.asarray(np.arange(ROPE_DIM)[:, None] + ROPE_LO == np.arange(LANES)[None, :], BF16)
    const = lambda *shape: pl.BlockSpec(shape, lambda i: (0,) * len(shape))
    row = lambda w: pl.BlockSpec((CACHE_ROWS, w), lambda i: (i, 0))
    return pl.pallas_call(
        _cache_kv_body,
        grid=(n // CACHE_ROWS,),
        in_specs=[row(KV_RANK), row(ROPE_DIM), const(ROPE_DIM, LANES), const(KV_RANK, N_HEADS * HEAD_W),
                  const(KV_RANK, N_HEADS * V_DIM), const(1, LANES), const(2 * HEAD_W, 2 * HEAD_W)],
        out_specs=[row(N_HEADS * HEAD_W), row(N_HEADS * V_DIM)],
        out_shape=[jax.ShapeDtypeStruct((n, N_HEADS * HEAD_W), BF16), jax.ShapeDtypeStruct((n, N_HEADS * V_DIM), BF16)],
        compiler_params=_cparams(),
        name="cache_kv",
    )(ckv, kr, place, wuk, wuv, kg, _pair_seg_matrix())


TQ = 256
TK = 256
assert TQ == TK and TQ % CHUNK == 0
SCORE_LOOKAHEAD = 8


def _qk(q, k):
    return lax.dot_general(q, k, (((1,), (1,)), ((), ())), preferred_element_type=F32)


def _merge_heads(o_ref, outs, rows):
    lane = lax.broadcasted_iota(jnp.int32, (rows, LANES), 1)
    for pr in range(N_HEADS // 2):
        o_ref[:, pr * LANES:(pr + 1) * LANES] = jnp.where(lane < V_DIM, outs[2 * pr], outs[2 * pr + 1]).astype(BF16)


def _prompt_attn_body(qt_ref, k_ref, vt_ref, rows_ref, wo_ref, out_ref, m_scr, acc_scr):
    qi = pl.program_id(1)
    m_scr[...] = jnp.full(m_scr.shape, NEG, F32)
    acc_scr[...] = jnp.zeros(acc_scr.shape, F32)

    def all_heads(j, mask):
        ks = pl.ds(pl.multiple_of(j * TK, TK), TK)

        def scores(hh):
            hcols = slice(hh * HEAD_W, (hh + 1) * HEAD_W)
            return _dot(k_ref[ks, hcols], qt_ref[0, hcols, :])

        ahead = [scores(hh) for hh in range(SCORE_LOOKAHEAD)]
        for hh in range(N_HEADS):
            vrows = slice((hh // 2) * VT_ROWS, (hh // 2 + 1) * VT_ROWS)
            s = ahead.pop(0)
            if hh + SCORE_LOOKAHEAD < N_HEADS:
                ahead.append(scores(hh + SCORE_LOOKAHEAD))
            if mask is not None:
                s = jnp.where(mask, s, NEG)
            m_old = m_scr[hh]
            m_new = jnp.maximum(m_old, jnp.max(s, axis=0, keepdims=True))
            p = jnp.exp2(s - m_new).astype(BF16)
            m_scr[hh] = m_new
            acc_scr[hh] = jnp.exp2(m_old - m_new) * acc_scr[hh] + _dot(vt_ref[j, vrows, :], p)

    def step(j, carry):
        all_heads(j, None)
        return carry

    lax.fori_loop(0, qi, step, 0)
    kc = lax.broadcasted_iota(jnp.int32, (TK, TQ), 0) // CHUNK
    qc = lax.broadcasted_iota(jnp.int32, (TK, TQ), 1) // CHUNK
    all_heads(qi, kc <= qc)
    row = lax.broadcasted_iota(jnp.int32, (LANES, TQ), 0)
    pairs = []
    for pr in range(N_HEADS // 2):
        even = acc_scr[2 * pr, :LANES] / acc_scr[2 * pr, LANES:LANES + 1]
        odd = acc_scr[2 * pr + 1, :LANES] / acc_scr[2 * pr + 1, LANES:LANES + 1]
        pairs.append(jnp.where(row < V_DIM, even, odd))
    o = jnp.concatenate(pairs, axis=0).T.astype(BF16)
    out_ref[...] = _load_rows(rows_ref, TQ) + _dot(o, wo_ref[...])


def _prompt_attn(qt, k, vt, rows, wo):
    nq = SEQ // TQ
    return pl.pallas_call(
        _prompt_attn_body,
        grid=(BATCH, nq),
        in_specs=[
            pl.BlockSpec((1, N_HEADS * HEAD_W, TQ), lambda b, i: (b * nq + i, 0, 0)),
            pl.BlockSpec((SEQ, N_HEADS * HEAD_W), lambda b, i: (b, 0)),
            pl.BlockSpec((SEQ // TK, N_HEADS // 2 * VT_ROWS, TK), lambda b, i: (b, 0, 0)),
            pl.BlockSpec((TQ * ROW_TILES, LANES), lambda b, i: (b * nq + i, 0)),
            pl.BlockSpec((N_HEADS * V_DIM, D_MODEL), lambda b, i: (0, 0)),
        ],
        out_specs=pl.BlockSpec((TQ, D_MODEL), lambda b, i: (b * nq + i, 0)),
        out_shape=jax.ShapeDtypeStruct((N_PROMPT, D_MODEL), F32),
        scratch_shapes=[pltpu.VMEM((N_HEADS, 1, TQ), F32), pltpu.VMEM((N_HEADS, VT_ROWS, TQ), F32)],
        compiler_params=_cparams(2),
        name="prompt_attn",
    )(qt, k, vt, rows, wo)


def _sample_attn_body(q_ref, kc_ref, vc_ref, kn_ref, vn_ref, rows_ref, wo_ref, out_ref, o_scr):
    outs = []
    for hh in range(N_HEADS):
        hcols = slice(hh * HEAD_W, (hh + 1) * HEAD_W)
        vcols = slice((hh // 2) * LANES, (hh // 2 + 1) * LANES)
        q = q_ref[:, hcols]
        sc = _qk(q, kc_ref[:, hcols])
        sn = _qk(q, kn_ref[:, hcols])
        m = jnp.maximum(jnp.max(sc, axis=-1, keepdims=True), jnp.max(sn, axis=-1, keepdims=True))
        pc = jnp.exp2(sc - m)
        pn = jnp.exp2(sn - m)
        l = jnp.sum(pc, axis=-1, keepdims=True) + jnp.sum(pn, axis=-1, keepdims=True)
        acc = _dot(pc.astype(BF16), vc_ref[:, vcols]) + _dot(pn.astype(BF16), vn_ref[:, vcols])
        outs.append(acc / l)
    _merge_heads(o_scr, outs, DEC_SEQ)
    out_ref[...] = _load_rows(rows_ref, DEC_SEQ) + _dot(o_scr[...], wo_ref[...])


def _sample_attn(q, kc, vc, kn, vn, rows, wo):
    off = N_PROMPT // DEC_SEQ
    return pl.pallas_call(
        _sample_attn_body,
        grid=(DEC_BATCH,),
        in_specs=[
            pl.BlockSpec((DEC_SEQ, N_HEADS * HEAD_W), lambda b: (b, 0)),
            pl.BlockSpec((PAST_LEN, N_HEADS * HEAD_W), lambda b: (b, 0)),
            pl.BlockSpec((PAST_LEN, N_HEADS * V_DIM), lambda b: (b, 0)),
            pl.BlockSpec((DEC_SEQ, N_HEADS * HEAD_W), lambda b: (off + b, 0)),
            pl.BlockSpec((DEC_SEQ, N_HEADS * V_DIM), lambda b: (b, 0)),
            pl.BlockSpec((DEC_SEQ * ROW_TILES, LANES), lambda b: (off + b, 0)),
            pl.BlockSpec((N_HEADS * V_DIM, D_MODEL), lambda b: (0, 0)),
        ],
        out_specs=pl.BlockSpec((DEC_SEQ, D_MODEL), lambda b: (b, 0)),
        out_shape=jax.ShapeDtypeStruct((N_DEC, D_MODEL), F32),
        scratch_shapes=[pltpu.VMEM((DEC_SEQ, N_HEADS * V_DIM), BF16)],
        compiler_params=_cparams(),
        name="sample_attn",
    )(q, kc, vc, kn, vn, rows, wo)


def _finish_body(pos_ref, sorted_hbm, yp_ref, ys_ref, xbuf, xsem):
    i = pl.program_id(0)
    y = _load_rows(_gather_tile(pos_ref, sorted_hbm, xbuf, xsem, TM), TM)

    @pl.when(i < N_TILES - 1)
    def _():
        yp_ref[...] = y

    @pl.when(i == N_TILES - 1)
    def _():
        ys_ref[...] = y


def _finish(pos, sorted_rows):
    return pl.pallas_call(
        _finish_body,
        grid_spec=pltpu.PrefetchScalarGridSpec(
            num_scalar_prefetch=1,
            grid=(N_TILES,),
            in_specs=[pl.BlockSpec(memory_space=pl.ANY)],
            out_specs=[pl.BlockSpec((TM, D_MODEL), lambda i, p: (jnp.minimum(i, N_TILES - 2), 0)),
                       pl.BlockSpec((N_DEC, D_MODEL), lambda i, p: (0, 0))],
            scratch_shapes=_gather_scratch(TM),
        ),
        out_shape=[jax.ShapeDtypeStruct((N_PROMPT, D_MODEL), F32), jax.ShapeDtypeStruct((N_DEC, D_MODEL), F32)],
        compiler_params=_cparams(),
        name="finish",
    )(pos, sorted_rows)


def _rope_tables():
    half = ROPE_DIM // 2
    inv_freq = ROPE_THETA ** (-jnp.arange(half, dtype=F32) / half)
    dec_pos = PAST_LEN + jnp.tile(jnp.arange(DEC_SEQ, dtype=jnp.int32), DEC_BATCH)
    pos = jnp.concatenate([jnp.arange(SEQ, dtype=jnp.int32), dec_pos])
    ang = pos.astype(F32)[:, None] * inv_freq[None, :]
    cos, sin = jnp.cos(ang), jnp.sin(ang)
    n = pos.shape[0]
    cos_t = jnp.ones((n, LANES), F32).at[:, ROPE_LO:ROPE_LO + ROPE_DIM].set(jnp.concatenate([cos, cos], axis=1))
    sin_t = jnp.zeros((n, LANES), F32).at[:, ROPE_LO:ROPE_LO + ROPE_DIM].set(jnp.concatenate([-sin, sin], axis=1))
    to_tiles = lambda a: a.reshape(n // TM, TM, half).transpose(0, 2, 1)
    return cos_t, sin_t, to_tiles(cos), to_tiles(sin)


def _on_lanes(vec, lo):
    return jnp.zeros((1, LANES), F32).at[0, lo:lo + vec.shape[0]].set(vec)


def kernel(x_prompt, x_sample, cache_ckv, cache_krope, norm_mix, norm_ffn, gm_w_in, gm_b_in, gm_ln_g, gm_ln_b, gm_w_s, gm_b_s, gm_w_out, gm_b_out, kv_norm, w_dkv, kv_a_norm, k_rope_norm, w_uk, w_uv, k_nope_norm, w_dq, q_a_norm, w_uq, q_nope_norm, q_rope_norm, w_o, moe_w_group, moe_b_group, moe_w_expert, moe_b_expert, moe_w1, moe_w3, moe_w2):
    nf0, nf1 = norm_ffn[0].reshape(1, D_MODEL), norm_ffn[1].reshape(1, D_MODEL)

    idx = np.arange(GMLP_BLOCK)
    allowed = (idx[None, :] // CHUNK) <= (idx[:, None] // CHUNK)
    ws_p = jnp.where(allowed[None], gm_w_s[0], 0.0).astype(BF16)
    same_seq = (idx[None, :] // DEC_SEQ) == (idx[:, None] // DEC_SEQ)
    ws_d = jnp.where(same_seq[None], jnp.tile(gm_w_s[0][:, :DEC_SEQ, :DEC_SEQ], (1, GMLP_BLOCK // DEC_SEQ, GMLP_BLOCK // DEC_SEQ)), 0.0).astype(BF16)
    bs_p = gm_b_s[0][:, :, None]
    bs_d = jnp.tile(gm_b_s[0][:, :DEC_SEQ], (1, GMLP_BLOCK // DEC_SEQ))[:, :, None]
    rows, v_rows, info, cnt = _gmlp_layer(
        x_prompt.reshape(N_PROMPT, D_MODEL), x_sample.reshape(N_DEC, D_MODEL),
        norm_mix[0].reshape(1, -1), gm_w_in[0].astype(BF16), gm_b_in[0].reshape(1, -1),
        gm_ln_g[0].reshape(1, -1), gm_ln_b[0].reshape(1, -1), ws_p, ws_d, bs_p, bs_d,
        gm_w_out[0].astype(BF16), gm_b_out[0].reshape(1, -1),
        nf0, *_router_weights(0, moe_w_group, moe_b_group, moe_w_expert, moe_b_expert))
    sorted_rows, pos = _moe_layer(0, rows, info, cnt, nf0, moe_w1, moe_w3, moe_w2)

    cos_t, sin_t, cos_tt, sin_tt = _rope_tables()
    wdkv = jnp.zeros((D_MODEL, KV_RANK + LANES), F32).at[:, :KV_RANK].set(w_dkv[:, :KV_RANK])
    wdkv = wdkv.at[:, KV_RANK + ROPE_LO:KV_RANK + ROPE_LO + ROPE_DIM].set(w_dkv[:, KV_RANK:]).astype(BF16)
    wuk = jnp.zeros((KV_RANK, N_HEADS, HEAD_W), F32).at[:, :, :NOPE_DIM].set(w_uk).reshape(KV_RANK, -1).astype(BF16)
    wuv = w_uv.reshape(KV_RANK, -1).astype(BF16)
    wuq = jnp.zeros((Q_RANK, N_HEADS, HEAD_W), F32).at[:, :, :NOPE_DIM + ROPE_DIM].set(w_uq[0]).reshape(Q_RANK, -1).astype(BF16)
    kg = _on_lanes(k_nope_norm, 0)
    krg = _on_lanes(k_rope_norm, ROPE_LO)
    qg = _on_lanes(jnp.concatenate([q_nope_norm[0], q_rope_norm[0]]), 0)
    rows, ckv_p, ckv_d, krope_p, krope_d, k_new, v_dec, vt_new, qt, q_dec = _mla_proj(
        pos, sorted_rows, cos_t, sin_t, cos_tt, sin_tt, kv_norm.reshape(1, -1), wdkv, kv_a_norm.reshape(1, -1), krg, wuk, wuv, kg,
        norm_mix[1].reshape(1, -1), w_dq[0].astype(BF16), q_a_norm[0].reshape(1, -1), wuq, qg)
    k_cache, v_cache = _cache_kv(cache_ckv.reshape(-1, KV_RANK), cache_krope.reshape(-1, ROPE_DIM), wuk, wuv, kg)

    wo = w_o[0].astype(BF16)
    h_prompt = _prompt_attn(qt, k_new, vt_new, rows, wo)
    h_dec = _sample_attn(q_dec, k_cache, v_cache, k_new, v_dec, rows, wo)
    rows, info, cnt = _router(h_prompt, h_dec, nf1,
                              *_router_weights(1, moe_w_group, moe_b_group, moe_w_expert, moe_b_expert))
    sorted_rows, pos = _moe_layer(1, rows, info, cnt, nf1, moe_w1, moe_w3, moe_w2)
    y_prompt, y_sample = _finish(pos, sorted_rows)

    return (y_prompt.reshape(BATCH, SEQ, D_MODEL), y_sample.reshape(DEC_BATCH, DEC_SEQ, D_MODEL),
            ckv_p.reshape(BATCH, SEQ, KV_RANK), krope_p.reshape(BATCH, SEQ, ROPE_DIM),
            ckv_d.reshape(DEC_BATCH, DEC_SEQ, KV_RANK), krope_d.reshape(DEC_BATCH, DEC_SEQ, ROPE_DIM),
            v_rows.reshape(1, DEC_BATCH, DEC_SEQ, D_GATE))
```

```python
import functools

import jax
import jax.numpy as jnp
import numpy as np
from jax import lax
from jax.experimental import pallas as pl
from jax.experimental.pallas import tpu as pltpu

F32 = jnp.float32
BF16 = jnp.bfloat16

D_MODEL = 1024
BATCH = 8
SEQ = 2048
DEC_BATCH = 16
DEC_SEQ = 16
PAST_LEN = 2048
CHUNK = 64
GMLP_BLOCK = 128
D_GATE = 2 * D_MODEL
N_SG = 8
SG_W = D_GATE // N_SG
N_HEADS = 8
NOPE_DIM = 64
ROPE_DIM = 32
V_DIM = 64
Q_RANK = 384
KV_RANK = 256
ROPE_THETA = 10000.0
SCALE = (NOPE_DIM + ROPE_DIM) ** -0.5
Q_SCALE = SCALE * float(np.log2(np.e))
N_EGROUPS = 4
EXPERTS_PER_GROUP = 4
N_EXPERTS = N_EGROUPS * EXPERTS_PER_GROUP
D_EXPERT = 512
EPS = 1e-6
NEG = -1e30

LANES = 128
SUBLANES = 8
ROW_TILES = D_MODEL // LANES
assert ROW_TILES == SUBLANES

N_PROMPT = BATCH * SEQ
N_DEC = DEC_BATCH * DEC_SEQ
T = N_PROMPT + N_DEC
TM = 256
assert N_PROMPT % TM == 0 and N_DEC == TM
N_TILES = T // TM
HEAD_W = LANES
ROPE_LO = NOPE_DIM
ROPE_HALF = ROPE_DIM // 2

PAIR_A = (0, 0, 0, 1, 1, 3)
PAIR_B = (1, 2, 3, 3, 2, 2)
N_PAIRS = 6
N_BUCKETS = N_EGROUPS * N_PAIRS
MOE_TILES = (T + N_BUCKETS * (TM - 1) + TM - 1) // TM
P_ROWS = MOE_TILES * TM

VMEM_LIMIT = 56 * 1024 * 1024


def _cparams(n_axes=1, vmem=VMEM_LIMIT):
    return pltpu.CompilerParams(dimension_semantics=("arbitrary",) * n_axes, vmem_limit_bytes=vmem)


def _rms(x, g):
    return x * lax.rsqrt(jnp.mean(x * x, axis=-1, keepdims=True) + EPS) * g


def _load_rows(ref, n):
    return jnp.concatenate([ref[pl.ds(s, n, stride=ROW_TILES), :] for s in range(ROW_TILES)], axis=1)


def _store_rows(ref, x, n):
    for s in range(ROW_TILES):
        ref[pl.ds(s, n, stride=ROW_TILES), :] = x[:, s * LANES:(s + 1) * LANES]


def _dot(a, b):
    return jnp.dot(a, b, preferred_element_type=F32)


def _split_dot(x, m):
    hi = x.astype(BF16)
    lo = (x - hi.astype(F32)).astype(BF16)
    return _dot(hi, m) + _dot(lo, m)


GELU_K1 = float(-2.0 * np.sqrt(2.0 / np.pi) * np.log2(np.e))
GELU_K3 = GELU_K1 * 0.044715


def _gmlp_body(xp_ref, xd_ref, nm_ref, win_ref, bin_ref, lng_ref, lnb_ref, wsp_ref, wsd_ref, bsp_ref, bsd_ref,
               wout_ref, bout_ref, nf_ref, wrh_ref, wr2_ref, br_ref,
               rows_ref, v_ref, info_ref, cnt_ref, gated_ref, carry_ref):
    i = pl.program_id(0)
    is_dec = i == N_TILES - 1
    _route_init(carry_ref)
    x = jnp.where(is_dec, xd_ref[...], xp_ref[...])
    xn = _rms(x, nm_ref[...]).astype(BF16)
    z = _dot(xn, win_ref[...]) + bin_ref[...]
    z = z / (1.0 + jnp.exp2(z * (GELU_K1 + GELU_K3 * (z * z))))
    u = z[:, :D_GATE]
    v = z[:, D_GATE:]
    mu = jnp.mean(v, axis=-1, keepdims=True)
    vc = v - mu
    var = jnp.mean(vc * vc, axis=-1, keepdims=True)
    v = vc * lax.rsqrt(var + EPS) * lng_ref[...] + lnb_ref[...]

    v_ref[...] = v
    vb = v.astype(BF16)
    for g in range(N_SG):
        ws = jnp.where(is_dec, wsd_ref[g], wsp_ref[g])
        bs = jnp.where(is_dec, bsd_ref[g], bsp_ref[g])
        for b in range(TM // GMLP_BLOCK):
            rows = slice(b * GMLP_BLOCK, (b + 1) * GMLP_BLOCK)
            cols = slice(g * SG_W, (g + 1) * SG_W)
            s = _dot(ws, vb[rows, cols]) + bs
            gated_ref[rows, cols] = (u[rows, cols] * s).astype(BF16)
    h = x + _dot(gated_ref[...], wout_ref[...]) + bout_ref[...]
    _store_rows(rows_ref, h, TM)
    _route_tile(h, nf_ref, wrh_ref, wr2_ref, br_ref, info_ref, cnt_ref, carry_ref)


def _gmlp_layer(x_prompt, x_dec, nm, w_in, b_in, ln_g, ln_b, ws_p, ws_d, bs_p, bs_d, w_out, b_out, nf, wr, br):
    const = lambda *shape: pl.BlockSpec(shape, lambda i: (0,) * len(shape))
    return pl.pallas_call(
        _gmlp_body,
        grid=(N_TILES,),
        in_specs=[
            pl.BlockSpec((TM, D_MODEL), lambda i: (jnp.minimum(i, N_TILES - 2), 0)), const(N_DEC, D_MODEL),
            const(1, D_MODEL), const(D_MODEL, 2 * D_GATE), const(1, 2 * D_GATE),
            const(1, D_GATE), const(1, D_GATE),
            const(N_SG, GMLP_BLOCK, GMLP_BLOCK), const(N_SG, GMLP_BLOCK, GMLP_BLOCK),
            const(N_SG, GMLP_BLOCK, 1), const(N_SG, GMLP_BLOCK, 1),
            const(D_GATE, D_MODEL), const(1, D_MODEL),
            const(1, D_MODEL), const(D_MODEL, LANES), const(D_MODEL, 2 * LANES), const(1, LANES),
        ],
        out_specs=[
            pl.BlockSpec((TM * ROW_TILES, LANES), lambda i: (i, 0)),
            const(N_DEC, D_GATE),
            pl.BlockSpec((SUBLANES, TM), lambda i: (0, i)),
            const(SUBLANES, LANES),
        ],
        out_shape=[jax.ShapeDtypeStruct((T * ROW_TILES, LANES), F32), jax.ShapeDtypeStruct((N_DEC, D_GATE), F32),
                   jax.ShapeDtypeStruct((SUBLANES, T), F32), jax.ShapeDtypeStruct((SUBLANES, LANES), F32)],
        scratch_shapes=[pltpu.VMEM((TM, D_GATE), BF16), pltpu.VMEM((SUBLANES, LANES), F32)],
        compiler_params=_cparams(),
        name="gmlp_layer",
    )(x_prompt, x_dec, nm, w_in, b_in, ln_g, ln_b, ws_p, ws_d, bs_p, bs_d, w_out, b_out,
      nf, *_router_split(wr), br)


def _router_body(hp_ref, hd_ref, nf_ref, wrh_ref, wr2_ref, br_ref, rows_ref, info_ref, cnt_ref, carry_ref):
    _route_init(carry_ref)
    h = jnp.where(pl.program_id(0) == N_TILES - 1, hd_ref[...], hp_ref[...])
    _store_rows(rows_ref, h, TM)
    _route_tile(h, nf_ref, wrh_ref, wr2_ref, br_ref, info_ref, cnt_ref, carry_ref)


def _router_split(wr):
    hi = wr.astype(BF16)
    return hi, jnp.concatenate([hi, (wr - hi.astype(F32)).astype(BF16)], axis=1)


def _route_init(carry_ref):
    @pl.when(pl.program_id(0) == 0)
    def _():
        carry_ref[...] = jnp.zeros_like(carry_ref)


def _route_tile(h, nf_ref, wrh_ref, wr2_ref, br_ref, info_ref, cnt_ref, carry_ref):
    xn = _rms(h, nf_ref[...])
    xh = xn.astype(BF16)
    xl = (xn - xh.astype(F32)).astype(BF16)
    both = _dot(xh, wr2_ref[...])
    logits = both[:, :LANES] + (_dot(xl, wrh_ref[...]) + both[:, LANES:]) + br_ref[...]
    lane = lax.broadcasted_iota(jnp.int32, (TM, LANES), 1).astype(F32)

    def first_max(vals):
        vmax = jnp.max(vals, axis=-1, keepdims=True)
        idx = jnp.min(jnp.where(vals == vmax, lane, float(LANES)), axis=-1, keepdims=True)
        return vmax, idx

    lg = jnp.where(lane < N_EGROUPS, logits, -jnp.inf)
    gmax, g_idx = first_max(lg)
    g_p = 1.0 / jnp.sum(jnp.exp(lg - gmax), axis=-1, keepdims=True)
    e_lo = N_EGROUPS + EXPERTS_PER_GROUP * g_idx
    le = jnp.where((lane >= e_lo) & (lane < e_lo + EXPERTS_PER_GROUP), logits, -jnp.inf)
    v1, i1 = first_max(le)
    v2, i2 = first_max(jnp.where(lane == i1, -jnp.inf, le))
    e2 = jnp.exp(v2 - v1)
    w1 = (1.0 / (1.0 + e2)) * g_p
    w2 = (e2 / (1.0 + e2)) * g_p
    a1 = i1 - e_lo
    a2 = i2 - e_lo
    lo = jnp.minimum(a1, a2)
    hi = jnp.maximum(a1, a2)
    pair = jnp.where(lo == 0.0, hi - 1.0, jnp.where(lo == 1.0, jnp.where(hi == 3.0, 3.0, 4.0), 5.0))
    ea = jnp.where(pair < 3.0, 0.0, jnp.where(pair < 5.0, 1.0, 3.0))
    ga = jnp.where(a1 == ea, w1, w2)
    gb = jnp.where(a1 == ea, w2, w1)
    bucket = g_idx * N_PAIRS + pair

    onehot = (lane == bucket).astype(F32)
    r = lax.broadcasted_iota(jnp.int32, (TM, TM), 0)
    c = lax.broadcasted_iota(jnp.int32, (TM, TM), 1)
    before = _dot((c < r).astype(BF16), onehot.astype(BF16))
    carry = carry_ref[0:1, :]
    rank = jnp.sum(onehot * (before + carry), axis=-1, keepdims=True)
    new_carry = carry + jnp.sum(onehot, axis=0, keepdims=True)
    carry_ref[...] = jnp.broadcast_to(new_carry, carry_ref.shape)
    cnt_ref[...] = jnp.broadcast_to(new_carry, cnt_ref.shape)
    info = jnp.where(lane == 0.0, bucket,
                     jnp.where(lane == 1.0, rank, jnp.where(lane == 2.0, ga, jnp.where(lane == 3.0, gb, 0.0))))
    info_ref[...] = info.T[:SUBLANES]


def _router(h_prompt, h_dec, nf, wr, br):
    const = lambda *shape: pl.BlockSpec(shape, lambda i: (0,) * len(shape))
    return pl.pallas_call(
        _router_body,
        grid=(N_TILES,),
        in_specs=[pl.BlockSpec((TM, D_MODEL), lambda i: (jnp.minimum(i, N_TILES - 2), 0)), const(N_DEC, D_MODEL),
                  const(1, D_MODEL), const(D_MODEL, LANES), const(D_MODEL, 2 * LANES), const(1, LANES)],
        out_specs=[pl.BlockSpec((TM * ROW_TILES, LANES), lambda i: (i, 0)),
                   pl.BlockSpec((SUBLANES, TM), lambda i: (0, i)),
                   const(SUBLANES, LANES)],
        out_shape=[jax.ShapeDtypeStruct((T * ROW_TILES, LANES), F32), jax.ShapeDtypeStruct((SUBLANES, T), F32),
                   jax.ShapeDtypeStruct((SUBLANES, LANES), F32)],
        scratch_shapes=[pltpu.VMEM((SUBLANES, LANES), F32)],
        compiler_params=_cparams(),
        name="moe_router",
    )(h_prompt, h_dec, nf, *_router_split(wr), br)


GATHER_UNROLL = 32


def _gather_rows_start(idx_ref, base, src_ref, dst_ref, sem, n):
    def group(g, carry):
        for u in range(GATHER_UNROLL):
            r = g * GATHER_UNROLL + u
            src = pl.multiple_of(idx_ref[base + r] * ROW_TILES, ROW_TILES)
            dst = pl.multiple_of(r * ROW_TILES, ROW_TILES)
            pltpu.make_async_copy(src_ref.at[pl.ds(src, ROW_TILES), :], dst_ref.at[pl.ds(dst, ROW_TILES), :],
                                  sem).start(priority=u % 2)
        return carry

    lax.fori_loop(0, n // GATHER_UNROLL, group, 0)


def _gather_rows_wait(src_ref, dst_ref, sem, n):
    pltpu.make_async_copy(src_ref.at[pl.ds(0, n * ROW_TILES), :], dst_ref.at[pl.ds(0, n * ROW_TILES), :], sem).wait()


def _gather_tile(idx_ref, src_ref, buf, sem, n, n_live=None, base_of=None, side_copy=None):
    i = pl.program_id(0)
    slot = lax.rem(i, 2)
    n_live = pl.num_programs(0) if n_live is None else n_live
    base_of = (lambda step: step * n) if base_of is None else base_of

    def start(step, s):
        _gather_rows_start(idx_ref, base_of(step), src_ref, buf.at[s], sem.at[s], n)
        if side_copy is not None:
            side_copy(step, s).start()

    @pl.when(i == 0)
    def _():
        start(0, 0)

    @pl.when(i + 1 < n_live)
    def _():
        start(i + 1, 1 - slot)

    @pl.when(i < n_live)
    def _():
        _gather_rows_wait(src_ref, buf.at[slot], sem.at[slot], n)
        if side_copy is not None:
            side_copy(i, slot).wait()

    return buf.at[slot]


def _gather_scratch(n):
    return [pltpu.VMEM((2, n * ROW_TILES, LANES), F32), pltpu.SemaphoreType.DMA((2,))]


GATE_WIN = TM + SUBLANES


def _moe_body(tok_ref, ea_ref, eb_ref, cha_ref, chb_ref, nlive_ref, first_ref, nrows_ref, rows_hbm, gates_hbm, nf_ref,
              w1a_ref, w3a_ref, w2a_ref, w1b_ref, w3b_ref, w2b_ref, out_ref,
              s1a, s3a, s2a, s1b, s3b, s2b, xbuf, xsem, gwin):
    i = pl.program_id(0)
    live = i < nlive_ref[0]

    def gate_copy(step, s):
        lo = pl.multiple_of((first_ref[step] // SUBLANES) * SUBLANES, SUBLANES)
        return pltpu.make_async_copy(gates_hbm.at[pl.ds(lo, GATE_WIN), :], gwin.at[s], xsem.at[s])

    x_ref = _gather_tile(tok_ref, rows_hbm, xbuf, xsem, TM, nlive_ref[0],
                         base_of=lambda step: first_ref[step], side_copy=gate_copy)

    @pl.when(cha_ref[i] == 1)
    def _():
        s1a[...] = w1a_ref[...].astype(BF16)
        s3a[...] = w3a_ref[...].astype(BF16)
        s2a[...] = w2a_ref[...].astype(BF16)

    @pl.when(chb_ref[i] == 1)
    def _():
        s1b[...] = w1b_ref[...].astype(BF16)
        s3b[...] = w3b_ref[...].astype(BF16)
        s2b[...] = w2b_ref[...].astype(BF16)

    @pl.when(live)
    def _():
        h = _load_rows(x_ref, TM)
        xn = _rms(h, nf_ref[...]).astype(BF16)
        g = gwin[lax.rem(i, 2), pl.ds(lax.rem(first_ref[i], SUBLANES), TM), :]
        row = lax.broadcasted_iota(jnp.int32, (TM, 1), 0)
        g = jnp.where(row < nrows_ref[i], g, 0.0)

        def ffn(w1, w3, w2):
            a = _dot(xn, w1[...])
            hdn = (a * (1.0 / (1.0 + jnp.exp(-a)))) * _dot(xn, w3[...])
            return _dot(hdn.astype(BF16), w2[...])

        y = g[:, 0:1] * ffn(s1a, s3a, s2a) + g[:, 1:2] * ffn(s1b, s3b, s2b)
        _store_rows(out_ref, h + y, TM)

    @pl.when(jnp.logical_not(live))
    def _():
        out_ref[...] = jnp.zeros_like(out_ref)


def _moe_ffn(layer, tok, ea, eb, cha, chb, n_live, first, nrows, rows, gates, nf, w1, w3, w2):
    wa = lambda shape: pl.BlockSpec((None, None) + shape, lambda i, tk, ea, eb, *_: (layer, ea[i], 0, 0))
    wb = lambda shape: pl.BlockSpec((None, None) + shape, lambda i, tk, ea, eb, *_: (layer, eb[i], 0, 0))
    up, down = (D_MODEL, D_EXPERT), (D_EXPERT, D_MODEL)
    return pl.pallas_call(
        _moe_body,
        grid_spec=pltpu.PrefetchScalarGridSpec(
            num_scalar_prefetch=8,
            grid=(MOE_TILES,),
            in_specs=[
                pl.BlockSpec(memory_space=pl.ANY),
                pl.BlockSpec(memory_space=pl.ANY),
                pl.BlockSpec((1, D_MODEL), lambda i, *_: (0, 0)),
                wa(up), wa(up), wa(down), wb(up), wb(up), wb(down),
            ],
            out_specs=pl.BlockSpec((TM * ROW_TILES, LANES), lambda i, *_: (i, 0)),
            scratch_shapes=[pltpu.VMEM(up, BF16), pltpu.VMEM(up, BF16), pltpu.VMEM(down, BF16),
                            pltpu.VMEM(up, BF16), pltpu.VMEM(up, BF16), pltpu.VMEM(down, BF16)]
            + _gather_scratch(TM) + [pltpu.VMEM((2, GATE_WIN, 2), F32)],
        ),
        out_shape=jax.ShapeDtypeStruct((P_ROWS * ROW_TILES, LANES), F32),
        compiler_params=_cparams(),
        name="moe_ffn",
    )(tok, ea, eb, cha, chb, n_live, first, nrows, rows, gates, nf, w1, w3, w2, w1, w3, w2)


def _router_weights(layer, w_group, b_group, w_expert, b_expert):
    wr = jnp.zeros((D_MODEL, LANES), F32)
    wr = wr.at[:, :N_EGROUPS].set(w_group[layer]).at[:, N_EGROUPS:N_EGROUPS + N_EXPERTS].set(w_expert[layer])
    br = jnp.zeros((1, LANES), F32)
    br = br.at[0, :N_EGROUPS].set(b_group[layer]).at[0, N_EGROUPS:N_EGROUPS + N_EXPERTS].set(b_expert[layer])
    return wr, br


def _moe_layer(layer, rows, info, cnt, nf, w1, w3, w2):
    bucket = info[0].astype(jnp.int32)
    rank = info[1].astype(jnp.int32)
    counts = cnt[0, :N_BUCKETS].astype(jnp.int32)
    n_tiles = (counts + TM - 1) // TM
    tile_end = jnp.cumsum(n_tiles)
    tile_start = tile_end - n_tiles
    start_of = jnp.sum(jnp.where(bucket[:, None] == jnp.arange(N_BUCKETS)[None, :], tile_start[None, :], 0), axis=1)
    pos = start_of * TM + rank
    _, tok, ga, gb = lax.sort((pos, jnp.arange(T, dtype=jnp.int32), info[2], info[3]), num_keys=1)
    tok = jnp.concatenate([tok, jnp.zeros((TM,), jnp.int32)])
    gates = jnp.concatenate([jnp.stack([ga, gb], axis=1), jnp.zeros((TM + SUBLANES, 2), F32)])
    total = tile_end[-1]
    j = jnp.minimum(jnp.arange(MOE_TILES), total - 1)
    tb = jnp.sum((j[:, None] >= tile_end[None, :]).astype(jnp.int32), axis=1)
    tokens_before = jnp.cumsum(counts) - counts
    in_bucket = (j - tile_start[tb]) * TM
    first = (tokens_before[tb] + in_bucket).astype(jnp.int32)
    nrows = jnp.clip(counts[tb] - in_bucket, 0, TM).astype(jnp.int32)
    grp, pair = tb // N_PAIRS, tb % N_PAIRS
    ea = (grp * EXPERTS_PER_GROUP + jnp.asarray(PAIR_A, jnp.int32)[pair]).astype(jnp.int32)
    eb = (grp * EXPERTS_PER_GROUP + jnp.asarray(PAIR_B, jnp.int32)[pair]).astype(jnp.int32)
    tile0 = jnp.arange(MOE_TILES) == 0
    cha = (tile0 | (ea != jnp.roll(ea, 1))).astype(jnp.int32)
    chb = (tile0 | (eb != jnp.roll(eb, 1))).astype(jnp.int32)
    n_live = total.reshape(1).astype(jnp.int32)
    return _moe_ffn(layer, tok, ea, eb, cha, chb, n_live, first, nrows, rows, gates, nf, w1, w3, w2), pos


VT_ROWS = LANES + 16


def _pair_seg_matrix():
    grp = np.arange(2 * HEAD_W) // HEAD_W
    return jnp.asarray(grp[:, None] == grp[None, :], BF16)


def _rope_swap(x):
    lane = lax.broadcasted_iota(jnp.int32, x.shape, 1)
    return jnp.where(lane < ROPE_LO + ROPE_HALF, pltpu.roll(x, LANES - ROPE_HALF, 1), pltpu.roll(x, ROPE_HALF, 1))


def _expand_k(cb, kr, wuk_ref, seg, kg_ref, k_ref):
    kn = _dot(cb, wuk_ref[...])
    pairs = [kn[:, pr * 2 * HEAD_W:(pr + 1) * 2 * HEAD_W] for pr in range(N_HEADS // 2)]
    ms = [_split_dot(x * x, seg) * (1.0 / NOPE_DIM) for x in pairs]
    kg2 = jnp.concatenate([kg_ref[...]] * 2, axis=1)
    kr2 = jnp.concatenate([kr, kr], axis=1)
    for pr in range(N_HEADS // 2):
        k_ref[:, pr * 2 * HEAD_W:(pr + 1) * 2 * HEAD_W] = (pairs[pr] * lax.rsqrt(ms[pr] + EPS) * kg2 + kr2).astype(BF16)


def _mla_proj_body(pos_ref, sorted_hbm, cos_ref, sin_ref, cost_ref, sint_ref, kvn_ref, wdkv_ref, kvan_ref, krg_ref,
                   wuk_ref, wuv_ref, wuvt_ref, kg_ref, nmq_ref, wdq_ref, qan_ref, wuqt_ref, qg_ref, seg_ref,
                   rows_ref, ckvp_ref, ckvd_ref, krp_ref, krd_ref, k_ref, vdec_ref, vt_ref, qt_ref, qdec_ref,
                   xbuf, xsem):
    is_dec = pl.program_id(0) == N_TILES - 1
    x_ref = _gather_tile(pos_ref, sorted_hbm, xbuf, xsem, TM)
    rows_ref[...] = x_ref[...]
    h = _load_rows(x_ref, TM)
    hn = h * lax.rsqrt(jnp.mean(h * h, axis=-1, keepdims=True) + EPS)
    c = _dot((hn * kvn_ref[...]).astype(BF16), wdkv_ref[...])
    ckv = _rms(c[:, :KV_RANK], kvan_ref[...])
    kr = c[:, KV_RANK:]
    kr = kr * lax.rsqrt(jnp.sum(kr * kr, axis=-1, keepdims=True) * (1.0 / ROPE_DIM) + EPS) * krg_ref[...]
    kr = kr * cos_ref[...] + _rope_swap(kr) * sin_ref[...]
    cb = ckv.astype(BF16)
    _expand_k(cb, kr, wuk_ref, seg_ref[...], kg_ref, k_ref)
    vt = _dot(wuvt_ref[...], ckv.T.astype(BF16)).astype(BF16)
    for pr in range(N_HEADS // 2):
        vt_ref[0, pr * VT_ROWS:pr * VT_ROWS + LANES, :] = vt[pr * LANES:(pr + 1) * LANES]
        vt_ref[0, pr * VT_ROWS + LANES:(pr + 1) * VT_ROWS, :] = jnp.ones((VT_ROWS - LANES, TM), BF16)
    cq = _rms(_dot((hn * nmq_ref[...]).astype(BF16), wdq_ref[...]), qan_ref[...])
    qt = _dot(wuqt_ref[...], cq.T.astype(BF16))
    cost, sint, qg = cost_ref[0], sint_ref[0], qg_ref[...]
    for hh in range(N_HEADS):
        x = qt[hh * HEAD_W:(hh + 1) * HEAD_W, :]
        xn, xr = x[:NOPE_DIM], x[ROPE_LO:ROPE_LO + ROPE_DIM]
        xn = xn * lax.rsqrt(jnp.mean(xn * xn, axis=0, keepdims=True) + EPS) * qg[:NOPE_DIM]
        xr = xr * lax.rsqrt(jnp.mean(xr * xr, axis=0, keepdims=True) + EPS) * qg[ROPE_LO:ROPE_LO + ROPE_DIM]
        x1, x2 = xr[:ROPE_HALF], xr[ROPE_HALF:]
        qh = jnp.concatenate([xn, x1 * cost - x2 * sint, x1 * sint + x2 * cost,
                              jnp.zeros((HEAD_W - NOPE_DIM - ROPE_DIM, TM), F32)], axis=0) * Q_SCALE
        qt_ref[0, hh * HEAD_W:(hh + 1) * HEAD_W, :] = qh.astype(BF16)

    @pl.when(jnp.logical_not(is_dec))
    def _():
        ckvp_ref[...] = ckv
        krp_ref[...] = kr[:, ROPE_LO:ROPE_LO + ROPE_DIM]

    @pl.when(is_dec)
    def _():
        ckvd_ref[...] = ckv
        krd_ref[...] = kr[:, ROPE_LO:ROPE_LO + ROPE_DIM]
        vdec_ref[...] = _dot(cb, wuv_ref[...]).astype(BF16)
        for hh in range(N_HEADS):
            rows = slice(hh * HEAD_W, (hh + 1) * HEAD_W)
            qdec_ref[:, rows] = qt_ref[0, rows, :].astype(F32).T.astype(BF16)


def _mla_proj(pos, sorted_rows, cos_t, sin_t, cos_tt, sin_tt, kvn, wdkv, kvan, krg, wuk, wuv, kg, nmq, wdq, qan, wuq, qg):
    const = lambda *shape: pl.BlockSpec(shape, lambda i, p: (0,) * len(shape))
    tab_tile = lambda i: jnp.where(i < N_PROMPT // TM, i % (SEQ // TM), SEQ // TM)
    tab = pl.BlockSpec((TM, LANES), lambda i, p: (tab_tile(i), 0))
    tab_t = pl.BlockSpec((1, ROPE_HALF, TM), lambda i, p: (tab_tile(i), 0, 0))
    row = lambda w: pl.BlockSpec((TM, w), lambda i, p: (i, 0))
    prow = lambda w: pl.BlockSpec((TM, w), lambda i, p: (jnp.minimum(i, N_TILES - 2), 0))
    return pl.pallas_call(
        _mla_proj_body,
        grid_spec=pltpu.PrefetchScalarGridSpec(
            num_scalar_prefetch=1,
            grid=(N_TILES,),
            in_specs=[
                pl.BlockSpec(memory_space=pl.ANY), tab, tab, tab_t, tab_t,
                const(1, D_MODEL), const(D_MODEL, KV_RANK + LANES), const(1, KV_RANK), const(1, LANES),
                const(KV_RANK, N_HEADS * HEAD_W), const(KV_RANK, N_HEADS * V_DIM), const(N_HEADS * V_DIM, KV_RANK),
                const(1, LANES),
                const(1, D_MODEL), const(D_MODEL, Q_RANK), const(1, Q_RANK), const(N_HEADS * HEAD_W, Q_RANK),
                const(HEAD_W, TM), const(2 * HEAD_W, 2 * HEAD_W),
            ],
            out_specs=[pl.BlockSpec((TM * ROW_TILES, LANES), lambda i, p: (i, 0)),
                       prow(KV_RANK), const(N_DEC, KV_RANK), prow(ROPE_DIM), const(N_DEC, ROPE_DIM),
                       row(N_HEADS * HEAD_W), const(N_DEC, N_HEADS * V_DIM),
                       pl.BlockSpec((1, N_HEADS // 2 * VT_ROWS, TM), lambda i, p: (i, 0, 0)),
                       pl.BlockSpec((1, N_HEADS * HEAD_W, TM), lambda i, p: (i, 0, 0)),
                       const(N_DEC, N_HEADS * HEAD_W)],
            scratch_shapes=_gather_scratch(TM),
        ),
        out_shape=[
            jax.ShapeDtypeStruct((T * ROW_TILES, LANES), F32),
            jax.ShapeDtypeStruct((N_PROMPT, KV_RANK), F32), jax.ShapeDtypeStruct((N_DEC, KV_RANK), F32),
            jax.ShapeDtypeStruct((N_PROMPT, ROPE_DIM), F32), jax.ShapeDtypeStruct((N_DEC, ROPE_DIM), F32),
            jax.ShapeDtypeStruct((T, N_HEADS * HEAD_W), BF16), jax.ShapeDtypeStruct((N_DEC, N_HEADS * V_DIM), BF16),
            jax.ShapeDtypeStruct((N_TILES, N_HEADS // 2 * VT_ROWS, TM), BF16),
            jax.ShapeDtypeStruct((N_TILES, N_HEADS * HEAD_W, TM), BF16),
            jax.ShapeDtypeStruct((N_DEC, N_HEADS * HEAD_W), BF16),
        ],
        compiler_params=_cparams(),
        name="mla_proj",
    )(pos, sorted_rows, cos_t, sin_t, cos_tt, sin_tt, kvn, wdkv, kvan, krg, wuk, wuv, wuv.T, kg, nmq, wdq, qan,
      wuq.T, jnp.broadcast_to(qg.reshape(HEAD_W, 1), (HEAD_W, TM)), _pair_seg_matrix())


def _cache_kv_body(ckv_ref, kr_ref, place_ref, wuk_ref, wuv_ref, kg_ref, seg_ref, k_ref, v_ref):
    kr = _dot(kr_ref[...].astype(BF16), place_ref[...])
    cb = ckv_ref[...].astype(BF16)
    _expand_k(cb, kr, wuk_ref, seg_ref[...], kg_ref, k_ref)
    v_ref[...] = _dot(cb, wuv_ref[...]).astype(BF16)


CACHE_ROWS = 1024
assert PAST_LEN % CACHE_ROWS == 0


def _cache_kv(ckv, kr, wuk, wuv, kg):
    n = ckv.shape[0]
    place = jnp.asarray(np.arange(ROPE_DIM)[:, None] + ROPE_LO == np.arange(LANES)[None, :], BF16)
    const = lambda *shape: pl.BlockSpec(shape, lambda i: (0,) * len(shape))
    row = lambda w: pl.BlockSpec((CACHE_ROWS, w), lambda i: (i, 0))
    return pl.pallas_call(
        _cache_kv_body,
        grid=(n // CACHE_ROWS,),
        in_specs=[row(KV_RANK), row(ROPE_DIM), const(ROPE_DIM, LANES), const(KV_RANK, N_HEADS * HEAD_W),
                  const(KV_RANK, N_HEADS * V_DIM), const(1, LANES), const(2 * HEAD_W, 2 * HEAD_W)],
        out_specs=[row(N_HEADS * HEAD_W), row(N_HEADS * V_DIM)],
        out_shape=[jax.ShapeDtypeStruct((n, N_HEADS * HEAD_W), BF16), jax.ShapeDtypeStruct((n, N_HEADS * V_DIM), BF16)],
        compiler_params=_cparams(),
        name="cache_kv",
    )(ckv, kr, place, wuk, wuv, kg, _pair_seg_matrix())


TQ = 256
TK = 256
assert TQ == TK and TQ % CHUNK == 0
SCORE_LOOKAHEAD = 8


def _qk(q, k):
    return lax.dot_general(q, k, (((1,), (1,)), ((), ())), preferred_element_type=F32)


def _merge_heads(o_ref, outs, rows):
    lane = lax.broadcasted_iota(jnp.int32, (rows, LANES), 1)
    for pr in range(N_HEADS // 2):
        o_ref[:, pr * LANES:(pr + 1) * LANES] = jnp.where(lane < V_DIM, outs[2 * pr], outs[2 * pr + 1]).astype(BF16)


def _prompt_attn_body(qt_ref, k_ref, vt_ref, rows_ref, wo_ref, out_ref, m_scr, acc_scr):
    qi = pl.program_id(1)
    m_scr[...] = jnp.full(m_scr.shape, NEG, F32)
    acc_scr[...] = jnp.zeros(acc_scr.shape, F32)

    def all_heads(j, mask):
        ks = pl.ds(pl.multiple_of(j * TK, TK), TK)

        def scores(hh):
            hcols = slice(hh * HEAD_W, (hh + 1) * HEAD_W)
            return _dot(k_ref[ks, hcols], qt_ref[0, hcols, :])

        ahead = [scores(hh) for hh in range(SCORE_LOOKAHEAD)]
        for hh in range(N_HEADS):
            vrows = slice((hh // 2) * VT_ROWS, (hh // 2 + 1) * VT_ROWS)
            s = ahead.pop(0)
            if hh + SCORE_LOOKAHEAD < N_HEADS:
                ahead.append(scores(hh + SCORE_LOOKAHEAD))
            if mask is not None:
                s = jnp.where(mask, s, NEG)
            m_old = m_scr[hh]
            m_new = jnp.maximum(m_old, jnp.max(s, axis=0, keepdims=True))
            p = jnp.exp2(s - m_new).astype(BF16)
            m_scr[hh] = m_new
            acc_scr[hh] = jnp.exp2(m_old - m_new) * acc_scr[hh] + _dot(vt_ref[j, vrows, :], p)

    def step(j, carry):
        all_heads(j, None)
        return carry

    lax.fori_loop(0, qi, step, 0)
    kc = lax.broadcasted_iota(jnp.int32, (TK, TQ), 0) // CHUNK
    qc = lax.broadcasted_iota(jnp.int32, (TK, TQ), 1) // CHUNK
    all_heads(qi, kc <= qc)
    row = lax.broadcasted_iota(jnp.int32, (LANES, TQ), 0)
    pairs = []
    for pr in range(N_HEADS // 2):
        even = acc_scr[2 * pr, :LANES] / acc_scr[2 * pr, LANES:LANES + 1]
        odd = acc_scr[2 * pr + 1, :LANES] / acc_scr[2 * pr + 1, LANES:LANES + 1]
        pairs.append(jnp.where(row < V_DIM, even, odd))
    o = jnp.concatenate(pairs, axis=0).T.astype(BF16)
    out_ref[...] = _load_rows(rows_ref, TQ) + _dot(o, wo_ref[...])


def _prompt_attn(qt, k, vt, rows, wo):
    nq = SEQ // TQ
    return pl.pallas_call(
        _prompt_attn_body,
        grid=(BATCH, nq),
        in_specs=[
            pl.BlockSpec((1, N_HEADS * HEAD_W, TQ), lambda b, i: (b * nq + i, 0, 0)),
            pl.BlockSpec((SEQ, N_HEADS * HEAD_W), lambda b, i: (b, 0)),
            pl.BlockSpec((SEQ // TK, N_HEADS // 2 * VT_ROWS, TK), lambda b, i: (b, 0, 0)),
            pl.BlockSpec((TQ * ROW_TILES, LANES), lambda b, i: (b * nq + i, 0)),
            pl.BlockSpec((N_HEADS * V_DIM, D_MODEL), lambda b, i: (0, 0)),
        ],
        out_specs=pl.BlockSpec((TQ, D_MODEL), lambda b, i: (b * nq + i, 0)),
        out_shape=jax.ShapeDtypeStruct((N_PROMPT, D_MODEL), F32),
        scratch_shapes=[pltpu.VMEM((N_HEADS, 1, TQ), F32), pltpu.VMEM((N_HEADS, VT_ROWS, TQ), F32)],
        compiler_params=_cparams(2),
        name="prompt_attn",
    )(qt, k, vt, rows, wo)


def _sample_attn_body(q_ref, kc_ref, vc_ref, kn_ref, vn_ref, rows_ref, wo_ref, out_ref, o_scr):
    outs = []
    for hh in range(N_HEADS):
        hcols = slice(hh * HEAD_W, (hh + 1) * HEAD_W)
        vcols = slice((hh // 2) * LANES, (hh // 2 + 1) * LANES)
        q = q_ref[:, hcols]
        sc = _qk(q, kc_ref[:, hcols])
        sn = _qk(q, kn_ref[:, hcols])
        m = jnp.maximum(jnp.max(sc, axis=-1, keepdims=True), jnp.max(sn, axis=-1, keepdims=True))
        pc = jnp.exp2(sc - m)
        pn = jnp.exp2(sn - m)
        l = jnp.sum(pc, axis=-1, keepdims=True) + jnp.sum(pn, axis=-1, keepdims=True)
        acc = _dot(pc.astype(BF16), vc_ref[:, vcols]) + _dot(pn.astype(BF16), vn_ref[:, vcols])
        outs.append(acc / l)
    _merge_heads(o_scr, outs, DEC_SEQ)
    out_ref[...] = _load_rows(rows_ref, DEC_SEQ) + _dot(o_scr[...], wo_ref[...])


def _sample_attn(q, kc, vc, kn, vn, rows, wo):
    off = N_PROMPT // DEC_SEQ
    return pl.pallas_call(
        _sample_attn_body,
        grid=(DEC_BATCH,),
        in_specs=[
            pl.BlockSpec((DEC_SEQ, N_HEADS * HEAD_W), lambda b: (b, 0)),
            pl.BlockSpec((PAST_LEN, N_HEADS * HEAD_W), lambda b: (b, 0)),
            pl.BlockSpec((PAST_LEN, N_HEADS * V_DIM), lambda b: (b, 0)),
            pl.BlockSpec((DEC_SEQ, N_HEADS * HEAD_W), lambda b: (off + b, 0)),
            pl.BlockSpec((DEC_SEQ, N_HEADS * V_DIM), lambda b: (b, 0)),
            pl.BlockSpec((DEC_SEQ * ROW_TILES, LANES), lambda b: (off + b, 0)),
            pl.BlockSpec((N_HEADS * V_DIM, D_MODEL), lambda b: (0, 0)),
        ],
        out_specs=pl.BlockSpec((DEC_SEQ, D_MODEL), lambda b: (b, 0)),
        out_shape=jax.ShapeDtypeStruct((N_DEC, D_MODEL), F32),
        scratch_shapes=[pltpu.VMEM((DEC_SEQ, N_HEADS * V_DIM), BF16)],
        compiler_params=_cparams(),
        name="sample_attn",
    )(q, kc, vc, kn, vn, rows, wo)


def _finish_body(pos_ref, sorted_hbm, yp_ref, ys_ref, xbuf, xsem):
    i = pl.program_id(0)
    y = _load_rows(_gather_tile(pos_ref, sorted_hbm, xbuf, xsem, TM), TM)

    @pl.when(i < N_TILES - 1)
    def _():
        yp_ref[...] = y

    @pl.when(i == N_TILES - 1)
    def _():
        ys_ref[...] = y


def _finish(pos, sorted_rows):
    return pl.pallas_call(
        _finish_body,
        grid_spec=pltpu.PrefetchScalarGridSpec(
            num_scalar_prefetch=1,
            grid=(N_TILES,),
            in_specs=[pl.BlockSpec(memory_space=pl.ANY)],
            out_specs=[pl.BlockSpec((TM, D_MODEL), lambda i, p: (jnp.minimum(i, N_TILES - 2), 0)),
                       pl.BlockSpec((N_DEC, D_MODEL), lambda i, p: (0, 0))],
            scratch_shapes=_gather_scratch(TM),
        ),
        out_shape=[jax.ShapeDtypeStruct((N_PROMPT, D_MODEL), F32), jax.ShapeDtypeStruct((N_DEC, D_MODEL), F32)],
        compiler_params=_cparams(),
        name="finish",
    )(pos, sorted_rows)


def _rope_tables():
    half = ROPE_DIM // 2
    inv_freq = ROPE_THETA ** (-jnp.arange(half, dtype=F32) / half)
    dec_pos = PAST_LEN + jnp.tile(jnp.arange(DEC_SEQ, dtype=jnp.int32), DEC_BATCH)
    pos = jnp.concatenate([jnp.arange(SEQ, dtype=jnp.int32), dec_pos])
    ang = pos.astype(F32)[:, None] * inv_freq[None, :]
    cos, sin = jnp.cos(ang), jnp.sin(ang)
    n = pos.shape[0]
    cos_t = jnp.ones((n, LANES), F32).at[:, ROPE_LO:ROPE_LO + ROPE_DIM].set(jnp.concatenate([cos, cos], axis=1))
    sin_t = jnp.zeros((n, LANES), F32).at[:, ROPE_LO:ROPE_LO + ROPE_DIM].set(jnp.concatenate([-sin, sin], axis=1))
    to_tiles = lambda a: a.reshape(n // TM, TM, half).transpose(0, 2, 1)
    return cos_t, sin_t, to_tiles(cos), to_tiles(sin)


def _on_lanes(vec, lo):
    return jnp.zeros((1, LANES), F32).at[0, lo:lo + vec.shape[0]].set(vec)


def kernel(x_prompt, x_sample, cache_ckv, cache_krope, norm_mix, norm_ffn, gm_w_in, gm_b_in, gm_ln_g, gm_ln_b, gm_w_s, gm_b_s, gm_w_out, gm_b_out, kv_norm, w_dkv, kv_a_norm, k_rope_norm, w_uk, w_uv, k_nope_norm, w_dq, q_a_norm, w_uq, q_nope_norm, q_rope_norm, w_o, moe_w_group, moe_b_group, moe_w_expert, moe_b_expert, moe_w1, moe_w3, moe_w2):
    nf0, nf1 = norm_ffn[0].reshape(1, D_MODEL), norm_ffn[1].reshape(1, D_MODEL)

    idx = np.arange(GMLP_BLOCK)
    allowed = (idx[None, :] // CHUNK) <= (idx[:, None] // CHUNK)
    ws_p = jnp.where(allowed[None], gm_w_s[0], 0.0).astype(BF16)
    same_seq = (idx[None, :] // DEC_SEQ) == (idx[:, None] // DEC_SEQ)
    ws_d = jnp.where(same_seq[None], jnp.tile(gm_w_s[0][:, :DEC_SEQ, :DEC_SEQ], (1, GMLP_BLOCK // DEC_SEQ, GMLP_BLOCK // DEC_SEQ)), 0.0).astype(BF16)
    bs_p = gm_b_s[0][:, :, None]
    bs_d = jnp.tile(gm_b_s[0][:, :DEC_SEQ], (1, GMLP_BLOCK // DEC_SEQ))[:, :, None]
    rows, v_rows, info, cnt = _gmlp_layer(
        x_prompt.reshape(N_PROMPT, D_MODEL), x_sample.reshape(N_DEC, D_MODEL),
        norm_mix[0].reshape(1, -1), gm_w_in[0].astype(BF16), gm_b_in[0].reshape(1, -1),
        gm_ln_g[0].reshape(1, -1), gm_ln_b[0].reshape(1, -1), ws_p, ws_d, bs_p, bs_d,
        gm_w_out[0].astype(BF16), gm_b_out[0].reshape(1, -1),
        nf0, *_router_weights(0, moe_w_group, moe_b_group, moe_w_expert, moe_b_expert))
    sorted_rows, pos = _moe_layer(0, rows, info, cnt, nf0, moe_w1, moe_w3, moe_w2)

    cos_t, sin_t, cos_tt, sin_tt = _rope_tables()
    wdkv = jnp.zeros((D_MODEL, KV_RANK + LANES), F32).at[:, :KV_RANK].set(w_dkv[:, :KV_RANK])
    wdkv = wdkv.at[:, KV_RANK + ROPE_LO:KV_RANK + ROPE_LO + ROPE_DIM].set(w_dkv[:, KV_RANK:]).astype(BF16)
    wuk = jnp.zeros((KV_RANK, N_HEADS, HEAD_W), F32).at[:, :, :NOPE_DIM].set(w_uk).reshape(KV_RANK, -1).astype(BF16)
    wuv = w_uv.reshape(KV_RANK, -1).astype(BF16)
    wuq = jnp.zeros((Q_RANK, N_HEADS, HEAD_W), F32).at[:, :, :NOPE_DIM + ROPE_DIM].set(w_uq[0]).reshape(Q_RANK, -1).astype(BF16)
    kg = _on_lanes(k_nope_norm, 0)
    krg = _on_lanes(k_rope_norm, ROPE_LO)
    qg = _on_lanes(jnp.concatenate([q_nope_norm[0], q_rope_norm[0]]), 0)
    rows, ckv_p, ckv_d, krope_p, krope_d, k_new, v_dec, vt_new, qt, q_dec = _mla_proj(
        pos, sorted_rows, cos_t, sin_t, cos_tt, sin_tt, kv_norm.reshape(1, -1), wdkv, kv_a_norm.reshape(1, -1), krg, wuk, wuv, kg,
        norm_mix[1].reshape(1, -1), w_dq[0].astype(BF16), q_a_norm[0].reshape(1, -1), wuq, qg)
    k_cache, v_cache = _cache_kv(cache_ckv.reshape(-1, KV_RANK), cache_krope.reshape(-1, ROPE_DIM), wuk, wuv, kg)

    wo = w_o[0].astype(BF16)
    h_prompt = _prompt_attn(qt, k_new, vt_new, rows, wo)
    h_dec = _sample_attn(q_dec, k_cache, v_cache, k_new, v_dec, rows, wo)
    rows, info, cnt = _router(h_prompt, h_dec, nf1,
                              *_router_weights(1, moe_w_group, moe_b_group, moe_w_expert, moe_b_expert))
    sorted_rows, pos = _moe_layer(1, rows, info, cnt, nf1, moe_w1, moe_w3, moe_w2)
    y_prompt, y_sample = _finish(pos, sorted_rows)

    return (y_prompt.reshape(BATCH, SEQ, D_MODEL), y_sample.reshape(DEC_BATCH, DEC_SEQ, D_MODEL),
            ckv_p.reshape(BATCH, SEQ, KV_RANK), krope_p.reshape(BATCH, SEQ, ROPE_DIM),
            ckv_d.reshape(DEC_BATCH, DEC_SEQ, KV_RANK), krope_d.reshape(DEC_BATCH, DEC_SEQ, ROPE_DIM),
            v_rows.reshape(1, DEC_BATCH, DEC_SEQ, D_GATE))
```

```python
import functools

import jax
import jax.numpy as jnp
import numpy as np
from jax import lax
from jax.experimental import pallas as pl
from jax.experimental.pallas import tpu as pltpu

F32 = jnp.float32
BF16 = jnp.bfloat16

D_MODEL = 1024
BATCH = 8
SEQ = 2048
DEC_BATCH = 16
DEC_SEQ = 16
PAST_LEN = 2048
CHUNK = 64
GMLP_BLOCK = 128
D_GATE = 2 * D_MODEL
N_SG = 8
SG_W = D_GATE // N_SG
N_HEADS = 8
NOPE_DIM = 64
ROPE_DIM = 32
V_DIM = 64
Q_RANK = 384
KV_RANK = 256
ROPE_THETA = 10000.0
SCALE = (NOPE_DIM + ROPE_DIM) ** -0.5
Q_SCALE = SCALE * float(np.log2(np.e))
N_EGROUPS = 4
EXPERTS_PER_GROUP = 4
N_EXPERTS = N_EGROUPS * EXPERTS_PER_GROUP
D_EXPERT = 512
EPS = 1e-6
NEG = -1e30

LANES = 128
SUBLANES = 8
ROW_TILES = D_MODEL // LANES
assert ROW_TILES == SUBLANES

N_PROMPT = BATCH * SEQ
N_DEC = DEC_BATCH * DEC_SEQ
T = N_PROMPT + N_DEC
TM = 256
assert N_PROMPT % TM == 0 and N_DEC == TM
N_TILES = T // TM
HEAD_W = LANES
ROPE_LO = NOPE_DIM
ROPE_HALF = ROPE_DIM // 2

PAIR_A = (0, 0, 0, 1, 1, 3)
PAIR_B = (1, 2, 3, 3, 2, 2)
N_PAIRS = 6
N_BUCKETS = N_EGROUPS * N_PAIRS
BUCKET_ROWS = 32
assert N_BUCKETS <= BUCKET_ROWS and BUCKET_ROWS % SUBLANES == 0
MOE_TILES = (T + N_BUCKETS * (TM - 1) + TM - 1) // TM
P_ROWS = MOE_TILES * TM

VMEM_LIMIT = 56 * 1024 * 1024


def _cparams(n_axes=1, vmem=VMEM_LIMIT):
    return pltpu.CompilerParams(dimension_semantics=("arbitrary",) * n_axes, vmem_limit_bytes=vmem)


def _rms(x, g):
    return x * lax.rsqrt(jnp.mean(x * x, axis=-1, keepdims=True) + EPS) * g


def _load_rows(ref, n):
    return jnp.concatenate([ref[pl.ds(s, n, stride=ROW_TILES), :] for s in range(ROW_TILES)], axis=1)


def _store_rows(ref, x, n):
    for s in range(ROW_TILES):
        ref[pl.ds(s, n, stride=ROW_TILES), :] = x[:, s * LANES:(s + 1) * LANES]


def _dot(a, b):
    return jnp.dot(a, b, preferred_element_type=F32)


def _split_dot(x, m):
    hi = x.astype(BF16)
    lo = (x - hi.astype(F32)).astype(BF16)
    return _dot(hi, m) + _dot(lo, m)


GELU_K1 = float(-2.0 * np.sqrt(2.0 / np.pi) * np.log2(np.e))
GELU_K3 = GELU_K1 * 0.044715


def _gmlp_body(xp_ref, xd_ref, nm_ref, win_ref, bin_ref, lng_ref, lnb_ref, wsp_ref, wsd_ref, bsp_ref, bsd_ref,
               wout_ref, bout_ref, nf_ref, wrh_ref, wr2_ref, br_ref,
               rows_ref, v_ref, info_ref, cnt_ref, gated_ref, carry_ref):
    i = pl.program_id(0)
    is_dec = i == N_TILES - 1
    _route_init(carry_ref)
    x = jnp.where(is_dec, xd_ref[...], xp_ref[...])
    xn = _rms(x, nm_ref[...]).astype(BF16)
    z = _dot(xn, win_ref[...]) + bin_ref[...]
    z = z / (1.0 + jnp.exp2(z * (GELU_K1 + GELU_K3 * (z * z))))
    u = z[:, :D_GATE]
    v = z[:, D_GATE:]
    mu = jnp.mean(v, axis=-1, keepdims=True)
    vc = v - mu
    var = jnp.mean(vc * vc, axis=-1, keepdims=True)
    v = vc * lax.rsqrt(var + EPS) * lng_ref[...] + lnb_ref[...]

    v_ref[...] = v
    vb = v.astype(BF16)
    for g in range(N_SG):
        ws = jnp.where(is_dec, wsd_ref[g], wsp_ref[g])
        bs = jnp.where(is_dec, bsd_ref[g], bsp_ref[g])
        for b in range(TM // GMLP_BLOCK):
            rows = slice(b * GMLP_BLOCK, (b + 1) * GMLP_BLOCK)
            cols = slice(g * SG_W, (g + 1) * SG_W)
            s = _dot(ws, vb[rows, cols]) + bs
            gated_ref[rows, cols] = (u[rows, cols] * s).astype(BF16)
    h = x + _dot(gated_ref[...], wout_ref[...]) + bout_ref[...]
    _store_rows(rows_ref, h, TM)
    _route_tile(h, nf_ref, wrh_ref, wr2_ref, br_ref, info_ref, cnt_ref, carry_ref)


def _gmlp_layer(x_prompt, x_dec, nm, w_in, b_in, ln_g, ln_b, ws_p, ws_d, bs_p, bs_d, w_out, b_out, nf, wr, br):
    const = lambda *shape: pl.BlockSpec(shape, lambda i: (0,) * len(shape))
    return pl.pallas_call(
        _gmlp_body,
        grid=(N_TILES,),
        in_specs=[
            pl.BlockSpec((TM, D_MODEL), lambda i: (jnp.minimum(i, N_TILES - 2), 0)), const(N_DEC, D_MODEL),
            const(1, D_MODEL), const(D_MODEL, 2 * D_GATE), const(1, 2 * D_GATE),
            const(1, D_GATE), const(1, D_GATE),
            const(N_SG, GMLP_BLOCK, GMLP_BLOCK), const(N_SG, GMLP_BLOCK, GMLP_BLOCK),
            const(N_SG, GMLP_BLOCK, 1), const(N_SG, GMLP_BLOCK, 1),
            const(D_GATE, D_MODEL), const(1, D_MODEL),
            const(1, D_MODEL), const(D_MODEL, LANES), const(D_MODEL, 2 * LANES), const(1, LANES),
        ],
        out_specs=[
            pl.BlockSpec((TM * ROW_TILES, LANES), lambda i: (i, 0)),
            const(N_DEC, D_GATE),
            pl.BlockSpec((SUBLANES, TM), lambda i: (0, i)),
            const(BUCKET_ROWS, LANES),
        ],
        out_shape=[jax.ShapeDtypeStruct((T * ROW_TILES, LANES), F32), jax.ShapeDtypeStruct((N_DEC, D_GATE), F32),
                   jax.ShapeDtypeStruct((SUBLANES, T), F32), jax.ShapeDtypeStruct((BUCKET_ROWS, LANES), F32)],
        scratch_shapes=[pltpu.VMEM((TM, D_GATE), BF16), pltpu.VMEM((BUCKET_ROWS, LANES), F32)],
        compiler_params=_cparams(),
        name="gmlp_layer",
    )(x_prompt, x_dec, nm, w_in, b_in, ln_g, ln_b, ws_p, ws_d, bs_p, bs_d, w_out, b_out,
      nf, *_router_split(wr), br)


def _router_body(hp_ref, hd_ref, nf_ref, wrh_ref, wr2_ref, br_ref, rows_ref, info_ref, cnt_ref, carry_ref):
    _route_init(carry_ref)
    h = jnp.where(pl.program_id(0) == N_TILES - 1, hd_ref[...], hp_ref[...])
    _store_rows(rows_ref, h, TM)
    _route_tile(h, nf_ref, wrh_ref, wr2_ref, br_ref, info_ref, cnt_ref, carry_ref)


def _router_split(wr):
    hi = wr.astype(BF16)
    return hi, jnp.concatenate([hi, (wr - hi.astype(F32)).astype(BF16)], axis=1)


def _route_init(carry_ref):
    @pl.when(pl.program_id(0) == 0)
    def _():
        carry_ref[...] = jnp.zeros_like(carry_ref)


def _route_tile(h, nf_ref, wrh_ref, wr2_ref, br_ref, info_ref, cnt_ref, carry_ref):
    xn = _rms(h, nf_ref[...])
    xh = xn.astype(BF16)
    xl = (xn - xh.astype(F32)).astype(BF16)
    both = _dot(xh, wr2_ref[...])
    logits = both[:, :LANES] + (_dot(xl, wrh_ref[...]) + both[:, LANES:]) + br_ref[...]
    lt = logits.T

    def first_max(vals, ids):
        vmax = jnp.max(vals, axis=0, keepdims=True)
        return vmax, jnp.min(jnp.where(vals == vmax, ids, float(LANES)), axis=0, keepdims=True)

    lg = lt[N_EXPERTS:N_EXPERTS + N_EGROUPS]
    gmax, g_idx = first_max(lg, lax.broadcasted_iota(jnp.int32, lg.shape, 0).astype(F32))
    g_p = 1.0 / jnp.sum(jnp.exp(lg - gmax), axis=0, keepdims=True)
    e_id = lax.broadcasted_iota(jnp.int32, (N_EXPERTS, TM), 0)
    in_group = (e_id // EXPERTS_PER_GROUP).astype(F32) == g_idx
    e_id = e_id.astype(F32)
    le = jnp.where(in_group, lt[:N_EXPERTS], -jnp.inf)
    v1, i1 = first_max(le, e_id)
    v2, i2 = first_max(jnp.where(e_id == i1, -jnp.inf, le), e_id)
    e2 = jnp.exp(v2 - v1)
    w1 = (1.0 / (1.0 + e2)) * g_p
    w2 = (e2 / (1.0 + e2)) * g_p
    a1 = i1 - EXPERTS_PER_GROUP * g_idx
    a2 = i2 - EXPERTS_PER_GROUP * g_idx
    lo = jnp.minimum(a1, a2)
    hi = jnp.maximum(a1, a2)
    pair = jnp.where(lo == 0.0, hi - 1.0, jnp.where(lo == 1.0, jnp.where(hi == 3.0, 3.0, 4.0), 5.0))
    ea = jnp.where(pair < 3.0, 0.0, jnp.where(pair < 5.0, 1.0, 3.0))
    ga = jnp.where(a1 == ea, w1, w2)
    gb = jnp.where(a1 == ea, w2, w1)
    bucket = g_idx * N_PAIRS + pair

    onehot = (lax.broadcasted_iota(jnp.int32, (BUCKET_ROWS, TM), 0).astype(F32) == bucket).astype(F32)
    r = lax.broadcasted_iota(jnp.int32, (TM, TM), 0)
    c = lax.broadcasted_iota(jnp.int32, (TM, TM), 1)
    before = _dot(onehot.astype(BF16), (r < c).astype(BF16))
    carry = carry_ref[:, 0:1]
    rank = jnp.sum(onehot * (before + carry), axis=0, keepdims=True)
    new_carry = carry + jnp.sum(onehot, axis=1, keepdims=True)
    carry_ref[...] = jnp.broadcast_to(new_carry, carry_ref.shape)
    cnt_ref[...] = jnp.broadcast_to(new_carry, cnt_ref.shape)
    info_ref[...] = jnp.concatenate([bucket, rank, ga, gb, jnp.zeros((SUBLANES - 4, TM), F32)], axis=0)


def _router(h_prompt, h_dec, nf, wr, br):
    const = lambda *shape: pl.BlockSpec(shape, lambda i: (0,) * len(shape))
    return pl.pallas_call(
        _router_body,
        grid=(N_TILES,),
        in_specs=[pl.BlockSpec((TM, D_MODEL), lambda i: (jnp.minimum(i, N_TILES - 2), 0)), const(N_DEC, D_MODEL),
                  const(1, D_MODEL), const(D_MODEL, LANES), const(D_MODEL, 2 * LANES), const(1, LANES)],
        out_specs=[pl.BlockSpec((TM * ROW_TILES, LANES), lambda i: (i, 0)),
                   pl.BlockSpec((SUBLANES, TM), lambda i: (0, i)),
                   const(BUCKET_ROWS, LANES)],
        out_shape=[jax.ShapeDtypeStruct((T * ROW_TILES, LANES), F32), jax.ShapeDtypeStruct((SUBLANES, T), F32),
                   jax.ShapeDtypeStruct((BUCKET_ROWS, LANES), F32)],
        scratch_shapes=[pltpu.VMEM((BUCKET_ROWS, LANES), F32)],
        compiler_params=_cparams(),
        name="moe_router",
    )(h_prompt, h_dec, nf, *_router_split(wr), br)


GATHER_UNROLL = 32


def _gather_rows_start(idx_ref, base, src_ref, dst_ref, sem, n):
    def group(g, carry):
        for u in range(GATHER_UNROLL):
            r = g * GATHER_UNROLL + u
            src = pl.multiple_of(idx_ref[base + r] * ROW_TILES, ROW_TILES)
            dst = pl.multiple_of(r * ROW_TILES, ROW_TILES)
            pltpu.make_async_copy(src_ref.at[pl.ds(src, ROW_TILES), :], dst_ref.at[pl.ds(dst, ROW_TILES), :],
                                  sem).start(priority=u % 2)
        return carry

    lax.fori_loop(0, n // GATHER_UNROLL, group, 0)


def _gather_rows_wait(src_ref, dst_ref, sem, n):
    pltpu.make_async_copy(src_ref.at[pl.ds(0, n * ROW_TILES), :], dst_ref.at[pl.ds(0, n * ROW_TILES), :], sem).wait()


def _gather_tile(idx_ref, src_ref, buf, sem, n, n_live=None, base_of=None, side_copy=None):
    i = pl.program_id(0)
    slot = lax.rem(i, 2)
    n_live = pl.num_programs(0) if n_live is None else n_live
    base_of = (lambda step: step * n) if base_of is None else base_of

    def start(step, s):
        _gather_rows_start(idx_ref, base_of(step), src_ref, buf.at[s], sem.at[s], n)
        if side_copy is not None:
            side_copy(step, s).start()

    @pl.when(i == 0)
    def _():
        start(0, 0)

    @pl.when(i + 1 < n_live)
    def _():
        start(i + 1, 1 - slot)

    @pl.when(i < n_live)
    def _():
        _gather_rows_wait(src_ref, buf.at[slot], sem.at[slot], n)
        if side_copy is not None:
            side_copy(i, slot).wait()

    return buf.at[slot]


def _gather_scratch(n):
    return [pltpu.VMEM((2, n * ROW_TILES, LANES), F32), pltpu.SemaphoreType.DMA((2,))]


GATE_WIN = TM + SUBLANES


def _moe_body(tok_ref, ea_ref, eb_ref, cha_ref, chb_ref, nlive_ref, first_ref, nrows_ref, rows_hbm, gates_hbm, nf_ref,
              w1a_ref, w3a_ref, w2a_ref, w1b_ref, w3b_ref, w2b_ref, out_ref,
              s1a, s3a, s2a, s1b, s3b, s2b, xbuf, xsem, gwin):
    i = pl.program_id(0)
    live = i < nlive_ref[0]

    def gate_copy(step, s):
        lo = pl.multiple_of((first_ref[step] // SUBLANES) * SUBLANES, SUBLANES)
        return pltpu.make_async_copy(gates_hbm.at[pl.ds(lo, GATE_WIN), :], gwin.at[s], xsem.at[s])

    x_ref = _gather_tile(tok_ref, rows_hbm, xbuf, xsem, TM, nlive_ref[0],
                         base_of=lambda step: first_ref[step], side_copy=gate_copy)

    @pl.when(cha_ref[i] == 1)
    def _():
        s1a[...] = w1a_ref[...].astype(BF16)
        s3a[...] = w3a_ref[...].astype(BF16)
        s2a[...] = w2a_ref[...].astype(BF16)

    @pl.when(chb_ref[i] == 1)
    def _():
        s1b[...] = w1b_ref[...].astype(BF16)
        s3b[...] = w3b_ref[...].astype(BF16)
        s2b[...] = w2b_ref[...].astype(BF16)

    @pl.when(live)
    def _():
        h = _load_rows(x_ref, TM)
        xn = _rms(h, nf_ref[...]).astype(BF16)
        g = gwin[lax.rem(i, 2), pl.ds(lax.rem(first_ref[i], SUBLANES), TM), :]
        row = lax.broadcasted_iota(jnp.int32, (TM, 1), 0)
        g = jnp.where(row < nrows_ref[i], g, 0.0)

        def ffn(w1, w3, w2):
            a = _dot(xn, w1[...])
            hdn = (a * (1.0 / (1.0 + jnp.exp(-a)))) * _dot(xn, w3[...])
            return _dot(hdn.astype(BF16), w2[...])

        y = g[:, 0:1] * ffn(s1a, s3a, s2a) + g[:, 1:2] * ffn(s1b, s3b, s2b)
        _store_rows(out_ref, h + y, TM)

    @pl.when(jnp.logical_not(live))
    def _():
        out_ref[...] = jnp.zeros_like(out_ref)


def _moe_ffn(layer, tok, ea, eb, cha, chb, n_live, first, nrows, rows, gates, nf, w1, w3, w2):
    wa = lambda shape: pl.BlockSpec((None, None) + shape, lambda i, tk, ea, eb, *_: (layer, ea[i], 0, 0))
    wb = lambda shape: pl.BlockSpec((None, None) + shape, lambda i, tk, ea, eb, *_: (layer, eb[i], 0, 0))
    up, down = (D_MODEL, D_EXPERT), (D_EXPERT, D_MODEL)
    return pl.pallas_call(
        _moe_body,
        grid_spec=pltpu.PrefetchScalarGridSpec(
            num_scalar_prefetch=8,
            grid=(MOE_TILES,),
            in_specs=[
                pl.BlockSpec(memory_space=pl.ANY),
                pl.BlockSpec(memory_space=pl.ANY),
                pl.BlockSpec((1, D_MODEL), lambda i, *_: (0, 0)),
                wa(up), wa(up), wa(down), wb(up), wb(up), wb(down),
            ],
            out_specs=pl.BlockSpec((TM * ROW_TILES, LANES), lambda i, *_: (i, 0)),
            scratch_shapes=[pltpu.VMEM(up, BF16), pltpu.VMEM(up, BF16), pltpu.VMEM(down, BF16),
                            pltpu.VMEM(up, BF16), pltpu.VMEM(up, BF16), pltpu.VMEM(down, BF16)]
            + _gather_scratch(TM) + [pltpu.VMEM((2, GATE_WIN, 2), F32)],
        ),
        out_shape=jax.ShapeDtypeStruct((P_ROWS * ROW_TILES, LANES), F32),
        compiler_params=_cparams(),
        name="moe_ffn",
    )(tok, ea, eb, cha, chb, n_live, first, nrows, rows, gates, nf, w1, w3, w2, w1, w3, w2)


def _router_weights(layer, w_group, b_group, w_expert, b_expert):
    pad = LANES - N_EXPERTS - N_EGROUPS
    wr = jnp.concatenate([w_expert[layer], w_group[layer], jnp.zeros((D_MODEL, pad), F32)], axis=1)
    br = jnp.concatenate([b_expert[layer], b_group[layer], jnp.zeros((pad,), F32)]).reshape(1, LANES)
    return wr, br


def _moe_layer(layer, rows, info, cnt, nf, w1, w3, w2):
    bucket = info[0].astype(jnp.int32)
    rank = info[1].astype(jnp.int32)
    counts = cnt[:N_BUCKETS, 0].astype(jnp.int32)
    n_tiles = (counts + TM - 1) // TM
    tile_end = jnp.cumsum(n_tiles)
    tile_start = tile_end - n_tiles
    start_of = jnp.sum(jnp.where(bucket[:, None] == jnp.arange(N_BUCKETS)[None, :], tile_start[None, :], 0), axis=1)
    pos = start_of * TM + rank
    _, tok, ga, gb = lax.sort((pos, jnp.arange(T, dtype=jnp.int32), info[2], info[3]), num_keys=1)
    tok = jnp.concatenate([tok, jnp.zeros((TM,), jnp.int32)])
    gates = jnp.concatenate([jnp.stack([ga, gb], axis=1), jnp.zeros((TM + SUBLANES, 2), F32)])
    total = tile_end[-1]
    j = jnp.minimum(jnp.arange(MOE_TILES), total - 1)
    tb = jnp.sum((j[:, None] >= tile_end[None, :]).astype(jnp.int32), axis=1)
    tokens_before = jnp.cumsum(counts) - counts
    in_bucket = (j - tile_start[tb]) * TM
    first = (tokens_before[tb] + in_bucket).astype(jnp.int32)
    nrows = jnp.clip(counts[tb] - in_bucket, 0, TM).astype(jnp.int32)
    grp, pair = tb // N_PAIRS, tb % N_PAIRS
    ea = (grp * EXPERTS_PER_GROUP + jnp.asarray(PAIR_A, jnp.int32)[pair]).astype(jnp.int32)
    eb = (grp * EXPERTS_PER_GROUP + jnp.asarray(PAIR_B, jnp.int32)[pair]).astype(jnp.int32)
    tile0 = jnp.arange(MOE_TILES) == 0
    cha = (tile0 | (ea != jnp.roll(ea, 1))).astype(jnp.int32)
    chb = (tile0 | (eb != jnp.roll(eb, 1))).astype(jnp.int32)
    n_live = total.reshape(1).astype(jnp.int32)
    return _moe_ffn(layer, tok, ea, eb, cha, chb, n_live, first, nrows, rows, gates, nf, w1, w3, w2), pos


VT_ROWS = LANES + 16


def _pair_seg_matrix():
    grp = np.arange(2 * HEAD_W) // HEAD_W
    return jnp.asarray(grp[:, None] == grp[None, :], BF16)


def _rope_swap(x):
    lane = lax.broadcasted_iota(jnp.int32, x.shape, 1)
    return jnp.where(lane < ROPE_LO + ROPE_HALF, pltpu.roll(x, LANES - ROPE_HALF, 1), pltpu.roll(x, ROPE_HALF, 1))


def _expand_k(cb, kr, wuk_ref, seg, kg_ref, k_ref):
    kn = _dot(cb, wuk_ref[...])
    pairs = [kn[:, pr * 2 * HEAD_W:(pr + 1) * 2 * HEAD_W] for pr in range(N_HEADS // 2)]
    ms = [_split_dot(x * x, seg) * (1.0 / NOPE_DIM) for x in pairs]
    kg2 = jnp.concatenate([kg_ref[...]] * 2, axis=1)
    kr2 = jnp.concatenate([kr, kr], axis=1)
    for pr in range(N_HEADS // 2):
        k_ref[:, pr * 2 * HEAD_W:(pr + 1) * 2 * HEAD_W] = (pairs[pr] * lax.rsqrt(ms[pr] + EPS) * kg2 + kr2).astype(BF16)


def _mla_proj_body(pos_ref, sorted_hbm, cos_ref, sin_ref, cost_ref, sint_ref, kvn_ref, wdkv_ref, kvan_ref, krg_ref,
                   wuk_ref, wuv_ref, wuvt_ref, kg_ref, nmq_ref, wdq_ref, qan_ref, wuqt_ref, qg_ref, seg_ref,
                   rows_ref, ckvp_ref, ckvd_ref, krp_ref, krd_ref, k_ref, vdec_ref, vt_ref, qt_ref, qdec_ref,
                   xbuf, xsem):
    is_dec = pl.program_id(0) == N_TILES - 1
    x_ref = _gather_tile(pos_ref, sorted_hbm, xbuf, xsem, TM)
    rows_ref[...] = x_ref[...]
    h = _load_rows(x_ref, TM)
    hn = h * lax.rsqrt(jnp.mean(h * h, axis=-1, keepdims=True) + EPS)
    c = _dot((hn * kvn_ref[...]).astype(BF16), wdkv_ref[...])
    ckv = _rms(c[:, :KV_RANK], kvan_ref[...])
    kr = c[:, KV_RANK:]
    kr = kr * lax.rsqrt(jnp.sum(kr * kr, axis=-1, keepdims=True) * (1.0 / ROPE_DIM) + EPS) * krg_ref[...]
    kr = kr * cos_ref[...] + _rope_swap(kr) * sin_ref[...]
    cb = ckv.astype(BF16)
    _expand_k(cb, kr, wuk_ref, seg_ref[...], kg_ref, k_ref)
    vt = _dot(wuvt_ref[...], ckv.T.astype(BF16)).astype(BF16)
    for pr in range(N_HEADS // 2):
        vt_ref[0, pr * VT_ROWS:pr * VT_ROWS + LANES, :] = vt[pr * LANES:(pr + 1) * LANES]
        vt_ref[0, pr * VT_ROWS + LANES:(pr + 1) * VT_ROWS, :] = jnp.ones((VT_ROWS - LANES, TM), BF16)
    cq = _rms(_dot((hn * nmq_ref[...]).astype(BF16), wdq_ref[...]), qan_ref[...])
    qt = _dot(wuqt_ref[...], cq.T.astype(BF16))
    cost, sint, qg = cost_ref[0], sint_ref[0], qg_ref[...]
    for hh in range(N_HEADS):
        x = qt[hh * HEAD_W:(hh + 1) * HEAD_W, :]
        xn, xr = x[:NOPE_DIM], x[ROPE_LO:ROPE_LO + ROPE_DIM]
        xn = xn * lax.rsqrt(jnp.mean(xn * xn, axis=0, keepdims=True) + EPS) * qg[:NOPE_DIM]
        xr = xr * lax.rsqrt(jnp.mean(xr * xr, axis=0, keepdims=True) + EPS) * qg[ROPE_LO:ROPE_LO + ROPE_DIM]
        x1, x2 = xr[:ROPE_HALF], xr[ROPE_HALF:]
        qh = jnp.concatenate([xn, x1 * cost - x2 * sint, x1 * sint + x2 * cost,
                              jnp.zeros((HEAD_W - NOPE_DIM - ROPE_DIM, TM), F32)], axis=0) * Q_SCALE
        qt_ref[0, hh * HEAD_W:(hh + 1) * HEAD_W, :] = qh.astype(BF16)

    @pl.when(jnp.logical_not(is_dec))
    def _():
        ckvp_ref[...] = ckv
        krp_ref[...] = kr[:, ROPE_LO:ROPE_LO + ROPE_DIM]

    @pl.when(is_dec)
    def _():
        ckvd_ref[...] = ckv
        krd_ref[...] = kr[:, ROPE_LO:ROPE_LO + ROPE_DIM]
        vdec_ref[...] = _dot(cb, wuv_ref[...]).astype(BF16)
        for hh in range(N_HEADS):
            rows = slice(hh * HEAD_W, (hh + 1) * HEAD_W)
            qdec_ref[:, rows] = qt_ref[0, rows, :].astype(F32).T.astype(BF16)


def _mla_proj(pos, sorted_rows, cos_t, sin_t, cos_tt, sin_tt, kvn, wdkv, kvan, krg, wuk, wuv, kg, nmq, wdq, qan, wuq, qg):
    const = lambda *shape: pl.BlockSpec(shape, lambda i, p: (0,) * len(shape))
    tab_tile = lambda i: jnp.where(i < N_PROMPT // TM, i % (SEQ // TM), SEQ // TM)
    tab = pl.BlockSpec((TM, LANES), lambda i, p: (tab_tile(i), 0))
    tab_t = pl.BlockSpec((1, ROPE_HALF, TM), lambda i, p: (tab_tile(i), 0, 0))
    row = lambda w: pl.BlockSpec((TM, w), lambda i, p: (i, 0))
    prow = lambda w: pl.BlockSpec((TM, w), lambda i, p: (jnp.minimum(i, N_TILES - 2), 0))
    return pl.pallas_call(
        _mla_proj_body,
        grid_spec=pltpu.PrefetchScalarGridSpec(
            num_scalar_prefetch=1,
            grid=(N_TILES,),
            in_specs=[
                pl.BlockSpec(memory_space=pl.ANY), tab, tab, tab_t, tab_t,
                const(1, D_MODEL), const(D_MODEL, KV_RANK + LANES), const(1, KV_RANK), const(1, LANES),
                const(KV_RANK, N_HEADS * HEAD_W), const(KV_RANK, N_HEADS * V_DIM), const(N_HEADS * V_DIM, KV_RANK),
                const(1, LANES),
                const(1, D_MODEL), const(D_MODEL, Q_RANK), const(1, Q_RANK), const(N_HEADS * HEAD_W, Q_RANK),
                const(HEAD_W, TM), const(2 * HEAD_W, 2 * HEAD_W),
            ],
            out_specs=[pl.BlockSpec((TM * ROW_TILES, LANES), lambda i, p: (i, 0)),
                       prow(KV_RANK), const(N_DEC, KV_RANK), prow(ROPE_DIM), const(N_DEC, ROPE_DIM),
                       row(N_HEADS * HEAD_W), const(N_DEC, N_HEADS * V_DIM),
                       pl.BlockSpec((1, N_HEADS // 2 * VT_ROWS, TM), lambda i, p: (i, 0, 0)),
                       pl.BlockSpec((1, N_HEADS * HEAD_W, TM), lambda i, p: (i, 0, 0)),
                       const(N_DEC, N_HEADS * HEAD_W)],
            scratch_shapes=_gather_scratch(TM),
        ),
        out_shape=[
            jax.ShapeDtypeStruct((T * ROW_TILES, LANES), F32),
            jax.ShapeDtypeStruct((N_PROMPT, KV_RANK), F32), jax.ShapeDtypeStruct((N_DEC, KV_RANK), F32),
            jax.ShapeDtypeStruct((N_PROMPT, ROPE_DIM), F32), jax.ShapeDtypeStruct((N_DEC, ROPE_DIM), F32),
            jax.ShapeDtypeStruct((T, N_HEADS * HEAD_W), BF16), jax.ShapeDtypeStruct((N_DEC, N_HEADS * V_DIM), BF16),
            jax.ShapeDtypeStruct((N_TILES, N_HEADS // 2 * VT_ROWS, TM), BF16),
            jax.ShapeDtypeStruct((N_TILES, N_HEADS * HEAD_W, TM), BF16),
            jax.ShapeDtypeStruct((N_DEC, N_HEADS * HEAD_W), BF16),
        ],
        compiler_params=_cparams(),
        name="mla_proj",
    )(pos, sorted_rows, cos_t, sin_t, cos_tt, sin_tt, kvn, wdkv, kvan, krg, wuk, wuv, wuv.T, kg, nmq, wdq, qan,
      wuq.T, jnp.broadcast_to(qg.reshape(HEAD_W, 1), (HEAD_W, TM)), _pair_seg_matrix())


def _cache_kv_body(ckv_ref, kr_ref, place_ref, wuk_ref, wuv_ref, kg_ref, seg_ref, k_ref, v_ref):
    kr = _dot(kr_ref[...].astype(BF16), place_ref[...])
    cb = ckv_ref[...].astype(BF16)
    _expand_k(cb, kr, wuk_ref, seg_ref[...], kg_ref, k_ref)
    v_ref[...] = _dot(cb, wuv_ref[...]).astype(BF16)


CACHE_ROWS = 1024
assert PAST_LEN % CACHE_ROWS == 0


def _cache_kv(ckv, kr, wuk, wuv, kg):
    n = ckv.shape[0]
    place = jnp.asarray(np.arange(ROPE_DIM)[:, None] + ROPE_LO == np.arange(LANES)[None, :], BF16)
    const = lambda *shape: pl.BlockSpec(shape, lambda i: (0,) * len(shape))
    row = lambda w: pl.BlockSpec((CACHE_ROWS, w), lambda i: (i, 0))
    return pl.pallas_call(
        _cache_kv_body,
        grid=(n // CACHE_ROWS,),
        in_specs=[row(KV_RANK), row(ROPE_DIM), const(ROPE_DIM, LANES), const(KV_RANK, N_HEADS * HEAD_W),
                  const(KV_RANK, N_HEADS * V_DIM), const(1, LANES), const(2 * HEAD_W, 2 * HEAD_W)],
        out_specs=[row(N_HEADS * HEAD_W), row(N_HEADS * V_DIM)],
        out_shape=[jax.ShapeDtypeStruct((n, N_HEADS * HEAD_W), BF16), jax.ShapeDtypeStruct((n, N_HEADS * V_DIM), BF16)],
        compiler_params=_cparams(),
        name="cache_kv",
    )(ckv, kr, place, wuk, wuv, kg, _pair_seg_matrix())


TQ = 256
TK = 256
assert TQ == TK and TQ % CHUNK == 0
SCORE_LOOKAHEAD = 8


def _qk(q, k):
    return lax.dot_general(q, k, (((1,), (1,)), ((), ())), preferred_element_type=F32)


def _merge_heads(o_ref, outs, rows):
    lane = lax.broadcasted_iota(jnp.int32, (rows, LANES), 1)
    for pr in range(N_HEADS // 2):
        o_ref[:, pr * LANES:(pr + 1) * LANES] = jnp.where(lane < V_DIM, outs[2 * pr], outs[2 * pr + 1]).astype(BF16)


def _prompt_attn_body(qt_ref, k_ref, vt_ref, rows_ref, wo_ref, out_ref, m_scr, acc_scr):
    qi = pl.program_id(1)
    m_scr[...] = jnp.full(m_scr.shape, NEG, F32)
    acc_scr[...] = jnp.zeros(acc_scr.shape, F32)

    def all_heads(j, mask):
        ks = pl.ds(pl.multiple_of(j * TK, TK), TK)

        def scores(hh):
            hcols = slice(hh * HEAD_W, (hh + 1) * HEAD_W)
            return _dot(k_ref[ks, hcols], qt_ref[0, hcols, :])

        ahead = [scores(hh) for hh in range(SCORE_LOOKAHEAD)]
        for hh in range(N_HEADS):
            vrows = slice((hh // 2) * VT_ROWS, (hh // 2 + 1) * VT_ROWS)
            s = ahead.pop(0)
            if hh + SCORE_LOOKAHEAD < N_HEADS:
                ahead.append(scores(hh + SCORE_LOOKAHEAD))
            if mask is not None:
                s = jnp.where(mask, s, NEG)
            m_old = m_scr[hh]
            m_new = jnp.maximum(m_old, jnp.max(s, axis=0, keepdims=True))
            p = jnp.exp2(s - m_new).astype(BF16)
            m_scr[hh] = m_new
            acc_scr[hh] = jnp.exp2(m_old - m_new) * acc_scr[hh] + _dot(vt_ref[j, vrows, :], p)

    def step(j, carry):
        all_heads(j, None)
        return carry

    lax.fori_loop(0, qi, step, 0)
    kc = lax.broadcasted_iota(jnp.int32, (TK, TQ), 0) // CHUNK
    qc = lax.broadcasted_iota(jnp.int32, (TK, TQ), 1) // CHUNK
    all_heads(qi, kc <= qc)
    row = lax.broadcasted_iota(jnp.int32, (LANES, TQ), 0)
    pairs = []
    for pr in range(N_HEADS // 2):
        even = acc_scr[2 * pr, :LANES] / acc_scr[2 * pr, LANES:LANES + 1]
        odd = acc_scr[2 * pr + 1, :LANES] / acc_scr[2 * pr + 1, LANES:LANES + 1]
        pairs.append(jnp.where(row < V_DIM, even, odd))
    o = jnp.concatenate(pairs, axis=0).T.astype(BF16)
    out_ref[...] = _load_rows(rows_ref, TQ) + _dot(o, wo_ref[...])


def _prompt_attn(qt, k, vt, rows, wo):
    nq = SEQ // TQ
    return pl.pallas_call(
        _prompt_attn_body,
        grid=(BATCH, nq),
        in_specs=[
            pl.BlockSpec((1, N_HEADS * HEAD_W, TQ), lambda b, i: (b * nq + i, 0, 0)),
            pl.BlockSpec((SEQ, N_HEADS * HEAD_W), lambda b, i: (b, 0)),
            pl.BlockSpec((SEQ // TK, N_HEADS // 2 * VT_ROWS, TK), lambda b, i: (b, 0, 0)),
            pl.BlockSpec((TQ * ROW_TILES, LANES), lambda b, i: (b * nq + i, 0)),
            pl.BlockSpec((N_HEADS * V_DIM, D_MODEL), lambda b, i: (0, 0)),
        ],
        out_specs=pl.BlockSpec((TQ, D_MODEL), lambda b, i: (b * nq + i, 0)),
        out_shape=jax.ShapeDtypeStruct((N_PROMPT, D_MODEL), F32),
        scratch_shapes=[pltpu.VMEM((N_HEADS, 1, TQ), F32), pltpu.VMEM((N_HEADS, VT_ROWS, TQ), F32)],
        compiler_params=_cparams(2),
        name="prompt_attn",
    )(qt, k, vt, rows, wo)


def _sample_attn_body(q_ref, kc_ref, vc_ref, kn_ref, vn_ref, rows_ref, wo_ref, out_ref, o_scr):
    outs = []
    for hh in range(N_HEADS):
        hcols = slice(hh * HEAD_W, (hh + 1) * HEAD_W)
        vcols = slice((hh // 2) * LANES, (hh // 2 + 1) * LANES)
        q = q_ref[:, hcols]
        sc = _qk(q, kc_ref[:, hcols])
        sn = _qk(q, kn_ref[:, hcols])
        m = jnp.maximum(jnp.max(sc, axis=-1, keepdims=True), jnp.max(sn, axis=-1, keepdims=True))
        pc = jnp.exp2(sc - m)
        pn = jnp.exp2(sn - m)
        l = jnp.sum(pc, axis=-1, keepdims=True) + jnp.sum(pn, axis=-1, keepdims=True)
        acc = _dot(pc.astype(BF16), vc_ref[:, vcols]) + _dot(pn.astype(BF16), vn_ref[:, vcols])
        outs.append(acc / l)
    _merge_heads(o_scr, outs, DEC_SEQ)
    out_ref[...] = _load_rows(rows_ref, DEC_SEQ) + _dot(o_scr[...], wo_ref[...])


def _sample_attn(q, kc, vc, kn, vn, rows, wo):
    off = N_PROMPT // DEC_SEQ
    return pl.pallas_call(
        _sample_attn_body,
        grid=(DEC_BATCH,),
        in_specs=[
            pl.BlockSpec((DEC_SEQ, N_HEADS * HEAD_W), lambda b: (b, 0)),
            pl.BlockSpec((PAST_LEN, N_HEADS * HEAD_W), lambda b: (b, 0)),
            pl.BlockSpec((PAST_LEN, N_HEADS * V_DIM), lambda b: (b, 0)),
            pl.BlockSpec((DEC_SEQ, N_HEADS * HEAD_W), lambda b: (off + b, 0)),
            pl.BlockSpec((DEC_SEQ, N_HEADS * V_DIM), lambda b: (b, 0)),
            pl.BlockSpec((DEC_SEQ * ROW_TILES, LANES), lambda b: (off + b, 0)),
            pl.BlockSpec((N_HEADS * V_DIM, D_MODEL), lambda b: (0, 0)),
        ],
        out_specs=pl.BlockSpec((DEC_SEQ, D_MODEL), lambda b: (b, 0)),
        out_shape=jax.ShapeDtypeStruct((N_DEC, D_MODEL), F32),
        scratch_shapes=[pltpu.VMEM((DEC_SEQ, N_HEADS * V_DIM), BF16)],
        compiler_params=_cparams(),
        name="sample_attn",
    )(q, kc, vc, kn, vn, rows, wo)


def _finish_body(pos_ref, sorted_hbm, yp_ref, ys_ref, xbuf, xsem):
    i = pl.program_id(0)
    y = _load_rows(_gather_tile(pos_ref, sorted_hbm, xbuf, xsem, TM), TM)

    @pl.when(i < N_TILES - 1)
    def _():
        yp_ref[...] = y

    @pl.when(i == N_TILES - 1)
    def _():
        ys_ref[...] = y


def _finish(pos, sorted_rows):
    return pl.pallas_call(
        _finish_body,
        grid_spec=pltpu.PrefetchScalarGridSpec(
            num_scalar_prefetch=1,
            grid=(N_TILES,),
            in_specs=[pl.BlockSpec(memory_space=pl.ANY)],
            out_specs=[pl.BlockSpec((TM, D_MODEL), lambda i, p: (jnp.minimum(i, N_TILES - 2), 0)),
                       pl.BlockSpec((N_DEC, D_MODEL), lambda i, p: (0, 0))],
            scratch_shapes=_gather_scratch(TM),
        ),
        out_shape=[jax.ShapeDtypeStruct((N_PROMPT, D_MODEL), F32), jax.ShapeDtypeStruct((N_DEC, D_MODEL), F32)],
        compiler_params=_cparams(),
        name="finish",
    )(pos, sorted_rows)


def _rope_tables():
    half = ROPE_DIM // 2
    inv_freq = ROPE_THETA ** (-jnp.arange(half, dtype=F32) / half)
    dec_pos = PAST_LEN + jnp.tile(jnp.arange(DEC_SEQ, dtype=jnp.int32), DEC_BATCH)
    pos = jnp.concatenate([jnp.arange(SEQ, dtype=jnp.int32), dec_pos])
    ang = pos.astype(F32)[:, None] * inv_freq[None, :]
    cos, sin = jnp.cos(ang), jnp.sin(ang)
    n = pos.shape[0]
    cos_t = jnp.ones((n, LANES), F32).at[:, ROPE_LO:ROPE_LO + ROPE_DIM].set(jnp.concatenate([cos, cos], axis=1))
    sin_t = jnp.zeros((n, LANES), F32).at[:, ROPE_LO:ROPE_LO + ROPE_DIM].set(jnp.concatenate([-sin, sin], axis=1))
    to_tiles = lambda a: a.reshape(n // TM, TM, half).transpose(0, 2, 1)
    return cos_t, sin_t, to_tiles(cos), to_tiles(sin)


def _on_lanes(vec, lo):
    return jnp.zeros((1, LANES), F32).at[0, lo:lo + vec.shape[0]].set(vec)


def kernel(x_prompt, x_sample, cache_ckv, cache_krope, norm_mix, norm_ffn, gm_w_in, gm_b_in, gm_ln_g, gm_ln_b, gm_w_s, gm_b_s, gm_w_out, gm_b_out, kv_norm, w_dkv, kv_a_norm, k_rope_norm, w_uk, w_uv, k_nope_norm, w_dq, q_a_norm, w_uq, q_nope_norm, q_rope_norm, w_o, moe_w_group, moe_b_group, moe_w_expert, moe_b_expert, moe_w1, moe_w3, moe_w2):
    nf0, nf1 = norm_ffn[0].reshape(1, D_MODEL), norm_ffn[1].reshape(1, D_MODEL)

    idx = np.arange(GMLP_BLOCK)
    allowed = (idx[None, :] // CHUNK) <= (idx[:, None] // CHUNK)
    ws_p = jnp.where(allowed[None], gm_w_s[0], 0.0).astype(BF16)
    same_seq = (idx[None, :] // DEC_SEQ) == (idx[:, None] // DEC_SEQ)
    ws_d = jnp.where(same_seq[None], jnp.tile(gm_w_s[0][:, :DEC_SEQ, :DEC_SEQ], (1, GMLP_BLOCK // DEC_SEQ, GMLP_BLOCK // DEC_SEQ)), 0.0).astype(BF16)
    bs_p = gm_b_s[0][:, :, None]
    bs_d = jnp.tile(gm_b_s[0][:, :DEC_SEQ], (1, GMLP_BLOCK // DEC_SEQ))[:, :, None]
    rows, v_rows, info, cnt = _gmlp_layer(
        x_prompt.reshape(N_PROMPT, D_MODEL), x_sample.reshape(N_DEC, D_MODEL),
        norm_mix[0].reshape(1, -1), gm_w_in[0].astype(BF16), gm_b_in[0].reshape(1, -1),
        gm_ln_g[0].reshape(1, -1), gm_ln_b[0].reshape(1, -1), ws_p, ws_d, bs_p, bs_d,
        gm_w_out[0].astype(BF16), gm_b_out[0].reshape(1, -1),
        nf0, *_router_weights(0, moe_w_group, moe_b_group, moe_w_expert, moe_b_expert))
    sorted_rows, pos = _moe_layer(0, rows, info, cnt, nf0, moe_w1, moe_w3, moe_w2)

    cos_t, sin_t, cos_tt, sin_tt = _rope_tables()
    wdkv = jnp.zeros((D_MODEL, KV_RANK + LANES), F32).at[:, :KV_RANK].set(w_dkv[:, :KV_RANK])
    wdkv = wdkv.at[:, KV_RANK + ROPE_LO:KV_RANK + ROPE_LO + ROPE_DIM].set(w_dkv[:, KV_RANK:]).astype(BF16)
    wuk = jnp.zeros((KV_RANK, N_HEADS, HEAD_W), F32).at[:, :, :NOPE_DIM].set(w_uk).reshape(KV_RANK, -1).astype(BF16)
    wuv = w_uv.reshape(KV_RANK, -1).astype(BF16)
    wuq = jnp.zeros((Q_RANK, N_HEADS, HEAD_W), F32).at[:, :, :NOPE_DIM + ROPE_DIM].set(w_uq[0]).reshape(Q_RANK, -1).astype(BF16)
    kg = _on_lanes(k_nope_norm, 0)
    krg = _on_lanes(k_rope_norm, ROPE_LO)
    qg = _on_lanes(jnp.concatenate([q_nope_norm[0], q_rope_norm[0]]), 0)
    rows, ckv_p, ckv_d, krope_p, krope_d, k_new, v_dec, vt_new, qt, q_dec = _mla_proj(
        pos, sorted_rows, cos_t, sin_t, cos_tt, sin_tt, kv_norm.reshape(1, -1), wdkv, kv_a_norm.reshape(1, -1), krg, wuk, wuv, kg,
        norm_mix[1].reshape(1, -1), w_dq[0].astype(BF16), q_a_norm[0].reshape(1, -1), wuq, qg)
    k_cache, v_cache = _cache_kv(cache_ckv.reshape(-1, KV_RANK), cache_krope.reshape(-1, ROPE_DIM), wuk, wuv, kg)

    wo = w_o[0].astype(BF16)
    h_prompt = _prompt_attn(qt, k_new, vt_new, rows, wo)
    h_dec = _sample_attn(q_dec, k_cache, v_cache, k_new, v_dec, rows, wo)
    rows, info, cnt = _router(h_prompt, h_dec, nf1,
                              *_router_weights(1, moe_w_group, moe_b_group, moe_w_expert, moe_b_expert))
    sorted_rows, pos = _moe_layer(1, rows, info, cnt, nf1, moe_w1, moe_w3, moe_w2)
    y_prompt, y_sample = _finish(pos, sorted_rows)

    return (y_prompt.reshape(BATCH, SEQ, D_MODEL), y_sample.reshape(DEC_BATCH, DEC_SEQ, D_MODEL),
            ckv_p.reshape(BATCH, SEQ, KV_RANK), krope_p.reshape(BATCH, SEQ, ROPE_DIM),
            ckv_d.reshape(DEC_BATCH, DEC_SEQ, KV_RANK), krope_d.reshape(DEC_BATCH, DEC_SEQ, ROPE_DIM),
            v_rows.reshape(1, DEC_BATCH, DEC_SEQ, D_GATE))
```

```python
import functools

import jax
import jax.numpy as jnp
import numpy as np
from jax import lax
from jax.experimental import pallas as pl
from jax.experimental.pallas import tpu as pltpu

F32 = jnp.float32
BF16 = jnp.bfloat16

D_MODEL = 1024
BATCH = 8
SEQ = 2048
DEC_BATCH = 16
DEC_SEQ = 16
PAST_LEN = 2048
CHUNK = 64
GMLP_BLOCK = 128
D_GATE = 2 * D_MODEL
N_SG = 8
SG_W = D_GATE // N_SG
N_HEADS = 8
NOPE_DIM = 64
ROPE_DIM = 32
V_DIM = 64
Q_RANK = 384
KV_RANK = 256
ROPE_THETA = 10000.0
SCALE = (NOPE_DIM + ROPE_DIM) ** -0.5
Q_SCALE = SCALE * float(np.log2(np.e))
N_EGROUPS = 4
EXPERTS_PER_GROUP = 4
N_EXPERTS = N_EGROUPS * EXPERTS_PER_GROUP
D_EXPERT = 512
EPS = 1e-6
NEG = -1e30

LANES = 128
SUBLANES = 8
ROW_TILES = D_MODEL // LANES
assert ROW_TILES == SUBLANES

N_PROMPT = BATCH * SEQ
N_DEC = DEC_BATCH * DEC_SEQ
T = N_PROMPT + N_DEC
TM = 256
assert N_PROMPT % TM == 0 and N_DEC == TM
N_TILES = T // TM
HEAD_W = LANES
ROPE_LO = NOPE_DIM
ROPE_HALF = ROPE_DIM // 2

PAIR_A = (0, 0, 0, 1, 1, 3)
PAIR_B = (1, 2, 3, 3, 2, 2)
N_PAIRS = 6
N_BUCKETS = N_EGROUPS * N_PAIRS
BUCKET_ROWS = 32
assert N_BUCKETS <= BUCKET_ROWS and BUCKET_ROWS % SUBLANES == 0
MOE_TILES = (T + N_BUCKETS * (TM - 1) + TM - 1) // TM
P_ROWS = MOE_TILES * TM

VMEM_LIMIT = 56 * 1024 * 1024


def _cparams(n_axes=1, vmem=VMEM_LIMIT):
    return pltpu.CompilerParams(dimension_semantics=("arbitrary",) * n_axes, vmem_limit_bytes=vmem)


def _rms(x, g):
    return x * lax.rsqrt(jnp.mean(x * x, axis=-1, keepdims=True) + EPS) * g


def _load_rows(ref, n):
    return jnp.concatenate([ref[pl.ds(s, n, stride=ROW_TILES), :] for s in range(ROW_TILES)], axis=1)


def _store_rows(ref, x, n):
    for s in range(ROW_TILES):
        ref[pl.ds(s, n, stride=ROW_TILES), :] = x[:, s * LANES:(s + 1) * LANES]


def _dot(a, b):
    return jnp.dot(a, b, preferred_element_type=F32)


GELU_K1 = float(-2.0 * np.sqrt(2.0 / np.pi) * np.log2(np.e))
GELU_K3 = GELU_K1 * 0.044715


def _gmlp_body(xp_ref, xd_ref, nm_ref, win_ref, bin_ref, lng_ref, lnb_ref, wsp_ref, wsd_ref, bsp_ref, bsd_ref,
               wout_ref, bout_ref, nf_ref, wrh_ref, wr2_ref, br_ref,
               rows_ref, v_ref, info_ref, cnt_ref, gated_ref, carry_ref):
    i = pl.program_id(0)
    is_dec = i == N_TILES - 1
    _route_init(carry_ref)
    x = jnp.where(is_dec, xd_ref[...], xp_ref[...])
    xn = _rms(x, nm_ref[...]).astype(BF16)
    z = _dot(xn, win_ref[...]) + bin_ref[...]
    z = z / (1.0 + jnp.exp2(z * (GELU_K1 + GELU_K3 * (z * z))))
    u = z[:, :D_GATE]
    v = z[:, D_GATE:]
    mu = jnp.mean(v, axis=-1, keepdims=True)
    vc = v - mu
    var = jnp.mean(vc * vc, axis=-1, keepdims=True)
    v = vc * lax.rsqrt(var + EPS) * lng_ref[...] + lnb_ref[...]

    v_ref[...] = v
    vb = v.astype(BF16)
    for g in range(N_SG):
        ws = jnp.where(is_dec, wsd_ref[g], wsp_ref[g])
        bs = jnp.where(is_dec, bsd_ref[g], bsp_ref[g])
        for b in range(TM // GMLP_BLOCK):
            rows = slice(b * GMLP_BLOCK, (b + 1) * GMLP_BLOCK)
            cols = slice(g * SG_W, (g + 1) * SG_W)
            s = _dot(ws, vb[rows, cols]) + bs
            gated_ref[rows, cols] = (u[rows, cols] * s).astype(BF16)
    h = x + _dot(gated_ref[...], wout_ref[...]) + bout_ref[...]
    _store_rows(rows_ref, h, TM)
    _route_tile(h, nf_ref, wrh_ref, wr2_ref, br_ref, info_ref, cnt_ref, carry_ref)


def _gmlp_layer(x_prompt, x_dec, nm, w_in, b_in, ln_g, ln_b, ws_p, ws_d, bs_p, bs_d, w_out, b_out, nf, wr, br):
    const = lambda *shape: pl.BlockSpec(shape, lambda i: (0,) * len(shape))
    return pl.pallas_call(
        _gmlp_body,
        grid=(N_TILES,),
        in_specs=[
            pl.BlockSpec((TM, D_MODEL), lambda i: (jnp.minimum(i, N_TILES - 2), 0)), const(N_DEC, D_MODEL),
            const(1, D_MODEL), const(D_MODEL, 2 * D_GATE), const(1, 2 * D_GATE),
            const(1, D_GATE), const(1, D_GATE),
            const(N_SG, GMLP_BLOCK, GMLP_BLOCK), const(N_SG, GMLP_BLOCK, GMLP_BLOCK),
            const(N_SG, GMLP_BLOCK, 1), const(N_SG, GMLP_BLOCK, 1),
            const(D_GATE, D_MODEL), const(1, D_MODEL),
            const(1, D_MODEL), const(D_MODEL, LANES), const(D_MODEL, 2 * LANES), const(1, LANES),
        ],
        out_specs=[
            pl.BlockSpec((TM * ROW_TILES, LANES), lambda i: (i, 0)),
            const(N_DEC, D_GATE),
            pl.BlockSpec((SUBLANES, TM), lambda i: (0, i)),
            const(BUCKET_ROWS, LANES),
        ],
        out_shape=[jax.ShapeDtypeStruct((T * ROW_TILES, LANES), F32), jax.ShapeDtypeStruct((N_DEC, D_GATE), F32),
                   jax.ShapeDtypeStruct((SUBLANES, T), F32), jax.ShapeDtypeStruct((BUCKET_ROWS, LANES), F32)],
        scratch_shapes=[pltpu.VMEM((TM, D_GATE), BF16), pltpu.VMEM((BUCKET_ROWS, LANES), F32)],
        compiler_params=_cparams(),
        name="gmlp_layer",
    )(x_prompt, x_dec, nm, w_in, b_in, ln_g, ln_b, ws_p, ws_d, bs_p, bs_d, w_out, b_out,
      nf, *_router_split(wr), br)


def _router_body(hp_ref, hd_ref, nf_ref, wrh_ref, wr2_ref, br_ref, rows_ref, info_ref, cnt_ref, carry_ref):
    _route_init(carry_ref)
    h = jnp.where(pl.program_id(0) == N_TILES - 1, hd_ref[...], hp_ref[...])
    _store_rows(rows_ref, h, TM)
    _route_tile(h, nf_ref, wrh_ref, wr2_ref, br_ref, info_ref, cnt_ref, carry_ref)


def _router_split(wr):
    hi = wr.astype(BF16)
    return hi, jnp.concatenate([hi, (wr - hi.astype(F32)).astype(BF16)], axis=1)


def _route_init(carry_ref):
    @pl.when(pl.program_id(0) == 0)
    def _():
        carry_ref[...] = jnp.zeros_like(carry_ref)


def _route_tile(h, nf_ref, wrh_ref, wr2_ref, br_ref, info_ref, cnt_ref, carry_ref):
    xn = _rms(h, nf_ref[...])
    xh = xn.astype(BF16)
    xl = (xn - xh.astype(F32)).astype(BF16)
    both = _dot(xh, wr2_ref[...])
    logits = both[:, :LANES] + (_dot(xl, wrh_ref[...]) + both[:, LANES:]) + br_ref[...]
    lt = logits.T

    def first_max(vals, ids):
        vmax = jnp.max(vals, axis=0, keepdims=True)
        return vmax, jnp.min(jnp.where(vals == vmax, ids, float(LANES)), axis=0, keepdims=True)

    lg = lt[N_EXPERTS:N_EXPERTS + N_EGROUPS]
    gmax, g_idx = first_max(lg, lax.broadcasted_iota(jnp.int32, lg.shape, 0).astype(F32))
    g_p = 1.0 / jnp.sum(jnp.exp(lg - gmax), axis=0, keepdims=True)
    e_id = lax.broadcasted_iota(jnp.int32, (N_EXPERTS, TM), 0)
    in_group = (e_id // EXPERTS_PER_GROUP).astype(F32) == g_idx
    e_id = e_id.astype(F32)
    le = jnp.where(in_group, lt[:N_EXPERTS], -jnp.inf)
    v1, i1 = first_max(le, e_id)
    v2, i2 = first_max(jnp.where(e_id == i1, -jnp.inf, le), e_id)
    e2 = jnp.exp(v2 - v1)
    w1 = (1.0 / (1.0 + e2)) * g_p
    w2 = (e2 / (1.0 + e2)) * g_p
    a1 = i1 - EXPERTS_PER_GROUP * g_idx
    a2 = i2 - EXPERTS_PER_GROUP * g_idx
    lo = jnp.minimum(a1, a2)
    hi = jnp.maximum(a1, a2)
    pair = jnp.where(lo == 0.0, hi - 1.0, jnp.where(lo == 1.0, jnp.where(hi == 3.0, 3.0, 4.0), 5.0))
    ea = jnp.where(pair < 3.0, 0.0, jnp.where(pair < 5.0, 1.0, 3.0))
    ga = jnp.where(a1 == ea, w1, w2)
    gb = jnp.where(a1 == ea, w2, w1)
    bucket = g_idx * N_PAIRS + pair

    onehot = (lax.broadcasted_iota(jnp.int32, (BUCKET_ROWS, TM), 0).astype(F32) == bucket).astype(F32)
    r = lax.broadcasted_iota(jnp.int32, (TM, TM), 0)
    c = lax.broadcasted_iota(jnp.int32, (TM, TM), 1)
    before = _dot(onehot.astype(BF16), (r < c).astype(BF16))
    carry = carry_ref[:, 0:1]
    rank = jnp.sum(onehot * (before + carry), axis=0, keepdims=True)
    new_carry = carry + jnp.sum(onehot, axis=1, keepdims=True)
    carry_ref[...] = jnp.broadcast_to(new_carry, carry_ref.shape)
    cnt_ref[...] = jnp.broadcast_to(new_carry, cnt_ref.shape)
    info_ref[...] = jnp.concatenate([bucket, rank, ga, gb, jnp.zeros((SUBLANES - 4, TM), F32)], axis=0)


def _router(h_prompt, h_dec, nf, wr, br):
    const = lambda *shape: pl.BlockSpec(shape, lambda i: (0,) * len(shape))
    return pl.pallas_call(
        _router_body,
        grid=(N_TILES,),
        in_specs=[pl.BlockSpec((TM, D_MODEL), lambda i: (jnp.minimum(i, N_TILES - 2), 0)), const(N_DEC, D_MODEL),
                  const(1, D_MODEL), const(D_MODEL, LANES), const(D_MODEL, 2 * LANES), const(1, LANES)],
        out_specs=[pl.BlockSpec((TM * ROW_TILES, LANES), lambda i: (i, 0)),
                   pl.BlockSpec((SUBLANES, TM), lambda i: (0, i)),
                   const(BUCKET_ROWS, LANES)],
        out_shape=[jax.ShapeDtypeStruct((T * ROW_TILES, LANES), F32), jax.ShapeDtypeStruct((SUBLANES, T), F32),
                   jax.ShapeDtypeStruct((BUCKET_ROWS, LANES), F32)],
        scratch_shapes=[pltpu.VMEM((BUCKET_ROWS, LANES), F32)],
        compiler_params=_cparams(),
        name="moe_router",
    )(h_prompt, h_dec, nf, *_router_split(wr), br)


GATHER_UNROLL = 32


def _gather_rows_start(idx_ref, base, src_ref, dst_ref, sem, n):
    def group(g, carry):
        for u in range(GATHER_UNROLL):
            r = g * GATHER_UNROLL + u
            src = pl.multiple_of(idx_ref[base + r] * ROW_TILES, ROW_TILES)
            dst = pl.multiple_of(r * ROW_TILES, ROW_TILES)
            pltpu.make_async_copy(src_ref.at[pl.ds(src, ROW_TILES), :], dst_ref.at[pl.ds(dst, ROW_TILES), :],
                                  sem).start(priority=u % 2)
        return carry

    lax.fori_loop(0, n // GATHER_UNROLL, group, 0)


def _gather_rows_wait(src_ref, dst_ref, sem, n):
    pltpu.make_async_copy(src_ref.at[pl.ds(0, n * ROW_TILES), :], dst_ref.at[pl.ds(0, n * ROW_TILES), :], sem).wait()


def _gather_tile(idx_ref, src_ref, buf, sem, n, n_live=None, base_of=None, side_copy=None):
    i = pl.program_id(0)
    slot = lax.rem(i, 2)
    n_live = pl.num_programs(0) if n_live is None else n_live
    base_of = (lambda step: step * n) if base_of is None else base_of

    def start(step, s):
        _gather_rows_start(idx_ref, base_of(step), src_ref, buf.at[s], sem.at[s], n)
        if side_copy is not None:
            side_copy(step, s).start()

    @pl.when(i == 0)
    def _():
        start(0, 0)

    @pl.when(i + 1 < n_live)
    def _():
        start(i + 1, 1 - slot)

    @pl.when(i < n_live)
    def _():
        _gather_rows_wait(src_ref, buf.at[slot], sem.at[slot], n)
        if side_copy is not None:
            side_copy(i, slot).wait()

    return buf.at[slot]


def _gather_scratch(n):
    return [pltpu.VMEM((2, n * ROW_TILES, LANES), F32), pltpu.SemaphoreType.DMA((2,))]


GATE_WIN = TM + SUBLANES


def _moe_body(tok_ref, ea_ref, eb_ref, cha_ref, chb_ref, nlive_ref, first_ref, nrows_ref, rows_hbm, gates_hbm, nf_ref,
              w1a_ref, w3a_ref, w2a_ref, w1b_ref, w3b_ref, w2b_ref, out_ref,
              s1a, s3a, s2a, s1b, s3b, s2b, xbuf, xsem, gwin):
    i = pl.program_id(0)
    live = i < nlive_ref[0]

    def gate_copy(step, s):
        lo = pl.multiple_of((first_ref[step] // SUBLANES) * SUBLANES, SUBLANES)
        return pltpu.make_async_copy(gates_hbm.at[pl.ds(lo, GATE_WIN), :], gwin.at[s], xsem.at[s])

    x_ref = _gather_tile(tok_ref, rows_hbm, xbuf, xsem, TM, nlive_ref[0],
                         base_of=lambda step: first_ref[step], side_copy=gate_copy)

    @pl.when(cha_ref[i] == 1)
    def _():
        s1a[...] = w1a_ref[...].astype(BF16)
        s3a[...] = w3a_ref[...].astype(BF16)
        s2a[...] = w2a_ref[...].astype(BF16)

    @pl.when(chb_ref[i] == 1)
    def _():
        s1b[...] = w1b_ref[...].astype(BF16)
        s3b[...] = w3b_ref[...].astype(BF16)
        s2b[...] = w2b_ref[...].astype(BF16)

    @pl.when(live)
    def _():
        h = _load_rows(x_ref, TM)
        xn = _rms(h, nf_ref[...]).astype(BF16)
        g = gwin[lax.rem(i, 2), pl.ds(lax.rem(first_ref[i], SUBLANES), TM), :]
        row = lax.broadcasted_iota(jnp.int32, (TM, 1), 0)
        g = jnp.where(row < nrows_ref[i], g, 0.0)

        def ffn(w1, w3, w2):
            a = _dot(xn, w1[...])
            hdn = (a * (1.0 / (1.0 + jnp.exp(-a)))) * _dot(xn, w3[...])
            return _dot(hdn.astype(BF16), w2[...])

        y = g[:, 0:1] * ffn(s1a, s3a, s2a) + g[:, 1:2] * ffn(s1b, s3b, s2b)
        _store_rows(out_ref, h + y, TM)

    @pl.when(jnp.logical_not(live))
    def _():
        out_ref[...] = jnp.zeros_like(out_ref)


def _moe_ffn(layer, tok, ea, eb, cha, chb, n_live, first, nrows, rows, gates, nf, w1, w3, w2):
    wa = lambda shape: pl.BlockSpec((None, None) + shape, lambda i, tk, ea, eb, *_: (layer, ea[i], 0, 0))
    wb = lambda shape: pl.BlockSpec((None, None) + shape, lambda i, tk, ea, eb, *_: (layer, eb[i], 0, 0))
    up, down = (D_MODEL, D_EXPERT), (D_EXPERT, D_MODEL)
    return pl.pallas_call(
        _moe_body,
        grid_spec=pltpu.PrefetchScalarGridSpec(
            num_scalar_prefetch=8,
            grid=(MOE_TILES,),
            in_specs=[
                pl.BlockSpec(memory_space=pl.ANY),
                pl.BlockSpec(memory_space=pl.ANY),
                pl.BlockSpec((1, D_MODEL), lambda i, *_: (0, 0)),
                wa(up), wa(up), wa(down), wb(up), wb(up), wb(down),
            ],
            out_specs=pl.BlockSpec((TM * ROW_TILES, LANES), lambda i, *_: (i, 0)),
            scratch_shapes=[pltpu.VMEM(up, BF16), pltpu.VMEM(up, BF16), pltpu.VMEM(down, BF16),
                            pltpu.VMEM(up, BF16), pltpu.VMEM(up, BF16), pltpu.VMEM(down, BF16)]
            + _gather_scratch(TM) + [pltpu.VMEM((2, GATE_WIN, 2), F32)],
        ),
        out_shape=jax.ShapeDtypeStruct((P_ROWS * ROW_TILES, LANES), F32),
        compiler_params=_cparams(),
        name="moe_ffn",
    )(tok, ea, eb, cha, chb, n_live, first, nrows, rows, gates, nf, w1, w3, w2, w1, w3, w2)


def _router_weights(layer, w_group, b_group, w_expert, b_expert):
    pad = LANES - N_EXPERTS - N_EGROUPS
    wr = jnp.concatenate([w_expert[layer], w_group[layer], jnp.zeros((D_MODEL, pad), F32)], axis=1)
    br = jnp.concatenate([b_expert[layer], b_group[layer], jnp.zeros((pad,), F32)]).reshape(1, LANES)
    return wr, br


def _moe_layer(layer, rows, info, cnt, nf, w1, w3, w2):
    bucket = info[0].astype(jnp.int32)
    rank = info[1].astype(jnp.int32)
    counts = cnt[:N_BUCKETS, 0].astype(jnp.int32)
    n_tiles = (counts + TM - 1) // TM
    tile_end = jnp.cumsum(n_tiles)
    tile_start = tile_end - n_tiles
    start_of = jnp.sum(jnp.where(bucket[:, None] == jnp.arange(N_BUCKETS)[None, :], tile_start[None, :], 0), axis=1)
    pos = start_of * TM + rank
    _, tok, ga, gb = lax.sort((pos, jnp.arange(T, dtype=jnp.int32), info[2], info[3]), num_keys=1)
    tok = jnp.concatenate([tok, jnp.zeros((TM,), jnp.int32)])
    gates = jnp.concatenate([jnp.stack([ga, gb], axis=1), jnp.zeros((TM + SUBLANES, 2), F32)])
    total = tile_end[-1]
    j = jnp.minimum(jnp.arange(MOE_TILES), total - 1)
    tb = jnp.sum((j[:, None] >= tile_end[None, :]).astype(jnp.int32), axis=1)
    tokens_before = jnp.cumsum(counts) - counts
    in_bucket = (j - tile_start[tb]) * TM
    first = (tokens_before[tb] + in_bucket).astype(jnp.int32)
    nrows = jnp.clip(counts[tb] - in_bucket, 0, TM).astype(jnp.int32)
    grp, pair = tb // N_PAIRS, tb % N_PAIRS
    ea = (grp * EXPERTS_PER_GROUP + jnp.asarray(PAIR_A, jnp.int32)[pair]).astype(jnp.int32)
    eb = (grp * EXPERTS_PER_GROUP + jnp.asarray(PAIR_B, jnp.int32)[pair]).astype(jnp.int32)
    tile0 = jnp.arange(MOE_TILES) == 0
    cha = (tile0 | (ea != jnp.roll(ea, 1))).astype(jnp.int32)
    chb = (tile0 | (eb != jnp.roll(eb, 1))).astype(jnp.int32)
    n_live = total.reshape(1).astype(jnp.int32)
    return _moe_ffn(layer, tok, ea, eb, cha, chb, n_live, first, nrows, rows, gates, nf, w1, w3, w2), pos


VT_ROWS = LANES + 16


def _rope_swap(x):
    lane = lax.broadcasted_iota(jnp.int32, x.shape, 1)
    return jnp.where(lane < ROPE_LO + ROPE_HALF, pltpu.roll(x, LANES - ROPE_HALF, 1), pltpu.roll(x, ROPE_HALF, 1))


def _expand_k(cb, cbt, kr, wuk_ref, wukt_ref, kg_ref, k_ref):
    n = cb.shape[0]
    kn = _dot(cb, wuk_ref[...])
    knt = _dot(wukt_ref[...], cbt)
    ms = [jnp.mean(jnp.square(knt[hh * NOPE_DIM:(hh + 1) * NOPE_DIM]), axis=0, keepdims=True)
          for hh in range(N_HEADS)]
    scale = lax.rsqrt(jnp.concatenate(ms + [jnp.ones((LANES - N_HEADS, n), F32)], axis=0) + EPS).T
    for hh in range(N_HEADS):
        cols = slice(hh * HEAD_W, (hh + 1) * HEAD_W)
        k_ref[:, cols] = (kn[:, cols] * scale[:, hh:hh + 1] * kg_ref[...] + kr).astype(BF16)


def _mla_proj_body(pos_ref, sorted_hbm, cos_ref, sin_ref, cost_ref, sint_ref, kvn_ref, wdkv_ref, kvan_ref, krg_ref,
                   wuk_ref, wuv_ref, wuvt_ref, kg_ref, nmq_ref, wdq_ref, qan_ref, wuqt_ref, qg_ref, wukt_ref,
                   rows_ref, ckvp_ref, ckvd_ref, krp_ref, krd_ref, k_ref, vdec_ref, vt_ref, qt_ref, qdec_ref,
                   xbuf, xsem):
    is_dec = pl.program_id(0) == N_TILES - 1
    x_ref = _gather_tile(pos_ref, sorted_hbm, xbuf, xsem, TM)
    rows_ref[...] = x_ref[...]
    h = _load_rows(x_ref, TM)
    hn = h * lax.rsqrt(jnp.mean(h * h, axis=-1, keepdims=True) + EPS)
    c = _dot((hn * kvn_ref[...]).astype(BF16), wdkv_ref[...])
    ckv = _rms(c[:, :KV_RANK], kvan_ref[...])
    kr = c[:, KV_RANK:]
    kr = kr * lax.rsqrt(jnp.sum(kr * kr, axis=-1, keepdims=True) * (1.0 / ROPE_DIM) + EPS) * krg_ref[...]
    kr = kr * cos_ref[...] + _rope_swap(kr) * sin_ref[...]
    cb = ckv.astype(BF16)
    cbt = ckv.T.astype(BF16)
    _expand_k(cb, cbt, kr, wuk_ref, wukt_ref, kg_ref, k_ref)
    vt = _dot(wuvt_ref[...], cbt).astype(BF16)
    for pr in range(N_HEADS // 2):
        vt_ref[0, pr * VT_ROWS:pr * VT_ROWS + LANES, :] = vt[pr * LANES:(pr + 1) * LANES]
        vt_ref[0, pr * VT_ROWS + LANES:(pr + 1) * VT_ROWS, :] = jnp.ones((VT_ROWS - LANES, TM), BF16)
    cq = _rms(_dot((hn * nmq_ref[...]).astype(BF16), wdq_ref[...]), qan_ref[...])
    qt = _dot(wuqt_ref[...], cq.T.astype(BF16))
    cost, sint, qg = cost_ref[0], sint_ref[0], qg_ref[...]
    for hh in range(N_HEADS):
        x = qt[hh * HEAD_W:(hh + 1) * HEAD_W, :]
        xn, xr = x[:NOPE_DIM], x[ROPE_LO:ROPE_LO + ROPE_DIM]
        xn = xn * lax.rsqrt(jnp.mean(xn * xn, axis=0, keepdims=True) + EPS) * qg[:NOPE_DIM]
        xr = xr * lax.rsqrt(jnp.mean(xr * xr, axis=0, keepdims=True) + EPS) * qg[ROPE_LO:ROPE_LO + ROPE_DIM]
        x1, x2 = xr[:ROPE_HALF], xr[ROPE_HALF:]
        qh = jnp.concatenate([xn, x1 * cost - x2 * sint, x1 * sint + x2 * cost,
                              jnp.zeros((HEAD_W - NOPE_DIM - ROPE_DIM, TM), F32)], axis=0) * Q_SCALE
        qt_ref[0, hh * HEAD_W:(hh + 1) * HEAD_W, :] = qh.astype(BF16)

    @pl.when(jnp.logical_not(is_dec))
    def _():
        ckvp_ref[...] = ckv
        krp_ref[...] = kr[:, ROPE_LO:ROPE_LO + ROPE_DIM]

    @pl.when(is_dec)
    def _():
        ckvd_ref[...] = ckv
        krd_ref[...] = kr[:, ROPE_LO:ROPE_LO + ROPE_DIM]
        vdec_ref[...] = _dot(cb, wuv_ref[...]).astype(BF16)
        for hh in range(N_HEADS):
            rows = slice(hh * HEAD_W, (hh + 1) * HEAD_W)
            qdec_ref[:, rows] = qt_ref[0, rows, :].astype(F32).T.astype(BF16)


def _mla_proj(pos, sorted_rows, cos_t, sin_t, cos_tt, sin_tt, kvn, wdkv, kvan, krg, wuk, wukt, wuv, kg, nmq, wdq, qan, wuq, qg):
    const = lambda *shape: pl.BlockSpec(shape, lambda i, p: (0,) * len(shape))
    tab_tile = lambda i: jnp.where(i < N_PROMPT // TM, i % (SEQ // TM), SEQ // TM)
    tab = pl.BlockSpec((TM, LANES), lambda i, p: (tab_tile(i), 0))
    tab_t = pl.BlockSpec((1, ROPE_HALF, TM), lambda i, p: (tab_tile(i), 0, 0))
    row = lambda w: pl.BlockSpec((TM, w), lambda i, p: (i, 0))
    prow = lambda w: pl.BlockSpec((TM, w), lambda i, p: (jnp.minimum(i, N_TILES - 2), 0))
    return pl.pallas_call(
        _mla_proj_body,
        grid_spec=pltpu.PrefetchScalarGridSpec(
            num_scalar_prefetch=1,
            grid=(N_TILES,),
            in_specs=[
                pl.BlockSpec(memory_space=pl.ANY), tab, tab, tab_t, tab_t,
                const(1, D_MODEL), const(D_MODEL, KV_RANK + LANES), const(1, KV_RANK), const(1, LANES),
                const(KV_RANK, N_HEADS * HEAD_W), const(KV_RANK, N_HEADS * V_DIM), const(N_HEADS * V_DIM, KV_RANK),
                const(1, LANES),
                const(1, D_MODEL), const(D_MODEL, Q_RANK), const(1, Q_RANK), const(N_HEADS * HEAD_W, Q_RANK),
                const(HEAD_W, TM), const(N_HEADS * NOPE_DIM, KV_RANK),
            ],
            out_specs=[pl.BlockSpec((TM * ROW_TILES, LANES), lambda i, p: (i, 0)),
                       prow(KV_RANK), const(N_DEC, KV_RANK), prow(ROPE_DIM), const(N_DEC, ROPE_DIM),
                       row(N_HEADS * HEAD_W), const(N_DEC, N_HEADS * V_DIM),
                       pl.BlockSpec((1, N_HEADS // 2 * VT_ROWS, TM), lambda i, p: (i, 0, 0)),
                       pl.BlockSpec((1, N_HEADS * HEAD_W, TM), lambda i, p: (i, 0, 0)),
                       const(N_DEC, N_HEADS * HEAD_W)],
            scratch_shapes=_gather_scratch(TM),
        ),
        out_shape=[
            jax.ShapeDtypeStruct((T * ROW_TILES, LANES), F32),
            jax.ShapeDtypeStruct((N_PROMPT, KV_RANK), F32), jax.ShapeDtypeStruct((N_DEC, KV_RANK), F32),
            jax.ShapeDtypeStruct((N_PROMPT, ROPE_DIM), F32), jax.ShapeDtypeStruct((N_DEC, ROPE_DIM), F32),
            jax.ShapeDtypeStruct((T, N_HEADS * HEAD_W), BF16), jax.ShapeDtypeStruct((N_DEC, N_HEADS * V_DIM), BF16),
            jax.ShapeDtypeStruct((N_TILES, N_HEADS // 2 * VT_ROWS, TM), BF16),
            jax.ShapeDtypeStruct((N_TILES, N_HEADS * HEAD_W, TM), BF16),
            jax.ShapeDtypeStruct((N_DEC, N_HEADS * HEAD_W), BF16),
        ],
        compiler_params=_cparams(),
        name="mla_proj",
    )(pos, sorted_rows, cos_t, sin_t, cos_tt, sin_tt, kvn, wdkv, kvan, krg, wuk, wuv, wuv.T, kg, nmq, wdq, qan,
      wuq.T, jnp.broadcast_to(qg.reshape(HEAD_W, 1), (HEAD_W, TM)), wukt)


def _cache_kv_body(ckv_ref, kr_ref, place_ref, wuk_ref, wuv_ref, kg_ref, wukt_ref, k_ref, v_ref):
    kr = _dot(kr_ref[...].astype(BF16), place_ref[...])
    cb = ckv_ref[...].astype(BF16)
    _expand_k(cb, ckv_ref[...].T.astype(BF16), kr, wuk_ref, wukt_ref, kg_ref, k_ref)
    v_ref[...] = _dot(cb, wuv_ref[...]).astype(BF16)


CACHE_ROWS = 1024
assert PAST_LEN % CACHE_ROWS == 0


def _cache_kv(ckv, kr, wuk, wukt, wuv, kg):
    n = ckv.shape[0]
    place = jnp.asarray(np.arange(ROPE_DIM)[:, None] + ROPE_LO == np.arange(LANES)[None, :], BF16)
    const = lambda *shape: pl.BlockSpec(shape, lambda i: (0,) * len(shape))
    row = lambda w: pl.BlockSpec((CACHE_ROWS, w), lambda i: (i, 0))
    return pl.pallas_call(
        _cache_kv_body,
        grid=(n // CACHE_ROWS,),
        in_specs=[row(KV_RANK), row(ROPE_DIM), const(ROPE_DIM, LANES), const(KV_RANK, N_HEADS * HEAD_W),
                  const(KV_RANK, N_HEADS * V_DIM), const(1, LANES), const(N_HEADS * NOPE_DIM, KV_RANK)],
        out_specs=[row(N_HEADS * HEAD_W), row(N_HEADS * V_DIM)],
        out_shape=[jax.ShapeDtypeStruct((n, N_HEADS * HEAD_W), BF16), jax.ShapeDtypeStruct((n, N_HEADS * V_DIM), BF16)],
        compiler_params=_cparams(),
        name="cache_kv",
    )(ckv, kr, place, wuk, wuv, kg, wukt)


TQ = 256
TK = 256
assert TQ == TK and TQ % CHUNK == 0
SCORE_LOOKAHEAD = 8


def _qk(q, k):
    return lax.dot_general(q, k, (((1,), (1,)), ((), ())), preferred_element_type=F32)


def _merge_heads(o_ref, outs, rows):
    lane = lax.broadcasted_iota(jnp.int32, (rows, LANES), 1)
    for pr in range(N_HEADS // 2):
        o_ref[:, pr * LANES:(pr + 1) * LANES] = jnp.where(lane < V_DIM, outs[2 * pr], outs[2 * pr + 1]).astype(BF16)


def _prompt_attn_body(qt_ref, k_ref, vt_ref, rows_ref, wo_ref, out_ref, m_scr, acc_scr):
    qi = pl.program_id(1)
    m_scr[...] = jnp.full(m_scr.shape, NEG, F32)
    acc_scr[...] = jnp.zeros(acc_scr.shape, F32)

    def all_heads(j, mask):
        ks = pl.ds(pl.multiple_of(j * TK, TK), TK)

        def scores(hh):
            hcols = slice(hh * HEAD_W, (hh + 1) * HEAD_W)
            return _dot(k_ref[ks, hcols], qt_ref[0, hcols, :])

        ahead = [scores(hh) for hh in range(SCORE_LOOKAHEAD)]
        for hh in range(N_HEADS):
            vrows = slice((hh // 2) * VT_ROWS, (hh // 2 + 1) * VT_ROWS)
            s = ahead.pop(0)
            if hh + SCORE_LOOKAHEAD < N_HEADS:
                ahead.append(scores(hh + SCORE_LOOKAHEAD))
            if mask is not None:
                s = jnp.where(mask, s, NEG)
            m_old = m_scr[hh]
            m_new = jnp.maximum(m_old, jnp.max(s, axis=0, keepdims=True))
            p = jnp.exp2(s - m_new).astype(BF16)
            m_scr[hh] = m_new
            acc_scr[hh] = jnp.exp2(m_old - m_new) * acc_scr[hh] + _dot(vt_ref[j, vrows, :], p)

    def step(j, carry):
        all_heads(j, None)
        return carry

    lax.fori_loop(0, qi, step, 0)
    kc = lax.broadcasted_iota(jnp.int32, (TK, TQ), 0) // CHUNK
    qc = lax.broadcasted_iota(jnp.int32, (TK, TQ), 1) // CHUNK
    all_heads(qi, kc <= qc)
    row = lax.broadcasted_iota(jnp.int32, (LANES, TQ), 0)
    pairs = []
    for pr in range(N_HEADS // 2):
        even = acc_scr[2 * pr, :LANES] / acc_scr[2 * pr, LANES:LANES + 1]
        odd = acc_scr[2 * pr + 1, :LANES] / acc_scr[2 * pr + 1, LANES:LANES + 1]
        pairs.append(jnp.where(row < V_DIM, even, odd))
    o = jnp.concatenate(pairs, axis=0).T.astype(BF16)
    out_ref[...] = _load_rows(rows_ref, TQ) + _dot(o, wo_ref[...])


def _prompt_attn(qt, k, vt, rows, wo):
    nq = SEQ // TQ
    return pl.pallas_call(
        _prompt_attn_body,
        grid=(BATCH, nq),
        in_specs=[
            pl.BlockSpec((1, N_HEADS * HEAD_W, TQ), lambda b, i: (b * nq + i, 0, 0)),
            pl.BlockSpec((SEQ, N_HEADS * HEAD_W), lambda b, i: (b, 0)),
            pl.BlockSpec((SEQ // TK, N_HEADS // 2 * VT_ROWS, TK), lambda b, i: (b, 0, 0)),
            pl.BlockSpec((TQ * ROW_TILES, LANES), lambda b, i: (b * nq + i, 0)),
            pl.BlockSpec((N_HEADS * V_DIM, D_MODEL), lambda b, i: (0, 0)),
        ],
        out_specs=pl.BlockSpec((TQ, D_MODEL), lambda b, i: (b * nq + i, 0)),
        out_shape=jax.ShapeDtypeStruct((N_PROMPT, D_MODEL), F32),
        scratch_shapes=[pltpu.VMEM((N_HEADS, 1, TQ), F32), pltpu.VMEM((N_HEADS, VT_ROWS, TQ), F32)],
        compiler_params=_cparams(2),
        name="prompt_attn",
    )(qt, k, vt, rows, wo)


def _sample_attn_body(q_ref, kc_ref, vc_ref, kn_ref, vn_ref, rows_ref, wo_ref, out_ref, o_scr):
    outs = []
    for hh in range(N_HEADS):
        hcols = slice(hh * HEAD_W, (hh + 1) * HEAD_W)
        vcols = slice((hh // 2) * LANES, (hh // 2 + 1) * LANES)
        q = q_ref[:, hcols]
        sc = _qk(q, kc_ref[:, hcols])
        sn = _qk(q, kn_ref[:, hcols])
        m = jnp.maximum(jnp.max(sc, axis=-1, keepdims=True), jnp.max(sn, axis=-1, keepdims=True))
        pc = jnp.exp2(sc - m)
        pn = jnp.exp2(sn - m)
        l = jnp.sum(pc, axis=-1, keepdims=True) + jnp.sum(pn, axis=-1, keepdims=True)
        acc = _dot(pc.astype(BF16), vc_ref[:, vcols]) + _dot(pn.astype(BF16), vn_ref[:, vcols])
        outs.append(acc / l)
    _merge_heads(o_scr, outs, DEC_SEQ)
    out_ref[...] = _load_rows(rows_ref, DEC_SEQ) + _dot(o_scr[...], wo_ref[...])


def _sample_attn(q, kc, vc, kn, vn, rows, wo):
    off = N_PROMPT // DEC_SEQ
    return pl.pallas_call(
        _sample_attn_body,
        grid=(DEC_BATCH,),
        in_specs=[
            pl.BlockSpec((DEC_SEQ, N_HEADS * HEAD_W), lambda b: (b, 0)),
            pl.BlockSpec((PAST_LEN, N_HEADS * HEAD_W), lambda b: (b, 0)),
            pl.BlockSpec((PAST_LEN, N_HEADS * V_DIM), lambda b: (b, 0)),
            pl.BlockSpec((DEC_SEQ, N_HEADS * HEAD_W), lambda b: (off + b, 0)),
            pl.BlockSpec((DEC_SEQ, N_HEADS * V_DIM), lambda b: (b, 0)),
            pl.BlockSpec((DEC_SEQ * ROW_TILES, LANES), lambda b: (off + b, 0)),
            pl.BlockSpec((N_HEADS * V_DIM, D_MODEL), lambda b: (0, 0)),
        ],
        out_specs=pl.BlockSpec((DEC_SEQ, D_MODEL), lambda b: (b, 0)),
        out_shape=jax.ShapeDtypeStruct((N_DEC, D_MODEL), F32),
        scratch_shapes=[pltpu.VMEM((DEC_SEQ, N_HEADS * V_DIM), BF16)],
        compiler_params=_cparams(),
        name="sample_attn",
    )(q, kc, vc, kn, vn, rows, wo)


def _finish_body(pos_ref, sorted_hbm, yp_ref, ys_ref, xbuf, xsem):
    i = pl.program_id(0)
    y = _load_rows(_gather_tile(pos_ref, sorted_hbm, xbuf, xsem, TM), TM)

    @pl.when(i < N_TILES - 1)
    def _():
        yp_ref[...] = y

    @pl.when(i == N_TILES - 1)
    def _():
        ys_ref[...] = y


def _finish(pos, sorted_rows):
    return pl.pallas_call(
        _finish_body,
        grid_spec=pltpu.PrefetchScalarGridSpec(
            num_scalar_prefetch=1,
            grid=(N_TILES,),
            in_specs=[pl.BlockSpec(memory_space=pl.ANY)],
            out_specs=[pl.BlockSpec((TM, D_MODEL), lambda i, p: (jnp.minimum(i, N_TILES - 2), 0)),
                       pl.BlockSpec((N_DEC, D_MODEL), lambda i, p: (0, 0))],
            scratch_shapes=_gather_scratch(TM),
        ),
        out_shape=[jax.ShapeDtypeStruct((N_PROMPT, D_MODEL), F32), jax.ShapeDtypeStruct((N_DEC, D_MODEL), F32)],
        compiler_params=_cparams(),
        name="finish",
    )(pos, sorted_rows)


def _rope_tables():
    half = ROPE_DIM // 2
    inv_freq = ROPE_THETA ** (-jnp.arange(half, dtype=F32) / half)
    dec_pos = PAST_LEN + jnp.tile(jnp.arange(DEC_SEQ, dtype=jnp.int32), DEC_BATCH)
    pos = jnp.concatenate([jnp.arange(SEQ, dtype=jnp.int32), dec_pos])
    ang = pos.astype(F32)[:, None] * inv_freq[None, :]
    cos, sin = jnp.cos(ang), jnp.sin(ang)
    n = pos.shape[0]
    cos_t = jnp.ones((n, LANES), F32).at[:, ROPE_LO:ROPE_LO + ROPE_DIM].set(jnp.concatenate([cos, cos], axis=1))
    sin_t = jnp.zeros((n, LANES), F32).at[:, ROPE_LO:ROPE_LO + ROPE_DIM].set(jnp.concatenate([-sin, sin], axis=1))
    to_tiles = lambda a: a.reshape(n // TM, TM, half).transpose(0, 2, 1)
    return cos_t, sin_t, to_tiles(cos), to_tiles(sin)


def _on_lanes(vec, lo):
    return jnp.zeros((1, LANES), F32).at[0, lo:lo + vec.shape[0]].set(vec)


def kernel(x_prompt, x_sample, cache_ckv, cache_krope, norm_mix, norm_ffn, gm_w_in, gm_b_in, gm_ln_g, gm_ln_b, gm_w_s, gm_b_s, gm_w_out, gm_b_out, kv_norm, w_dkv, kv_a_norm, k_rope_norm, w_uk, w_uv, k_nope_norm, w_dq, q_a_norm, w_uq, q_nope_norm, q_rope_norm, w_o, moe_w_group, moe_b_group, moe_w_expert, moe_b_expert, moe_w1, moe_w3, moe_w2):
    nf0, nf1 = norm_ffn[0].reshape(1, D_MODEL), norm_ffn[1].reshape(1, D_MODEL)

    idx = np.arange(GMLP_BLOCK)
    allowed = (idx[None, :] // CHUNK) <= (idx[:, None] // CHUNK)
    ws_p = jnp.where(allowed[None], gm_w_s[0], 0.0).astype(BF16)
    same_seq = (idx[None, :] // DEC_SEQ) == (idx[:, None] // DEC_SEQ)
    ws_d = jnp.where(same_seq[None], jnp.tile(gm_w_s[0][:, :DEC_SEQ, :DEC_SEQ], (1, GMLP_BLOCK // DEC_SEQ, GMLP_BLOCK // DEC_SEQ)), 0.0).astype(BF16)
    bs_p = gm_b_s[0][:, :, None]
    bs_d = jnp.tile(gm_b_s[0][:, :DEC_SEQ], (1, GMLP_BLOCK // DEC_SEQ))[:, :, None]
    rows, v_rows, info, cnt = _gmlp_layer(
        x_prompt.reshape(N_PROMPT, D_MODEL), x_sample.reshape(N_DEC, D_MODEL),
        norm_mix[0].reshape(1, -1), gm_w_in[0].astype(BF16), gm_b_in[0].reshape(1, -1),
        gm_ln_g[0].reshape(1, -1), gm_ln_b[0].reshape(1, -1), ws_p, ws_d, bs_p, bs_d,
        gm_w_out[0].astype(BF16), gm_b_out[0].reshape(1, -1),
        nf0, *_router_weights(0, moe_w_group, moe_b_group, moe_w_expert, moe_b_expert))
    sorted_rows, pos = _moe_layer(0, rows, info, cnt, nf0, moe_w1, moe_w3, moe_w2)

    cos_t, sin_t, cos_tt, sin_tt = _rope_tables()
    wdkv = jnp.zeros((D_MODEL, KV_RANK + LANES), F32).at[:, :KV_RANK].set(w_dkv[:, :KV_RANK])
    wdkv = wdkv.at[:, KV_RANK + ROPE_LO:KV_RANK + ROPE_LO + ROPE_DIM].set(w_dkv[:, KV_RANK:]).astype(BF16)
    wuk = jnp.zeros((KV_RANK, N_HEADS, HEAD_W), F32).at[:, :, :NOPE_DIM].set(w_uk).reshape(KV_RANK, -1).astype(BF16)
    wukt = w_uk.reshape(KV_RANK, -1).T.astype(BF16)
    wuv = w_uv.reshape(KV_RANK, -1).astype(BF16)
    wuq =jnp.zeros((Q_RANK, N_HEADS, HEAD_W), F32).at[:, :, :NOPE_DIM + ROPE_DIM].set(w_uq[0]).reshape(Q_RANK, -1).astype(BF16)
    kg = _on_lanes(k_nope_norm, 0)
    krg = _on_lanes(k_rope_norm, ROPE_LO)
    qg = _on_lanes(jnp.concatenate([q_nope_norm[0], q_rope_norm[0]]), 0)
    rows, ckv_p, ckv_d, krope_p, krope_d, k_new, v_dec, vt_new, qt, q_dec = _mla_proj(
        pos, sorted_rows, cos_t, sin_t, cos_tt, sin_tt, kv_norm.reshape(1, -1), wdkv, kv_a_norm.reshape(1, -1), krg, wuk, wukt, wuv, kg,
        norm_mix[1].reshape(1, -1), w_dq[0].astype(BF16), q_a_norm[0].reshape(1, -1), wuq, qg)
    k_cache, v_cache = _cache_kv(cache_ckv.reshape(-1, KV_RANK), cache_krope.reshape(-1, ROPE_DIM), wuk, wukt, wuv, kg)

    wo = w_o[0].astype(BF16)
    h_prompt = _prompt_attn(qt, k_new, vt_new, rows, wo)
    h_dec = _sample_attn(q_dec, k_cache, v_cache, k_new, v_dec, rows, wo)
    rows, info, cnt = _router(h_prompt, h_dec, nf1,
                              *_router_weights(1, moe_w_group, moe_b_group, moe_w_expert, moe_b_expert))
    sorted_rows, pos = _moe_layer(1, rows, info, cnt, nf1, moe_w1, moe_w3, moe_w2)
    y_prompt, y_sample = _finish(pos, sorted_rows)

    return (y_prompt.reshape(BATCH, SEQ, D_MODEL), y_sample.reshape(DEC_BATCH, DEC_SEQ, D_MODEL),
            ckv_p.reshape(BATCH, SEQ, KV_RANK), krope_p.reshape(BATCH, SEQ, ROPE_DIM),
            ckv_d.reshape(DEC_BATCH, DEC_SEQ, KV_RANK), krope_d.reshape(DEC_BATCH, DEC_SEQ, ROPE_DIM),
            v_rows.reshape(1, DEC_BATCH, DEC_SEQ, D_GATE))
```

```python
import functools

import jax
import jax.numpy as jnp
import numpy as np
from jax import lax
from jax.experimental import pallas as pl
from jax.experimental.pallas import tpu as pltpu

F32 = jnp.float32
BF16 = jnp.bfloat16

D_MODEL = 1024
BATCH = 8
SEQ = 2048
DEC_BATCH = 16
DEC_SEQ = 16
PAST_LEN = 2048
CHUNK = 64
GMLP_BLOCK = 128
D_GATE = 2 * D_MODEL
N_SG = 8
SG_W = D_GATE // N_SG
N_HEADS = 8
NOPE_DIM = 64
ROPE_DIM = 32
V_DIM = 64
Q_RANK = 384
KV_RANK = 256
ROPE_THETA = 10000.0
SCALE = (NOPE_DIM + ROPE_DIM) ** -0.5
Q_SCALE = SCALE * float(np.log2(np.e))
N_EGROUPS = 4
EXPERTS_PER_GROUP = 4
N_EXPERTS = N_EGROUPS * EXPERTS_PER_GROUP
D_EXPERT = 512
EPS = 1e-6
NEG = -1e30

LANES = 128
SUBLANES = 8
ROW_TILES = D_MODEL // LANES
assert ROW_TILES == SUBLANES

N_PROMPT = BATCH * SEQ
N_DEC = DEC_BATCH * DEC_SEQ
T = N_PROMPT + N_DEC
TM = 256
assert N_PROMPT % TM == 0 and N_DEC == TM
N_TILES = T // TM
HEAD_W = LANES
ROPE_LO = NOPE_DIM
ROPE_HALF = ROPE_DIM // 2

PAIR_A = (0, 0, 0, 1, 1, 3)
PAIR_B = (1, 2, 3, 3, 2, 2)
N_PAIRS = 6
N_BUCKETS = N_EGROUPS * N_PAIRS
BUCKET_ROWS = 32
assert N_BUCKETS <= BUCKET_ROWS and BUCKET_ROWS % SUBLANES == 0
MOE_TILES = (T + N_BUCKETS * (TM - 1) + TM - 1) // TM
P_ROWS = MOE_TILES * TM

VMEM_LIMIT = 56 * 1024 * 1024


def _cparams(n_axes=1, vmem=VMEM_LIMIT):
    return pltpu.CompilerParams(dimension_semantics=("arbitrary",) * n_axes, vmem_limit_bytes=vmem)


def _rms(x, g):
    return x * lax.rsqrt(jnp.mean(x * x, axis=-1, keepdims=True) + EPS) * g


def _load_rows(ref, n):
    return jnp.concatenate([ref[pl.ds(s, n, stride=ROW_TILES), :] for s in range(ROW_TILES)], axis=1)


def _store_rows(ref, x, n):
    for s in range(ROW_TILES):
        ref[pl.ds(s, n, stride=ROW_TILES), :] = x[:, s * LANES:(s + 1) * LANES]


def _dot(a, b):
    return jnp.dot(a, b, preferred_element_type=F32)


GELU_K1 = float(-2.0 * np.sqrt(2.0 / np.pi) * np.log2(np.e))
GELU_K3 = GELU_K1 * 0.044715


def _gmlp_body(xp_ref, xd_ref, nm_ref, win_ref, bin_ref, lng_ref, lnb_ref, wsp_ref, wsd_ref, bsp_ref, bsd_ref,
               wout_ref, bout_ref, nf_ref, wrh_ref, wr2_ref, br_ref,
               rows_ref, v_ref, info_ref, cnt_ref, gated_ref, carry_ref):
    i = pl.program_id(0)
    is_dec = i == N_TILES - 1
    _route_init(carry_ref)
    x = jnp.where(is_dec, xd_ref[...], xp_ref[...])
    xn = _rms(x, nm_ref[...]).astype(BF16)
    z = _dot(xn, win_ref[...]) + bin_ref[...]
    z = z / (1.0 + jnp.exp2(z * (GELU_K1 + GELU_K3 * (z * z))))
    u = z[:, :D_GATE]
    v = z[:, D_GATE:]
    mu = jnp.mean(v, axis=-1, keepdims=True)
    vc = v - mu
    var = jnp.mean(vc * vc, axis=-1, keepdims=True)
    v = vc * lax.rsqrt(var + EPS) * lng_ref[...] + lnb_ref[...]

    v_ref[...] = v
    vb = v.astype(BF16)
    for g in range(N_SG):
        ws = jnp.where(is_dec, wsd_ref[g], wsp_ref[g])
        bs = jnp.where(is_dec, bsd_ref[g], bsp_ref[g])
        for b in range(TM // GMLP_BLOCK):
            rows = slice(b * GMLP_BLOCK, (b + 1) * GMLP_BLOCK)
            cols = slice(g * SG_W, (g + 1) * SG_W)
            s = _dot(ws, vb[rows, cols]) + bs
            gated_ref[rows, cols] = (u[rows, cols] * s).astype(BF16)
    h = x + _dot(gated_ref[...], wout_ref[...]) + bout_ref[...]
    _store_rows(rows_ref, h, TM)
    _route_tile(h, nf_ref, wrh_ref, wr2_ref, br_ref, info_ref, cnt_ref, carry_ref)


def _gmlp_layer(x_prompt, x_dec, nm, w_in, b_in, ln_g, ln_b, ws_p, ws_d, bs_p, bs_d, w_out, b_out, nf, wr, br):
    const = lambda *shape: pl.BlockSpec(shape, lambda i: (0,) * len(shape))
    return pl.pallas_call(
        _gmlp_body,
        grid=(N_TILES,),
        in_specs=[
            pl.BlockSpec((TM, D_MODEL), lambda i: (jnp.minimum(i, N_TILES - 2), 0)), const(N_DEC, D_MODEL),
            const(1, D_MODEL), const(D_MODEL, 2 * D_GATE), const(1, 2 * D_GATE),
            const(1, D_GATE), const(1, D_GATE),
            const(N_SG, GMLP_BLOCK, GMLP_BLOCK), const(N_SG, GMLP_BLOCK, GMLP_BLOCK),
            const(N_SG, GMLP_BLOCK, 1), const(N_SG, GMLP_BLOCK, 1),
            const(D_GATE, D_MODEL), const(1, D_MODEL),
            const(1, D_MODEL), const(D_MODEL, LANES), const(D_MODEL, 2 * LANES), const(1, LANES),
        ],
        out_specs=[
            pl.BlockSpec((TM * ROW_TILES, LANES), lambda i: (i, 0)),
            const(N_DEC, D_GATE),
            pl.BlockSpec((SUBLANES, TM), lambda i: (0, i)),
            const(BUCKET_ROWS, LANES),
        ],
        out_shape=[jax.ShapeDtypeStruct((T * ROW_TILES, LANES), F32), jax.ShapeDtypeStruct((N_DEC, D_GATE), F32),
                   jax.ShapeDtypeStruct((SUBLANES, T), F32), jax.ShapeDtypeStruct((BUCKET_ROWS, LANES), F32)],
        scratch_shapes=[pltpu.VMEM((TM, D_GATE), BF16), pltpu.VMEM((BUCKET_ROWS, LANES), F32)],
        compiler_params=_cparams(),
        name="gmlp_layer",
    )(x_prompt, x_dec, nm, w_in, b_in, ln_g, ln_b, ws_p, ws_d, bs_p, bs_d, w_out, b_out,
      nf, *_router_split(wr), br)


def _router_body(hp_ref, hd_ref, nf_ref, wrh_ref, wr2_ref, br_ref, rows_ref, info_ref, cnt_ref, carry_ref):
    _route_init(carry_ref)
    h = jnp.where(pl.program_id(0) == N_TILES - 1, hd_ref[...], hp_ref[...])
    _store_rows(rows_ref, h, TM)
    _route_tile(h, nf_ref, wrh_ref, wr2_ref, br_ref, info_ref, cnt_ref, carry_ref)


def _router_split(wr):
    hi = wr.astype(BF16)
    return hi, jnp.concatenate([hi, (wr - hi.astype(F32)).astype(BF16)], axis=1)


def _route_init(carry_ref):
    @pl.when(pl.program_id(0) == 0)
    def _():
        carry_ref[...] = jnp.zeros_like(carry_ref)


def _route_tile(h, nf_ref, wrh_ref, wr2_ref, br_ref, info_ref, cnt_ref, carry_ref):
    xn = _rms(h, nf_ref[...])
    xh = xn.astype(BF16)
    xl = (xn - xh.astype(F32)).astype(BF16)
    both = _dot(xh, wr2_ref[...])
    logits = both[:, :LANES] + (_dot(xl, wrh_ref[...]) + both[:, LANES:]) + br_ref[...]
    lt = logits.T

    def first_max(vals, ids):
        vmax = jnp.max(vals, axis=0, keepdims=True)
        return vmax, jnp.min(jnp.where(vals == vmax, ids, float(LANES)), axis=0, keepdims=True)

    lg = lt[N_EXPERTS:N_EXPERTS + N_EGROUPS]
    gmax, g_idx = first_max(lg, lax.broadcasted_iota(jnp.int32, lg.shape, 0).astype(F32))
    g_p = 1.0 / jnp.sum(jnp.exp(lg - gmax), axis=0, keepdims=True)
    e_id = lax.broadcasted_iota(jnp.int32, (N_EXPERTS, TM), 0)
    in_group = (e_id // EXPERTS_PER_GROUP).astype(F32) == g_idx
    e_id = e_id.astype(F32)
    le = jnp.where(in_group, lt[:N_EXPERTS], -jnp.inf)
    v1, i1 = first_max(le, e_id)
    v2, i2 = first_max(jnp.where(e_id == i1, -jnp.inf, le), e_id)
    e2 = jnp.exp(v2 - v1)
    w1 = (1.0 / (1.0 + e2)) * g_p
    w2 = (e2 / (1.0 + e2)) * g_p
    a1 = i1 - EXPERTS_PER_GROUP * g_idx
    a2 = i2 - EXPERTS_PER_GROUP * g_idx
    lo = jnp.minimum(a1, a2)
    hi = jnp.maximum(a1, a2)
    pair = jnp.where(lo == 0.0, hi - 1.0, jnp.where(lo == 1.0, jnp.where(hi == 3.0, 3.0, 4.0), 5.0))
    ea = jnp.where(pair < 3.0, 0.0, jnp.where(pair < 5.0, 1.0, 3.0))
    ga = jnp.where(a1 == ea, w1, w2)
    gb = jnp.where(a1 == ea, w2, w1)
    bucket = g_idx * N_PAIRS + pair

    onehot = (lax.broadcasted_iota(jnp.int32, (BUCKET_ROWS, TM), 0).astype(F32) == bucket).astype(F32)
    r = lax.broadcasted_iota(jnp.int32, (TM, TM), 0)
    c = lax.broadcasted_iota(jnp.int32, (TM, TM), 1)
    before = _dot(onehot.astype(BF16), (r < c).astype(BF16))
    carry = carry_ref[:, 0:1]
    rank = jnp.sum(onehot * (before + carry), axis=0, keepdims=True)
    new_carry = carry + jnp.sum(onehot, axis=1, keepdims=True)
    carry_ref[...] = jnp.broadcast_to(new_carry, carry_ref.shape)
    cnt_ref[...] = jnp.broadcast_to(new_carry, cnt_ref.shape)
    info_ref[...] = jnp.concatenate([bucket, rank, ga, gb, jnp.zeros((SUBLANES - 4, TM), F32)], axis=0)


def _router(h_prompt, h_dec, nf, wr, br):
    const = lambda *shape: pl.BlockSpec(shape, lambda i: (0,) * len(shape))
    return pl.pallas_call(
        _router_body,
        grid=(N_TILES,),
        in_specs=[pl.BlockSpec((TM, D_MODEL), lambda i: (jnp.minimum(i, N_TILES - 2), 0)), const(N_DEC, D_MODEL),
                  const(1, D_MODEL), const(D_MODEL, LANES), const(D_MODEL, 2 * LANES), const(1, LANES)],
        out_specs=[pl.BlockSpec((TM * ROW_TILES, LANES), lambda i: (i, 0)),
                   pl.BlockSpec((SUBLANES, TM), lambda i: (0, i)),
                   const(BUCKET_ROWS, LANES)],
        out_shape=[jax.ShapeDtypeStruct((T * ROW_TILES, LANES), F32), jax.ShapeDtypeStruct((SUBLANES, T), F32),
                   jax.ShapeDtypeStruct((BUCKET_ROWS, LANES), F32)],
        scratch_shapes=[pltpu.VMEM((BUCKET_ROWS, LANES), F32)],
        compiler_params=_cparams(),
        name="moe_router",
    )(h_prompt, h_dec, nf, *_router_split(wr), br)


GATHER_UNROLL = 32


def _gather_rows_start(idx_ref, base, src_ref, dst_ref, sem, n):
    def group(g, carry):
        for u in range(GATHER_UNROLL):
            r = g * GATHER_UNROLL + u
            src = pl.multiple_of(idx_ref[base + r] * ROW_TILES, ROW_TILES)
            dst = pl.multiple_of(r * ROW_TILES, ROW_TILES)
            pltpu.make_async_copy(src_ref.at[pl.ds(src, ROW_TILES), :], dst_ref.at[pl.ds(dst, ROW_TILES), :],
                                  sem).start(priority=u % 2)
        return carry

    lax.fori_loop(0, n // GATHER_UNROLL, group, 0)


def _gather_rows_wait(src_ref, dst_ref, sem, n):
    pltpu.make_async_copy(src_ref.at[pl.ds(0, n * ROW_TILES), :], dst_ref.at[pl.ds(0, n * ROW_TILES), :], sem).wait()


def _gather_tile(idx_ref, src_ref, buf, sem, n, n_live=None, base_of=None, side_copy=None):
    i = pl.program_id(0)
    slot = lax.rem(i, 2)
    n_live = pl.num_programs(0) if n_live is None else n_live
    base_of = (lambda step: step * n) if base_of is None else base_of

    def start(step, s):
        _gather_rows_start(idx_ref, base_of(step), src_ref, buf.at[s], sem.at[s], n)
        if side_copy is not None:
            side_copy(step, s).start()

    @pl.when(i == 0)
    def _():
        start(0, 0)

    @pl.when(i + 1 < n_live)
    def _():
        start(i + 1, 1 - slot)

    @pl.when(i < n_live)
    def _():
        _gather_rows_wait(src_ref, buf.at[slot], sem.at[slot], n)
        if side_copy is not None:
            side_copy(i, slot).wait()

    return buf.at[slot]


def _gather_scratch(n):
    return [pltpu.VMEM((2, n * ROW_TILES, LANES), F32), pltpu.SemaphoreType.DMA((2,))]


GATE_WIN = TM + SUBLANES


def _moe_body(tok_ref, ea_ref, eb_ref, cha_ref, chb_ref, nlive_ref, first_ref, nrows_ref, rows_hbm, gates_hbm, nf_ref,
              w1a_ref, w3a_ref, w2a_ref, w1b_ref, w3b_ref, w2b_ref, out_ref,
              s1a, s3a, s2a, s1b, s3b, s2b, xbuf, xsem, gwin):
    i = pl.program_id(0)
    live = i < nlive_ref[0]

    def gate_copy(step, s):
        lo = pl.multiple_of((first_ref[step] // SUBLANES) * SUBLANES, SUBLANES)
        return pltpu.make_async_copy(gates_hbm.at[pl.ds(lo, GATE_WIN), :], gwin.at[s], xsem.at[s])

    x_ref = _gather_tile(tok_ref, rows_hbm, xbuf, xsem, TM, nlive_ref[0],
                         base_of=lambda step: first_ref[step], side_copy=gate_copy)

    @pl.when(cha_ref[i] == 1)
    def _():
        s1a[...] = w1a_ref[...].astype(BF16)
        s3a[...] = w3a_ref[...].astype(BF16)
        s2a[...] = w2a_ref[...].astype(BF16)

    @pl.when(chb_ref[i] == 1)
    def _():
        s1b[...] = w1b_ref[...].astype(BF16)
        s3b[...] = w3b_ref[...].astype(BF16)
        s2b[...] = w2b_ref[...].astype(BF16)

    @pl.when(live)
    def _():
        h = _load_rows(x_ref, TM)
        xn = _rms(h, nf_ref[...]).astype(BF16)
        g = gwin[lax.rem(i, 2), pl.ds(lax.rem(first_ref[i], SUBLANES), TM), :]
        row = lax.broadcasted_iota(jnp.int32, (TM, 1), 0)
        g = jnp.where(row < nrows_ref[i], g, 0.0)

        def ffn(w1, w3, w2):
            a = _dot(xn, w1[...])
            hdn = (a * (1.0 / (1.0 + jnp.exp(-a)))) * _dot(xn, w3[...])
            return _dot(hdn.astype(BF16), w2[...])

        y = g[:, 0:1] * ffn(s1a, s3a, s2a) + g[:, 1:2] * ffn(s1b, s3b, s2b)
        _store_rows(out_ref, h + y, TM)

    @pl.when(jnp.logical_not(live))
    def _():
        out_ref[...] = jnp.zeros_like(out_ref)


def _moe_ffn(layer, tok, ea, eb, cha, chb, n_live, first, nrows, rows, gates, nf, w1, w3, w2):
    wa = lambda shape: pl.BlockSpec((None, None) + shape, lambda i, tk, ea, eb, *_: (layer, ea[i], 0, 0))
    wb = lambda shape: pl.BlockSpec((None, None) + shape, lambda i, tk, ea, eb, *_: (layer, eb[i], 0, 0))
    up, down = (D_MODEL, D_EXPERT), (D_EXPERT, D_MODEL)
    return pl.pallas_call(
        _moe_body,
        grid_spec=pltpu.PrefetchScalarGridSpec(
            num_scalar_prefetch=8,
            grid=(MOE_TILES,),
            in_specs=[
                pl.BlockSpec(memory_space=pl.ANY),
                pl.BlockSpec(memory_space=pl.ANY),
                pl.BlockSpec((1, D_MODEL), lambda i, *_: (0, 0)),
                wa(up), wa(up), wa(down), wb(up), wb(up), wb(down),
            ],
            out_specs=pl.BlockSpec((TM * ROW_TILES, LANES), lambda i, *_: (i, 0)),
            scratch_shapes=[pltpu.VMEM(up, BF16), pltpu.VMEM(up, BF16), pltpu.VMEM(down, BF16),
                            pltpu.VMEM(up, BF16), pltpu.VMEM(up, BF16), pltpu.VMEM(down, BF16)]
            + _gather_scratch(TM) + [pltpu.VMEM((2, GATE_WIN, 2), F32)],
        ),
        out_shape=jax.ShapeDtypeStruct((P_ROWS * ROW_TILES, LANES), F32),
        compiler_params=_cparams(),
        name="moe_ffn",
    )(tok, ea, eb, cha, chb, n_live, first, nrows, rows, gates, nf, w1, w3, w2, w1, w3, w2)


def _router_weights(layer, w_group, b_group, w_expert, b_expert):
    pad = LANES - N_EXPERTS - N_EGROUPS
    wr = jnp.concatenate([w_expert[layer], w_group[layer], jnp.zeros((D_MODEL, pad), F32)], axis=1)
    br = jnp.concatenate([b_expert[layer], b_group[layer], jnp.zeros((pad,), F32)]).reshape(1, LANES)
    return wr, br


def _moe_layer(layer, rows, info, cnt, nf, w1, w3, w2):
    bucket = info[0].astype(jnp.int32)
    rank = info[1].astype(jnp.int32)
    counts = cnt[:N_BUCKETS, 0].astype(jnp.int32)
    n_tiles = (counts + TM - 1) // TM
    tile_end = jnp.cumsum(n_tiles)
    tile_start = tile_end - n_tiles
    start_of = jnp.sum(jnp.where(bucket[:, None] == jnp.arange(N_BUCKETS)[None, :], tile_start[None, :], 0), axis=1)
    pos = start_of * TM + rank
    _, tok, ga, gb = lax.sort((pos, jnp.arange(T, dtype=jnp.int32), info[2], info[3]), num_keys=1)
    tok = jnp.concatenate([tok, jnp.zeros((TM,), jnp.int32)])
    gates = jnp.concatenate([jnp.stack([ga, gb], axis=1), jnp.zeros((TM + SUBLANES, 2), F32)])
    total = tile_end[-1]
    j = jnp.minimum(jnp.arange(MOE_TILES), total - 1)
    tb = jnp.sum((j[:, None] >= tile_end[None, :]).astype(jnp.int32), axis=1)
    tokens_before = jnp.cumsum(counts) - counts
    in_bucket = (j - tile_start[tb]) * TM
    first = (tokens_before[tb] + in_bucket).astype(jnp.int32)
    nrows = jnp.clip(counts[tb] - in_bucket, 0, TM).astype(jnp.int32)
    grp, pair = tb // N_PAIRS, tb % N_PAIRS
    ea = (grp * EXPERTS_PER_GROUP + jnp.asarray(PAIR_A, jnp.int32)[pair]).astype(jnp.int32)
    eb = (grp * EXPERTS_PER_GROUP + jnp.asarray(PAIR_B, jnp.int32)[pair]).astype(jnp.int32)
    tile0 = jnp.arange(MOE_TILES) == 0
    cha = (tile0 | (ea != jnp.roll(ea, 1))).astype(jnp.int32)
    chb = (tile0 | (eb != jnp.roll(eb, 1))).astype(jnp.int32)
    n_live = total.reshape(1).astype(jnp.int32)
    return _moe_ffn(layer, tok, ea, eb, cha, chb, n_live, first, nrows, rows, gates, nf, w1, w3, w2), pos


VT_ROWS = LANES + 16


def _rope_swap(x):
    lane = lax.broadcasted_iota(jnp.int32, x.shape, 1)
    return jnp.where(lane < ROPE_LO + ROPE_HALF, pltpu.roll(x, LANES - ROPE_HALF, 1), pltpu.roll(x, ROPE_HALF, 1))


def _expand_k(cb, cbt, kr, wuk_ref, wukt_ref, kg_ref, k_ref):
    n = cb.shape[0]
    kn = _dot(cb, wuk_ref[...])
    knt = _dot(wukt_ref[...], cbt)
    ms = [jnp.mean(jnp.square(knt[hh * NOPE_DIM:(hh + 1) * NOPE_DIM]), axis=0, keepdims=True)
          for hh in range(N_HEADS)]
    scale = lax.rsqrt(jnp.concatenate(ms + [jnp.ones((LANES - N_HEADS, n), F32)], axis=0) + EPS).T
    for hh in range(N_HEADS):
        cols = slice(hh * HEAD_W, (hh + 1) * HEAD_W)
        k_ref[:, cols] = (kn[:, cols] * scale[:, hh:hh + 1] * kg_ref[...] + kr).astype(BF16)


def _mla_proj_body(pos_ref, sorted_hbm, cos_ref, sin_ref, cost_ref, sint_ref, kvn_ref, wdkv_ref, kvan_ref, krg_ref,
                   wuk_ref, wuv_ref, wuvt_ref, kg_ref, nmq_ref, wdq_ref, qan_ref, wuqt_ref, qg_ref, wukt_ref,
                   rows_ref, ckvp_ref, ckvd_ref, krp_ref, krd_ref, k_ref, vdec_ref, vt_ref, qt_ref, qdec_ref,
                   xbuf, xsem):
    is_dec = pl.program_id(0) == N_TILES - 1
    x_ref = _gather_tile(pos_ref, sorted_hbm, xbuf, xsem, TM)
    h = _load_rows(x_ref, TM)
    rows_ref[...] = h
    hn = h * lax.rsqrt(jnp.mean(h * h, axis=-1, keepdims=True) + EPS)
    c = _dot((hn * kvn_ref[...]).astype(BF16), wdkv_ref[...])
    ckv = _rms(c[:, :KV_RANK], kvan_ref[...])
    kr = c[:, KV_RANK:]
    kr = kr * lax.rsqrt(jnp.sum(kr * kr, axis=-1, keepdims=True) * (1.0 / ROPE_DIM) + EPS) * krg_ref[...]
    kr = kr * cos_ref[...] + _rope_swap(kr) * sin_ref[...]
    cb = ckv.astype(BF16)
    cbt = ckv.T.astype(BF16)
    _expand_k(cb, cbt, kr, wuk_ref, wukt_ref, kg_ref, k_ref)
    vt = _dot(wuvt_ref[...], cbt).astype(BF16)
    for pr in range(N_HEADS // 2):
        vt_ref[0, pr * VT_ROWS:pr * VT_ROWS + LANES, :] = vt[pr * LANES:(pr + 1) * LANES]
        vt_ref[0, pr * VT_ROWS + LANES:(pr + 1) * VT_ROWS, :] = jnp.ones((VT_ROWS - LANES, TM), BF16)
    cq = _rms(_dot((hn * nmq_ref[...]).astype(BF16), wdq_ref[...]), qan_ref[...])
    qt = _dot(wuqt_ref[...], cq.T.astype(BF16))
    cost, sint, qg = cost_ref[0], sint_ref[0], qg_ref[...]
    for hh in range(N_HEADS):
        x = qt[hh * HEAD_W:(hh + 1) * HEAD_W, :]
        xn, xr = x[:NOPE_DIM], x[ROPE_LO:ROPE_LO + ROPE_DIM]
        xn = xn * lax.rsqrt(jnp.mean(xn * xn, axis=0, keepdims=True) + EPS) * qg[:NOPE_DIM]
        xr = xr * lax.rsqrt(jnp.mean(xr * xr, axis=0, keepdims=True) + EPS) * qg[ROPE_LO:ROPE_LO + ROPE_DIM]
        x1, x2 = xr[:ROPE_HALF], xr[ROPE_HALF:]
        qh = jnp.concatenate([xn, x1 * cost - x2 * sint, x1 * sint + x2 * cost,
                              jnp.zeros((HEAD_W - NOPE_DIM - ROPE_DIM, TM), F32)], axis=0) * Q_SCALE
        qt_ref[0, hh * HEAD_W:(hh + 1) * HEAD_W, :] = qh.astype(BF16)

    @pl.when(jnp.logical_not(is_dec))
    def _():
        ckvp_ref[...] = ckv
        krp_ref[...] = kr[:, ROPE_LO:ROPE_LO + ROPE_DIM]

    @pl.when(is_dec)
    def _():
        ckvd_ref[...] = ckv
        krd_ref[...] = kr[:, ROPE_LO:ROPE_LO + ROPE_DIM]
        vdec_ref[...] = _dot(cb, wuv_ref[...]).astype(BF16)
        for hh in range(N_HEADS):
            rows = slice(hh * HEAD_W, (hh + 1) * HEAD_W)
            qdec_ref[:, rows] = qt_ref[0, rows, :].astype(F32).T.astype(BF16)


def _mla_proj(pos, sorted_rows, cos_t, sin_t, cos_tt, sin_tt, kvn, wdkv, kvan, krg, wuk, wukt, wuv, kg, nmq, wdq, qan, wuq, qg):
    const = lambda *shape: pl.BlockSpec(shape, lambda i, p: (0,) * len(shape))
    tab_tile = lambda i: jnp.where(i < N_PROMPT // TM, i % (SEQ // TM), SEQ // TM)
    tab = pl.BlockSpec((TM, LANES), lambda i, p: (tab_tile(i), 0))
    tab_t = pl.BlockSpec((1, ROPE_HALF, TM), lambda i, p: (tab_tile(i), 0, 0))
    row = lambda w: pl.BlockSpec((TM, w), lambda i, p: (i, 0))
    prow = lambda w: pl.BlockSpec((TM, w), lambda i, p: (jnp.minimum(i, N_TILES - 2), 0))
    return pl.pallas_call(
        _mla_proj_body,
        grid_spec=pltpu.PrefetchScalarGridSpec(
            num_scalar_prefetch=1,
            grid=(N_TILES,),
            in_specs=[
                pl.BlockSpec(memory_space=pl.ANY), tab, tab, tab_t, tab_t,
                const(1, D_MODEL), const(D_MODEL, KV_RANK + LANES), const(1, KV_RANK), const(1, LANES),
                const(KV_RANK, N_HEADS * HEAD_W), const(KV_RANK, N_HEADS * V_DIM), const(N_HEADS * V_DIM, KV_RANK),
                const(1, LANES),
                const(1, D_MODEL), const(D_MODEL, Q_RANK), const(1, Q_RANK), const(N_HEADS * HEAD_W, Q_RANK),
                const(HEAD_W, TM), const(N_HEADS * NOPE_DIM, KV_RANK),
            ],
            out_specs=[row(D_MODEL),
                       prow(KV_RANK), const(N_DEC, KV_RANK), prow(ROPE_DIM), const(N_DEC, ROPE_DIM),
                       row(N_HEADS * HEAD_W), const(N_DEC, N_HEADS * V_DIM),
                       pl.BlockSpec((1, N_HEADS // 2 * VT_ROWS, TM), lambda i, p: (i, 0, 0)),
                       pl.BlockSpec((1, N_HEADS * HEAD_W, TM), lambda i, p: (i, 0, 0)),
                       const(N_DEC, N_HEADS * HEAD_W)],
            scratch_shapes=_gather_scratch(TM),
        ),
        out_shape=[
            jax.ShapeDtypeStruct((T, D_MODEL), F32),
            jax.ShapeDtypeStruct((N_PROMPT, KV_RANK), F32), jax.ShapeDtypeStruct((N_DEC, KV_RANK), F32),
            jax.ShapeDtypeStruct((N_PROMPT, ROPE_DIM), F32), jax.ShapeDtypeStruct((N_DEC, ROPE_DIM), F32),
            jax.ShapeDtypeStruct((T, N_HEADS * HEAD_W), BF16), jax.ShapeDtypeStruct((N_DEC, N_HEADS * V_DIM), BF16),
            jax.ShapeDtypeStruct((N_TILES, N_HEADS // 2 * VT_ROWS, TM), BF16),
            jax.ShapeDtypeStruct((N_TILES, N_HEADS * HEAD_W, TM), BF16),
            jax.ShapeDtypeStruct((N_DEC, N_HEADS * HEAD_W), BF16),
        ],
        compiler_params=_cparams(),
        name="mla_proj",
    )(pos, sorted_rows, cos_t, sin_t, cos_tt, sin_tt, kvn, wdkv, kvan, krg, wuk, wuv, wuv.T, kg, nmq, wdq, qan,
      wuq.T, jnp.broadcast_to(qg.reshape(HEAD_W, 1), (HEAD_W, TM)), wukt)


def _cache_kv_body(ckv_ref, kr_ref, place_ref, wuk_ref, wuv_ref, kg_ref, wukt_ref, k_ref, v_ref):
    kr = _dot(kr_ref[...].astype(BF16), place_ref[...])
    cb = ckv_ref[...].astype(BF16)
    _expand_k(cb, ckv_ref[...].T.astype(BF16), kr, wuk_ref, wukt_ref, kg_ref, k_ref)
    v_ref[...] = _dot(cb, wuv_ref[...]).astype(BF16)


CACHE_ROWS = 1024
assert PAST_LEN % CACHE_ROWS == 0


def _cache_kv(ckv, kr, wuk, wukt, wuv, kg):
    n = ckv.shape[0]
    place = jnp.asarray(np.arange(ROPE_DIM)[:, None] + ROPE_LO == np.arange(LANES)[None, :], BF16)
    const = lambda *shape: pl.BlockSpec(shape, lambda i: (0,) * len(shape))
    row = lambda w: pl.BlockSpec((CACHE_ROWS, w), lambda i: (i, 0))
    return pl.pallas_call(
        _cache_kv_body,
        grid=(n // CACHE_ROWS,),
        in_specs=[row(KV_RANK), row(ROPE_DIM), const(ROPE_DIM, LANES), const(KV_RANK, N_HEADS * HEAD_W),
                  const(KV_RANK, N_HEADS * V_DIM), const(1, LANES), const(N_HEADS * NOPE_DIM, KV_RANK)],
        out_specs=[row(N_HEADS * HEAD_W), row(N_HEADS * V_DIM)],
        out_shape=[jax.ShapeDtypeStruct((n, N_HEADS * HEAD_W), BF16), jax.ShapeDtypeStruct((n, N_HEADS * V_DIM), BF16)],
        compiler_params=_cparams(),
        name="cache_kv",
    )(ckv, kr, place, wuk, wuv, kg, wukt)


TQ = 256
TK = 256
assert TQ == TK and TQ % CHUNK == 0
SCORE_LOOKAHEAD = 8


def _qk(q, k):
    return lax.dot_general(q, k, (((1,), (1,)), ((), ())), preferred_element_type=F32)


def _merge_heads(o_ref, outs, rows):
    lane = lax.broadcasted_iota(jnp.int32, (rows, LANES), 1)
    for pr in range(N_HEADS // 2):
        o_ref[:, pr * LANES:(pr + 1) * LANES] = jnp.where(lane < V_DIM, outs[2 * pr], outs[2 * pr + 1]).astype(BF16)


def _prompt_attn_body(qt_ref, k_ref, vt_ref, rows_ref, wo_ref, out_ref, m_scr, acc_scr):
    qi = pl.program_id(1)
    m_scr[...] = jnp.full(m_scr.shape, NEG, F32)
    acc_scr[...] = jnp.zeros(acc_scr.shape, F32)

    def all_heads(j, mask):
        ks = pl.ds(pl.multiple_of(j * TK, TK), TK)

        def scores(hh):
            hcols = slice(hh * HEAD_W, (hh + 1) * HEAD_W)
            return _dot(k_ref[ks, hcols], qt_ref[0, hcols, :])

        ahead = [scores(hh) for hh in range(SCORE_LOOKAHEAD)]
        for hh in range(N_HEADS):
            vrows = slice((hh // 2) * VT_ROWS, (hh // 2 + 1) * VT_ROWS)
            s = ahead.pop(0)
            if hh + SCORE_LOOKAHEAD < N_HEADS:
                ahead.append(scores(hh + SCORE_LOOKAHEAD))
            if mask is not None:
                s = jnp.where(mask, s, NEG)
            m_old = m_scr[hh]
            m_new = jnp.maximum(m_old, jnp.max(s, axis=0, keepdims=True))
            p = jnp.exp2(s - m_new).astype(BF16)
            m_scr[hh] = m_new
            acc_scr[hh] = jnp.exp2(m_old - m_new) * acc_scr[hh] + _dot(vt_ref[j, vrows, :], p)

    def step(j, carry):
        all_heads(j, None)
        return carry

    lax.fori_loop(0, qi, step, 0)
    kc = lax.broadcasted_iota(jnp.int32, (TK, TQ), 0) // CHUNK
    qc = lax.broadcasted_iota(jnp.int32, (TK, TQ), 1) // CHUNK
    all_heads(qi, kc <= qc)
    row = lax.broadcasted_iota(jnp.int32, (LANES, TQ), 0)
    pairs = []
    for pr in range(N_HEADS // 2):
        even = acc_scr[2 * pr, :LANES] / acc_scr[2 * pr, LANES:LANES + 1]
        odd = acc_scr[2 * pr + 1, :LANES] / acc_scr[2 * pr + 1, LANES:LANES + 1]
        pairs.append(jnp.where(row < V_DIM, even, odd))
    o = jnp.concatenate(pairs, axis=0).T.astype(BF16)
    out_ref[...] = rows_ref[...] + _dot(o, wo_ref[...])


def _prompt_attn(qt, k, vt, rows, wo):
    nq = SEQ // TQ
    return pl.pallas_call(
        _prompt_attn_body,
        grid=(BATCH, nq),
        in_specs=[
            pl.BlockSpec((1, N_HEADS * HEAD_W, TQ), lambda b, i: (b * nq + i, 0, 0)),
            pl.BlockSpec((SEQ, N_HEADS * HEAD_W), lambda b, i: (b, 0)),
            pl.BlockSpec((SEQ // TK, N_HEADS // 2 * VT_ROWS, TK), lambda b, i: (b, 0, 0)),
            pl.BlockSpec((TQ, D_MODEL), lambda b, i: (b * nq + i, 0)),
            pl.BlockSpec((N_HEADS * V_DIM, D_MODEL), lambda b, i: (0, 0)),
        ],
        out_specs=pl.BlockSpec((TQ, D_MODEL), lambda b, i: (b * nq + i, 0)),
        out_shape=jax.ShapeDtypeStruct((N_PROMPT, D_MODEL), F32),
        scratch_shapes=[pltpu.VMEM((N_HEADS, 1, TQ), F32), pltpu.VMEM((N_HEADS, VT_ROWS, TQ), F32)],
        compiler_params=_cparams(2),
        name="prompt_attn",
    )(qt, k, vt, rows, wo)


def _sample_attn_body(q_ref, kc_ref, vc_ref, kn_ref, vn_ref, rows_ref, wo_ref, out_ref, o_scr):
    outs = []
    for hh in range(N_HEADS):
        hcols = slice(hh * HEAD_W, (hh + 1) * HEAD_W)
        vcols = slice((hh // 2) * LANES, (hh // 2 + 1) * LANES)
        q = q_ref[:, hcols]
        sc = _qk(q, kc_ref[:, hcols])
        sn = _qk(q, kn_ref[:, hcols])
        m = jnp.maximum(jnp.max(sc, axis=-1, keepdims=True), jnp.max(sn, axis=-1, keepdims=True))
        pc = jnp.exp2(sc - m)
        pn = jnp.exp2(sn - m)
        l = jnp.sum(pc, axis=-1, keepdims=True) + jnp.sum(pn, axis=-1, keepdims=True)
        acc = _dot(pc.astype(BF16), vc_ref[:, vcols]) + _dot(pn.astype(BF16), vn_ref[:, vcols])
        outs.append(acc / l)
    _merge_heads(o_scr, outs, DEC_SEQ)
    out_ref[...] = rows_ref[...] + _dot(o_scr[...], wo_ref[...])


def _sample_attn(q, kc, vc, kn, vn, rows, wo):
    off = N_PROMPT // DEC_SEQ
    return pl.pallas_call(
        _sample_attn_body,
        grid=(DEC_BATCH,),
        in_specs=[
            pl.BlockSpec((DEC_SEQ, N_HEADS * HEAD_W), lambda b: (b, 0)),
            pl.BlockSpec((PAST_LEN, N_HEADS * HEAD_W), lambda b: (b, 0)),
            pl.BlockSpec((PAST_LEN, N_HEADS * V_DIM), lambda b: (b, 0)),
            pl.BlockSpec((DEC_SEQ, N_HEADS * HEAD_W), lambda b: (off + b, 0)),
            pl.BlockSpec((DEC_SEQ, N_HEADS * V_DIM), lambda b: (b, 0)),
            pl.BlockSpec((DEC_SEQ, D_MODEL), lambda b: (off + b, 0)),
            pl.BlockSpec((N_HEADS * V_DIM, D_MODEL), lambda b: (0, 0)),
        ],
        out_specs=pl.BlockSpec((DEC_SEQ, D_MODEL), lambda b: (b, 0)),
        out_shape=jax.ShapeDtypeStruct((N_DEC, D_MODEL), F32),
        scratch_shapes=[pltpu.VMEM((DEC_SEQ, N_HEADS * V_DIM), BF16)],
        compiler_params=_cparams(),
        name="sample_attn",
    )(q, kc, vc, kn, vn, rows, wo)


def _finish_body(pos_ref, sorted_hbm, yp_ref, ys_ref, xbuf, xsem):
    i = pl.program_id(0)
    y = _load_rows(_gather_tile(pos_ref, sorted_hbm, xbuf, xsem, TM), TM)

    @pl.when(i < N_TILES - 1)
    def _():
        yp_ref[...] = y

    @pl.when(i == N_TILES - 1)
    def _():
        ys_ref[...] = y


def _finish(pos, sorted_rows):
    return pl.pallas_call(
        _finish_body,
        grid_spec=pltpu.PrefetchScalarGridSpec(
            num_scalar_prefetch=1,
            grid=(N_TILES,),
            in_specs=[pl.BlockSpec(memory_space=pl.ANY)],
            out_specs=[pl.BlockSpec((TM, D_MODEL), lambda i, p: (jnp.minimum(i, N_TILES - 2), 0)),
                       pl.BlockSpec((N_DEC, D_MODEL), lambda i, p: (0, 0))],
            scratch_shapes=_gather_scratch(TM),
        ),
        out_shape=[jax.ShapeDtypeStruct((N_PROMPT, D_MODEL), F32), jax.ShapeDtypeStruct((N_DEC, D_MODEL), F32)],
        compiler_params=_cparams(),
        name="finish",
    )(pos, sorted_rows)


def _rope_tables():
    half = ROPE_DIM // 2
    inv_freq = ROPE_THETA ** (-jnp.arange(half, dtype=F32) / half)
    dec_pos = PAST_LEN + jnp.tile(jnp.arange(DEC_SEQ, dtype=jnp.int32), DEC_BATCH)
    pos = jnp.concatenate([jnp.arange(SEQ, dtype=jnp.int32), dec_pos])
    ang = pos.astype(F32)[:, None] * inv_freq[None, :]
    cos, sin = jnp.cos(ang), jnp.sin(ang)
    n = pos.shape[0]
    cos_t = jnp.ones((n, LANES), F32).at[:, ROPE_LO:ROPE_LO + ROPE_DIM].set(jnp.concatenate([cos, cos], axis=1))
    sin_t = jnp.zeros((n, LANES), F32).at[:, ROPE_LO:ROPE_LO + ROPE_DIM].set(jnp.concatenate([-sin, sin], axis=1))
    to_tiles = lambda a: a.reshape(n // TM, TM, half).transpose(0, 2, 1)
    return cos_t, sin_t, to_tiles(cos), to_tiles(sin)


def _on_lanes(vec, lo):
    return jnp.zeros((1, LANES), F32).at[0, lo:lo + vec.shape[0]].set(vec)


def kernel(x_prompt, x_sample, cache_ckv, cache_krope, norm_mix, norm_ffn, gm_w_in, gm_b_in, gm_ln_g, gm_ln_b, gm_w_s, gm_b_s, gm_w_out, gm_b_out, kv_norm, w_dkv, kv_a_norm, k_rope_norm, w_uk, w_uv, k_nope_norm, w_dq, q_a_norm, w_uq, q_nope_norm, q_rope_norm, w_o, moe_w_group, moe_b_group, moe_w_expert, moe_b_expert, moe_w1, moe_w3, moe_w2):
    nf0, nf1 = norm_ffn[0].reshape(1, D_MODEL), norm_ffn[1].reshape(1, D_MODEL)

    idx = np.arange(GMLP_BLOCK)
    allowed = (idx[None, :] // CHUNK) <= (idx[:, None] // CHUNK)
    ws_p = jnp.where(allowed[None], gm_w_s[0], 0.0).astype(BF16)
    same_seq = (idx[None, :] // DEC_SEQ) == (idx[:, None] // DEC_SEQ)
    ws_d = jnp.where(same_seq[None], jnp.tile(gm_w_s[0][:, :DEC_SEQ, :DEC_SEQ], (1, GMLP_BLOCK // DEC_SEQ, GMLP_BLOCK // DEC_SEQ)), 0.0).astype(BF16)
    bs_p = gm_b_s[0][:, :, None]
    bs_d = jnp.tile(gm_b_s[0][:, :DEC_SEQ], (1, GMLP_BLOCK // DEC_SEQ))[:, :, None]
    rows, v_rows, info, cnt = _gmlp_layer(
        x_prompt.reshape(N_PROMPT, D_MODEL), x_sample.reshape(N_DEC, D_MODEL),
        norm_mix[0].reshape(1, -1), gm_w_in[0].astype(BF16), gm_b_in[0].reshape(1, -1),
        gm_ln_g[0].reshape(1, -1), gm_ln_b[0].reshape(1, -1), ws_p, ws_d, bs_p, bs_d,
        gm_w_out[0].astype(BF16), gm_b_out[0].reshape(1, -1),
        nf0, *_router_weights(0, moe_w_group, moe_b_group, moe_w_expert, moe_b_expert))
    sorted_rows, pos = _moe_layer(0, rows, info, cnt, nf0, moe_w1, moe_w3, moe_w2)

    cos_t, sin_t, cos_tt, sin_tt = _rope_tables()
    wdkv = jnp.zeros((D_MODEL, KV_RANK + LANES), F32).at[:, :KV_RANK].set(w_dkv[:, :KV_RANK])
    wdkv = wdkv.at[:, KV_RANK + ROPE_LO:KV_RANK + ROPE_LO + ROPE_DIM].set(w_dkv[:, KV_RANK:]).astype(BF16)
    wuk = jnp.zeros((KV_RANK, N_HEADS, HEAD_W), F32).at[:, :, :NOPE_DIM].set(w_uk).reshape(KV_RANK, -1).astype(BF16)
    wukt = w_uk.reshape(KV_RANK, -1).T.astype(BF16)
    wuv = w_uv.reshape(KV_RANK, -1).astype(BF16)
    wuq =jnp.zeros((Q_RANK, N_HEADS, HEAD_W), F32).at[:, :, :NOPE_DIM + ROPE_DIM].set(w_uq[0]).reshape(Q_RANK, -1).astype(BF16)
    kg = _on_lanes(k_nope_norm, 0)
    krg = _on_lanes(k_rope_norm, ROPE_LO)
    qg = _on_lanes(jnp.concatenate([q_nope_norm[0], q_rope_norm[0]]), 0)
    rows, ckv_p, ckv_d, krope_p, krope_d, k_new, v_dec, vt_new, qt, q_dec = _mla_proj(
        pos, sorted_rows, cos_t, sin_t, cos_tt, sin_tt, kv_norm.reshape(1, -1), wdkv, kv_a_norm.reshape(1, -1), krg, wuk, wukt, wuv, kg,
        norm_mix[1].reshape(1, -1), w_dq[0].astype(BF16), q_a_norm[0].reshape(1, -1), wuq, qg)
    k_cache, v_cache = _cache_kv(cache_ckv.reshape(-1, KV_RANK), cache_krope.reshape(-1, ROPE_DIM), wuk, wukt, wuv, kg)

    wo = w_o[0].astype(BF16)
    h_prompt = _prompt_attn(qt, k_new, vt_new, rows, wo)
    h_dec = _sample_attn(q_dec, k_cache, v_cache, k_new, v_dec, rows, wo)
    rows, info, cnt = _router(h_prompt, h_dec, nf1,
                              *_router_weights(1, moe_w_group, moe_b_group, moe_w_expert, moe_b_expert))
    sorted_rows, pos = _moe_layer(1, rows, info, cnt, nf1, moe_w1, moe_w3, moe_w2)
    y_prompt, y_sample = _finish(pos, sorted_rows)

    return (y_prompt.reshape(BATCH, SEQ, D_MODEL), y_sample.reshape(DEC_BATCH, DEC_SEQ, D_MODEL),
            ckv_p.reshape(BATCH, SEQ, KV_RANK), krope_p.reshape(BATCH, SEQ, ROPE_DIM),
            ckv_d.reshape(DEC_BATCH, DEC_SEQ, KV_RANK), krope_d.reshape(DEC_BATCH, DEC_SEQ, ROPE_DIM),
            v_rows.reshape(1, DEC_BATCH, DEC_SEQ, D_GATE))
```

```python
import functools

import jax
import jax.numpy as jnp
import numpy as np
from jax import lax
from jax.experimental import pallas as pl
from jax.experimental.pallas import tpu as pltpu

F32 = jnp.float32
BF16 = jnp.bfloat16

D_MODEL = 1024
BATCH = 8
SEQ = 2048
DEC_BATCH = 16
DEC_SEQ = 16
PAST_LEN = 2048
CHUNK = 64
GMLP_BLOCK = 128
D_GATE = 2 * D_MODEL
N_SG = 8
SG_W = D_GATE // N_SG
N_HEADS = 8
NOPE_DIM = 64
ROPE_DIM = 32
V_DIM = 64
Q_RANK = 384
KV_RANK = 256
ROPE_THETA = 10000.0
SCALE = (NOPE_DIM + ROPE_DIM) ** -0.5
Q_SCALE = SCALE * float(np.log2(np.e))
N_EGROUPS = 4
EXPERTS_PER_GROUP = 4
N_EXPERTS = N_EGROUPS * EXPERTS_PER_GROUP
D_EXPERT = 512
EPS = 1e-6
NEG = -1e30

LANES = 128
SUBLANES = 8
ROW_TILES = D_MODEL // LANES
assert ROW_TILES == SUBLANES

N_PROMPT = BATCH * SEQ
N_DEC = DEC_BATCH * DEC_SEQ
T = N_PROMPT + N_DEC
TM = 256
assert N_PROMPT % TM == 0 and N_DEC == TM
N_TILES = T // TM
HEAD_W = LANES
ROPE_LO = NOPE_DIM
ROPE_HALF = ROPE_DIM // 2

PAIR_A = (0, 0, 0, 1, 1, 3)
PAIR_B = (1, 2, 3, 3, 2, 2)
N_PAIRS = 6
N_BUCKETS = N_EGROUPS * N_PAIRS
BUCKET_ROWS = 32
assert N_BUCKETS <= BUCKET_ROWS and BUCKET_ROWS % SUBLANES == 0
MOE_TILES = (T + N_BUCKETS * (TM - 1) + TM - 1) // TM
P_ROWS = MOE_TILES * TM

VMEM_LIMIT = 56 * 1024 * 1024


def _cparams(n_axes=1, vmem=VMEM_LIMIT):
    return pltpu.CompilerParams(dimension_semantics=("arbitrary",) * n_axes, vmem_limit_bytes=vmem)


def _rms(x, g):
    return x * lax.rsqrt(jnp.mean(x * x, axis=-1, keepdims=True) + EPS) * g


def _load_rows(ref, n):
    return jnp.concatenate([ref[pl.ds(s, n, stride=ROW_TILES), :] for s in range(ROW_TILES)], axis=1)


def _store_rows(ref, x, n):
    for s in range(ROW_TILES):
        ref[pl.ds(s, n, stride=ROW_TILES), :] = x[:, s * LANES:(s + 1) * LANES]


def _dot(a, b):
    return jnp.dot(a, b, preferred_element_type=F32)


GELU_K1 = float(-2.0 * np.sqrt(2.0 / np.pi) * np.log2(np.e))
GELU_K3 = GELU_K1 * 0.044715


def _gmlp_body(xp_ref, xd_ref, nm_ref, win_ref, bin_ref, lng_ref, lnb_ref, wsp_ref, wsd_ref, bsp_ref, bsd_ref,
               wout_ref, bout_ref, nf_ref, wrh_ref, wr2_ref, br_ref,
               rows_ref, v_ref, info_ref, cnt_ref, gated_ref, carry_ref):
    i = pl.program_id(0)
    is_dec = i == N_TILES - 1
    _route_init(carry_ref)
    x = jnp.where(is_dec, xd_ref[...], xp_ref[...])
    xn = _rms(x, nm_ref[...]).astype(BF16)
    z = _dot(xn, win_ref[...]) + bin_ref[...]
    z = z / (1.0 + jnp.exp2(z * (GELU_K1 + GELU_K3 * (z * z))))
    u = z[:, :D_GATE]
    v = z[:, D_GATE:]
    mu = jnp.mean(v, axis=-1, keepdims=True)
    vc = v - mu
    var = jnp.mean(vc * vc, axis=-1, keepdims=True)
    v = vc * lax.rsqrt(var + EPS) * lng_ref[...] + lnb_ref[...]

    v_ref[...] = v
    vb = v.astype(BF16)
    for g in range(N_SG):
        ws = jnp.where(is_dec, wsd_ref[g], wsp_ref[g])
        bs = jnp.where(is_dec, bsd_ref[g], bsp_ref[g])
        for b in range(TM // GMLP_BLOCK):
            rows = slice(b * GMLP_BLOCK, (b + 1) * GMLP_BLOCK)
            cols = slice(g * SG_W, (g + 1) * SG_W)
            s = _dot(ws, vb[rows, cols]) + bs
            gated_ref[rows, cols] = (u[rows, cols] * s).astype(BF16)
    h = x + _dot(gated_ref[...], wout_ref[...]) + bout_ref[...]
    _store_rows(rows_ref, h, TM)
    _route_tile(h, nf_ref, wrh_ref, wr2_ref, br_ref, info_ref, cnt_ref, carry_ref)


def _gmlp_layer(x_prompt, x_dec, nm, w_in, b_in, ln_g, ln_b, ws_p, ws_d, bs_p, bs_d, w_out, b_out, nf, wr, br):
    const = lambda *shape: pl.BlockSpec(shape, lambda i: (0,) * len(shape))
    return pl.pallas_call(
        _gmlp_body,
        grid=(N_TILES,),
        in_specs=[
            pl.BlockSpec((TM, D_MODEL), lambda i: (jnp.minimum(i, N_TILES - 2), 0)), const(N_DEC, D_MODEL),
            const(1, D_MODEL), const(D_MODEL, 2 * D_GATE), const(1, 2 * D_GATE),
            const(1, D_GATE), const(1, D_GATE),
            const(N_SG, GMLP_BLOCK, GMLP_BLOCK), const(N_SG, GMLP_BLOCK, GMLP_BLOCK),
            const(N_SG, GMLP_BLOCK, 1), const(N_SG, GMLP_BLOCK, 1),
            const(D_GATE, D_MODEL), const(1, D_MODEL),
            const(1, D_MODEL), const(D_MODEL, LANES), const(D_MODEL, 2 * LANES), const(1, LANES),
        ],
        out_specs=[
            pl.BlockSpec((TM * ROW_TILES, LANES), lambda i: (i, 0)),
            const(N_DEC, D_GATE),
            pl.BlockSpec((SUBLANES, TM), lambda i: (0, i)),
            const(BUCKET_ROWS, LANES),
        ],
        out_shape=[jax.ShapeDtypeStruct((T * ROW_TILES, LANES), F32), jax.ShapeDtypeStruct((N_DEC, D_GATE), F32),
                   jax.ShapeDtypeStruct((SUBLANES, T), F32), jax.ShapeDtypeStruct((BUCKET_ROWS, LANES), F32)],
        scratch_shapes=[pltpu.VMEM((TM, D_GATE), BF16), pltpu.VMEM((BUCKET_ROWS, LANES), F32)],
        compiler_params=_cparams(),
        name="gmlp_layer",
    )(x_prompt, x_dec, nm, w_in, b_in, ln_g, ln_b, ws_p, ws_d, bs_p, bs_d, w_out, b_out,
      nf, *_router_split(wr), br)


def _router_body(hp_ref, hd_ref, nf_ref, wrh_ref, wr2_ref, br_ref, rows_ref, info_ref, cnt_ref, carry_ref):
    _route_init(carry_ref)
    h = jnp.where(pl.program_id(0) == N_TILES - 1, hd_ref[...], hp_ref[...])
    _store_rows(rows_ref, h, TM)
    _route_tile(h, nf_ref, wrh_ref, wr2_ref, br_ref, info_ref, cnt_ref, carry_ref)


def _router_split(wr):
    hi = wr.astype(BF16)
    return hi, jnp.concatenate([hi, (wr - hi.astype(F32)).astype(BF16)], axis=1)


def _route_init(carry_ref):
    @pl.when(pl.program_id(0) == 0)
    def _():
        carry_ref[...] = jnp.zeros_like(carry_ref)


def _route_tile(h, nf_ref, wrh_ref, wr2_ref, br_ref, info_ref, cnt_ref, carry_ref):
    xn = _rms(h, nf_ref[...])
    xh = xn.astype(BF16)
    xl = (xn - xh.astype(F32)).astype(BF16)
    both = _dot(xh, wr2_ref[...])
    logits = both[:, :LANES] + (_dot(xl, wrh_ref[...]) + both[:, LANES:]) + br_ref[...]
    lt = logits.T

    def first_max(vals, ids):
        vmax = jnp.max(vals, axis=0, keepdims=True)
        return vmax, jnp.min(jnp.where(vals == vmax, ids, float(LANES)), axis=0, keepdims=True)

    lg = lt[N_EXPERTS:N_EXPERTS + N_EGROUPS]
    gmax, g_idx = first_max(lg, lax.broadcasted_iota(jnp.int32, lg.shape, 0).astype(F32))
    g_p = 1.0 / jnp.sum(jnp.exp(lg - gmax), axis=0, keepdims=True)
    e_id = lax.broadcasted_iota(jnp.int32, (N_EXPERTS, TM), 0)
    in_group = (e_id // EXPERTS_PER_GROUP).astype(F32) == g_idx
    e_id = e_id.astype(F32)
    le = jnp.where(in_group, lt[:N_EXPERTS], -jnp.inf)
    v1, i1 = first_max(le, e_id)
    v2, i2 = first_max(jnp.where(e_id == i1, -jnp.inf, le), e_id)
    e2 = jnp.exp(v2 - v1)
    w1 = (1.0 / (1.0 + e2)) * g_p
    w2 = (e2 / (1.0 + e2)) * g_p
    a1 = i1 - EXPERTS_PER_GROUP * g_idx
    a2 = i2 - EXPERTS_PER_GROUP * g_idx
    lo = jnp.minimum(a1, a2)
    hi = jnp.maximum(a1, a2)
    pair = jnp.where(lo == 0.0, hi - 1.0, jnp.where(lo == 1.0, jnp.where(hi == 3.0, 3.0, 4.0), 5.0))
    ea = jnp.where(pair < 3.0, 0.0, jnp.where(pair < 5.0, 1.0, 3.0))
    ga = jnp.where(a1 == ea, w1, w2)
    gb = jnp.where(a1 == ea, w2, w1)
    bucket = g_idx * N_PAIRS + pair

    onehot = (lax.broadcasted_iota(jnp.int32, (BUCKET_ROWS, TM), 0).astype(F32) == bucket).astype(F32)
    r = lax.broadcasted_iota(jnp.int32, (TM, TM), 0)
    c = lax.broadcasted_iota(jnp.int32, (TM, TM), 1)
    before = _dot(onehot.astype(BF16), (r < c).astype(BF16))
    carry = carry_ref[:, 0:1]
    rank = jnp.sum(onehot * (before + carry), axis=0, keepdims=True)
    new_carry = carry + jnp.sum(onehot, axis=1, keepdims=True)
    carry_ref[...] = jnp.broadcast_to(new_carry, carry_ref.shape)
    cnt_ref[...] = jnp.broadcast_to(new_carry, cnt_ref.shape)
    info_ref[...] = jnp.concatenate([bucket, rank, ga, gb, jnp.zeros((SUBLANES - 4, TM), F32)], axis=0)


def _router(h_prompt, h_dec, nf, wr, br):
    const = lambda *shape: pl.BlockSpec(shape, lambda i: (0,) * len(shape))
    return pl.pallas_call(
        _router_body,
        grid=(N_TILES,),
        in_specs=[pl.BlockSpec((TM, D_MODEL), lambda i: (jnp.minimum(i, N_TILES - 2), 0)), const(N_DEC, D_MODEL),
                  const(1, D_MODEL), const(D_MODEL, LANES), const(D_MODEL, 2 * LANES), const(1, LANES)],
        out_specs=[pl.BlockSpec((TM * ROW_TILES, LANES), lambda i: (i, 0)),
                   pl.BlockSpec((SUBLANES, TM), lambda i: (0, i)),
                   const(BUCKET_ROWS, LANES)],
        out_shape=[jax.ShapeDtypeStruct((T * ROW_TILES, LANES), F32), jax.ShapeDtypeStruct((SUBLANES, T), F32),
                   jax.ShapeDtypeStruct((BUCKET_ROWS, LANES), F32)],
        scratch_shapes=[pltpu.VMEM((BUCKET_ROWS, LANES), F32)],
        compiler_params=_cparams(),
        name="moe_router",
    )(h_prompt, h_dec, nf, *_router_split(wr), br)


GATHER_UNROLL = 32


def _gather_rows_start(idx_ref, base, src_ref, dst_ref, sem, n):
    def group(g, carry):
        for u in range(GATHER_UNROLL):
            r = g * GATHER_UNROLL + u
            src = pl.multiple_of(idx_ref[base + r] * ROW_TILES, ROW_TILES)
            dst = pl.multiple_of(r * ROW_TILES, ROW_TILES)
            pltpu.make_async_copy(src_ref.at[pl.ds(src, ROW_TILES), :], dst_ref.at[pl.ds(dst, ROW_TILES), :],
                                  sem).start(priority=u % 2)
        return carry

    lax.fori_loop(0, n // GATHER_UNROLL, group, 0)


def _gather_rows_wait(src_ref, dst_ref, sem, n):
    pltpu.make_async_copy(src_ref.at[pl.ds(0, n * ROW_TILES), :], dst_ref.at[pl.ds(0, n * ROW_TILES), :], sem).wait()


def _gather_tile(idx_ref, src_ref, buf, sem, n, n_live=None, base_of=None, side_copy=None):
    i = pl.program_id(0)
    slot = lax.rem(i, 2)
    n_live = pl.num_programs(0) if n_live is None else n_live
    base_of = (lambda step: step * n) if base_of is None else base_of

    def start(step, s):
        _gather_rows_start(idx_ref, base_of(step), src_ref, buf.at[s], sem.at[s], n)
        if side_copy is not None:
            side_copy(step, s).start()

    @pl.when(i == 0)
    def _():
        start(0, 0)

    @pl.when(i + 1 < n_live)
    def _():
        start(i + 1, 1 - slot)

    @pl.when(i < n_live)
    def _():
        _gather_rows_wait(src_ref, buf.at[slot], sem.at[slot], n)
        if side_copy is not None:
            side_copy(i, slot).wait()

    return buf.at[slot]


def _gather_scratch(n):
    return [pltpu.VMEM((2, n * ROW_TILES, LANES), F32), pltpu.SemaphoreType.DMA((2,))]


GATE_WIN = TM + SUBLANES


def _moe_body(tok_ref, ea_ref, eb_ref, cha_ref, chb_ref, nlive_ref, first_ref, nrows_ref, rows_hbm, gates_hbm, nf_ref,
              w1a_ref, w3a_ref, w2a_ref, w1b_ref, w3b_ref, w2b_ref, out_ref,
              s1a, s3a, s2a, s1b, s3b, s2b, xbuf, xsem, gwin):
    i = pl.program_id(0)
    live = i < nlive_ref[0]

    def gate_copy(step, s):
        lo = pl.multiple_of((first_ref[step] // SUBLANES) * SUBLANES, SUBLANES)
        return pltpu.make_async_copy(gates_hbm.at[pl.ds(lo, GATE_WIN), :], gwin.at[s], xsem.at[s])

    x_ref = _gather_tile(tok_ref, rows_hbm, xbuf, xsem, TM, nlive_ref[0],
                         base_of=lambda step: first_ref[step], side_copy=gate_copy)

    @pl.when(cha_ref[i] == 1)
    def _():
        s1a[...] = w1a_ref[...].astype(BF16)
        s3a[...] = w3a_ref[...].astype(BF16)
        s2a[...] = w2a_ref[...].astype(BF16)

    @pl.when(chb_ref[i] == 1)
    def _():
        s1b[...] = w1b_ref[...].astype(BF16)
        s3b[...] = w3b_ref[...].astype(BF16)
        s2b[...] = w2b_ref[...].astype(BF16)

    @pl.when(live)
    def _():
        h = _load_rows(x_ref, TM)
        xn = _rms(h, nf_ref[...]).astype(BF16)
        g = gwin[lax.rem(i, 2), pl.ds(lax.rem(first_ref[i], SUBLANES), TM), :]
        row = lax.broadcasted_iota(jnp.int32, (TM, 1), 0)
        g = jnp.where(row < nrows_ref[i], g, 0.0)

        def ffn(w1, w3, w2):
            a = _dot(xn, w1[...])
            hdn = (a * (1.0 / (1.0 + jnp.exp(-a)))) * _dot(xn, w3[...])
            return _dot(hdn.astype(BF16), w2[...])

        y = g[:, 0:1] * ffn(s1a, s3a, s2a) + g[:, 1:2] * ffn(s1b, s3b, s2b)
        _store_rows(out_ref, h + y, TM)

    @pl.when(jnp.logical_not(live))
    def _():
        out_ref[...] = jnp.zeros_like(out_ref)


def _moe_ffn(layer, tok, ea, eb, cha, chb, n_live, first, nrows, rows, gates, nf, w1, w3, w2):
    wa = lambda shape: pl.BlockSpec((None, None) + shape, lambda i, tk, ea, eb, *_: (layer, ea[i], 0, 0))
    wb = lambda shape: pl.BlockSpec((None, None) + shape, lambda i, tk, ea, eb, *_: (layer, eb[i], 0, 0))
    up, down = (D_MODEL, D_EXPERT), (D_EXPERT, D_MODEL)
    return pl.pallas_call(
        _moe_body,
        grid_spec=pltpu.PrefetchScalarGridSpec(
            num_scalar_prefetch=8,
            grid=(MOE_TILES,),
            in_specs=[
                pl.BlockSpec(memory_space=pl.ANY),
                pl.BlockSpec(memory_space=pl.ANY),
                pl.BlockSpec((1, D_MODEL), lambda i, *_: (0, 0)),
                wa(up), wa(up), wa(down), wb(up), wb(up), wb(down),
            ],
            out_specs=pl.BlockSpec((TM * ROW_TILES, LANES), lambda i, *_: (i, 0)),
            scratch_shapes=[pltpu.VMEM(up, BF16), pltpu.VMEM(up, BF16), pltpu.VMEM(down, BF16),
                            pltpu.VMEM(up, BF16), pltpu.VMEM(up, BF16), pltpu.VMEM(down, BF16)]
            + _gather_scratch(TM) + [pltpu.VMEM((2, GATE_WIN, 2), F32)],
        ),
        out_shape=jax.ShapeDtypeStruct((P_ROWS * ROW_TILES, LANES), F32),
        compiler_params=_cparams(),
        name="moe_ffn",
    )(tok, ea, eb, cha, chb, n_live, first, nrows, rows, gates, nf, w1, w3, w2, w1, w3, w2)


def _router_weights(layer, w_group, b_group, w_expert, b_expert):
    pad = LANES - N_EXPERTS - N_EGROUPS
    wr = jnp.concatenate([w_expert[layer], w_group[layer], jnp.zeros((D_MODEL, pad), F32)], axis=1)
    br = jnp.concatenate([b_expert[layer], b_group[layer], jnp.zeros((pad,), F32)]).reshape(1, LANES)
    return wr, br


def _moe_layer(layer, rows, info, cnt, nf, w1, w3, w2):
    bucket = info[0].astype(jnp.int32)
    rank = info[1].astype(jnp.int32)
    counts = cnt[:N_BUCKETS, 0].astype(jnp.int32)
    n_tiles = (counts + TM - 1) // TM
    tile_end = jnp.cumsum(n_tiles)
    tile_start = tile_end - n_tiles
    start_of = jnp.sum(jnp.where(bucket[:, None] == jnp.arange(N_BUCKETS)[None, :], tile_start[None, :], 0), axis=1)
    pos = start_of * TM + rank
    _, tok, ga, gb = lax.sort((pos, jnp.arange(T, dtype=jnp.int32), info[2], info[3]), num_keys=1)
    tok = jnp.concatenate([tok, jnp.zeros((TM,), jnp.int32)])
    gates = jnp.concatenate([jnp.stack([ga, gb], axis=1), jnp.zeros((TM + SUBLANES, 2), F32)])
    total = tile_end[-1]
    j = jnp.minimum(jnp.arange(MOE_TILES), total - 1)
    tb = jnp.sum((j[:, None] >= tile_end[None, :]).astype(jnp.int32), axis=1)
    tokens_before = jnp.cumsum(counts) - counts
    in_bucket = (j - tile_start[tb]) * TM
    first = (tokens_before[tb] + in_bucket).astype(jnp.int32)
    nrows = jnp.clip(counts[tb] - in_bucket, 0, TM).astype(jnp.int32)
    grp, pair = tb // N_PAIRS, tb % N_PAIRS
    ea = (grp * EXPERTS_PER_GROUP + jnp.asarray(PAIR_A, jnp.int32)[pair]).astype(jnp.int32)
    eb = (grp * EXPERTS_PER_GROUP + jnp.asarray(PAIR_B, jnp.int32)[pair]).astype(jnp.int32)
    tile0 = jnp.arange(MOE_TILES) == 0
    cha = (tile0 | (ea != jnp.roll(ea, 1))).astype(jnp.int32)
    chb = (tile0 | (eb != jnp.roll(eb, 1))).astype(jnp.int32)
    n_live = total.reshape(1).astype(jnp.int32)
    return _moe_ffn(layer, tok, ea, eb, cha, chb, n_live, first, nrows, rows, gates, nf, w1, w3, w2), pos


VT_ROWS = LANES + 16


def _rope_swap(x):
    lane = lax.broadcasted_iota(jnp.int32, x.shape, 1)
    return jnp.where(lane < ROPE_LO + ROPE_HALF, pltpu.roll(x, LANES - ROPE_HALF, 1), pltpu.roll(x, ROPE_HALF, 1))


def _expand_k(cb, cbt, kr, wuk_ref, wukt_ref, kg_ref, k_ref):
    n = cb.shape[0]
    kn = _dot(cb, wuk_ref[...])
    knt = _dot(wukt_ref[...], cbt)
    ms = [jnp.mean(jnp.square(knt[hh * NOPE_DIM:(hh + 1) * NOPE_DIM]), axis=0, keepdims=True)
          for hh in range(N_HEADS)]
    scale = lax.rsqrt(jnp.concatenate(ms + [jnp.ones((LANES - N_HEADS, n), F32)], axis=0) + EPS).T
    for hh in range(N_HEADS):
        cols = slice(hh * HEAD_W, (hh + 1) * HEAD_W)
        k_ref[:, cols] = (kn[:, cols] * scale[:, hh:hh + 1] * kg_ref[...] + kr).astype(BF16)


def _mla_proj_body(pos_ref, sorted_hbm, cos_ref, sin_ref, cost_ref, sint_ref, kvn_ref, wdkv_ref, kvan_ref, krg_ref,
                   wuk_ref, wuv_ref, wuvt_ref, kg_ref, nmq_ref, wdq_ref, qan_ref, wuqt_ref, qg_ref, wukt_ref,
                   rows_ref, ckvp_ref, ckvd_ref, krp_ref, krd_ref, k_ref, vdec_ref, vt_ref, qt_ref, qdec_ref,
                   xbuf, xsem):
    is_dec = pl.program_id(0) == N_TILES - 1
    x_ref = _gather_tile(pos_ref, sorted_hbm, xbuf, xsem, TM)
    h = _load_rows(x_ref, TM)
    rows_ref[...] = h
    hn = h * lax.rsqrt(jnp.mean(h * h, axis=-1, keepdims=True) + EPS)
    c = _dot((hn * kvn_ref[...]).astype(BF16), wdkv_ref[...])
    ckv = _rms(c[:, :KV_RANK], kvan_ref[...])
    kr = c[:, KV_RANK:]
    kr = kr * lax.rsqrt(jnp.sum(kr * kr, axis=-1, keepdims=True) * (1.0 / ROPE_DIM) + EPS) * krg_ref[...]
    kr = kr * cos_ref[...] + _rope_swap(kr) * sin_ref[...]
    cb = ckv.astype(BF16)
    cbt = ckv.T.astype(BF16)
    _expand_k(cb, cbt, kr, wuk_ref, wukt_ref, kg_ref, k_ref)
    vt = _dot(wuvt_ref[...], cbt).astype(BF16)
    for pr in range(N_HEADS // 2):
        vt_ref[0, pr * VT_ROWS:pr * VT_ROWS + LANES, :] = vt[pr * LANES:(pr + 1) * LANES]
        vt_ref[0, pr * VT_ROWS + LANES:(pr + 1) * VT_ROWS, :] = jnp.ones((VT_ROWS - LANES, TM), BF16)
    cq = _rms(_dot((hn * nmq_ref[...]).astype(BF16), wdq_ref[...]), qan_ref[...])
    qt = _dot(wuqt_ref[...], cq.T.astype(BF16))
    cost, sint, qg = cost_ref[0], sint_ref[0], qg_ref[...]
    for hh in range(N_HEADS):
        x = qt[hh * HEAD_W:(hh + 1) * HEAD_W, :]
        xn, xr = x[:NOPE_DIM], x[ROPE_LO:ROPE_LO + ROPE_DIM]
        xn = xn * lax.rsqrt(jnp.mean(xn * xn, axis=0, keepdims=True) + EPS) * qg[:NOPE_DIM]
        xr = xr * lax.rsqrt(jnp.mean(xr * xr, axis=0, keepdims=True) + EPS) * qg[ROPE_LO:ROPE_LO + ROPE_DIM]
        x1, x2 = xr[:ROPE_HALF], xr[ROPE_HALF:]
        qh = jnp.concatenate([xn, x1 * cost - x2 * sint, x1 * sint + x2 * cost,
                              jnp.zeros((HEAD_W - NOPE_DIM - ROPE_DIM, TM), F32)], axis=0) * Q_SCALE
        qt_ref[0, hh * HEAD_W:(hh + 1) * HEAD_W, :] = qh.astype(BF16)

    @pl.when(jnp.logical_not(is_dec))
    def _():
        ckvp_ref[...] = ckv
        krp_ref[...] = kr[:, ROPE_LO:ROPE_LO + ROPE_DIM]

    @pl.when(is_dec)
    def _():
        ckvd_ref[...] = ckv
        krd_ref[...] = kr[:, ROPE_LO:ROPE_LO + ROPE_DIM]
        vdec_ref[...] = _dot(cb, wuv_ref[...]).astype(BF16)
        for hh in range(N_HEADS):
            rows = slice(hh * HEAD_W, (hh + 1) * HEAD_W)
            qdec_ref[:, rows] = qt_ref[0, rows, :].astype(F32).T.astype(BF16)


def _mla_proj(pos, sorted_rows, cos_t, sin_t, cos_tt, sin_tt, kvn, wdkv, kvan, krg, wuk, wukt, wuv, kg, nmq, wdq, qan, wuq, qg):
    const = lambda *shape: pl.BlockSpec(shape, lambda i, p: (0,) * len(shape))
    tab_tile = lambda i: jnp.where(i < N_PROMPT // TM, i % (SEQ // TM), SEQ // TM)
    tab = pl.BlockSpec((TM, LANES), lambda i, p: (tab_tile(i), 0))
    tab_t = pl.BlockSpec((1, ROPE_HALF, TM), lambda i, p: (tab_tile(i), 0, 0))
    row = lambda w: pl.BlockSpec((TM, w), lambda i, p: (i, 0))
    prow = lambda w: pl.BlockSpec((TM, w), lambda i, p: (jnp.minimum(i, N_TILES - 2), 0))
    return pl.pallas_call(
        _mla_proj_body,
        grid_spec=pltpu.PrefetchScalarGridSpec(
            num_scalar_prefetch=1,
            grid=(N_TILES,),
            in_specs=[
                pl.BlockSpec(memory_space=pl.ANY), tab, tab, tab_t, tab_t,
                const(1, D_MODEL), const(D_MODEL, KV_RANK + LANES), const(1, KV_RANK), const(1, LANES),
                const(KV_RANK, N_HEADS * HEAD_W), const(KV_RANK, N_HEADS * V_DIM), const(N_HEADS * V_DIM, KV_RANK),
                const(1, LANES),
                const(1, D_MODEL), const(D_MODEL, Q_RANK), const(1, Q_RANK), const(N_HEADS * HEAD_W, Q_RANK),
                const(HEAD_W, TM), const(N_HEADS * NOPE_DIM, KV_RANK),
            ],
            out_specs=[row(D_MODEL),
                       prow(KV_RANK), const(N_DEC, KV_RANK), prow(ROPE_DIM), const(N_DEC, ROPE_DIM),
                       row(N_HEADS * HEAD_W), const(N_DEC, N_HEADS * V_DIM),
                       pl.BlockSpec((1, N_HEADS // 2 * VT_ROWS, TM), lambda i, p: (i, 0, 0)),
                       pl.BlockSpec((1, N_HEADS * HEAD_W, TM), lambda i, p: (i, 0, 0)),
                       const(N_DEC, N_HEADS * HEAD_W)],
            scratch_shapes=_gather_scratch(TM),
        ),
        out_shape=[
            jax.ShapeDtypeStruct((T, D_MODEL), F32),
            jax.ShapeDtypeStruct((N_PROMPT, KV_RANK), F32), jax.ShapeDtypeStruct((N_DEC, KV_RANK), F32),
            jax.ShapeDtypeStruct((N_PROMPT, ROPE_DIM), F32), jax.ShapeDtypeStruct((N_DEC, ROPE_DIM), F32),
            jax.ShapeDtypeStruct((T, N_HEADS * HEAD_W), BF16), jax.ShapeDtypeStruct((N_DEC, N_HEADS * V_DIM), BF16),
            jax.ShapeDtypeStruct((N_TILES, N_HEADS // 2 * VT_ROWS, TM), BF16),
            jax.ShapeDtypeStruct((N_TILES, N_HEADS * HEAD_W, TM), BF16),
            jax.ShapeDtypeStruct((N_DEC, N_HEADS * HEAD_W), BF16),
        ],
        compiler_params=_cparams(),
        name="mla_proj",
    )(pos, sorted_rows, cos_t, sin_t, cos_tt, sin_tt, kvn, wdkv, kvan, krg, wuk, wuv, wuv.T, kg, nmq, wdq, qan,
      wuq.T, jnp.broadcast_to(qg.reshape(HEAD_W, 1), (HEAD_W, TM)), wukt)


def _cache_kv_body(ckv_ref, kr_ref, place_ref, wuk_ref, wuv_ref, kg_ref, wukt_ref, k_ref, v_ref):
    kr = _dot(kr_ref[...].astype(BF16), place_ref[...])
    cb = ckv_ref[...].astype(BF16)
    _expand_k(cb, ckv_ref[...].T.astype(BF16), kr, wuk_ref, wukt_ref, kg_ref, k_ref)
    v_ref[...] = _dot(cb, wuv_ref[...]).astype(BF16)


CACHE_ROWS = 1024
assert PAST_LEN % CACHE_ROWS == 0


def _cache_kv(ckv, kr, wuk, wukt, wuv, kg):
    n = ckv.shape[0]
    place = jnp.asarray(np.arange(ROPE_DIM)[:, None] + ROPE_LO == np.arange(LANES)[None, :], BF16)
    const = lambda *shape: pl.BlockSpec(shape, lambda i: (0,) * len(shape))
    row = lambda w: pl.BlockSpec((CACHE_ROWS, w), lambda i: (i, 0))
    return pl.pallas_call(
        _cache_kv_body,
        grid=(n // CACHE_ROWS,),
        in_specs=[row(KV_RANK), row(ROPE_DIM), const(ROPE_DIM, LANES), const(KV_RANK, N_HEADS * HEAD_W),
                  const(KV_RANK, N_HEADS * V_DIM), const(1, LANES), const(N_HEADS * NOPE_DIM, KV_RANK)],
        out_specs=[row(N_HEADS * HEAD_W), row(N_HEADS * V_DIM)],
        out_shape=[jax.ShapeDtypeStruct((n, N_HEADS * HEAD_W), BF16), jax.ShapeDtypeStruct((n, N_HEADS * V_DIM), BF16)],
        compiler_params=_cparams(),
        name="cache_kv",
    )(ckv, kr, place, wuk, wuv, kg, wukt)


TQ = 256
TK = 256
assert TQ == TK and TQ % CHUNK == 0
SCORE_LOOKAHEAD = 8


def _qk(q, k):
    return lax.dot_general(q, k, (((1,), (1,)), ((), ())), preferred_element_type=F32)


def _merge_heads(o_ref, outs, rows):
    lane = lax.broadcasted_iota(jnp.int32, (rows, LANES), 1)
    for pr in range(N_HEADS // 2):
        o_ref[:, pr * LANES:(pr + 1) * LANES] = jnp.where(lane < V_DIM, outs[2 * pr], outs[2 * pr + 1]).astype(BF16)


def _prompt_attn_body(qt_ref, k_ref, vt_ref, rows_ref, wo_ref, out_ref, m_scr, acc_scr):
    qi = pl.program_id(1)
    m_scr[...] = jnp.full(m_scr.shape, NEG, F32)
    acc_scr[...] = jnp.zeros(acc_scr.shape, F32)

    def all_heads(j, mask):
        ks = pl.ds(pl.multiple_of(j * TK, TK), TK)

        def scores(hh):
            hcols = slice(hh * HEAD_W, (hh + 1) * HEAD_W)
            return _dot(k_ref[ks, hcols], qt_ref[0, hcols, :])

        ahead = [scores(hh) for hh in range(SCORE_LOOKAHEAD)]
        for hh in range(N_HEADS):
            vrows = slice((hh // 2) * VT_ROWS, (hh // 2 + 1) * VT_ROWS)
            s = ahead.pop(0)
            if hh + SCORE_LOOKAHEAD < N_HEADS:
                ahead.append(scores(hh + SCORE_LOOKAHEAD))
            if mask is not None:
                s = jnp.where(mask, s, NEG)
            m_old = m_scr[hh]
            m_new = jnp.maximum(m_old, jnp.max(s, axis=0, keepdims=True))
            p = jnp.exp2(s - m_new).astype(BF16)
            m_scr[hh] = m_new
            acc_scr[hh] = jnp.exp2(m_old - m_new) * acc_scr[hh] + _dot(vt_ref[j, vrows, :], p)

    def step(j, carry):
        all_heads(j, None)
        return carry

    lax.fori_loop(0, qi, step, 0)
    kc = lax.broadcasted_iota(jnp.int32, (TK, TQ), 0) // CHUNK
    qc = lax.broadcasted_iota(jnp.int32, (TK, TQ), 1) // CHUNK
    all_heads(qi, kc <= qc)
    row = lax.broadcasted_iota(jnp.int32, (LANES, TQ), 0)
    pairs = []
    for pr in range(N_HEADS // 2):
        even = acc_scr[2 * pr, :LANES] / acc_scr[2 * pr, LANES:LANES + 1]
        odd = acc_scr[2 * pr + 1, :LANES] / acc_scr[2 * pr + 1, LANES:LANES + 1]
        pairs.append(jnp.where(row < V_DIM, even, odd))
    o = jnp.concatenate(pairs, axis=0).T.astype(BF16)
    out_ref[...] = rows_ref[...] + _dot(o, wo_ref[...])


def _prompt_attn(qt, k, vt, rows, wo):
    nq = SEQ // TQ
    return pl.pallas_call(
        _prompt_attn_body,
        grid=(BATCH, nq),
        in_specs=[
            pl.BlockSpec((1, N_HEADS * HEAD_W, TQ), lambda b, i: (b * nq + i, 0, 0)),
            pl.BlockSpec((SEQ, N_HEADS * HEAD_W), lambda b, i: (b, 0)),
            pl.BlockSpec((SEQ // TK, N_HEADS // 2 * VT_ROWS, TK), lambda b, i: (b, 0, 0)),
            pl.BlockSpec((TQ, D_MODEL), lambda b, i: (b * nq + i, 0)),
            pl.BlockSpec((N_HEADS * V_DIM, D_MODEL), lambda b, i: (0, 0)),
        ],
        out_specs=pl.BlockSpec((TQ, D_MODEL), lambda b, i: (b * nq + i, 0)),
        out_shape=jax.ShapeDtypeStruct((N_PROMPT, D_MODEL), F32),
        scratch_shapes=[pltpu.VMEM((N_HEADS, 1, TQ), F32), pltpu.VMEM((N_HEADS, VT_ROWS, TQ), F32)],
        compiler_params=_cparams(2),
        name="prompt_attn",
    )(qt, k, vt, rows, wo)


def _sample_attn_body(q_ref, kc_ref, vc_ref, kn_ref, vn_ref, rows_ref, wo_ref, out_ref, o_scr):
    outs = []
    for hh in range(N_HEADS):
        hcols = slice(hh * HEAD_W, (hh + 1) * HEAD_W)
        vcols = slice((hh // 2) * LANES, (hh // 2 + 1) * LANES)
        q = q_ref[:, hcols]
        sc = _qk(q, kc_ref[:, hcols])
        sn = _qk(q, kn_ref[:, hcols])
        m = jnp.maximum(jnp.max(sc, axis=-1, keepdims=True), jnp.max(sn, axis=-1, keepdims=True))
        pc = jnp.exp2(sc - m)
        pn = jnp.exp2(sn - m)
        l = jnp.sum(pc, axis=-1, keepdims=True) + jnp.sum(pn, axis=-1, keepdims=True)
        acc = _dot(pc.astype(BF16), vc_ref[:, vcols]) + _dot(pn.astype(BF16), vn_ref[:, vcols])
        outs.append(acc / l)
    _merge_heads(o_scr, outs, DEC_SEQ)
    out_ref[...] = rows_ref[...] + _dot(o_scr[...], wo_ref[...])


def _sample_attn(q, kc, vc, kn, vn, rows, wo):
    off = N_PROMPT // DEC_SEQ
    return pl.pallas_call(
        _sample_attn_body,
        grid=(DEC_BATCH,),
        in_specs=[
            pl.BlockSpec((DEC_SEQ, N_HEADS * HEAD_W), lambda b: (b, 0)),
            pl.BlockSpec((PAST_LEN, N_HEADS * HEAD_W), lambda b: (b, 0)),
            pl.BlockSpec((PAST_LEN, N_HEADS * V_DIM), lambda b: (b, 0)),
            pl.BlockSpec((DEC_SEQ, N_HEADS * HEAD_W), lambda b: (off + b, 0)),
            pl.BlockSpec((DEC_SEQ, N_HEADS * V_DIM), lambda b: (b, 0)),
            pl.BlockSpec((DEC_SEQ, D_MODEL), lambda b: (off + b, 0)),
            pl.BlockSpec((N_HEADS * V_DIM, D_MODEL), lambda b: (0, 0)),
        ],
        out_specs=pl.BlockSpec((DEC_SEQ, D_MODEL), lambda b: (b, 0)),
        out_shape=jax.ShapeDtypeStruct((N_DEC, D_MODEL), F32),
        scratch_shapes=[pltpu.VMEM((DEC_SEQ, N_HEADS * V_DIM), BF16)],
        compiler_params=_cparams(),
        name="sample_attn",
    )(q, kc, vc, kn, vn, rows, wo)


def _finish_body(pos_ref, sorted_hbm, yp_ref, ys_ref, xbuf, xsem):
    i = pl.program_id(0)
    x_ref = _gather_tile(pos_ref, sorted_hbm, xbuf, xsem, TM)

    def relayout(dst_ref):
        for s in range(ROW_TILES):
            dst_ref[:, s * LANES:(s + 1) * LANES] = x_ref[pl.ds(s, TM, stride=ROW_TILES), :]

    @pl.when(i < N_TILES - 1)
    def _():
        relayout(yp_ref)

    @pl.when(i == N_TILES - 1)
    def _():
        relayout(ys_ref)


def _finish(pos, sorted_rows):
    return pl.pallas_call(
        _finish_body,
        grid_spec=pltpu.PrefetchScalarGridSpec(
            num_scalar_prefetch=1,
            grid=(N_TILES,),
            in_specs=[pl.BlockSpec(memory_space=pl.ANY)],
            out_specs=[pl.BlockSpec((TM, D_MODEL), lambda i, p: (jnp.minimum(i, N_TILES - 2), 0)),
                       pl.BlockSpec((N_DEC, D_MODEL), lambda i, p: (0, 0))],
            scratch_shapes=_gather_scratch(TM),
        ),
        out_shape=[jax.ShapeDtypeStruct((N_PROMPT, D_MODEL), F32), jax.ShapeDtypeStruct((N_DEC, D_MODEL), F32)],
        compiler_params=_cparams(),
        name="finish",
    )(pos, sorted_rows)


def _rope_tables():
    half = ROPE_DIM // 2
    inv_freq = ROPE_THETA ** (-jnp.arange(half, dtype=F32) / half)
    dec_pos = PAST_LEN + jnp.tile(jnp.arange(DEC_SEQ, dtype=jnp.int32), DEC_BATCH)
    pos = jnp.concatenate([jnp.arange(SEQ, dtype=jnp.int32), dec_pos])
    ang = pos.astype(F32)[:, None] * inv_freq[None, :]
    cos, sin = jnp.cos(ang), jnp.sin(ang)
    n = pos.shape[0]
    cos_t = jnp.ones((n, LANES), F32).at[:, ROPE_LO:ROPE_LO + ROPE_DIM].set(jnp.concatenate([cos, cos], axis=1))
    sin_t = jnp.zeros((n, LANES), F32).at[:, ROPE_LO:ROPE_LO + ROPE_DIM].set(jnp.concatenate([-sin, sin], axis=1))
    to_tiles = lambda a: a.reshape(n // TM, TM, half).transpose(0, 2, 1)
    return cos_t, sin_t, to_tiles(cos), to_tiles(sin)


def _on_lanes(vec, lo):
    return jnp.zeros((1, LANES), F32).at[0, lo:lo + vec.shape[0]].set(vec)


def kernel(x_prompt, x_sample, cache_ckv, cache_krope, norm_mix, norm_ffn, gm_w_in, gm_b_in, gm_ln_g, gm_ln_b, gm_w_s, gm_b_s, gm_w_out, gm_b_out, kv_norm, w_dkv, kv_a_norm, k_rope_norm, w_uk, w_uv, k_nope_norm, w_dq, q_a_norm, w_uq, q_nope_norm, q_rope_norm, w_o, moe_w_group, moe_b_group, moe_w_expert, moe_b_expert, moe_w1, moe_w3, moe_w2):
    nf0, nf1 = norm_ffn[0].reshape(1, D_MODEL), norm_ffn[1].reshape(1, D_MODEL)

    idx = np.arange(GMLP_BLOCK)
    allowed = (idx[None, :] // CHUNK) <= (idx[:, None] // CHUNK)
    ws_p = jnp.where(allowed[None], gm_w_s[0], 0.0).astype(BF16)
    same_seq = (idx[None, :] // DEC_SEQ) == (idx[:, None] // DEC_SEQ)
    ws_d = jnp.where(same_seq[None], jnp.tile(gm_w_s[0][:, :DEC_SEQ, :DEC_SEQ], (1, GMLP_BLOCK // DEC_SEQ, GMLP_BLOCK // DEC_SEQ)), 0.0).astype(BF16)
    bs_p = gm_b_s[0][:, :, None]
    bs_d = jnp.tile(gm_b_s[0][:, :DEC_SEQ], (1, GMLP_BLOCK // DEC_SEQ))[:, :, None]
    rows, v_rows, info, cnt = _gmlp_layer(
        x_prompt.reshape(N_PROMPT, D_MODEL), x_sample.reshape(N_DEC, D_MODEL),
        norm_mix[0].reshape(1, -1), gm_w_in[0].astype(BF16), gm_b_in[0].reshape(1, -1),
        gm_ln_g[0].reshape(1, -1), gm_ln_b[0].reshape(1, -1), ws_p, ws_d, bs_p, bs_d,
        gm_w_out[0].astype(BF16), gm_b_out[0].reshape(1, -1),
        nf0, *_router_weights(0, moe_w_group, moe_b_group, moe_w_expert, moe_b_expert))
    sorted_rows, pos = _moe_layer(0, rows, info, cnt, nf0, moe_w1, moe_w3, moe_w2)

    cos_t, sin_t, cos_tt, sin_tt = _rope_tables()
    wdkv = jnp.zeros((D_MODEL, KV_RANK + LANES), F32).at[:, :KV_RANK].set(w_dkv[:, :KV_RANK])
    wdkv = wdkv.at[:, KV_RANK + ROPE_LO:KV_RANK + ROPE_LO + ROPE_DIM].set(w_dkv[:, KV_RANK:]).astype(BF16)
    wuk = jnp.zeros((KV_RANK, N_HEADS, HEAD_W), F32).at[:, :, :NOPE_DIM].set(w_uk).reshape(KV_RANK, -1).astype(BF16)
    wukt = w_uk.reshape(KV_RANK, -1).T.astype(BF16)
    wuv = w_uv.reshape(KV_RANK, -1).astype(BF16)
    wuq =jnp.zeros((Q_RANK, N_HEADS, HEAD_W), F32).at[:, :, :NOPE_DIM + ROPE_DIM].set(w_uq[0]).reshape(Q_RANK, -1).astype(BF16)
    kg = _on_lanes(k_nope_norm, 0)
    krg = _on_lanes(k_rope_norm, ROPE_LO)
    qg = _on_lanes(jnp.concatenate([q_nope_norm[0], q_rope_norm[0]]), 0)
    rows, ckv_p, ckv_d, krope_p, krope_d, k_new, v_dec, vt_new, qt, q_dec = _mla_proj(
        pos, sorted_rows, cos_t, sin_t, cos_tt, sin_tt, kv_norm.reshape(1, -1), wdkv, kv_a_norm.reshape(1, -1), krg, wuk, wukt, wuv, kg,
        norm_mix[1].reshape(1, -1), w_dq[0].astype(BF16), q_a_norm[0].reshape(1, -1), wuq, qg)
    k_cache, v_cache = _cache_kv(cache_ckv.reshape(-1, KV_RANK), cache_krope.reshape(-1, ROPE_DIM), wuk, wukt, wuv, kg)

    wo = w_o[0].astype(BF16)
    h_prompt = _prompt_attn(qt, k_new, vt_new, rows, wo)
    h_dec = _sample_attn(q_dec, k_cache, v_cache, k_new, v_dec, rows, wo)
    rows, info, cnt = _router(h_prompt, h_dec, nf1,
                              *_router_weights(1, moe_w_group, moe_b_group, moe_w_expert, moe_b_expert))
    sorted_rows, pos = _moe_layer(1, rows, info, cnt, nf1, moe_w1, moe_w3, moe_w2)
    y_prompt, y_sample = _finish(pos, sorted_rows)

    return (y_prompt.reshape(BATCH, SEQ, D_MODEL), y_sample.reshape(DEC_BATCH, DEC_SEQ, D_MODEL),
            ckv_p.reshape(BATCH, SEQ, KV_RANK), krope_p.reshape(BATCH, SEQ, ROPE_DIM),
            ckv_d.reshape(DEC_BATCH, DEC_SEQ, KV_RANK), krope_d.reshape(DEC_BATCH, DEC_SEQ, ROPE_DIM),
            v_rows.reshape(1, DEC_BATCH, DEC_SEQ, D_GATE))
```

```python
import functools

import jax
import jax.numpy as jnp
import numpy as np
from jax import lax
from jax.experimental import pallas as pl
from jax.experimental.pallas import tpu as pltpu

F32 = jnp.float32
BF16 = jnp.bfloat16

D_MODEL = 1024
BATCH = 8
SEQ = 2048
DEC_BATCH = 16
DEC_SEQ = 16
PAST_LEN = 2048
CHUNK = 64
GMLP_BLOCK = 128
D_GATE = 2 * D_MODEL
N_SG = 8
SG_W = D_GATE // N_SG
N_HEADS = 8
NOPE_DIM = 64
ROPE_DIM = 32
V_DIM = 64
Q_RANK = 384
KV_RANK = 256
ROPE_THETA = 10000.0
SCALE = (NOPE_DIM + ROPE_DIM) ** -0.5
Q_SCALE = SCALE * float(np.log2(np.e))
N_EGROUPS = 4
EXPERTS_PER_GROUP = 4
N_EXPERTS = N_EGROUPS * EXPERTS_PER_GROUP
D_EXPERT = 512
EPS = 1e-6
NEG = -1e30

LANES = 128
SUBLANES = 8
ROW_TILES = D_MODEL // LANES
assert ROW_TILES == SUBLANES

N_PROMPT = BATCH * SEQ
N_DEC = DEC_BATCH * DEC_SEQ
T = N_PROMPT + N_DEC
TM = 256
assert N_PROMPT % TM == 0 and N_DEC == TM
N_TILES = T // TM
HEAD_W = LANES
ROPE_LO = NOPE_DIM
ROPE_HALF = ROPE_DIM // 2

PAIR_A = (0, 0, 0, 1, 1, 3)
PAIR_B = (1, 2, 3, 3, 2, 2)
N_PAIRS = 6
N_BUCKETS = N_EGROUPS * N_PAIRS
BUCKET_ROWS = 32
assert N_BUCKETS <= BUCKET_ROWS and BUCKET_ROWS % SUBLANES == 0
MOE_TILES = (T + N_BUCKETS * (TM - 1) + TM - 1) // TM
P_ROWS = MOE_TILES * TM

VMEM_LIMIT = 56 * 1024 * 1024


def _cparams(n_axes=1, vmem=VMEM_LIMIT):
    return pltpu.CompilerParams(dimension_semantics=("arbitrary",) * n_axes, vmem_limit_bytes=vmem)


def _rms(x, g):
    return x * lax.rsqrt(jnp.mean(x * x, axis=-1, keepdims=True) + EPS) * g


def _load_rows(ref, n):
    return jnp.concatenate([ref[pl.ds(s, n, stride=ROW_TILES), :] for s in range(ROW_TILES)], axis=1)


def _store_rows(ref, x, n):
    for s in range(ROW_TILES):
        ref[pl.ds(s, n, stride=ROW_TILES), :] = x[:, s * LANES:(s + 1) * LANES]


def _dot(a, b):
    return jnp.dot(a, b, preferred_element_type=F32)


GELU_K1 = float(-2.0 * np.sqrt(2.0 / np.pi) * np.log2(np.e))
GELU_K3 = GELU_K1 * 0.044715


def _gmlp_body(xp_ref, xd_ref, nm_ref, win_ref, bin_ref, lng_ref, lnb_ref, wsp_ref, wsd_ref, bsp_ref, bsd_ref,
               wout_ref, bout_ref, nf_ref, wrh_ref, wr2_ref, br_ref,
               rows_ref, v_ref, info_ref, cnt_ref, gated_ref, carry_ref):
    i = pl.program_id(0)
    is_dec = i == N_TILES - 1
    _route_init(carry_ref)
    x = jnp.where(is_dec, xd_ref[...], xp_ref[...])
    xn = _rms(x, nm_ref[...]).astype(BF16)
    z = _dot(xn, win_ref[...]) + bin_ref[...]
    z = z / (1.0 + jnp.exp2(z * (GELU_K1 + GELU_K3 * (z * z))))
    u = z[:, :D_GATE]
    v = z[:, D_GATE:]
    mu = jnp.mean(v, axis=-1, keepdims=True)
    vc = v - mu
    var = jnp.mean(vc * vc, axis=-1, keepdims=True)
    v = vc * lax.rsqrt(var + EPS) * lng_ref[...] + lnb_ref[...]
    vb = v.astype(BF16)
    for g in range(N_SG):
        ws = jnp.where(is_dec, wsd_ref[g], wsp_ref[g])
        bs = jnp.where(is_dec, bsd_ref[g], bsp_ref[g])
        for b in range(TM // GMLP_BLOCK):
            rows = slice(b * GMLP_BLOCK, (b + 1) * GMLP_BLOCK)
            cols = slice(g * SG_W, (g + 1) * SG_W)
            s = _dot(ws, vb[rows, cols]) + bs
            gated_ref[rows, cols] = (u[rows, cols] * s).astype(BF16)
    h = x + _dot(gated_ref[...], wout_ref[...]) + bout_ref[...]
    _store_rows(rows_ref, h, TM)
    _route_tile(h, nf_ref, wrh_ref, wr2_ref, br_ref, info_ref, cnt_ref, carry_ref)

    @pl.when(is_dec)
    def _():
        v_ref[...] = v


def _gmlp_layer(x_prompt, x_dec, nm, w_in, b_in, ln_g, ln_b, ws_p, ws_d, bs_p, bs_d, w_out, b_out, nf, wr, br):
    const = lambda *shape: pl.BlockSpec(shape, lambda i: (0,) * len(shape))
    return pl.pallas_call(
        _gmlp_body,
        grid=(N_TILES,),
        in_specs=[
            pl.BlockSpec((TM, D_MODEL), lambda i: (jnp.minimum(i, N_TILES - 2), 0)), const(N_DEC, D_MODEL),
            const(1, D_MODEL), const(D_MODEL, 2 * D_GATE), const(1, 2 * D_GATE),
            const(1, D_GATE), const(1, D_GATE),
            const(N_SG, GMLP_BLOCK, GMLP_BLOCK), const(N_SG, GMLP_BLOCK, GMLP_BLOCK),
            const(N_SG, GMLP_BLOCK, 1), const(N_SG, GMLP_BLOCK, 1),
            const(D_GATE, D_MODEL), const(1, D_MODEL),
            const(1, D_MODEL), const(D_MODEL, LANES), const(D_MODEL, 2 * LANES), const(1, LANES),
        ],
        out_specs=[
            pl.BlockSpec((TM * ROW_TILES, LANES), lambda i: (i, 0)),
            const(N_DEC, D_GATE),
            pl.BlockSpec((SUBLANES, TM), lambda i: (0, i)),
            const(BUCKET_ROWS, LANES),
        ],
        out_shape=[jax.ShapeDtypeStruct((T * ROW_TILES, LANES), F32), jax.ShapeDtypeStruct((N_DEC, D_GATE), F32),
                   jax.ShapeDtypeStruct((SUBLANES, T), F32), jax.ShapeDtypeStruct((BUCKET_ROWS, LANES), F32)],
        scratch_shapes=[pltpu.VMEM((TM, D_GATE), BF16), pltpu.VMEM((BUCKET_ROWS, LANES), F32)],
        compiler_params=_cparams(),
        name="gmlp_layer",
    )(x_prompt, x_dec, nm, w_in, b_in, ln_g, ln_b, ws_p, ws_d, bs_p, bs_d, w_out, b_out,
      nf, *_router_split(wr), br)


def _router_body(hp_ref, hd_ref, nf_ref, wrh_ref, wr2_ref, br_ref, rows_ref, info_ref, cnt_ref, carry_ref):
    _route_init(carry_ref)
    h = jnp.where(pl.program_id(0) == N_TILES - 1, hd_ref[...], hp_ref[...])
    _store_rows(rows_ref, h, TM)
    _route_tile(h, nf_ref, wrh_ref, wr2_ref, br_ref, info_ref, cnt_ref, carry_ref)


def _router_split(wr):
    hi = wr.astype(BF16)
    return hi, jnp.concatenate([hi, (wr - hi.astype(F32)).astype(BF16)], axis=1)


def _route_init(carry_ref):
    @pl.when(pl.program_id(0) == 0)
    def _():
        carry_ref[...] = jnp.zeros_like(carry_ref)


def _route_tile(h, nf_ref, wrh_ref, wr2_ref, br_ref, info_ref, cnt_ref, carry_ref):
    xn = _rms(h, nf_ref[...])
    xh = xn.astype(BF16)
    xl = (xn - xh.astype(F32)).astype(BF16)
    both = _dot(xh, wr2_ref[...])
    logits = both[:, :LANES] + (_dot(xl, wrh_ref[...]) + both[:, LANES:]) + br_ref[...]
    lt = logits.T

    def first_max(vals, ids):
        vmax = jnp.max(vals, axis=0, keepdims=True)
        return vmax, jnp.min(jnp.where(vals == vmax, ids, float(LANES)), axis=0, keepdims=True)

    lg = lt[N_EXPERTS:N_EXPERTS + N_EGROUPS]
    gmax, g_idx = first_max(lg, lax.broadcasted_iota(jnp.int32, lg.shape, 0).astype(F32))
    g_p = 1.0 / jnp.sum(jnp.exp(lg - gmax), axis=0, keepdims=True)
    e_id = lax.broadcasted_iota(jnp.int32, (N_EXPERTS, TM), 0)
    in_group = (e_id // EXPERTS_PER_GROUP).astype(F32) == g_idx
    e_id = e_id.astype(F32)
    le = jnp.where(in_group, lt[:N_EXPERTS], -jnp.inf)
    v1, i1 = first_max(le, e_id)
    v2, i2 = first_max(jnp.where(e_id == i1, -jnp.inf, le), e_id)
    e2 = jnp.exp(v2 - v1)
    w1 = (1.0 / (1.0 + e2)) * g_p
    w2 = (e2 / (1.0 + e2)) * g_p
    a1 = i1 - EXPERTS_PER_GROUP * g_idx
    a2 = i2 - EXPERTS_PER_GROUP * g_idx
    lo = jnp.minimum(a1, a2)
    hi = jnp.maximum(a1, a2)
    pair = jnp.where(lo == 0.0, hi - 1.0, jnp.where(lo == 1.0, jnp.where(hi == 3.0, 3.0, 4.0), 5.0))
    ea = jnp.where(pair < 3.0, 0.0, jnp.where(pair < 5.0, 1.0, 3.0))
    ga = jnp.where(a1 == ea, w1, w2)
    gb = jnp.where(a1 == ea, w2, w1)
    bucket = g_idx * N_PAIRS + pair

    onehot = (lax.broadcasted_iota(jnp.int32, (BUCKET_ROWS, TM), 0).astype(F32) == bucket).astype(F32)
    r = lax.broadcasted_iota(jnp.int32, (TM, TM), 0)
    c = lax.broadcasted_iota(jnp.int32, (TM, TM), 1)
    before = _dot(onehot.astype(BF16), (r < c).astype(BF16))
    carry = carry_ref[:, 0:1]
    rank = jnp.sum(onehot * (before + carry), axis=0, keepdims=True)
    new_carry = carry + jnp.sum(onehot, axis=1, keepdims=True)
    carry_ref[...] = jnp.broadcast_to(new_carry, carry_ref.shape)
    cnt_ref[...] = jnp.broadcast_to(new_carry, cnt_ref.shape)
    info_ref[...] = jnp.concatenate([bucket, rank, ga, gb, jnp.zeros((SUBLANES - 4, TM), F32)], axis=0)


def _router(h_prompt, h_dec, nf, wr, br):
    const = lambda *shape: pl.BlockSpec(shape, lambda i: (0,) * len(shape))
    return pl.pallas_call(
        _router_body,
        grid=(N_TILES,),
        in_specs=[pl.BlockSpec((TM, D_MODEL), lambda i: (jnp.minimum(i, N_TILES - 2), 0)), const(N_DEC, D_MODEL),
                  const(1, D_MODEL), const(D_MODEL, LANES), const(D_MODEL, 2 * LANES), const(1, LANES)],
        out_specs=[pl.BlockSpec((TM * ROW_TILES, LANES), lambda i: (i, 0)),
                   pl.BlockSpec((SUBLANES, TM), lambda i: (0, i)),
                   const(BUCKET_ROWS, LANES)],
        out_shape=[jax.ShapeDtypeStruct((T * ROW_TILES, LANES), F32), jax.ShapeDtypeStruct((SUBLANES, T), F32),
                   jax.ShapeDtypeStruct((BUCKET_ROWS, LANES), F32)],
        scratch_shapes=[pltpu.VMEM((BUCKET_ROWS, LANES), F32)],
        compiler_params=_cparams(),
        name="moe_router",
    )(h_prompt, h_dec, nf, *_router_split(wr), br)


GATHER_UNROLL = 32


def _gather_rows_start(idx_ref, base, src_ref, dst_ref, sem, n):
    def group(g, carry):
        for u in range(GATHER_UNROLL):
            r = g * GATHER_UNROLL + u
            src = pl.multiple_of(idx_ref[base + r] * ROW_TILES, ROW_TILES)
            dst = pl.multiple_of(r * ROW_TILES, ROW_TILES)
            pltpu.make_async_copy(src_ref.at[pl.ds(src, ROW_TILES), :], dst_ref.at[pl.ds(dst, ROW_TILES), :],
                                  sem).start(priority=u % 2)
        return carry

    lax.fori_loop(0, n // GATHER_UNROLL, group, 0)


def _gather_rows_wait(src_ref, dst_ref, sem, n):
    pltpu.make_async_copy(src_ref.at[pl.ds(0, n * ROW_TILES), :], dst_ref.at[pl.ds(0, n * ROW_TILES), :], sem).wait()


def _gather_tile(idx_ref, src_ref, buf, sem, n, n_live=None, base_of=None, side_copy=None):
    i = pl.program_id(0)
    slot = lax.rem(i, 2)
    n_live = pl.num_programs(0) if n_live is None else n_live
    base_of = (lambda step: step * n) if base_of is None else base_of

    def start(step, s):
        _gather_rows_start(idx_ref, base_of(step), src_ref, buf.at[s], sem.at[s], n)
        if side_copy is not None:
            side_copy(step, s).start()

    @pl.when(i == 0)
    def _():
        start(0, 0)

    @pl.when(i + 1 < n_live)
    def _():
        start(i + 1, 1 - slot)

    @pl.when(i < n_live)
    def _():
        _gather_rows_wait(src_ref, buf.at[slot], sem.at[slot], n)
        if side_copy is not None:
            side_copy(i, slot).wait()

    return buf.at[slot]


def _gather_scratch(n):
    return [pltpu.VMEM((2, n * ROW_TILES, LANES), F32), pltpu.SemaphoreType.DMA((2,))]


GATE_WIN = TM + SUBLANES


def _moe_body(tok_ref, ea_ref, eb_ref, cha_ref, chb_ref, nlive_ref, first_ref, nrows_ref, rows_hbm, gates_hbm, nf_ref,
              w1a_ref, w3a_ref, w2a_ref, w1b_ref, w3b_ref, w2b_ref, out_ref,
              s1a, s3a, s2a, s1b, s3b, s2b, xbuf, xsem, gwin):
    i = pl.program_id(0)
    live = i < nlive_ref[0]

    def gate_copy(step, s):
        lo = pl.multiple_of((first_ref[step] // SUBLANES) * SUBLANES, SUBLANES)
        return pltpu.make_async_copy(gates_hbm.at[pl.ds(lo, GATE_WIN), :], gwin.at[s], xsem.at[s])

    x_ref = _gather_tile(tok_ref, rows_hbm, xbuf, xsem, TM, nlive_ref[0],
                         base_of=lambda step: first_ref[step], side_copy=gate_copy)

    @pl.when(cha_ref[i] == 1)
    def _():
        s1a[...] = w1a_ref[...].astype(BF16)
        s3a[...] = w3a_ref[...].astype(BF16)
        s2a[...] = w2a_ref[...].astype(BF16)

    @pl.when(chb_ref[i] == 1)
    def _():
        s1b[...] = w1b_ref[...].astype(BF16)
        s3b[...] = w3b_ref[...].astype(BF16)
        s2b[...] = w2b_ref[...].astype(BF16)

    @pl.when(live)
    def _():
        h = _load_rows(x_ref, TM)
        xn = _rms(h, nf_ref[...]).astype(BF16)
        g = gwin[lax.rem(i, 2), pl.ds(lax.rem(first_ref[i], SUBLANES), TM), :]
        row = lax.broadcasted_iota(jnp.int32, (TM, 1), 0)
        g = jnp.where(row < nrows_ref[i], g, 0.0)

        def ffn(w1, w3, w2):
            a = _dot(xn, w1[...])
            hdn = (a * (1.0 / (1.0 + jnp.exp(-a)))) * _dot(xn, w3[...])
            return _dot(hdn.astype(BF16), w2[...])

        y = g[:, 0:1] * ffn(s1a, s3a, s2a) + g[:, 1:2] * ffn(s1b, s3b, s2b)
        _store_rows(out_ref, h + y, TM)

    @pl.when(jnp.logical_not(live))
    def _():
        out_ref[...] = jnp.zeros_like(out_ref)


def _moe_ffn(layer, tok, ea, eb, cha, chb, n_live, first, nrows, rows, gates, nf, w1, w3, w2):
    wa = lambda shape: pl.BlockSpec((None, None) + shape, lambda i, tk, ea, eb, *_: (layer, ea[i], 0, 0))
    wb = lambda shape: pl.BlockSpec((None, None) + shape, lambda i, tk, ea, eb, *_: (layer, eb[i], 0, 0))
    up, down = (D_MODEL, D_EXPERT), (D_EXPERT, D_MODEL)
    return pl.pallas_call(
        _moe_body,
        grid_spec=pltpu.PrefetchScalarGridSpec(
            num_scalar_prefetch=8,
            grid=(MOE_TILES,),
            in_specs=[
                pl.BlockSpec(memory_space=pl.ANY),
                pl.BlockSpec(memory_space=pl.ANY),
                pl.BlockSpec((1, D_MODEL), lambda i, *_: (0, 0)),
                wa(up), wa(up), wa(down), wb(up), wb(up), wb(down),
            ],
            out_specs=pl.BlockSpec((TM * ROW_TILES, LANES), lambda i, *_: (i, 0)),
            scratch_shapes=[pltpu.VMEM(up, BF16), pltpu.VMEM(up, BF16), pltpu.VMEM(down, BF16),
                            pltpu.VMEM(up, BF16), pltpu.VMEM(up, BF16), pltpu.VMEM(down, BF16)]
            + _gather_scratch(TM) + [pltpu.VMEM((2, GATE_WIN, 2), F32)],
        ),
        out_shape=jax.ShapeDtypeStruct((P_ROWS * ROW_TILES, LANES), F32),
        compiler_params=_cparams(),
        name="moe_ffn",
    )(tok, ea, eb, cha, chb, n_live, first, nrows, rows, gates, nf, w1, w3, w2, w1, w3, w2)


def _router_weights(layer, w_group, b_group, w_expert, b_expert):
    pad = LANES - N_EXPERTS - N_EGROUPS
    wr = jnp.concatenate([w_expert[layer], w_group[layer], jnp.zeros((D_MODEL, pad), F32)], axis=1)
    br = jnp.concatenate([b_expert[layer], b_group[layer], jnp.zeros((pad,), F32)]).reshape(1, LANES)
    return wr, br


def _moe_layer(layer, rows, info, cnt, nf, w1, w3, w2):
    bucket = info[0].astype(jnp.int32)
    rank = info[1].astype(jnp.int32)
    counts = cnt[:N_BUCKETS, 0].astype(jnp.int32)
    n_tiles = (counts + TM - 1) // TM
    tile_end = jnp.cumsum(n_tiles)
    tile_start = tile_end - n_tiles
    start_of = jnp.sum(jnp.where(bucket[:, None] == jnp.arange(N_BUCKETS)[None, :], tile_start[None, :], 0), axis=1)
    pos = start_of * TM + rank
    _, tok, ga, gb = lax.sort((pos, jnp.arange(T, dtype=jnp.int32), info[2], info[3]), num_keys=1)
    tok = jnp.concatenate([tok, jnp.zeros((TM,), jnp.int32)])
    gates = jnp.concatenate([jnp.stack([ga, gb], axis=1), jnp.zeros((TM + SUBLANES, 2), F32)])
    total = tile_end[-1]
    j = jnp.minimum(jnp.arange(MOE_TILES), total - 1)
    tb = jnp.sum((j[:, None] >= tile_end[None, :]).astype(jnp.int32), axis=1)
    tokens_before = jnp.cumsum(counts) - counts
    in_bucket = (j - tile_start[tb]) * TM
    first = (tokens_before[tb] + in_bucket).astype(jnp.int32)
    nrows = jnp.clip(counts[tb] - in_bucket, 0, TM).astype(jnp.int32)
    grp, pair = tb // N_PAIRS, tb % N_PAIRS
    ea = (grp * EXPERTS_PER_GROUP + jnp.asarray(PAIR_A, jnp.int32)[pair]).astype(jnp.int32)
    eb = (grp * EXPERTS_PER_GROUP + jnp.asarray(PAIR_B, jnp.int32)[pair]).astype(jnp.int32)
    tile0 = jnp.arange(MOE_TILES) == 0
    cha = (tile0 | (ea != jnp.roll(ea, 1))).astype(jnp.int32)
    chb = (tile0 | (eb != jnp.roll(eb, 1))).astype(jnp.int32)
    n_live = total.reshape(1).astype(jnp.int32)
    return _moe_ffn(layer, tok, ea, eb, cha, chb, n_live, first, nrows, rows, gates, nf, w1, w3, w2), pos


VT_ROWS = LANES + 16


def _rope_swap(x):
    lane = lax.broadcasted_iota(jnp.int32, x.shape, 1)
    return jnp.where(lane < ROPE_LO + ROPE_HALF, pltpu.roll(x, LANES - ROPE_HALF, 1), pltpu.roll(x, ROPE_HALF, 1))


def _expand_k(cb, cbt, kr, wuk_ref, wukt_ref, kg_ref, k_ref):
    n = cb.shape[0]
    kn = _dot(cb, wuk_ref[...])
    knt = _dot(wukt_ref[...], cbt)
    ms = [jnp.mean(jnp.square(knt[hh * NOPE_DIM:(hh + 1) * NOPE_DIM]), axis=0, keepdims=True)
          for hh in range(N_HEADS)]
    scale = lax.rsqrt(jnp.concatenate(ms + [jnp.ones((LANES - N_HEADS, n), F32)], axis=0) + EPS).T
    for hh in range(N_HEADS):
        cols = slice(hh * HEAD_W, (hh + 1) * HEAD_W)
        k_ref[:, cols] = (kn[:, cols] * scale[:, hh:hh + 1] * kg_ref[...] + kr).astype(BF16)


def _mla_proj_body(pos_ref, sorted_hbm, cos_ref, sin_ref, cost_ref, sint_ref, kvn_ref, wdkv_ref, kvan_ref, krg_ref,
                   wuk_ref, wuv_ref, wuvt_ref, kg_ref, nmq_ref, wdq_ref, qan_ref, wuqt_ref, qg_ref, wukt_ref,
                   rows_ref, ckvp_ref, ckvd_ref, krp_ref, krd_ref, k_ref, vdec_ref, vt_ref, qt_ref, qdec_ref,
                   xbuf, xsem):
    is_dec = pl.program_id(0) == N_TILES - 1
    x_ref = _gather_tile(pos_ref, sorted_hbm, xbuf, xsem, TM)
    h = _load_rows(x_ref, TM)
    rows_ref[...] = h
    hn = h * lax.rsqrt(jnp.mean(h * h, axis=-1, keepdims=True) + EPS)
    c = _dot((hn * kvn_ref[...]).astype(BF16), wdkv_ref[...])
    ckv = _rms(c[:, :KV_RANK], kvan_ref[...])
    kr = c[:, KV_RANK:]
    kr = kr * lax.rsqrt(jnp.sum(kr * kr, axis=-1, keepdims=True) * (1.0 / ROPE_DIM) + EPS) * krg_ref[...]
    kr = kr * cos_ref[...] + _rope_swap(kr) * sin_ref[...]
    cb = ckv.astype(BF16)
    cbt = ckv.T.astype(BF16)
    _expand_k(cb, cbt, kr, wuk_ref, wukt_ref, kg_ref, k_ref)
    vt = _dot(wuvt_ref[...], cbt).astype(BF16)
    for pr in range(N_HEADS // 2):
        vt_ref[0, pr * VT_ROWS:pr * VT_ROWS + LANES, :] = vt[pr * LANES:(pr + 1) * LANES]
        vt_ref[0, pr * VT_ROWS + LANES:(pr + 1) * VT_ROWS, :] = jnp.ones((VT_ROWS - LANES, TM), BF16)
    cq = _rms(_dot((hn * nmq_ref[...]).astype(BF16), wdq_ref[...]), qan_ref[...])
    qt = _dot(wuqt_ref[...], cq.T.astype(BF16))
    cost, sint, qg = cost_ref[0], sint_ref[0], qg_ref[...]
    for hh in range(N_HEADS):
        x = qt[hh * HEAD_W:(hh + 1) * HEAD_W, :]
        xn, xr = x[:NOPE_DIM], x[ROPE_LO:ROPE_LO + ROPE_DIM]
        xn = xn * lax.rsqrt(jnp.mean(xn * xn, axis=0, keepdims=True) + EPS) * qg[:NOPE_DIM]
        xr = xr * lax.rsqrt(jnp.mean(xr * xr, axis=0, keepdims=True) + EPS) * qg[ROPE_LO:ROPE_LO + ROPE_DIM]
        x1, x2 = xr[:ROPE_HALF], xr[ROPE_HALF:]
        qh = jnp.concatenate([xn, x1 * cost - x2 * sint, x1 * sint + x2 * cost,
                              jnp.zeros((HEAD_W - NOPE_DIM - ROPE_DIM, TM), F32)], axis=0) * Q_SCALE
        qt_ref[0, hh * HEAD_W:(hh + 1) * HEAD_W, :] = qh.astype(BF16)

    @pl.when(jnp.logical_not(is_dec))
    def _():
        ckvp_ref[...] = ckv
        krp_ref[...] = kr[:, ROPE_LO:ROPE_LO + ROPE_DIM]

    @pl.when(is_dec)
    def _():
        ckvd_ref[...] = ckv
        krd_ref[...] = kr[:, ROPE_LO:ROPE_LO + ROPE_DIM]
        vdec_ref[...] = _dot(cb, wuv_ref[...]).astype(BF16)
        for hh in range(N_HEADS):
            rows = slice(hh * HEAD_W, (hh + 1) * HEAD_W)
            qdec_ref[:, rows] = qt_ref[0, rows, :].astype(F32).T.astype(BF16)


def _mla_proj(pos, sorted_rows, cos_t, sin_t, cos_tt, sin_tt, kvn, wdkv, kvan, krg, wuk, wukt, wuv, kg, nmq, wdq, qan, wuq, qg):
    const = lambda *shape: pl.BlockSpec(shape, lambda i, p: (0,) * len(shape))
    tab_tile = lambda i: jnp.where(i < N_PROMPT // TM, i % (SEQ // TM), SEQ // TM)
    tab = pl.BlockSpec((TM, LANES), lambda i, p: (tab_tile(i), 0))
    tab_t = pl.BlockSpec((1, ROPE_HALF, TM), lambda i, p: (tab_tile(i), 0, 0))
    row = lambda w: pl.BlockSpec((TM, w), lambda i, p: (i, 0))
    prow = lambda w: pl.BlockSpec((TM, w), lambda i, p: (jnp.minimum(i, N_TILES - 2), 0))
    return pl.pallas_call(
        _mla_proj_body,
        grid_spec=pltpu.PrefetchScalarGridSpec(
            num_scalar_prefetch=1,
            grid=(N_TILES,),
            in_specs=[
                pl.BlockSpec(memory_space=pl.ANY), tab, tab, tab_t, tab_t,
                const(1, D_MODEL), const(D_MODEL, KV_RANK + LANES), const(1, KV_RANK), const(1, LANES),
                const(KV_RANK, N_HEADS * HEAD_W), const(KV_RANK, N_HEADS * V_DIM), const(N_HEADS * V_DIM, KV_RANK),
                const(1, LANES),
                const(1, D_MODEL), const(D_MODEL, Q_RANK), const(1, Q_RANK), const(N_HEADS * HEAD_W, Q_RANK),
                const(HEAD_W, TM), const(N_HEADS * NOPE_DIM, KV_RANK),
            ],
            out_specs=[row(D_MODEL),
                       prow(KV_RANK), const(N_DEC, KV_RANK), prow(ROPE_DIM), const(N_DEC, ROPE_DIM),
                       row(N_HEADS * HEAD_W), const(N_DEC, N_HEADS * V_DIM),
                       pl.BlockSpec((1, N_HEADS // 2 * VT_ROWS, TM), lambda i, p: (i, 0, 0)),
                       pl.BlockSpec((1, N_HEADS * HEAD_W, TM), lambda i, p: (i, 0, 0)),
                       const(N_DEC, N_HEADS * HEAD_W)],
            scratch_shapes=_gather_scratch(TM),
        ),
        out_shape=[
            jax.ShapeDtypeStruct((T, D_MODEL), F32),
            jax.ShapeDtypeStruct((N_PROMPT, KV_RANK), F32), jax.ShapeDtypeStruct((N_DEC, KV_RANK), F32),
            jax.ShapeDtypeStruct((N_PROMPT, ROPE_DIM), F32), jax.ShapeDtypeStruct((N_DEC, ROPE_DIM), F32),
            jax.ShapeDtypeStruct((T, N_HEADS * HEAD_W), BF16), jax.ShapeDtypeStruct((N_DEC, N_HEADS * V_DIM), BF16),
            jax.ShapeDtypeStruct((N_TILES, N_HEADS // 2 * VT_ROWS, TM), BF16),
            jax.ShapeDtypeStruct((N_TILES, N_HEADS * HEAD_W, TM), BF16),
            jax.ShapeDtypeStruct((N_DEC, N_HEADS * HEAD_W), BF16),
        ],
        compiler_params=_cparams(),
        name="mla_proj",
    )(pos, sorted_rows, cos_t, sin_t, cos_tt, sin_tt, kvn, wdkv, kvan, krg, wuk, wuv, wuv.T, kg, nmq, wdq, qan,
      wuq.T, jnp.broadcast_to(qg.reshape(HEAD_W, 1), (HEAD_W, TM)), wukt)


def _cache_kv_body(ckv_ref, kr_ref, place_ref, wuk_ref, wuv_ref, kg_ref, wukt_ref, k_ref, v_ref):
    kr = _dot(kr_ref[...].astype(BF16), place_ref[...])
    cb = ckv_ref[...].astype(BF16)
    _expand_k(cb, ckv_ref[...].T.astype(BF16), kr, wuk_ref, wukt_ref, kg_ref, k_ref)
    v_ref[...] = _dot(cb, wuv_ref[...]).astype(BF16)


CACHE_ROWS = 1024
assert PAST_LEN % CACHE_ROWS == 0


def _cache_kv(ckv, kr, wuk, wukt, wuv, kg):
    n = ckv.shape[0]
    place = jnp.asarray(np.arange(ROPE_DIM)[:, None] + ROPE_LO == np.arange(LANES)[None, :], BF16)
    const = lambda *shape: pl.BlockSpec(shape, lambda i: (0,) * len(shape))
    row = lambda w: pl.BlockSpec((CACHE_ROWS, w), lambda i: (i, 0))
    return pl.pallas_call(
        _cache_kv_body,
        grid=(n // CACHE_ROWS,),
        in_specs=[row(KV_RANK), row(ROPE_DIM), const(ROPE_DIM, LANES), const(KV_RANK, N_HEADS * HEAD_W),
                  const(KV_RANK, N_HEADS * V_DIM), const(1, LANES), const(N_HEADS * NOPE_DIM, KV_RANK)],
        out_specs=[row(N_HEADS * HEAD_W), row(N_HEADS * V_DIM)],
        out_shape=[jax.ShapeDtypeStruct((n, N_HEADS * HEAD_W), BF16), jax.ShapeDtypeStruct((n, N_HEADS * V_DIM), BF16)],
        compiler_params=_cparams(),
        name="cache_kv",
    )(ckv, kr, place, wuk, wuv, kg, wukt)


TQ = 256
TK = 256
assert TQ == TK and TQ % CHUNK == 0
SCORE_LOOKAHEAD = 8


def _qk(q, k):
    return lax.dot_general(q, k, (((1,), (1,)), ((), ())), preferred_element_type=F32)


def _merge_heads(o_ref, outs, rows):
    lane = lax.broadcasted_iota(jnp.int32, (rows, LANES), 1)
    for pr in range(N_HEADS // 2):
        o_ref[:, pr * LANES:(pr + 1) * LANES] = jnp.where(lane < V_DIM, outs[2 * pr], outs[2 * pr + 1]).astype(BF16)


def _prompt_attn_body(qt_ref, k_ref, vt_ref, rows_ref, wo_ref, out_ref, m_scr, acc_scr):
    qi = pl.program_id(1)
    m_scr[...] = jnp.full(m_scr.shape, NEG, F32)
    acc_scr[...] = jnp.zeros(acc_scr.shape, F32)

    def all_heads(j, mask):
        ks = pl.ds(pl.multiple_of(j * TK, TK), TK)

        def scores(hh):
            hcols = slice(hh * HEAD_W, (hh + 1) * HEAD_W)
            return _dot(k_ref[ks, hcols], qt_ref[0, hcols, :])

        ahead = [scores(hh) for hh in range(SCORE_LOOKAHEAD)]
        for hh in range(N_HEADS):
            vrows = slice((hh // 2) * VT_ROWS, (hh // 2 + 1) * VT_ROWS)
            s = ahead.pop(0)
            if hh + SCORE_LOOKAHEAD < N_HEADS:
                ahead.append(scores(hh + SCORE_LOOKAHEAD))
            if mask is not None:
                s = jnp.where(mask, s, NEG)
            m_old = m_scr[hh]
            m_new = jnp.maximum(m_old, jnp.max(s, axis=0, keepdims=True))
            p = jnp.exp2(s - m_new).astype(BF16)
            m_scr[hh] = m_new
            acc_scr[hh] = jnp.exp2(m_old - m_new) * acc_scr[hh] + _dot(vt_ref[j, vrows, :], p)

    def step(j, carry):
        all_heads(j, None)
        return carry

    lax.fori_loop(0, qi, step, 0)
    kc = lax.broadcasted_iota(jnp.int32, (TK, TQ), 0) // CHUNK
    qc = lax.broadcasted_iota(jnp.int32, (TK, TQ), 1) // CHUNK
    all_heads(qi, kc <= qc)
    row = lax.broadcasted_iota(jnp.int32, (LANES, TQ), 0)
    pairs = []
    for pr in range(N_HEADS // 2):
        even = acc_scr[2 * pr, :LANES] / acc_scr[2 * pr, LANES:LANES + 1]
        odd = acc_scr[2 * pr + 1, :LANES] / acc_scr[2 * pr + 1, LANES:LANES + 1]
        pairs.append(jnp.where(row < V_DIM, even, odd))
    o = jnp.concatenate(pairs, axis=0).T.astype(BF16)
    out_ref[...] = rows_ref[...] + _dot(o, wo_ref[...])


def _prompt_attn(qt, k, vt, rows, wo):
    nq = SEQ // TQ
    return pl.pallas_call(
        _prompt_attn_body,
        grid=(BATCH, nq),
        in_specs=[
            pl.BlockSpec((1, N_HEADS * HEAD_W, TQ), lambda b, i: (b * nq + i, 0, 0)),
            pl.BlockSpec((SEQ, N_HEADS * HEAD_W), lambda b, i: (b, 0)),
            pl.BlockSpec((SEQ // TK, N_HEADS // 2 * VT_ROWS, TK), lambda b, i: (b, 0, 0)),
            pl.BlockSpec((TQ, D_MODEL), lambda b, i: (b * nq + i, 0)),
            pl.BlockSpec((N_HEADS * V_DIM, D_MODEL), lambda b, i: (0, 0)),
        ],
        out_specs=pl.BlockSpec((TQ, D_MODEL), lambda b, i: (b * nq + i, 0)),
        out_shape=jax.ShapeDtypeStruct((N_PROMPT, D_MODEL), F32),
        scratch_shapes=[pltpu.VMEM((N_HEADS, 1, TQ), F32), pltpu.VMEM((N_HEADS, VT_ROWS, TQ), F32)],
        compiler_params=_cparams(2),
        name="prompt_attn",
    )(qt, k, vt, rows, wo)


def _sample_attn_body(q_ref, kc_ref, vc_ref, kn_ref, vn_ref, rows_ref, wo_ref, out_ref, o_scr):
    outs = []
    for hh in range(N_HEADS):
        hcols = slice(hh * HEAD_W, (hh + 1) * HEAD_W)
        vcols = slice((hh // 2) * LANES, (hh // 2 + 1) * LANES)
        q = q_ref[:, hcols]
        sc = _qk(q, kc_ref[:, hcols])
        sn = _qk(q, kn_ref[:, hcols])
        m = jnp.maximum(jnp.max(sc, axis=-1, keepdims=True), jnp.max(sn, axis=-1, keepdims=True))
        pc = jnp.exp2(sc - m)
        pn = jnp.exp2(sn - m)
        l = jnp.sum(pc, axis=-1, keepdims=True) + jnp.sum(pn, axis=-1, keepdims=True)
        acc = _dot(pc.astype(BF16), vc_ref[:, vcols]) + _dot(pn.astype(BF16), vn_ref[:, vcols])
        outs.append(acc / l)
    _merge_heads(o_scr, outs, DEC_SEQ)
    out_ref[...] = rows_ref[...] + _dot(o_scr[...], wo_ref[...])


def _sample_attn(q, kc, vc, kn, vn, rows, wo):
    off = N_PROMPT // DEC_SEQ
    return pl.pallas_call(
        _sample_attn_body,
        grid=(DEC_BATCH,),
        in_specs=[
            pl.BlockSpec((DEC_SEQ, N_HEADS * HEAD_W), lambda b: (b, 0)),
            pl.BlockSpec((PAST_LEN, N_HEADS * HEAD_W), lambda b: (b, 0)),
            pl.BlockSpec((PAST_LEN, N_HEADS * V_DIM), lambda b: (b, 0)),
            pl.BlockSpec((DEC_SEQ, N_HEADS * HEAD_W), lambda b: (off + b, 0)),
            pl.BlockSpec((DEC_SEQ, N_HEADS * V_DIM), lambda b: (b, 0)),
            pl.BlockSpec((DEC_SEQ, D_MODEL), lambda b: (off + b, 0)),
            pl.BlockSpec((N_HEADS * V_DIM, D_MODEL), lambda b: (0, 0)),
        ],
        out_specs=pl.BlockSpec((DEC_SEQ, D_MODEL), lambda b: (b, 0)),
        out_shape=jax.ShapeDtypeStruct((N_DEC, D_MODEL), F32),
        scratch_shapes=[pltpu.VMEM((DEC_SEQ, N_HEADS * V_DIM), BF16)],
        compiler_params=_cparams(),
        name="sample_attn",
    )(q, kc, vc, kn, vn, rows, wo)


def _finish_body(pos_ref, sorted_hbm, yp_ref, ys_ref, xbuf, xsem):
    i = pl.program_id(0)
    x_ref = _gather_tile(pos_ref, sorted_hbm, xbuf, xsem, TM)

    def relayout(dst_ref):
        for s in range(ROW_TILES):
            dst_ref[:, s * LANES:(s + 1) * LANES] = x_ref[pl.ds(s, TM, stride=ROW_TILES), :]

    @pl.when(i < N_TILES - 1)
    def _():
        relayout(yp_ref)

    @pl.when(i == N_TILES - 1)
    def _():
        relayout(ys_ref)


def _finish(pos, sorted_rows):
    return pl.pallas_call(
        _finish_body,
        grid_spec=pltpu.PrefetchScalarGridSpec(
            num_scalar_prefetch=1,
            grid=(N_TILES,),
            in_specs=[pl.BlockSpec(memory_space=pl.ANY)],
            out_specs=[pl.BlockSpec((TM, D_MODEL), lambda i, p: (jnp.minimum(i, N_TILES - 2), 0)),
                       pl.BlockSpec((N_DEC, D_MODEL), lambda i, p: (0, 0))],
            scratch_shapes=_gather_scratch(TM),
        ),
        out_shape=[jax.ShapeDtypeStruct((N_PROMPT, D_MODEL), F32), jax.ShapeDtypeStruct((N_DEC, D_MODEL), F32)],
        compiler_params=_cparams(),
        name="finish",
    )(pos, sorted_rows)


def _rope_tables():
    half = ROPE_DIM // 2
    inv_freq = ROPE_THETA ** (-jnp.arange(half, dtype=F32) / half)
    dec_pos = PAST_LEN + jnp.tile(jnp.arange(DEC_SEQ, dtype=jnp.int32), DEC_BATCH)
    pos = jnp.concatenate([jnp.arange(SEQ, dtype=jnp.int32), dec_pos])
    ang = pos.astype(F32)[:, None] * inv_freq[None, :]
    cos, sin = jnp.cos(ang), jnp.sin(ang)
    n = pos.shape[0]
    cos_t = jnp.ones((n, LANES), F32).at[:, ROPE_LO:ROPE_LO + ROPE_DIM].set(jnp.concatenate([cos, cos], axis=1))
    sin_t = jnp.zeros((n, LANES), F32).at[:, ROPE_LO:ROPE_LO + ROPE_DIM].set(jnp.concatenate([-sin, sin], axis=1))
    to_tiles = lambda a: a.reshape(n // TM, TM, half).transpose(0, 2, 1)
    return cos_t, sin_t, to_tiles(cos), to_tiles(sin)


def _on_lanes(vec, lo):
    return jnp.zeros((1, LANES), F32).at[0, lo:lo + vec.shape[0]].set(vec)


def kernel(x_prompt, x_sample, cache_ckv, cache_krope, norm_mix, norm_ffn, gm_w_in, gm_b_in, gm_ln_g, gm_ln_b, gm_w_s, gm_b_s, gm_w_out, gm_b_out, kv_norm, w_dkv, kv_a_norm, k_rope_norm, w_uk, w_uv, k_nope_norm, w_dq, q_a_norm, w_uq, q_nope_norm, q_rope_norm, w_o, moe_w_group, moe_b_group, moe_w_expert, moe_b_expert, moe_w1, moe_w3, moe_w2):
    nf0, nf1 = norm_ffn[0].reshape(1, D_MODEL), norm_ffn[1].reshape(1, D_MODEL)

    idx = np.arange(GMLP_BLOCK)
    allowed = (idx[None, :] // CHUNK) <= (idx[:, None] // CHUNK)
    ws_p = jnp.where(allowed[None], gm_w_s[0], 0.0).astype(BF16)
    same_seq = (idx[None, :] // DEC_SEQ) == (idx[:, None] // DEC_SEQ)
    ws_d = jnp.where(same_seq[None], jnp.tile(gm_w_s[0][:, :DEC_SEQ, :DEC_SEQ], (1, GMLP_BLOCK // DEC_SEQ, GMLP_BLOCK // DEC_SEQ)), 0.0).astype(BF16)
    bs_p = gm_b_s[0][:, :, None]
    bs_d = jnp.tile(gm_b_s[0][:, :DEC_SEQ], (1, GMLP_BLOCK // DEC_SEQ))[:, :, None]
    rows, v_rows, info, cnt = _gmlp_layer(
        x_prompt.reshape(N_PROMPT, D_MODEL), x_sample.reshape(N_DEC, D_MODEL),
        norm_mix[0].reshape(1, -1), gm_w_in[0].astype(BF16), gm_b_in[0].reshape(1, -1),
        gm_ln_g[0].reshape(1, -1), gm_ln_b[0].reshape(1, -1), ws_p, ws_d, bs_p, bs_d,
        gm_w_out[0].astype(BF16), gm_b_out[0].reshape(1, -1),
        nf0, *_router_weights(0, moe_w_group, moe_b_group, moe_w_expert, moe_b_expert))
    sorted_rows, pos = _moe_layer(0, rows, info, cnt, nf0, moe_w1, moe_w3, moe_w2)

    cos_t, sin_t, cos_tt, sin_tt = _rope_tables()
    wdkv = jnp.zeros((D_MODEL, KV_RANK + LANES), F32).at[:, :KV_RANK].set(w_dkv[:, :KV_RANK])
    wdkv = wdkv.at[:, KV_RANK + ROPE_LO:KV_RANK + ROPE_LO + ROPE_DIM].set(w_dkv[:, KV_RANK:]).astype(BF16)
    wuk = jnp.zeros((KV_RANK, N_HEADS, HEAD_W), F32).at[:, :, :NOPE_DIM].set(w_uk).reshape(KV_RANK, -1).astype(BF16)
    wukt = w_uk.reshape(KV_RANK, -1).T.astype(BF16)
    wuv = w_uv.reshape(KV_RANK, -1).astype(BF16)
    wuq =jnp.zeros((Q_RANK, N_HEADS, HEAD_W), F32).at[:, :, :NOPE_DIM + ROPE_DIM].set(w_uq[0]).reshape(Q_RANK, -1).astype(BF16)
    kg = _on_lanes(k_nope_norm, 0)
    krg = _on_lanes(k_rope_norm, ROPE_LO)
    qg = _on_lanes(jnp.concatenate([q_nope_norm[0], q_rope_norm[0]]), 0)
    rows, ckv_p, ckv_d, krope_p, krope_d, k_new, v_dec, vt_new, qt, q_dec = _mla_proj(
        pos, sorted_rows, cos_t, sin_t, cos_tt, sin_tt, kv_norm.reshape(1, -1), wdkv, kv_a_norm.reshape(1, -1), krg, wuk, wukt, wuv, kg,
        norm_mix[1].reshape(1, -1), w_dq[0].astype(BF16), q_a_norm[0].reshape(1, -1), wuq, qg)
    k_cache, v_cache = _cache_kv(cache_ckv.reshape(-1, KV_RANK), cache_krope.reshape(-1, ROPE_DIM), wuk, wukt, wuv, kg)

    wo = w_o[0].astype(BF16)
    h_prompt = _prompt_attn(qt, k_new, vt_new, rows, wo)
    h_dec = _sample_attn(q_dec, k_cache, v_cache, k_new, v_dec, rows, wo)
    rows, info, cnt = _router(h_prompt, h_dec, nf1,
                              *_router_weights(1, moe_w_group, moe_b_group, moe_w_expert, moe_b_expert))
    sorted_rows, pos = _moe_layer(1, rows, info, cnt, nf1, moe_w1, moe_w3, moe_w2)
    y_prompt, y_sample = _finish(pos, sorted_rows)

    return (y_prompt.reshape(BATCH, SEQ, D_MODEL), y_sample.reshape(DEC_BATCH, DEC_SEQ, D_MODEL),
            ckv_p.reshape(BATCH, SEQ, KV_RANK), krope_p.reshape(BATCH, SEQ, ROPE_DIM),
            ckv_d.reshape(DEC_BATCH, DEC_SEQ, KV_RANK), krope_d.reshape(DEC_BATCH, DEC_SEQ, ROPE_DIM),
            v_rows.reshape(1, DEC_BATCH, DEC_SEQ, D_GATE))
```

```python
import functools

import jax
import jax.numpy as jnp
import numpy as np
from jax import lax
from jax.experimental import pallas as pl
from jax.experimental.pallas import tpu as pltpu

F32 = jnp.float32
BF16 = jnp.bfloat16

D_MODEL = 1024
BATCH = 8
SEQ = 2048
DEC_BATCH = 16
DEC_SEQ = 16
PAST_LEN = 2048
CHUNK = 64
GMLP_BLOCK = 128
D_GATE = 2 * D_MODEL
N_SG = 8
SG_W = D_GATE // N_SG
N_HEADS = 8
NOPE_DIM = 64
ROPE_DIM = 32
V_DIM = 64
Q_RANK = 384
KV_RANK = 256
ROPE_THETA = 10000.0
SCALE = (NOPE_DIM + ROPE_DIM) ** -0.5
Q_SCALE = SCALE * float(np.log2(np.e))
N_EGROUPS = 4
EXPERTS_PER_GROUP = 4
N_EXPERTS = N_EGROUPS * EXPERTS_PER_GROUP
D_EXPERT = 512
EPS = 1e-6
NEG = -1e30

LANES = 128
SUBLANES = 8
ROW_TILES = D_MODEL // LANES
assert ROW_TILES == SUBLANES

N_PROMPT = BATCH * SEQ
N_DEC = DEC_BATCH * DEC_SEQ
T = N_PROMPT + N_DEC
TM = 256
assert N_PROMPT % TM == 0 and N_DEC == TM
N_TILES = T // TM
HEAD_W = LANES
ROPE_LO = NOPE_DIM
ROPE_HALF = ROPE_DIM // 2

PAIR_A = (0, 0, 0, 1, 1, 3)
PAIR_B = (1, 2, 3, 3, 2, 2)
N_PAIRS = 6
N_BUCKETS = N_EGROUPS * N_PAIRS
BUCKET_ROWS = 32
assert N_BUCKETS <= BUCKET_ROWS and BUCKET_ROWS % SUBLANES == 0
MOE_TILES = (T + N_BUCKETS * (TM - 1) + TM - 1) // TM
P_ROWS = MOE_TILES * TM

VMEM_LIMIT = 56 * 1024 * 1024


def _cparams(n_axes=1, vmem=VMEM_LIMIT):
    return pltpu.CompilerParams(dimension_semantics=("arbitrary",) * n_axes, vmem_limit_bytes=vmem)


def _rms(x, g):
    return x * lax.rsqrt(jnp.mean(x * x, axis=-1, keepdims=True) + EPS) * g


def _load_rows(ref, n):
    return jnp.concatenate([ref[pl.ds(s, n, stride=ROW_TILES), :] for s in range(ROW_TILES)], axis=1)


def _store_rows(ref, x, n):
    for s in range(ROW_TILES):
        ref[pl.ds(s, n, stride=ROW_TILES), :] = x[:, s * LANES:(s + 1) * LANES]


def _dot(a, b):
    return jnp.dot(a, b, preferred_element_type=F32)


GELU_K1 = float(-2.0 * np.sqrt(2.0 / np.pi) * np.log2(np.e))
GELU_K3 = GELU_K1 * 0.044715


def _gmlp_body(xp_ref, xd_ref, nm_ref, win_ref, bin_ref, lng_ref, lnb_ref, wsp_ref, wsd_ref, bsp_ref, bsd_ref,
               wout_ref, bout_ref, nf_ref, wrh_ref, wr2_ref, br_ref,
               rows_ref, v_ref, info_ref, cnt_ref, gated_ref, carry_ref):
    i = pl.program_id(0)
    is_dec = i == N_TILES - 1
    _route_init(carry_ref)
    x = jnp.where(is_dec, xd_ref[...], xp_ref[...])
    xn = _rms(x, nm_ref[...]).astype(BF16)
    z = _dot(xn, win_ref[...]) + bin_ref[...]
    z = z / (1.0 + jnp.exp2(z * (GELU_K1 + GELU_K3 * (z * z))))
    u = z[:, :D_GATE]
    v = z[:, D_GATE:]
    mu = jnp.mean(v, axis=-1, keepdims=True)
    vc = v - mu
    var = jnp.mean(vc * vc, axis=-1, keepdims=True)
    v = vc * lax.rsqrt(var + EPS) * lng_ref[...] + lnb_ref[...]

    v_ref[...] = v
    vb = v.astype(BF16)
    for g in range(N_SG):
        ws = jnp.where(is_dec, wsd_ref[g], wsp_ref[g])
        bs = jnp.where(is_dec, bsd_ref[g], bsp_ref[g])
        for b in range(TM // GMLP_BLOCK):
            rows = slice(b * GMLP_BLOCK, (b + 1) * GMLP_BLOCK)
            cols = slice(g * SG_W, (g + 1) * SG_W)
            s = _dot(ws, vb[rows, cols]) + bs
            gated_ref[rows, cols] = (u[rows, cols] * s).astype(BF16)
    h = x + _dot(gated_ref[...], wout_ref[...]) + bout_ref[...]
    _store_rows(rows_ref, h, TM)
    _route_tile(h, nf_ref, wrh_ref, wr2_ref, br_ref, info_ref, cnt_ref, carry_ref)


def _gmlp_layer(x_prompt, x_dec, nm, w_in, b_in, ln_g, ln_b, ws_p, ws_d, bs_p, bs_d, w_out, b_out, nf, wr, br):
    const = lambda *shape: pl.BlockSpec(shape, lambda i: (0,) * len(shape))
    return pl.pallas_call(
        _gmlp_body,
        grid=(N_TILES,),
        in_specs=[
            pl.BlockSpec((TM, D_MODEL), lambda i: (jnp.minimum(i, N_TILES - 2), 0)), const(N_DEC, D_MODEL),
            const(1, D_MODEL), const(D_MODEL, 2 * D_GATE), const(1, 2 * D_GATE),
            const(1, D_GATE), const(1, D_GATE),
            const(N_SG, GMLP_BLOCK, GMLP_BLOCK), const(N_SG, GMLP_BLOCK, GMLP_BLOCK),
            const(N_SG, GMLP_BLOCK, 1), const(N_SG, GMLP_BLOCK, 1),
            const(D_GATE, D_MODEL), const(1, D_MODEL),
            const(1, D_MODEL), const(D_MODEL, LANES), const(D_MODEL, 2 * LANES), const(1, LANES),
        ],
        out_specs=[
            pl.BlockSpec((TM * ROW_TILES, LANES), lambda i: (i, 0)),
            const(N_DEC, D_GATE),
            pl.BlockSpec((SUBLANES, TM), lambda i: (0, i)),
            const(BUCKET_ROWS, LANES),
        ],
        out_shape=[jax.ShapeDtypeStruct((T * ROW_TILES, LANES), F32), jax.ShapeDtypeStruct((N_DEC, D_GATE), F32),
                   jax.ShapeDtypeStruct((SUBLANES, T), F32), jax.ShapeDtypeStruct((BUCKET_ROWS, LANES), F32)],
        scratch_shapes=[pltpu.VMEM((TM, D_GATE), BF16), pltpu.VMEM((BUCKET_ROWS, LANES), F32)],
        compiler_params=_cparams(),
        name="gmlp_layer",
    )(x_prompt, x_dec, nm, w_in, b_in, ln_g, ln_b, ws_p, ws_d, bs_p, bs_d, w_out, b_out,
      nf, *_router_split(wr), br)


def _router_body(hp_ref, hd_ref, nf_ref, wrh_ref, wr2_ref, br_ref, rows_ref, info_ref, cnt_ref, carry_ref):
    _route_init(carry_ref)
    h = jnp.where(pl.program_id(0) == N_TILES - 1, hd_ref[...], hp_ref[...])
    _store_rows(rows_ref, h, TM)
    _route_tile(h, nf_ref, wrh_ref, wr2_ref, br_ref, info_ref, cnt_ref, carry_ref)


def _router_split(wr):
    hi = wr.astype(BF16)
    return hi, jnp.concatenate([hi, (wr - hi.astype(F32)).astype(BF16)], axis=1)


def _route_init(carry_ref):
    @pl.when(pl.program_id(0) == 0)
    def _():
        carry_ref[...] = jnp.zeros_like(carry_ref)


def _route_tile(h, nf_ref, wrh_ref, wr2_ref, br_ref, info_ref, cnt_ref, carry_ref):
    xn = _rms(h, nf_ref[...])
    xh = xn.astype(BF16)
    xl = (xn - xh.astype(F32)).astype(BF16)
    both = _dot(xh, wr2_ref[...])
    logits = both[:, :LANES] + (_dot(xl, wrh_ref[...]) + both[:, LANES:]) + br_ref[...]
    lt = logits.T

    def first_max(vals, ids):
        vmax = jnp.max(vals, axis=0, keepdims=True)
        return vmax, jnp.min(jnp.where(vals == vmax, ids, float(LANES)), axis=0, keepdims=True)

    lg = lt[N_EXPERTS:N_EXPERTS + N_EGROUPS]
    gmax, g_idx = first_max(lg, lax.broadcasted_iota(jnp.int32, lg.shape, 0).astype(F32))
    g_p = 1.0 / jnp.sum(jnp.exp(lg - gmax), axis=0, keepdims=True)
    e_id = lax.broadcasted_iota(jnp.int32, (N_EXPERTS, TM), 0)
    in_group = (e_id // EXPERTS_PER_GROUP).astype(F32) == g_idx
    e_id = e_id.astype(F32)
    le = jnp.where(in_group, lt[:N_EXPERTS], -jnp.inf)
    v1, i1 = first_max(le, e_id)
    v2, i2 = first_max(jnp.where(e_id == i1, -jnp.inf, le), e_id)
    e2 = jnp.exp(v2 - v1)
    w1 = (1.0 / (1.0 + e2)) * g_p
    w2 = (e2 / (1.0 + e2)) * g_p
    a1 = i1 - EXPERTS_PER_GROUP * g_idx
    a2 = i2 - EXPERTS_PER_GROUP * g_idx
    lo = jnp.minimum(a1, a2)
    hi = jnp.maximum(a1, a2)
    pair = jnp.where(lo == 0.0, hi - 1.0, jnp.where(lo == 1.0, jnp.where(hi == 3.0, 3.0, 4.0), 5.0))
    ea = jnp.where(pair < 3.0, 0.0, jnp.where(pair < 5.0, 1.0, 3.0))
    ga = jnp.where(a1 == ea, w1, w2)
    gb = jnp.where(a1 == ea, w2, w1)
    bucket = g_idx * N_PAIRS + pair

    onehot = (lax.broadcasted_iota(jnp.int32, (BUCKET_ROWS, TM), 0).astype(F32) == bucket).astype(F32)
    r = lax.broadcasted_iota(jnp.int32, (TM, TM), 0)
    c = lax.broadcasted_iota(jnp.int32, (TM, TM), 1)
    before = _dot(onehot.astype(BF16), (r < c).astype(BF16))
    carry = carry_ref[:, 0:1]
    rank = jnp.sum(onehot * (before + carry), axis=0, keepdims=True)
    new_carry = carry + jnp.sum(onehot, axis=1, keepdims=True)
    carry_ref[...] = jnp.broadcast_to(new_carry, carry_ref.shape)
    cnt_ref[...] = jnp.broadcast_to(new_carry, cnt_ref.shape)
    info_ref[...] = jnp.concatenate([bucket, rank, ga, gb, jnp.zeros((SUBLANES - 4, TM), F32)], axis=0)


def _router(h_prompt, h_dec, nf, wr, br):
    const = lambda *shape: pl.BlockSpec(shape, lambda i: (0,) * len(shape))
    return pl.pallas_call(
        _router_body,
        grid=(N_TILES,),
        in_specs=[pl.BlockSpec((TM, D_MODEL), lambda i: (jnp.minimum(i, N_TILES - 2), 0)), const(N_DEC, D_MODEL),
                  const(1, D_MODEL), const(D_MODEL, LANES), const(D_MODEL, 2 * LANES), const(1, LANES)],
        out_specs=[pl.BlockSpec((TM * ROW_TILES, LANES), lambda i: (i, 0)),
                   pl.BlockSpec((SUBLANES, TM), lambda i: (0, i)),
                   const(BUCKET_ROWS, LANES)],
        out_shape=[jax.ShapeDtypeStruct((T * ROW_TILES, LANES), F32), jax.ShapeDtypeStruct((SUBLANES, T), F32),
                   jax.ShapeDtypeStruct((BUCKET_ROWS, LANES), F32)],
        scratch_shapes=[pltpu.VMEM((BUCKET_ROWS, LANES), F32)],
        compiler_params=_cparams(),
        name="moe_router",
    )(h_prompt, h_dec, nf, *_router_split(wr), br)


GATHER_UNROLL = 64


def _gather_rows_start(idx_ref, base, src_ref, dst_ref, sem, n):
    def group(g, carry):
        for u in range(GATHER_UNROLL):
            r = g * GATHER_UNROLL + u
            src = pl.multiple_of(idx_ref[base + r] * ROW_TILES, ROW_TILES)
            dst = pl.multiple_of(r * ROW_TILES, ROW_TILES)
            pltpu.make_async_copy(src_ref.at[pl.ds(src, ROW_TILES), :], dst_ref.at[pl.ds(dst, ROW_TILES), :],
                                  sem).start(priority=u % 2)
        return carry

    lax.fori_loop(0, n // GATHER_UNROLL, group, 0)


def _gather_rows_wait(src_ref, dst_ref, sem, n):
    pltpu.make_async_copy(src_ref.at[pl.ds(0, n * ROW_TILES), :], dst_ref.at[pl.ds(0, n * ROW_TILES), :], sem).wait()


def _gather_tile(idx_ref, src_ref, buf, sem, n, n_live=None, base_of=None, side_copy=None):
    i = pl.program_id(0)
    slot = lax.rem(i, 2)
    n_live = pl.num_programs(0) if n_live is None else n_live
    base_of = (lambda step: step * n) if base_of is None else base_of

    def start(step, s):
        _gather_rows_start(idx_ref, base_of(step), src_ref, buf.at[s], sem.at[s], n)
        if side_copy is not None:
            side_copy(step, s).start()

    @pl.when(i == 0)
    def _():
        start(0, 0)

    @pl.when(i + 1 < n_live)
    def _():
        start(i + 1, 1 - slot)

    @pl.when(i < n_live)
    def _():
        _gather_rows_wait(src_ref, buf.at[slot], sem.at[slot], n)
        if side_copy is not None:
            side_copy(i, slot).wait()

    return buf.at[slot]


def _gather_scratch(n):
    return [pltpu.VMEM((2, n * ROW_TILES, LANES), F32), pltpu.SemaphoreType.DMA((2,))]


GATE_WIN = TM + SUBLANES


def _moe_body(tok_ref, ea_ref, eb_ref, cha_ref, chb_ref, nlive_ref, first_ref, nrows_ref, rows_hbm, gates_hbm, nf_ref,
              w1a_ref, w3a_ref, w2a_ref, w1b_ref, w3b_ref, w2b_ref, out_ref,
              s1a, s3a, s2a, s1b, s3b, s2b, xbuf, xsem, gwin):
    i = pl.program_id(0)
    live = i < nlive_ref[0]

    def gate_copy(step, s):
        lo = pl.multiple_of((first_ref[step] // SUBLANES) * SUBLANES, SUBLANES)
        return pltpu.make_async_copy(gates_hbm.at[pl.ds(lo, GATE_WIN), :], gwin.at[s], xsem.at[s])

    x_ref = _gather_tile(tok_ref, rows_hbm, xbuf, xsem, TM, nlive_ref[0],
                         base_of=lambda step: first_ref[step], side_copy=gate_copy)

    @pl.when(cha_ref[i] == 1)
    def _():
        s1a[...] = w1a_ref[...].astype(BF16)
        s3a[...] = w3a_ref[...].astype(BF16)
        s2a[...] = w2a_ref[...].astype(BF16)

    @pl.when(chb_ref[i] == 1)
    def _():
        s1b[...] = w1b_ref[...].astype(BF16)
        s3b[...] = w3b_ref[...].astype(BF16)
        s2b[...] = w2b_ref[...].astype(BF16)

    @pl.when(live)
    def _():
        h = _load_rows(x_ref, TM)
        xn = _rms(h, nf_ref[...]).astype(BF16)
        g = gwin[lax.rem(i, 2), pl.ds(lax.rem(first_ref[i], SUBLANES), TM), :]
        row = lax.broadcasted_iota(jnp.int32, (TM, 1), 0)
        g = jnp.where(row < nrows_ref[i], g, 0.0)

        def ffn(w1, w3, w2):
            a = _dot(xn, w1[...])
            hdn = (a * (1.0 / (1.0 + jnp.exp(-a)))) * _dot(xn, w3[...])
            return _dot(hdn.astype(BF16), w2[...])

        y = g[:, 0:1] * ffn(s1a, s3a, s2a) + g[:, 1:2] * ffn(s1b, s3b, s2b)
        _store_rows(out_ref, h + y, TM)

    @pl.when(jnp.logical_not(live))
    def _():
        out_ref[...] = jnp.zeros_like(out_ref)


def _moe_ffn(layer, tok, ea, eb, cha, chb, n_live, first, nrows, rows, gates, nf, w1, w3, w2):
    wa = lambda shape: pl.BlockSpec((None, None) + shape, lambda i, tk, ea, eb, *_: (layer, ea[i], 0, 0))
    wb = lambda shape: pl.BlockSpec((None, None) + shape, lambda i, tk, ea, eb, *_: (layer, eb[i], 0, 0))
    up, down = (D_MODEL, D_EXPERT), (D_EXPERT, D_MODEL)
    return pl.pallas_call(
        _moe_body,
        grid_spec=pltpu.PrefetchScalarGridSpec(
            num_scalar_prefetch=8,
            grid=(MOE_TILES,),
            in_specs=[
                pl.BlockSpec(memory_space=pl.ANY),
                pl.BlockSpec(memory_space=pl.ANY),
                pl.BlockSpec((1, D_MODEL), lambda i, *_: (0, 0)),
                wa(up), wa(up), wa(down), wb(up), wb(up), wb(down),
            ],
            out_specs=pl.BlockSpec((TM * ROW_TILES, LANES), lambda i, *_: (i, 0)),
            scratch_shapes=[pltpu.VMEM(up, BF16), pltpu.VMEM(up, BF16), pltpu.VMEM(down, BF16),
                            pltpu.VMEM(up, BF16), pltpu.VMEM(up, BF16), pltpu.VMEM(down, BF16)]
            + _gather_scratch(TM) + [pltpu.VMEM((2, GATE_WIN, 2), F32)],
        ),
        out_shape=jax.ShapeDtypeStruct((P_ROWS * ROW_TILES, LANES), F32),
        compiler_params=_cparams(),
        name="moe_ffn",
    )(tok, ea, eb, cha, chb, n_live, first, nrows, rows, gates, nf, w1, w3, w2, w1, w3, w2)


def _router_weights(layer, w_group, b_group, w_expert, b_expert):
    pad = LANES - N_EXPERTS - N_EGROUPS
    wr = jnp.concatenate([w_expert[layer], w_group[layer], jnp.zeros((D_MODEL, pad), F32)], axis=1)
    br = jnp.concatenate([b_expert[layer], b_group[layer], jnp.zeros((pad,), F32)]).reshape(1, LANES)
    return wr, br


def _moe_layer(layer, rows, info, cnt, nf, w1, w3, w2):
    bucket = info[0].astype(jnp.int32)
    rank = info[1].astype(jnp.int32)
    counts = cnt[:N_BUCKETS, 0].astype(jnp.int32)
    n_tiles = (counts + TM - 1) // TM
    tile_end = jnp.cumsum(n_tiles)
    tile_start = tile_end - n_tiles
    start_of = jnp.sum(jnp.where(bucket[:, None] == jnp.arange(N_BUCKETS)[None, :], tile_start[None, :], 0), axis=1)
    pos = start_of * TM + rank
    _, tok, ga, gb = lax.sort((pos, jnp.arange(T, dtype=jnp.int32), info[2], info[3]), num_keys=1)
    tok = jnp.concatenate([tok, jnp.zeros((TM,), jnp.int32)])
    gates = jnp.concatenate([jnp.stack([ga, gb], axis=1), jnp.zeros((TM + SUBLANES, 2), F32)])
    total = tile_end[-1]
    j = jnp.minimum(jnp.arange(MOE_TILES), total - 1)
    tb = jnp.sum((j[:, None] >= tile_end[None, :]).astype(jnp.int32), axis=1)
    tokens_before = jnp.cumsum(counts) - counts
    in_bucket = (j - tile_start[tb]) * TM
    first = (tokens_before[tb] + in_bucket).astype(jnp.int32)
    nrows = jnp.clip(counts[tb] - in_bucket, 0, TM).astype(jnp.int32)
    grp, pair = tb // N_PAIRS, tb % N_PAIRS
    ea = (grp * EXPERTS_PER_GROUP + jnp.asarray(PAIR_A, jnp.int32)[pair]).astype(jnp.int32)
    eb = (grp * EXPERTS_PER_GROUP + jnp.asarray(PAIR_B, jnp.int32)[pair]).astype(jnp.int32)
    tile0 = jnp.arange(MOE_TILES) == 0
    cha = (tile0 | (ea != jnp.roll(ea, 1))).astype(jnp.int32)
    chb = (tile0 | (eb != jnp.roll(eb, 1))).astype(jnp.int32)
    n_live = total.reshape(1).astype(jnp.int32)
    return _moe_ffn(layer, tok, ea, eb, cha, chb, n_live, first, nrows, rows, gates, nf, w1, w3, w2), pos


VT_ROWS = LANES + 16


def _rope_swap(x):
    lane = lax.broadcasted_iota(jnp.int32, x.shape, 1)
    return jnp.where(lane < ROPE_LO + ROPE_HALF, pltpu.roll(x, LANES - ROPE_HALF, 1), pltpu.roll(x, ROPE_HALF, 1))


def _expand_k(cb, cbt, kr, wuk_ref, wukt_ref, kg_ref, k_ref):
    n = cb.shape[0]
    kn = _dot(cb, wuk_ref[...])
    knt = _dot(wukt_ref[...], cbt)
    ms = [jnp.mean(jnp.square(knt[hh * NOPE_DIM:(hh + 1) * NOPE_DIM]), axis=0, keepdims=True)
          for hh in range(N_HEADS)]
    scale = lax.rsqrt(jnp.concatenate(ms + [jnp.ones((LANES - N_HEADS, n), F32)], axis=0) + EPS).T
    for hh in range(N_HEADS):
        cols = slice(hh * HEAD_W, (hh + 1) * HEAD_W)
        k_ref[:, cols] = (kn[:, cols] * scale[:, hh:hh + 1] * kg_ref[...] + kr).astype(BF16)


def _mla_proj_body(pos_ref, sorted_hbm, cos_ref, sin_ref, cost_ref, sint_ref, kvn_ref, wdkv_ref, kvan_ref, krg_ref,
                   wuk_ref, wuv_ref, wuvt_ref, kg_ref, nmq_ref, wdq_ref, qan_ref, wuqt_ref, qg_ref, wukt_ref,
                   rows_ref, ckvp_ref, ckvd_ref, krp_ref, krd_ref, k_ref, vdec_ref, vt_ref, qt_ref, qdec_ref,
                   xbuf, xsem):
    is_dec = pl.program_id(0) == N_TILES - 1
    x_ref = _gather_tile(pos_ref, sorted_hbm, xbuf, xsem, TM)
    h = _load_rows(x_ref, TM)
    rows_ref[...] = h
    hn = h * lax.rsqrt(jnp.mean(h * h, axis=-1, keepdims=True) + EPS)
    c = _dot((hn * kvn_ref[...]).astype(BF16), wdkv_ref[...])
    ckv = _rms(c[:, :KV_RANK], kvan_ref[...])
    kr = c[:, KV_RANK:]
    kr = kr * lax.rsqrt(jnp.sum(kr * kr, axis=-1, keepdims=True) * (1.0 / ROPE_DIM) + EPS) * krg_ref[...]
    kr = kr * cos_ref[...] + _rope_swap(kr) * sin_ref[...]
    cb = ckv.astype(BF16)
    cbt = ckv.T.astype(BF16)
    _expand_k(cb, cbt, kr, wuk_ref, wukt_ref, kg_ref, k_ref)
    vt = _dot(wuvt_ref[...], cbt).astype(BF16)
    for pr in range(N_HEADS // 2):
        vt_ref[0, pr * VT_ROWS:pr * VT_ROWS + LANES, :] = vt[pr * LANES:(pr + 1) * LANES]
        vt_ref[0, pr * VT_ROWS + LANES:(pr + 1) * VT_ROWS, :] = jnp.ones((VT_ROWS - LANES, TM), BF16)
    cq = _rms(_dot((hn * nmq_ref[...]).astype(BF16), wdq_ref[...]), qan_ref[...])
    qt = _dot(wuqt_ref[...], cq.T.astype(BF16))
    cost, sint, qg = cost_ref[0], sint_ref[0], qg_ref[...]
    for hh in range(N_HEADS):
        x = qt[hh * HEAD_W:(hh + 1) * HEAD_W, :]
        xn, xr = x[:NOPE_DIM], x[ROPE_LO:ROPE_LO + ROPE_DIM]
        xn = xn * lax.rsqrt(jnp.mean(xn * xn, axis=0, keepdims=True) + EPS) * qg[:NOPE_DIM]
        xr = xr * lax.rsqrt(jnp.mean(xr * xr, axis=0, keepdims=True) + EPS) * qg[ROPE_LO:ROPE_LO + ROPE_DIM]
        x1, x2 = xr[:ROPE_HALF], xr[ROPE_HALF:]
        qh = jnp.concatenate([xn, x1 * cost - x2 * sint, x1 * sint + x2 * cost,
                              jnp.zeros((HEAD_W - NOPE_DIM - ROPE_DIM, TM), F32)], axis=0) * Q_SCALE
        qt_ref[0, hh * HEAD_W:(hh + 1) * HEAD_W, :] = qh.astype(BF16)

    @pl.when(jnp.logical_not(is_dec))
    def _():
        ckvp_ref[...] = ckv
        krp_ref[...] = kr[:, ROPE_LO:ROPE_LO + ROPE_DIM]

    @pl.when(is_dec)
    def _():
        ckvd_ref[...] = ckv
        krd_ref[...] = kr[:, ROPE_LO:ROPE_LO + ROPE_DIM]
        vdec_ref[...] = _dot(cb, wuv_ref[...]).astype(BF16)
        for hh in range(N_HEADS):
            rows = slice(hh * HEAD_W, (hh + 1) * HEAD_W)
            qdec_ref[:, rows] = qt_ref[0, rows, :].astype(F32).T.astype(BF16)


def _mla_proj(pos, sorted_rows, cos_t, sin_t, cos_tt, sin_tt, kvn, wdkv, kvan, krg, wuk, wukt, wuv, kg, nmq, wdq, qan, wuq, qg):
    const = lambda *shape: pl.BlockSpec(shape, lambda i, p: (0,) * len(shape))
    tab_tile = lambda i: jnp.where(i < N_PROMPT // TM, i % (SEQ // TM), SEQ // TM)
    tab = pl.BlockSpec((TM, LANES), lambda i, p: (tab_tile(i), 0))
    tab_t = pl.BlockSpec((1, ROPE_HALF, TM), lambda i, p: (tab_tile(i), 0, 0))
    row = lambda w: pl.BlockSpec((TM, w), lambda i, p: (i, 0))
    prow = lambda w: pl.BlockSpec((TM, w), lambda i, p: (jnp.minimum(i, N_TILES - 2), 0))
    return pl.pallas_call(
        _mla_proj_body,
        grid_spec=pltpu.PrefetchScalarGridSpec(
            num_scalar_prefetch=1,
            grid=(N_TILES,),
            in_specs=[
                pl.BlockSpec(memory_space=pl.ANY), tab, tab, tab_t, tab_t,
                const(1, D_MODEL), const(D_MODEL, KV_RANK + LANES), const(1, KV_RANK), const(1, LANES),
                const(KV_RANK, N_HEADS * HEAD_W), const(KV_RANK, N_HEADS * V_DIM), const(N_HEADS * V_DIM, KV_RANK),
                const(1, LANES),
                const(1, D_MODEL), const(D_MODEL, Q_RANK), const(1, Q_RANK), const(N_HEADS * HEAD_W, Q_RANK),
                const(HEAD_W, TM), const(N_HEADS * NOPE_DIM, KV_RANK),
            ],
            out_specs=[row(D_MODEL),
                       prow(KV_RANK), const(N_DEC, KV_RANK), prow(ROPE_DIM), const(N_DEC, ROPE_DIM),
                       row(N_HEADS * HEAD_W), const(N_DEC, N_HEADS * V_DIM),
                       pl.BlockSpec((1, N_HEADS // 2 * VT_ROWS, TM), lambda i, p: (i, 0, 0)),
                       pl.BlockSpec((1, N_HEADS * HEAD_W, TM), lambda i, p: (i, 0, 0)),
                       const(N_DEC, N_HEADS * HEAD_W)],
            scratch_shapes=_gather_scratch(TM),
        ),
        out_shape=[
            jax.ShapeDtypeStruct((T, D_MODEL), F32),
            jax.ShapeDtypeStruct((N_PROMPT, KV_RANK), F32), jax.ShapeDtypeStruct((N_DEC, KV_RANK), F32),
            jax.ShapeDtypeStruct((N_PROMPT, ROPE_DIM), F32), jax.ShapeDtypeStruct((N_DEC, ROPE_DIM), F32),
            jax.ShapeDtypeStruct((T, N_HEADS * HEAD_W), BF16), jax.ShapeDtypeStruct((N_DEC, N_HEADS * V_DIM), BF16),
            jax.ShapeDtypeStruct((N_TILES, N_HEADS // 2 * VT_ROWS, TM), BF16),
            jax.ShapeDtypeStruct((N_TILES, N_HEADS * HEAD_W, TM), BF16),
            jax.ShapeDtypeStruct((N_DEC, N_HEADS * HEAD_W), BF16),
        ],
        compiler_params=_cparams(),
        name="mla_proj",
    )(pos, sorted_rows, cos_t, sin_t, cos_tt, sin_tt, kvn, wdkv, kvan, krg, wuk, wuv, wuv.T, kg, nmq, wdq, qan,
      wuq.T, jnp.broadcast_to(qg.reshape(HEAD_W, 1), (HEAD_W, TM)), wukt)


def _cache_kv_body(ckv_ref, kr_ref, place_ref, wuk_ref, wuv_ref, kg_ref, wukt_ref, k_ref, v_ref):
    kr = _dot(kr_ref[...].astype(BF16), place_ref[...])
    cb = ckv_ref[...].astype(BF16)
    _expand_k(cb, ckv_ref[...].T.astype(BF16), kr, wuk_ref, wukt_ref, kg_ref, k_ref)
    v_ref[...] = _dot(cb, wuv_ref[...]).astype(BF16)


CACHE_ROWS = 1024
assert PAST_LEN % CACHE_ROWS == 0


def _cache_kv(ckv, kr, wuk, wukt, wuv, kg):
    n = ckv.shape[0]
    place = jnp.asarray(np.arange(ROPE_DIM)[:, None] + ROPE_LO == np.arange(LANES)[None, :], BF16)
    const = lambda *shape: pl.BlockSpec(shape, lambda i: (0,) * len(shape))
    row = lambda w: pl.BlockSpec((CACHE_ROWS, w), lambda i: (i, 0))
    return pl.pallas_call(
        _cache_kv_body,
        grid=(n // CACHE_ROWS,),
        in_specs=[row(KV_RANK), row(ROPE_DIM), const(ROPE_DIM, LANES), const(KV_RANK, N_HEADS * HEAD_W),
                  const(KV_RANK, N_HEADS * V_DIM), const(1, LANES), const(N_HEADS * NOPE_DIM, KV_RANK)],
        out_specs=[row(N_HEADS * HEAD_W), row(N_HEADS * V_DIM)],
        out_shape=[jax.ShapeDtypeStruct((n, N_HEADS * HEAD_W), BF16), jax.ShapeDtypeStruct((n, N_HEADS * V_DIM), BF16)],
        compiler_params=_cparams(),
        name="cache_kv",
    )(ckv, kr, place, wuk, wuv, kg, wukt)


TQ = 256
TK = 256
assert TQ == TK and TQ % CHUNK == 0
SCORE_LOOKAHEAD = 8


def _qk(q, k):
    return lax.dot_general(q, k, (((1,), (1,)), ((), ())), preferred_element_type=F32)


def _merge_heads(o_ref, outs, rows):
    lane = lax.broadcasted_iota(jnp.int32, (rows, LANES), 1)
    for pr in range(N_HEADS // 2):
        o_ref[:, pr * LANES:(pr + 1) * LANES] = jnp.where(lane < V_DIM, outs[2 * pr], outs[2 * pr + 1]).astype(BF16)


def _prompt_attn_body(qt_ref, k_ref, vt_ref, rows_ref, wo_ref, out_ref, m_scr, acc_scr):
    qi = pl.program_id(1)
    m_scr[...] = jnp.full(m_scr.shape, NEG, F32)
    acc_scr[...] = jnp.zeros(acc_scr.shape, F32)

    def all_heads(j, mask):
        ks = pl.ds(pl.multiple_of(j * TK, TK), TK)

        def scores(hh):
            hcols = slice(hh * HEAD_W, (hh + 1) * HEAD_W)
            return _dot(k_ref[ks, hcols], qt_ref[0, hcols, :])

        ahead = [scores(hh) for hh in range(SCORE_LOOKAHEAD)]
        for hh in range(N_HEADS):
            vrows = slice((hh // 2) * VT_ROWS, (hh // 2 + 1) * VT_ROWS)
            s = ahead.pop(0)
            if hh + SCORE_LOOKAHEAD < N_HEADS:
                ahead.append(scores(hh + SCORE_LOOKAHEAD))
            if mask is not None:
                s = jnp.where(mask, s, NEG)
            m_old = m_scr[hh]
            m_new = jnp.maximum(m_old, jnp.max(s, axis=0, keepdims=True))
            p = jnp.exp2(s - m_new).astype(BF16)
            m_scr[hh] = m_new
            acc_scr[hh] = jnp.exp2(m_old - m_new) * acc_scr[hh] + _dot(vt_ref[j, vrows, :], p)

    def step(j, carry):
        all_heads(j, None)
        return carry

    lax.fori_loop(0, qi, step, 0)
    kc = lax.broadcasted_iota(jnp.int32, (TK, TQ), 0) // CHUNK
    qc = lax.broadcasted_iota(jnp.int32, (TK, TQ), 1) // CHUNK
    all_heads(qi, kc <= qc)
    row = lax.broadcasted_iota(jnp.int32, (LANES, TQ), 0)
    pairs = []
    for pr in range(N_HEADS // 2):
        even = acc_scr[2 * pr, :LANES] / acc_scr[2 * pr, LANES:LANES + 1]
        odd = acc_scr[2 * pr + 1, :LANES] / acc_scr[2 * pr + 1, LANES:LANES + 1]
        pairs.append(jnp.where(row < V_DIM, even, odd))
    o = jnp.concatenate(pairs, axis=0).T.astype(BF16)
    out_ref[...] = rows_ref[...] + _dot(o, wo_ref[...])


def _prompt_attn(qt, k, vt, rows, wo):
    nq = SEQ // TQ
    return pl.pallas_call(
        _prompt_attn_body,
        grid=(BATCH, nq),
        in_specs=[
            pl.BlockSpec((1, N_HEADS * HEAD_W, TQ), lambda b, i: (b * nq + i, 0, 0)),
            pl.BlockSpec((SEQ, N_HEADS * HEAD_W), lambda b, i: (b, 0)),
            pl.BlockSpec((SEQ // TK, N_HEADS // 2 * VT_ROWS, TK), lambda b, i: (b, 0, 0)),
            pl.BlockSpec((TQ, D_MODEL), lambda b, i: (b * nq + i, 0)),
            pl.BlockSpec((N_HEADS * V_DIM, D_MODEL), lambda b, i: (0, 0)),
        ],
        out_specs=pl.BlockSpec((TQ, D_MODEL), lambda b, i: (b * nq + i, 0)),
        out_shape=jax.ShapeDtypeStruct((N_PROMPT, D_MODEL), F32),
        scratch_shapes=[pltpu.VMEM((N_HEADS, 1, TQ), F32), pltpu.VMEM((N_HEADS, VT_ROWS, TQ), F32)],
        compiler_params=_cparams(2),
        name="prompt_attn",
    )(qt, k, vt, rows, wo)


def _sample_attn_body(q_ref, kc_ref, vc_ref, kn_ref, vn_ref, rows_ref, wo_ref, out_ref, o_scr):
    outs = []
    for hh in range(N_HEADS):
        hcols = slice(hh * HEAD_W, (hh + 1) * HEAD_W)
        vcols = slice((hh // 2) * LANES, (hh // 2 + 1) * LANES)
        q = q_ref[:, hcols]
        sc = _qk(q, kc_ref[:, hcols])
        sn = _qk(q, kn_ref[:, hcols])
        m = jnp.maximum(jnp.max(sc, axis=-1, keepdims=True), jnp.max(sn, axis=-1, keepdims=True))
        pc = jnp.exp2(sc - m)
        pn = jnp.exp2(sn - m)
        l = jnp.sum(pc, axis=-1, keepdims=True) + jnp.sum(pn, axis=-1, keepdims=True)
        acc = _dot(pc.astype(BF16), vc_ref[:, vcols]) + _dot(pn.astype(BF16), vn_ref[:, vcols])
        outs.append(acc / l)
    _merge_heads(o_scr, outs, DEC_SEQ)
    out_ref[...] = rows_ref[...] + _dot(o_scr[...], wo_ref[...])


def _sample_attn(q, kc, vc, kn, vn, rows, wo):
    off = N_PROMPT // DEC_SEQ
    return pl.pallas_call(
        _sample_attn_body,
        grid=(DEC_BATCH,),
        in_specs=[
            pl.BlockSpec((DEC_SEQ, N_HEADS * HEAD_W), lambda b: (b, 0)),
            pl.BlockSpec((PAST_LEN, N_HEADS * HEAD_W), lambda b: (b, 0)),
            pl.BlockSpec((PAST_LEN, N_HEADS * V_DIM), lambda b: (b, 0)),
            pl.BlockSpec((DEC_SEQ, N_HEADS * HEAD_W), lambda b: (off + b, 0)),
            pl.BlockSpec((DEC_SEQ, N_HEADS * V_DIM), lambda b: (b, 0)),
            pl.BlockSpec((DEC_SEQ, D_MODEL), lambda b: (off + b, 0)),
            pl.BlockSpec((N_HEADS * V_DIM, D_MODEL), lambda b: (0, 0)),
        ],
        out_specs=pl.BlockSpec((DEC_SEQ, D_MODEL), lambda b: (b, 0)),
        out_shape=jax.ShapeDtypeStruct((N_DEC, D_MODEL), F32),
        scratch_shapes=[pltpu.VMEM((DEC_SEQ, N_HEADS * V_DIM), BF16)],
        compiler_params=_cparams(),
        name="sample_attn",
    )(q, kc, vc, kn, vn, rows, wo)


def _finish_body(pos_ref, sorted_hbm, yp_ref, ys_ref, xbuf, xsem):
    i = pl.program_id(0)
    x_ref = _gather_tile(pos_ref, sorted_hbm, xbuf, xsem, TM)

    def relayout(dst_ref):
        for s in range(ROW_TILES):
            dst_ref[:, s * LANES:(s + 1) * LANES] = x_ref[pl.ds(s, TM, stride=ROW_TILES), :]

    @pl.when(i < N_TILES - 1)
    def _():
        relayout(yp_ref)

    @pl.when(i == N_TILES - 1)
    def _():
        relayout(ys_ref)


def _finish(pos, sorted_rows):
    return pl.pallas_call(
        _finish_body,
        grid_spec=pltpu.PrefetchScalarGridSpec(
            num_scalar_prefetch=1,
            grid=(N_TILES,),
            in_specs=[pl.BlockSpec(memory_space=pl.ANY)],
            out_specs=[pl.BlockSpec((TM, D_MODEL), lambda i, p: (jnp.minimum(i, N_TILES - 2), 0)),
                       pl.BlockSpec((N_DEC, D_MODEL), lambda i, p: (0, 0))],
            scratch_shapes=_gather_scratch(TM),
        ),
        out_shape=[jax.ShapeDtypeStruct((N_PROMPT, D_MODEL), F32), jax.ShapeDtypeStruct((N_DEC, D_MODEL), F32)],
        compiler_params=_cparams(),
        name="finish",
    )(pos, sorted_rows)


def _rope_tables():
    half = ROPE_DIM // 2
    inv_freq = ROPE_THETA ** (-jnp.arange(half, dtype=F32) / half)
    dec_pos = PAST_LEN + jnp.tile(jnp.arange(DEC_SEQ, dtype=jnp.int32), DEC_BATCH)
    pos = jnp.concatenate([jnp.arange(SEQ, dtype=jnp.int32), dec_pos])
    ang = pos.astype(F32)[:, None] * inv_freq[None, :]
    cos, sin = jnp.cos(ang), jnp.sin(ang)
    n = pos.shape[0]
    cos_t = jnp.ones((n, LANES), F32).at[:, ROPE_LO:ROPE_LO + ROPE_DIM].set(jnp.concatenate([cos, cos], axis=1))
    sin_t = jnp.zeros((n, LANES), F32).at[:, ROPE_LO:ROPE_LO + ROPE_DIM].set(jnp.concatenate([-sin, sin], axis=1))
    to_tiles = lambda a: a.reshape(n // TM, TM, half).transpose(0, 2, 1)
    return cos_t, sin_t, to_tiles(cos), to_tiles(sin)


def _on_lanes(vec, lo):
    return jnp.zeros((1, LANES), F32).at[0, lo:lo + vec.shape[0]].set(vec)


def kernel(x_prompt, x_sample, cache_ckv, cache_krope, norm_mix, norm_ffn, gm_w_in, gm_b_in, gm_ln_g, gm_ln_b, gm_w_s, gm_b_s, gm_w_out, gm_b_out, kv_norm, w_dkv, kv_a_norm, k_rope_norm, w_uk, w_uv, k_nope_norm, w_dq, q_a_norm, w_uq, q_nope_norm, q_rope_norm, w_o, moe_w_group, moe_b_group, moe_w_expert, moe_b_expert, moe_w1, moe_w3, moe_w2):
    nf0, nf1 = norm_ffn[0].reshape(1, D_MODEL), norm_ffn[1].reshape(1, D_MODEL)

    idx = np.arange(GMLP_BLOCK)
    allowed = (idx[None, :] // CHUNK) <= (idx[:, None] // CHUNK)
    ws_p = jnp.where(allowed[None], gm_w_s[0], 0.0).astype(BF16)
    same_seq = (idx[None, :] // DEC_SEQ) == (idx[:, None] // DEC_SEQ)
    ws_d = jnp.where(same_seq[None], jnp.tile(gm_w_s[0][:, :DEC_SEQ, :DEC_SEQ], (1, GMLP_BLOCK // DEC_SEQ, GMLP_BLOCK // DEC_SEQ)), 0.0).astype(BF16)
    bs_p = gm_b_s[0][:, :, None]
    bs_d = jnp.tile(gm_b_s[0][:, :DEC_SEQ], (1, GMLP_BLOCK // DEC_SEQ))[:, :, None]
    rows, v_rows, info, cnt = _gmlp_layer(
        x_prompt.reshape(N_PROMPT, D_MODEL), x_sample.reshape(N_DEC, D_MODEL),
        norm_mix[0].reshape(1, -1), gm_w_in[0].astype(BF16), gm_b_in[0].reshape(1, -1),
        gm_ln_g[0].reshape(1, -1), gm_ln_b[0].reshape(1, -1), ws_p, ws_d, bs_p, bs_d,
        gm_w_out[0].astype(BF16), gm_b_out[0].reshape(1, -1),
        nf0, *_router_weights(0, moe_w_group, moe_b_group, moe_w_expert, moe_b_expert))
    sorted_rows, pos = _moe_layer(0, rows, info, cnt, nf0, moe_w1, moe_w3, moe_w2)

    cos_t, sin_t, cos_tt, sin_tt = _rope_tables()
    wdkv = jnp.zeros((D_MODEL, KV_RANK + LANES), F32).at[:, :KV_RANK].set(w_dkv[:, :KV_RANK])
    wdkv = wdkv.at[:, KV_RANK + ROPE_LO:KV_RANK + ROPE_LO + ROPE_DIM].set(w_dkv[:, KV_RANK:]).astype(BF16)
    wuk = jnp.zeros((KV_RANK, N_HEADS, HEAD_W), F32).at[:, :, :NOPE_DIM].set(w_uk).reshape(KV_RANK, -1).astype(BF16)
    wukt = w_uk.reshape(KV_RANK, -1).T.astype(BF16)
    wuv = w_uv.reshape(KV_RANK, -1).astype(BF16)
    wuq =jnp.zeros((Q_RANK, N_HEADS, HEAD_W), F32).at[:, :, :NOPE_DIM + ROPE_DIM].set(w_uq[0]).reshape(Q_RANK, -1).astype(BF16)
    kg = _on_lanes(k_nope_norm, 0)
    krg = _on_lanes(k_rope_norm, ROPE_LO)
    qg = _on_lanes(jnp.concatenate([q_nope_norm[0], q_rope_norm[0]]), 0)
    rows, ckv_p, ckv_d, krope_p, krope_d, k_new, v_dec, vt_new, qt, q_dec = _mla_proj(
        pos, sorted_rows, cos_t, sin_t, cos_tt, sin_tt, kv_norm.reshape(1, -1), wdkv, kv_a_norm.reshape(1, -1), krg, wuk, wukt, wuv, kg,
        norm_mix[1].reshape(1, -1), w_dq[0].astype(BF16), q_a_norm[0].reshape(1, -1), wuq, qg)
    k_cache, v_cache = _cache_kv(cache_ckv.reshape(-1, KV_RANK), cache_krope.reshape(-1, ROPE_DIM), wuk, wukt, wuv, kg)

    wo = w_o[0].astype(BF16)
    h_prompt = _prompt_attn(qt, k_new, vt_new, rows, wo)
    h_dec = _sample_attn(q_dec, k_cache, v_cache, k_new, v_dec, rows, wo)
    rows, info, cnt = _router(h_prompt, h_dec, nf1,
                              *_router_weights(1, moe_w_group, moe_b_group, moe_w_expert, moe_b_expert))
    sorted_rows, pos = _moe_layer(1, rows, info, cnt, nf1, moe_w1, moe_w3, moe_w2)
    y_prompt, y_sample = _finish(pos, sorted_rows)

    return (y_prompt.reshape(BATCH, SEQ, D_MODEL), y_sample.reshape(DEC_BATCH, DEC_SEQ, D_MODEL),
            ckv_p.reshape(BATCH, SEQ, KV_RANK), krope_p.reshape(BATCH, SEQ, ROPE_DIM),
            ckv_d.reshape(DEC_BATCH, DEC_SEQ, KV_RANK), krope_d.reshape(DEC_BATCH, DEC_SEQ, ROPE_DIM),
            v_rows.reshape(1, DEC_BATCH, DEC_SEQ, D_GATE))
```

```python
import functools

import jax
import jax.numpy as jnp
import numpy as np
from jax import lax
from jax.experimental import pallas as pl
from jax.experimental.pallas import tpu as pltpu

F32 = jnp.float32
BF16 = jnp.bfloat16

D_MODEL = 1024
BATCH = 8
SEQ = 2048
DEC_BATCH = 16
DEC_SEQ = 16
PAST_LEN = 2048
CHUNK = 64
GMLP_BLOCK = 128
D_GATE = 2 * D_MODEL
N_SG = 8
SG_W = D_GATE // N_SG
N_HEADS = 8
NOPE_DIM = 64
ROPE_DIM = 32
V_DIM = 64
Q_RANK = 384
KV_RANK = 256
ROPE_THETA = 10000.0
SCALE = (NOPE_DIM + ROPE_DIM) ** -0.5
Q_SCALE = SCALE * float(np.log2(np.e))
N_EGROUPS = 4
EXPERTS_PER_GROUP = 4
N_EXPERTS = N_EGROUPS * EXPERTS_PER_GROUP
D_EXPERT = 512
EPS = 1e-6
NEG = -1e30

LANES = 128
SUBLANES = 8
ROW_TILES = D_MODEL // LANES
assert ROW_TILES == SUBLANES

N_PROMPT = BATCH * SEQ
N_DEC = DEC_BATCH * DEC_SEQ
T = N_PROMPT + N_DEC
TM = 256
assert N_PROMPT % TM == 0 and N_DEC == TM
N_TILES = T // TM
HEAD_W = LANES
ROPE_LO = NOPE_DIM
ROPE_HALF = ROPE_DIM // 2

PAIR_A = (0, 0, 0, 1, 1, 3)
PAIR_B = (1, 2, 3, 3, 2, 2)
N_PAIRS = 6
N_BUCKETS = N_EGROUPS * N_PAIRS
BUCKET_ROWS = 32
assert N_BUCKETS <= BUCKET_ROWS and BUCKET_ROWS % SUBLANES == 0
MOE_TILES = (T + N_BUCKETS * (TM - 1) + TM - 1) // TM
P_ROWS = MOE_TILES * TM

VMEM_LIMIT = 56 * 1024 * 1024


def _cparams(n_axes=1, vmem=VMEM_LIMIT):
    return pltpu.CompilerParams(dimension_semantics=("arbitrary",) * n_axes, vmem_limit_bytes=vmem)


def _rms(x, g):
    return x * lax.rsqrt(jnp.mean(x * x, axis=-1, keepdims=True) + EPS) * g


def _load_rows(ref, n):
    return jnp.concatenate([ref[pl.ds(s, n, stride=ROW_TILES), :] for s in range(ROW_TILES)], axis=1)


def _store_rows(ref, x, n):
    for s in range(ROW_TILES):
        ref[pl.ds(s, n, stride=ROW_TILES), :] = x[:, s * LANES:(s + 1) * LANES]


def _dot(a, b):
    return jnp.dot(a, b, preferred_element_type=F32)


GELU_K1 = float(-2.0 * np.sqrt(2.0 / np.pi) * np.log2(np.e))
GELU_K3 = GELU_K1 * 0.044715


def _gmlp_body(xp_ref, xd_ref, nm_ref, win_ref, bin_ref, lng_ref, lnb_ref, wsp_ref, wsd_ref, bsp_ref, bsd_ref,
               wout_ref, bout_ref, nf_ref, wrh_ref, wr2_ref, br_ref,
               rows_ref, v_ref, info_ref, cnt_ref, gated_ref, carry_ref):
    i = pl.program_id(0)
    is_dec = i == N_TILES - 1
    _route_init(carry_ref)
    x = jnp.where(is_dec, xd_ref[...], xp_ref[...])
    xn = _rms(x, nm_ref[...]).astype(BF16)
    z = _dot(xn, win_ref[...]) + bin_ref[...]
    z = z / (1.0 + jnp.exp2(z * (GELU_K1 + GELU_K3 * (z * z))))
    u = z[:, :D_GATE]
    v = z[:, D_GATE:]
    mu = jnp.mean(v, axis=-1, keepdims=True)
    vc = v - mu
    var = jnp.mean(vc * vc, axis=-1, keepdims=True)
    v = vc * lax.rsqrt(var + EPS) * lng_ref[...] + lnb_ref[...]

    v_ref[...] = v
    vb = v.astype(BF16)
    for g in range(N_SG):
        ws = jnp.where(is_dec, wsd_ref[g], wsp_ref[g])
        bs = jnp.where(is_dec, bsd_ref[g], bsp_ref[g])
        for b in range(TM // GMLP_BLOCK):
            rows = slice(b * GMLP_BLOCK, (b + 1) * GMLP_BLOCK)
            cols = slice(g * SG_W, (g + 1) * SG_W)
            s = _dot(ws, vb[rows, cols]) + bs
            gated_ref[rows, cols] = (u[rows, cols] * s).astype(BF16)
    h = x + _dot(gated_ref[...], wout_ref[...]) + bout_ref[...]
    _store_rows(rows_ref, h, TM)
    _route_tile(h, nf_ref, wrh_ref, wr2_ref, br_ref, info_ref, cnt_ref, carry_ref)


def _gmlp_layer(x_prompt, x_dec, nm, w_in, b_in, ln_g, ln_b, ws_p, ws_d, bs_p, bs_d, w_out, b_out, nf, wr, br):
    const = lambda *shape: pl.BlockSpec(shape, lambda i: (0,) * len(shape))
    return pl.pallas_call(
        _gmlp_body,
        grid=(N_TILES,),
        in_specs=[
            pl.BlockSpec((TM, D_MODEL), lambda i: (jnp.minimum(i, N_TILES - 2), 0)), const(N_DEC, D_MODEL),
            const(1, D_MODEL), const(D_MODEL, 2 * D_GATE), const(1, 2 * D_GATE),
            const(1, D_GATE), const(1, D_GATE),
            const(N_SG, GMLP_BLOCK, GMLP_BLOCK), const(N_SG, GMLP_BLOCK, GMLP_BLOCK),
            const(N_SG, GMLP_BLOCK, 1), const(N_SG, GMLP_BLOCK, 1),
            const(D_GATE, D_MODEL), const(1, D_MODEL),
            const(1, D_MODEL), const(D_MODEL, LANES), const(D_MODEL, 2 * LANES), const(1, LANES),
        ],
        out_specs=[
            pl.BlockSpec((TM * ROW_TILES, LANES), lambda i: (i, 0)),
            const(N_DEC, D_GATE),
            pl.BlockSpec((SUBLANES, TM), lambda i: (0, i)),
            const(BUCKET_ROWS, LANES),
        ],
        out_shape=[jax.ShapeDtypeStruct((T * ROW_TILES, LANES), F32), jax.ShapeDtypeStruct((N_DEC, D_GATE), F32),
                   jax.ShapeDtypeStruct((SUBLANES, T), F32), jax.ShapeDtypeStruct((BUCKET_ROWS, LANES), F32)],
        scratch_shapes=[pltpu.VMEM((TM, D_GATE), BF16), pltpu.VMEM((BUCKET_ROWS, LANES), F32)],
        compiler_params=_cparams(),
        name="gmlp_layer",
    )(x_prompt, x_dec, nm, w_in, b_in, ln_g, ln_b, ws_p, ws_d, bs_p, bs_d, w_out, b_out,
      nf, *_router_split(wr), br)


def _router_body(hp_ref, hd_ref, nf_ref, wrh_ref, wr2_ref, br_ref, rows_ref, info_ref, cnt_ref, carry_ref):
    _route_init(carry_ref)
    h = jnp.where(pl.program_id(0) == N_TILES - 1, hd_ref[...], hp_ref[...])
    _store_rows(rows_ref, h, TM)
    _route_tile(h, nf_ref, wrh_ref, wr2_ref, br_ref, info_ref, cnt_ref, carry_ref)


def _router_split(wr):
    hi = wr.astype(BF16)
    return hi, jnp.concatenate([hi, (wr - hi.astype(F32)).astype(BF16)], axis=1)


def _route_init(carry_ref):
    @pl.when(pl.program_id(0) == 0)
    def _():
        carry_ref[...] = jnp.zeros_like(carry_ref)


def _route_tile(h, nf_ref, wrh_ref, wr2_ref, br_ref, info_ref, cnt_ref, carry_ref):
    xn = _rms(h, nf_ref[...])
    xh = xn.astype(BF16)
    xl = (xn - xh.astype(F32)).astype(BF16)
    both = _dot(xh, wr2_ref[...])
    logits = both[:, :LANES] + (_dot(xl, wrh_ref[...]) + both[:, LANES:]) + br_ref[...]
    lt = logits.T

    def first_max(vals, ids):
        vmax = jnp.max(vals, axis=0, keepdims=True)
        return vmax, jnp.min(jnp.where(vals == vmax, ids, float(LANES)), axis=0, keepdims=True)

    lg = lt[N_EXPERTS:N_EXPERTS + N_EGROUPS]
    gmax, g_idx = first_max(lg, lax.broadcasted_iota(jnp.int32, lg.shape, 0).astype(F32))
    g_p = 1.0 / jnp.sum(jnp.exp(lg - gmax), axis=0, keepdims=True)
    e_id = lax.broadcasted_iota(jnp.int32, (N_EXPERTS, TM), 0)
    in_group = (e_id // EXPERTS_PER_GROUP).astype(F32) == g_idx
    e_id = e_id.astype(F32)
    le = jnp.where(in_group, lt[:N_EXPERTS], -jnp.inf)
    v1, i1 = first_max(le, e_id)
    v2, i2 = first_max(jnp.where(e_id == i1, -jnp.inf, le), e_id)
    e2 = jnp.exp(v2 - v1)
    w1 = (1.0 / (1.0 + e2)) * g_p
    w2 = (e2 / (1.0 + e2)) * g_p
    a1 = i1 - EXPERTS_PER_GROUP * g_idx
    a2 = i2 - EXPERTS_PER_GROUP * g_idx
    lo = jnp.minimum(a1, a2)
    hi = jnp.maximum(a1, a2)
    pair = jnp.where(lo == 0.0, hi - 1.0, jnp.where(lo == 1.0, jnp.where(hi == 3.0, 3.0, 4.0), 5.0))
    ea = jnp.where(pair < 3.0, 0.0, jnp.where(pair < 5.0, 1.0, 3.0))
    ga = jnp.where(a1 == ea, w1, w2)
    gb = jnp.where(a1 == ea, w2, w1)
    bucket = g_idx * N_PAIRS + pair

    onehot = (lax.broadcasted_iota(jnp.int32, (BUCKET_ROWS, TM), 0).astype(F32) == bucket).astype(F32)
    r = lax.broadcasted_iota(jnp.int32, (TM, TM), 0)
    c = lax.broadcasted_iota(jnp.int32, (TM, TM), 1)
    before = _dot(onehot.astype(BF16), (r < c).astype(BF16))
    carry = carry_ref[:, 0:1]
    rank = jnp.sum(onehot * (before + carry), axis=0, keepdims=True)
    new_carry = carry + jnp.sum(onehot, axis=1, keepdims=True)
    carry_ref[...] = jnp.broadcast_to(new_carry, carry_ref.shape)
    cnt_ref[...] = jnp.broadcast_to(new_carry, cnt_ref.shape)
    info_ref[...] = jnp.concatenate([bucket, rank, ga, gb, jnp.zeros((SUBLANES - 4, TM), F32)], axis=0)


def _router(h_prompt, h_dec, nf, wr, br):
    const = lambda *shape: pl.BlockSpec(shape, lambda i: (0,) * len(shape))
    return pl.pallas_call(
        _router_body,
        grid=(N_TILES,),
        in_specs=[pl.BlockSpec((TM, D_MODEL), lambda i: (jnp.minimum(i, N_TILES - 2), 0)), const(N_DEC, D_MODEL),
                  const(1, D_MODEL), const(D_MODEL, LANES), const(D_MODEL, 2 * LANES), const(1, LANES)],
        out_specs=[pl.BlockSpec((TM * ROW_TILES, LANES), lambda i: (i, 0)),
                   pl.BlockSpec((SUBLANES, TM), lambda i: (0, i)),
                   const(BUCKET_ROWS, LANES)],
        out_shape=[jax.ShapeDtypeStruct((T * ROW_TILES, LANES), F32), jax.ShapeDtypeStruct((SUBLANES, T), F32),
                   jax.ShapeDtypeStruct((BUCKET_ROWS, LANES), F32)],
        scratch_shapes=[pltpu.VMEM((BUCKET_ROWS, LANES), F32)],
        compiler_params=_cparams(),
        name="moe_router",
    )(h_prompt, h_dec, nf, *_router_split(wr), br)


GATHER_UNROLL = 64
GATHER_DEPTH = 3


def _gather_rows_start(idx_ref, base, src_ref, dst_ref, sem, n):
    def group(g, carry):
        for u in range(GATHER_UNROLL):
            r = g * GATHER_UNROLL + u
            src = pl.multiple_of(idx_ref[base + r] * ROW_TILES, ROW_TILES)
            dst = pl.multiple_of(r * ROW_TILES, ROW_TILES)
            pltpu.make_async_copy(src_ref.at[pl.ds(src, ROW_TILES), :], dst_ref.at[pl.ds(dst, ROW_TILES), :],
                                  sem).start(priority=u % 2)
        return carry

    lax.fori_loop(0, n // GATHER_UNROLL, group, 0)


def _gather_rows_wait(src_ref, dst_ref, sem, n):
    pltpu.make_async_copy(src_ref.at[pl.ds(0, n * ROW_TILES), :], dst_ref.at[pl.ds(0, n * ROW_TILES), :], sem).wait()


def _gather_tile(idx_ref, src_ref, buf, sem, n, n_live=None, base_of=None, side_copy=None):
    i = pl.program_id(0)
    slot = lax.rem(i, GATHER_DEPTH)
    n_live = pl.num_programs(0) if n_live is None else n_live
    base_of = (lambda step: step * n) if base_of is None else base_of
    ahead = GATHER_DEPTH - 1

    def start(step, s):
        _gather_rows_start(idx_ref, base_of(step), src_ref, buf.at[s], sem.at[s], n)
        if side_copy is not None:
            side_copy(step, s).start()

    for d in range(ahead):
        @pl.when((i == 0) & (d < n_live))
        def _(d=d):
            start(d, d)

    @pl.when(i + ahead < n_live)
    def _():
        start(i + ahead, lax.rem(i + ahead, GATHER_DEPTH))

    @pl.when(i < n_live)
    def _():
        _gather_rows_wait(src_ref, buf.at[slot], sem.at[slot], n)
        if side_copy is not None:
            side_copy(i, slot).wait()

    return buf.at[slot]


def _gather_scratch(n):
    return [pltpu.VMEM((GATHER_DEPTH, n * ROW_TILES, LANES), F32), pltpu.SemaphoreType.DMA((GATHER_DEPTH,))]


GATE_WIN = TM + SUBLANES


def _moe_body(tok_ref, ea_ref, eb_ref, cha_ref, chb_ref, nlive_ref, first_ref, nrows_ref, rows_hbm, gates_hbm, nf_ref,
              w1a_ref, w3a_ref, w2a_ref, w1b_ref, w3b_ref, w2b_ref, out_ref,
              s1a, s3a, s2a, s1b, s3b, s2b, xbuf, xsem, gwin):
    i = pl.program_id(0)
    live = i < nlive_ref[0]

    def gate_copy(step, s):
        lo = pl.multiple_of((first_ref[step] // SUBLANES) * SUBLANES, SUBLANES)
        return pltpu.make_async_copy(gates_hbm.at[pl.ds(lo, GATE_WIN), :], gwin.at[s], xsem.at[s])

    x_ref = _gather_tile(tok_ref, rows_hbm, xbuf, xsem, TM, nlive_ref[0],
                         base_of=lambda step: first_ref[step], side_copy=gate_copy)

    @pl.when(cha_ref[i] == 1)
    def _():
        s1a[...] = w1a_ref[...].astype(BF16)
        s3a[...] = w3a_ref[...].astype(BF16)
        s2a[...] = w2a_ref[...].astype(BF16)

    @pl.when(chb_ref[i] == 1)
    def _():
        s1b[...] = w1b_ref[...].astype(BF16)
        s3b[...] = w3b_ref[...].astype(BF16)
        s2b[...] = w2b_ref[...].astype(BF16)

    @pl.when(live)
    def _():
        h = _load_rows(x_ref, TM)
        xn = _rms(h, nf_ref[...]).astype(BF16)
        g = gwin[lax.rem(i, GATHER_DEPTH), pl.ds(lax.rem(first_ref[i], SUBLANES), TM), :]
        row = lax.broadcasted_iota(jnp.int32, (TM, 1), 0)
        g = jnp.where(row < nrows_ref[i], g, 0.0)

        def ffn(w1, w3, w2):
            a = _dot(xn, w1[...])
            hdn = (a * (1.0 / (1.0 + jnp.exp(-a)))) * _dot(xn, w3[...])
            return _dot(hdn.astype(BF16), w2[...])

        y = g[:, 0:1] * ffn(s1a, s3a, s2a) + g[:, 1:2] * ffn(s1b, s3b, s2b)
        _store_rows(out_ref, h + y, TM)

    @pl.when(jnp.logical_not(live))
    def _():
        out_ref[...] = jnp.zeros_like(out_ref)


def _moe_ffn(layer, tok, ea, eb, cha, chb, n_live, first, nrows, rows, gates, nf, w1, w3, w2):
    wa = lambda shape: pl.BlockSpec((None, None) + shape, lambda i, tk, ea, eb, *_: (layer, ea[i], 0, 0))
    wb = lambda shape: pl.BlockSpec((None, None) + shape, lambda i, tk, ea, eb, *_: (layer, eb[i], 0, 0))
    up, down = (D_MODEL, D_EXPERT), (D_EXPERT, D_MODEL)
    return pl.pallas_call(
        _moe_body,
        grid_spec=pltpu.PrefetchScalarGridSpec(
            num_scalar_prefetch=8,
            grid=(MOE_TILES,),
            in_specs=[
                pl.BlockSpec(memory_space=pl.ANY),
                pl.BlockSpec(memory_space=pl.ANY),
                pl.BlockSpec((1, D_MODEL), lambda i, *_: (0, 0)),
                wa(up), wa(up), wa(down), wb(up), wb(up), wb(down),
            ],
            out_specs=pl.BlockSpec((TM * ROW_TILES, LANES), lambda i, *_: (i, 0)),
            scratch_shapes=[pltpu.VMEM(up, BF16), pltpu.VMEM(up, BF16), pltpu.VMEM(down, BF16),
                            pltpu.VMEM(up, BF16), pltpu.VMEM(up, BF16), pltpu.VMEM(down, BF16)]
            + _gather_scratch(TM) + [pltpu.VMEM((GATHER_DEPTH, GATE_WIN, 2), F32)],
        ),
        out_shape=jax.ShapeDtypeStruct((P_ROWS * ROW_TILES, LANES), F32),
        compiler_params=_cparams(),
        name="moe_ffn",
    )(tok, ea, eb, cha, chb, n_live, first, nrows, rows, gates, nf, w1, w3, w2, w1, w3, w2)


def _router_weights(layer, w_group, b_group, w_expert, b_expert):
    pad = LANES - N_EXPERTS - N_EGROUPS
    wr = jnp.concatenate([w_expert[layer], w_group[layer], jnp.zeros((D_MODEL, pad), F32)], axis=1)
    br = jnp.concatenate([b_expert[layer], b_group[layer], jnp.zeros((pad,), F32)]).reshape(1, LANES)
    return wr, br


def _moe_layer(layer, rows, info, cnt, nf, w1, w3, w2):
    bucket = info[0].astype(jnp.int32)
    rank = info[1].astype(jnp.int32)
    counts = cnt[:N_BUCKETS, 0].astype(jnp.int32)
    n_tiles = (counts + TM - 1) // TM
    tile_end = jnp.cumsum(n_tiles)
    tile_start = tile_end - n_tiles
    start_of = jnp.sum(jnp.where(bucket[:, None] == jnp.arange(N_BUCKETS)[None, :], tile_start[None, :], 0), axis=1)
    pos = start_of * TM + rank
    _, tok, ga, gb = lax.sort((pos, jnp.arange(T, dtype=jnp.int32), info[2], info[3]), num_keys=1)
    tok = jnp.concatenate([tok, jnp.zeros((TM,), jnp.int32)])
    gates = jnp.concatenate([jnp.stack([ga, gb], axis=1), jnp.zeros((TM + SUBLANES, 2), F32)])
    total = tile_end[-1]
    j = jnp.minimum(jnp.arange(MOE_TILES), total - 1)
    tb = jnp.sum((j[:, None] >= tile_end[None, :]).astype(jnp.int32), axis=1)
    tokens_before = jnp.cumsum(counts) - counts
    in_bucket = (j - tile_start[tb]) * TM
    first = (tokens_before[tb] + in_bucket).astype(jnp.int32)
    nrows = jnp.clip(counts[tb] - in_bucket, 0, TM).astype(jnp.int32)
    grp, pair = tb // N_PAIRS, tb % N_PAIRS
    ea = (grp * EXPERTS_PER_GROUP + jnp.asarray(PAIR_A, jnp.int32)[pair]).astype(jnp.int32)
    eb = (grp * EXPERTS_PER_GROUP + jnp.asarray(PAIR_B, jnp.int32)[pair]).astype(jnp.int32)
    tile0 = jnp.arange(MOE_TILES) == 0
    cha = (tile0 | (ea != jnp.roll(ea, 1))).astype(jnp.int32)
    chb = (tile0 | (eb != jnp.roll(eb, 1))).astype(jnp.int32)
    n_live = total.reshape(1).astype(jnp.int32)
    return _moe_ffn(layer, tok, ea, eb, cha, chb, n_live, first, nrows, rows, gates, nf, w1, w3, w2), pos


VT_ROWS = LANES + 16


def _rope_swap(x):
    lane = lax.broadcasted_iota(jnp.int32, x.shape, 1)
    return jnp.where(lane < ROPE_LO + ROPE_HALF, pltpu.roll(x, LANES - ROPE_HALF, 1), pltpu.roll(x, ROPE_HALF, 1))


def _expand_k(cb, cbt, kr, wuk_ref, wukt_ref, kg_ref, k_ref):
    n = cb.shape[0]
    kn = _dot(cb, wuk_ref[...])
    knt = _dot(wukt_ref[...], cbt)
    ms = [jnp.mean(jnp.square(knt[hh * NOPE_DIM:(hh + 1) * NOPE_DIM]), axis=0, keepdims=True)
          for hh in range(N_HEADS)]
    scale = lax.rsqrt(jnp.concatenate(ms + [jnp.ones((LANES - N_HEADS, n), F32)], axis=0) + EPS).T
    for hh in range(N_HEADS):
        cols = slice(hh * HEAD_W, (hh + 1) * HEAD_W)
        k_ref[:, cols] = (kn[:, cols] * scale[:, hh:hh + 1] * kg_ref[...] + kr).astype(BF16)


def _mla_proj_body(pos_ref, sorted_hbm, cos_ref, sin_ref, cost_ref, sint_ref, kvn_ref, wdkv_ref, kvan_ref, krg_ref,
                   wuk_ref, wuv_ref, wuvt_ref, kg_ref, nmq_ref, wdq_ref, qan_ref, wuqt_ref, qg_ref, wukt_ref,
                   rows_ref, ckvp_ref, ckvd_ref, krp_ref, krd_ref, k_ref, vdec_ref, vt_ref, qt_ref, qdec_ref,
                   xbuf, xsem):
    is_dec = pl.program_id(0) == N_TILES - 1
    x_ref = _gather_tile(pos_ref, sorted_hbm, xbuf, xsem, TM)
    h = _load_rows(x_ref, TM)
    rows_ref[...] = h
    hn = h * lax.rsqrt(jnp.mean(h * h, axis=-1, keepdims=True) + EPS)
    c = _dot((hn * kvn_ref[...]).astype(BF16), wdkv_ref[...])
    ckv = _rms(c[:, :KV_RANK], kvan_ref[...])
    kr = c[:, KV_RANK:]
    kr = kr * lax.rsqrt(jnp.sum(kr * kr, axis=-1, keepdims=True) * (1.0 / ROPE_DIM) + EPS) * krg_ref[...]
    kr = kr * cos_ref[...] + _rope_swap(kr) * sin_ref[...]
    cb = ckv.astype(BF16)
    cbt = ckv.T.astype(BF16)
    _expand_k(cb, cbt, kr, wuk_ref, wukt_ref, kg_ref, k_ref)
    vt = _dot(wuvt_ref[...], cbt).astype(BF16)
    for pr in range(N_HEADS // 2):
        vt_ref[0, pr * VT_ROWS:pr * VT_ROWS + LANES, :] = vt[pr * LANES:(pr + 1) * LANES]
        vt_ref[0, pr * VT_ROWS + LANES:(pr + 1) * VT_ROWS, :] = jnp.ones((VT_ROWS - LANES, TM), BF16)
    cq = _rms(_dot((hn * nmq_ref[...]).astype(BF16), wdq_ref[...]), qan_ref[...])
    qt = _dot(wuqt_ref[...], cq.T.astype(BF16))
    cost, sint, qg = cost_ref[0], sint_ref[0], qg_ref[...]
    for hh in range(N_HEADS):
        x = qt[hh * HEAD_W:(hh + 1) * HEAD_W, :]
        xn, xr = x[:NOPE_DIM], x[ROPE_LO:ROPE_LO + ROPE_DIM]
        xn = xn * lax.rsqrt(jnp.mean(xn * xn, axis=0, keepdims=True) + EPS) * qg[:NOPE_DIM]
        xr = xr * lax.rsqrt(jnp.mean(xr * xr, axis=0, keepdims=True) + EPS) * qg[ROPE_LO:ROPE_LO + ROPE_DIM]
        x1, x2 = xr[:ROPE_HALF], xr[ROPE_HALF:]
        qh = jnp.concatenate([xn, x1 * cost - x2 * sint, x1 * sint + x2 * cost,
                              jnp.zeros((HEAD_W - NOPE_DIM - ROPE_DIM, TM), F32)], axis=0) * Q_SCALE
        qt_ref[0, hh * HEAD_W:(hh + 1) * HEAD_W, :] = qh.astype(BF16)

    @pl.when(jnp.logical_not(is_dec))
    def _():
        ckvp_ref[...] = ckv
        krp_ref[...] = kr[:, ROPE_LO:ROPE_LO + ROPE_DIM]

    @pl.when(is_dec)
    def _():
        ckvd_ref[...] = ckv
        krd_ref[...] = kr[:, ROPE_LO:ROPE_LO + ROPE_DIM]
        vdec_ref[...] = _dot(cb, wuv_ref[...]).astype(BF16)
        for hh in range(N_HEADS):
            rows = slice(hh * HEAD_W, (hh + 1) * HEAD_W)
            qdec_ref[:, rows] = qt_ref[0, rows, :].astype(F32).T.astype(BF16)


def _mla_proj(pos, sorted_rows, cos_t, sin_t, cos_tt, sin_tt, kvn, wdkv, kvan, krg, wuk, wukt, wuv, kg, nmq, wdq, qan, wuq, qg):
    const = lambda *shape: pl.BlockSpec(shape, lambda i, p: (0,) * len(shape))
    tab_tile = lambda i: jnp.where(i < N_PROMPT // TM, i % (SEQ // TM), SEQ // TM)
    tab = pl.BlockSpec((TM, LANES), lambda i, p: (tab_tile(i), 0))
    tab_t = pl.BlockSpec((1, ROPE_HALF, TM), lambda i, p: (tab_tile(i), 0, 0))
    row = lambda w: pl.BlockSpec((TM, w), lambda i, p: (i, 0))
    prow = lambda w: pl.BlockSpec((TM, w), lambda i, p: (jnp.minimum(i, N_TILES - 2), 0))
    return pl.pallas_call(
        _mla_proj_body,
        grid_spec=pltpu.PrefetchScalarGridSpec(
            num_scalar_prefetch=1,
            grid=(N_TILES,),
            in_specs=[
                pl.BlockSpec(memory_space=pl.ANY), tab, tab, tab_t, tab_t,
                const(1, D_MODEL), const(D_MODEL, KV_RANK + LANES), const(1, KV_RANK), const(1, LANES),
                const(KV_RANK, N_HEADS * HEAD_W), const(KV_RANK, N_HEADS * V_DIM), const(N_HEADS * V_DIM, KV_RANK),
                const(1, LANES),
                const(1, D_MODEL), const(D_MODEL, Q_RANK), const(1, Q_RANK), const(N_HEADS * HEAD_W, Q_RANK),
                const(HEAD_W, TM), const(N_HEADS * NOPE_DIM, KV_RANK),
            ],
            out_specs=[row(D_MODEL),
                       prow(KV_RANK), const(N_DEC, KV_RANK), prow(ROPE_DIM), const(N_DEC, ROPE_DIM),
                       row(N_HEADS * HEAD_W), const(N_DEC, N_HEADS * V_DIM),
                       pl.BlockSpec((1, N_HEADS // 2 * VT_ROWS, TM), lambda i, p: (i, 0, 0)),
                       pl.BlockSpec((1, N_HEADS * HEAD_W, TM), lambda i, p: (i, 0, 0)),
                       const(N_DEC, N_HEADS * HEAD_W)],
            scratch_shapes=_gather_scratch(TM),
        ),
        out_shape=[
            jax.ShapeDtypeStruct((T, D_MODEL), F32),
            jax.ShapeDtypeStruct((N_PROMPT, KV_RANK), F32), jax.ShapeDtypeStruct((N_DEC, KV_RANK), F32),
            jax.ShapeDtypeStruct((N_PROMPT, ROPE_DIM), F32), jax.ShapeDtypeStruct((N_DEC, ROPE_DIM), F32),
            jax.ShapeDtypeStruct((T, N_HEADS * HEAD_W), BF16), jax.ShapeDtypeStruct((N_DEC, N_HEADS * V_DIM), BF16),
            jax.ShapeDtypeStruct((N_TILES, N_HEADS // 2 * VT_ROWS, TM), BF16),
            jax.ShapeDtypeStruct((N_TILES, N_HEADS * HEAD_W, TM), BF16),
            jax.ShapeDtypeStruct((N_DEC, N_HEADS * HEAD_W), BF16),
        ],
        compiler_params=_cparams(),
        name="mla_proj",
    )(pos, sorted_rows, cos_t, sin_t, cos_tt, sin_tt, kvn, wdkv, kvan, krg, wuk, wuv, wuv.T, kg, nmq, wdq, qan,
      wuq.T, jnp.broadcast_to(qg.reshape(HEAD_W, 1), (HEAD_W, TM)), wukt)


def _cache_kv_body(ckv_ref, kr_ref, place_ref, wuk_ref, wuv_ref, kg_ref, wukt_ref, k_ref, v_ref):
    kr = _dot(kr_ref[...].astype(BF16), place_ref[...])
    cb = ckv_ref[...].astype(BF16)
    _expand_k(cb, ckv_ref[...].T.astype(BF16), kr, wuk_ref, wukt_ref, kg_ref, k_ref)
    v_ref[...] = _dot(cb, wuv_ref[...]).astype(BF16)


CACHE_ROWS = 1024
assert PAST_LEN % CACHE_ROWS == 0


def _cache_kv(ckv, kr, wuk, wukt, wuv, kg):
    n = ckv.shape[0]
    place = jnp.asarray(np.arange(ROPE_DIM)[:, None] + ROPE_LO == np.arange(LANES)[None, :], BF16)
    const = lambda *shape: pl.BlockSpec(shape, lambda i: (0,) * len(shape))
    row = lambda w: pl.BlockSpec((CACHE_ROWS, w), lambda i: (i, 0))
    return pl.pallas_call(
        _cache_kv_body,
        grid=(n // CACHE_ROWS,),
        in_specs=[row(KV_RANK), row(ROPE_DIM), const(ROPE_DIM, LANES), const(KV_RANK, N_HEADS * HEAD_W),
                  const(KV_RANK, N_HEADS * V_DIM), const(1, LANES), const(N_HEADS * NOPE_DIM, KV_RANK)],
        out_specs=[row(N_HEADS * HEAD_W), row(N_HEADS * V_DIM)],
        out_shape=[jax.ShapeDtypeStruct((n, N_HEADS * HEAD_W), BF16), jax.ShapeDtypeStruct((n, N_HEADS * V_DIM), BF16)],
        compiler_params=_cparams(),
        name="cache_kv",
    )(ckv, kr, place, wuk, wuv, kg, wukt)


TQ = 256
TK = 256
assert TQ == TK and TQ % CHUNK == 0
SCORE_LOOKAHEAD = 8


def _qk(q, k):
    return lax.dot_general(q, k, (((1,), (1,)), ((), ())), preferred_element_type=F32)


def _merge_heads(o_ref, outs, rows):
    lane = lax.broadcasted_iota(jnp.int32, (rows, LANES), 1)
    for pr in range(N_HEADS // 2):
        o_ref[:, pr * LANES:(pr + 1) * LANES] = jnp.where(lane < V_DIM, outs[2 * pr], outs[2 * pr + 1]).astype(BF16)


def _prompt_attn_body(qt_ref, k_ref, vt_ref, rows_ref, wo_ref, out_ref, m_scr, acc_scr):
    qi = pl.program_id(1)
    m_scr[...] = jnp.full(m_scr.shape, NEG, F32)
    acc_scr[...] = jnp.zeros(acc_scr.shape, F32)

    def all_heads(j, mask):
        ks = pl.ds(pl.multiple_of(j * TK, TK), TK)

        def scores(hh):
            hcols = slice(hh * HEAD_W, (hh + 1) * HEAD_W)
            return _dot(k_ref[ks, hcols], qt_ref[0, hcols, :])

        ahead = [scores(hh) for hh in range(SCORE_LOOKAHEAD)]
        for hh in range(N_HEADS):
            vrows = slice((hh // 2) * VT_ROWS, (hh // 2 + 1) * VT_ROWS)
            s = ahead.pop(0)
            if hh + SCORE_LOOKAHEAD < N_HEADS:
                ahead.append(scores(hh + SCORE_LOOKAHEAD))
            if mask is not None:
                s = jnp.where(mask, s, NEG)
            m_old = m_scr[hh]
            m_new = jnp.maximum(m_old, jnp.max(s, axis=0, keepdims=True))
            p = jnp.exp2(s - m_new).astype(BF16)
            m_scr[hh] = m_new
            acc_scr[hh] = jnp.exp2(m_old - m_new) * acc_scr[hh] + _dot(vt_ref[j, vrows, :], p)

    def step(j, carry):
        all_heads(j, None)
        return carry

    lax.fori_loop(0, qi, step, 0)
    kc = lax.broadcasted_iota(jnp.int32, (TK, TQ), 0) // CHUNK
    qc = lax.broadcasted_iota(jnp.int32, (TK, TQ), 1) // CHUNK
    all_heads(qi, kc <= qc)
    row = lax.broadcasted_iota(jnp.int32, (LANES, TQ), 0)
    pairs = []
    for pr in range(N_HEADS // 2):
        even = acc_scr[2 * pr, :LANES] / acc_scr[2 * pr, LANES:LANES + 1]
        odd = acc_scr[2 * pr + 1, :LANES] / acc_scr[2 * pr + 1, LANES:LANES + 1]
        pairs.append(jnp.where(row < V_DIM, even, odd))
    o = jnp.concatenate(pairs, axis=0).T.astype(BF16)
    out_ref[...] = rows_ref[...] + _dot(o, wo_ref[...])


def _prompt_attn(qt, k, vt, rows, wo):
    nq = SEQ // TQ
    return pl.pallas_call(
        _prompt_attn_body,
        grid=(BATCH, nq),
        in_specs=[
            pl.BlockSpec((1, N_HEADS * HEAD_W, TQ), lambda b, i: (b * nq + i, 0, 0)),
            pl.BlockSpec((SEQ, N_HEADS * HEAD_W), lambda b, i: (b, 0)),
            pl.BlockSpec((SEQ // TK, N_HEADS // 2 * VT_ROWS, TK), lambda b, i: (b, 0, 0)),
            pl.BlockSpec((TQ, D_MODEL), lambda b, i: (b * nq + i, 0)),
            pl.BlockSpec((N_HEADS * V_DIM, D_MODEL), lambda b, i: (0, 0)),
        ],
        out_specs=pl.BlockSpec((TQ, D_MODEL), lambda b, i: (b * nq + i, 0)),
        out_shape=jax.ShapeDtypeStruct((N_PROMPT, D_MODEL), F32),
        scratch_shapes=[pltpu.VMEM((N_HEADS, 1, TQ), F32), pltpu.VMEM((N_HEADS, VT_ROWS, TQ), F32)],
        compiler_params=_cparams(2),
        name="prompt_attn",
    )(qt, k, vt, rows, wo)


def _sample_attn_body(q_ref, kc_ref, vc_ref, kn_ref, vn_ref, rows_ref, wo_ref, out_ref, o_scr):
    outs = []
    for hh in range(N_HEADS):
        hcols = slice(hh * HEAD_W, (hh + 1) * HEAD_W)
        vcols = slice((hh // 2) * LANES, (hh // 2 + 1) * LANES)
        q = q_ref[:, hcols]
        sc = _qk(q, kc_ref[:, hcols])
        sn = _qk(q, kn_ref[:, hcols])
        m = jnp.maximum(jnp.max(sc, axis=-1, keepdims=True), jnp.max(sn, axis=-1, keepdims=True))
        pc = jnp.exp2(sc - m)
        pn = jnp.exp2(sn - m)
        l = jnp.sum(pc, axis=-1, keepdims=True) + jnp.sum(pn, axis=-1, keepdims=True)
        acc = _dot(pc.astype(BF16), vc_ref[:, vcols]) + _dot(pn.astype(BF16), vn_ref[:, vcols])
        outs.append(acc / l)
    _merge_heads(o_scr, outs, DEC_SEQ)
    out_ref[...] = rows_ref[...] + _dot(o_scr[...], wo_ref[...])


def _sample_attn(q, kc, vc, kn, vn, rows, wo):
    off = N_PROMPT // DEC_SEQ
    return pl.pallas_call(
        _sample_attn_body,
        grid=(DEC_BATCH,),
        in_specs=[
            pl.BlockSpec((DEC_SEQ, N_HEADS * HEAD_W), lambda b: (b, 0)),
            pl.BlockSpec((PAST_LEN, N_HEADS * HEAD_W), lambda b: (b, 0)),
            pl.BlockSpec((PAST_LEN, N_HEADS * V_DIM), lambda b: (b, 0)),
            pl.BlockSpec((DEC_SEQ, N_HEADS * HEAD_W), lambda b: (off + b, 0)),
            pl.BlockSpec((DEC_SEQ, N_HEADS * V_DIM), lambda b: (b, 0)),
            pl.BlockSpec((DEC_SEQ, D_MODEL), lambda b: (off + b, 0)),
            pl.BlockSpec((N_HEADS * V_DIM, D_MODEL), lambda b: (0, 0)),
        ],
        out_specs=pl.BlockSpec((DEC_SEQ, D_MODEL), lambda b: (b, 0)),
        out_shape=jax.ShapeDtypeStruct((N_DEC, D_MODEL), F32),
        scratch_shapes=[pltpu.VMEM((DEC_SEQ, N_HEADS * V_DIM), BF16)],
        compiler_params=_cparams(),
        name="sample_attn",
    )(q, kc, vc, kn, vn, rows, wo)


def _finish_body(pos_ref, sorted_hbm, yp_ref, ys_ref, xbuf, xsem):
    i = pl.program_id(0)
    x_ref = _gather_tile(pos_ref, sorted_hbm, xbuf, xsem, TM)

    def relayout(dst_ref):
        for s in range(ROW_TILES):
            dst_ref[:, s * LANES:(s + 1) * LANES] = x_ref[pl.ds(s, TM, stride=ROW_TILES), :]

    @pl.when(i < N_TILES - 1)
    def _():
        relayout(yp_ref)

    @pl.when(i == N_TILES - 1)
    def _():
        relayout(ys_ref)


def _finish(pos, sorted_rows):
    return pl.pallas_call(
        _finish_body,
        grid_spec=pltpu.PrefetchScalarGridSpec(
            num_scalar_prefetch=1,
            grid=(N_TILES,),
            in_specs=[pl.BlockSpec(memory_space=pl.ANY)],
            out_specs=[pl.BlockSpec((TM, D_MODEL), lambda i, p: (jnp.minimum(i, N_TILES - 2), 0)),
                       pl.BlockSpec((N_DEC, D_MODEL), lambda i, p: (0, 0))],
            scratch_shapes=_gather_scratch(TM),
        ),
        out_shape=[jax.ShapeDtypeStruct((N_PROMPT, D_MODEL), F32), jax.ShapeDtypeStruct((N_DEC, D_MODEL), F32)],
        compiler_params=_cparams(),
        name="finish",
    )(pos, sorted_rows)


def _rope_tables():
    half = ROPE_DIM // 2
    inv_freq = ROPE_THETA ** (-jnp.arange(half, dtype=F32) / half)
    dec_pos = PAST_LEN + jnp.tile(jnp.arange(DEC_SEQ, dtype=jnp.int32), DEC_BATCH)
    pos = jnp.concatenate([jnp.arange(SEQ, dtype=jnp.int32), dec_pos])
    ang = pos.astype(F32)[:, None] * inv_freq[None, :]
    cos, sin = jnp.cos(ang), jnp.sin(ang)
    n = pos.shape[0]
    cos_t = jnp.ones((n, LANES), F32).at[:, ROPE_LO:ROPE_LO + ROPE_DIM].set(jnp.concatenate([cos, cos], axis=1))
    sin_t = jnp.zeros((n, LANES), F32).at[:, ROPE_LO:ROPE_LO + ROPE_DIM].set(jnp.concatenate([-sin, sin], axis=1))
    to_tiles = lambda a: a.reshape(n // TM, TM, half).transpose(0, 2, 1)
    return cos_t, sin_t, to_tiles(cos), to_tiles(sin)


def _on_lanes(vec, lo):
    return jnp.zeros((1, LANES), F32).at[0, lo:lo + vec.shape[0]].set(vec)


def kernel(x_prompt, x_sample, cache_ckv, cache_krope, norm_mix, norm_ffn, gm_w_in, gm_b_in, gm_ln_g, gm_ln_b, gm_w_s, gm_b_s, gm_w_out, gm_b_out, kv_norm, w_dkv, kv_a_norm, k_rope_norm, w_uk, w_uv, k_nope_norm, w_dq, q_a_norm, w_uq, q_nope_norm, q_rope_norm, w_o, moe_w_group, moe_b_group, moe_w_expert, moe_b_expert, moe_w1, moe_w3, moe_w2):
    nf0, nf1 = norm_ffn[0].reshape(1, D_MODEL), norm_ffn[1].reshape(1, D_MODEL)

    idx = np.arange(GMLP_BLOCK)
    allowed = (idx[None, :] // CHUNK) <= (idx[:, None] // CHUNK)
    ws_p = jnp.where(allowed[None], gm_w_s[0], 0.0).astype(BF16)
    same_seq = (idx[None, :] // DEC_SEQ) == (idx[:, None] // DEC_SEQ)
    ws_d = jnp.where(same_seq[None], jnp.tile(gm_w_s[0][:, :DEC_SEQ, :DEC_SEQ], (1, GMLP_BLOCK // DEC_SEQ, GMLP_BLOCK // DEC_SEQ)), 0.0).astype(BF16)
    bs_p = gm_b_s[0][:, :, None]
    bs_d = jnp.tile(gm_b_s[0][:, :DEC_SEQ], (1, GMLP_BLOCK // DEC_SEQ))[:, :, None]
    rows, v_rows, info, cnt = _gmlp_layer(
        x_prompt.reshape(N_PROMPT, D_MODEL), x_sample.reshape(N_DEC, D_MODEL),
        norm_mix[0].reshape(1, -1), gm_w_in[0].astype(BF16), gm_b_in[0].reshape(1, -1),
        gm_ln_g[0].reshape(1, -1), gm_ln_b[0].reshape(1, -1), ws_p, ws_d, bs_p, bs_d,
        gm_w_out[0].astype(BF16), gm_b_out[0].reshape(1, -1),
        nf0, *_router_weights(0, moe_w_group, moe_b_group, moe_w_expert, moe_b_expert))
    sorted_rows, pos = _moe_layer(0, rows, info, cnt, nf0, moe_w1, moe_w3, moe_w2)

    cos_t, sin_t, cos_tt, sin_tt = _rope_tables()
    wdkv = jnp.zeros((D_MODEL, KV_RANK + LANES), F32).at[:, :KV_RANK].set(w_dkv[:, :KV_RANK])
    wdkv = wdkv.at[:, KV_RANK + ROPE_LO:KV_RANK + ROPE_LO + ROPE_DIM].set(w_dkv[:, KV_RANK:]).astype(BF16)
    wuk = jnp.zeros((KV_RANK, N_HEADS, HEAD_W), F32).at[:, :, :NOPE_DIM].set(w_uk).reshape(KV_RANK, -1).astype(BF16)
    wukt = w_uk.reshape(KV_RANK, -1).T.astype(BF16)
    wuv = w_uv.reshape(KV_RANK, -1).astype(BF16)
    wuq =jnp.zeros((Q_RANK, N_HEADS, HEAD_W), F32).at[:, :, :NOPE_DIM + ROPE_DIM].set(w_uq[0]).reshape(Q_RANK, -1).astype(BF16)
    kg = _on_lanes(k_nope_norm, 0)
    krg = _on_lanes(k_rope_norm, ROPE_LO)
    qg = _on_lanes(jnp.concatenate([q_nope_norm[0], q_rope_norm[0]]), 0)
    rows, ckv_p, ckv_d, krope_p, krope_d, k_new, v_dec, vt_new, qt, q_dec = _mla_proj(
        pos, sorted_rows, cos_t, sin_t, cos_tt, sin_tt, kv_norm.reshape(1, -1), wdkv, kv_a_norm.reshape(1, -1), krg, wuk, wukt, wuv, kg,
        norm_mix[1].reshape(1, -1), w_dq[0].astype(BF16), q_a_norm[0].reshape(1, -1), wuq, qg)
    k_cache, v_cache = _cache_kv(cache_ckv.reshape(-1, KV_RANK), cache_krope.reshape(-1, ROPE_DIM), wuk, wukt, wuv, kg)

    wo = w_o[0].astype(BF16)
    h_prompt = _prompt_attn(qt, k_new, vt_new, rows, wo)
    h_dec = _sample_attn(q_dec, k_cache, v_cache, k_new, v_dec, rows, wo)
    rows, info, cnt = _router(h_prompt, h_dec, nf1,
                              *_router_weights(1, moe_w_group, moe_b_group, moe_w_expert, moe_b_expert))
    sorted_rows, pos = _moe_layer(1, rows, info, cnt, nf1, moe_w1, moe_w3, moe_w2)
    y_prompt, y_sample = _finish(pos, sorted_rows)

    return (y_prompt.reshape(BATCH, SEQ, D_MODEL), y_sample.reshape(DEC_BATCH, DEC_SEQ, D_MODEL),
            ckv_p.reshape(BATCH, SEQ, KV_RANK), krope_p.reshape(BATCH, SEQ, ROPE_DIM),
            ckv_d.reshape(DEC_BATCH, DEC_SEQ, KV_RANK), krope_d.reshape(DEC_BATCH, DEC_SEQ, ROPE_DIM),
            v_rows.reshape(1, DEC_BATCH, DEC_SEQ, D_GATE))
```

```python
import functools

import jax
import jax.numpy as jnp
import numpy as np
from jax import lax
from jax.experimental import pallas as pl
from jax.experimental.pallas import tpu as pltpu

F32 = jnp.float32
BF16 = jnp.bfloat16

D_MODEL = 1024
BATCH = 8
SEQ = 2048
DEC_BATCH = 16
DEC_SEQ = 16
PAST_LEN = 2048
CHUNK = 64
GMLP_BLOCK = 128
D_GATE = 2 * D_MODEL
N_SG = 8
SG_W = D_GATE // N_SG
N_HEADS = 8
NOPE_DIM = 64
ROPE_DIM = 32
V_DIM = 64
Q_RANK = 384
KV_RANK = 256
ROPE_THETA = 10000.0
SCALE = (NOPE_DIM + ROPE_DIM) ** -0.5
Q_SCALE = SCALE * float(np.log2(np.e))
N_EGROUPS = 4
EXPERTS_PER_GROUP = 4
N_EXPERTS = N_EGROUPS * EXPERTS_PER_GROUP
D_EXPERT = 512
EPS = 1e-6
NEG = -1e30

LANES = 128
SUBLANES = 8
ROW_TILES = D_MODEL // LANES
assert ROW_TILES == SUBLANES

N_PROMPT = BATCH * SEQ
N_DEC = DEC_BATCH * DEC_SEQ
T = N_PROMPT + N_DEC
TM = 256
assert N_PROMPT % TM == 0 and N_DEC == TM
N_TILES = T // TM
HEAD_W = LANES
ROPE_LO = NOPE_DIM
ROPE_HALF = ROPE_DIM // 2

PAIR_A = (0, 0, 0, 1, 1, 3)
PAIR_B = (1, 2, 3, 3, 2, 2)
N_PAIRS = 6
N_BUCKETS = N_EGROUPS * N_PAIRS
BUCKET_ROWS = 32
assert N_BUCKETS <= BUCKET_ROWS and BUCKET_ROWS % SUBLANES == 0
MOE_TILES = (T + N_BUCKETS * (TM - 1) + TM - 1) // TM
P_ROWS = MOE_TILES * TM

VMEM_LIMIT = 56 * 1024 * 1024


def _cparams(n_axes=1, vmem=VMEM_LIMIT):
    return pltpu.CompilerParams(dimension_semantics=("arbitrary",) * n_axes, vmem_limit_bytes=vmem)


def _rms(x, g):
    return x * lax.rsqrt(jnp.mean(x * x, axis=-1, keepdims=True) + EPS) * g


def _load_rows(ref, n):
    return jnp.concatenate([ref[pl.ds(s, n, stride=ROW_TILES), :] for s in range(ROW_TILES)], axis=1)


def _store_rows(ref, x, n):
    for s in range(ROW_TILES):
        ref[pl.ds(s, n, stride=ROW_TILES), :] = x[:, s * LANES:(s + 1) * LANES]


def _dot(a, b):
    return jnp.dot(a, b, preferred_element_type=F32)


GELU_K1 = float(-2.0 * np.sqrt(2.0 / np.pi) * np.log2(np.e))
GELU_K3 = GELU_K1 * 0.044715


def _gmlp_body(xp_ref, xd_ref, nm_ref, win_ref, bin_ref, lng_ref, lnb_ref, wsp_ref, wsd_ref, bsp_ref, bsd_ref,
               wout_ref, bout_ref, nf_ref, wrh_ref, wr2_ref, br_ref,
               rows_ref, v_ref, info_ref, cnt_ref, gated_ref, carry_ref):
    i = pl.program_id(0)
    is_dec = i == N_TILES - 1
    _route_init(carry_ref)
    x = jnp.where(is_dec, xd_ref[...], xp_ref[...])
    xn = _rms(x, nm_ref[...]).astype(BF16)
    z = _dot(xn, win_ref[...]) + bin_ref[...]
    z = z / (1.0 + jnp.exp2(z * (GELU_K1 + GELU_K3 * (z * z))))
    u = z[:, :D_GATE]
    v = z[:, D_GATE:]
    mu = jnp.mean(v, axis=-1, keepdims=True)
    vc = v - mu
    var = jnp.mean(vc * vc, axis=-1, keepdims=True)
    v = vc * lax.rsqrt(var + EPS) * lng_ref[...] + lnb_ref[...]

    v_ref[...] = v
    vb = v.astype(BF16)
    for g in range(N_SG):
        ws = jnp.where(is_dec, wsd_ref[g], wsp_ref[g])
        bs = jnp.where(is_dec, bsd_ref[g], bsp_ref[g])
        for b in range(TM // GMLP_BLOCK):
            rows = slice(b * GMLP_BLOCK, (b + 1) * GMLP_BLOCK)
            cols = slice(g * SG_W, (g + 1) * SG_W)
            s = _dot(ws, vb[rows, cols]) + bs
            gated_ref[rows, cols] = (u[rows, cols] * s).astype(BF16)
    h = x + _dot(gated_ref[...], wout_ref[...]) + bout_ref[...]
    _store_rows(rows_ref, h, TM)
    _route_tile(h, nf_ref, wrh_ref, wr2_ref, br_ref, info_ref, cnt_ref, carry_ref)


def _gmlp_layer(x_prompt, x_dec, nm, w_in, b_in, ln_g, ln_b, ws_p, ws_d, bs_p, bs_d, w_out, b_out, nf, wr, br):
    const = lambda *shape: pl.BlockSpec(shape, lambda i: (0,) * len(shape))
    return pl.pallas_call(
        _gmlp_body,
        grid=(N_TILES,),
        in_specs=[
            pl.BlockSpec((TM, D_MODEL), lambda i: (jnp.minimum(i, N_TILES - 2), 0)), const(N_DEC, D_MODEL),
            const(1, D_MODEL), const(D_MODEL, 2 * D_GATE), const(1, 2 * D_GATE),
            const(1, D_GATE), const(1, D_GATE),
            const(N_SG, GMLP_BLOCK, GMLP_BLOCK), const(N_SG, GMLP_BLOCK, GMLP_BLOCK),
            const(N_SG, GMLP_BLOCK, 1), const(N_SG, GMLP_BLOCK, 1),
            const(D_GATE, D_MODEL), const(1, D_MODEL),
            const(1, D_MODEL), const(D_MODEL, LANES), const(D_MODEL, 2 * LANES), const(1, LANES),
        ],
        out_specs=[
            pl.BlockSpec((TM * ROW_TILES, LANES), lambda i: (i, 0)),
            const(N_DEC, D_GATE),
            pl.BlockSpec((SUBLANES, TM), lambda i: (0, i)),
            const(BUCKET_ROWS, LANES),
        ],
        out_shape=[jax.ShapeDtypeStruct((T * ROW_TILES, LANES), F32), jax.ShapeDtypeStruct((N_DEC, D_GATE), F32),
                   jax.ShapeDtypeStruct((SUBLANES, T), F32), jax.ShapeDtypeStruct((BUCKET_ROWS, LANES), F32)],
        scratch_shapes=[pltpu.VMEM((TM, D_GATE), BF16), pltpu.VMEM((BUCKET_ROWS, LANES), F32)],
        compiler_params=_cparams(),
        name="gmlp_layer",
    )(x_prompt, x_dec, nm, w_in, b_in, ln_g, ln_b, ws_p, ws_d, bs_p, bs_d, w_out, b_out,
      nf, *_router_split(wr), br)


def _router_body(hp_ref, hd_ref, nf_ref, wrh_ref, wr2_ref, br_ref, rows_ref, info_ref, cnt_ref, carry_ref):
    _route_init(carry_ref)
    h = jnp.where(pl.program_id(0) == N_TILES - 1, hd_ref[...], hp_ref[...])
    _store_rows(rows_ref, h, TM)
    _route_tile(h, nf_ref, wrh_ref, wr2_ref, br_ref, info_ref, cnt_ref, carry_ref)


def _router_split(wr):
    hi = wr.astype(BF16)
    return hi, jnp.concatenate([hi, (wr - hi.astype(F32)).astype(BF16)], axis=1)


def _route_init(carry_ref):
    @pl.when(pl.program_id(0) == 0)
    def _():
        carry_ref[...] = jnp.zeros_like(carry_ref)


def _route_tile(h, nf_ref, wrh_ref, wr2_ref, br_ref, info_ref, cnt_ref, carry_ref):
    xn = _rms(h, nf_ref[...])
    xh = xn.astype(BF16)
    xl = (xn - xh.astype(F32)).astype(BF16)
    both = _dot(xh, wr2_ref[...])
    logits = both[:, :LANES] + (_dot(xl, wrh_ref[...]) + both[:, LANES:]) + br_ref[...]
    lt = logits.T

    def first_max(vals, ids):
        vmax = jnp.max(vals, axis=0, keepdims=True)
        return vmax, jnp.min(jnp.where(vals == vmax, ids, float(LANES)), axis=0, keepdims=True)

    lg = lt[N_EXPERTS:N_EXPERTS + N_EGROUPS]
    gmax, g_idx = first_max(lg, lax.broadcasted_iota(jnp.int32, lg.shape, 0).astype(F32))
    g_p = 1.0 / jnp.sum(jnp.exp(lg - gmax), axis=0, keepdims=True)
    e_id = lax.broadcasted_iota(jnp.int32, (N_EXPERTS, TM), 0)
    in_group = (e_id // EXPERTS_PER_GROUP).astype(F32) == g_idx
    e_id = e_id.astype(F32)
    le = jnp.where(in_group, lt[:N_EXPERTS], -jnp.inf)
    v1, i1 = first_max(le, e_id)
    v2, i2 = first_max(jnp.where(e_id == i1, -jnp.inf, le), e_id)
    e2 = jnp.exp(v2 - v1)
    w1 = (1.0 / (1.0 + e2)) * g_p
    w2 = (e2 / (1.0 + e2)) * g_p
    a1 = i1 - EXPERTS_PER_GROUP * g_idx
    a2 = i2 - EXPERTS_PER_GROUP * g_idx
    lo = jnp.minimum(a1, a2)
    hi = jnp.maximum(a1, a2)
    pair = jnp.where(lo == 0.0, hi - 1.0, jnp.where(lo == 1.0, jnp.where(hi == 3.0, 3.0, 4.0), 5.0))
    ea = jnp.where(pair < 3.0, 0.0, jnp.where(pair < 5.0, 1.0, 3.0))
    ga = jnp.where(a1 == ea, w1, w2)
    gb = jnp.where(a1 == ea, w2, w1)
    bucket = g_idx * N_PAIRS + pair

    onehot = (lax.broadcasted_iota(jnp.int32, (BUCKET_ROWS, TM), 0).astype(F32) == bucket).astype(F32)
    r = lax.broadcasted_iota(jnp.int32, (TM, TM), 0)
    c = lax.broadcasted_iota(jnp.int32, (TM, TM), 1)
    before = _dot(onehot.astype(BF16), (r < c).astype(BF16))
    carry = carry_ref[:, 0:1]
    rank = jnp.sum(onehot * (before + carry), axis=0, keepdims=True)
    new_carry = carry + jnp.sum(onehot, axis=1, keepdims=True)
    carry_ref[...] = jnp.broadcast_to(new_carry, carry_ref.shape)
    cnt_ref[...] = jnp.broadcast_to(new_carry, cnt_ref.shape)
    info_ref[...] = jnp.concatenate([bucket, rank, ga, gb, jnp.zeros((SUBLANES - 4, TM), F32)], axis=0)


def _router(h_prompt, h_dec, nf, wr, br):
    const = lambda *shape: pl.BlockSpec(shape, lambda i: (0,) * len(shape))
    return pl.pallas_call(
        _router_body,
        grid=(N_TILES,),
        in_specs=[pl.BlockSpec((TM, D_MODEL), lambda i: (jnp.minimum(i, N_TILES - 2), 0)), const(N_DEC, D_MODEL),
                  const(1, D_MODEL), const(D_MODEL, LANES), const(D_MODEL, 2 * LANES), const(1, LANES)],
        out_specs=[pl.BlockSpec((TM * ROW_TILES, LANES), lambda i: (i, 0)),
                   pl.BlockSpec((SUBLANES, TM), lambda i: (0, i)),
                   const(BUCKET_ROWS, LANES)],
        out_shape=[jax.ShapeDtypeStruct((T * ROW_TILES, LANES), F32), jax.ShapeDtypeStruct((SUBLANES, T), F32),
                   jax.ShapeDtypeStruct((BUCKET_ROWS, LANES), F32)],
        scratch_shapes=[pltpu.VMEM((BUCKET_ROWS, LANES), F32)],
        compiler_params=_cparams(),
        name="moe_router",
    )(h_prompt, h_dec, nf, *_router_split(wr), br)


GATHER_UNROLL = 64
GATHER_DEPTH = 4


def _gather_rows_start(idx_ref, base, src_ref, dst_ref, sem, n):
    def group(g, carry):
        for u in range(GATHER_UNROLL):
            r = g * GATHER_UNROLL + u
            src = pl.multiple_of(idx_ref[base + r] * ROW_TILES, ROW_TILES)
            dst = pl.multiple_of(r * ROW_TILES, ROW_TILES)
            pltpu.make_async_copy(src_ref.at[pl.ds(src, ROW_TILES), :], dst_ref.at[pl.ds(dst, ROW_TILES), :],
                                  sem).start(priority=u % 2)
        return carry

    lax.fori_loop(0, n // GATHER_UNROLL, group, 0)


def _gather_rows_wait(src_ref, dst_ref, sem, n):
    pltpu.make_async_copy(src_ref.at[pl.ds(0, n * ROW_TILES), :], dst_ref.at[pl.ds(0, n * ROW_TILES), :], sem).wait()


def _gather_tile(idx_ref, src_ref, buf, sem, n, n_live=None, base_of=None, side_copy=None):
    i = pl.program_id(0)
    slot = lax.rem(i, GATHER_DEPTH)
    n_live = pl.num_programs(0) if n_live is None else n_live
    base_of = (lambda step: step * n) if base_of is None else base_of
    ahead = GATHER_DEPTH - 1

    def start(step, s):
        _gather_rows_start(idx_ref, base_of(step), src_ref, buf.at[s], sem.at[s], n)
        if side_copy is not None:
            side_copy(step, s).start()

    for d in range(ahead):
        @pl.when((i == 0) & (d < n_live))
        def _(d=d):
            start(d, d)

    @pl.when(i + ahead < n_live)
    def _():
        start(i + ahead, lax.rem(i + ahead, GATHER_DEPTH))

    @pl.when(i < n_live)
    def _():
        _gather_rows_wait(src_ref, buf.at[slot], sem.at[slot], n)
        if side_copy is not None:
            side_copy(i, slot).wait()

    return buf.at[slot]


def _gather_scratch(n):
    return [pltpu.VMEM((GATHER_DEPTH, n * ROW_TILES, LANES), F32), pltpu.SemaphoreType.DMA((GATHER_DEPTH,))]


GATE_WIN = TM + SUBLANES


def _moe_body(tok_ref, ea_ref, eb_ref, cha_ref, chb_ref, nlive_ref, first_ref, nrows_ref, rows_hbm, gates_hbm, nf_ref,
              w1a_ref, w3a_ref, w2a_ref, w1b_ref, w3b_ref, w2b_ref, out_ref,
              s1a, s3a, s2a, s1b, s3b, s2b, xbuf, xsem, gwin):
    i = pl.program_id(0)
    live = i < nlive_ref[0]

    def gate_copy(step, s):
        lo = pl.multiple_of((first_ref[step] // SUBLANES) * SUBLANES, SUBLANES)
        return pltpu.make_async_copy(gates_hbm.at[pl.ds(lo, GATE_WIN), :], gwin.at[s], xsem.at[s])

    x_ref = _gather_tile(tok_ref, rows_hbm, xbuf, xsem, TM, nlive_ref[0],
                         base_of=lambda step: first_ref[step], side_copy=gate_copy)

    @pl.when(cha_ref[i] == 1)
    def _():
        s1a[...] = w1a_ref[...].astype(BF16)
        s3a[...] = w3a_ref[...].astype(BF16)
        s2a[...] = w2a_ref[...].astype(BF16)

    @pl.when(chb_ref[i] == 1)
    def _():
        s1b[...] = w1b_ref[...].astype(BF16)
        s3b[...] = w3b_ref[...].astype(BF16)
        s2b[...] = w2b_ref[...].astype(BF16)

    @pl.when(live)
    def _():
        h = _load_rows(x_ref, TM)
        xn = _rms(h, nf_ref[...]).astype(BF16)
        g = gwin[lax.rem(i, GATHER_DEPTH), pl.ds(lax.rem(first_ref[i], SUBLANES), TM), :]
        row = lax.broadcasted_iota(jnp.int32, (TM, 1), 0)
        g = jnp.where(row < nrows_ref[i], g, 0.0)

        def ffn(w1, w3, w2):
            a = _dot(xn, w1[...])
            hdn = (a * (1.0 / (1.0 + jnp.exp(-a)))) * _dot(xn, w3[...])
            return _dot(hdn.astype(BF16), w2[...])

        y = g[:, 0:1] * ffn(s1a, s3a, s2a) + g[:, 1:2] * ffn(s1b, s3b, s2b)
        _store_rows(out_ref, h + y, TM)

    @pl.when(jnp.logical_not(live))
    def _():
        out_ref[...] = jnp.zeros_like(out_ref)


def _moe_ffn(layer, tok, ea, eb, cha, chb, n_live, first, nrows, rows, gates, nf, w1, w3, w2):
    wa = lambda shape: pl.BlockSpec((None, None) + shape, lambda i, tk, ea, eb, *_: (layer, ea[i], 0, 0))
    wb = lambda shape: pl.BlockSpec((None, None) + shape, lambda i, tk, ea, eb, *_: (layer, eb[i], 0, 0))
    up, down = (D_MODEL, D_EXPERT), (D_EXPERT, D_MODEL)
    return pl.pallas_call(
        _moe_body,
        grid_spec=pltpu.PrefetchScalarGridSpec(
            num_scalar_prefetch=8,
            grid=(MOE_TILES,),
            in_specs=[
                pl.BlockSpec(memory_space=pl.ANY),
                pl.BlockSpec(memory_space=pl.ANY),
                pl.BlockSpec((1, D_MODEL), lambda i, *_: (0, 0)),
                wa(up), wa(up), wa(down), wb(up), wb(up), wb(down),
            ],
            out_specs=pl.BlockSpec((TM * ROW_TILES, LANES), lambda i, *_: (i, 0)),
            scratch_shapes=[pltpu.VMEM(up, BF16), pltpu.VMEM(up, BF16), pltpu.VMEM(down, BF16),
                            pltpu.VMEM(up, BF16), pltpu.VMEM(up, BF16), pltpu.VMEM(down, BF16)]
            + _gather_scratch(TM) + [pltpu.VMEM((GATHER_DEPTH, GATE_WIN, 2), F32)],
        ),
        out_shape=jax.ShapeDtypeStruct((P_ROWS * ROW_TILES, LANES), F32),
        compiler_params=_cparams(),
        name="moe_ffn",
    )(tok, ea, eb, cha, chb, n_live, first, nrows, rows, gates, nf, w1, w3, w2, w1, w3, w2)


def _router_weights(layer, w_group, b_group, w_expert, b_expert):
    pad = LANES - N_EXPERTS - N_EGROUPS
    wr = jnp.concatenate([w_expert[layer], w_group[layer], jnp.zeros((D_MODEL, pad), F32)], axis=1)
    br = jnp.concatenate([b_expert[layer], b_group[layer], jnp.zeros((pad,), F32)]).reshape(1, LANES)
    return wr, br


def _moe_layer(layer, rows, info, cnt, nf, w1, w3, w2):
    bucket = info[0].astype(jnp.int32)
    rank = info[1].astype(jnp.int32)
    counts = cnt[:N_BUCKETS, 0].astype(jnp.int32)
    n_tiles = (counts + TM - 1) // TM
    tile_end = jnp.cumsum(n_tiles)
    tile_start = tile_end - n_tiles
    start_of = jnp.sum(jnp.where(bucket[:, None] == jnp.arange(N_BUCKETS)[None, :], tile_start[None, :], 0), axis=1)
    pos = start_of * TM + rank
    _, tok, ga, gb = lax.sort((pos, jnp.arange(T, dtype=jnp.int32), info[2], info[3]), num_keys=1)
    tok = jnp.concatenate([tok, jnp.zeros((TM,), jnp.int32)])
    gates = jnp.concatenate([jnp.stack([ga, gb], axis=1), jnp.zeros((TM + SUBLANES, 2), F32)])
    total = tile_end[-1]
    j = jnp.minimum(jnp.arange(MOE_TILES), total - 1)
    tb = jnp.sum((j[:, None] >= tile_end[None, :]).astype(jnp.int32), axis=1)
    tokens_before = jnp.cumsum(counts) - counts
    in_bucket = (j - tile_start[tb]) * TM
    first = (tokens_before[tb] + in_bucket).astype(jnp.int32)
    nrows = jnp.clip(counts[tb] - in_bucket, 0, TM).astype(jnp.int32)
    grp, pair = tb // N_PAIRS, tb % N_PAIRS
    ea = (grp * EXPERTS_PER_GROUP + jnp.asarray(PAIR_A, jnp.int32)[pair]).astype(jnp.int32)
    eb = (grp * EXPERTS_PER_GROUP + jnp.asarray(PAIR_B, jnp.int32)[pair]).astype(jnp.int32)
    tile0 = jnp.arange(MOE_TILES) == 0
    cha = (tile0 | (ea != jnp.roll(ea, 1))).astype(jnp.int32)
    chb = (tile0 | (eb != jnp.roll(eb, 1))).astype(jnp.int32)
    n_live = total.reshape(1).astype(jnp.int32)
    return _moe_ffn(layer, tok, ea, eb, cha, chb, n_live, first, nrows, rows, gates, nf, w1, w3, w2), pos


VT_ROWS = LANES + 16


def _rope_swap(x):
    lane = lax.broadcasted_iota(jnp.int32, x.shape, 1)
    return jnp.where(lane < ROPE_LO + ROPE_HALF, pltpu.roll(x, LANES - ROPE_HALF, 1), pltpu.roll(x, ROPE_HALF, 1))


def _expand_k(cb, cbt, kr, wuk_ref, wukt_ref, kg_ref, k_ref):
    n = cb.shape[0]
    kn = _dot(cb, wuk_ref[...])
    knt = _dot(wukt_ref[...], cbt)
    ms = [jnp.mean(jnp.square(knt[hh * NOPE_DIM:(hh + 1) * NOPE_DIM]), axis=0, keepdims=True)
          for hh in range(N_HEADS)]
    scale = lax.rsqrt(jnp.concatenate(ms + [jnp.ones((LANES - N_HEADS, n), F32)], axis=0) + EPS).T
    for hh in range(N_HEADS):
        cols = slice(hh * HEAD_W, (hh + 1) * HEAD_W)
        k_ref[:, cols] = (kn[:, cols] * scale[:, hh:hh + 1] * kg_ref[...] + kr).astype(BF16)


def _mla_proj_body(pos_ref, sorted_hbm, cos_ref, sin_ref, cost_ref, sint_ref, kvn_ref, wdkv_ref, kvan_ref, krg_ref,
                   wuk_ref, wuv_ref, wuvt_ref, kg_ref, nmq_ref, wdq_ref, qan_ref, wuqt_ref, qg_ref, wukt_ref,
                   rows_ref, ckvp_ref, ckvd_ref, krp_ref, krd_ref, k_ref, vdec_ref, vt_ref, qt_ref, qdec_ref,
                   xbuf, xsem):
    is_dec = pl.program_id(0) == N_TILES - 1
    x_ref = _gather_tile(pos_ref, sorted_hbm, xbuf, xsem, TM)
    h = _load_rows(x_ref, TM)
    rows_ref[...] = h
    hn = h * lax.rsqrt(jnp.mean(h * h, axis=-1, keepdims=True) + EPS)
    c = _dot((hn * kvn_ref[...]).astype(BF16), wdkv_ref[...])
    ckv = _rms(c[:, :KV_RANK], kvan_ref[...])
    kr = c[:, KV_RANK:]
    kr = kr * lax.rsqrt(jnp.sum(kr * kr, axis=-1, keepdims=True) * (1.0 / ROPE_DIM) + EPS) * krg_ref[...]
    kr = kr * cos_ref[...] + _rope_swap(kr) * sin_ref[...]
    cb = ckv.astype(BF16)
    cbt = ckv.T.astype(BF16)
    _expand_k(cb, cbt, kr, wuk_ref, wukt_ref, kg_ref, k_ref)
    vt = _dot(wuvt_ref[...], cbt).astype(BF16)
    for pr in range(N_HEADS // 2):
        vt_ref[0, pr * VT_ROWS:pr * VT_ROWS + LANES, :] = vt[pr * LANES:(pr + 1) * LANES]
        vt_ref[0, pr * VT_ROWS + LANES:(pr + 1) * VT_ROWS, :] = jnp.ones((VT_ROWS - LANES, TM), BF16)
    cq = _rms(_dot((hn * nmq_ref[...]).astype(BF16), wdq_ref[...]), qan_ref[...])
    qt = _dot(wuqt_ref[...], cq.T.astype(BF16))
    cost, sint, qg = cost_ref[0], sint_ref[0], qg_ref[...]
    for hh in range(N_HEADS):
        x = qt[hh * HEAD_W:(hh + 1) * HEAD_W, :]
        xn, xr = x[:NOPE_DIM], x[ROPE_LO:ROPE_LO + ROPE_DIM]
        xn = xn * lax.rsqrt(jnp.mean(xn * xn, axis=0, keepdims=True) + EPS) * qg[:NOPE_DIM]
        xr = xr * lax.rsqrt(jnp.mean(xr * xr, axis=0, keepdims=True) + EPS) * qg[ROPE_LO:ROPE_LO + ROPE_DIM]
        x1, x2 = xr[:ROPE_HALF], xr[ROPE_HALF:]
        qh = jnp.concatenate([xn, x1 * cost - x2 * sint, x1 * sint + x2 * cost,
                              jnp.zeros((HEAD_W - NOPE_DIM - ROPE_DIM, TM), F32)], axis=0) * Q_SCALE
        qt_ref[0, hh * HEAD_W:(hh + 1) * HEAD_W, :] = qh.astype(BF16)

    @pl.when(jnp.logical_not(is_dec))
    def _():
        ckvp_ref[...] = ckv
        krp_ref[...] = kr[:, ROPE_LO:ROPE_LO + ROPE_DIM]

    @pl.when(is_dec)
    def _():
        ckvd_ref[...] = ckv
        krd_ref[...] = kr[:, ROPE_LO:ROPE_LO + ROPE_DIM]
        vdec_ref[...] = _dot(cb, wuv_ref[...]).astype(BF16)
        for hh in range(N_HEADS):
            rows = slice(hh * HEAD_W, (hh + 1) * HEAD_W)
            qdec_ref[:, rows] = qt_ref[0, rows, :].astype(F32).T.astype(BF16)


def _mla_proj(pos, sorted_rows, cos_t, sin_t, cos_tt, sin_tt, kvn, wdkv, kvan, krg, wuk, wukt, wuv, kg, nmq, wdq, qan, wuq, qg):
    const = lambda *shape: pl.BlockSpec(shape, lambda i, p: (0,) * len(shape))
    tab_tile = lambda i: jnp.where(i < N_PROMPT // TM, i % (SEQ // TM), SEQ // TM)
    tab = pl.BlockSpec((TM, LANES), lambda i, p: (tab_tile(i), 0))
    tab_t = pl.BlockSpec((1, ROPE_HALF, TM), lambda i, p: (tab_tile(i), 0, 0))
    row = lambda w: pl.BlockSpec((TM, w), lambda i, p: (i, 0))
    prow = lambda w: pl.BlockSpec((TM, w), lambda i, p: (jnp.minimum(i, N_TILES - 2), 0))
    return pl.pallas_call(
        _mla_proj_body,
        grid_spec=pltpu.PrefetchScalarGridSpec(
            num_scalar_prefetch=1,
            grid=(N_TILES,),
            in_specs=[
                pl.BlockSpec(memory_space=pl.ANY), tab, tab, tab_t, tab_t,
                const(1, D_MODEL), const(D_MODEL, KV_RANK + LANES), const(1, KV_RANK), const(1, LANES),
                const(KV_RANK, N_HEADS * HEAD_W), const(KV_RANK, N_HEADS * V_DIM), const(N_HEADS * V_DIM, KV_RANK),
                const(1, LANES),
                const(1, D_MODEL), const(D_MODEL, Q_RANK), const(1, Q_RANK), const(N_HEADS * HEAD_W, Q_RANK),
                const(HEAD_W, TM), const(N_HEADS * NOPE_DIM, KV_RANK),
            ],
            out_specs=[row(D_MODEL),
                       prow(KV_RANK), const(N_DEC, KV_RANK), prow(ROPE_DIM), const(N_DEC, ROPE_DIM),
                       row(N_HEADS * HEAD_W), const(N_DEC, N_HEADS * V_DIM),
                       pl.BlockSpec((1, N_HEADS // 2 * VT_ROWS, TM), lambda i, p: (i, 0, 0)),
                       pl.BlockSpec((1, N_HEADS * HEAD_W, TM), lambda i, p: (i, 0, 0)),
                       const(N_DEC, N_HEADS * HEAD_W)],
            scratch_shapes=_gather_scratch(TM),
        ),
        out_shape=[
            jax.ShapeDtypeStruct((T, D_MODEL), F32),
            jax.ShapeDtypeStruct((N_PROMPT, KV_RANK), F32), jax.ShapeDtypeStruct((N_DEC, KV_RANK), F32),
            jax.ShapeDtypeStruct((N_PROMPT, ROPE_DIM), F32), jax.ShapeDtypeStruct((N_DEC, ROPE_DIM), F32),
            jax.ShapeDtypeStruct((T, N_HEADS * HEAD_W), BF16), jax.ShapeDtypeStruct((N_DEC, N_HEADS * V_DIM), BF16),
            jax.ShapeDtypeStruct((N_TILES, N_HEADS // 2 * VT_ROWS, TM), BF16),
            jax.ShapeDtypeStruct((N_TILES, N_HEADS * HEAD_W, TM), BF16),
            jax.ShapeDtypeStruct((N_DEC, N_HEADS * HEAD_W), BF16),
        ],
        compiler_params=_cparams(),
        name="mla_proj",
    )(pos, sorted_rows, cos_t, sin_t, cos_tt, sin_tt, kvn, wdkv, kvan, krg, wuk, wuv, wuv.T, kg, nmq, wdq, qan,
      wuq.T, jnp.broadcast_to(qg.reshape(HEAD_W, 1), (HEAD_W, TM)), wukt)


def _cache_kv_body(ckv_ref, kr_ref, place_ref, wuk_ref, wuv_ref, kg_ref, wukt_ref, k_ref, v_ref):
    kr = _dot(kr_ref[...].astype(BF16), place_ref[...])
    cb = ckv_ref[...].astype(BF16)
    _expand_k(cb, ckv_ref[...].T.astype(BF16), kr, wuk_ref, wukt_ref, kg_ref, k_ref)
    v_ref[...] = _dot(cb, wuv_ref[...]).astype(BF16)


CACHE_ROWS = 1024
assert PAST_LEN % CACHE_ROWS == 0


def _cache_kv(ckv, kr, wuk, wukt, wuv, kg):
    n = ckv.shape[0]
    place = jnp.asarray(np.arange(ROPE_DIM)[:, None] + ROPE_LO == np.arange(LANES)[None, :], BF16)
    const = lambda *shape: pl.BlockSpec(shape, lambda i: (0,) * len(shape))
    row = lambda w: pl.BlockSpec((CACHE_ROWS, w), lambda i: (i, 0))
    return pl.pallas_call(
        _cache_kv_body,
        grid=(n // CACHE_ROWS,),
        in_specs=[row(KV_RANK), row(ROPE_DIM), const(ROPE_DIM, LANES), const(KV_RANK, N_HEADS * HEAD_W),
                  const(KV_RANK, N_HEADS * V_DIM), const(1, LANES), const(N_HEADS * NOPE_DIM, KV_RANK)],
        out_specs=[row(N_HEADS * HEAD_W), row(N_HEADS * V_DIM)],
        out_shape=[jax.ShapeDtypeStruct((n, N_HEADS * HEAD_W), BF16), jax.ShapeDtypeStruct((n, N_HEADS * V_DIM), BF16)],
        compiler_params=_cparams(),
        name="cache_kv",
    )(ckv, kr, place, wuk, wuv, kg, wukt)


TQ = 256
TK = 256
assert TQ == TK and TQ % CHUNK == 0
SCORE_LOOKAHEAD = 8


def _qk(q, k):
    return lax.dot_general(q, k, (((1,), (1,)), ((), ())), preferred_element_type=F32)


def _merge_heads(o_ref, outs, rows):
    lane = lax.broadcasted_iota(jnp.int32, (rows, LANES), 1)
    for pr in range(N_HEADS // 2):
        o_ref[:, pr * LANES:(pr + 1) * LANES] = jnp.where(lane < V_DIM, outs[2 * pr], outs[2 * pr + 1]).astype(BF16)


def _prompt_attn_body(qt_ref, k_ref, vt_ref, rows_ref, wo_ref, out_ref, m_scr, acc_scr):
    qi = pl.program_id(1)
    m_scr[...] = jnp.full(m_scr.shape, NEG, F32)
    acc_scr[...] = jnp.zeros(acc_scr.shape, F32)

    def all_heads(j, mask):
        ks = pl.ds(pl.multiple_of(j * TK, TK), TK)

        def scores(hh):
            hcols = slice(hh * HEAD_W, (hh + 1) * HEAD_W)
            return _dot(k_ref[ks, hcols], qt_ref[0, hcols, :])

        ahead = [scores(hh) for hh in range(SCORE_LOOKAHEAD)]
        for hh in range(N_HEADS):
            vrows = slice((hh // 2) * VT_ROWS, (hh // 2 + 1) * VT_ROWS)
            s = ahead.pop(0)
            if hh + SCORE_LOOKAHEAD < N_HEADS:
                ahead.append(scores(hh + SCORE_LOOKAHEAD))
            if mask is not None:
                s = jnp.where(mask, s, NEG)
            m_old = m_scr[hh]
            m_new = jnp.maximum(m_old, jnp.max(s, axis=0, keepdims=True))
            p = jnp.exp2(s - m_new).astype(BF16)
            m_scr[hh] = m_new
            acc_scr[hh] = jnp.exp2(m_old - m_new) * acc_scr[hh] + _dot(vt_ref[j, vrows, :], p)

    def step(j, carry):
        all_heads(j, None)
        return carry

    lax.fori_loop(0, qi, step, 0)
    kc = lax.broadcasted_iota(jnp.int32, (TK, TQ), 0) // CHUNK
    qc = lax.broadcasted_iota(jnp.int32, (TK, TQ), 1) // CHUNK
    all_heads(qi, kc <= qc)
    row = lax.broadcasted_iota(jnp.int32, (LANES, TQ), 0)
    pairs = []
    for pr in range(N_HEADS // 2):
        even = acc_scr[2 * pr, :LANES] / acc_scr[2 * pr, LANES:LANES + 1]
        odd = acc_scr[2 * pr + 1, :LANES] / acc_scr[2 * pr + 1, LANES:LANES + 1]
        pairs.append(jnp.where(row < V_DIM, even, odd))
    o = jnp.concatenate(pairs, axis=0).T.astype(BF16)
    out_ref[...] = rows_ref[...] + _dot(o, wo_ref[...])


def _prompt_attn(qt, k, vt, rows, wo):
    nq = SEQ // TQ
    return pl.pallas_call(
        _prompt_attn_body,
        grid=(BATCH, nq),
        in_specs=[
            pl.BlockSpec((1, N_HEADS * HEAD_W, TQ), lambda b, i: (b * nq + i, 0, 0)),
            pl.BlockSpec((SEQ, N_HEADS * HEAD_W), lambda b, i: (b, 0)),
            pl.BlockSpec((SEQ // TK, N_HEADS // 2 * VT_ROWS, TK), lambda b, i: (b, 0, 0)),
            pl.BlockSpec((TQ, D_MODEL), lambda b, i: (b * nq + i, 0)),
            pl.BlockSpec((N_HEADS * V_DIM, D_MODEL), lambda b, i: (0, 0)),
        ],
        out_specs=pl.BlockSpec((TQ, D_MODEL), lambda b, i: (b * nq + i, 0)),
        out_shape=jax.ShapeDtypeStruct((N_PROMPT, D_MODEL), F32),
        scratch_shapes=[pltpu.VMEM((N_HEADS, 1, TQ), F32), pltpu.VMEM((N_HEADS, VT_ROWS, TQ), F32)],
        compiler_params=_cparams(2),
        name="prompt_attn",
    )(qt, k, vt, rows, wo)


def _sample_attn_body(q_ref, kc_ref, vc_ref, kn_ref, vn_ref, rows_ref, wo_ref, out_ref, o_scr):
    outs = []
    for hh in range(N_HEADS):
        hcols = slice(hh * HEAD_W, (hh + 1) * HEAD_W)
        vcols = slice((hh // 2) * LANES, (hh // 2 + 1) * LANES)
        q = q_ref[:, hcols]
        sc = _qk(q, kc_ref[:, hcols])
        sn = _qk(q, kn_ref[:, hcols])
        m = jnp.maximum(jnp.max(sc, axis=-1, keepdims=True), jnp.max(sn, axis=-1, keepdims=True))
        pc = jnp.exp2(sc - m)
        pn = jnp.exp2(sn - m)
        l = jnp.sum(pc, axis=-1, keepdims=True) + jnp.sum(pn, axis=-1, keepdims=True)
        acc = _dot(pc.astype(BF16), vc_ref[:, vcols]) + _dot(pn.astype(BF16), vn_ref[:, vcols])
        outs.append(acc / l)
    _merge_heads(o_scr, outs, DEC_SEQ)
    out_ref[...] = rows_ref[...] + _dot(o_scr[...], wo_ref[...])


def _sample_attn(q, kc, vc, kn, vn, rows, wo):
    off = N_PROMPT // DEC_SEQ
    return pl.pallas_call(
        _sample_attn_body,
        grid=(DEC_BATCH,),
        in_specs=[
            pl.BlockSpec((DEC_SEQ, N_HEADS * HEAD_W), lambda b: (b, 0)),
            pl.BlockSpec((PAST_LEN, N_HEADS * HEAD_W), lambda b: (b, 0)),
            pl.BlockSpec((PAST_LEN, N_HEADS * V_DIM), lambda b: (b, 0)),
            pl.BlockSpec((DEC_SEQ, N_HEADS * HEAD_W), lambda b: (off + b, 0)),
            pl.BlockSpec((DEC_SEQ, N_HEADS * V_DIM), lambda b: (b, 0)),
            pl.BlockSpec((DEC_SEQ, D_MODEL), lambda b: (off + b, 0)),
            pl.BlockSpec((N_HEADS * V_DIM, D_MODEL), lambda b: (0, 0)),
        ],
        out_specs=pl.BlockSpec((DEC_SEQ, D_MODEL), lambda b: (b, 0)),
        out_shape=jax.ShapeDtypeStruct((N_DEC, D_MODEL), F32),
        scratch_shapes=[pltpu.VMEM((DEC_SEQ, N_HEADS * V_DIM), BF16)],
        compiler_params=_cparams(),
        name="sample_attn",
    )(q, kc, vc, kn, vn, rows, wo)


def _finish_body(pos_ref, sorted_hbm, yp_ref, ys_ref, xbuf, xsem):
    i = pl.program_id(0)
    x_ref = _gather_tile(pos_ref, sorted_hbm, xbuf, xsem, TM)

    def relayout(dst_ref):
        for s in range(ROW_TILES):
            dst_ref[:, s * LANES:(s + 1) * LANES] = x_ref[pl.ds(s, TM, stride=ROW_TILES), :]

    @pl.when(i < N_TILES - 1)
    def _():
        relayout(yp_ref)

    @pl.when(i == N_TILES - 1)
    def _():
        relayout(ys_ref)


def _finish(pos, sorted_rows):
    return pl.pallas_call(
        _finish_body,
        grid_spec=pltpu.PrefetchScalarGridSpec(
            num_scalar_prefetch=1,
            grid=(N_TILES,),
            in_specs=[pl.BlockSpec(memory_space=pl.ANY)],
            out_specs=[pl.BlockSpec((TM, D_MODEL), lambda i, p: (jnp.minimum(i, N_TILES - 2), 0)),
                       pl.BlockSpec((N_DEC, D_MODEL), lambda i, p: (0, 0))],
            scratch_shapes=_gather_scratch(TM),
        ),
        out_shape=[jax.ShapeDtypeStruct((N_PROMPT, D_MODEL), F32), jax.ShapeDtypeStruct((N_DEC, D_MODEL), F32)],
        compiler_params=_cparams(),
        name="finish",
    )(pos, sorted_rows)


def _rope_tables():
    half = ROPE_DIM // 2
    inv_freq = ROPE_THETA ** (-jnp.arange(half, dtype=F32) / half)
    dec_pos = PAST_LEN + jnp.tile(jnp.arange(DEC_SEQ, dtype=jnp.int32), DEC_BATCH)
    pos = jnp.concatenate([jnp.arange(SEQ, dtype=jnp.int32), dec_pos])
    ang = pos.astype(F32)[:, None] * inv_freq[None, :]
    cos, sin = jnp.cos(ang), jnp.sin(ang)
    n = pos.shape[0]
    cos_t = jnp.ones((n, LANES), F32).at[:, ROPE_LO:ROPE_LO + ROPE_DIM].set(jnp.concatenate([cos, cos], axis=1))
    sin_t = jnp.zeros((n, LANES), F32).at[:, ROPE_LO:ROPE_LO + ROPE_DIM].set(jnp.concatenate([-sin, sin], axis=1))
    to_tiles = lambda a: a.reshape(n // TM, TM, half).transpose(0, 2, 1)
    return cos_t, sin_t, to_tiles(cos), to_tiles(sin)


def _on_lanes(vec, lo):
    return jnp.zeros((1, LANES), F32).at[0, lo:lo + vec.shape[0]].set(vec)


def kernel(x_prompt, x_sample, cache_ckv, cache_krope, norm_mix, norm_ffn, gm_w_in, gm_b_in, gm_ln_g, gm_ln_b, gm_w_s, gm_b_s, gm_w_out, gm_b_out, kv_norm, w_dkv, kv_a_norm, k_rope_norm, w_uk, w_uv, k_nope_norm, w_dq, q_a_norm, w_uq, q_nope_norm, q_rope_norm, w_o, moe_w_group, moe_b_group, moe_w_expert, moe_b_expert, moe_w1, moe_w3, moe_w2):
    nf0, nf1 = norm_ffn[0].reshape(1, D_MODEL), norm_ffn[1].reshape(1, D_MODEL)

    idx = np.arange(GMLP_BLOCK)
    allowed = (idx[None, :] // CHUNK) <= (idx[:, None] // CHUNK)
    ws_p = jnp.where(allowed[None], gm_w_s[0], 0.0).astype(BF16)
    same_seq = (idx[None, :] // DEC_SEQ) == (idx[:, None] // DEC_SEQ)
    ws_d = jnp.where(same_seq[None], jnp.tile(gm_w_s[0][:, :DEC_SEQ, :DEC_SEQ], (1, GMLP_BLOCK // DEC_SEQ, GMLP_BLOCK // DEC_SEQ)), 0.0).astype(BF16)
    bs_p = gm_b_s[0][:, :, None]
    bs_d = jnp.tile(gm_b_s[0][:, :DEC_SEQ], (1, GMLP_BLOCK // DEC_SEQ))[:, :, None]
    rows, v_rows, info, cnt = _gmlp_layer(
        x_prompt.reshape(N_PROMPT, D_MODEL), x_sample.reshape(N_DEC, D_MODEL),
        norm_mix[0].reshape(1, -1), gm_w_in[0].astype(BF16), gm_b_in[0].reshape(1, -1),
        gm_ln_g[0].reshape(1, -1), gm_ln_b[0].reshape(1, -1), ws_p, ws_d, bs_p, bs_d,
        gm_w_out[0].astype(BF16), gm_b_out[0].reshape(1, -1),
        nf0, *_router_weights(0, moe_w_group, moe_b_group, moe_w_expert, moe_b_expert))
    sorted_rows, pos = _moe_layer(0, rows, info, cnt, nf0, moe_w1, moe_w3, moe_w2)

    cos_t, sin_t, cos_tt, sin_tt = _rope_tables()
    wdkv = jnp.zeros((D_MODEL, KV_RANK + LANES), F32).at[:, :KV_RANK].set(w_dkv[:, :KV_RANK])
    wdkv = wdkv.at[:, KV_RANK + ROPE_LO:KV_RANK + ROPE_LO + ROPE_DIM].set(w_dkv[:, KV_RANK:]).astype(BF16)
    wuk = jnp.zeros((KV_RANK, N_HEADS, HEAD_W), F32).at[:, :, :NOPE_DIM].set(w_uk).reshape(KV_RANK, -1).astype(BF16)
    wukt = w_uk.reshape(KV_RANK, -1).T.astype(BF16)
    wuv = w_uv.reshape(KV_RANK, -1).astype(BF16)
    wuq =jnp.zeros((Q_RANK, N_HEADS, HEAD_W), F32).at[:, :, :NOPE_DIM + ROPE_DIM].set(w_uq[0]).reshape(Q_RANK, -1).astype(BF16)
    kg = _on_lanes(k_nope_norm, 0)
    krg = _on_lanes(k_rope_norm, ROPE_LO)
    qg = _on_lanes(jnp.concatenate([q_nope_norm[0], q_rope_norm[0]]), 0)
    rows, ckv_p, ckv_d, krope_p, krope_d, k_new, v_dec, vt_new, qt, q_dec = _mla_proj(
        pos, sorted_rows, cos_t, sin_t, cos_tt, sin_tt, kv_norm.reshape(1, -1), wdkv, kv_a_norm.reshape(1, -1), krg, wuk, wukt, wuv, kg,
        norm_mix[1].reshape(1, -1), w_dq[0].astype(BF16), q_a_norm[0].reshape(1, -1), wuq, qg)
    k_cache, v_cache = _cache_kv(cache_ckv.reshape(-1, KV_RANK), cache_krope.reshape(-1, ROPE_DIM), wuk, wukt, wuv, kg)

    wo = w_o[0].astype(BF16)
    h_prompt = _prompt_attn(qt, k_new, vt_new, rows, wo)
    h_dec = _sample_attn(q_dec, k_cache, v_cache, k_new, v_dec, rows, wo)
    rows, info, cnt = _router(h_prompt, h_dec, nf1,
                              *_router_weights(1, moe_w_group, moe_b_group, moe_w_expert, moe_b_expert))
    sorted_rows, pos = _moe_layer(1, rows, info, cnt, nf1, moe_w1, moe_w3, moe_w2)
    y_prompt, y_sample = _finish(pos, sorted_rows)

    return (y_prompt.reshape(BATCH, SEQ, D_MODEL), y_sample.reshape(DEC_BATCH, DEC_SEQ, D_MODEL),
            ckv_p.reshape(BATCH, SEQ, KV_RANK), krope_p.reshape(BATCH, SEQ, ROPE_DIM),
            ckv_d.reshape(DEC_BATCH, DEC_SEQ, KV_RANK), krope_d.reshape(DEC_BATCH, DEC_SEQ, ROPE_DIM),
            v_rows.reshape(1, DEC_BATCH, DEC_SEQ, D_GATE))
```
